```python
import math
import jax, jax.numpy as jnp
from jax import lax
import numpy as np

D_MODEL = 1024
BATCH = 32
SEQ = 2048
DEPTH = 4

CTX_LEN = 256
GRID_W = 64
HEAD_DIM = 64
N_Q_HEADS = 8
N_KV_HEADS = 2
Q_PER_KV = N_Q_HEADS // N_KV_HEADS
ATTN_WIDTH = N_Q_HEADS * HEAD_DIM
KV_WIDTH = N_KV_HEADS * HEAD_DIM
WINDOW = 128
Q_BLOCK = 128
ROPE_BASE = 10000.0
ROPE_PAIRS_PER_AXIS = HEAD_DIM // 4
CONV_WIDTH = 256
CONV_K = 3
SSM_WIDTH = 256
SSM_GROUP = 16
SSM_GROUPS = SSM_WIDTH // SSM_GROUP
SSM_STATE = 64
MIX_WIDTH = ATTN_WIDTH + CONV_WIDTH + SSM_WIDTH
IN_WIDTH = ATTN_WIDTH + 2 * KV_WIDTH + 3 * CONV_WIDTH + SSM_WIDTH
D_FF = 4 * D_MODEL
N_MOD = 6
EPS = 1e-6
NEG_INF = -1e30

kernel_name = 'hybrid_dit_parallel_groups'


def rms_norm(t, g):
    tf = t.astype(jnp.float32)
    y = tf * lax.rsqrt(jnp.mean(tf * tf, axis=-1, keepdims=True) + EPS)
    return y.astype(t.dtype) * g


def modulate(t, shift, scale):
    return t * (1 + scale) + shift


def split_in_proj(p):
    sizes = [ATTN_WIDTH, KV_WIDTH, KV_WIDTH, CONV_WIDTH, CONV_WIDTH, CONV_WIDTH]
    return jnp.split(p, list(np.cumsum(sizes)), axis=-1)


def axial_rope_tables(n_tokens):
    rows = n_tokens // GRID_W
    row = jnp.broadcast_to(jnp.arange(rows)[:, None], (rows, GRID_W)).reshape(-1)
    col = jnp.broadcast_to(jnp.arange(GRID_W)[None, :], (rows, GRID_W)).reshape(-1)
    freqs = ROPE_BASE ** (-jnp.arange(ROPE_PAIRS_PER_AXIS, dtype=jnp.float32) / ROPE_PAIRS_PER_AXIS)
    ang = jnp.concatenate([row[:, None].astype(jnp.float32) * freqs,
                           col[:, None].astype(jnp.float32) * freqs], axis=-1)
    return jnp.cos(ang), jnp.sin(ang)


def _rotate(t, c, s):
    t1, t2 = jnp.split(t, 2, axis=-1)
    return jnp.concatenate([t1 * c - t2 * s, t1 * s + t2 * c], axis=-1)


def apply_axial_rope(t, cos, sin):
    c = cos[:, None, :].astype(t.dtype)
    s = sin[:, None, :].astype(t.dtype)
    n = ROPE_PAIRS_PER_AXIS
    half = HEAD_DIM // 2
    return jnp.concatenate([_rotate(t[..., :half], c[..., :n], s[..., :n]),
                            _rotate(t[..., half:], c[..., n:], s[..., n:])], axis=-1)


def windowed_latent_attention(q, k, v, kc, vc, sink):
    bsz, n_lat = q.shape[0], q.shape[1]
    n_ctx = kc.shape[1]
    n_blocks = n_lat // Q_BLOCK
    span = Q_BLOCK + 2 * WINDOW
    scale = HEAD_DIM ** -0.5
    pad = ((0, 0), (WINDOW, WINDOW), (0, 0), (0, 0))
    kp = jnp.pad(k, pad)
    vp = jnp.pad(v, pad)
    s_ctx_all = None

    def block(i):
        start = i * Q_BLOCK
        qb = lax.dynamic_slice_in_dim(q, start, Q_BLOCK, axis=1)
        kb = lax.dynamic_slice_in_dim(kp, start, span, axis=1)
        vb = lax.dynamic_slice_in_dim(vp, start, span, axis=1)
        qpos = start + jnp.arange(Q_BLOCK)
        kpos = start - WINDOW + jnp.arange(span)
        mask = (jnp.abs(qpos[:, None] - kpos[None, :]) <= WINDOW) & (kpos >= 0) & (kpos < n_lat)
        s_lat = jnp.einsum('bqhgd,bkhd->bhgqk', qb, kb).astype(jnp.float32) * scale
        s_lat = jnp.where(mask, s_lat, NEG_INF)
        s_ctx = jnp.einsum('bqhgd,bkhd->bhgqk', qb, kc).astype(jnp.float32) * scale
        s_sink = jnp.broadcast_to(sink.astype(jnp.float32)[None, :, :, None, None],
                                  s_ctx.shape[:-1] + (1,))
        p = jax.nn.softmax(jnp.concatenate([s_lat, s_ctx, s_sink], axis=-1), axis=-1)
        p_lat = p[..., :span].astype(v.dtype)
        p_ctx = p[..., span:span + n_ctx].astype(v.dtype)
        return (jnp.einsum('bhgqk,bkhd->bqhgd', p_lat, vb)
                + jnp.einsum('bhgqk,bkhd->bqhgd', p_ctx, vc))

    o = lax.map(block, jnp.arange(n_blocks))
    return jnp.moveaxis(o, 0, 1).reshape(bsz, n_lat, ATTN_WIDTH)


def context_attention(qc, kc, vc, sink):
    bsz, n_ctx = qc.shape[0], qc.shape[1]
    s = jnp.einsum('bqhgd,bkhd->bhgqk', qc, kc).astype(jnp.float32) * (HEAD_DIM ** -0.5)
    s_sink = jnp.broadcast_to(sink.astype(jnp.float32)[None, :, :, None, None], s.shape[:-1] + (1,))
    p = jax.nn.softmax(jnp.concatenate([s, s_sink], axis=-1), axis=-1)[..., :n_ctx]
    o = jnp.einsum('bhgqk,bkhd->bqhgd', p.astype(vc.dtype), vc)
    return o.reshape(bsz, n_ctx, ATTN_WIDTH)


def centred_conv3(z, w):
    zp = jnp.pad(z, ((0, 0), (1, 1), (0, 0)))
    return zp[:, :-2] * w[0] + zp[:, 1:-1] * w[1] + zp[:, 2:] * w[2]


def diag_scan(lam_bar, drive, h0, reverse):
    if h0 is not None:
        edge = -1 if reverse else 0
        drive = drive.at[:, edge].add(lam_bar * h0)
    decay = jnp.broadcast_to(lam_bar, (1, drive.shape[1]) + lam_bar.shape)

    def combine(left, right):
        a_l, b_l = left
        a_r, b_r = right
        return a_l * a_r, a_r * b_l + b_r

    _, h = lax.associative_scan(combine, (decay, drive), reverse=reverse, axis=1)
    return h


def s5_bidirectional(u, uc, lam_re, lam_im, log_dt, b_re, b_im, c_re, c_im, d_skip, w_glu, b_glu,
                     with_ctx_out):
    f32 = jnp.float32
    lam = lax.complex(lam_re.astype(f32), lam_im.astype(f32))
    dt = jnp.exp(log_dt.astype(f32))[..., None]
    lam_bar = jnp.exp(lam * dt)
    b_bar = ((lam_bar - 1) / lam)[..., None] * lax.complex(b_re.astype(f32), b_im.astype(f32))
    c_mat = lax.complex(c_re.astype(f32), c_im.astype(f32))

    def drive(t, direction):
        tg = t.astype(f32).reshape(t.shape[0], t.shape[1], SSM_GROUPS, SSM_GROUP).astype(jnp.complex64)
        return jnp.einsum('blgi,gpi->blgp', tg, b_bar[direction])

    h_ctx_f = diag_scan(lam_bar[0], drive(uc, 0), None, False)
    h_ctx_b = diag_scan(lam_bar[1], drive(uc, 1), None, True)
    h_lat_f = diag_scan(lam_bar[0], drive(u, 0), h_ctx_f[:, -1], False)
    h_lat_b = diag_scan(lam_bar[1], drive(u, 1), h_ctx_b[:, 0], True)

    def readout(t, hf, hb):
        y = jnp.real(jnp.einsum('blgp,gip->blgi', hf, c_mat[0])
                     + jnp.einsum('blgp,gip->blgi', hb, c_mat[1])).reshape(t.shape)
        y = (y + d_skip.astype(f32) * t.astype(f32)).astype(t.dtype)
        g = jax.nn.gelu(y)
        return g * jax.nn.sigmoid(g @ w_glu + b_glu)

    out_ctx = readout(uc, h_ctx_f, h_ctx_b) if with_ctx_out else None
    return readout(u, h_lat_f, h_lat_b), out_ctx


def hybrid_mixer(h_lat, h_ctx, w_in, conv_w, sink, lam_re, lam_im, log_dt, b_re, b_im, c_re, c_im,
                 d_skip, w_glu, b_glu, w_out, cos, sin, with_ctx_out):
    bsz, n_lat, _ = h_lat.shape
    n_ctx = h_ctx.shape[1]
    q, k, v, cb, cc, cx, u = split_in_proj(h_lat @ w_in)
    qc, kc, vc, cbc, ccc, cxc, uc = split_in_proj(h_ctx @ w_in)
    sink = sink.reshape(N_KV_HEADS, Q_PER_KV)
    q = apply_axial_rope(q.reshape(bsz, n_lat, N_Q_HEADS, HEAD_DIM), cos, sin)
    q = q.reshape(bsz, n_lat, N_KV_HEADS, Q_PER_KV, HEAD_DIM)
    k = apply_axial_rope(k.reshape(bsz, n_lat, N_KV_HEADS, HEAD_DIM), cos, sin)
    v = v.reshape(bsz, n_lat, N_KV_HEADS, HEAD_DIM)
    kc = kc.reshape(bsz, n_ctx, N_KV_HEADS, HEAD_DIM)
    vc = vc.reshape(bsz, n_ctx, N_KV_HEADS, HEAD_DIM)
    attn = windowed_latent_attention(q, k, v, kc, vc, sink)
    conv = cb * centred_conv3(cc * cx, conv_w)
    ssm, ssm_c = s5_bidirectional(u, uc, lam_re, lam_im, log_dt, b_re, b_im, c_re, c_im, d_skip,
                                  w_glu, b_glu, with_ctx_out)
    out_lat = jnp.concatenate([attn, conv, ssm], axis=-1) @ w_out
    if not with_ctx_out:
        return out_lat, None
    qc = qc.reshape(bsz, n_ctx, N_KV_HEADS, Q_PER_KV, HEAD_DIM)
    attn_c = context_attention(qc, kc, vc, sink)
    conv_c = cbc * centred_conv3(ccc * cxc, conv_w)
    out_ctx = jnp.concatenate([attn_c, conv_c, ssm_c], axis=-1) @ w_out
    return out_lat, out_ctx


def squared_relu_mlp(t, w1, w2):
    return jnp.square(jax.nn.relu(t @ w1)) @ w2


def _fwd_setup_inputs(seed: int = 0) -> dict:
    key = jax.random.key(seed)
    ks = jax.random.split(key, 24)
    f32 = jnp.float32
    nrm = lambda k, shape, s: jax.random.normal(k, shape, f32) * s
    lam_im_base = jnp.pi * jnp.arange(SSM_STATE, dtype=f32)
    return {
        'x': nrm(ks[0], (BATCH, SEQ, D_MODEL), 1.0),
        'c': nrm(ks[1], (BATCH, D_MODEL), 1.0),
        'ctx': nrm(ks[2], (BATCH, CTX_LEN, D_MODEL), 1.0),
        'c_ctx': nrm(ks[3], (D_MODEL,), 1.0),
        'w_ada': nrm(ks[4], (DEPTH, D_MODEL, N_MOD * D_MODEL), 0.5 * D_MODEL ** -0.5),
        'b_ada': nrm(ks[5], (DEPTH, N_MOD * D_MODEL), 0.02),
        'norm_g': 1.0 + nrm(ks[6], (DEPTH, 4, D_MODEL), 0.02),
        'w_in': nrm(ks[7], (DEPTH, D_MODEL, IN_WIDTH), D_MODEL ** -0.5),
        'conv_w': nrm(ks[8], (DEPTH, CONV_K, CONV_WIDTH), CONV_K ** -0.5),
        'attn_sink': nrm(ks[9], (DEPTH, N_Q_HEADS), 0.5),
        'ssm_lam_re': -0.5 + nrm(ks[10], (DEPTH, 2, SSM_GROUPS, SSM_STATE), 0.01),
        'ssm_lam_im': lam_im_base + nrm(ks[11], (DEPTH, 2, SSM_GROUPS, SSM_STATE), 0.01),
        'ssm_log_dt': jax.random.uniform(ks[12], (DEPTH, 2, SSM_GROUPS), f32,
                                         minval=math.log(1e-3), maxval=math.log(1e-1)),
        'ssm_b_re': nrm(ks[13], (DEPTH, 2, SSM_GROUPS, SSM_STATE, SSM_GROUP), (2 * SSM_GROUP) ** -0.5),
        'ssm_b_im': nrm(ks[14], (DEPTH, 2, SSM_GROUPS, SSM_STATE, SSM_GROUP), (2 * SSM_GROUP) ** -0.5),
        'ssm_c_re': nrm(ks[15], (DEPTH, 2, SSM_GROUPS, SSM_GROUP, SSM_STATE), (2 * SSM_STATE) ** -0.5),
        'ssm_c_im': nrm(ks[16], (DEPTH, 2, SSM_GROUPS, SSM_GROUP, SSM_STATE), (2 * SSM_STATE) ** -0.5),
        'ssm_d': nrm(ks[17], (DEPTH, SSM_WIDTH), 1.0),
        'w_glu': nrm(ks[18], (DEPTH, SSM_WIDTH, SSM_WIDTH), SSM_WIDTH ** -0.5),
        'b_glu': nrm(ks[19], (DEPTH, SSM_WIDTH), 0.02),
        'w_out': nrm(ks[20], (DEPTH, MIX_WIDTH, D_MODEL), MIX_WIDTH ** -0.5),
        'w_mlp_in': nrm(ks[21], (DEPTH, D_MODEL, D_FF), D_MODEL ** -0.5),
        'w_mlp_out': nrm(ks[22], (DEPTH, D_FF, D_MODEL), D_FF ** -0.5),
    }


def _fwd_reference(x, c, ctx, c_ctx, w_ada, b_ada, norm_g, w_in, conv_w, attn_sink, ssm_lam_re, ssm_lam_im,
              ssm_log_dt, ssm_b_re, ssm_b_im, ssm_c_re, ssm_c_im, ssm_d, w_glu, b_glu, w_out,
              w_mlp_in, w_mlp_out):
    cos, sin = axial_rope_tables(x.shape[1])
    c_act = jax.nn.silu(c)
    c_ctx_act = jax.nn.silu(c_ctx)
    h, hc = x, ctx
    for l in range(DEPTH):
        with_ctx_out = l < DEPTH - 1
        mod = (c_act @ w_ada[l] + b_ada[l])[:, None, :]
        mod_c = c_ctx_act @ w_ada[l] + b_ada[l]
        sh1, sc1, g1, sh2, sc2, g2 = jnp.split(mod, N_MOD, axis=-1)
        sh1c, sc1c, g1c, sh2c, sc2c, g2c = jnp.split(mod_c, N_MOD, axis=-1)
        g_pre_mix, g_post_mix, g_pre_mlp, g_post_mlp = norm_g[l]
        a_lat = modulate(rms_norm(h, g_pre_mix), sh1, sc1)
        a_ctx = modulate(rms_norm(hc, g_pre_mix), sh1c, sc1c)
        m_lat, m_ctx = hybrid_mixer(a_lat, a_ctx, w_in[l], conv_w[l], attn_sink[l], ssm_lam_re[l],
                                    ssm_lam_im[l], ssm_log_dt[l], ssm_b_re[l], ssm_b_im[l], ssm_c_re[l],
                                    ssm_c_im[l], ssm_d[l], w_glu[l], b_glu[l], w_out[l], cos, sin,
                                    with_ctx_out)
        h = h + g1 * rms_norm(m_lat, g_post_mix)
        f_lat = squared_relu_mlp(modulate(rms_norm(h, g_pre_mlp), sh2, sc2), w_mlp_in[l], w_mlp_out[l])
        h = h + g2 * rms_norm(f_lat, g_post_mlp)
        if with_ctx_out:
            hc = hc + g1c * rms_norm(m_ctx, g_post_mix)
            f_ctx = squared_relu_mlp(modulate(rms_norm(hc, g_pre_mlp), sh2c, sc2c), w_mlp_in[l], w_mlp_out[l])
            hc = hc + g2c * rms_norm(f_ctx, g_post_mlp)
    return h


import jax as _jax
import jax.numpy as _jnp

TWIN_FORMAT = 'train_step'
FWD_PARAMS = ['x', 'c', 'ctx', 'c_ctx', 'w_ada', 'b_ada', 'norm_g', 'w_in', 'conv_w', 'attn_sink', 'ssm_lam_re', 'ssm_lam_im', 'ssm_log_dt', 'ssm_b_re', 'ssm_b_im', 'ssm_c_re', 'ssm_c_im', 'ssm_d', 'w_glu', 'b_glu', 'w_out', 'w_mlp_in', 'w_mlp_out']
TWIN_WEIGHTS = ['c_ctx', 'w_ada', 'b_ada', 'norm_g', 'w_in', 'conv_w', 'attn_sink', 'ssm_lam_re', 'ssm_lam_im', 'ssm_log_dt', 'ssm_b_re', 'ssm_b_im', 'ssm_c_re', 'ssm_c_im', 'ssm_d', 'w_glu', 'b_glu', 'w_out', 'w_mlp_in', 'w_mlp_out']
TWIN_DIFF_INPUT = 'x'
TWIN_INPUTS = ['x', 'c', 'ctx', 'c_ctx', 'w_ada', 'b_ada', 'norm_g', 'w_in', 'conv_w', 'attn_sink', 'ssm_lam_re', 'ssm_lam_im', 'ssm_log_dt', 'ssm_b_re', 'ssm_b_im', 'ssm_c_re', 'ssm_c_im', 'ssm_d', 'w_glu', 'b_glu', 'w_out', 'w_mlp_in', 'w_mlp_out', 'loss_target', 'm_c_ctx', 'm_w_ada', 'm_b_ada', 'm_norm_g', 'm_w_in', 'm_conv_w', 'm_attn_sink', 'm_ssm_lam_re', 'm_ssm_lam_im', 'm_ssm_log_dt', 'm_ssm_b_re', 'm_ssm_b_im', 'm_ssm_c_re', 'm_ssm_c_im', 'm_ssm_d', 'm_w_glu', 'm_b_glu', 'm_w_out', 'm_w_mlp_in', 'm_w_mlp_out', 'v_c_ctx', 'v_w_ada', 'v_b_ada', 'v_norm_g', 'v_w_in', 'v_conv_w', 'v_attn_sink', 'v_ssm_lam_re', 'v_ssm_lam_im', 'v_ssm_log_dt', 'v_ssm_b_re', 'v_ssm_b_im', 'v_ssm_c_re', 'v_ssm_c_im', 'v_ssm_d', 'v_w_glu', 'v_b_glu', 'v_w_out', 'v_w_mlp_in', 'v_w_mlp_out']
TWIN_OUTPUTS = ['loss', 'grad_x', 'grad_c_ctx', 'grad_w_ada', 'grad_b_ada', 'grad_norm_g', 'grad_w_in', 'grad_conv_w', 'grad_attn_sink', 'grad_ssm_lam_re', 'grad_ssm_lam_im', 'grad_ssm_log_dt', 'grad_ssm_b_re', 'grad_ssm_b_im', 'grad_ssm_c_re', 'grad_ssm_c_im', 'grad_ssm_d', 'grad_w_glu', 'grad_b_glu', 'grad_w_out', 'grad_w_mlp_in', 'grad_w_mlp_out', 'delta_c_ctx', 'delta_w_ada', 'delta_b_ada', 'delta_norm_g', 'delta_w_in', 'delta_conv_w', 'delta_attn_sink', 'delta_ssm_lam_re', 'delta_ssm_lam_im', 'delta_ssm_log_dt', 'delta_ssm_b_re', 'delta_ssm_b_im', 'delta_ssm_c_re', 'delta_ssm_c_im', 'delta_ssm_d', 'delta_w_glu', 'delta_b_glu', 'delta_w_out', 'delta_w_mlp_in', 'delta_w_mlp_out', 'new_m_c_ctx', 'new_m_w_ada', 'new_m_b_ada', 'new_m_norm_g', 'new_m_w_in', 'new_m_conv_w', 'new_m_attn_sink', 'new_m_ssm_lam_re', 'new_m_ssm_lam_im', 'new_m_ssm_log_dt', 'new_m_ssm_b_re', 'new_m_ssm_b_im', 'new_m_ssm_c_re', 'new_m_ssm_c_im', 'new_m_ssm_d', 'new_m_w_glu', 'new_m_b_glu', 'new_m_w_out', 'new_m_w_mlp_in', 'new_m_w_mlp_out', 'new_v_c_ctx', 'new_v_w_ada', 'new_v_b_ada', 'new_v_norm_g', 'new_v_w_in', 'new_v_conv_w', 'new_v_attn_sink', 'new_v_ssm_lam_re', 'new_v_ssm_lam_im', 'new_v_ssm_log_dt', 'new_v_ssm_b_re', 'new_v_ssm_b_im', 'new_v_ssm_c_re', 'new_v_ssm_c_im', 'new_v_ssm_d', 'new_v_w_glu', 'new_v_b_glu', 'new_v_w_out', 'new_v_w_mlp_in', 'new_v_w_mlp_out']
TWIN_LEAF_KINDS = {'loss': 'loss', 'grad_x': 'grad_x', 'grad_c_ctx': 'grad_w', 'grad_w_ada': 'grad_w', 'grad_b_ada': 'grad_w', 'grad_norm_g': 'grad_w', 'grad_w_in': 'grad_w', 'grad_conv_w': 'grad_w', 'grad_attn_sink': 'grad_w', 'grad_ssm_lam_re': 'grad_w', 'grad_ssm_lam_im': 'grad_w', 'grad_ssm_log_dt': 'grad_w', 'grad_ssm_b_re': 'grad_w', 'grad_ssm_b_im': 'grad_w', 'grad_ssm_c_re': 'grad_w', 'grad_ssm_c_im': 'grad_w', 'grad_ssm_d': 'grad_w', 'grad_w_glu': 'grad_w', 'grad_b_glu': 'grad_w', 'grad_w_out': 'grad_w', 'grad_w_mlp_in': 'grad_w', 'grad_w_mlp_out': 'grad_w', 'delta_c_ctx': 'delta_w', 'delta_w_ada': 'delta_w', 'delta_b_ada': 'delta_w', 'delta_norm_g': 'delta_w', 'delta_w_in': 'delta_w', 'delta_conv_w': 'delta_w', 'delta_attn_sink': 'delta_w', 'delta_ssm_lam_re': 'delta_w', 'delta_ssm_lam_im': 'delta_w', 'delta_ssm_log_dt': 'delta_w', 'delta_ssm_b_re': 'delta_w', 'delta_ssm_b_im': 'delta_w', 'delta_ssm_c_re': 'delta_w', 'delta_ssm_c_im': 'delta_w', 'delta_ssm_d': 'delta_w', 'delta_w_glu': 'delta_w', 'delta_b_glu': 'delta_w', 'delta_w_out': 'delta_w', 'delta_w_mlp_in': 'delta_w', 'delta_w_mlp_out': 'delta_w', 'new_m_c_ctx': 'new_m', 'new_m_w_ada': 'new_m', 'new_m_b_ada': 'new_m', 'new_m_norm_g': 'new_m', 'new_m_w_in': 'new_m', 'new_m_conv_w': 'new_m', 'new_m_attn_sink': 'new_m', 'new_m_ssm_lam_re': 'new_m', 'new_m_ssm_lam_im': 'new_m', 'new_m_ssm_log_dt': 'new_m', 'new_m_ssm_b_re': 'new_m', 'new_m_ssm_b_im': 'new_m', 'new_m_ssm_c_re': 'new_m', 'new_m_ssm_c_im': 'new_m', 'new_m_ssm_d': 'new_m', 'new_m_w_glu': 'new_m', 'new_m_b_glu': 'new_m', 'new_m_w_out': 'new_m', 'new_m_w_mlp_in': 'new_m', 'new_m_w_mlp_out': 'new_m', 'new_v_c_ctx': 'new_v', 'new_v_w_ada': 'new_v', 'new_v_b_ada': 'new_v', 'new_v_norm_g': 'new_v', 'new_v_w_in': 'new_v', 'new_v_conv_w': 'new_v', 'new_v_attn_sink': 'new_v', 'new_v_ssm_lam_re': 'new_v', 'new_v_ssm_lam_im': 'new_v', 'new_v_ssm_log_dt': 'new_v', 'new_v_ssm_b_re': 'new_v', 'new_v_ssm_b_im': 'new_v', 'new_v_ssm_c_re': 'new_v', 'new_v_ssm_c_im': 'new_v', 'new_v_ssm_d': 'new_v', 'new_v_w_glu': 'new_v', 'new_v_b_glu': 'new_v', 'new_v_w_out': 'new_v', 'new_v_w_mlp_in': 'new_v', 'new_v_w_mlp_out': 'new_v'}


def _forward(args):
    return _fwd_reference(*[args[k] for k in FWD_PARAMS])


def _output_shape():
    out = _jax.eval_shape(lambda: _forward(_fwd_setup_inputs(0)))
    return out.shape, out.dtype

N_MICROBATCH = 1
ADAM_LR = 0.001
ADAM_B1 = 0.9
ADAM_B2 = 0.999
ADAM_EPS = 1e-08
ADAM_WD = 0.01
ADAM_STEP = 10
PER_EXAMPLE_BATCH_AXIS = {'x': 0, 'c': 0, 'ctx': 0, 'loss_target': 0}
SHARED_INPUTS = []
_WEIGHT_DTYPES = {'c_ctx': _jnp.float32, 'w_ada': _jnp.float32, 'b_ada': _jnp.float32, 'norm_g': _jnp.float32, 'w_in': _jnp.float32, 'conv_w': _jnp.float32, 'attn_sink': _jnp.float32, 'ssm_lam_re': _jnp.float32, 'ssm_lam_im': _jnp.float32, 'ssm_log_dt': _jnp.float32, 'ssm_b_re': _jnp.float32, 'ssm_b_im': _jnp.float32, 'ssm_c_re': _jnp.float32, 'ssm_c_im': _jnp.float32, 'ssm_d': _jnp.float32, 'w_glu': _jnp.float32, 'b_glu': _jnp.float32, 'w_out': _jnp.float32, 'w_mlp_in': _jnp.float32, 'w_mlp_out': _jnp.float32}
MOMENT_SCALE = {'c_ctx': 1.180441e+00, 'w_ada': 3.785272e+00, 'b_ada': 7.098590e+00, 'norm_g': 5.262910e+00, 'w_in': 4.601875e-01, 'conv_w': 3.467905e-01, 'attn_sink': 9.494807e-03, 'ssm_lam_re': 6.374980e-02, 'ssm_lam_im': 4.678362e-02, 'ssm_log_dt': 2.432651e+00, 'ssm_b_re': 3.800587e-02, 'ssm_b_im': 4.263809e-02, 'ssm_c_re': 7.017374e-02, 'ssm_c_im': 7.349192e-02, 'ssm_d': 1.000710e+00, 'w_glu': 1.367498e-01, 'b_glu': 3.979694e-01, 'w_out': 8.110330e-01, 'w_mlp_in': 3.128086e-01, 'w_mlp_out': 1.555554e+00}


def _to_microbatches(a, axis):
    t = _jnp.moveaxis(a, axis, 0)
    t = t.reshape((N_MICROBATCH, t.shape[0] // N_MICROBATCH) + t.shape[1:])
    return _jnp.moveaxis(t, 1, axis + 1)


def setup_inputs(seed: int = 0) -> dict:
    inp = _fwd_setup_inputs(seed)
    key = _jax.random.fold_in(_jax.random.key(seed), 7919)
    shape, _ = _output_shape()
    out = dict(inp)
    out["loss_target"] = _jax.random.normal(_jax.random.fold_in(key, 0), shape, _jnp.float32)
    for i, name in enumerate(TWIN_WEIGHTS):
        w = inp[name].astype(_jnp.float32)
        if MOMENT_SCALE is None:
            s = _jnp.sqrt(_jnp.mean(_jnp.square(w)) + 1e-30)
        else:
            s = MOMENT_SCALE[name]
        km, kv = _jax.random.split(_jax.random.fold_in(key, i + 1))
        out[name] = w
        out["m_" + name] = s * _jax.random.normal(km, w.shape, _jnp.float32)
        out["v_" + name] = (s * s) * _jax.random.uniform(kv, w.shape, _jnp.float32, 0.5, 1.5)
    if N_MICROBATCH > 1:
        for name, axis in PER_EXAMPLE_BATCH_AXIS.items():
            out[name] = _to_microbatches(out[name], axis)
    return {'x': out['x'], 'c': out['c'], 'ctx': out['ctx'], 'c_ctx': out['c_ctx'], 'w_ada': out['w_ada'], 'b_ada': out['b_ada'], 'norm_g': out['norm_g'], 'w_in': out['w_in'], 'conv_w': out['conv_w'], 'attn_sink': out['attn_sink'], 'ssm_lam_re': out['ssm_lam_re'], 'ssm_lam_im': out['ssm_lam_im'], 'ssm_log_dt': out['ssm_log_dt'], 'ssm_b_re': out['ssm_b_re'], 'ssm_b_im': out['ssm_b_im'], 'ssm_c_re': out['ssm_c_re'], 'ssm_c_im': out['ssm_c_im'], 'ssm_d': out['ssm_d'], 'w_glu': out['w_glu'], 'b_glu': out['b_glu'], 'w_out': out['w_out'], 'w_mlp_in': out['w_mlp_in'], 'w_mlp_out': out['w_mlp_out'], 'loss_target': out['loss_target'], 'm_c_ctx': out['m_c_ctx'], 'm_w_ada': out['m_w_ada'], 'm_b_ada': out['m_b_ada'], 'm_norm_g': out['m_norm_g'], 'm_w_in': out['m_w_in'], 'm_conv_w': out['m_conv_w'], 'm_attn_sink': out['m_attn_sink'], 'm_ssm_lam_re': out['m_ssm_lam_re'], 'm_ssm_lam_im': out['m_ssm_lam_im'], 'm_ssm_log_dt': out['m_ssm_log_dt'], 'm_ssm_b_re': out['m_ssm_b_re'], 'm_ssm_b_im': out['m_ssm_b_im'], 'm_ssm_c_re': out['m_ssm_c_re'], 'm_ssm_c_im': out['m_ssm_c_im'], 'm_ssm_d': out['m_ssm_d'], 'm_w_glu': out['m_w_glu'], 'm_b_glu': out['m_b_glu'], 'm_w_out': out['m_w_out'], 'm_w_mlp_in': out['m_w_mlp_in'], 'm_w_mlp_out': out['m_w_mlp_out'], 'v_c_ctx': out['v_c_ctx'], 'v_w_ada': out['v_w_ada'], 'v_b_ada': out['v_b_ada'], 'v_norm_g': out['v_norm_g'], 'v_w_in': out['v_w_in'], 'v_conv_w': out['v_conv_w'], 'v_attn_sink': out['v_attn_sink'], 'v_ssm_lam_re': out['v_ssm_lam_re'], 'v_ssm_lam_im': out['v_ssm_lam_im'], 'v_ssm_log_dt': out['v_ssm_log_dt'], 'v_ssm_b_re': out['v_ssm_b_re'], 'v_ssm_b_im': out['v_ssm_b_im'], 'v_ssm_c_re': out['v_ssm_c_re'], 'v_ssm_c_im': out['v_ssm_c_im'], 'v_ssm_d': out['v_ssm_d'], 'v_w_glu': out['v_w_glu'], 'v_b_glu': out['v_b_glu'], 'v_w_out': out['v_w_out'], 'v_w_mlp_in': out['v_w_mlp_in'], 'v_w_mlp_out': out['v_w_mlp_out']}


def _loss(weights, diff, rest, loss_target):
    with _jax.named_scope("forward"):
        args = {**rest, TWIN_DIFF_INPUT: diff, **{k: w.astype(_WEIGHT_DTYPES[k]) for k, w in weights.items()}}
        y = _forward(args)
    with _jax.named_scope("loss_head"):
        err = _jnp.square(y.astype(_jnp.float32) - loss_target)
        return 0.5 * _jnp.sum(_jnp.mean(err, axis=-1)) if err.ndim else 0.5 * err


def _adamw(w, g, m, v):
    m = ADAM_B1 * m + (1.0 - ADAM_B1) * g
    v = ADAM_B2 * v + (1.0 - ADAM_B2) * _jnp.square(g)
    m_hat = m / (1.0 - ADAM_B1 ** ADAM_STEP)
    v_hat = v / (1.0 - ADAM_B2 ** ADAM_STEP)
    delta = -ADAM_LR * (m_hat / (_jnp.sqrt(v_hat) + ADAM_EPS) + ADAM_WD * w)
    return delta, m, v


def reference(x, c, ctx, c_ctx, w_ada, b_ada, norm_g, w_in, conv_w, attn_sink, ssm_lam_re, ssm_lam_im, ssm_log_dt, ssm_b_re, ssm_b_im, ssm_c_re, ssm_c_im, ssm_d, w_glu, b_glu, w_out, w_mlp_in, w_mlp_out, loss_target, m_c_ctx, m_w_ada, m_b_ada, m_norm_g, m_w_in, m_conv_w, m_attn_sink, m_ssm_lam_re, m_ssm_lam_im, m_ssm_log_dt, m_ssm_b_re, m_ssm_b_im, m_ssm_c_re, m_ssm_c_im, m_ssm_d, m_w_glu, m_b_glu, m_w_out, m_w_mlp_in, m_w_mlp_out, v_c_ctx, v_w_ada, v_b_ada, v_norm_g, v_w_in, v_conv_w, v_attn_sink, v_ssm_lam_re, v_ssm_lam_im, v_ssm_log_dt, v_ssm_b_re, v_ssm_b_im, v_ssm_c_re, v_ssm_c_im, v_ssm_d, v_w_glu, v_b_glu, v_w_out, v_w_mlp_in, v_w_mlp_out):
    given = dict(x=x, c=c, ctx=ctx, c_ctx=c_ctx, w_ada=w_ada, b_ada=b_ada, norm_g=norm_g, w_in=w_in, conv_w=conv_w, attn_sink=attn_sink, ssm_lam_re=ssm_lam_re, ssm_lam_im=ssm_lam_im, ssm_log_dt=ssm_log_dt, ssm_b_re=ssm_b_re, ssm_b_im=ssm_b_im, ssm_c_re=ssm_c_re, ssm_c_im=ssm_c_im, ssm_d=ssm_d, w_glu=w_glu, b_glu=b_glu, w_out=w_out, w_mlp_in=w_mlp_in, w_mlp_out=w_mlp_out, loss_target=loss_target, m_c_ctx=m_c_ctx, m_w_ada=m_w_ada, m_b_ada=m_b_ada, m_norm_g=m_norm_g, m_w_in=m_w_in, m_conv_w=m_conv_w, m_attn_sink=m_attn_sink, m_ssm_lam_re=m_ssm_lam_re, m_ssm_lam_im=m_ssm_lam_im, m_ssm_log_dt=m_ssm_log_dt, m_ssm_b_re=m_ssm_b_re, m_ssm_b_im=m_ssm_b_im, m_ssm_c_re=m_ssm_c_re, m_ssm_c_im=m_ssm_c_im, m_ssm_d=m_ssm_d, m_w_glu=m_w_glu, m_b_glu=m_b_glu, m_w_out=m_w_out, m_w_mlp_in=m_w_mlp_in, m_w_mlp_out=m_w_mlp_out, v_c_ctx=v_c_ctx, v_w_ada=v_w_ada, v_b_ada=v_b_ada, v_norm_g=v_norm_g, v_w_in=v_w_in, v_conv_w=v_conv_w, v_attn_sink=v_attn_sink, v_ssm_lam_re=v_ssm_lam_re, v_ssm_lam_im=v_ssm_lam_im, v_ssm_log_dt=v_ssm_log_dt, v_ssm_b_re=v_ssm_b_re, v_ssm_b_im=v_ssm_b_im, v_ssm_c_re=v_ssm_c_re, v_ssm_c_im=v_ssm_c_im, v_ssm_d=v_ssm_d, v_w_glu=v_w_glu, v_b_glu=v_b_glu, v_w_out=v_w_out, v_w_mlp_in=v_w_mlp_in, v_w_mlp_out=v_w_mlp_out)
    weights = {n: given[n] for n in TWIN_WEIGHTS}
    shared = {n: given[n] for n in SHARED_INPUTS}
    per_example = {n: given[n] for n in ['x', 'c', 'ctx']}
    grad_fn = _jax.value_and_grad(_loss, argnums=(0, 1))

    def one_microbatch(ex, loss_target):
        ex = dict(ex)
        diff = ex.pop(TWIN_DIFF_INPUT)
        return grad_fn(weights, diff, {**shared, **ex}, loss_target)

    if N_MICROBATCH == 1:
        loss, (grad_w, grad_x) = one_microbatch(per_example, given["loss_target"])
    else:
        def body(carry, xs):
            loss_sum, grad_sum = carry
            l_k, (gw_k, gx_k) = one_microbatch(xs[0], xs[1])
            with _jax.named_scope("update"):
                return (loss_sum + l_k, _jax.tree.map(_jnp.add, grad_sum, gw_k)), gx_k

        init = (_jnp.zeros((), _jnp.float32), _jax.tree.map(_jnp.zeros_like, weights))
        (loss, grad_w), grad_x = _jax.lax.scan(body, init, (per_example, given["loss_target"]))
    with _jax.named_scope("update"):
        delta_w, new_m, new_v = {}, {}, {}
        for n in TWIN_WEIGHTS:
            delta_w[n], new_m[n], new_v[n] = _adamw(weights[n], grad_w[n], given["m_" + n], given["v_" + n])
    return (loss, grad_x, *[grad_w[n] for n in TWIN_WEIGHTS], *[delta_w[n] for n in TWIN_WEIGHTS],
            *[new_m[n] for n in TWIN_WEIGHTS], *[new_v[n] for n in TWIN_WEIGHTS])
```

```python
import functools
import math

import jax
import jax.numpy as jnp
from jax import lax
from jax.experimental import pallas as pl
from jax.experimental.pallas import tpu as pltpu

F32 = jnp.float32
MXU = jnp.bfloat16

D = 1024
DFF = 4096
NMOD = 6
EPS = 1e-6
HD = 64
NQ = 8
NKV = 2
QPK = 4
AW = 512
KVW = 128
WIN = 128
QB = 128
CW = 256
SW = 256
SGRP = 16
SCH = 16
SST = 64
INW = 1792
GRID_W = 64
ROPE_BASE = 10000.0
NEG = -1e30
SCALE = HD ** -0.5
NLAYER = 4

TB = 256
TF = 1024
TC = 32
NCHIP = 4

LR, B1, B2, AEPS, WD, STEP = 0.001, 0.9, 0.999, 1e-08, 0.01, 10

MESH = pl.DeviceIdType.MESH
SDS = jax.ShapeDtypeStruct
BS = pl.BlockSpec
ANY = pl.BlockSpec(memory_space=pl.ANY)


def _cp(sem, vmem_mb=None):
    kw = dict(dimension_semantics=sem)
    if vmem_mb is not None:
        kw["vmem_limit_bytes"] = vmem_mb * 1024 * 1024
    return pltpu.CompilerParams(**kw)


def _dot(a, b):
    return jnp.dot(a, b, preferred_element_type=F32)


def _dot_nt(a, b):
    return lax.dot_general(a, b, (((1,), (1,)), ((), ())), preferred_element_type=F32)


def _dot_tn(a, b):
    return lax.dot_general(a, b, (((0,), (0,)), ((), ())), preferred_element_type=F32)


def _rms(x):
    r = lax.rsqrt(jnp.mean(x * x, axis=-1, keepdims=True) + EPS)
    return x * r, r


def _rms_bwd(dn, n, r):
    return r * (dn - n * jnp.mean(dn * n, axis=-1, keepdims=True))


_GC = math.sqrt(2.0 / math.pi)


def _gelu(y):
    return 0.5 * y * (1.0 + jnp.tanh(_GC * (y + 0.044715 * y * y * y)))


def _gelu_grad(y):
    th = jnp.tanh(_GC * (y + 0.044715 * y * y * y))
    return 0.5 * (1.0 + th) + 0.5 * y * (1.0 - th * th) * _GC * (1.0 + 3 * 0.044715 * y * y)


def _sigmoid(v):
    return 1.0 / (1.0 + jnp.exp(-v))


def _colsum(x):
    return jnp.sum(x, axis=0, keepdims=True)


def _tok(width, col=0):
    return BS((None, TB, width), lambda b, j: (b, j, col))


def _tok3(width, col=None):
    if col is None:
        return BS((None, TB, width), lambda b, j, f: (b, j, f))
    return BS((None, TB, width), lambda b, j, f: (b, j, col))


def _mod_spec(nct):
    return BS((None, None, 8, D), lambda b, j: (b, jnp.where(j >= nct, 1, 0), 0, 0))


def _mod_spec3(nct):
    return BS((None, None, 8, D), lambda b, j, f: (b, jnp.where(j >= nct, 1, 0), 0, 0))


def _full(shape):
    nd = len(shape)
    return BS(shape, lambda *a: (0,) * nd)


def _ada_fwd(cact8, w, b):
    def body(c_ref, w_ref, b_ref, o_ref):
        o_ref[...] = _dot(c_ref[...].astype(MXU), w_ref[...]) + b_ref[...]

    return pl.pallas_call(
        body, name="ada_fwd", grid=(NMOD,),
        in_specs=[BS((8, D), lambda j: (0, 0)), BS((D, D), lambda j: (0, j)), BS((1, D), lambda j: (0, j))],
        out_specs=BS((8, D), lambda j: (0, j)),
        out_shape=SDS((8, NMOD * D), F32), compiler_params=_cp(("arbitrary",)),
    )(cact8, w, b)


def _ada_bwd_c(dmod8, w):
    def body(d_ref, w_ref, o_ref):
        @pl.when(pl.program_id(0) == 0)
        def _():
            o_ref[...] = jnp.zeros_like(o_ref)
        o_ref[...] += _dot_nt(d_ref[...], w_ref[...])

    return pl.pallas_call(
        body, name="ada_bwd_c", grid=(NMOD,),
        in_specs=[BS((8, D), lambda j: (0, j)), BS((D, D), lambda j: (0, j))],
        out_specs=BS((8, D), lambda j: (0, 0)),
        out_shape=SDS((8, D), F32), compiler_params=_cp(("arbitrary",)),
    )(dmod8, w)


def _pre_mix(h, modtok, ng, w_in, nct):
    B, S, _ = h.shape

    def body(h_ref, mod_ref, g_ref, w_ref, p_ref, a_ref):
        n, _ = _rms(h_ref[...])
        a = ((n * g_ref[0:1, :]) * (1.0 + mod_ref[1:2, :]) + mod_ref[0:1, :]).astype(MXU)
        a_ref[...] = a
        p_ref[...] = _dot(a, w_ref[...])

    return pl.pallas_call(
        body, name="pre_mix", grid=(B, S // TB),
        in_specs=[_tok(D), _mod_spec(nct), _full((8, D)), _full((D, INW))],
        out_specs=[_tok(INW), _tok(D)],
        out_shape=[SDS((B, S, INW), F32), SDS((B, S, D), MXU)],
        compiler_params=_cp(("arbitrary", "arbitrary"), 40),
    )(h, modtok, ng, w_in)


def _pre_bwd(dp, dh_part, h, modtok, ng, w_in, nct):
    B, S, _ = h.shape

    def body(dp_ref, dhp_ref, h_ref, mod_ref, g_ref, w_ref, dh_ref, vec_ref):
        da = _dot_nt(dp_ref[...], w_ref[...])
        n, r = _rms(h_ref[...])
        g = g_ref[0:1, :]
        sc1 = 1.0 + mod_ref[1:2, :]
        vec_ref[...] = jnp.zeros_like(vec_ref)
        vec_ref[0:1, :] = _colsum(da)
        vec_ref[1:2, :] = _colsum(da * (n * g))
        vec_ref[2:3, :] = _colsum(da * sc1 * n)
        dh_ref[...] = dhp_ref[...] + _rms_bwd(da * sc1 * g, n, r)

    return pl.pallas_call(
        body, name="pre_bwd", grid=(B, S // TB),
        in_specs=[_tok(INW), _tok(D), _tok(D), _mod_spec(nct), _full((8, D)), _full((D, INW))],
        out_specs=[_tok(D), BS((None, None, 8, D), lambda b, j: (b, j, 0, 0))],
        out_shape=[SDS((B, S, D), F32), SDS((B, S // TB, 8, D), F32)],
        compiler_params=_cp(("arbitrary", "arbitrary"), 40),
    )(dp, dh_part, h, modtok, ng, w_in)


def _rope(x, cos, sin, width, out_dtype, name):
    B, S, _ = x.shape

    def body(x_ref, c_ref, s_ref, o_ref):
        lane = lax.broadcasted_iota(jnp.int32, (TB, 128), 1)
        first = (lane % 32) < 16
        c = c_ref[...]
        s = s_ref[...]
        for k in range(width // 128):
            t = x_ref[:, k * 128:(k + 1) * 128].astype(F32)
            partner = jnp.where(first, pltpu.roll(t, 112, 1), pltpu.roll(t, 16, 1))
            o_ref[:, k * 128:(k + 1) * 128] = (t * c + partner * s).astype(out_dtype)

    tab = BS((TB, 128), lambda b, j: (j, 0))
    return pl.pallas_call(
        body, name=name, grid=(B, S // TB),
        in_specs=[_tok(width), tab, tab],
        out_specs=_tok(width),
        out_shape=SDS((B, S, width), out_dtype),
        compiler_params=_cp(("arbitrary", "arbitrary")),
    )(x, cos, sin)


def _attn_masks(i, S, Lc):
    ncb = Lc // QB
    is_lat = i >= ncb
    w0 = pl.multiple_of(jnp.clip((i - 1) * QB, 0, S - 3 * QB), QB)
    qpos = i * QB + lax.broadcasted_iota(jnp.int32, (QB, 3 * QB), 0)
    kpos = w0 + lax.broadcasted_iota(jnp.int32, (QB, 3 * QB), 1)
    mask = jnp.logical_and(jnp.logical_and(kpos >= Lc, jnp.abs(qpos - kpos) <= WIN), is_lat)
    return w0, mask


def _attn_probs(qh, kch, kwh, mask, sk):
    sc = _dot_nt(qh, kch) * SCALE
    sw = jnp.where(mask, _dot_nt(qh, kwh) * SCALE, NEG)
    m = jnp.maximum(jnp.maximum(jnp.max(sc, axis=-1, keepdims=True), jnp.max(sw, axis=-1, keepdims=True)), sk)
    ec = jnp.exp(sc - m)
    ew = jnp.exp(sw - m)
    es = jnp.exp(sk - m)
    inv = 1.0 / (jnp.sum(ec, axis=-1, keepdims=True) + jnp.sum(ew, axis=-1, keepdims=True) + es)
    return ec * inv, ew * inv, es * inv


def _attn_fwd(sink, qk, p, Lc):
    B, S, _ = qk.shape

    def body(sink_ref, q_ref, k_ref, v_ref, o_ref):
        w0, mask = _attn_masks(pl.program_id(1), S, Lc)
        kc = k_ref[0:Lc, :]
        vc = v_ref[0:Lc, :].astype(MXU)
        kw = k_ref[pl.ds(w0, 3 * QB), :]
        vw = v_ref[pl.ds(w0, 3 * QB), :].astype(MXU)
        for hk in range(NKV):
            cs = slice(hk * HD, (hk + 1) * HD)
            for g in range(QPK):
                hq = hk * QPK + g
                qs = slice(hq * HD, (hq + 1) * HD)
                pc, pw, _ = _attn_probs(q_ref[:, qs], kc[:, cs], kw[:, cs], mask, sink_ref[hq])
                o = _dot(pc.astype(MXU), vc[:, cs]) + _dot(pw.astype(MXU), vw[:, cs])
                o_ref[:, qs] = o.astype(o_ref.dtype)

    return pl.pallas_call(
        body, name="attn_fwd", grid=(B, S // QB),
        in_specs=[BS(memory_space=pltpu.SMEM),
                  BS((None, QB, AW), lambda b, i: (b, i, 0)),
                  BS((None, S, KVW), lambda b, i: (b, 0, AW // KVW)),
                  BS((None, S, KVW), lambda b, i: (b, 0, (AW + KVW) // KVW))],
        out_specs=BS((None, QB, AW), lambda b, i: (b, i, 0)),
        out_shape=SDS((B, S, AW), MXU),
        compiler_params=_cp(("arbitrary", "arbitrary")),
    )(sink, qk, qk, p)


def _attn_bwd(sink, qk, p, do, Lc):
    B, S, _ = qk.shape

    def body(sink_ref, q_ref, k_ref, v_ref, do_ref, dq_ref, dk_ref, dv_ref, ds_ref):
        i = pl.program_id(1)

        @pl.when(i == 0)
        def _():
            dk_ref[...] = jnp.zeros_like(dk_ref)
            dv_ref[...] = jnp.zeros_like(dv_ref)
            ds_ref[...] = jnp.zeros_like(ds_ref)

        w0, mask = _attn_masks(i, S, Lc)
        kc = k_ref[0:Lc, :]
        vc = v_ref[0:Lc, :].astype(MXU)
        kw = k_ref[pl.ds(w0, 3 * QB), :]
        vw = v_ref[pl.ds(w0, 3 * QB), :].astype(MXU)
        for hk in range(NKV):
            cs = slice(hk * HD, (hk + 1) * HD)
            dkc = jnp.zeros((Lc, HD), F32)
            dkw = jnp.zeros((3 * QB, HD), F32)
            dvc = jnp.zeros((Lc, HD), F32)
            dvw = jnp.zeros((3 * QB, HD), F32)
            for g in range(QPK):
                hq = hk * QPK + g
                qs = slice(hq * HD, (hq + 1) * HD)
                qh = q_ref[:, qs]
                pc, pw, ps = _attn_probs(qh, kc[:, cs], kw[:, cs], mask, sink_ref[hq])
                dob = do_ref[:, qs].astype(MXU)
                dpc = _dot_nt(dob, vc[:, cs])
                dpw = _dot_nt(dob, vw[:, cs])
                delta = jnp.sum(pc * dpc, axis=-1, keepdims=True) + jnp.sum(pw * dpw, axis=-1, keepdims=True)
                dsc = (pc * (dpc - delta) * SCALE).astype(MXU)
                dsw = (pw * (dpw - delta) * SCALE).astype(MXU)
                dq_ref[:, qs] = _dot(dsc, kc[:, cs]) + _dot(dsw, kw[:, cs])
                dkc = dkc + _dot_tn(dsc, qh)
                dkw = dkw + _dot_tn(dsw, qh)
                dvc = dvc + _dot_tn(pc.astype(MXU), dob)
                dvw = dvw + _dot_tn(pw.astype(MXU), dob)
                dsink = -jnp.sum(ps * delta, axis=0, keepdims=True)
                ds_ref[hq:hq + 1, :] += jnp.broadcast_to(dsink, (1, 128))
            dk_ref[0:Lc, cs] += dkc
            dv_ref[0:Lc, cs] += dvc
            dk_ref[pl.ds(w0, 3 * QB), cs] += dkw
            dv_ref[pl.ds(w0, 3 * QB), cs] += dvw

    kv_acc = BS((None, S, KVW), lambda b, i: (b, 0, 0))
    return pl.pallas_call(
        body, name="attn_bwd", grid=(B, S // QB),
        in_specs=[BS(memory_space=pltpu.SMEM),
                  BS((None, QB, AW), lambda b, i: (b, i, 0)),
                  BS((None, S, KVW), lambda b, i: (b, 0, AW // KVW)),
                  BS((None, S, KVW), lambda b, i: (b, 0, (AW + KVW) // KVW)),
                  BS((None, QB, AW), lambda b, i: (b, i, 0))],
        out_specs=[BS((None, QB, AW), lambda b, i: (b, i, 0)), kv_acc, kv_acc,
                   BS((None, 8, 128), lambda b, i: (b, 0, 0))],
        out_shape=[SDS((B, S, AW), F32), SDS((B, S, KVW), F32), SDS((B, S, KVW), F32), SDS((B, 8, 128), F32)],
        compiler_params=_cp(("arbitrary", "arbitrary")),
    )(sink, qk, qk, p, do)


def _halo_specs(width, col, S):
    nb8 = S // 8
    per = TB // 8
    prev = BS((None, 8, width), lambda b, j: (b, jnp.maximum(j * per - 1, 0), col))
    nxt = BS((None, 8, width), lambda b, j: (b, jnp.minimum((j + 1) * per, nb8 - 1), col))
    return prev, nxt


def _shift_dn(z, prev_row, at_start):
    row = lax.broadcasted_iota(jnp.int32, z.shape, 0)
    first = jnp.where(at_start, 0.0, 1.0) * prev_row
    return jnp.where(row == 0, first, pltpu.roll(z, 1, 0))


def _shift_up(z, next_row, at_end):
    row = lax.broadcasted_iota(jnp.int32, z.shape, 0)
    last = jnp.where(at_end, 0.0, 1.0) * next_row
    return jnp.where(row == z.shape[0] - 1, last, pltpu.roll(z, z.shape[0] - 1, 0))


def _conv_fwd(p, cw8, nct):
    B, S, _ = p.shape
    nt = S // TB

    def body(p_ref, pp_ref, pn_ref, w_ref, o_ref):
        j = pl.program_id(1)
        at_start = jnp.logical_or(j == 0, j == nct)
        at_end = jnp.logical_or(j == nct - 1, j == nt - 1)
        z = p_ref[:, 256:512] * p_ref[:, 512:768]
        zprev = pp_ref[7:8, 256:512] * pp_ref[7:8, 512:768]
        znext = pn_ref[0:1, 256:512] * pn_ref[0:1, 512:768]
        c3 = (_shift_dn(z, zprev, at_start) * w_ref[0:1, :] + z * w_ref[1:2, :]
              + _shift_up(z, znext, at_end) * w_ref[2:3, :])
        o_ref[...] = p_ref[:, 0:256] * c3

    prev, nxt = _halo_specs(3 * CW, 1, S)
    return pl.pallas_call(
        body, name="conv_fwd", grid=(B, nt),
        in_specs=[_tok(3 * CW, 1), prev, nxt, _full((8, CW))],
        out_specs=_tok(CW),
        out_shape=SDS((B, S, CW), F32),
        compiler_params=_cp(("arbitrary", "arbitrary")),
    )(p, p, p, cw8)


def _conv_bwd(p, dcv, cw8, nct):
    B, S, _ = p.shape
    nt = S // TB

    def body(p_ref, pp_ref, pn_ref, d_ref, dp_ref, dn_ref, w_ref, o_ref, dw_ref):
        j = pl.program_id(1)
        at_start = jnp.logical_or(j == 0, j == nct)
        at_end = jnp.logical_or(j == nct - 1, j == nt - 1)
        cb, cc, cx = p_ref[:, 0:256], p_ref[:, 256:512], p_ref[:, 512:768]
        z = cc * cx
        zm = _shift_dn(z, pp_ref[7:8, 256:512] * pp_ref[7:8, 512:768], at_start)
        zp = _shift_up(z, pn_ref[0:1, 256:512] * pn_ref[0:1, 512:768], at_end)
        d = d_ref[...]
        e = d * cb
        em = _shift_dn(e, dp_ref[7:8, :] * pp_ref[7:8, 0:256], at_start)
        ep = _shift_up(e, dn_ref[0:1, :] * pn_ref[0:1, 0:256], at_end)
        dz = e * w_ref[1:2, :] + ep * w_ref[0:1, :] + em * w_ref[2:3, :]
        o_ref[:, 0:256] = d * (zm * w_ref[0:1, :] + z * w_ref[1:2, :] + zp * w_ref[2:3, :])
        o_ref[:, 256:512] = dz * cx
        o_ref[:, 512:768] = dz * cc
        dw_ref[...] = jnp.zeros_like(dw_ref)
        dw_ref[0:1, :] = _colsum(e * zm)
        dw_ref[1:2, :] = _colsum(e * z)
        dw_ref[2:3, :] = _colsum(e * zp)

    prev, nxt = _halo_specs(3 * CW, 1, S)
    dprev, dnxt = _halo_specs(CW, 0, S)
    return pl.pallas_call(
        body, name="conv_bwd", grid=(B, nt),
        in_specs=[_tok(3 * CW, 1), prev, nxt, _tok(CW), dprev, dnxt, _full((8, CW))],
        out_specs=[_tok(3 * CW), BS((None, None, 8, CW), lambda b, j: (b, j, 0, 0))],
        out_shape=[SDS((B, S, 3 * CW), F32), SDS((B, nt, 8, CW), F32)],
        compiler_params=_cp(("arbitrary", "arbitrary")),
    )(p, p, p, dcv, dcv, dcv, cw8)


def _scan_fwd(u, b4, a, c2, nch):
    rows_all = u.shape[0]
    nk = rows_all // (TC * nch)
    rows = TC * nch
    HW = SGRP * SST

    def body(u_ref, b4_ref, a_ref, c2_ref, y_ref, hb_ref, hcar, dbuf, hbuf):
        @pl.when(pl.program_id(0) == 0)
        def _():
            hcar[...] = jnp.zeros_like(hcar)

        fwd = (lax.broadcasted_iota(jnp.int32, (rows, 1), 0) % nch) < (nch // 2)
        d4 = _dot(u_ref[...].astype(MXU), b4_ref[...])
        dbuf[...] = jnp.where(fwd, d4[:, :2 * HW], d4[:, 2 * HW:])
        hb_ref[...] = hcar[...]
        ar, ai = a_ref[:, :HW], a_ref[:, HW:]

        def step(t, h):
            hr, hi = h
            r0 = pl.multiple_of(t * nch, nch)
            nr = ar * hr - ai * hi + dbuf[pl.ds(r0, nch), :HW]
            ni = ar * hi + ai * hr + dbuf[pl.ds(r0, nch), HW:]
            hbuf[pl.ds(r0, nch), :HW] = nr
            hbuf[pl.ds(r0, nch), HW:] = ni
            return nr, ni

        hr, hi = lax.fori_loop(0, TC, step, (hcar[:, :HW], hcar[:, HW:]))
        hcar[:, :HW] = hr
        hcar[:, HW:] = hi
        y2 = _dot(hbuf[...].astype(MXU), c2_ref[...])
        y_ref[...] = jnp.where(fwd, y2[:, :SW], y2[:, SW:])

    return pl.pallas_call(
        body, name="scan_fwd", grid=(nk,),
        in_specs=[BS((rows, SW), lambda k: (k, 0)), _full((SW, 4 * HW)), _full((nch, 2 * HW)),
                  _full((2 * HW, 2 * SW))],
        out_specs=[BS((rows, SW), lambda k: (k, 0)), BS((None, nch, 2 * HW), lambda k: (k, 0, 0))],
        out_shape=[SDS((rows_all, SW), F32), SDS((nk, nch, 2 * HW), F32)],
        scratch_shapes=[pltpu.VMEM((nch, 2 * HW), F32), pltpu.VMEM((rows, 2 * HW), F32),
                        pltpu.VMEM((rows, 2 * HW), F32)],
        compiler_params=_cp(("arbitrary",), 48),
    )(u, b4, a, c2)


def _scan_bwd(u, dy, hb, b4, a, c2, nch):
    rows_all = u.shape[0]
    nk = rows_all // (TC * nch)
    rows = TC * nch
    HW = SGRP * SST

    def body(u_ref, dy_ref, hb_ref, b4_ref, a_ref, c2_ref, du_ref, db4_ref, da_ref, dc2_ref,
             gcar, dbuf, hbuf, gbuf):
        @pl.when(pl.program_id(0) == 0)
        def _():
            gcar[...] = jnp.zeros_like(gcar)
            db4_ref[...] = jnp.zeros_like(db4_ref)
            da_ref[...] = jnp.zeros_like(da_ref)
            dc2_ref[...] = jnp.zeros_like(dc2_ref)

        fwd = (lax.broadcasted_iota(jnp.int32, (rows, 1), 0) % nch) < (nch // 2)
        ub = u_ref[...].astype(MXU)
        d4 = _dot(ub, b4_ref[...])
        dbuf[...] = jnp.where(fwd, d4[:, :2 * HW], d4[:, 2 * HW:])
        ar, ai = a_ref[:, :HW], a_ref[:, HW:]
        hbuf[0:nch, :] = hb_ref[...]

        def fstep(t, h):
            hr, hi = h
            r0 = pl.multiple_of(t * nch, nch)
            nr = ar * hr - ai * hi + dbuf[pl.ds(r0, nch), :HW]
            ni = ar * hi + ai * hr + dbuf[pl.ds(r0, nch), HW:]
            hbuf[pl.ds(r0 + nch, nch), :HW] = nr
            hbuf[pl.ds(r0 + nch, nch), HW:] = ni
            return nr, ni

        lax.fori_loop(0, TC, fstep, (hb_ref[:, :HW], hb_ref[:, HW:]))

        dyv = dy_ref[...]
        dy2 = jnp.concatenate([jnp.where(fwd, dyv, 0.0), jnp.where(fwd, 0.0, dyv)], axis=1).astype(MXU)
        gbuf[...] = _dot_nt(dy2, c2_ref[...])
        dc2_ref[...] += _dot_tn(hbuf[nch:, :].astype(MXU), dy2)

        def bstep(i, cary):
            cr, ci = cary
            r0 = pl.multiple_of((TC - 1 - i) * nch, nch)
            gr = gbuf[pl.ds(r0, nch), :HW] + cr
            gi = gbuf[pl.ds(r0, nch), HW:] + ci
            gbuf[pl.ds(r0, nch), :HW] = gr
            gbuf[pl.ds(r0, nch), HW:] = gi
            return ar * gr + ai * gi, ar * gi - ai * gr

        cr, ci = lax.fori_loop(0, TC, bstep, (gcar[:, :HW], gcar[:, HW:]))
        gcar[:, :HW] = cr
        gcar[:, HW:] = ci

        gr = gbuf[:, :HW].reshape(TC, nch, HW)
        gi = gbuf[:, HW:].reshape(TC, nch, HW)
        hpr = hbuf[0:rows, :HW].reshape(TC, nch, HW)
        hpi = hbuf[0:rows, HW:].reshape(TC, nch, HW)
        da_ref[:, :HW] += jnp.sum(gr * hpr + gi * hpi, axis=0)
        da_ref[:, HW:] += jnp.sum(gi * hpr - gr * hpi, axis=0)

        g = gbuf[...]
        dd4 = jnp.concatenate([jnp.where(fwd, g, 0.0), jnp.where(fwd, 0.0, g)], axis=1).astype(MXU)
        du_ref[...] = _dot_nt(dd4, b4_ref[...])
        db4_ref[...] += _dot_tn(ub, dd4)

    rev = lambda k: (nk - 1 - k, 0)
    return pl.pallas_call(
        body, name="scan_bwd", grid=(nk,),
        in_specs=[BS((rows, SW), rev), BS((rows, SW), rev), BS((None, nch, 2 * HW), lambda k: (nk - 1 - k, 0, 0)),
                  _full((SW, 4 * HW)), _full((nch, 2 * HW)), _full((2 * HW, 2 * SW))],
        out_specs=[BS((rows, SW), rev), _full((SW, 4 * HW)), _full((nch, 2 * HW)), _full((2 * HW, 2 * SW))],
        out_shape=[SDS((rows_all, SW), F32), SDS((SW, 4 * HW), F32), SDS((nch, 2 * HW), F32),
                   SDS((2 * HW, 2 * SW), F32)],
        scratch_shapes=[pltpu.VMEM((nch, 2 * HW), F32), pltpu.VMEM((rows, 2 * HW), F32),
                        pltpu.VMEM((rows + nch, 2 * HW), F32), pltpu.VMEM((rows, 2 * HW), F32)],
        compiler_params=_cp(("arbitrary",), 56),
    )(u, dy, hb, b4, a, c2)


def _post_mix(o, cv, yf, yb, p, h, modtok, ng, dsk, wg, bg, wo, nct):
    B, S, _ = h.shape

    def body(o_ref, cv_ref, yf_ref, yb_ref, u_ref, h_ref, mod_ref, g_ref, dsk_ref, wg_ref, bg_ref, wo_ref,
             h1_ref, mix_ref, m_ref, y_ref):
        y = yf_ref[...] + yb_ref[...] + dsk_ref[...] * u_ref[...]
        g = _gelu(y)
        s = g * _sigmoid(_dot(g.astype(MXU), wg_ref[...]) + bg_ref[...])
        mix_ref[:, 0:AW] = o_ref[...]
        mix_ref[:, AW:AW + CW] = cv_ref[...].astype(MXU)
        mix_ref[:, AW + CW:] = s.astype(MXU)
        m = _dot(mix_ref[...], wo_ref[...])
        n, _ = _rms(m)
        h1_ref[...] = h_ref[...] + mod_ref[2:3, :] * (n * g_ref[1:2, :])
        m_ref[...] = m
        y_ref[...] = y

    return pl.pallas_call(
        body, name="post_mix", grid=(B, S // TB),
        in_specs=[_tok(AW), _tok(CW), _tok(SW), _tok(SW), _tok(SW, (INW - SW) // SW), _tok(D), _mod_spec(nct),
                  _full((8, D)), _full((1, SW)), _full((SW, SW)), _full((1, SW)), _full((D, D))],
        out_specs=[_tok(D), _tok(D), _tok(D), _tok(SW)],
        out_shape=[SDS((B, S, D), F32), SDS((B, S, D), MXU), SDS((B, S, D), F32), SDS((B, S, SW), F32)],
        compiler_params=_cp(("arbitrary", "arbitrary"), 40),
    )(o, cv, yf, yb, p, h, modtok, ng, dsk, wg, bg, wo)


def _post_bwd(dh1, m, y, p, modtok, ng, dsk, wg, bg, wo, nct):
    B, S, _ = m.shape

    def body(dh_ref, m_ref, y_ref, u_ref, mod_ref, g_ref, dsk_ref, wg_ref, bg_ref, wo_ref,
             dm_ref, da_ref, dc_ref, dy_ref, du_ref, vec_ref, vec2_ref, dwg_ref):
        @pl.when(jnp.logical_and(pl.program_id(0) == 0, pl.program_id(1) == 0))
        def _():
            dwg_ref[...] = jnp.zeros_like(dwg_ref)

        n, r = _rms(m_ref[...])
        dh1 = dh_ref[...]
        gpm = g_ref[1:2, :]
        dr = dh1 * mod_ref[2:3, :]
        vec_ref[...] = jnp.zeros_like(vec_ref)
        vec_ref[0:1, :] = _colsum(dh1 * (n * gpm))
        vec_ref[1:2, :] = _colsum(dr * n)
        dm = _rms_bwd(dr * gpm, n, r).astype(MXU)
        dm_ref[...] = dm
        dmix = _dot_nt(dm, wo_ref[...])
        da_ref[...] = dmix[:, 0:AW]
        dc_ref[...] = dmix[:, AW:AW + CW]
        ds = dmix[:, AW + CW:]
        yv = y_ref[...]
        g = _gelu(yv)
        gb = g.astype(MXU)
        sg = _sigmoid(_dot(gb, wg_ref[...]) + bg_ref[...])
        dv = ds * g * sg * (1.0 - sg)
        dvb = dv.astype(MXU)
        dg = ds * sg + _dot_nt(dvb, wg_ref[...])
        dwg_ref[...] += _dot_tn(gb, dvb)
        dy = dg * _gelu_grad(yv)
        dy_ref[...] = dy
        du_ref[...] = dy * dsk_ref[...]
        vec2_ref[...] = jnp.zeros_like(vec2_ref)
        vec2_ref[0:1, :] = _colsum(dy * u_ref[...])
        vec2_ref[1:2, :] = _colsum(dv)

    nt = S // TB
    return pl.pallas_call(
        body, name="post_bwd", grid=(B, nt),
        in_specs=[_tok(D), _tok(D), _tok(SW), _tok(SW, (INW - SW) // SW), _mod_spec(nct), _full((8, D)),
                  _full((1, SW)), _full((SW, SW)), _full((1, SW)), _full((D, D))],
        out_specs=[_tok(D), _tok(AW), _tok(CW), _tok(SW), _tok(SW),
                   BS((None, None, 8, D), lambda b, j: (b, j, 0, 0)),
                   BS((None, None, 8, SW), lambda b, j: (b, j, 0, 0)), _full((SW, SW))],
        out_shape=[SDS((B, S, D), MXU), SDS((B, S, AW), F32), SDS((B, S, CW), F32), SDS((B, S, SW), F32),
                   SDS((B, S, SW), F32), SDS((B, nt, 8, D), F32), SDS((B, nt, 8, SW), F32), SDS((SW, SW), F32)],
        compiler_params=_cp(("arbitrary", "arbitrary"), 40),
    )(dh1, m, y, p, modtok, ng, dsk, wg, bg, wo)


def _mlp_fwd(h1, modtok, ng, w1, w2, nct):
    B, S, _ = h1.shape
    nf = DFF // TF

    def body(h_ref, mod_ref, g_ref, w1_ref, w2_ref, h2_ref, a2_ref, z_ref, f_ref, acc):
        fi = pl.program_id(2)

        @pl.when(fi == 0)
        def _():
            n, _ = _rms(h_ref[...])
            a2_ref[...] = ((n * g_ref[2:3, :]) * (1.0 + mod_ref[4:5, :]) + mod_ref[3:4, :]).astype(MXU)
            acc[...] = jnp.zeros_like(acc)

        z = _dot(a2_ref[...], w1_ref[...])
        z_ref[...] = z.astype(MXU)
        r = jnp.maximum(z, 0.0)
        acc[...] += _dot((r * r).astype(MXU), w2_ref[...])

        @pl.when(fi == nf - 1)
        def _():
            ff = acc[...]
            f_ref[...] = ff
            n, _ = _rms(ff)
            h2_ref[...] = h_ref[...] + mod_ref[5:6, :] * (n * g_ref[3:4, :])

    return pl.pallas_call(
        body, name="mlp_fwd", grid=(B, S // TB, nf),
        in_specs=[_tok3(D, 0), _mod_spec3(nct), _full((8, D)), BS((D, TF), lambda b, j, f: (0, f)),
                  BS((TF, D), lambda b, j, f: (f, 0))],
        out_specs=[_tok3(D, 0), _tok3(D, 0), _tok3(TF), _tok3(D, 0)],
        out_shape=[SDS((B, S, D), F32), SDS((B, S, D), MXU), SDS((B, S, DFF), MXU), SDS((B, S, D), F32)],
        scratch_shapes=[pltpu.VMEM((TB, D), F32)],
        compiler_params=_cp(("arbitrary", "arbitrary", "arbitrary"), 48),
    )(h1, modtok, ng, w1, w2)


def _mlp_bwd(dh2, h1, f, z, modtok, ng, w1, w2, nct):
    B, S, _ = h1.shape
    nf = DFF // TF

    def body(dh_ref, h_ref, f_ref, z_ref, mod_ref, g_ref, w1_ref, w2_ref, dh1_ref, df_ref, dz_ref, vec_ref, acc):
        fi = pl.program_id(2)

        @pl.when(fi == 0)
        def _():
            n, r = _rms(f_ref[...])
            dh2 = dh_ref[...]
            gp = g_ref[3:4, :]
            dr = dh2 * mod_ref[5:6, :]
            vec_ref[...] = jnp.zeros_like(vec_ref)
            vec_ref[3:4, :] = _colsum(dh2 * (n * gp))
            vec_ref[4:5, :] = _colsum(dr * n)
            df_ref[...] = _rms_bwd(dr * gp, n, r).astype(MXU)
            acc[...] = jnp.zeros_like(acc)

        dr2 = _dot_nt(df_ref[...], w2_ref[...])
        dz = (dr2 * 2.0 * jnp.maximum(z_ref[...].astype(F32), 0.0)).astype(MXU)
        dz_ref[...] = dz
        acc[...] += _dot_nt(dz, w1_ref[...])

        @pl.when(fi == nf - 1)
        def _():
            da = acc[...]
            n, r = _rms(h_ref[...])
            g = g_ref[2:3, :]
            sc1 = 1.0 + mod_ref[4:5, :]
            vec_ref[0:1, :] = _colsum(da)
            vec_ref[1:2, :] = _colsum(da * (n * g))
            vec_ref[2:3, :] = _colsum(da * sc1 * n)
            dh1_ref[...] = dh_ref[...] + _rms_bwd(da * sc1 * g, n, r)

    return pl.pallas_call(
        body, name="mlp_bwd", grid=(B, S // TB, nf),
        in_specs=[_tok3(D, 0), _tok3(D, 0), _tok3(D, 0), _tok3(TF), _mod_spec3(nct), _full((8, D)),
                  BS((D, TF), lambda b, j, f: (0, f)), BS((TF, D), lambda b, j, f: (f, 0))],
        out_specs=[_tok3(D, 0), _tok3(D, 0), _tok3(TF), BS((None, None, 8, D), lambda b, j, f: (b, j, 0, 0))],
        out_shape=[SDS((B, S, D), F32), SDS((B, S, D), MXU), SDS((B, S, DFF), MXU), SDS((B, S // TB, 8, D), F32)],
        scratch_shapes=[pltpu.VMEM((TB, D), F32)],
        compiler_params=_cp(("arbitrary", "arbitrary", "arbitrary"), 48),
    )(dh2, h1, f, z, modtok, ng, w1, w2)


def _loss(h, target, Lc):
    B, S, _ = h.shape
    nt, nct = S // TB, Lc // TB

    def body(h_ref, t_ref, dh_ref, l_ref):
        lat = pl.program_id(1) >= nct
        err = jnp.where(lat, h_ref[...] - t_ref[...], 0.0)
        dh_ref[...] = err * (1.0 / D)
        l_ref[...] = jnp.broadcast_to(jnp.sum(err * err, keepdims=True), (8, 128))

    return pl.pallas_call(
        body, name="loss", grid=(B, nt),
        in_specs=[_tok(D), BS((None, TB, D), lambda b, j: (b, jnp.maximum(j - nct, 0), 0))],
        out_specs=[_tok(D), BS((None, None, 8, 128), lambda b, j: (b, j, 0, 0))],
        out_shape=[SDS((B, S, D), F32), SDS((B, nt, 8, 128), F32)],
        compiler_params=_cp(("arbitrary", "arbitrary")),
    )(h, target)


def _mm_tn(a, b, name, relu2=False):
    T, M = a.shape
    N = b.shape[1]
    def blk(n):
        return max(b for b in range(128, 1025, 128) if n % b == 0)

    bm, bn, tk = blk(M), blk(N), min(T, 512)
    assert T % tk == 0
    nkk = T // tk

    def body(a_ref, b_ref, o_ref):
        @pl.when(pl.program_id(2) == 0)
        def _():
            o_ref[...] = jnp.zeros_like(o_ref)

        av = a_ref[...]
        if relu2:
            r = jnp.maximum(av.astype(F32), 0.0)
            av = (r * r).astype(MXU)
        o_ref[...] += _dot_tn(av, b_ref[...])

    return pl.pallas_call(
        body, name=name, grid=(M // bm, N // bn, nkk),
        in_specs=[BS((tk, bm), lambda i, j, k: (k, i)), BS((tk, bn), lambda i, j, k: (k, j))],
        out_specs=BS((bm, bn), lambda i, j, k: (i, j)),
        out_shape=SDS((M, N), F32),
        compiler_params=_cp(("arbitrary", "arbitrary", "arbitrary"), 40),
    )(a, b)


_PAD_ROWS = 64


def _block_rows(R):
    for br in (256, _PAD_ROWS):
        if R % br == 0:
            return br
    raise ValueError(f"row count {R} is not a multiple of {_PAD_ROWS}")


def _adamw(w, ga, gb, m, v, name, row0=0):
    R, C = w.shape
    br = _block_rows(R)
    off = row0 // br

    def body(w_ref, ga_ref, gb_ref, m_ref, v_ref, g_out, d_out, m_out, v_out):
        g = ga_ref[...] + gb_ref[...]
        m2 = B1 * m_ref[...] + (1.0 - B1) * g
        v2 = B2 * v_ref[...] + (1.0 - B2) * (g * g)
        m_hat = m2 / (1.0 - B1 ** STEP)
        v_hat = v2 / (1.0 - B2 ** STEP)
        g_out[...] = g
        d_out[...] = -LR * (m_hat / (jnp.sqrt(v_hat) + AEPS) + WD * w_ref[...])
        m_out[...] = m2
        v_out[...] = v2

    own = BS((br, C), lambda i: (i, 0))
    shifted = BS((br, C), lambda i: (i + off, 0))
    return pl.pallas_call(
        body, name=name, grid=(R // br,),
        in_specs=[own, shifted, shifted, own, own],
        out_specs=[own] * 4, out_shape=[SDS((R, C), F32)] * 4,
        compiler_params=_cp(("arbitrary",)),
    )(w, ga, gb, m, v)


def _my_xyc():
    return lax.axis_index("x"), lax.axis_index("y"), lax.axis_index("c")


def _chip_peers(x, y):
    return [(1 - x, y), (x, 1 - y), (1 - x, 1 - y)]


def _gather_chips(shard, name):
    R, C = shard.shape

    def body(s_ref, o_ref, ssem, rsem, lsem):
        x, y, c = _my_xyc()
        mine = pltpu.make_async_copy(s_ref, o_ref.at[2 * x + y], lsem)
        mine.start()
        cps = []
        for j, (px, py) in enumerate(_chip_peers(x, y)):
            cps.append(pltpu.make_async_remote_copy(
                src_ref=s_ref, dst_ref=o_ref.at[2 * x + y], send_sem=ssem.at[j], recv_sem=rsem.at[j],
                device_id=(px, py, c), device_id_type=MESH))
            cps[-1].start()
        for j, (px, py) in enumerate(_chip_peers(x, y)):
            pltpu.make_async_remote_copy(
                src_ref=s_ref, dst_ref=o_ref.at[2 * px + py], send_sem=ssem.at[j], recv_sem=rsem.at[j],
                device_id=(px, py, c), device_id_type=MESH).wait_recv()
        for cp in cps:
            cp.wait_send()
        mine.wait()

    return pl.pallas_call(
        body, name=name, in_specs=[ANY], out_specs=ANY,
        out_shape=SDS((NCHIP, R, C), shard.dtype),
        scratch_shapes=[pltpu.SemaphoreType.DMA((3,)), pltpu.SemaphoreType.DMA((3,)), pltpu.SemaphoreType.DMA],
    )(shard)


def _scatter_chips(send, name):
    _, R, C = send.shape

    def body(s_ref, o_ref, ssem, rsem):
        x, y, c = _my_xyc()
        cps = []
        for j, (px, py) in enumerate(_chip_peers(x, y)):
            cps.append(pltpu.make_async_remote_copy(
                src_ref=s_ref.at[2 * px + py], dst_ref=o_ref.at[j], send_sem=ssem.at[j], recv_sem=rsem.at[j],
                device_id=(px, py, c), device_id_type=MESH))
            cps[-1].start()
        for cp in cps:
            cp.wait()

    return pl.pallas_call(
        body, name=name, in_specs=[ANY], out_specs=ANY,
        out_shape=SDS((3, R, C), send.dtype),
        scratch_shapes=[pltpu.SemaphoreType.DMA((3,)), pltpu.SemaphoreType.DMA((3,))],
    )(send)


def _swap_cores(v, name):
    def body(s_ref, o_ref, ssem, rsem):
        x, y, c = _my_xyc()
        cp = pltpu.make_async_remote_copy(src_ref=s_ref, dst_ref=o_ref, send_sem=ssem, recv_sem=rsem,
                                          device_id=(x, y, 1 - c), device_id_type=MESH)
        cp.start()
        cp.wait()

    return pl.pallas_call(
        body, name=name, in_specs=[ANY], out_specs=ANY, out_shape=SDS(v.shape, v.dtype),
        scratch_shapes=[pltpu.SemaphoreType.DMA, pltpu.SemaphoreType.DMA],
    )(v)


def _gather_all(v, name):
    R, C = v.shape

    def body(s_ref, o_ref, ssem, rsem, lsem):
        x, y, c = _my_xyc()
        me = 4 * x + 2 * y + c
        mine = pltpu.make_async_copy(s_ref, o_ref.at[me], lsem)
        mine.start()

        def peer(j):
            fx, fy, fc = (j >> 2) & 1, (j >> 1) & 1, j & 1
            return (x ^ fx, y ^ fy, c ^ fc)

        cps = []
        for j in range(1, 8):
            cps.append(pltpu.make_async_remote_copy(
                src_ref=s_ref, dst_ref=o_ref.at[me], send_sem=ssem.at[j - 1], recv_sem=rsem.at[j - 1],
                device_id=peer(j), device_id_type=MESH))
            cps[-1].start()
        for j in range(1, 8):
            px, py, pc = peer(j)
            pltpu.make_async_remote_copy(
                src_ref=s_ref, dst_ref=o_ref.at[4 * px + 2 * py + pc], send_sem=ssem.at[j - 1],
                recv_sem=rsem.at[j - 1], device_id=peer(j), device_id_type=MESH).wait_recv()
        for cp in cps:
            cp.wait_send()
        mine.wait()

    return pl.pallas_call(
        body, name=name, in_specs=[ANY], out_specs=ANY, out_shape=SDS((8, R, C), v.dtype),
        scratch_shapes=[pltpu.SemaphoreType.DMA((7,)), pltpu.SemaphoreType.DMA((7,)), pltpu.SemaphoreType.DMA],
    )(v)


def _sum_slots(own, slots, name):
    n, R, C = slots.shape
    br = _block_rows(R)

    def body(*refs):
        o_ref = refs[-1]
        if own is None:
            acc = refs[0][0].astype(F32)
            first = 1
            s_ref = refs[0]
        else:
            acc = refs[0][...]
            first = 0
            s_ref = refs[1]
        for k in range(first, n):
            acc = acc + s_ref[k].astype(F32)
        o_ref[...] = acc

    row = BS((br, C), lambda i: (i, 0))
    slab = BS((n, br, C), lambda i: (0, i, 0))
    ins, args = ([slab], [slots]) if own is None else ([row, slab], [own, slots])
    return pl.pallas_call(
        body, name=name, grid=(R // br,), in_specs=ins, out_specs=row, out_shape=SDS((R, C), F32),
        compiler_params=_cp(("arbitrary",)),
    )(*args)


def _rope_tables(L, Lc):
    n = jnp.arange(L)
    row = (n // GRID_W).astype(F32)
    col = (n % GRID_W).astype(F32)
    freqs = ROPE_BASE ** (-jnp.arange(16, dtype=F32) / 16)
    lane = jnp.arange(128)
    dd = lane % HD
    fr = freqs[dd % 16]
    ang = jnp.where(dd < 32, row[:, None], col[:, None]) * fr[None, :]
    sign = jnp.where((dd % 32) < 16, -1.0, 1.0)
    cos = jnp.concatenate([jnp.ones((Lc, 128), F32), jnp.cos(ang)], axis=0)
    sin = jnp.concatenate([jnp.zeros((Lc, 128), F32), jnp.sin(ang) * sign[None, :]], axis=0)
    return cos, sin


def _ssm_prep(lam_re, lam_im, log_dt, b_re, b_im, c_re, c_im, B):
    dt = jnp.exp(log_dt)[..., None]
    mag = jnp.exp(lam_re * dt)
    ar = mag * jnp.cos(lam_im * dt)
    ai = mag * jnp.sin(lam_im * dt)
    den = lam_re * lam_re + lam_im * lam_im
    kr = ((ar - 1.0) * lam_re + ai * lam_im) / den
    ki = (ai * lam_re - (ar - 1.0) * lam_im) / den
    bbr = kr[..., None] * b_re - ki[..., None] * b_im
    bbi = kr[..., None] * b_im + ki[..., None] * b_re
    eye = jnp.eye(SGRP, dtype=F32)

    def bblk(t):
        return jnp.einsum("gpi,gh->gihp", t, eye).reshape(SGRP * SCH, SGRP * SST)

    def cblk(t):
        return jnp.einsum("gip,gh->gphi", t, eye).reshape(SGRP * SST, SGRP * SCH)

    b4 = jnp.concatenate([bblk(bbr[0]), bblk(bbi[0]), bblk(bbr[1]), bblk(bbi[1])], axis=1)
    c2 = jnp.concatenate([jnp.concatenate([cblk(c_re[0]), -cblk(c_im[0])], axis=0),
                          jnp.concatenate([cblk(c_re[1]), -cblk(c_im[1])], axis=0)], axis=1)
    a2 = jnp.concatenate([ar.reshape(2, -1), ai.reshape(2, -1)], axis=1)
    a = jnp.repeat(a2, B, axis=0)
    return a, b4, c2


def _to_chains(u, Lc):
    B, S, W = u.shape
    rev = jnp.concatenate([u[:, :Lc][:, ::-1], u[:, Lc:][:, ::-1]], axis=1)
    both = jnp.concatenate([u, rev], axis=0)
    return jnp.transpose(both, (1, 0, 2)).reshape(S * 2 * B, W)


def _from_chains(y, B, Lc):
    W = y.shape[-1]
    S = y.shape[0] // (2 * B)
    both = jnp.transpose(y.reshape(S, 2 * B, W), (1, 0, 2))
    f, r = both[:B], both[B:]
    return f, jnp.concatenate([r[:, :Lc][:, ::-1], r[:, Lc:][:, ::-1]], axis=1)


def _pad_rows(v, rows=8):
    return jnp.concatenate([v, jnp.zeros((rows - v.shape[0],) + v.shape[1:], v.dtype)], axis=0)


_BIG = ("w_ada", "w_in", "w_out", "w_mlp_in", "w_mlp_out")
_BIG_AXIS = {"w_ada": 2, "w_in": 2, "w_out": 1, "w_mlp_in": 2, "w_mlp_out": 1}


def _split_shards(g, axis):
    L, r, c = g.shape
    if axis == 2:
        return jnp.transpose(g.reshape(L, r, NCHIP, c // NCHIP), (2, 0, 1, 3))
    return jnp.transpose(g.reshape(L, NCHIP, r // NCHIP, c), (1, 0, 2, 3))


def _join_shards(s, axis):
    _, L, r, c = s.shape
    if axis == 2:
        return jnp.transpose(s, (1, 2, 0, 3)).reshape(L, r, NCHIP * c)
    return jnp.transpose(s, (1, 0, 2, 3)).reshape(L, NCHIP * r, c)


def _local_step(x, c, ctx, loss_target, P):
    B, L, _ = x.shape
    Lc = ctx.shape[1]
    S = Lc + L
    nct = Lc // TB
    nt = S // TB
    nch = 2 * B
    cos, sin = _rope_tables(L, Lc)

    c_act = jax.nn.silu(c)
    cc_act, silu_vjp = jax.vjp(jax.nn.silu, P["c_ctx"])
    cact8 = _pad_rows(jnp.concatenate([c_act, cc_act[None, :]], axis=0))
    h = jnp.concatenate([ctx, x], axis=1)

    saved = []
    for l in range(NLAYER):
        mod8 = _ada_fwd(cact8, P["w_ada"][l], P["b_ada"][l][None, :])
        mod6 = mod8.reshape(8, NMOD, D)
        modtok = jnp.stack([jnp.broadcast_to(mod6[B], (B, NMOD, D)), mod6[:B]], axis=1)
        modtok = jnp.concatenate([modtok, jnp.zeros((B, 2, 2, D), F32)], axis=2)
        ng = _pad_rows(P["norm_g"][l])
        cw8 = _pad_rows(P["conv_w"][l])
        (a_s, b4, c2), prep_vjp = jax.vjp(
            lambda *t: _ssm_prep(*t, B), P["ssm_lam_re"][l], P["ssm_lam_im"][l], P["ssm_log_dt"][l],
            P["ssm_b_re"][l], P["ssm_b_im"][l], P["ssm_c_re"][l], P["ssm_c_im"][l])
        b4m, c2m = b4.astype(MXU), c2.astype(MXU)
        wg = P["w_glu"][l].astype(MXU)
        dsk = P["ssm_d"][l][None, :]
        bg = P["b_glu"][l][None, :]

        p, a = _pre_mix(h, modtok, ng, P["w_in"][l], nct)
        qk = _rope(p, cos, sin, AW + KVW, MXU, "rope_fwd")
        o = _attn_fwd(P["attn_sink"][l], qk, p, Lc)
        cv = _conv_fwd(p, cw8, nct)
        u = _to_chains(p[:, :, INW - SW:], Lc)
        ych, hb = _scan_fwd(u, b4m, a_s, c2m, nch)
        yf, yb = _from_chains(ych, B, Lc)
        h1, mix, m, y = _post_mix(o, cv, yf, yb, p, h, modtok, ng, dsk, wg, bg, P["w_out"][l], nct)
        h2, a2, z, f = _mlp_fwd(h1, modtok, ng, P["w_mlp_in"][l], P["w_mlp_out"][l], nct)
        saved.append(dict(h=h, modtok=modtok, ng=ng, cw8=cw8, a_s=a_s, b4m=b4m, c2m=c2m, wg=wg, dsk=dsk, bg=bg,
                          p=p, a=a, qk=qk, u=u, hb=hb, mix=mix, m=m, y=y, h1=h1, a2=a2, z=z, f=f,
                          prep_vjp=prep_vjp))
        h = h2

    dh, lpart = _loss(h, loss_target, Lc)
    loss_local = 0.5 / D * jnp.sum(lpart[:, :, 0, 0])

    G = {k: [None] * NLAYER for k in ("w_ada", "b_ada", "norm_g", "w_in", "conv_w", "attn_sink", "ssm_lam_re",
                                      "ssm_lam_im", "ssm_log_dt", "ssm_b_re", "ssm_b_im", "ssm_c_re", "ssm_c_im",
                                      "ssm_d", "w_glu", "b_glu", "w_out", "w_mlp_in", "w_mlp_out")}
    dcact = jnp.zeros((8, D), F32)
    T = B * S
    for l in reversed(range(NLAYER)):
        sv = saved[l]
        modtok, ng = sv["modtok"], sv["ng"]
        dh1, df, dz, vec_m = _mlp_bwd(dh, sv["h1"], sv["f"], sv["z"], modtok, ng, P["w_mlp_in"][l],
                                      P["w_mlp_out"][l], nct)
        G["w_mlp_out"][l] = _mm_tn(sv["z"].reshape(T, DFF), df.reshape(T, D), "dw_mlp_out", relu2=True)
        G["w_mlp_in"][l] = _mm_tn(sv["a2"].reshape(T, D), dz.reshape(T, DFF), "dw_mlp_in")
        dm, dattn, dcv, dy, du_skip, vec_p, vec_p2, dwg = _post_bwd(
            dh1, sv["m"], sv["y"], sv["p"], modtok, ng, sv["dsk"], sv["wg"], sv["bg"], P["w_out"][l], nct)
        G["w_out"][l] = _mm_tn(sv["mix"].reshape(T, D), dm.reshape(T, D), "dw_out")
        G["w_glu"][l] = dwg
        G["ssm_d"][l] = jnp.sum(vec_p2[:, :, 0, :], axis=(0, 1))
        G["b_glu"][l] = jnp.sum(vec_p2[:, :, 1, :], axis=(0, 1))
        dych = _to_chains(dy, Lc)
        duch, db4, da_s, dc2 = _scan_bwd(sv["u"], dych, sv["hb"], sv["b4m"], sv["a_s"], sv["c2m"], nch)
        duf, dub = _from_chains(duch, B, Lc)
        du = du_skip + duf + dub
        gs = sv["prep_vjp"]((da_s, db4, dc2))
        for k, name in enumerate(("ssm_lam_re", "ssm_lam_im", "ssm_log_dt", "ssm_b_re", "ssm_b_im", "ssm_c_re",
                                  "ssm_c_im")):
            G[name][l] = gs[k]
        dconv, dcw = _conv_bwd(sv["p"], dcv, sv["cw8"], nct)
        G["conv_w"][l] = jnp.sum(dcw[:, :, 0:3, :], axis=(0, 1))
        dq_r, dk_r, dv, dsink = _attn_bwd(P["attn_sink"][l], sv["qk"], sv["p"], dattn, Lc)
        G["attn_sink"][l] = jnp.sum(dsink[:, :, 0], axis=0)
        dq = _rope(dq_r, cos, -sin, AW, MXU, "rope_bwd_q")
        dk = _rope(dk_r, cos, -sin, KVW, MXU, "rope_bwd_k")
        dp = jnp.concatenate([dq, dk, dv.astype(MXU), dconv.astype(MXU), du.astype(MXU)], axis=-1)
        G["w_in"][l] = _mm_tn(sv["a"].reshape(T, D), dp.reshape(T, INW), "dw_in")
        dh, vec_i = _pre_bwd(dp, dh1, sv["h"], modtok, ng, P["w_in"][l], nct)
        G["norm_g"][l] = jnp.stack([jnp.sum(vec_i[:, :, 2], axis=(0, 1)), jnp.sum(vec_p[:, :, 1], axis=(0, 1)),
                                    jnp.sum(vec_m[:, :, 2], axis=(0, 1)), jnp.sum(vec_m[:, :, 4], axis=(0, 1))])
        dmod_blk = jnp.stack([vec_i[:, :, 0], vec_i[:, :, 1], vec_p[:, :, 0],
                              vec_m[:, :, 0], vec_m[:, :, 1], vec_m[:, :, 3]], axis=2)
        dmod_lat = jnp.sum(dmod_blk[:, nct:], axis=1).reshape(B, NMOD * D)
        dmod_ctx = jnp.sum(dmod_blk[:, :nct], axis=(0, 1)).reshape(1, NMOD * D)
        dmod8 = _pad_rows(jnp.concatenate([dmod_lat, dmod_ctx], axis=0))
        G["b_ada"][l] = jnp.sum(dmod8, axis=0)
        dmod8m = dmod8.astype(MXU)
        G["w_ada"][l] = _mm_tn(cact8.astype(MXU), dmod8m, "dw_ada")
        dcact = dcact + _ada_bwd_c(dmod8m, P["w_ada"][l])

    grads = {k: jnp.stack(v) for k, v in G.items()}
    grads["c_ctx"] = silu_vjp(dcact[B])[0]
    return loss_local, dh[:, Lc:, :], grads


_WEIGHTS = ["c_ctx", "w_ada", "b_ada", "norm_g", "w_in", "conv_w", "attn_sink", "ssm_lam_re", "ssm_lam_im",
            "ssm_log_dt", "ssm_b_re", "ssm_b_im", "ssm_c_re", "ssm_c_im", "ssm_d", "w_glu", "b_glu", "w_out",
            "w_mlp_in", "w_mlp_out"]
_SMALL = [n for n in _WEIGHTS if n not in _BIG]
_SMALL_SHARDED = {"norm_g": 2, "conv_w": 2, "w_glu": 1}
_LANES = 1024


def _flat_pad(parts):
    v = jnp.concatenate([p.reshape(-1) for p in parts])
    rows = -(-v.shape[0] // (_PAD_ROWS * _LANES)) * _PAD_ROWS
    return jnp.concatenate([v, jnp.zeros((rows * _LANES - v.shape[0],), v.dtype)]).reshape(rows, _LANES)


def _unflat(flat, shapes):
    v = flat.reshape(-1)
    out, off = [], 0
    for s in shapes:
        n = math.prod(s)
        out.append(v[off:off + n].reshape(s))
        off += n
    return out


def kernel(x, c, ctx, c_ctx, w_ada, b_ada, norm_g, w_in, conv_w, attn_sink, ssm_lam_re, ssm_lam_im, ssm_log_dt, ssm_b_re, ssm_b_im, ssm_c_re, ssm_c_im, ssm_d, w_glu, b_glu, w_out, w_mlp_in, w_mlp_out, loss_target, m_c_ctx, m_w_ada, m_b_ada, m_norm_g, m_w_in, m_conv_w, m_attn_sink, m_ssm_lam_re, m_ssm_lam_im, m_ssm_log_dt, m_ssm_b_re, m_ssm_b_im, m_ssm_c_re, m_ssm_c_im, m_ssm_d, m_w_glu, m_b_glu, m_w_out, m_w_mlp_in, m_w_mlp_out, v_c_ctx, v_w_ada, v_b_ada, v_norm_g, v_w_in, v_conv_w, v_attn_sink, v_ssm_lam_re, v_ssm_lam_im, v_ssm_log_dt, v_ssm_b_re, v_ssm_b_im, v_ssm_c_re, v_ssm_c_im, v_ssm_d, v_w_glu, v_b_glu, v_w_out, v_w_mlp_in, v_w_mlp_out):
    W = dict(c_ctx=c_ctx, w_ada=w_ada, b_ada=b_ada, norm_g=norm_g, w_in=w_in, conv_w=conv_w, attn_sink=attn_sink,
             ssm_lam_re=ssm_lam_re, ssm_lam_im=ssm_lam_im, ssm_log_dt=ssm_log_dt, ssm_b_re=ssm_b_re,
             ssm_b_im=ssm_b_im, ssm_c_re=ssm_c_re, ssm_c_im=ssm_c_im, ssm_d=ssm_d, w_glu=w_glu, b_glu=b_glu,
             w_out=w_out, w_mlp_in=w_mlp_in, w_mlp_out=w_mlp_out)
    M = dict(c_ctx=m_c_ctx, w_ada=m_w_ada, b_ada=m_b_ada, norm_g=m_norm_g, w_in=m_w_in, conv_w=m_conv_w,
             attn_sink=m_attn_sink, ssm_lam_re=m_ssm_lam_re, ssm_lam_im=m_ssm_lam_im, ssm_log_dt=m_ssm_log_dt,
             ssm_b_re=m_ssm_b_re, ssm_b_im=m_ssm_b_im, ssm_c_re=m_ssm_c_re, ssm_c_im=m_ssm_c_im, ssm_d=m_ssm_d,
             w_glu=m_w_glu, b_glu=m_b_glu, w_out=m_w_out, w_mlp_in=m_w_mlp_in, w_mlp_out=m_w_mlp_out)
    V = dict(c_ctx=v_c_ctx, w_ada=v_w_ada, b_ada=v_b_ada, norm_g=v_norm_g, w_in=v_w_in, conv_w=v_conv_w,
             attn_sink=v_attn_sink, ssm_lam_re=v_ssm_lam_re, ssm_lam_im=v_ssm_lam_im, ssm_log_dt=v_ssm_log_dt,
             ssm_b_re=v_ssm_b_re, ssm_b_im=v_ssm_b_im, ssm_c_re=v_ssm_c_re, ssm_c_im=v_ssm_c_im, ssm_d=v_ssm_d,
             w_glu=v_w_glu, b_glu=v_b_glu, w_out=v_w_out, w_mlp_in=v_w_mlp_in, w_mlp_out=v_w_mlp_out)
    q_me = 2 * lax.axis_index("x") + lax.axis_index("y")

    big_rows = [W[n].size // _LANES for n in _BIG]
    big_off = [sum(big_rows[:k]) for k in range(len(_BIG))]
    wsh = jnp.concatenate([W[n].reshape(-1, _LANES) for n in _BIG], axis=0).astype(MXU)
    wall = _gather_chips(wsh, "gather_big")
    P = dict(W)
    for n, r0, nr in zip(_BIG, big_off, big_rows):
        P[n] = _join_shards(wall[:, r0:r0 + nr].reshape((NCHIP,) + W[n].shape), _BIG_AXIS[n])
    ssh_names = list(_SMALL_SHARDED)
    ssh = _flat_pad([W[n] for n in ssh_names])
    sall = _gather_chips(ssh, "gather_small")
    parts = [_unflat(sall[q], [W[n].shape for n in ssh_names]) for q in range(NCHIP)]
    for k, n in enumerate(ssh_names):
        P[n] = _join_shards(jnp.stack([parts[q][k] for q in range(NCHIP)]), _SMALL_SHARDED[n])

    loss_local, grad_x, G = _local_step(x, c, ctx, loss_target, P)
    loss = lax.psum(loss_local, ("x", "y", "c"))

    split = [_split_shards(G[n], _BIG_AXIS[n]).reshape(NCHIP, -1, _LANES) for n in _BIG]
    send = jnp.concatenate(split, axis=1)
    own = lax.dynamic_index_in_dim(send, q_me, axis=0, keepdims=False)
    recv = _scatter_chips(send.astype(MXU), "scatter_big")
    part = _sum_slots(own, recv, "sum_chips")
    sib = _swap_cores(part, "swap_big")

    out = {}
    for n, r0, nr in zip(_BIG, big_off, big_rows):
        res = _adamw(W[n].reshape(nr, _LANES), part, sib, M[n].reshape(nr, _LANES), V[n].reshape(nr, _LANES),
                     "adamw_" + n, row0=r0)
        out[n] = [t.reshape(W[n].shape) for t in res]

    gsmall = _flat_pad([G[n] for n in _SMALL])
    gsum = _sum_slots(None, _gather_all(gsmall, "gather_grads"), "sum_devices")
    gfull = dict(zip(_SMALL, _unflat(gsum, [G[n].shape for n in _SMALL])))
    for n, ax in _SMALL_SHARDED.items():
        width = W[n].shape[ax]
        gfull[n] = lax.dynamic_slice_in_dim(gfull[n], q_me * width, width, axis=ax)
    shapes = [W[n].shape for n in _SMALL]
    gflat = _flat_pad([gfull[n] for n in _SMALL])
    res = _adamw(_flat_pad([W[n] for n in _SMALL]), gflat, jnp.zeros_like(gflat),
                 _flat_pad([M[n] for n in _SMALL]), _flat_pad([V[n] for n in _SMALL]), "adamw_small")
    for k, t in enumerate(res):
        for n, piece in zip(_SMALL, _unflat(t, shapes)):
            out.setdefault(n, [None] * 4)[k] = piece

    return (loss, grad_x, *[out[n][0] for n in _WEIGHTS], *[out[n][1] for n in _WEIGHTS],
            *[out[n][2] for n in _WEIGHTS], *[out[n][3] for n in _WEIGHTS])
```

```python
import functools
import math

import jax
import jax.numpy as jnp
import numpy as np
from jax import lax
from jax.experimental import pallas as pl
from jax.experimental.pallas import tpu as pltpu

F32 = jnp.float32
MXU = jnp.bfloat16

D = 1024
DFF = 4096
NMOD = 6
EPS = 1e-6
HD = 64
NQ = 8
NKV = 2
QPK = 4
AW = 512
KVW = 128
WIN = 128
QB = 128
CW = 256
SW = 256
SGRP = 16
SCH = 16
SST = 64
INW = 1792
GRID_W = 64
ROPE_BASE = 10000.0
NEG = -1e30
SCALE = HD ** -0.5
NLAYER = 4

TB = 256
TF = 1024
TC = 32
NCHIP = 4

LR, B1, B2, AEPS, WD, STEP = 0.001, 0.9, 0.999, 1e-08, 0.01, 10

MESH = pl.DeviceIdType.MESH
SDS = jax.ShapeDtypeStruct
BS = pl.BlockSpec
ANY = pl.BlockSpec(memory_space=pl.ANY)


def _cp(sem, vmem_mb=None):
    kw = dict(dimension_semantics=sem)
    if vmem_mb is not None:
        kw["vmem_limit_bytes"] = vmem_mb * 1024 * 1024
    return pltpu.CompilerParams(**kw)


def _dot(a, b):
    return jnp.dot(a, b, preferred_element_type=F32)


def _dot_nt(a, b):
    return lax.dot_general(a, b, (((1,), (1,)), ((), ())), preferred_element_type=F32)


def _dot_tn(a, b):
    return lax.dot_general(a, b, (((0,), (0,)), ((), ())), preferred_element_type=F32)


def _rms(x):
    r = lax.rsqrt(jnp.mean(x * x, axis=-1, keepdims=True) + EPS)
    return x * r, r


def _rms_bwd(dn, n, r):
    return r * (dn - n * jnp.mean(dn * n, axis=-1, keepdims=True))


_GC = math.sqrt(2.0 / math.pi)


def _gelu(y):
    return 0.5 * y * (1.0 + jnp.tanh(_GC * (y + 0.044715 * y * y * y)))


def _gelu_grad(y):
    th = jnp.tanh(_GC * (y + 0.044715 * y * y * y))
    return 0.5 * (1.0 + th) + 0.5 * y * (1.0 - th * th) * _GC * (1.0 + 3 * 0.044715 * y * y)


def _sigmoid(v):
    return 1.0 / (1.0 + jnp.exp(-v))


def _colsum(x):
    return jnp.sum(x, axis=0, keepdims=True)


def _tok(width, col=0):
    return BS((None, TB, width), lambda b, j: (b, j, col))


def _mod_spec(nct):
    return BS((None, None, 8, D), lambda b, j: (b, jnp.where(j >= nct, 1, 0), 0, 0))


def _full(shape):
    nd = len(shape)
    return BS(shape, lambda *a: (0,) * nd)


def _resident(shape):
    nd = len(shape)
    return BS(shape, lambda *a: (0,) * nd, pipeline_mode=pl.Buffered(1))


def _ada_fwd(cact8, w, b):
    def body(c_ref, w_ref, b_ref, o_ref):
        o_ref[...] = _dot(c_ref[...].astype(MXU), w_ref[...]) + b_ref[...]

    return pl.pallas_call(
        body, name="ada_fwd", grid=(NMOD,),
        in_specs=[BS((8, D), lambda j: (0, 0)), BS((D, D), lambda j: (0, j)), BS((1, D), lambda j: (0, j))],
        out_specs=BS((8, D), lambda j: (0, j)),
        out_shape=SDS((8, NMOD * D), F32), compiler_params=_cp(("arbitrary",)),
    )(cact8, w, b)


def _ada_bwd_c(dmod8, w):
    def body(d_ref, w_ref, o_ref):
        @pl.when(pl.program_id(0) == 0)
        def _():
            o_ref[...] = jnp.zeros_like(o_ref)
        o_ref[...] += _dot_nt(d_ref[...], w_ref[...])

    return pl.pallas_call(
        body, name="ada_bwd_c", grid=(NMOD,),
        in_specs=[BS((8, D), lambda j: (0, j)), BS((D, D), lambda j: (0, j))],
        out_specs=BS((8, D), lambda j: (0, 0)),
        out_shape=SDS((8, D), F32), compiler_params=_cp(("arbitrary",)),
    )(dmod8, w)


def _pre_mix(h, modtok, ng, w_in, nct):
    B, S, _ = h.shape

    def body(h_ref, mod_ref, g_ref, w_ref, p_ref, a_ref):
        n, _ = _rms(h_ref[...])
        a = ((n * g_ref[0:1, :]) * (1.0 + mod_ref[1:2, :]) + mod_ref[0:1, :]).astype(MXU)
        a_ref[...] = a
        p_ref[...] = _dot(a, w_ref[...])

    return pl.pallas_call(
        body, name="pre_mix", grid=(B, S // TB),
        in_specs=[_tok(D), _mod_spec(nct), _full((8, D)), _full((D, INW))],
        out_specs=[_tok(INW), _tok(D)],
        out_shape=[SDS((B, S, INW), F32), SDS((B, S, D), MXU)],
        compiler_params=_cp(("arbitrary", "arbitrary"), 40),
    )(h, modtok, ng, w_in)


def _pre_bwd(dp, dh_part, h, modtok, ng, w_in, nct):
    B, S, _ = h.shape

    def body(dp_ref, dhp_ref, h_ref, mod_ref, g_ref, w_ref, dh_ref, vec_ref):
        da = _dot_nt(dp_ref[...], w_ref[...])
        n, r = _rms(h_ref[...])
        g = g_ref[0:1, :]
        sc1 = 1.0 + mod_ref[1:2, :]
        vec_ref[...] = jnp.zeros_like(vec_ref)
        vec_ref[0:1, :] = _colsum(da)
        vec_ref[1:2, :] = _colsum(da * (n * g))
        vec_ref[2:3, :] = _colsum(da * sc1 * n)
        dh_ref[...] = dhp_ref[...] + _rms_bwd(da * sc1 * g, n, r)

    return pl.pallas_call(
        body, name="pre_bwd", grid=(B, S // TB),
        in_specs=[_tok(INW), _tok(D), _tok(D), _mod_spec(nct), _full((8, D)), _full((D, INW))],
        out_specs=[_tok(D), BS((None, None, 8, D), lambda b, j: (b, j, 0, 0))],
        out_shape=[SDS((B, S, D), F32), SDS((B, S // TB, 8, D), F32)],
        compiler_params=_cp(("arbitrary", "arbitrary"), 40),
    )(dp, dh_part, h, modtok, ng, w_in)


def _rope(x, cos, sin, width, out_dtype, name):
    B, S, _ = x.shape

    def body(x_ref, c_ref, s_ref, o_ref):
        lane = lax.broadcasted_iota(jnp.int32, (TB, 128), 1)
        first = (lane % 32) < 16
        c = c_ref[...]
        s = s_ref[...]
        for k in range(width // 128):
            t = x_ref[:, k * 128:(k + 1) * 128].astype(F32)
            partner = jnp.where(first, pltpu.roll(t, 112, 1), pltpu.roll(t, 16, 1))
            o_ref[:, k * 128:(k + 1) * 128] = (t * c + partner * s).astype(out_dtype)

    tab = BS((TB, 128), lambda b, j: (j, 0))
    return pl.pallas_call(
        body, name=name, grid=(B, S // TB),
        in_specs=[_tok(width), tab, tab],
        out_specs=_tok(width),
        out_shape=SDS((B, S, width), out_dtype),
        compiler_params=_cp(("arbitrary", "arbitrary")),
    )(x, cos, sin)


def _attn_masks(i, S, Lc):
    ncb = Lc // QB
    is_lat = i >= ncb
    w0 = pl.multiple_of(jnp.clip((i - 1) * QB, 0, S - 3 * QB), QB)
    qpos = i * QB + lax.broadcasted_iota(jnp.int32, (QB, 3 * QB), 0)
    kpos = w0 + lax.broadcasted_iota(jnp.int32, (QB, 3 * QB), 1)
    mask = jnp.logical_and(jnp.logical_and(kpos >= Lc, jnp.abs(qpos - kpos) <= WIN), is_lat)
    return w0, mask


def _attn_probs(qh, kch, kwh, mask, sk):
    sc = _dot_nt(qh, kch) * SCALE
    sw = jnp.where(mask, _dot_nt(qh, kwh) * SCALE, NEG)
    m = jnp.maximum(jnp.maximum(jnp.max(sc, axis=-1, keepdims=True), jnp.max(sw, axis=-1, keepdims=True)), sk)
    ec = jnp.exp(sc - m)
    ew = jnp.exp(sw - m)
    es = jnp.exp(sk - m)
    inv = 1.0 / (jnp.sum(ec, axis=-1, keepdims=True) + jnp.sum(ew, axis=-1, keepdims=True) + es)
    return ec * inv, ew * inv, es * inv


def _attn_fwd(sink, qk, p, Lc):
    B, S, _ = qk.shape

    def body(sink_ref, q_ref, k_ref, v_ref, o_ref):
        w0, mask = _attn_masks(pl.program_id(1), S, Lc)
        kc = k_ref[0:Lc, :]
        vc = v_ref[0:Lc, :].astype(MXU)
        kw = k_ref[pl.ds(w0, 3 * QB), :]
        vw = v_ref[pl.ds(w0, 3 * QB), :].astype(MXU)
        outs = []
        for hk in range(NKV):
            cs = slice(hk * HD, (hk + 1) * HD)
            for g in range(QPK):
                hq = hk * QPK + g
                qs = slice(hq * HD, (hq + 1) * HD)
                pc, pw, _ = _attn_probs(q_ref[:, qs], kc[:, cs], kw[:, cs], mask, sink_ref[hq])
                outs.append(_dot(pc.astype(MXU), vc[:, cs]) + _dot(pw.astype(MXU), vw[:, cs]))
        o_ref[...] = jnp.concatenate(outs, axis=1).astype(o_ref.dtype)

    return pl.pallas_call(
        body, name="attn_fwd", grid=(B, S // QB),
        in_specs=[BS(memory_space=pltpu.SMEM),
                  BS((None, QB, AW), lambda b, i: (b, i, 0)),
                  BS((None, S, KVW), lambda b, i: (b, 0, AW // KVW)),
                  BS((None, S, KVW), lambda b, i: (b, 0, (AW + KVW) // KVW))],
        out_specs=BS((None, QB, AW), lambda b, i: (b, i, 0)),
        out_shape=SDS((B, S, AW), MXU),
        compiler_params=_cp(("arbitrary", "arbitrary")),
    )(sink, qk, qk, p)


def _attn_bwd(sink, qk, p, do, Lc):
    B, S, _ = qk.shape

    def body(sink_ref, q_ref, k_ref, v_ref, do_ref, dq_ref, dk_ref, dv_ref, ds_ref):
        i = pl.program_id(1)

        @pl.when(i == 0)
        def _():
            dk_ref[...] = jnp.zeros_like(dk_ref)
            dv_ref[...] = jnp.zeros_like(dv_ref)
            ds_ref[...] = jnp.zeros_like(ds_ref)

        w0, mask = _attn_masks(i, S, Lc)
        kc = k_ref[0:Lc, :]
        vc = v_ref[0:Lc, :].astype(MXU)
        kw = k_ref[pl.ds(w0, 3 * QB), :]
        vw = v_ref[pl.ds(w0, 3 * QB), :].astype(MXU)
        dqs, dsinks, dkcs, dkws, dvcs, dvws = [], [], [], [], [], []
        for hk in range(NKV):
            cs = slice(hk * HD, (hk + 1) * HD)
            dkc = jnp.zeros((Lc, HD), F32)
            dkw = jnp.zeros((3 * QB, HD), F32)
            dvc = jnp.zeros((Lc, HD), F32)
            dvw = jnp.zeros((3 * QB, HD), F32)
            for g in range(QPK):
                hq = hk * QPK + g
                qs = slice(hq * HD, (hq + 1) * HD)
                qh = q_ref[:, qs]
                pc, pw, ps = _attn_probs(qh, kc[:, cs], kw[:, cs], mask, sink_ref[hq])
                dob = do_ref[:, qs].astype(MXU)
                dpc = _dot_nt(dob, vc[:, cs])
                dpw = _dot_nt(dob, vw[:, cs])
                delta = jnp.sum(pc * dpc, axis=-1, keepdims=True) + jnp.sum(pw * dpw, axis=-1, keepdims=True)
                dsc = (pc * (dpc - delta) * SCALE).astype(MXU)
                dsw = (pw * (dpw - delta) * SCALE).astype(MXU)
                dqs.append(_dot(dsc, kc[:, cs]) + _dot(dsw, kw[:, cs]))
                dkc = dkc + _dot_tn(dsc, qh)
                dkw = dkw + _dot_tn(dsw, qh)
                dvc = dvc + _dot_tn(pc.astype(MXU), dob)
                dvw = dvw + _dot_tn(pw.astype(MXU), dob)
                dsinks.append(jnp.broadcast_to(-jnp.sum(ps * delta, axis=0, keepdims=True), (1, 128)))
            dkcs.append(dkc)
            dkws.append(dkw)
            dvcs.append(dvc)
            dvws.append(dvw)
        dq_ref[...] = jnp.concatenate(dqs, axis=1)
        ds_ref[...] += jnp.concatenate(dsinks, axis=0)
        dk_ref[0:Lc, :] += jnp.concatenate(dkcs, axis=1)
        dv_ref[0:Lc, :] += jnp.concatenate(dvcs, axis=1)
        dk_ref[pl.ds(w0, 3 * QB), :] += jnp.concatenate(dkws, axis=1)
        dv_ref[pl.ds(w0, 3 * QB), :] += jnp.concatenate(dvws, axis=1)

    kv_acc = BS((None, S, KVW), lambda b, i: (b, 0, 0))
    return pl.pallas_call(
        body, name="attn_bwd", grid=(B, S // QB),
        in_specs=[BS(memory_space=pltpu.SMEM),
                  BS((None, QB, AW), lambda b, i: (b, i, 0)),
                  BS((None, S, KVW), lambda b, i: (b, 0, AW // KVW)),
                  BS((None, S, KVW), lambda b, i: (b, 0, (AW + KVW) // KVW)),
                  BS((None, QB, AW), lambda b, i: (b, i, 0))],
        out_specs=[BS((None, QB, AW), lambda b, i: (b, i, 0)), kv_acc, kv_acc,
                   BS((None, 8, 128), lambda b, i: (b, 0, 0))],
        out_shape=[SDS((B, S, AW), F32), SDS((B, S, KVW), F32), SDS((B, S, KVW), F32), SDS((B, 8, 128), F32)],
        compiler_params=_cp(("arbitrary", "arbitrary")),
    )(sink, qk, qk, p, do)


def _halo_specs(width, col, S):
    nb8 = S // 8
    per = TB // 8
    prev = BS((None, 8, width), lambda b, j: (b, jnp.maximum(j * per - 1, 0), col))
    nxt = BS((None, 8, width), lambda b, j: (b, jnp.minimum((j + 1) * per, nb8 - 1), col))
    return prev, nxt


def _shift_dn(z, prev_row, at_start):
    row = lax.broadcasted_iota(jnp.int32, z.shape, 0)
    first = jnp.where(at_start, 0.0, 1.0) * prev_row
    return jnp.where(row == 0, first, pltpu.roll(z, 1, 0))


def _shift_up(z, next_row, at_end):
    row = lax.broadcasted_iota(jnp.int32, z.shape, 0)
    last = jnp.where(at_end, 0.0, 1.0) * next_row
    return jnp.where(row == z.shape[0] - 1, last, pltpu.roll(z, z.shape[0] - 1, 0))


def _conv_fwd(p, cw8, nct):
    B, S, _ = p.shape
    nt = S // TB

    def body(p_ref, pp_ref, pn_ref, w_ref, o_ref):
        j = pl.program_id(1)
        at_start = jnp.logical_or(j == 0, j == nct)
        at_end = jnp.logical_or(j == nct - 1, j == nt - 1)
        z = p_ref[:, 256:512] * p_ref[:, 512:768]
        zprev = pp_ref[7:8, 256:512] * pp_ref[7:8, 512:768]
        znext = pn_ref[0:1, 256:512] * pn_ref[0:1, 512:768]
        c3 = (_shift_dn(z, zprev, at_start) * w_ref[0:1, :] + z * w_ref[1:2, :]
              + _shift_up(z, znext, at_end) * w_ref[2:3, :])
        o_ref[...] = p_ref[:, 0:256] * c3

    prev, nxt = _halo_specs(3 * CW, 1, S)
    return pl.pallas_call(
        body, name="conv_fwd", grid=(B, nt),
        in_specs=[_tok(3 * CW, 1), prev, nxt, _full((8, CW))],
        out_specs=_tok(CW),
        out_shape=SDS((B, S, CW), F32),
        compiler_params=_cp(("arbitrary", "arbitrary")),
    )(p, p, p, cw8)


def _conv_bwd(p, dcv, cw8, nct):
    B, S, _ = p.shape
    nt = S // TB

    def body(p_ref, pp_ref, pn_ref, d_ref, dp_ref, dn_ref, w_ref, o_ref, dw_ref):
        j = pl.program_id(1)
        at_start = jnp.logical_or(j == 0, j == nct)
        at_end = jnp.logical_or(j == nct - 1, j == nt - 1)
        cb, cc, cx = p_ref[:, 0:256], p_ref[:, 256:512], p_ref[:, 512:768]
        z = cc * cx
        zm = _shift_dn(z, pp_ref[7:8, 256:512] * pp_ref[7:8, 512:768], at_start)
        zp = _shift_up(z, pn_ref[0:1, 256:512] * pn_ref[0:1, 512:768], at_end)
        d = d_ref[...]
        e = d * cb
        em = _shift_dn(e, dp_ref[7:8, :] * pp_ref[7:8, 0:256], at_start)
        ep = _shift_up(e, dn_ref[0:1, :] * pn_ref[0:1, 0:256], at_end)
        dz = e * w_ref[1:2, :] + ep * w_ref[0:1, :] + em * w_ref[2:3, :]
        o_ref[:, 0:256] = d * (zm * w_ref[0:1, :] + z * w_ref[1:2, :] + zp * w_ref[2:3, :])
        o_ref[:, 256:512] = dz * cx
        o_ref[:, 512:768] = dz * cc
        dw_ref[...] = jnp.zeros_like(dw_ref)
        dw_ref[0:1, :] = _colsum(e * zm)
        dw_ref[1:2, :] = _colsum(e * z)
        dw_ref[2:3, :] = _colsum(e * zp)

    prev, nxt = _halo_specs(3 * CW, 1, S)
    dprev, dnxt = _halo_specs(CW, 0, S)
    return pl.pallas_call(
        body, name="conv_bwd", grid=(B, nt),
        in_specs=[_tok(3 * CW, 1), prev, nxt, _tok(CW), dprev, dnxt, _full((8, CW))],
        out_specs=[_tok(3 * CW), BS((None, None, 8, CW), lambda b, j: (b, j, 0, 0))],
        out_shape=[SDS((B, S, 3 * CW), F32), SDS((B, nt, 8, CW), F32)],
        compiler_params=_cp(("arbitrary", "arbitrary")),
    )(p, p, p, dcv, dcv, dcv, cw8)


HW = SGRP * SST


def _rev_chunk(k, nk, ncc):
    return jnp.where(k < ncc, ncc - 1 - k, nk - 1 - k + ncc)


def _scan_perm(B):
    n = B * TC
    pm = np.zeros((2 * n, 2 * n), np.float32)
    for t in range(TC):
        for e in range(B):
            pm[t * 2 * B + e, e * TC + t] = 1.0
            pm[t * 2 * B + B + e, n + e * TC + (TC - 1 - t)] = 1.0
    return pm


def _scan_drive(uf_ref, ub_ref, pm_ref, b4_ref, fwd, dbuf, n):
    u_tok = jnp.concatenate([uf_ref[...].reshape(n, SW), ub_ref[...].reshape(n, SW)], axis=0).astype(MXU)
    u_tm = _dot(pm_ref[...], u_tok).astype(MXU)
    d4 = _dot(u_tm, b4_ref[...])
    dbuf[...] = jnp.where(fwd, d4[:, :2 * HW], d4[:, 2 * HW:])
    return u_tm


def _scan_fwd(p, pm, pmt, b4, a, c2, Lc):
    B, S, _ = p.shape
    nch, n = 2 * B, B * TC
    nk, ncc = S // TC, Lc // TC
    ucol = (INW - SW) // SW

    def body(uf_ref, ub_ref, pm_ref, pmt_ref, b4_ref, a_ref, c2_ref, yf_ref, yb_ref, hb_ref, hcar, dbuf, hbuf):
        @pl.when(pl.program_id(0) == 0)
        def _():
            hcar[...] = jnp.zeros_like(hcar)

        fwd = (lax.broadcasted_iota(jnp.int32, (2 * n, 1), 0) % nch) < B
        _scan_drive(uf_ref, ub_ref, pm_ref, b4_ref, fwd, dbuf, n)
        hb_ref[...] = hcar[...]
        ar, ai = a_ref[:, :HW], a_ref[:, HW:]

        def step(t, h):
            hr, hi = h
            r0 = pl.multiple_of(t * nch, nch)
            nr = ar * hr - ai * hi + dbuf[pl.ds(r0, nch), :HW]
            ni = ar * hi + ai * hr + dbuf[pl.ds(r0, nch), HW:]
            hbuf[pl.ds(r0, nch), :HW] = nr
            hbuf[pl.ds(r0, nch), HW:] = ni
            return nr, ni

        hr, hi = lax.fori_loop(0, TC, step, (hcar[:, :HW], hcar[:, HW:]))
        hcar[:, :HW] = hr
        hcar[:, HW:] = hi
        h_tok = _dot(pmt_ref[...], hbuf[...].astype(MXU)).astype(MXU)
        yf_ref[...] = _dot(h_tok[:n], c2_ref[:, :SW]).reshape(B, TC, SW)
        yb_ref[...] = _dot(h_tok[n:], c2_ref[:, SW:]).reshape(B, TC, SW)

    return pl.pallas_call(
        body, name="scan_fwd", grid=(nk,),
        in_specs=[BS((B, TC, SW), lambda k: (0, k, ucol)),
                  BS((B, TC, SW), lambda k: (0, _rev_chunk(k, nk, ncc), ucol)),
                  _full((2 * n, 2 * n)), _full((2 * n, 2 * n)),
                  _full((SW, 4 * HW)), _full((nch, 2 * HW)), _full((2 * HW, 2 * SW))],
        out_specs=[BS((B, TC, SW), lambda k: (0, k, 0)),
                   BS((B, TC, SW), lambda k: (0, _rev_chunk(k, nk, ncc), 0)),
                   BS((None, nch, 2 * HW), lambda k: (k, 0, 0))],
        out_shape=[SDS((B, S, SW), F32), SDS((B, S, SW), F32), SDS((nk, nch, 2 * HW), F32)],
        scratch_shapes=[pltpu.VMEM((nch, 2 * HW), F32), pltpu.VMEM((2 * n, 2 * HW), F32),
                        pltpu.VMEM((2 * n, 2 * HW), F32)],
        compiler_params=_cp(("arbitrary",), 48),
    )(p, p, pm, pmt, b4, a, c2)


def _scan_bwd(p, dy, hb, pm, pmt, b4, a, c2, Lc):
    B, S, _ = p.shape
    nch, n = 2 * B, B * TC
    nk, ncc = S // TC, Lc // TC
    ucol = (INW - SW) // SW
    rows = 2 * n

    def body(uf_ref, ub_ref, dyf_ref, dyb_ref, hb_ref, pm_ref, pmt_ref, b4_ref, a_ref, c2_ref,
             duf_ref, dub_ref, db4_ref, da_ref, dc2_ref, gcar, dbuf, hbuf, gbuf):
        @pl.when(pl.program_id(0) == 0)
        def _():
            gcar[...] = jnp.zeros_like(gcar)
            db4_ref[...] = jnp.zeros_like(db4_ref)
            da_ref[...] = jnp.zeros_like(da_ref)
            dc2_ref[...] = jnp.zeros_like(dc2_ref)

        fwd = (lax.broadcasted_iota(jnp.int32, (rows, 1), 0) % nch) < B
        u_tm = _scan_drive(uf_ref, ub_ref, pm_ref, b4_ref, fwd, dbuf, n)
        ar, ai = a_ref[:, :HW], a_ref[:, HW:]
        hbuf[0:nch, :] = hb_ref[...]

        def fstep(t, h):
            hr, hi = h
            r0 = pl.multiple_of(t * nch, nch)
            nr = ar * hr - ai * hi + dbuf[pl.ds(r0, nch), :HW]
            ni = ar * hi + ai * hr + dbuf[pl.ds(r0, nch), HW:]
            hbuf[pl.ds(r0 + nch, nch), :HW] = nr
            hbuf[pl.ds(r0 + nch, nch), HW:] = ni
            return nr, ni

        lax.fori_loop(0, TC, fstep, (hb_ref[:, :HW], hb_ref[:, HW:]))

        dy_tok = jnp.concatenate([dyf_ref[...].reshape(n, SW), dyb_ref[...].reshape(n, SW)], axis=0).astype(MXU)
        dy_tm = _dot(pm_ref[...], dy_tok)
        dy2 = jnp.concatenate([jnp.where(fwd, dy_tm, 0.0), jnp.where(fwd, 0.0, dy_tm)], axis=1).astype(MXU)
        gbuf[...] = _dot_nt(dy2, c2_ref[...])
        dc2_ref[...] += _dot_tn(hbuf[nch:, :].astype(MXU), dy2)

        def bstep(i, cary):
            cr, ci = cary
            r0 = pl.multiple_of((TC - 1 - i) * nch, nch)
            gr = gbuf[pl.ds(r0, nch), :HW] + cr
            gi = gbuf[pl.ds(r0, nch), HW:] + ci
            gbuf[pl.ds(r0, nch), :HW] = gr
            gbuf[pl.ds(r0, nch), HW:] = gi
            return ar * gr + ai * gi, ar * gi - ai * gr

        cr, ci = lax.fori_loop(0, TC, bstep, (gcar[:, :HW], gcar[:, HW:]))
        gcar[:, :HW] = cr
        gcar[:, HW:] = ci

        gr = gbuf[:, :HW].reshape(TC, nch, HW)
        gi = gbuf[:, HW:].reshape(TC, nch, HW)
        hpr = hbuf[0:rows, :HW].reshape(TC, nch, HW)
        hpi = hbuf[0:rows, HW:].reshape(TC, nch, HW)
        da_ref[:, :HW] += jnp.sum(gr * hpr + gi * hpi, axis=0)
        da_ref[:, HW:] += jnp.sum(gi * hpr - gr * hpi, axis=0)

        g = gbuf[...]
        gm = g.astype(MXU)
        dd4 = jnp.concatenate([jnp.where(fwd, g, 0.0), jnp.where(fwd, 0.0, g)], axis=1).astype(MXU)
        db4_ref[...] += _dot_tn(u_tm, dd4)
        g_tok = _dot(pmt_ref[...], gm).astype(MXU)
        duf_ref[...] = _dot_nt(g_tok[:n], b4_ref[:, :2 * HW]).reshape(B, TC, SW)
        dub_ref[...] = _dot_nt(g_tok[n:], b4_ref[:, 2 * HW:]).reshape(B, TC, SW)

    fwd_blk = lambda col: BS((B, TC, SW), lambda k: (0, nk - 1 - k, col))
    bwd_blk = lambda col: BS((B, TC, SW), lambda k: (0, _rev_chunk(nk - 1 - k, nk, ncc), col))
    return pl.pallas_call(
        body, name="scan_bwd", grid=(nk,),
        in_specs=[fwd_blk(ucol), bwd_blk(ucol), fwd_blk(0), bwd_blk(0),
                  BS((None, nch, 2 * HW), lambda k: (nk - 1 - k, 0, 0)),
                  _full((rows, rows)), _full((rows, rows)),
                  _full((SW, 4 * HW)), _full((nch, 2 * HW)), _full((2 * HW, 2 * SW))],
        out_specs=[fwd_blk(0), bwd_blk(0), _full((SW, 4 * HW)), _full((nch, 2 * HW)), _full((2 * HW, 2 * SW))],
        out_shape=[SDS((B, S, SW), F32), SDS((B, S, SW), F32), SDS((SW, 4 * HW), F32), SDS((nch, 2 * HW), F32),
                   SDS((2 * HW, 2 * SW), F32)],
        scratch_shapes=[pltpu.VMEM((nch, 2 * HW), F32), pltpu.VMEM((rows, 2 * HW), F32),
                        pltpu.VMEM((rows + nch, 2 * HW), F32), pltpu.VMEM((rows, 2 * HW), F32)],
        compiler_params=_cp(("arbitrary",), 56),
    )(p, p, dy, dy, hb, pm, pmt, b4, a, c2)


def _post_mix(o, cv, yf, yb, p, h, modtok, ng, dsk, wg, bg, wo, nct):
    B, S, _ = h.shape

    def body(o_ref, cv_ref, yf_ref, yb_ref, u_ref, h_ref, mod_ref, g_ref, dsk_ref, wg_ref, bg_ref, wo_ref,
             h1_ref, mix_ref, m_ref, y_ref):
        y = yf_ref[...] + yb_ref[...] + dsk_ref[...] * u_ref[...]
        g = _gelu(y)
        s = g * _sigmoid(_dot(g.astype(MXU), wg_ref[...]) + bg_ref[...])
        mix_ref[:, 0:AW] = o_ref[...]
        mix_ref[:, AW:AW + CW] = cv_ref[...].astype(MXU)
        mix_ref[:, AW + CW:] = s.astype(MXU)
        m = _dot(mix_ref[...], wo_ref[...])
        n, _ = _rms(m)
        h1_ref[...] = h_ref[...] + mod_ref[2:3, :] * (n * g_ref[1:2, :])
        m_ref[...] = m
        y_ref[...] = y

    return pl.pallas_call(
        body, name="post_mix", grid=(B, S // TB),
        in_specs=[_tok(AW), _tok(CW), _tok(SW), _tok(SW), _tok(SW, (INW - SW) // SW), _tok(D), _mod_spec(nct),
                  _full((8, D)), _full((1, SW)), _full((SW, SW)), _full((1, SW)), _full((D, D))],
        out_specs=[_tok(D), _tok(D), _tok(D), _tok(SW)],
        out_shape=[SDS((B, S, D), F32), SDS((B, S, D), MXU), SDS((B, S, D), F32), SDS((B, S, SW), F32)],
        compiler_params=_cp(("arbitrary", "arbitrary"), 40),
    )(o, cv, yf, yb, p, h, modtok, ng, dsk, wg, bg, wo)


def _post_bwd(dh1, m, y, p, modtok, ng, dsk, wg, bg, wo, nct):
    B, S, _ = m.shape

    def body(dh_ref, m_ref, y_ref, u_ref, mod_ref, g_ref, dsk_ref, wg_ref, bg_ref, wo_ref,
             dm_ref, da_ref, dc_ref, dy_ref, du_ref, vec_ref, vec2_ref, dwg_ref):
        @pl.when(jnp.logical_and(pl.program_id(0) == 0, pl.program_id(1) == 0))
        def _():
            dwg_ref[...] = jnp.zeros_like(dwg_ref)

        n, r = _rms(m_ref[...])
        dh1 = dh_ref[...]
        gpm = g_ref[1:2, :]
        dr = dh1 * mod_ref[2:3, :]
        vec_ref[...] = jnp.zeros_like(vec_ref)
        vec_ref[0:1, :] = _colsum(dh1 * (n * gpm))
        vec_ref[1:2, :] = _colsum(dr * n)
        dm = _rms_bwd(dr * gpm, n, r).astype(MXU)
        dm_ref[...] = dm
        dmix = _dot_nt(dm, wo_ref[...])
        da_ref[...] = dmix[:, 0:AW]
        dc_ref[...] = dmix[:, AW:AW + CW]
        ds = dmix[:, AW + CW:]
        yv = y_ref[...]
        g = _gelu(yv)
        gb = g.astype(MXU)
        sg = _sigmoid(_dot(gb, wg_ref[...]) + bg_ref[...])
        dv = ds * g * sg * (1.0 - sg)
        dvb = dv.astype(MXU)
        dg = ds * sg + _dot_nt(dvb, wg_ref[...])
        dwg_ref[...] += _dot_tn(gb, dvb)
        dy = dg * _gelu_grad(yv)
        dy_ref[...] = dy
        du_ref[...] = dy * dsk_ref[...]
        vec2_ref[...] = jnp.zeros_like(vec2_ref)
        vec2_ref[0:1, :] = _colsum(dy * u_ref[...])
        vec2_ref[1:2, :] = _colsum(dv)

    nt = S // TB
    return pl.pallas_call(
        body, name="post_bwd", grid=(B, nt),
        in_specs=[_tok(D), _tok(D), _tok(SW), _tok(SW, (INW - SW) // SW), _mod_spec(nct), _full((8, D)),
                  _full((1, SW)), _full((SW, SW)), _full((1, SW)), _full((D, D))],
        out_specs=[_tok(D), _tok(AW), _tok(CW), _tok(SW), _tok(SW),
                   BS((None, None, 8, D), lambda b, j: (b, j, 0, 0)),
                   BS((None, None, 8, SW), lambda b, j: (b, j, 0, 0)), _full((SW, SW))],
        out_shape=[SDS((B, S, D), MXU), SDS((B, S, AW), F32), SDS((B, S, CW), F32), SDS((B, S, SW), F32),
                   SDS((B, S, SW), F32), SDS((B, nt, 8, D), F32), SDS((B, nt, 8, SW), F32), SDS((SW, SW), F32)],
        compiler_params=_cp(("arbitrary", "arbitrary"), 40),
    )(dh1, m, y, p, modtok, ng, dsk, wg, bg, wo)


def _mlp_fwd(h1, modtok, ng, w1, w2, nct):
    B, S, _ = h1.shape

    def body(h_ref, mod_ref, g_ref, w1_ref, w2_ref, h2_ref, a2_ref, z_ref, f_ref):
        n, _ = _rms(h_ref[...])
        a2 = ((n * g_ref[2:3, :]) * (1.0 + mod_ref[4:5, :]) + mod_ref[3:4, :]).astype(MXU)
        a2_ref[...] = a2
        ff = jnp.zeros((TB, D), F32)
        for c in range(DFF // TF):
            cs = slice(c * TF, (c + 1) * TF)
            z = _dot(a2, w1_ref[:, cs])
            z_ref[:, cs] = z.astype(MXU)
            r = jnp.maximum(z, 0.0)
            ff = ff + _dot((r * r).astype(MXU), w2_ref[cs, :])
        f_ref[...] = ff
        n, _ = _rms(ff)
        h2_ref[...] = h_ref[...] + mod_ref[5:6, :] * (n * g_ref[3:4, :])

    return pl.pallas_call(
        body, name="mlp_fwd", grid=(B, S // TB),
        in_specs=[_tok(D), _mod_spec(nct), _full((8, D)), _resident((D, DFF)), _resident((DFF, D))],
        out_specs=[_tok(D), _tok(D), _tok(DFF), _tok(D)],
        out_shape=[SDS((B, S, D), F32), SDS((B, S, D), MXU), SDS((B, S, DFF), MXU), SDS((B, S, D), F32)],
        compiler_params=_cp(("arbitrary", "arbitrary"), 48),
    )(h1, modtok, ng, w1, w2)


def _mlp_bwd(dh2, h1, f, z, modtok, ng, w1, w2, nct):
    B, S, _ = h1.shape

    def body(dh_ref, h_ref, f_ref, z_ref, mod_ref, g_ref, w1_ref, w2_ref, dh1_ref, df_ref, dz_ref, vec_ref):
        n, r = _rms(f_ref[...])
        dh2 = dh_ref[...]
        gp = g_ref[3:4, :]
        dr = dh2 * mod_ref[5:6, :]
        vec_ref[...] = jnp.zeros_like(vec_ref)
        vec_ref[3:4, :] = _colsum(dh2 * (n * gp))
        vec_ref[4:5, :] = _colsum(dr * n)
        df = _rms_bwd(dr * gp, n, r).astype(MXU)
        df_ref[...] = df
        da = jnp.zeros((TB, D), F32)
        for c in range(DFF // TF):
            cs = slice(c * TF, (c + 1) * TF)
            dr2 = _dot_nt(df, w2_ref[cs, :])
            dz = (dr2 * 2.0 * jnp.maximum(z_ref[:, cs].astype(F32), 0.0)).astype(MXU)
            dz_ref[:, cs] = dz
            da = da + _dot_nt(dz, w1_ref[:, cs])
        n, r = _rms(h_ref[...])
        g = g_ref[2:3, :]
        sc1 = 1.0 + mod_ref[4:5, :]
        vec_ref[0:1, :] = _colsum(da)
        vec_ref[1:2, :] = _colsum(da * (n * g))
        vec_ref[2:3, :] = _colsum(da * sc1 * n)
        dh1_ref[...] = dh2 + _rms_bwd(da * sc1 * g, n, r)

    return pl.pallas_call(
        body, name="mlp_bwd", grid=(B, S // TB),
        in_specs=[_tok(D), _tok(D), _tok(D), _tok(DFF), _mod_spec(nct), _full((8, D)),
                  _resident((D, DFF)), _resident((DFF, D))],
        out_specs=[_tok(D), _tok(D), _tok(DFF), BS((None, None, 8, D), lambda b, j: (b, j, 0, 0))],
        out_shape=[SDS((B, S, D), F32), SDS((B, S, D), MXU), SDS((B, S, DFF), MXU), SDS((B, S // TB, 8, D), F32)],
        compiler_params=_cp(("arbitrary", "arbitrary"), 56),
    )(dh2, h1, f, z, modtok, ng, w1, w2)


def _loss(h, target, Lc):
    B, S, _ = h.shape
    nt, nct = S // TB, Lc // TB

    def body(h_ref, t_ref, dh_ref, l_ref):
        lat = pl.program_id(1) >= nct
        err = jnp.where(lat, h_ref[...] - t_ref[...], 0.0)
        dh_ref[...] = err * (1.0 / D)
        l_ref[...] = jnp.broadcast_to(jnp.sum(err * err, keepdims=True), (8, 128))

    return pl.pallas_call(
        body, name="loss", grid=(B, nt),
        in_specs=[_tok(D), BS((None, TB, D), lambda b, j: (b, jnp.maximum(j - nct, 0), 0))],
        out_specs=[_tok(D), BS((None, None, 8, 128), lambda b, j: (b, j, 0, 0))],
        out_shape=[SDS((B, S, D), F32), SDS((B, nt, 8, 128), F32)],
        compiler_params=_cp(("arbitrary", "arbitrary")),
    )(h, target)


def _mm_tn(a, b, name, relu2=False):
    T, M = a.shape
    N = b.shape[1]
    def blk(n):
        return max(b for b in range(128, 1025, 128) if n % b == 0)

    bm, bn, tk = blk(M), blk(N), min(T, 512)
    assert T % tk == 0
    nkk = T // tk

    def body(a_ref, b_ref, o_ref):
        @pl.when(pl.program_id(2) == 0)
        def _():
            o_ref[...] = jnp.zeros_like(o_ref)

        av = a_ref[...]
        if relu2:
            r = jnp.maximum(av.astype(F32), 0.0)
            av = (r * r).astype(MXU)
        o_ref[...] += _dot_tn(av, b_ref[...])

    return pl.pallas_call(
        body, name=name, grid=(M // bm, N // bn, nkk),
        in_specs=[BS((tk, bm), lambda i, j, k: (k, i)), BS((tk, bn), lambda i, j, k: (k, j))],
        out_specs=BS((bm, bn), lambda i, j, k: (i, j)),
        out_shape=SDS((M, N), F32),
        compiler_params=_cp(("arbitrary", "arbitrary", "arbitrary"), 40),
    )(a, b)


_PAD_ROWS = 64


def _block_rows(R):
    for br in (256, _PAD_ROWS):
        if R % br == 0:
            return br
    raise ValueError(f"row count {R} is not a multiple of {_PAD_ROWS}")


def _adamw(w, ga, gb, m, v, name, row0=0):
    R, C = w.shape
    br = _block_rows(R)
    off = row0 // br

    def body(w_ref, ga_ref, gb_ref, m_ref, v_ref, g_out, d_out, m_out, v_out):
        g = ga_ref[...] + gb_ref[...]
        m2 = B1 * m_ref[...] + (1.0 - B1) * g
        v2 = B2 * v_ref[...] + (1.0 - B2) * (g * g)
        m_hat = m2 / (1.0 - B1 ** STEP)
        v_hat = v2 / (1.0 - B2 ** STEP)
        g_out[...] = g
        d_out[...] = -LR * (m_hat / (jnp.sqrt(v_hat) + AEPS) + WD * w_ref[...])
        m_out[...] = m2
        v_out[...] = v2

    own = BS((br, C), lambda i: (i, 0))
    shifted = BS((br, C), lambda i: (i + off, 0))
    return pl.pallas_call(
        body, name=name, grid=(R // br,),
        in_specs=[own, shifted, shifted, own, own],
        out_specs=[own] * 4, out_shape=[SDS((R, C), F32)] * 4,
        compiler_params=_cp(("arbitrary",)),
    )(w, ga, gb, m, v)


def _my_xyc():
    return lax.axis_index("x"), lax.axis_index("y"), lax.axis_index("c")


def _chip_peers(x, y):
    return [(1 - x, y), (x, 1 - y), (1 - x, 1 - y)]


def _gather_chips(shard, name):
    R, C = shard.shape

    def body(s_ref, o_ref, ssem, rsem, lsem):
        x, y, c = _my_xyc()
        mine = pltpu.make_async_copy(s_ref, o_ref.at[2 * x + y], lsem)
        mine.start()
        cps = []
        for j, (px, py) in enumerate(_chip_peers(x, y)):
            cps.append(pltpu.make_async_remote_copy(
                src_ref=s_ref, dst_ref=o_ref.at[2 * x + y], send_sem=ssem.at[j], recv_sem=rsem.at[j],
                device_id=(px, py, c), device_id_type=MESH))
            cps[-1].start()
        for j, (px, py) in enumerate(_chip_peers(x, y)):
            pltpu.make_async_remote_copy(
                src_ref=s_ref, dst_ref=o_ref.at[2 * px + py], send_sem=ssem.at[j], recv_sem=rsem.at[j],
                device_id=(px, py, c), device_id_type=MESH).wait_recv()
        for cp in cps:
            cp.wait_send()
        mine.wait()

    return pl.pallas_call(
        body, name=name, in_specs=[ANY], out_specs=ANY,
        out_shape=SDS((NCHIP, R, C), shard.dtype),
        scratch_shapes=[pltpu.SemaphoreType.DMA((3,)), pltpu.SemaphoreType.DMA((3,)), pltpu.SemaphoreType.DMA],
    )(shard)


def _scatter_chips(send, name):
    _, R, C = send.shape

    def body(s_ref, o_ref, ssem, rsem):
        x, y, c = _my_xyc()
        cps = []
        for j, (px, py) in enumerate(_chip_peers(x, y)):
            cps.append(pltpu.make_async_remote_copy(
                src_ref=s_ref.at[2 * px + py], dst_ref=o_ref.at[j], send_sem=ssem.at[j], recv_sem=rsem.at[j],
                device_id=(px, py, c), device_id_type=MESH))
            cps[-1].start()
        for cp in cps:
            cp.wait()

    return pl.pallas_call(
        body, name=name, in_specs=[ANY], out_specs=ANY,
        out_shape=SDS((3, R, C), send.dtype),
        scratch_shapes=[pltpu.SemaphoreType.DMA((3,)), pltpu.SemaphoreType.DMA((3,))],
    )(send)


def _swap_cores(v, name):
    def body(s_ref, o_ref, ssem, rsem):
        x, y, c = _my_xyc()
        cp = pltpu.make_async_remote_copy(src_ref=s_ref, dst_ref=o_ref, send_sem=ssem, recv_sem=rsem,
                                          device_id=(x, y, 1 - c), device_id_type=MESH)
        cp.start()
        cp.wait()

    return pl.pallas_call(
        body, name=name, in_specs=[ANY], out_specs=ANY, out_shape=SDS(v.shape, v.dtype),
        scratch_shapes=[pltpu.SemaphoreType.DMA, pltpu.SemaphoreType.DMA],
    )(v)


def _gather_all(v, name):
    R, C = v.shape

    def body(s_ref, o_ref, ssem, rsem, lsem):
        x, y, c = _my_xyc()
        me = 4 * x + 2 * y + c
        mine = pltpu.make_async_copy(s_ref, o_ref.at[me], lsem)
        mine.start()

        def peer(j):
            fx, fy, fc = (j >> 2) & 1, (j >> 1) & 1, j & 1
            return (x ^ fx, y ^ fy, c ^ fc)

        cps = []
        for j in range(1, 8):
            cps.append(pltpu.make_async_remote_copy(
                src_ref=s_ref, dst_ref=o_ref.at[me], send_sem=ssem.at[j - 1], recv_sem=rsem.at[j - 1],
                device_id=peer(j), device_id_type=MESH))
            cps[-1].start()
        for j in range(1, 8):
            px, py, pc = peer(j)
            pltpu.make_async_remote_copy(
                src_ref=s_ref, dst_ref=o_ref.at[4 * px + 2 * py + pc], send_sem=ssem.at[j - 1],
                recv_sem=rsem.at[j - 1], device_id=peer(j), device_id_type=MESH).wait_recv()
        for cp in cps:
            cp.wait_send()
        mine.wait()

    return pl.pallas_call(
        body, name=name, in_specs=[ANY], out_specs=ANY, out_shape=SDS((8, R, C), v.dtype),
        scratch_shapes=[pltpu.SemaphoreType.DMA((7,)), pltpu.SemaphoreType.DMA((7,)), pltpu.SemaphoreType.DMA],
    )(v)


def _sum_slots(own, slots, name):
    n, R, C = slots.shape
    br = _block_rows(R)

    def body(*refs):
        o_ref = refs[-1]
        if own is None:
            acc = refs[0][0].astype(F32)
            first = 1
            s_ref = refs[0]
        else:
            acc = refs[0][...]
            first = 0
            s_ref = refs[1]
        for k in range(first, n):
            acc = acc + s_ref[k].astype(F32)
        o_ref[...] = acc

    row = BS((br, C), lambda i: (i, 0))
    slab = BS((n, br, C), lambda i: (0, i, 0))
    ins, args = ([slab], [slots]) if own is None else ([row, slab], [own, slots])
    return pl.pallas_call(
        body, name=name, grid=(R // br,), in_specs=ins, out_specs=row, out_shape=SDS((R, C), F32),
        compiler_params=_cp(("arbitrary",)),
    )(*args)


def _rope_tables(L, Lc):
    n = jnp.arange(L)
    row = (n // GRID_W).astype(F32)
    col = (n % GRID_W).astype(F32)
    freqs = ROPE_BASE ** (-jnp.arange(16, dtype=F32) / 16)
    lane = jnp.arange(128)
    dd = lane % HD
    fr = freqs[dd % 16]
    ang = jnp.where(dd < 32, row[:, None], col[:, None]) * fr[None, :]
    sign = jnp.where((dd % 32) < 16, -1.0, 1.0)
    cos = jnp.concatenate([jnp.ones((Lc, 128), F32), jnp.cos(ang)], axis=0)
    sin = jnp.concatenate([jnp.zeros((Lc, 128), F32), jnp.sin(ang) * sign[None, :]], axis=0)
    return cos, sin


def _ssm_prep(lam_re, lam_im, log_dt, b_re, b_im, c_re, c_im, B):
    dt = jnp.exp(log_dt)[..., None]
    mag = jnp.exp(lam_re * dt)
    ar = mag * jnp.cos(lam_im * dt)
    ai = mag * jnp.sin(lam_im * dt)
    den = lam_re * lam_re + lam_im * lam_im
    kr = ((ar - 1.0) * lam_re + ai * lam_im) / den
    ki = (ai * lam_re - (ar - 1.0) * lam_im) / den
    bbr = kr[..., None] * b_re - ki[..., None] * b_im
    bbi = kr[..., None] * b_im + ki[..., None] * b_re
    eye = jnp.eye(SGRP, dtype=F32)

    def bblk(t):
        return jnp.einsum("gpi,gh->gihp", t, eye).reshape(SGRP * SCH, SGRP * SST)

    def cblk(t):
        return jnp.einsum("gip,gh->gphi", t, eye).reshape(SGRP * SST, SGRP * SCH)

    b4 = jnp.concatenate([bblk(bbr[0]), bblk(bbi[0]), bblk(bbr[1]), bblk(bbi[1])], axis=1)
    c2 = jnp.concatenate([jnp.concatenate([cblk(c_re[0]), -cblk(c_im[0])], axis=0),
                          jnp.concatenate([cblk(c_re[1]), -cblk(c_im[1])], axis=0)], axis=1)
    a2 = jnp.concatenate([ar.reshape(2, -1), ai.reshape(2, -1)], axis=1)
    a = jnp.repeat(a2, B, axis=0)
    return a, b4, c2


def _pad_rows(v, rows=8):
    return jnp.concatenate([v, jnp.zeros((rows - v.shape[0],) + v.shape[1:], v.dtype)], axis=0)


_BIG = ("w_ada", "w_in", "w_out", "w_mlp_in", "w_mlp_out")
_BIG_AXIS = {"w_ada": 2, "w_in": 2, "w_out": 1, "w_mlp_in": 2, "w_mlp_out": 1}


def _split_shards(g, axis):
    L, r, c = g.shape
    if axis == 2:
        return jnp.transpose(g.reshape(L, r, NCHIP, c // NCHIP), (2, 0, 1, 3))
    return jnp.transpose(g.reshape(L, NCHIP, r // NCHIP, c), (1, 0, 2, 3))


def _join_shards(s, axis):
    _, L, r, c = s.shape
    if axis == 2:
        return jnp.transpose(s, (1, 2, 0, 3)).reshape(L, r, NCHIP * c)
    return jnp.transpose(s, (1, 0, 2, 3)).reshape(L, NCHIP * r, c)


def _local_step(x, c, ctx, loss_target, P):
    B, L, _ = x.shape
    Lc = ctx.shape[1]
    S = Lc + L
    nct = Lc // TB
    nt = S // TB
    nch = 2 * B
    cos, sin = _rope_tables(L, Lc)
    perm = _scan_perm(B)
    pm, pmt = jnp.asarray(perm, MXU), jnp.asarray(perm.T, MXU)

    c_act = jax.nn.silu(c)
    cc_act, silu_vjp = jax.vjp(jax.nn.silu, P["c_ctx"])
    cact8 = _pad_rows(jnp.concatenate([c_act, cc_act[None, :]], axis=0))
    h = jnp.concatenate([ctx, x], axis=1)

    saved = []
    for l in range(NLAYER):
        mod8 = _ada_fwd(cact8, P["w_ada"][l], P["b_ada"][l][None, :])
        mod6 = mod8.reshape(8, NMOD, D)
        modtok = jnp.stack([jnp.broadcast_to(mod6[B], (B, NMOD, D)), mod6[:B]], axis=1)
        modtok = jnp.concatenate([modtok, jnp.zeros((B, 2, 2, D), F32)], axis=2)
        ng = _pad_rows(P["norm_g"][l])
        cw8 = _pad_rows(P["conv_w"][l])
        (a_s, b4, c2), prep_vjp = jax.vjp(
            lambda *t: _ssm_prep(*t, B), P["ssm_lam_re"][l], P["ssm_lam_im"][l], P["ssm_log_dt"][l],
            P["ssm_b_re"][l], P["ssm_b_im"][l], P["ssm_c_re"][l], P["ssm_c_im"][l])
        b4m, c2m = b4.astype(MXU), c2.astype(MXU)
        wg = P["w_glu"][l].astype(MXU)
        dsk = P["ssm_d"][l][None, :]
        bg = P["b_glu"][l][None, :]

        p, a = _pre_mix(h, modtok, ng, P["w_in"][l], nct)
        qk = _rope(p, cos, sin, AW + KVW, MXU, "rope_fwd")
        o = _attn_fwd(P["attn_sink"][l], qk, p, Lc)
        cv = _conv_fwd(p, cw8, nct)
        yf, yb, hb = _scan_fwd(p, pm, pmt, b4m, a_s, c2m, Lc)
        h1, mix, m, y = _post_mix(o, cv, yf, yb, p, h, modtok, ng, dsk, wg, bg, P["w_out"][l], nct)
        h2, a2, z, f = _mlp_fwd(h1, modtok, ng, P["w_mlp_in"][l], P["w_mlp_out"][l], nct)
        saved.append(dict(h=h, modtok=modtok, ng=ng, cw8=cw8, a_s=a_s, b4m=b4m, c2m=c2m, wg=wg, dsk=dsk, bg=bg,
                          p=p, a=a, qk=qk, hb=hb, mix=mix, m=m, y=y, h1=h1, a2=a2, z=z, f=f,
                          prep_vjp=prep_vjp))
        h = h2

    dh, lpart = _loss(h, loss_target, Lc)
    loss_local = 0.5 / D * jnp.sum(lpart[:, :, 0, 0])

    G = {k: [None] * NLAYER for k in ("w_ada", "b_ada", "norm_g", "w_in", "conv_w", "attn_sink", "ssm_lam_re",
                                      "ssm_lam_im", "ssm_log_dt", "ssm_b_re", "ssm_b_im", "ssm_c_re", "ssm_c_im",
                                      "ssm_d", "w_glu", "b_glu", "w_out", "w_mlp_in", "w_mlp_out")}
    dcact = jnp.zeros((8, D), F32)
    T = B * S
    for l in reversed(range(NLAYER)):
        sv = saved[l]
        modtok, ng = sv["modtok"], sv["ng"]
        dh1, df, dz, vec_m = _mlp_bwd(dh, sv["h1"], sv["f"], sv["z"], modtok, ng, P["w_mlp_in"][l],
                                      P["w_mlp_out"][l], nct)
        G["w_mlp_out"][l] = _mm_tn(sv["z"].reshape(T, DFF), df.reshape(T, D), "dw_mlp_out", relu2=True)
        G["w_mlp_in"][l] = _mm_tn(sv["a2"].reshape(T, D), dz.reshape(T, DFF), "dw_mlp_in")
        dm, dattn, dcv, dy, du_skip, vec_p, vec_p2, dwg = _post_bwd(
            dh1, sv["m"], sv["y"], sv["p"], modtok, ng, sv["dsk"], sv["wg"], sv["bg"], P["w_out"][l], nct)
        G["w_out"][l] = _mm_tn(sv["mix"].reshape(T, D), dm.reshape(T, D), "dw_out")
        G["w_glu"][l] = dwg
        G["ssm_d"][l] = jnp.sum(vec_p2[:, :, 0, :], axis=(0, 1))
        G["b_glu"][l] = jnp.sum(vec_p2[:, :, 1, :], axis=(0, 1))
        duf, dub, db4, da_s, dc2 = _scan_bwd(sv["p"], dy, sv["hb"], pm, pmt, sv["b4m"], sv["a_s"], sv["c2m"], Lc)
        du = du_skip + duf + dub
        gs = sv["prep_vjp"]((da_s, db4, dc2))
        for k, name in enumerate(("ssm_lam_re", "ssm_lam_im", "ssm_log_dt", "ssm_b_re", "ssm_b_im", "ssm_c_re",
                                  "ssm_c_im")):
            G[name][l] = gs[k]
        dconv, dcw = _conv_bwd(sv["p"], dcv, sv["cw8"], nct)
        G["conv_w"][l] = jnp.sum(dcw[:, :, 0:3, :], axis=(0, 1))
        dq_r, dk_r, dv, dsink = _attn_bwd(P["attn_sink"][l], sv["qk"], sv["p"], dattn, Lc)
        G["attn_sink"][l] = jnp.sum(dsink[:, :, 0], axis=0)
        dq = _rope(dq_r, cos, -sin, AW, MXU, "rope_bwd_q")
        dk = _rope(dk_r, cos, -sin, KVW, MXU, "rope_bwd_k")
        dp = jnp.concatenate([dq, dk, dv.astype(MXU), dconv.astype(MXU), du.astype(MXU)], axis=-1)
        G["w_in"][l] = _mm_tn(sv["a"].reshape(T, D), dp.reshape(T, INW), "dw_in")
        dh, vec_i = _pre_bwd(dp, dh1, sv["h"], modtok, ng, P["w_in"][l], nct)
        G["norm_g"][l] = jnp.stack([jnp.sum(vec_i[:, :, 2], axis=(0, 1)), jnp.sum(vec_p[:, :, 1], axis=(0, 1)),
                                    jnp.sum(vec_m[:, :, 2], axis=(0, 1)), jnp.sum(vec_m[:, :, 4], axis=(0, 1))])
        dmod_blk = jnp.stack([vec_i[:, :, 0], vec_i[:, :, 1], vec_p[:, :, 0],
                              vec_m[:, :, 0], vec_m[:, :, 1], vec_m[:, :, 3]], axis=2)
        dmod_lat = jnp.sum(dmod_blk[:, nct:], axis=1).reshape(B, NMOD * D)
        dmod_ctx = jnp.sum(dmod_blk[:, :nct], axis=(0, 1)).reshape(1, NMOD * D)
        dmod8 = _pad_rows(jnp.concatenate([dmod_lat, dmod_ctx], axis=0))
        G["b_ada"][l] = jnp.sum(dmod8, axis=0)
        dmod8m = dmod8.astype(MXU)
        G["w_ada"][l] = _mm_tn(cact8.astype(MXU), dmod8m, "dw_ada")
        dcact = dcact + _ada_bwd_c(dmod8m, P["w_ada"][l])

    grads = {k: jnp.stack(v) for k, v in G.items()}
    grads["c_ctx"] = silu_vjp(dcact[B])[0]
    return loss_local, dh[:, Lc:, :], grads


_WEIGHTS = ["c_ctx", "w_ada", "b_ada", "norm_g", "w_in", "conv_w", "attn_sink", "ssm_lam_re", "ssm_lam_im",
            "ssm_log_dt", "ssm_b_re", "ssm_b_im", "ssm_c_re", "ssm_c_im", "ssm_d", "w_glu", "b_glu", "w_out",
            "w_mlp_in", "w_mlp_out"]
_SMALL = [n for n in _WEIGHTS if n not in _BIG]
_SMALL_SHARDED = {"norm_g": 2, "conv_w": 2, "w_glu": 1}
_LANES = 1024


def _flat_pad(parts):
    v = jnp.concatenate([p.reshape(-1) for p in parts])
    rows = -(-v.shape[0] // (_PAD_ROWS * _LANES)) * _PAD_ROWS
    return jnp.concatenate([v, jnp.zeros((rows * _LANES - v.shape[0],), v.dtype)]).reshape(rows, _LANES)


def _unflat(flat, shapes):
    v = flat.reshape(-1)
    out, off = [], 0
    for s in shapes:
        n = math.prod(s)
        out.append(v[off:off + n].reshape(s))
        off += n
    return out


def kernel(x, c, ctx, c_ctx, w_ada, b_ada, norm_g, w_in, conv_w, attn_sink, ssm_lam_re, ssm_lam_im, ssm_log_dt, ssm_b_re, ssm_b_im, ssm_c_re, ssm_c_im, ssm_d, w_glu, b_glu, w_out, w_mlp_in, w_mlp_out, loss_target, m_c_ctx, m_w_ada, m_b_ada, m_norm_g, m_w_in, m_conv_w, m_attn_sink, m_ssm_lam_re, m_ssm_lam_im, m_ssm_log_dt, m_ssm_b_re, m_ssm_b_im, m_ssm_c_re, m_ssm_c_im, m_ssm_d, m_w_glu, m_b_glu, m_w_out, m_w_mlp_in, m_w_mlp_out, v_c_ctx, v_w_ada, v_b_ada, v_norm_g, v_w_in, v_conv_w, v_attn_sink, v_ssm_lam_re, v_ssm_lam_im, v_ssm_log_dt, v_ssm_b_re, v_ssm_b_im, v_ssm_c_re, v_ssm_c_im, v_ssm_d, v_w_glu, v_b_glu, v_w_out, v_w_mlp_in, v_w_mlp_out):
    W = dict(c_ctx=c_ctx, w_ada=w_ada, b_ada=b_ada, norm_g=norm_g, w_in=w_in, conv_w=conv_w, attn_sink=attn_sink,
             ssm_lam_re=ssm_lam_re, ssm_lam_im=ssm_lam_im, ssm_log_dt=ssm_log_dt, ssm_b_re=ssm_b_re,
             ssm_b_im=ssm_b_im, ssm_c_re=ssm_c_re, ssm_c_im=ssm_c_im, ssm_d=ssm_d, w_glu=w_glu, b_glu=b_glu,
             w_out=w_out, w_mlp_in=w_mlp_in, w_mlp_out=w_mlp_out)
    M = dict(c_ctx=m_c_ctx, w_ada=m_w_ada, b_ada=m_b_ada, norm_g=m_norm_g, w_in=m_w_in, conv_w=m_conv_w,
             attn_sink=m_attn_sink, ssm_lam_re=m_ssm_lam_re, ssm_lam_im=m_ssm_lam_im, ssm_log_dt=m_ssm_log_dt,
             ssm_b_re=m_ssm_b_re, ssm_b_im=m_ssm_b_im, ssm_c_re=m_ssm_c_re, ssm_c_im=m_ssm_c_im, ssm_d=m_ssm_d,
             w_glu=m_w_glu, b_glu=m_b_glu, w_out=m_w_out, w_mlp_in=m_w_mlp_in, w_mlp_out=m_w_mlp_out)
    V = dict(c_ctx=v_c_ctx, w_ada=v_w_ada, b_ada=v_b_ada, norm_g=v_norm_g, w_in=v_w_in, conv_w=v_conv_w,
             attn_sink=v_attn_sink, ssm_lam_re=v_ssm_lam_re, ssm_lam_im=v_ssm_lam_im, ssm_log_dt=v_ssm_log_dt,
             ssm_b_re=v_ssm_b_re, ssm_b_im=v_ssm_b_im, ssm_c_re=v_ssm_c_re, ssm_c_im=v_ssm_c_im, ssm_d=v_ssm_d,
             w_glu=v_w_glu, b_glu=v_b_glu, w_out=v_w_out, w_mlp_in=v_w_mlp_in, w_mlp_out=v_w_mlp_out)
    q_me = 2 * lax.axis_index("x") + lax.axis_index("y")

    big_rows = [W[n].size // _LANES for n in _BIG]
    big_off = [sum(big_rows[:k]) for k in range(len(_BIG))]
    wsh = jnp.concatenate([W[n].reshape(-1, _LANES) for n in _BIG], axis=0).astype(MXU)
    wall = _gather_chips(wsh, "gather_big")
    P = dict(W)
    for n, r0, nr in zip(_BIG, big_off, big_rows):
        P[n] = _join_shards(wall[:, r0:r0 + nr].reshape((NCHIP,) + W[n].shape), _BIG_AXIS[n])
    ssh_names = list(_SMALL_SHARDED)
    ssh = _flat_pad([W[n] for n in ssh_names])
    sall = _gather_chips(ssh, "gather_small")
    parts = [_unflat(sall[q], [W[n].shape for n in ssh_names]) for q in range(NCHIP)]
    for k, n in enumerate(ssh_names):
        P[n] = _join_shards(jnp.stack([parts[q][k] for q in range(NCHIP)]), _SMALL_SHARDED[n])

    loss_local, grad_x, G = _local_step(x, c, ctx, loss_target, P)
    loss = lax.psum(loss_local, ("x", "y", "c"))

    split = [_split_shards(G[n], _BIG_AXIS[n]).reshape(NCHIP, -1, _LANES) for n in _BIG]
    send = jnp.concatenate(split, axis=1)
    own = lax.dynamic_index_in_dim(send, q_me, axis=0, keepdims=False)
    recv = _scatter_chips(send.astype(MXU), "scatter_big")
    part = _sum_slots(own, recv, "sum_chips")
    sib = _swap_cores(part, "swap_big")

    out = {}
    for n, r0, nr in zip(_BIG, big_off, big_rows):
        res = _adamw(W[n].reshape(nr, _LANES), part, sib, M[n].reshape(nr, _LANES), V[n].reshape(nr, _LANES),
                     "adamw_" + n, row0=r0)
        out[n] = [t.reshape(W[n].shape) for t in res]

    gsmall = _flat_pad([G[n] for n in _SMALL])
    gsum = _sum_slots(None, _gather_all(gsmall, "gather_grads"), "sum_devices")
    gfull = dict(zip(_SMALL, _unflat(gsum, [G[n].shape for n in _SMALL])))
    for n, ax in _SMALL_SHARDED.items():
        width = W[n].shape[ax]
        gfull[n] = lax.dynamic_slice_in_dim(gfull[n], q_me * width, width, axis=ax)
    shapes = [W[n].shape for n in _SMALL]
    gflat = _flat_pad([gfull[n] for n in _SMALL])
    res = _adamw(_flat_pad([W[n] for n in _SMALL]), gflat, jnp.zeros_like(gflat),
                 _flat_pad([M[n] for n in _SMALL]), _flat_pad([V[n] for n in _SMALL]), "adamw_small")
    for k, t in enumerate(res):
        for n, piece in zip(_SMALL, _unflat(t, shapes)):
            out.setdefault(n, [None] * 4)[k] = piece

    return (loss, grad_x, *[out[n][0] for n in _WEIGHTS], *[out[n][1] for n in _WEIGHTS],
            *[out[n][2] for n in _WEIGHTS], *[out[n][3] for n in _WEIGHTS])
```

```python
import functools
import math

import jax
import jax.numpy as jnp
import numpy as np
from jax import lax
from jax.experimental import pallas as pl
from jax.experimental.pallas import tpu as pltpu

F32 = jnp.float32
MXU = jnp.bfloat16

D = 1024
DFF = 4096
NMOD = 6
EPS = 1e-6
HD = 64
NQ = 8
NKV = 2
QPK = 4
AW = 512
KVW = 128
WIN = 128
QB = 128
CW = 256
SW = 256
SGRP = 16
SCH = 16
SST = 64
INW = 1792
GRID_W = 64
ROPE_BASE = 10000.0
NEG = -1e30
SCALE = HD ** -0.5
NLAYER = 4

TB = 256
TF = 1024
TC = 32
NCHIP = 4

LR, B1, B2, AEPS, WD, STEP = 0.001, 0.9, 0.999, 1e-08, 0.01, 10

MESH = pl.DeviceIdType.MESH
SDS = jax.ShapeDtypeStruct
BS = pl.BlockSpec
ANY = pl.BlockSpec(memory_space=pl.ANY)


def _cp(sem, vmem_mb=None):
    kw = dict(dimension_semantics=sem)
    if vmem_mb is not None:
        kw["vmem_limit_bytes"] = vmem_mb * 1024 * 1024
    return pltpu.CompilerParams(**kw)


def _dot(a, b):
    return jnp.dot(a, b, preferred_element_type=F32)


def _dot_nt(a, b):
    return lax.dot_general(a, b, (((1,), (1,)), ((), ())), preferred_element_type=F32)


def _dot_tn(a, b):
    return lax.dot_general(a, b, (((0,), (0,)), ((), ())), preferred_element_type=F32)


def _rms(x):
    r = lax.rsqrt(jnp.mean(x * x, axis=-1, keepdims=True) + EPS)
    return x * r, r


def _rms_bwd(dn, n, r):
    return r * (dn - n * jnp.mean(dn * n, axis=-1, keepdims=True))


_GC = math.sqrt(2.0 / math.pi)


def _gelu(y):
    return 0.5 * y * (1.0 + jnp.tanh(_GC * (y + 0.044715 * y * y * y)))


def _gelu_grad(y):
    th = jnp.tanh(_GC * (y + 0.044715 * y * y * y))
    return 0.5 * (1.0 + th) + 0.5 * y * (1.0 - th * th) * _GC * (1.0 + 3 * 0.044715 * y * y)


def _sigmoid(v):
    return 1.0 / (1.0 + jnp.exp(-v))


def _colsum(x):
    return jnp.sum(x, axis=0, keepdims=True)


def _tok(width, col=0):
    return BS((None, TB, width), lambda b, j: (b, j, col))


def _mod_spec(nct):
    return BS((None, None, 8, D), lambda b, j: (b, jnp.where(j >= nct, 1, 0), 0, 0))


def _full(shape):
    nd = len(shape)
    return BS(shape, lambda *a: (0,) * nd)


def _resident(shape):
    nd = len(shape)
    return BS(shape, lambda *a: (0,) * nd, pipeline_mode=pl.Buffered(1))


def _ada_fwd(cact8, w, b):
    def body(c_ref, w_ref, b_ref, o_ref):
        o_ref[...] = _dot(c_ref[...].astype(MXU), w_ref[...]) + b_ref[...]

    return pl.pallas_call(
        body, name="ada_fwd", grid=(NMOD,),
        in_specs=[BS((8, D), lambda j: (0, 0)), BS((D, D), lambda j: (0, j)), BS((1, D), lambda j: (0, j))],
        out_specs=BS((8, D), lambda j: (0, j)),
        out_shape=SDS((8, NMOD * D), F32), compiler_params=_cp(("arbitrary",)),
    )(cact8, w, b)


def _ada_bwd_c(dmod8, w):
    def body(d_ref, w_ref, o_ref):
        @pl.when(pl.program_id(0) == 0)
        def _():
            o_ref[...] = jnp.zeros_like(o_ref)
        o_ref[...] += _dot_nt(d_ref[...], w_ref[...])

    return pl.pallas_call(
        body, name="ada_bwd_c", grid=(NMOD,),
        in_specs=[BS((8, D), lambda j: (0, j)), BS((D, D), lambda j: (0, j))],
        out_specs=BS((8, D), lambda j: (0, 0)),
        out_shape=SDS((8, D), F32), compiler_params=_cp(("arbitrary",)),
    )(dmod8, w)


def _pre_mix(h, modtok, ng, w_in, nct):
    B, S, _ = h.shape

    def body(h_ref, mod_ref, g_ref, w_ref, p_ref, a_ref):
        n, _ = _rms(h_ref[...])
        a = ((n * g_ref[0:1, :]) * (1.0 + mod_ref[1:2, :]) + mod_ref[0:1, :]).astype(MXU)
        a_ref[...] = a
        p_ref[...] = _dot(a, w_ref[...])

    return pl.pallas_call(
        body, name="pre_mix", grid=(B, S // TB),
        in_specs=[_tok(D), _mod_spec(nct), _full((8, D)), _full((D, INW))],
        out_specs=[_tok(INW), _tok(D)],
        out_shape=[SDS((B, S, INW), F32), SDS((B, S, D), MXU)],
        compiler_params=_cp(("arbitrary", "arbitrary"), 40),
    )(h, modtok, ng, w_in)


def _pre_bwd(dp, dh_part, h, modtok, ng, w_in, nct):
    B, S, _ = h.shape

    def body(dp_ref, dhp_ref, h_ref, mod_ref, g_ref, w_ref, dh_ref, vec_ref):
        da = _dot_nt(dp_ref[...], w_ref[...])
        n, r = _rms(h_ref[...])
        g = g_ref[0:1, :]
        sc1 = 1.0 + mod_ref[1:2, :]
        vec_ref[...] = jnp.zeros_like(vec_ref)
        vec_ref[0:1, :] = _colsum(da)
        vec_ref[1:2, :] = _colsum(da * (n * g))
        vec_ref[2:3, :] = _colsum(da * sc1 * n)
        dh_ref[...] = dhp_ref[...] + _rms_bwd(da * sc1 * g, n, r)

    return pl.pallas_call(
        body, name="pre_bwd", grid=(B, S // TB),
        in_specs=[_tok(INW), _tok(D), _tok(D), _mod_spec(nct), _full((8, D)), _full((D, INW))],
        out_specs=[_tok(D), BS((None, None, 8, D), lambda b, j: (b, j, 0, 0))],
        out_shape=[SDS((B, S, D), F32), SDS((B, S // TB, 8, D), F32)],
        compiler_params=_cp(("arbitrary", "arbitrary"), 40),
    )(dp, dh_part, h, modtok, ng, w_in)


def _rotate(t, c, s):
    lane = lax.broadcasted_iota(jnp.int32, t.shape, 1)
    partner = jnp.where((lane % 32) < 16, pltpu.roll(t, 112, 1), pltpu.roll(t, 16, 1))
    return t * c + partner * s


def _to_groups(cols):
    lo = lax.broadcasted_iota(jnp.int32, cols[0].shape, 1) < HD
    out = []
    for g in range(QPK):
        a, b = cols[g // 2], cols[2 + g // 2]
        out.append(jnp.where(lo, a, pltpu.roll(b, HD, 1)) if g % 2 == 0 else jnp.where(lo, pltpu.roll(a, HD, 1), b))
    return out


def _from_groups(tiles):
    lo = lax.broadcasted_iota(jnp.int32, tiles[0].shape, 1) < HD
    out = []
    for k in range(NQ // 2):
        a, b = tiles[(2 * k) % QPK], tiles[(2 * k) % QPK + 1]
        out.append(jnp.where(lo, a, pltpu.roll(b, HD, 1)) if k < 2 else jnp.where(lo, pltpu.roll(a, HD, 1), b))
    return out


def _grp(width=128):
    return BS((None, QPK, TB, width), lambda b, j: (b, 0, j, 0))


_TAB = BS((TB, 128), lambda b, j: (j, 0))


def _rope_fwd(p, cos, sin):
    B, S, _ = p.shape

    def body(x_ref, c_ref, s_ref, q_ref, k_ref):
        c, s = c_ref[...], s_ref[...]
        rot = [_rotate(x_ref[:, k * 128:(k + 1) * 128], c, s) for k in range((AW + KVW) // 128)]
        k_ref[...] = rot[AW // 128].astype(MXU)
        for g, t in enumerate(_to_groups(rot[:AW // 128])):
            q_ref[g] = (t * SCALE).astype(MXU)

    return pl.pallas_call(
        body, name="rope_fwd", grid=(B, S // TB),
        in_specs=[_tok(AW + KVW), _TAB, _TAB],
        out_specs=[_grp(), _tok(KVW)],
        out_shape=[SDS((B, QPK, S, 128), MXU), SDS((B, S, KVW), MXU)],
        compiler_params=_cp(("arbitrary", "arbitrary")),
    )(p, cos, sin)


def _rope_bwd(dq4, dk, cos, nsin):
    B, _, S, _ = dq4.shape

    def body(q_ref, k_ref, c_ref, s_ref, o_ref):
        c, s = c_ref[...], s_ref[...]
        cols = _from_groups([q_ref[g] * SCALE for g in range(QPK)]) + [k_ref[...]]
        for k, t in enumerate(cols):
            o_ref[:, k * 128:(k + 1) * 128] = _rotate(t, c, s).astype(MXU)

    return pl.pallas_call(
        body, name="rope_bwd", grid=(B, S // TB),
        in_specs=[_grp(), _tok(KVW), _TAB, _TAB],
        out_specs=_tok(AW + KVW),
        out_shape=SDS((B, S, AW + KVW), MXU),
        compiler_params=_cp(("arbitrary", "arbitrary")),
    )(dq4, dk, cos, nsin)


def _attn_masks(i, S, Lc):
    ncb = Lc // QB
    is_lat = i >= ncb
    w0 = pl.multiple_of(jnp.clip((i - 1) * QB, 0, S - 3 * QB), QB)
    shape = (QPK * QB, 3 * QB)
    qpos = i * QB + jnp.bitwise_and(lax.broadcasted_iota(jnp.int32, shape, 0), QB - 1)
    kpos = w0 + lax.broadcasted_iota(jnp.int32, shape, 1)
    mask = jnp.logical_and(jnp.logical_and(kpos >= Lc, jnp.abs(qpos - kpos) <= WIN), is_lat)
    return w0, mask


def _attn_probs(qh, kch, kwh, mask, sk):
    sc = _dot_nt(qh, kch)
    sw = jnp.where(mask, _dot_nt(qh, kwh), NEG)
    m = jnp.maximum(jnp.maximum(jnp.max(sc, axis=-1, keepdims=True), jnp.max(sw, axis=-1, keepdims=True)), sk)
    ec = jnp.exp(sc - m)
    ew = jnp.exp(sw - m)
    es = jnp.exp(sk - m)
    inv = 1.0 / (jnp.sum(ec, axis=-1, keepdims=True) + jnp.sum(ew, axis=-1, keepdims=True) + es)
    return ec * inv, ew * inv, es * inv


def _sink_col(sink_ref, hk):
    return jnp.concatenate([jnp.full((QB, 1), sink_ref[hk * QPK + g], F32) for g in range(QPK)], axis=0)


_QGRP = BS((None, QPK, QB, 128), lambda b, i: (b, 0, i, 0))


def _first_last(B, nb):
    b, i = pl.program_id(0), pl.program_id(1)
    return jnp.logical_and(b == 0, i == 0), jnp.logical_and(b == B - 1, i == nb - 1)


def _attn_fwd(sink, q4, k, p, Lc, gather=None):
    B, S, _ = k.shape

    def body(sink_ref, q_ref, k_ref, v_ref, *rest):
        if gather is None:
            (o_ref,) = rest
        else:
            s_ref, o_ref, *comm = rest
            first, last = _first_last(B, S // QB)
            pl.when(first)(lambda: _gather_start(s_ref, *comm))
        w0, mask = _attn_masks(pl.program_id(1), S, Lc)
        kc = k_ref[0:Lc, :]
        vc = v_ref[0:Lc, :].astype(MXU)
        kw = k_ref[pl.ds(w0, 3 * QB), :]
        vw = v_ref[pl.ds(w0, 3 * QB), :].astype(MXU)
        q = q_ref[...].reshape(QPK * QB, 128)
        outs = []
        for hk in range(NKV):
            cs = slice(hk * HD, (hk + 1) * HD)
            pc, pw, _ = _attn_probs(q[:, cs], kc[:, cs], kw[:, cs], mask, _sink_col(sink_ref, hk))
            o = _dot(pc.astype(MXU), vc[:, cs]) + _dot(pw.astype(MXU), vw[:, cs])
            outs += [o[g * QB:(g + 1) * QB] for g in range(QPK)]
        o_ref[...] = jnp.concatenate(outs, axis=1).astype(o_ref.dtype)
        if gather is not None:
            pl.when(last)(lambda: _gather_wait(s_ref, *comm))

    in_specs = [BS(memory_space=pltpu.SMEM), _QGRP, BS((None, S, KVW), lambda b, i: (b, 0, 0)),
                BS((None, S, KVW), lambda b, i: (b, 0, (AW + KVW) // KVW))]
    out_specs = [BS((None, QB, AW), lambda b, i: (b, i, 0))]
    out_shape = [SDS((B, S, AW), MXU)]
    args, scratch = [sink, q4, k, p], []
    if gather is not None:
        in_specs.append(ANY)
        out_specs.append(ANY)
        out_shape.append(SDS((NCHIP,) + gather.shape, gather.dtype))
        args.append(gather)
        scratch = _gather_sems()
    res = pl.pallas_call(
        body, name="attn_fwd" if gather is None else "attn_fwd_gather", grid=(B, S // QB),
        in_specs=in_specs, out_specs=out_specs, out_shape=out_shape, scratch_shapes=scratch,
        compiler_params=_cp(("arbitrary", "arbitrary"), 40),
    )(*args)
    return res[0] if gather is None else res


def _attn_bwd(sink, q4, k, p, do4, Lc, scatter=None):
    B, S, _ = k.shape

    def body(sink_ref, q_ref, k_ref, v_ref, do_ref, *rest):
        if scatter is None:
            dq_ref, dk_ref, dv_ref, ds_ref = rest
        else:
            s_ref, dq_ref, dk_ref, dv_ref, ds_ref, *comm = rest
            first, last = _first_last(B, S // QB)
            pl.when(first)(lambda: _scatter_start(s_ref, *comm))
        i = pl.program_id(1)

        @pl.when(i == 0)
        def _():
            dk_ref[...] = jnp.zeros_like(dk_ref)
            dv_ref[...] = jnp.zeros_like(dv_ref)
            ds_ref[...] = jnp.zeros_like(ds_ref)

        w0, mask = _attn_masks(i, S, Lc)
        kc = k_ref[0:Lc, :]
        vc = v_ref[0:Lc, :].astype(MXU)
        kw = k_ref[pl.ds(w0, 3 * QB), :]
        vw = v_ref[pl.ds(w0, 3 * QB), :].astype(MXU)
        q = q_ref[...].reshape(QPK * QB, 128)
        do = do_ref[...].reshape(QPK * QB, 128)
        dqs, dsinks, dkcs, dkws, dvcs, dvws = [], [], [], [], [], []
        for hk in range(NKV):
            cs = slice(hk * HD, (hk + 1) * HD)
            qh = q[:, cs]
            pc, pw, ps = _attn_probs(qh, kc[:, cs], kw[:, cs], mask, _sink_col(sink_ref, hk))
            dob = do[:, cs].astype(MXU)
            dpc = _dot_nt(dob, vc[:, cs])
            dpw = _dot_nt(dob, vw[:, cs])
            delta = jnp.sum(pc * dpc, axis=-1, keepdims=True) + jnp.sum(pw * dpw, axis=-1, keepdims=True)
            dsc = (pc * (dpc - delta)).astype(MXU)
            dsw = (pw * (dpw - delta)).astype(MXU)
            dqs.append(_dot(dsc, kc[:, cs]) + _dot(dsw, kw[:, cs]))
            dkcs.append(_dot_tn(dsc, qh))
            dkws.append(_dot_tn(dsw, qh))
            dvcs.append(_dot_tn(pc.astype(MXU), dob))
            dvws.append(_dot_tn(pw.astype(MXU), dob))
            psd = ps * delta
            dsinks += [jnp.broadcast_to(-jnp.sum(psd[g * QB:(g + 1) * QB], axis=0, keepdims=True), (1, 128))
                       for g in range(QPK)]
        for g in range(QPK):
            dq_ref[g] = jnp.concatenate([dq[g * QB:(g + 1) * QB] for dq in dqs], axis=1)
        ds_ref[...] += jnp.concatenate(dsinks, axis=0)
        dk_ref[0:Lc, :] += jnp.concatenate(dkcs, axis=1)
        dv_ref[0:Lc, :] += jnp.concatenate(dvcs, axis=1)
        dk_ref[pl.ds(w0, 3 * QB), :] += jnp.concatenate(dkws, axis=1)
        dv_ref[pl.ds(w0, 3 * QB), :] += jnp.concatenate(dvws, axis=1)
        if scatter is not None:
            pl.when(last)(lambda: _scatter_wait(s_ref, *comm))

    kv_acc = BS((None, S, KVW), lambda b, i: (b, 0, 0))
    in_specs = [BS(memory_space=pltpu.SMEM), _QGRP, BS((None, S, KVW), lambda b, i: (b, 0, 0)),
                BS((None, S, KVW), lambda b, i: (b, 0, (AW + KVW) // KVW)), _QGRP]
    out_specs = [_QGRP, kv_acc, kv_acc, BS((None, 8, 128), lambda b, i: (b, 0, 0))]
    out_shape = [SDS((B, QPK, S, 128), F32), SDS((B, S, KVW), F32), SDS((B, S, KVW), F32), SDS((B, 8, 128), F32)]
    args, scratch = [sink, q4, k, p, do4], []
    if scatter is not None:
        in_specs.append(ANY)
        out_specs.append(ANY)
        out_shape.append(SDS((3,) + scatter.shape[1:], scatter.dtype))
        args.append(scatter)
        scratch = _scatter_sems()
    return pl.pallas_call(
        body, name="attn_bwd" if scatter is None else "attn_bwd_scatter", grid=(B, S // QB),
        in_specs=in_specs, out_specs=out_specs, out_shape=out_shape, scratch_shapes=scratch,
        compiler_params=_cp(("arbitrary", "arbitrary"), 48),
    )(*args)


def _halo_specs(width, col, S):
    nb8 = S // 8
    per = TB // 8
    prev = BS((None, 8, width), lambda b, j: (b, jnp.maximum(j * per - 1, 0), col))
    nxt = BS((None, 8, width), lambda b, j: (b, jnp.minimum((j + 1) * per, nb8 - 1), col))
    return prev, nxt


def _shift_dn(z, prev_row, at_start):
    row = lax.broadcasted_iota(jnp.int32, z.shape, 0)
    first = jnp.where(at_start, 0.0, 1.0) * prev_row
    return jnp.where(row == 0, first, pltpu.roll(z, 1, 0))


def _shift_up(z, next_row, at_end):
    row = lax.broadcasted_iota(jnp.int32, z.shape, 0)
    last = jnp.where(at_end, 0.0, 1.0) * next_row
    return jnp.where(row == z.shape[0] - 1, last, pltpu.roll(z, z.shape[0] - 1, 0))


def _conv_fwd(p, cw8, nct):
    B, S, _ = p.shape
    nt = S // TB

    def body(p_ref, pp_ref, pn_ref, w_ref, o_ref):
        j = pl.program_id(1)
        at_start = jnp.logical_or(j == 0, j == nct)
        at_end = jnp.logical_or(j == nct - 1, j == nt - 1)
        z = p_ref[:, 256:512] * p_ref[:, 512:768]
        zprev = pp_ref[7:8, 256:512] * pp_ref[7:8, 512:768]
        znext = pn_ref[0:1, 256:512] * pn_ref[0:1, 512:768]
        c3 = (_shift_dn(z, zprev, at_start) * w_ref[0:1, :] + z * w_ref[1:2, :]
              + _shift_up(z, znext, at_end) * w_ref[2:3, :])
        o_ref[...] = p_ref[:, 0:256] * c3

    prev, nxt = _halo_specs(3 * CW, 1, S)
    return pl.pallas_call(
        body, name="conv_fwd", grid=(B, nt),
        in_specs=[_tok(3 * CW, 1), prev, nxt, _full((8, CW))],
        out_specs=_tok(CW),
        out_shape=SDS((B, S, CW), F32),
        compiler_params=_cp(("arbitrary", "arbitrary")),
    )(p, p, p, cw8)


def _conv_bwd(p, dcv, cw8, nct):
    B, S, _ = p.shape
    nt = S // TB

    def body(p_ref, pp_ref, pn_ref, d_ref, dp_ref, dn_ref, w_ref, o_ref, dw_ref):
        j = pl.program_id(1)
        at_start = jnp.logical_or(j == 0, j == nct)
        at_end = jnp.logical_or(j == nct - 1, j == nt - 1)
        cb, cc, cx = p_ref[:, 0:256], p_ref[:, 256:512], p_ref[:, 512:768]
        z = cc * cx
        zm = _shift_dn(z, pp_ref[7:8, 256:512] * pp_ref[7:8, 512:768], at_start)
        zp = _shift_up(z, pn_ref[0:1, 256:512] * pn_ref[0:1, 512:768], at_end)
        d = d_ref[...]
        e = d * cb
        em = _shift_dn(e, dp_ref[7:8, :] * pp_ref[7:8, 0:256], at_start)
        ep = _shift_up(e, dn_ref[0:1, :] * pn_ref[0:1, 0:256], at_end)
        dz = e * w_ref[1:2, :] + ep * w_ref[0:1, :] + em * w_ref[2:3, :]
        o_ref[:, 0:256] = d * (zm * w_ref[0:1, :] + z * w_ref[1:2, :] + zp * w_ref[2:3, :])
        o_ref[:, 256:512] = dz * cx
        o_ref[:, 512:768] = dz * cc
        dw_ref[...] = jnp.zeros_like(dw_ref)
        dw_ref[0:1, :] = _colsum(e * zm)
        dw_ref[1:2, :] = _colsum(e * z)
        dw_ref[2:3, :] = _colsum(e * zp)

    prev, nxt = _halo_specs(3 * CW, 1, S)
    dprev, dnxt = _halo_specs(CW, 0, S)
    return pl.pallas_call(
        body, name="conv_bwd", grid=(B, nt),
        in_specs=[_tok(3 * CW, 1), prev, nxt, _tok(CW), dprev, dnxt, _full((8, CW))],
        out_specs=[_tok(3 * CW), BS((None, None, 8, CW), lambda b, j: (b, j, 0, 0))],
        out_shape=[SDS((B, S, 3 * CW), F32), SDS((B, nt, 8, CW), F32)],
        compiler_params=_cp(("arbitrary", "arbitrary")),
    )(p, p, p, dcv, dcv, dcv, cw8)


HW = SGRP * SST


def _rev_chunk(k, nk, ncc):
    return jnp.where(k < ncc, ncc - 1 - k, nk - 1 - k + ncc)


def _scan_perm(B):
    n = B * TC
    pm = np.zeros((2 * n, 2 * n), np.float32)
    for t in range(TC):
        for e in range(B):
            pm[t * 2 * B + e, e * TC + t] = 1.0
            pm[t * 2 * B + B + e, n + e * TC + (TC - 1 - t)] = 1.0
    return pm


def _scan_drive(uf_ref, ub_ref, pm_ref, b4_ref, fwd, dbuf, n):
    u_tok = jnp.concatenate([uf_ref[...].reshape(n, SW), ub_ref[...].reshape(n, SW)], axis=0).astype(MXU)
    u_tm = _dot(pm_ref[...], u_tok).astype(MXU)
    d4 = _dot(u_tm, b4_ref[...])
    dbuf[...] = jnp.where(fwd, d4[:, :2 * HW], d4[:, 2 * HW:])
    return u_tm


def _scan_fwd(p, pm, pmt, b4, a, c2, Lc):
    B, S, _ = p.shape
    nch, n = 2 * B, B * TC
    nk, ncc = S // TC, Lc // TC
    ucol = (INW - SW) // SW

    def body(uf_ref, ub_ref, pm_ref, pmt_ref, b4_ref, a_ref, c2_ref, yf_ref, yb_ref, hb_ref, hcar, dbuf, hbuf):
        @pl.when(pl.program_id(0) == 0)
        def _():
            hcar[...] = jnp.zeros_like(hcar)

        fwd = (lax.broadcasted_iota(jnp.int32, (2 * n, 1), 0) % nch) < B
        _scan_drive(uf_ref, ub_ref, pm_ref, b4_ref, fwd, dbuf, n)
        hb_ref[...] = hcar[...]
        ar, ai = a_ref[:, :HW], a_ref[:, HW:]

        def step(t, h):
            hr, hi = h
            r0 = pl.multiple_of(t * nch, nch)
            nr = ar * hr - ai * hi + dbuf[pl.ds(r0, nch), :HW]
            ni = ar * hi + ai * hr + dbuf[pl.ds(r0, nch), HW:]
            hbuf[pl.ds(r0, nch), :HW] = nr
            hbuf[pl.ds(r0, nch), HW:] = ni
            return nr, ni

        hr, hi = lax.fori_loop(0, TC, step, (hcar[:, :HW], hcar[:, HW:]))
        hcar[:, :HW] = hr
        hcar[:, HW:] = hi
        h_tok = _dot(pmt_ref[...], hbuf[...].astype(MXU)).astype(MXU)
        yf_ref[...] = _dot(h_tok[:n], c2_ref[:, :SW]).reshape(B, TC, SW)
        yb_ref[...] = _dot(h_tok[n:], c2_ref[:, SW:]).reshape(B, TC, SW)

    return pl.pallas_call(
        body, name="scan_fwd", grid=(nk,),
        in_specs=[BS((B, TC, SW), lambda k: (0, k, ucol)),
                  BS((B, TC, SW), lambda k: (0, _rev_chunk(k, nk, ncc), ucol)),
                  _full((2 * n, 2 * n)), _full((2 * n, 2 * n)),
                  _full((SW, 4 * HW)), _full((nch, 2 * HW)), _full((2 * HW, 2 * SW))],
        out_specs=[BS((B, TC, SW), lambda k: (0, k, 0)),
                   BS((B, TC, SW), lambda k: (0, _rev_chunk(k, nk, ncc), 0)),
                   BS((None, nch, 2 * HW), lambda k: (k, 0, 0))],
        out_shape=[SDS((B, S, SW), F32), SDS((B, S, SW), F32), SDS((nk, nch, 2 * HW), F32)],
        scratch_shapes=[pltpu.VMEM((nch, 2 * HW), F32), pltpu.VMEM((2 * n, 2 * HW), F32),
                        pltpu.VMEM((2 * n, 2 * HW), F32)],
        compiler_params=_cp(("arbitrary",), 48),
    )(p, p, pm, pmt, b4, a, c2)


def _scan_bwd(p, dy, hb, pm, pmt, b4, a, c2, Lc):
    B, S, _ = p.shape
    nch, n = 2 * B, B * TC
    nk, ncc = S // TC, Lc // TC
    ucol = (INW - SW) // SW
    rows = 2 * n

    def body(uf_ref, ub_ref, dyf_ref, dyb_ref, hb_ref, pm_ref, pmt_ref, b4_ref, a_ref, c2_ref,
             duf_ref, dub_ref, db4_ref, da_ref, dc2_ref, gcar, dbuf, hbuf, gbuf):
        @pl.when(pl.program_id(0) == 0)
        def _():
            gcar[...] = jnp.zeros_like(gcar)
            db4_ref[...] = jnp.zeros_like(db4_ref)
            da_ref[...] = jnp.zeros_like(da_ref)
            dc2_ref[...] = jnp.zeros_like(dc2_ref)

        fwd = (lax.broadcasted_iota(jnp.int32, (rows, 1), 0) % nch) < B
        u_tm = _scan_drive(uf_ref, ub_ref, pm_ref, b4_ref, fwd, dbuf, n)
        ar, ai = a_ref[:, :HW], a_ref[:, HW:]
        hbuf[0:nch, :] = hb_ref[...]

        def fstep(t, h):
            hr, hi = h
            r0 = pl.multiple_of(t * nch, nch)
            nr = ar * hr - ai * hi + dbuf[pl.ds(r0, nch), :HW]
            ni = ar * hi + ai * hr + dbuf[pl.ds(r0, nch), HW:]
            hbuf[pl.ds(r0 + nch, nch), :HW] = nr
            hbuf[pl.ds(r0 + nch, nch), HW:] = ni
            return nr, ni

        lax.fori_loop(0, TC, fstep, (hb_ref[:, :HW], hb_ref[:, HW:]))

        dy_tok = jnp.concatenate([dyf_ref[...].reshape(n, SW), dyb_ref[...].reshape(n, SW)], axis=0).astype(MXU)
        dy_tm = _dot(pm_ref[...], dy_tok)
        dy2 = jnp.concatenate([jnp.where(fwd, dy_tm, 0.0), jnp.where(fwd, 0.0, dy_tm)], axis=1).astype(MXU)
        gbuf[...] = _dot_nt(dy2, c2_ref[...])
        dc2_ref[...] += _dot_tn(hbuf[nch:, :].astype(MXU), dy2)

        def bstep(i, cary):
            cr, ci = cary
            r0 = pl.multiple_of((TC - 1 - i) * nch, nch)
            gr = gbuf[pl.ds(r0, nch), :HW] + cr
            gi = gbuf[pl.ds(r0, nch), HW:] + ci
            gbuf[pl.ds(r0, nch), :HW] = gr
            gbuf[pl.ds(r0, nch), HW:] = gi
            return ar * gr + ai * gi, ar * gi - ai * gr

        cr, ci = lax.fori_loop(0, TC, bstep, (gcar[:, :HW], gcar[:, HW:]))
        gcar[:, :HW] = cr
        gcar[:, HW:] = ci

        gr = gbuf[:, :HW].reshape(TC, nch, HW)
        gi = gbuf[:, HW:].reshape(TC, nch, HW)
        hpr = hbuf[0:rows, :HW].reshape(TC, nch, HW)
        hpi = hbuf[0:rows, HW:].reshape(TC, nch, HW)
        da_ref[:, :HW] += jnp.sum(gr * hpr + gi * hpi, axis=0)
        da_ref[:, HW:] += jnp.sum(gi * hpr - gr * hpi, axis=0)

        g = gbuf[...]
        gm = g.astype(MXU)
        dd4 = jnp.concatenate([jnp.where(fwd, g, 0.0), jnp.where(fwd, 0.0, g)], axis=1).astype(MXU)
        db4_ref[...] += _dot_tn(u_tm, dd4)
        g_tok = _dot(pmt_ref[...], gm).astype(MXU)
        duf_ref[...] = _dot_nt(g_tok[:n], b4_ref[:, :2 * HW]).reshape(B, TC, SW)
        dub_ref[...] = _dot_nt(g_tok[n:], b4_ref[:, 2 * HW:]).reshape(B, TC, SW)

    fwd_blk = lambda col: BS((B, TC, SW), lambda k: (0, nk - 1 - k, col))
    bwd_blk = lambda col: BS((B, TC, SW), lambda k: (0, _rev_chunk(nk - 1 - k, nk, ncc), col))
    return pl.pallas_call(
        body, name="scan_bwd", grid=(nk,),
        in_specs=[fwd_blk(ucol), bwd_blk(ucol), fwd_blk(0), bwd_blk(0),
                  BS((None, nch, 2 * HW), lambda k: (nk - 1 - k, 0, 0)),
                  _full((rows, rows)), _full((rows, rows)),
                  _full((SW, 4 * HW)), _full((nch, 2 * HW)), _full((2 * HW, 2 * SW))],
        out_specs=[fwd_blk(0), bwd_blk(0), _full((SW, 4 * HW)), _full((nch, 2 * HW)), _full((2 * HW, 2 * SW))],
        out_shape=[SDS((B, S, SW), F32), SDS((B, S, SW), F32), SDS((SW, 4 * HW), F32), SDS((nch, 2 * HW), F32),
                   SDS((2 * HW, 2 * SW), F32)],
        scratch_shapes=[pltpu.VMEM((nch, 2 * HW), F32), pltpu.VMEM((rows, 2 * HW), F32),
                        pltpu.VMEM((rows + nch, 2 * HW), F32), pltpu.VMEM((rows, 2 * HW), F32)],
        compiler_params=_cp(("arbitrary",), 56),
    )(p, p, dy, dy, hb, pm, pmt, b4, a, c2)


def _post_mix(o, cv, yf, yb, p, h, modtok, ng, dsk, wg, bg, wo, nct):
    B, S, _ = h.shape

    def body(o_ref, cv_ref, yf_ref, yb_ref, u_ref, h_ref, mod_ref, g_ref, dsk_ref, wg_ref, bg_ref, wo_ref,
             h1_ref, mix_ref, m_ref, y_ref):
        y = yf_ref[...] + yb_ref[...] + dsk_ref[...] * u_ref[...]
        g = _gelu(y)
        s = g * _sigmoid(_dot(g.astype(MXU), wg_ref[...]) + bg_ref[...])
        mix_ref[:, 0:AW] = o_ref[...]
        mix_ref[:, AW:AW + CW] = cv_ref[...].astype(MXU)
        mix_ref[:, AW + CW:] = s.astype(MXU)
        m = _dot(mix_ref[...], wo_ref[...])
        n, _ = _rms(m)
        h1_ref[...] = h_ref[...] + mod_ref[2:3, :] * (n * g_ref[1:2, :])
        m_ref[...] = m
        y_ref[...] = y

    return pl.pallas_call(
        body, name="post_mix", grid=(B, S // TB),
        in_specs=[_tok(AW), _tok(CW), _tok(SW), _tok(SW), _tok(SW, (INW - SW) // SW), _tok(D), _mod_spec(nct),
                  _full((8, D)), _full((1, SW)), _full((SW, SW)), _full((1, SW)), _full((D, D))],
        out_specs=[_tok(D), _tok(D), _tok(D), _tok(SW)],
        out_shape=[SDS((B, S, D), F32), SDS((B, S, D), MXU), SDS((B, S, D), F32), SDS((B, S, SW), F32)],
        compiler_params=_cp(("arbitrary", "arbitrary"), 40),
    )(o, cv, yf, yb, p, h, modtok, ng, dsk, wg, bg, wo)


def _post_bwd(dh1, m, y, p, modtok, ng, dsk, wg, bg, wo, nct):
    B, S, _ = m.shape

    def body(dh_ref, m_ref, y_ref, u_ref, mod_ref, g_ref, dsk_ref, wg_ref, bg_ref, wo_ref,
             dm_ref, da_ref, dc_ref, dy_ref, du_ref, vec_ref, vec2_ref, dwg_ref):
        @pl.when(jnp.logical_and(pl.program_id(0) == 0, pl.program_id(1) == 0))
        def _():
            dwg_ref[...] = jnp.zeros_like(dwg_ref)

        n, r = _rms(m_ref[...])
        dh1 = dh_ref[...]
        gpm = g_ref[1:2, :]
        dr = dh1 * mod_ref[2:3, :]
        vec_ref[...] = jnp.zeros_like(vec_ref)
        vec_ref[0:1, :] = _colsum(dh1 * (n * gpm))
        vec_ref[1:2, :] = _colsum(dr * n)
        dm = _rms_bwd(dr * gpm, n, r).astype(MXU)
        dm_ref[...] = dm
        dmix = _dot_nt(dm, wo_ref[...])
        for g, t in enumerate(_to_groups([dmix[:, k * 128:(k + 1) * 128] for k in range(AW // 128)])):
            da_ref[g] = t
        dc_ref[...] = dmix[:, AW:AW + CW]
        ds = dmix[:, AW + CW:]
        yv = y_ref[...]
        g = _gelu(yv)
        gb = g.astype(MXU)
        sg = _sigmoid(_dot(gb, wg_ref[...]) + bg_ref[...])
        dv = ds * g * sg * (1.0 - sg)
        dvb = dv.astype(MXU)
        dg = ds * sg + _dot_nt(dvb, wg_ref[...])
        dwg_ref[...] += _dot_tn(gb, dvb)
        dy = dg * _gelu_grad(yv)
        dy_ref[...] = dy
        du_ref[...] = dy * dsk_ref[...]
        vec2_ref[...] = jnp.zeros_like(vec2_ref)
        vec2_ref[0:1, :] = _colsum(dy * u_ref[...])
        vec2_ref[1:2, :] = _colsum(dv)

    nt = S // TB
    return pl.pallas_call(
        body, name="post_bwd", grid=(B, nt),
        in_specs=[_tok(D), _tok(D), _tok(SW), _tok(SW, (INW - SW) // SW), _mod_spec(nct), _full((8, D)),
                  _full((1, SW)), _full((SW, SW)), _full((1, SW)), _full((D, D))],
        out_specs=[_tok(D), _grp(), _tok(CW), _tok(SW), _tok(SW),
                   BS((None, None, 8, D), lambda b, j: (b, j, 0, 0)),
                   BS((None, None, 8, SW), lambda b, j: (b, j, 0, 0)), _full((SW, SW))],
        out_shape=[SDS((B, S, D), MXU), SDS((B, QPK, S, 128), F32), SDS((B, S, CW), F32), SDS((B, S, SW), F32),
                   SDS((B, S, SW), F32), SDS((B, nt, 8, D), F32), SDS((B, nt, 8, SW), F32), SDS((SW, SW), F32)],
        compiler_params=_cp(("arbitrary", "arbitrary"), 40),
    )(dh1, m, y, p, modtok, ng, dsk, wg, bg, wo)


def _mlp_fwd(h1, modtok, ng, w1, w2, nct):
    B, S, _ = h1.shape

    def body(h_ref, mod_ref, g_ref, w1_ref, w2_ref, h2_ref, a2_ref, z_ref, f_ref):
        n, _ = _rms(h_ref[...])
        a2 = ((n * g_ref[2:3, :]) * (1.0 + mod_ref[4:5, :]) + mod_ref[3:4, :]).astype(MXU)
        a2_ref[...] = a2
        ff = jnp.zeros((TB, D), F32)
        for c in range(DFF // TF):
            cs = slice(c * TF, (c + 1) * TF)
            z = _dot(a2, w1_ref[:, cs])
            z_ref[:, cs] = z.astype(MXU)
            r = jnp.maximum(z, 0.0)
            ff = ff + _dot((r * r).astype(MXU), w2_ref[cs, :])
        f_ref[...] = ff
        n, _ = _rms(ff)
        h2_ref[...] = h_ref[...] + mod_ref[5:6, :] * (n * g_ref[3:4, :])

    return pl.pallas_call(
        body, name="mlp_fwd", grid=(B, S // TB),
        in_specs=[_tok(D), _mod_spec(nct), _full((8, D)), _resident((D, DFF)), _resident((DFF, D))],
        out_specs=[_tok(D), _tok(D), _tok(DFF), _tok(D)],
        out_shape=[SDS((B, S, D), F32), SDS((B, S, D), MXU), SDS((B, S, DFF), MXU), SDS((B, S, D), F32)],
        compiler_params=_cp(("arbitrary", "arbitrary"), 48),
    )(h1, modtok, ng, w1, w2)


def _mlp_bwd(dh2, h1, f, z, modtok, ng, w1, w2, nct):
    B, S, _ = h1.shape

    def body(dh_ref, h_ref, f_ref, z_ref, mod_ref, g_ref, w1_ref, w2_ref, dh1_ref, df_ref, dz_ref, vec_ref):
        n, r = _rms(f_ref[...])
        dh2 = dh_ref[...]
        gp = g_ref[3:4, :]
        dr = dh2 * mod_ref[5:6, :]
        vec_ref[...] = jnp.zeros_like(vec_ref)
        vec_ref[3:4, :] = _colsum(dh2 * (n * gp))
        vec_ref[4:5, :] = _colsum(dr * n)
        df = _rms_bwd(dr * gp, n, r).astype(MXU)
        df_ref[...] = df
        da = jnp.zeros((TB, D), F32)
        for c in range(DFF // TF):
            cs = slice(c * TF, (c + 1) * TF)
            dr2 = _dot_nt(df, w2_ref[cs, :])
            dz = (dr2 * 2.0 * jnp.maximum(z_ref[:, cs].astype(F32), 0.0)).astype(MXU)
            dz_ref[:, cs] = dz
            da = da + _dot_nt(dz, w1_ref[:, cs])
        n, r = _rms(h_ref[...])
        g = g_ref[2:3, :]
        sc1 = 1.0 + mod_ref[4:5, :]
        vec_ref[0:1, :] = _colsum(da)
        vec_ref[1:2, :] = _colsum(da * (n * g))
        vec_ref[2:3, :] = _colsum(da * sc1 * n)
        dh1_ref[...] = dh2 + _rms_bwd(da * sc1 * g, n, r)

    return pl.pallas_call(
        body, name="mlp_bwd", grid=(B, S // TB),
        in_specs=[_tok(D), _tok(D), _tok(D), _tok(DFF), _mod_spec(nct), _full((8, D)),
                  _resident((D, DFF)), _resident((DFF, D))],
        out_specs=[_tok(D), _tok(D), _tok(DFF), BS((None, None, 8, D), lambda b, j: (b, j, 0, 0))],
        out_shape=[SDS((B, S, D), F32), SDS((B, S, D), MXU), SDS((B, S, DFF), MXU), SDS((B, S // TB, 8, D), F32)],
        compiler_params=_cp(("arbitrary", "arbitrary"), 56),
    )(dh2, h1, f, z, modtok, ng, w1, w2)


def _loss(h, target, Lc):
    B, S, _ = h.shape
    nt, nct = S // TB, Lc // TB

    def body(h_ref, t_ref, dh_ref, l_ref):
        lat = pl.program_id(1) >= nct
        err = jnp.where(lat, h_ref[...] - t_ref[...], 0.0)
        dh_ref[...] = err * (1.0 / D)
        l_ref[...] = jnp.broadcast_to(jnp.sum(err * err, keepdims=True), (8, 128))

    return pl.pallas_call(
        body, name="loss", grid=(B, nt),
        in_specs=[_tok(D), BS((None, TB, D), lambda b, j: (b, jnp.maximum(j - nct, 0), 0))],
        out_specs=[_tok(D), BS((None, None, 8, 128), lambda b, j: (b, j, 0, 0))],
        out_shape=[SDS((B, S, D), F32), SDS((B, nt, 8, 128), F32)],
        compiler_params=_cp(("arbitrary", "arbitrary")),
    )(h, target)


def _mm_tn(a, b, name, relu2=False):
    T, M = a.shape
    N = b.shape[1]
    def blk(n):
        return max(b for b in range(128, 1025, 128) if n % b == 0)

    bm, bn, tk = blk(M), blk(N), min(T, 512)
    assert T % tk == 0
    nkk = T // tk

    def body(a_ref, b_ref, o_ref):
        @pl.when(pl.program_id(2) == 0)
        def _():
            o_ref[...] = jnp.zeros_like(o_ref)

        av = a_ref[...]
        if relu2:
            r = jnp.maximum(av.astype(F32), 0.0)
            av = (r * r).astype(MXU)
        o_ref[...] += _dot_tn(av, b_ref[...])

    return pl.pallas_call(
        body, name=name, grid=(M // bm, N // bn, nkk),
        in_specs=[BS((tk, bm), lambda i, j, k: (k, i)), BS((tk, bn), lambda i, j, k: (k, j))],
        out_specs=BS((bm, bn), lambda i, j, k: (i, j)),
        out_shape=SDS((M, N), F32),
        compiler_params=_cp(("arbitrary", "arbitrary", "arbitrary"), 40),
    )(a, b)


_PAD_ROWS = 64


def _block_rows(R):
    for br in (256, _PAD_ROWS):
        if R % br == 0:
            return br
    raise ValueError(f"row count {R} is not a multiple of {_PAD_ROWS}")


def _adamw(w, ga, gb, m, v, name, row0=0):
    R, C = w.shape
    br = _block_rows(R)
    off = row0 // br

    def body(w_ref, ga_ref, gb_ref, m_ref, v_ref, g_out, d_out, m_out, v_out):
        g = ga_ref[...] + gb_ref[...]
        m2 = B1 * m_ref[...] + (1.0 - B1) * g
        v2 = B2 * v_ref[...] + (1.0 - B2) * (g * g)
        m_hat = m2 / (1.0 - B1 ** STEP)
        v_hat = v2 / (1.0 - B2 ** STEP)
        g_out[...] = g
        d_out[...] = -LR * (m_hat / (jnp.sqrt(v_hat) + AEPS) + WD * w_ref[...])
        m_out[...] = m2
        v_out[...] = v2

    own = BS((br, C), lambda i: (i, 0))
    shifted = BS((br, C), lambda i: (i + off, 0))
    return pl.pallas_call(
        body, name=name, grid=(R // br,),
        in_specs=[own, shifted, shifted, own, own],
        out_specs=[own] * 4, out_shape=[SDS((R, C), F32)] * 4,
        compiler_params=_cp(("arbitrary",)),
    )(w, ga, gb, m, v)


def _my_xyc():
    return lax.axis_index("x"), lax.axis_index("y"), lax.axis_index("c")


def _chip_peers(x, y):
    return [(1 - x, y), (x, 1 - y), (1 - x, 1 - y)]


def _gather_chips(shard, name):
    R, C = shard.shape

    def body(s_ref, o_ref, ssem, rsem, lsem):
        _gather_start(s_ref, o_ref, ssem, rsem, lsem)
        _gather_wait(s_ref, o_ref, ssem, rsem, lsem)

    return pl.pallas_call(
        body, name=name, in_specs=[ANY], out_specs=ANY,
        out_shape=SDS((NCHIP, R, C), shard.dtype), scratch_shapes=_gather_sems(),
    )(shard)


def _gather_sems():
    return [pltpu.SemaphoreType.DMA((3,)), pltpu.SemaphoreType.DMA((3,)), pltpu.SemaphoreType.DMA]


def _gather_copies(s_ref, o_ref, ssem, rsem, lsem):
    x, y, c = _my_xyc()
    mine = pltpu.make_async_copy(s_ref, o_ref.at[2 * x + y], lsem)
    sends, recvs = [], []
    for j, (px, py) in enumerate(_chip_peers(x, y)):
        sems = dict(send_sem=ssem.at[j], recv_sem=rsem.at[j], device_id=(px, py, c), device_id_type=MESH)
        sends.append(pltpu.make_async_remote_copy(src_ref=s_ref, dst_ref=o_ref.at[2 * x + y], **sems))
        recvs.append(pltpu.make_async_remote_copy(src_ref=s_ref, dst_ref=o_ref.at[2 * px + py], **sems))
    return mine, sends, recvs


def _gather_start(*refs):
    mine, sends, _ = _gather_copies(*refs)
    mine.start()
    for cp in sends:
        cp.start()


def _gather_wait(*refs):
    mine, sends, recvs = _gather_copies(*refs)
    for cp in recvs:
        cp.wait_recv()
    for cp in sends:
        cp.wait_send()
    mine.wait()


def _scatter_sems():
    return [pltpu.SemaphoreType.DMA((3,)), pltpu.SemaphoreType.DMA((3,))]


def _scatter_copies(s_ref, o_ref, ssem, rsem):
    x, y, c = _my_xyc()
    return [pltpu.make_async_remote_copy(
        src_ref=s_ref.at[2 * px + py], dst_ref=o_ref.at[j], send_sem=ssem.at[j], recv_sem=rsem.at[j],
        device_id=(px, py, c), device_id_type=MESH) for j, (px, py) in enumerate(_chip_peers(x, y))]


def _scatter_start(*refs):
    for cp in _scatter_copies(*refs):
        cp.start()


def _scatter_wait(*refs):
    for cp in _scatter_copies(*refs):
        cp.wait()


def _scatter_chips(send, name):
    _, R, C = send.shape

    def body(s_ref, o_ref, ssem, rsem):
        _scatter_start(s_ref, o_ref, ssem, rsem)
        _scatter_wait(s_ref, o_ref, ssem, rsem)

    return pl.pallas_call(
        body, name=name, in_specs=[ANY], out_specs=ANY,
        out_shape=SDS((3, R, C), send.dtype), scratch_shapes=_scatter_sems(),
    )(send)


def _swap_cores(v, name):
    def body(s_ref, o_ref, ssem, rsem):
        x, y, c = _my_xyc()
        cp = pltpu.make_async_remote_copy(src_ref=s_ref, dst_ref=o_ref, send_sem=ssem, recv_sem=rsem,
                                          device_id=(x, y, 1 - c), device_id_type=MESH)
        cp.start()
        cp.wait()

    return pl.pallas_call(
        body, name=name, in_specs=[ANY], out_specs=ANY, out_shape=SDS(v.shape, v.dtype),
        scratch_shapes=[pltpu.SemaphoreType.DMA, pltpu.SemaphoreType.DMA],
    )(v)


def _gather_all(v, name):
    R, C = v.shape

    def body(s_ref, o_ref, ssem, rsem, lsem):
        x, y, c = _my_xyc()
        me = 4 * x + 2 * y + c
        mine = pltpu.make_async_copy(s_ref, o_ref.at[me], lsem)
        mine.start()

        def peer(j):
            fx, fy, fc = (j >> 2) & 1, (j >> 1) & 1, j & 1
            return (x ^ fx, y ^ fy, c ^ fc)

        cps = []
        for j in range(1, 8):
            cps.append(pltpu.make_async_remote_copy(
                src_ref=s_ref, dst_ref=o_ref.at[me], send_sem=ssem.at[j - 1], recv_sem=rsem.at[j - 1],
                device_id=peer(j), device_id_type=MESH))
            cps[-1].start()
        for j in range(1, 8):
            px, py, pc = peer(j)
            pltpu.make_async_remote_copy(
                src_ref=s_ref, dst_ref=o_ref.at[4 * px + 2 * py + pc], send_sem=ssem.at[j - 1],
                recv_sem=rsem.at[j - 1], device_id=peer(j), device_id_type=MESH).wait_recv()
        for cp in cps:
            cp.wait_send()
        mine.wait()

    return pl.pallas_call(
        body, name=name, in_specs=[ANY], out_specs=ANY, out_shape=SDS((8, R, C), v.dtype),
        scratch_shapes=[pltpu.SemaphoreType.DMA((7,)), pltpu.SemaphoreType.DMA((7,)), pltpu.SemaphoreType.DMA],
    )(v)


def _sum_slots(own, slots, name):
    n, R, C = slots.shape
    br = _block_rows(R)

    def body(*refs):
        o_ref = refs[-1]
        if own is None:
            acc = refs[0][0].astype(F32)
            first = 1
            s_ref = refs[0]
        else:
            acc = refs[0][...]
            first = 0
            s_ref = refs[1]
        for k in range(first, n):
            acc = acc + s_ref[k].astype(F32)
        o_ref[...] = acc

    row = BS((br, C), lambda i: (i, 0))
    slab = BS((n, br, C), lambda i: (0, i, 0))
    ins, args = ([slab], [slots]) if own is None else ([row, slab], [own, slots])
    return pl.pallas_call(
        body, name=name, grid=(R // br,), in_specs=ins, out_specs=row, out_shape=SDS((R, C), F32),
        compiler_params=_cp(("arbitrary",)),
    )(*args)


def _rope_tables(L, Lc):
    n = jnp.arange(L)
    row = (n // GRID_W).astype(F32)
    col = (n % GRID_W).astype(F32)
    freqs = ROPE_BASE ** (-jnp.arange(16, dtype=F32) / 16)
    lane = jnp.arange(128)
    dd = lane % HD
    fr = freqs[dd % 16]
    ang = jnp.where(dd < 32, row[:, None], col[:, None]) * fr[None, :]
    sign = jnp.where((dd % 32) < 16, -1.0, 1.0)
    cos = jnp.concatenate([jnp.ones((Lc, 128), F32), jnp.cos(ang)], axis=0)
    sin = jnp.concatenate([jnp.zeros((Lc, 128), F32), jnp.sin(ang) * sign[None, :]], axis=0)
    return cos, sin


def _ssm_prep(lam_re, lam_im, log_dt, b_re, b_im, c_re, c_im, B):
    dt = jnp.exp(log_dt)[..., None]
    mag = jnp.exp(lam_re * dt)
    ar = mag * jnp.cos(lam_im * dt)
    ai = mag * jnp.sin(lam_im * dt)
    den = lam_re * lam_re + lam_im * lam_im
    kr = ((ar - 1.0) * lam_re + ai * lam_im) / den
    ki = (ai * lam_re - (ar - 1.0) * lam_im) / den
    bbr = kr[..., None] * b_re - ki[..., None] * b_im
    bbi = kr[..., None] * b_im + ki[..., None] * b_re
    eye = jnp.eye(SGRP, dtype=F32)

    def bblk(t):
        return jnp.einsum("gpi,gh->gihp", t, eye).reshape(SGRP * SCH, SGRP * SST)

    def cblk(t):
        return jnp.einsum("gip,gh->gphi", t, eye).reshape(SGRP * SST, SGRP * SCH)

    b4 = jnp.concatenate([bblk(bbr[0]), bblk(bbi[0]), bblk(bbr[1]), bblk(bbi[1])], axis=1)
    c2 = jnp.concatenate([jnp.concatenate([cblk(c_re[0]), -cblk(c_im[0])], axis=0),
                          jnp.concatenate([cblk(c_re[1]), -cblk(c_im[1])], axis=0)], axis=1)
    a2 = jnp.concatenate([ar.reshape(2, -1), ai.reshape(2, -1)], axis=1)
    a = jnp.repeat(a2, B, axis=0)
    return a, b4, c2


def _pad_rows(v, rows=8):
    return jnp.concatenate([v, jnp.zeros((rows - v.shape[0],) + v.shape[1:], v.dtype)], axis=0)


_BIG = ("w_ada", "w_in", "w_out", "w_mlp_in", "w_mlp_out")
_BIG_AXIS = {"w_ada": 2, "w_in": 2, "w_out": 1, "w_mlp_in": 2, "w_mlp_out": 1}


def _split_shards(g, axis):
    L, r, c = g.shape
    if axis == 2:
        return jnp.transpose(g.reshape(L, r, NCHIP, c // NCHIP), (2, 0, 1, 3))
    return jnp.transpose(g.reshape(L, NCHIP, r // NCHIP, c), (1, 0, 2, 3))


def _join_shards(s, axis):
    _, L, r, c = s.shape
    if axis == 2:
        return jnp.transpose(s, (1, 2, 0, 3)).reshape(L, r, NCHIP * c)
    return jnp.transpose(s, (1, 0, 2, 3)).reshape(L, NCHIP * r, c)


def _local_step(x, c, ctx, loss_target, P, ex):
    B, L, _ = x.shape
    Lc = ctx.shape[1]
    S = Lc + L
    nct = Lc // TB
    nt = S // TB
    nch = 2 * B
    cos, sin = _rope_tables(L, Lc)
    perm = _scan_perm(B)
    pm, pmt = jnp.asarray(perm, MXU), jnp.asarray(perm.T, MXU)

    c_act = jax.nn.silu(c)
    cc_act, silu_vjp = jax.vjp(jax.nn.silu, P["c_ctx"])
    cact8 = _pad_rows(jnp.concatenate([c_act, cc_act[None, :]], axis=0))
    h = jnp.concatenate([ctx, x], axis=1)

    saved = []
    for l in range(NLAYER):
        Wl = ex.weights(l)
        mod8 = _ada_fwd(cact8, Wl["w_ada"], P["b_ada"][l][None, :])
        mod6 = mod8.reshape(8, NMOD, D)
        modtok = jnp.stack([jnp.broadcast_to(mod6[B], (B, NMOD, D)), mod6[:B]], axis=1)
        modtok = jnp.concatenate([modtok, jnp.zeros((B, 2, 2, D), F32)], axis=2)
        ng = _pad_rows(P["norm_g"][l])
        cw8 = _pad_rows(P["conv_w"][l])
        (a_s, b4, c2), prep_vjp = jax.vjp(
            lambda *t: _ssm_prep(*t, B), P["ssm_lam_re"][l], P["ssm_lam_im"][l], P["ssm_log_dt"][l],
            P["ssm_b_re"][l], P["ssm_b_im"][l], P["ssm_c_re"][l], P["ssm_c_im"][l])
        b4m, c2m = b4.astype(MXU), c2.astype(MXU)
        wg = P["w_glu"][l].astype(MXU)
        dsk = P["ssm_d"][l][None, :]
        bg = P["b_glu"][l][None, :]

        p, a = _pre_mix(h, modtok, ng, Wl["w_in"], nct)
        q4, kr = _rope_fwd(p, cos, sin)
        o = ex.attn_fwd(l, P["attn_sink"][l], q4, kr, p, Lc)
        cv = _conv_fwd(p, cw8, nct)
        yf, yb, hb = _scan_fwd(p, pm, pmt, b4m, a_s, c2m, Lc)
        h1, mix, m, y = _post_mix(o, cv, yf, yb, p, h, modtok, ng, dsk, wg, bg, Wl["w_out"], nct)
        h2, a2, z, f = _mlp_fwd(h1, modtok, ng, Wl["w_mlp_in"], Wl["w_mlp_out"], nct)
        saved.append(dict(Wl=Wl, h=h, modtok=modtok, ng=ng, cw8=cw8, a_s=a_s, b4m=b4m, c2m=c2m, wg=wg, dsk=dsk, bg=bg,
                          p=p, a=a, q4=q4, kr=kr, hb=hb, mix=mix, m=m, y=y, h1=h1, a2=a2, z=z, f=f,
                          prep_vjp=prep_vjp))
        h = h2

    dh, lpart = _loss(h, loss_target, Lc)
    loss_local = 0.5 / D * jnp.sum(lpart[:, :, 0, 0])

    G = {k: [None] * NLAYER for k in _SMALL if k != "c_ctx"}
    dcact = jnp.zeros((8, D), F32)
    T = B * S
    for l in reversed(range(NLAYER)):
        sv = saved[l]
        modtok, ng, Wl, GB = sv["modtok"], sv["ng"], sv["Wl"], {}
        dh1, df, dz, vec_m = _mlp_bwd(dh, sv["h1"], sv["f"], sv["z"], modtok, ng, Wl["w_mlp_in"],
                                      Wl["w_mlp_out"], nct)
        GB["w_mlp_out"] = _mm_tn(sv["z"].reshape(T, DFF), df.reshape(T, D), "dw_mlp_out", relu2=True)
        GB["w_mlp_in"] = _mm_tn(sv["a2"].reshape(T, D), dz.reshape(T, DFF), "dw_mlp_in")
        dm, dattn, dcv, dy, du_skip, vec_p, vec_p2, dwg = _post_bwd(
            dh1, sv["m"], sv["y"], sv["p"], modtok, ng, sv["dsk"], sv["wg"], sv["bg"], Wl["w_out"], nct)
        GB["w_out"] = _mm_tn(sv["mix"].reshape(T, D), dm.reshape(T, D), "dw_out")
        G["w_glu"][l] = dwg
        G["ssm_d"][l] = jnp.sum(vec_p2[:, :, 0, :], axis=(0, 1))
        G["b_glu"][l] = jnp.sum(vec_p2[:, :, 1, :], axis=(0, 1))
        duf, dub, db4, da_s, dc2 = _scan_bwd(sv["p"], dy, sv["hb"], pm, pmt, sv["b4m"], sv["a_s"], sv["c2m"], Lc)
        du = du_skip + duf + dub
        gs = sv["prep_vjp"]((da_s, db4, dc2))
        for k, name in enumerate(("ssm_lam_re", "ssm_lam_im", "ssm_log_dt", "ssm_b_re", "ssm_b_im", "ssm_c_re",
                                  "ssm_c_im")):
            G[name][l] = gs[k]
        dconv, dcw = _conv_bwd(sv["p"], dcv, sv["cw8"], nct)
        G["conv_w"][l] = jnp.sum(dcw[:, :, 0:3, :], axis=(0, 1))
        dq_r, dk_r, dv, dsink = ex.attn_bwd(l, P["attn_sink"][l], sv["q4"], sv["kr"], sv["p"], dattn, Lc)
        G["attn_sink"][l] = jnp.sum(dsink[:, :, 0], axis=0)
        dqk = _rope_bwd(dq_r, dk_r, cos, -sin)
        dp = jnp.concatenate([dqk, dv.astype(MXU), dconv.astype(MXU), du.astype(MXU)], axis=-1)
        GB["w_in"] = _mm_tn(sv["a"].reshape(T, D), dp.reshape(T, INW), "dw_in")
        dh, vec_i = _pre_bwd(dp, dh1, sv["h"], modtok, ng, Wl["w_in"], nct)
        G["norm_g"][l] = jnp.stack([jnp.sum(vec_i[:, :, 2], axis=(0, 1)), jnp.sum(vec_p[:, :, 1], axis=(0, 1)),
                                    jnp.sum(vec_m[:, :, 2], axis=(0, 1)), jnp.sum(vec_m[:, :, 4], axis=(0, 1))])
        dmod_blk = jnp.stack([vec_i[:, :, 0], vec_i[:, :, 1], vec_p[:, :, 0],
                              vec_m[:, :, 0], vec_m[:, :, 1], vec_m[:, :, 3]], axis=2)
        dmod_lat = jnp.sum(dmod_blk[:, nct:], axis=1).reshape(B, NMOD * D)
        dmod_ctx = jnp.sum(dmod_blk[:, :nct], axis=(0, 1)).reshape(1, NMOD * D)
        dmod8 = _pad_rows(jnp.concatenate([dmod_lat, dmod_ctx], axis=0))
        G["b_ada"][l] = jnp.sum(dmod8, axis=0)
        dmod8m = dmod8.astype(MXU)
        GB["w_ada"] = _mm_tn(cact8.astype(MXU), dmod8m, "dw_ada")
        dcact = dcact + _ada_bwd_c(dmod8m, Wl["w_ada"])
        ex.submit(l, GB)

    grads = {k: jnp.stack(v) for k, v in G.items()}
    grads["c_ctx"] = silu_vjp(dcact[B])[0]
    return loss_local, dh[:, Lc:, :], grads


_WEIGHTS = ["c_ctx", "w_ada", "b_ada", "norm_g", "w_in", "conv_w", "attn_sink", "ssm_lam_re", "ssm_lam_im",
            "ssm_log_dt", "ssm_b_re", "ssm_b_im", "ssm_c_re", "ssm_c_im", "ssm_d", "w_glu", "b_glu", "w_out",
            "w_mlp_in", "w_mlp_out"]
_SMALL = [n for n in _WEIGHTS if n not in _BIG]
_SMALL_SHARDED = {"norm_g": 2, "conv_w": 2, "w_glu": 1}
_LANES = 1024


class _Exchange:
    def __init__(self, W, q_me):
        self.W, self.q_me = W, q_me
        self.shapes = {n: W[n].shape[1:] for n in _BIG}
        self.rows = {n: math.prod(self.shapes[n]) // _LANES for n in _BIG}
        self.off = {n: sum(self.rows[m] for m in _BIG[:k]) for k, n in enumerate(_BIG)}
        self.wall = {0: _gather_chips(self._shard(0), "gather_big")}
        self.own, self.recv, self.pending = {}, {}, None

    def _shard(self, l):
        return jnp.concatenate([self.W[n][l].reshape(-1, _LANES) for n in _BIG], axis=0).astype(MXU)

    def weights(self, l):
        out = {}
        for n in _BIG:
            s = self.wall[l][:, self.off[n]:self.off[n] + self.rows[n]].reshape((NCHIP, 1) + self.shapes[n])
            out[n] = _join_shards(s, _BIG_AXIS[n])[0]
        return out

    def attn_fwd(self, l, *args):
        if l + 1 == NLAYER:
            return _attn_fwd(*args)
        o, self.wall[l + 1] = _attn_fwd(*args, gather=self._shard(l + 1))
        return o

    def submit(self, l, G):
        send = jnp.concatenate([_split_shards(G[n][None], _BIG_AXIS[n]).reshape(NCHIP, -1, _LANES) for n in _BIG],
                               axis=1)
        self.own[l] = lax.dynamic_index_in_dim(send, self.q_me, axis=0, keepdims=False)
        self.pending = (l, send.astype(MXU))

    def attn_bwd(self, l, *args):
        if self.pending is None:
            return _attn_bwd(*args)
        lp, send = self.pending
        self.pending = None
        *res, self.recv[lp] = _attn_bwd(*args, scatter=send)
        return res

    def finish(self):
        lp, send = self.pending
        self.recv[lp] = _scatter_chips(send, "scatter_big")
        parts = [_sum_slots(self.own[l], self.recv[l], "sum_chips") for l in range(NLAYER)]
        part = jnp.concatenate([parts[l][self.off[n]:self.off[n] + self.rows[n]]
                                for n in _BIG for l in range(NLAYER)], axis=0)
        return part, _swap_cores(part, "swap_big")


def _flat_pad(parts):
    v = jnp.concatenate([p.reshape(-1) for p in parts])
    rows = -(-v.shape[0] // (_PAD_ROWS * _LANES)) * _PAD_ROWS
    return jnp.concatenate([v, jnp.zeros((rows * _LANES - v.shape[0],), v.dtype)]).reshape(rows, _LANES)


def _unflat(flat, shapes):
    v = flat.reshape(-1)
    out, off = [], 0
    for s in shapes:
        n = math.prod(s)
        out.append(v[off:off + n].reshape(s))
        off += n
    return out


def kernel(x, c, ctx, c_ctx, w_ada, b_ada, norm_g, w_in, conv_w, attn_sink, ssm_lam_re, ssm_lam_im, ssm_log_dt, ssm_b_re, ssm_b_im, ssm_c_re, ssm_c_im, ssm_d, w_glu, b_glu, w_out, w_mlp_in, w_mlp_out, loss_target, m_c_ctx, m_w_ada, m_b_ada, m_norm_g, m_w_in, m_conv_w, m_attn_sink, m_ssm_lam_re, m_ssm_lam_im, m_ssm_log_dt, m_ssm_b_re, m_ssm_b_im, m_ssm_c_re, m_ssm_c_im, m_ssm_d, m_w_glu, m_b_glu, m_w_out, m_w_mlp_in, m_w_mlp_out, v_c_ctx, v_w_ada, v_b_ada, v_norm_g, v_w_in, v_conv_w, v_attn_sink, v_ssm_lam_re, v_ssm_lam_im, v_ssm_log_dt, v_ssm_b_re, v_ssm_b_im, v_ssm_c_re, v_ssm_c_im, v_ssm_d, v_w_glu, v_b_glu, v_w_out, v_w_mlp_in, v_w_mlp_out):
    W = dict(c_ctx=c_ctx, w_ada=w_ada, b_ada=b_ada, norm_g=norm_g, w_in=w_in, conv_w=conv_w, attn_sink=attn_sink,
             ssm_lam_re=ssm_lam_re, ssm_lam_im=ssm_lam_im, ssm_log_dt=ssm_log_dt, ssm_b_re=ssm_b_re,
             ssm_b_im=ssm_b_im, ssm_c_re=ssm_c_re, ssm_c_im=ssm_c_im, ssm_d=ssm_d, w_glu=w_glu, b_glu=b_glu,
             w_out=w_out, w_mlp_in=w_mlp_in, w_mlp_out=w_mlp_out)
    M = dict(c_ctx=m_c_ctx, w_ada=m_w_ada, b_ada=m_b_ada, norm_g=m_norm_g, w_in=m_w_in, conv_w=m_conv_w,
             attn_sink=m_attn_sink, ssm_lam_re=m_ssm_lam_re, ssm_lam_im=m_ssm_lam_im, ssm_log_dt=m_ssm_log_dt,
             ssm_b_re=m_ssm_b_re, ssm_b_im=m_ssm_b_im, ssm_c_re=m_ssm_c_re, ssm_c_im=m_ssm_c_im, ssm_d=m_ssm_d,
             w_glu=m_w_glu, b_glu=m_b_glu, w_out=m_w_out, w_mlp_in=m_w_mlp_in, w_mlp_out=m_w_mlp_out)
    V = dict(c_ctx=v_c_ctx, w_ada=v_w_ada, b_ada=v_b_ada, norm_g=v_norm_g, w_in=v_w_in, conv_w=v_conv_w,
             attn_sink=v_attn_sink, ssm_lam_re=v_ssm_lam_re, ssm_lam_im=v_ssm_lam_im, ssm_log_dt=v_ssm_log_dt,
             ssm_b_re=v_ssm_b_re, ssm_b_im=v_ssm_b_im, ssm_c_re=v_ssm_c_re, ssm_c_im=v_ssm_c_im, ssm_d=v_ssm_d,
             w_glu=v_w_glu, b_glu=v_b_glu, w_out=v_w_out, w_mlp_in=v_w_mlp_in, w_mlp_out=v_w_mlp_out)
    q_me = 2 * lax.axis_index("x") + lax.axis_index("y")

    ex = _Exchange(W, q_me)
    P = {n: W[n] for n in _SMALL}
    ssh_names = list(_SMALL_SHARDED)
    ssh = _flat_pad([W[n] for n in ssh_names])
    sall = _gather_chips(ssh, "gather_small")
    parts = [_unflat(sall[q], [W[n].shape for n in ssh_names]) for q in range(NCHIP)]
    for k, n in enumerate(ssh_names):
        P[n] = _join_shards(jnp.stack([parts[q][k] for q in range(NCHIP)]), _SMALL_SHARDED[n])

    loss_local, grad_x, G = _local_step(x, c, ctx, loss_target, P, ex)
    loss = lax.psum(loss_local, ("x", "y", "c"))

    part, sib = ex.finish()
    big_rows = [W[n].size // _LANES for n in _BIG]
    big_off = [sum(big_rows[:k]) for k in range(len(_BIG))]

    out = {}
    for n, r0, nr in zip(_BIG, big_off, big_rows):
        res = _adamw(W[n].reshape(nr, _LANES), part, sib, M[n].reshape(nr, _LANES), V[n].reshape(nr, _LANES),
                     "adamw_" + n, row0=r0)
        out[n] = [t.reshape(W[n].shape) for t in res]

    gsmall = _flat_pad([G[n] for n in _SMALL])
    gsum = _sum_slots(None, _gather_all(gsmall, "gather_grads"), "sum_devices")
    gfull = dict(zip(_SMALL, _unflat(gsum, [G[n].shape for n in _SMALL])))
    for n, ax in _SMALL_SHARDED.items():
        width = W[n].shape[ax]
        gfull[n] = lax.dynamic_slice_in_dim(gfull[n], q_me * width, width, axis=ax)
    shapes = [W[n].shape for n in _SMALL]
    gflat = _flat_pad([gfull[n] for n in _SMALL])
    res = _adamw(_flat_pad([W[n] for n in _SMALL]), gflat, jnp.zeros_like(gflat),
                 _flat_pad([M[n] for n in _SMALL]), _flat_pad([V[n] for n in _SMALL]), "adamw_small")
    for k, t in enumerate(res):
        for n, piece in zip(_SMALL, _unflat(t, shapes)):
            out.setdefault(n, [None] * 4)[k] = piece

    return (loss, grad_x, *[out[n][0] for n in _WEIGHTS], *[out[n][1] for n in _WEIGHTS],
            *[out[n][2] for n in _WEIGHTS], *[out[n][3] for n in _WEIGHTS])
```

```python
import functools
import math

import jax
import jax.numpy as jnp
import numpy as np
from jax import lax
from jax.experimental import pallas as pl
from jax.experimental.pallas import tpu as pltpu

F32 = jnp.float32
MXU = jnp.bfloat16

D = 1024
DFF = 4096
NMOD = 6
EPS = 1e-6
HD = 64
NQ = 8
NKV = 2
QPK = 4
AW = 512
KVW = 128
WIN = 128
QB = 128
CW = 256
SW = 256
SGRP = 16
SCH = 16
SST = 64
INW = 1792
GRID_W = 64
ROPE_BASE = 10000.0
NEG = -1e30
SCALE = HD ** -0.5
NLAYER = 4

TB = 256
TF = 1024
TC = 32
NCHIP = 4

LR, B1, B2, AEPS, WD, STEP = 0.001, 0.9, 0.999, 1e-08, 0.01, 10

MESH = pl.DeviceIdType.MESH
SDS = jax.ShapeDtypeStruct
BS = pl.BlockSpec
ANY = pl.BlockSpec(memory_space=pl.ANY)


def _cp(sem, vmem_mb=None):
    kw = dict(dimension_semantics=sem)
    if vmem_mb is not None:
        kw["vmem_limit_bytes"] = vmem_mb * 1024 * 1024
    return pltpu.CompilerParams(**kw)


def _dot(a, b):
    return jnp.dot(a, b, preferred_element_type=F32)


def _dot_nt(a, b):
    return lax.dot_general(a, b, (((1,), (1,)), ((), ())), preferred_element_type=F32)


def _dot_tn(a, b):
    return lax.dot_general(a, b, (((0,), (0,)), ((), ())), preferred_element_type=F32)


def _rms(x):
    r = lax.rsqrt(jnp.mean(x * x, axis=-1, keepdims=True) + EPS)
    return x * r, r


def _rms_bwd(dn, n, r):
    return r * (dn - n * jnp.mean(dn * n, axis=-1, keepdims=True))


_GC = math.sqrt(2.0 / math.pi)


def _gelu(y):
    return 0.5 * y * (1.0 + jnp.tanh(_GC * (y + 0.044715 * y * y * y)))


def _gelu_grad(y):
    th = jnp.tanh(_GC * (y + 0.044715 * y * y * y))
    return 0.5 * (1.0 + th) + 0.5 * y * (1.0 - th * th) * _GC * (1.0 + 3 * 0.044715 * y * y)


def _sigmoid(v):
    return 1.0 / (1.0 + jnp.exp(-v))


def _colsum(x):
    return jnp.sum(x, axis=0, keepdims=True)


def _tok(width, col=0):
    return BS((None, TB, width), lambda b, j: (b, j, col))


def _mod_spec(nct):
    return BS((None, None, 8, D), lambda b, j: (b, jnp.where(j >= nct, 1, 0), 0, 0))


def _full(shape):
    nd = len(shape)
    return BS(shape, lambda *a: (0,) * nd)


def _resident(shape):
    nd = len(shape)
    return BS(shape, lambda *a: (0,) * nd, pipeline_mode=pl.Buffered(1))


def _ada_fwd(cact8, w4, b):
    wq = w4.shape[2]

    def body(c_ref, w_ref, b_ref, o_ref):
        o_ref[...] = _dot(c_ref[...].astype(MXU), w_ref[...]) + b_ref[...]

    return pl.pallas_call(
        body, name="ada_fwd", grid=(NCHIP,),
        in_specs=[BS((8, D), lambda j: (0, 0)), BS((None, D, wq), lambda j: (j, 0, 0)), BS((1, wq), lambda j: (0, j))],
        out_specs=BS((8, wq), lambda j: (0, j)),
        out_shape=SDS((8, NMOD * D), F32), compiler_params=_cp(("arbitrary",)),
    )(cact8, w4, b)


def _ada_bwd_c(dmod8, w4):
    wq = w4.shape[2]

    def body(d_ref, w_ref, o_ref):
        @pl.when(pl.program_id(0) == 0)
        def _():
            o_ref[...] = jnp.zeros_like(o_ref)
        o_ref[...] += _dot_nt(d_ref[...], w_ref[...])

    return pl.pallas_call(
        body, name="ada_bwd_c", grid=(NCHIP,),
        in_specs=[BS((8, wq), lambda j: (0, j)), BS((None, D, wq), lambda j: (j, 0, 0))],
        out_specs=BS((8, D), lambda j: (0, 0)),
        out_shape=SDS((8, D), F32), compiler_params=_cp(("arbitrary",)),
    )(dmod8, w4)


def _pre_mix(h, modtok, ng, w_in, nct):
    B, S, _ = h.shape

    def body(h_ref, mod_ref, g_ref, w_ref, p_ref, a_ref):
        n, _ = _rms(h_ref[...])
        a = ((n * g_ref[0:1, :]) * (1.0 + mod_ref[1:2, :]) + mod_ref[0:1, :]).astype(MXU)
        a_ref[...] = a
        p_ref[...] = _dot(a, w_ref[...])

    return pl.pallas_call(
        body, name="pre_mix", grid=(B, S // TB),
        in_specs=[_tok(D), _mod_spec(nct), _full((8, D)), _full((D, INW))],
        out_specs=[_tok(INW), _tok(D)],
        out_shape=[SDS((B, S, INW), F32), SDS((B, S, D), MXU)],
        compiler_params=_cp(("arbitrary", "arbitrary"), 40),
    )(h, modtok, ng, w_in)


def _pre_bwd(dp, dh_part, h, modtok, ng, w_in, nct):
    B, S, _ = h.shape

    def body(dp_ref, dhp_ref, h_ref, mod_ref, g_ref, w_ref, dh_ref, vec_ref):
        da = _dot_nt(dp_ref[...], w_ref[...])
        n, r = _rms(h_ref[...])
        g = g_ref[0:1, :]
        sc1 = 1.0 + mod_ref[1:2, :]
        vec_ref[...] = jnp.zeros_like(vec_ref)
        vec_ref[0:1, :] = _colsum(da)
        vec_ref[1:2, :] = _colsum(da * (n * g))
        vec_ref[2:3, :] = _colsum(da * sc1 * n)
        dh_ref[...] = dhp_ref[...] + _rms_bwd(da * sc1 * g, n, r)

    return pl.pallas_call(
        body, name="pre_bwd", grid=(B, S // TB),
        in_specs=[_tok(INW), _tok(D), _tok(D), _mod_spec(nct), _full((8, D)), _full((D, INW))],
        out_specs=[_tok(D), BS((None, None, 8, D), lambda b, j: (b, j, 0, 0))],
        out_shape=[SDS((B, S, D), F32), SDS((B, S // TB, 8, D), F32)],
        compiler_params=_cp(("arbitrary", "arbitrary"), 40),
    )(dp, dh_part, h, modtok, ng, w_in)


def _rotate(t, c, s):
    lane = lax.broadcasted_iota(jnp.int32, t.shape, 1)
    partner = jnp.where((lane % 32) < 16, pltpu.roll(t, 112, 1), pltpu.roll(t, 16, 1))
    return t * c + partner * s


def _to_groups(cols):
    lo = lax.broadcasted_iota(jnp.int32, cols[0].shape, 1) < HD
    out = []
    for g in range(QPK):
        a, b = cols[g // 2], cols[2 + g // 2]
        out.append(jnp.where(lo, a, pltpu.roll(b, HD, 1)) if g % 2 == 0 else jnp.where(lo, pltpu.roll(a, HD, 1), b))
    return out


def _from_groups(tiles):
    lo = lax.broadcasted_iota(jnp.int32, tiles[0].shape, 1) < HD
    out = []
    for k in range(NQ // 2):
        a, b = tiles[(2 * k) % QPK], tiles[(2 * k) % QPK + 1]
        out.append(jnp.where(lo, a, pltpu.roll(b, HD, 1)) if k < 2 else jnp.where(lo, pltpu.roll(a, HD, 1), b))
    return out


def _grp(width=128):
    return BS((None, QPK, TB, width), lambda b, j: (b, 0, j, 0))


_TAB = BS((TB, 128), lambda b, j: (j, 0))


def _rope_fwd(p, cos, sin):
    B, S, _ = p.shape

    def body(x_ref, c_ref, s_ref, q_ref, k_ref):
        c, s = c_ref[...], s_ref[...]
        rot = [_rotate(x_ref[:, k * 128:(k + 1) * 128], c, s) for k in range((AW + KVW) // 128)]
        k_ref[...] = rot[AW // 128].astype(MXU)
        for g, t in enumerate(_to_groups(rot[:AW // 128])):
            q_ref[g] = (t * SCALE).astype(MXU)

    return pl.pallas_call(
        body, name="rope_fwd", grid=(B, S // TB),
        in_specs=[_tok(AW + KVW), _TAB, _TAB],
        out_specs=[_grp(), _tok(KVW)],
        out_shape=[SDS((B, QPK, S, 128), MXU), SDS((B, S, KVW), MXU)],
        compiler_params=_cp(("arbitrary", "arbitrary")),
    )(p, cos, sin)


def _rope_bwd(dq4, dk, cos, nsin):
    B, _, S, _ = dq4.shape

    def body(q_ref, k_ref, c_ref, s_ref, o_ref):
        c, s = c_ref[...], s_ref[...]
        cols = _from_groups([q_ref[g] * SCALE for g in range(QPK)]) + [k_ref[...]]
        for k, t in enumerate(cols):
            o_ref[:, k * 128:(k + 1) * 128] = _rotate(t, c, s).astype(MXU)

    return pl.pallas_call(
        body, name="rope_bwd", grid=(B, S // TB),
        in_specs=[_grp(), _tok(KVW), _TAB, _TAB],
        out_specs=_tok(AW + KVW),
        out_shape=SDS((B, S, AW + KVW), MXU),
        compiler_params=_cp(("arbitrary", "arbitrary")),
    )(dq4, dk, cos, nsin)


def _attn_masks(i, S, Lc):
    ncb = Lc // QB
    is_lat = i >= ncb
    w0 = pl.multiple_of(jnp.clip((i - 1) * QB, 0, S - 3 * QB), QB)
    shape = (QPK * QB, 3 * QB)
    qpos = i * QB + jnp.bitwise_and(lax.broadcasted_iota(jnp.int32, shape, 0), QB - 1)
    kpos = w0 + lax.broadcasted_iota(jnp.int32, shape, 1)
    mask = jnp.logical_and(jnp.logical_and(kpos >= Lc, jnp.abs(qpos - kpos) <= WIN), is_lat)
    return w0, mask


def _attn_probs(qh, kch, kwh, mask, sk):
    sc = _dot_nt(qh, kch)
    sw = jnp.where(mask, _dot_nt(qh, kwh), NEG)
    m = jnp.maximum(jnp.maximum(jnp.max(sc, axis=-1, keepdims=True), jnp.max(sw, axis=-1, keepdims=True)), sk)
    ec = jnp.exp(sc - m)
    ew = jnp.exp(sw - m)
    es = jnp.exp(sk - m)
    inv = 1.0 / (jnp.sum(ec, axis=-1, keepdims=True) + jnp.sum(ew, axis=-1, keepdims=True) + es)
    return ec * inv, ew * inv, es * inv


def _sink_col(sink_ref, hk):
    return jnp.concatenate([jnp.full((QB, 1), sink_ref[hk * QPK + g], F32) for g in range(QPK)], axis=0)


_QGRP = BS((None, QPK, QB, 128), lambda b, i: (b, 0, i, 0))


def _first_last(grid):
    first = last = None
    for ax, n in enumerate(grid):
        i = pl.program_id(ax)
        first = (i == 0) if first is None else jnp.logical_and(first, i == 0)
        last = (i == n - 1) if last is None else jnp.logical_and(last, i == n - 1)
    return first, last


def _call_with(ex, body, name, grid, in_specs, out_specs, out_shape, args, scratch, params):
    n_in, n_out = len(in_specs), len(out_specs)
    if ex is None:
        res = pl.pallas_call(body, name=name, grid=grid, in_specs=in_specs, out_specs=out_specs,
                             out_shape=out_shape, scratch_shapes=scratch, compiler_params=params)(*args)
        return list(res), []

    def carrying(*refs):
        own, comm = _carried(ex, refs, n_in, n_out)
        first, last = _first_last(grid)
        pl.when(first)(lambda: ex.start(*comm))
        body(*own)
        pl.when(last)(lambda: ex.wait(*comm))

    res = pl.pallas_call(
        carrying, name=f"{name}_{type(ex).__name__.strip('_').lower()}", grid=grid,
        in_specs=in_specs + ex.specs, out_specs=out_specs + ex.specs, out_shape=out_shape + ex.out_shape,
        scratch_shapes=scratch + ex.sems, compiler_params=params)(*args, *ex.args)
    return list(res[:n_out]), list(res[n_out:])


def _attn_fwd(sink, q4, k, p, Lc, ex=None):
    B, S, _ = k.shape

    def body(sink_ref, q_ref, k_ref, v_ref, o_ref):
        w0, mask = _attn_masks(pl.program_id(1), S, Lc)
        kc = k_ref[0:Lc, :]
        vc = v_ref[0:Lc, :].astype(MXU)
        kw = k_ref[pl.ds(w0, 3 * QB), :]
        vw = v_ref[pl.ds(w0, 3 * QB), :].astype(MXU)
        q = q_ref[...].reshape(QPK * QB, 128)
        outs = []
        for hk in range(NKV):
            cs = slice(hk * HD, (hk + 1) * HD)
            pc, pw, _ = _attn_probs(q[:, cs], kc[:, cs], kw[:, cs], mask, _sink_col(sink_ref, hk))
            o = _dot(pc.astype(MXU), vc[:, cs]) + _dot(pw.astype(MXU), vw[:, cs])
            outs += [o[g * QB:(g + 1) * QB] for g in range(QPK)]
        o_ref[...] = jnp.concatenate(outs, axis=1).astype(o_ref.dtype)

    return _call_with(
        ex, body, "attn_fwd", (B, S // QB),
        [BS(memory_space=pltpu.SMEM), _QGRP, BS((None, S, KVW), lambda b, i: (b, 0, 0)),
         BS((None, S, KVW), lambda b, i: (b, 0, (AW + KVW) // KVW))],
        [BS((None, QB, AW), lambda b, i: (b, i, 0))], [SDS((B, S, AW), MXU)], [sink, q4, k, p], [],
        _cp(("arbitrary", "arbitrary"), 40))


def _attn_bwd(sink, q4, k, p, do4, Lc):
    B, S, _ = k.shape

    def body(sink_ref, q_ref, k_ref, v_ref, do_ref, dq_ref, dk_ref, dv_ref, ds_ref):
        i = pl.program_id(1)

        @pl.when(i == 0)
        def _():
            dk_ref[...] = jnp.zeros_like(dk_ref)
            dv_ref[...] = jnp.zeros_like(dv_ref)
            ds_ref[...] = jnp.zeros_like(ds_ref)

        w0, mask = _attn_masks(i, S, Lc)
        kc = k_ref[0:Lc, :]
        vc = v_ref[0:Lc, :].astype(MXU)
        kw = k_ref[pl.ds(w0, 3 * QB), :]
        vw = v_ref[pl.ds(w0, 3 * QB), :].astype(MXU)
        q = q_ref[...].reshape(QPK * QB, 128)
        do = do_ref[...].reshape(QPK * QB, 128)
        dqs, dsinks, dkcs, dkws, dvcs, dvws = [], [], [], [], [], []
        for hk in range(NKV):
            cs = slice(hk * HD, (hk + 1) * HD)
            qh = q[:, cs]
            pc, pw, ps = _attn_probs(qh, kc[:, cs], kw[:, cs], mask, _sink_col(sink_ref, hk))
            dob = do[:, cs].astype(MXU)
            dpc = _dot_nt(dob, vc[:, cs])
            dpw = _dot_nt(dob, vw[:, cs])
            delta = jnp.sum(pc * dpc, axis=-1, keepdims=True) + jnp.sum(pw * dpw, axis=-1, keepdims=True)
            dsc = (pc * (dpc - delta)).astype(MXU)
            dsw = (pw * (dpw - delta)).astype(MXU)
            dqs.append(_dot(dsc, kc[:, cs]) + _dot(dsw, kw[:, cs]))
            dkcs.append(_dot_tn(dsc, qh))
            dkws.append(_dot_tn(dsw, qh))
            dvcs.append(_dot_tn(pc.astype(MXU), dob))
            dvws.append(_dot_tn(pw.astype(MXU), dob))
            psd = ps * delta
            dsinks += [jnp.broadcast_to(-jnp.sum(psd[g * QB:(g + 1) * QB], axis=0, keepdims=True), (1, 128))
                       for g in range(QPK)]
        for g in range(QPK):
            dq_ref[g] = jnp.concatenate([dq[g * QB:(g + 1) * QB] for dq in dqs], axis=1)
        ds_ref[...] += jnp.concatenate(dsinks, axis=0)
        dk_ref[0:Lc, :] += jnp.concatenate(dkcs, axis=1)
        dv_ref[0:Lc, :] += jnp.concatenate(dvcs, axis=1)
        dk_ref[pl.ds(w0, 3 * QB), :] += jnp.concatenate(dkws, axis=1)
        dv_ref[pl.ds(w0, 3 * QB), :] += jnp.concatenate(dvws, axis=1)

    kv_acc = BS((None, S, KVW), lambda b, i: (b, 0, 0))
    return pl.pallas_call(
        body, name="attn_bwd", grid=(B, S // QB),
        in_specs=[BS(memory_space=pltpu.SMEM), _QGRP, BS((None, S, KVW), lambda b, i: (b, 0, 0)),
                  BS((None, S, KVW), lambda b, i: (b, 0, (AW + KVW) // KVW)), _QGRP],
        out_specs=[_QGRP, kv_acc, kv_acc, BS((None, 8, 128), lambda b, i: (b, 0, 0))],
        out_shape=[SDS((B, QPK, S, 128), F32), SDS((B, S, KVW), F32), SDS((B, S, KVW), F32), SDS((B, 8, 128), F32)],
        compiler_params=_cp(("arbitrary", "arbitrary"), 48),
    )(sink, q4, k, p, do4)


def _halo_specs(width, col, S):
    nb8 = S // 8
    per = TB // 8
    prev = BS((None, 8, width), lambda b, j: (b, jnp.maximum(j * per - 1, 0), col))
    nxt = BS((None, 8, width), lambda b, j: (b, jnp.minimum((j + 1) * per, nb8 - 1), col))
    return prev, nxt


def _shift_dn(z, prev_row, at_start):
    row = lax.broadcasted_iota(jnp.int32, z.shape, 0)
    first = jnp.where(at_start, 0.0, 1.0) * prev_row
    return jnp.where(row == 0, first, pltpu.roll(z, 1, 0))


def _shift_up(z, next_row, at_end):
    row = lax.broadcasted_iota(jnp.int32, z.shape, 0)
    last = jnp.where(at_end, 0.0, 1.0) * next_row
    return jnp.where(row == z.shape[0] - 1, last, pltpu.roll(z, z.shape[0] - 1, 0))


def _conv_fwd(p, cw8, nct):
    B, S, _ = p.shape
    nt = S // TB

    def body(p_ref, pp_ref, pn_ref, w_ref, o_ref):
        j = pl.program_id(1)
        at_start = jnp.logical_or(j == 0, j == nct)
        at_end = jnp.logical_or(j == nct - 1, j == nt - 1)
        z = p_ref[:, 256:512] * p_ref[:, 512:768]
        zprev = pp_ref[7:8, 256:512] * pp_ref[7:8, 512:768]
        znext = pn_ref[0:1, 256:512] * pn_ref[0:1, 512:768]
        c3 = (_shift_dn(z, zprev, at_start) * w_ref[0:1, :] + z * w_ref[1:2, :]
              + _shift_up(z, znext, at_end) * w_ref[2:3, :])
        o_ref[...] = p_ref[:, 0:256] * c3

    prev, nxt = _halo_specs(3 * CW, 1, S)
    return pl.pallas_call(
        body, name="conv_fwd", grid=(B, nt),
        in_specs=[_tok(3 * CW, 1), prev, nxt, _full((8, CW))],
        out_specs=_tok(CW),
        out_shape=SDS((B, S, CW), F32),
        compiler_params=_cp(("arbitrary", "arbitrary")),
    )(p, p, p, cw8)


def _conv_bwd(p, dcv, cw8, nct):
    B, S, _ = p.shape
    nt = S // TB

    def body(p_ref, pp_ref, pn_ref, d_ref, dp_ref, dn_ref, w_ref, o_ref, dw_ref):
        j = pl.program_id(1)
        at_start = jnp.logical_or(j == 0, j == nct)
        at_end = jnp.logical_or(j == nct - 1, j == nt - 1)
        cb, cc, cx = p_ref[:, 0:256], p_ref[:, 256:512], p_ref[:, 512:768]
        z = cc * cx
        zm = _shift_dn(z, pp_ref[7:8, 256:512] * pp_ref[7:8, 512:768], at_start)
        zp = _shift_up(z, pn_ref[0:1, 256:512] * pn_ref[0:1, 512:768], at_end)
        d = d_ref[...]
        e = d * cb
        em = _shift_dn(e, dp_ref[7:8, :] * pp_ref[7:8, 0:256], at_start)
        ep = _shift_up(e, dn_ref[0:1, :] * pn_ref[0:1, 0:256], at_end)
        dz = e * w_ref[1:2, :] + ep * w_ref[0:1, :] + em * w_ref[2:3, :]
        o_ref[:, 0:256] = d * (zm * w_ref[0:1, :] + z * w_ref[1:2, :] + zp * w_ref[2:3, :])
        o_ref[:, 256:512] = dz * cx
        o_ref[:, 512:768] = dz * cc
        dw_ref[...] = jnp.zeros_like(dw_ref)
        dw_ref[0:1, :] = _colsum(e * zm)
        dw_ref[1:2, :] = _colsum(e * z)
        dw_ref[2:3, :] = _colsum(e * zp)

    prev, nxt = _halo_specs(3 * CW, 1, S)
    dprev, dnxt = _halo_specs(CW, 0, S)
    return pl.pallas_call(
        body, name="conv_bwd", grid=(B, nt),
        in_specs=[_tok(3 * CW, 1), prev, nxt, _tok(CW), dprev, dnxt, _full((8, CW))],
        out_specs=[_tok(3 * CW), BS((None, None, 8, CW), lambda b, j: (b, j, 0, 0))],
        out_shape=[SDS((B, S, 3 * CW), F32), SDS((B, nt, 8, CW), F32)],
        compiler_params=_cp(("arbitrary", "arbitrary")),
    )(p, p, p, dcv, dcv, dcv, cw8)


HW = SGRP * SST


def _rev_chunk(k, nk, ncc):
    return jnp.where(k < ncc, ncc - 1 - k, nk - 1 - k + ncc)


def _scan_perm(B):
    n = B * TC
    pm = np.zeros((2 * n, 2 * n), np.float32)
    for t in range(TC):
        for e in range(B):
            pm[t * 2 * B + e, e * TC + t] = 1.0
            pm[t * 2 * B + B + e, n + e * TC + (TC - 1 - t)] = 1.0
    return pm


def _scan_drive(uf_ref, ub_ref, pm_ref, b4_ref, fwd, dbuf, n):
    u_tok = jnp.concatenate([uf_ref[...].reshape(n, SW), ub_ref[...].reshape(n, SW)], axis=0).astype(MXU)
    u_tm = _dot(pm_ref[...], u_tok).astype(MXU)
    d4 = _dot(u_tm, b4_ref[...])
    dbuf[...] = jnp.where(fwd, d4[:, :2 * HW], d4[:, 2 * HW:])
    return u_tm


def _scan_fwd(p, pm, pmt, b4, a, c2, Lc):
    B, S, _ = p.shape
    nch, n = 2 * B, B * TC
    nk, ncc = S // TC, Lc // TC
    ucol = (INW - SW) // SW

    def body(uf_ref, ub_ref, pm_ref, pmt_ref, b4_ref, a_ref, c2_ref, yf_ref, yb_ref, hb_ref, hcar, dbuf, hbuf):
        @pl.when(pl.program_id(0) == 0)
        def _():
            hcar[...] = jnp.zeros_like(hcar)

        fwd = (lax.broadcasted_iota(jnp.int32, (2 * n, 1), 0) % nch) < B
        _scan_drive(uf_ref, ub_ref, pm_ref, b4_ref, fwd, dbuf, n)
        hb_ref[...] = hcar[...]
        ar, ai = a_ref[:, :HW], a_ref[:, HW:]

        def step(t, h):
            hr, hi = h
            r0 = pl.multiple_of(t * nch, nch)
            nr = ar * hr - ai * hi + dbuf[pl.ds(r0, nch), :HW]
            ni = ar * hi + ai * hr + dbuf[pl.ds(r0, nch), HW:]
            hbuf[pl.ds(r0, nch), :HW] = nr
            hbuf[pl.ds(r0, nch), HW:] = ni
            return nr, ni

        hr, hi = lax.fori_loop(0, TC, step, (hcar[:, :HW], hcar[:, HW:]))
        hcar[:, :HW] = hr
        hcar[:, HW:] = hi
        h_tok = _dot(pmt_ref[...], hbuf[...].astype(MXU)).astype(MXU)
        yf_ref[...] = _dot(h_tok[:n], c2_ref[:, :SW]).reshape(B, TC, SW)
        yb_ref[...] = _dot(h_tok[n:], c2_ref[:, SW:]).reshape(B, TC, SW)

    return pl.pallas_call(
        body, name="scan_fwd", grid=(nk,),
        in_specs=[BS((B, TC, SW), lambda k: (0, k, ucol)),
                  BS((B, TC, SW), lambda k: (0, _rev_chunk(k, nk, ncc), ucol)),
                  _full((2 * n, 2 * n)), _full((2 * n, 2 * n)),
                  _full((SW, 4 * HW)), _full((nch, 2 * HW)), _full((2 * HW, 2 * SW))],
        out_specs=[BS((B, TC, SW), lambda k: (0, k, 0)),
                   BS((B, TC, SW), lambda k: (0, _rev_chunk(k, nk, ncc), 0)),
                   BS((None, nch, 2 * HW), lambda k: (k, 0, 0))],
        out_shape=[SDS((B, S, SW), F32), SDS((B, S, SW), F32), SDS((nk, nch, 2 * HW), F32)],
        scratch_shapes=[pltpu.VMEM((nch, 2 * HW), F32), pltpu.VMEM((2 * n, 2 * HW), F32),
                        pltpu.VMEM((2 * n, 2 * HW), F32)],
        compiler_params=_cp(("arbitrary",), 48),
    )(p, p, pm, pmt, b4, a, c2)


def _scan_bwd(p, dy, hb, pm, pmt, b4, a, c2, Lc, ex=None):
    B, S, _ = p.shape
    nch, n = 2 * B, B * TC
    nk, ncc = S // TC, Lc // TC
    ucol = (INW - SW) // SW
    rows = 2 * n

    def body(uf_ref, ub_ref, dyf_ref, dyb_ref, hb_ref, pm_ref, pmt_ref, b4_ref, a_ref, c2_ref,
             duf_ref, dub_ref, db4_ref, da_ref, dc2_ref, gcar, dbuf, hbuf, gbuf):
        @pl.when(pl.program_id(0) == 0)
        def _():
            gcar[...] = jnp.zeros_like(gcar)
            db4_ref[...] = jnp.zeros_like(db4_ref)
            da_ref[...] = jnp.zeros_like(da_ref)
            dc2_ref[...] = jnp.zeros_like(dc2_ref)

        fwd = (lax.broadcasted_iota(jnp.int32, (rows, 1), 0) % nch) < B
        u_tm = _scan_drive(uf_ref, ub_ref, pm_ref, b4_ref, fwd, dbuf, n)
        ar, ai = a_ref[:, :HW], a_ref[:, HW:]
        hbuf[0:nch, :] = hb_ref[...]

        def fstep(t, h):
            hr, hi = h
            r0 = pl.multiple_of(t * nch, nch)
            nr = ar * hr - ai * hi + dbuf[pl.ds(r0, nch), :HW]
            ni = ar * hi + ai * hr + dbuf[pl.ds(r0, nch), HW:]
            hbuf[pl.ds(r0 + nch, nch), :HW] = nr
            hbuf[pl.ds(r0 + nch, nch), HW:] = ni
            return nr, ni

        lax.fori_loop(0, TC, fstep, (hb_ref[:, :HW], hb_ref[:, HW:]))

        dy_tok = jnp.concatenate([dyf_ref[...].reshape(n, SW), dyb_ref[...].reshape(n, SW)], axis=0).astype(MXU)
        dy_tm = _dot(pm_ref[...], dy_tok)
        dy2 = jnp.concatenate([jnp.where(fwd, dy_tm, 0.0), jnp.where(fwd, 0.0, dy_tm)], axis=1).astype(MXU)
        gbuf[...] = _dot_nt(dy2, c2_ref[...])
        dc2_ref[...] += _dot_tn(hbuf[nch:, :].astype(MXU), dy2)

        def bstep(i, cary):
            cr, ci = cary
            r0 = pl.multiple_of((TC - 1 - i) * nch, nch)
            gr = gbuf[pl.ds(r0, nch), :HW] + cr
            gi = gbuf[pl.ds(r0, nch), HW:] + ci
            gbuf[pl.ds(r0, nch), :HW] = gr
            gbuf[pl.ds(r0, nch), HW:] = gi
            return ar * gr + ai * gi, ar * gi - ai * gr

        cr, ci = lax.fori_loop(0, TC, bstep, (gcar[:, :HW], gcar[:, HW:]))
        gcar[:, :HW] = cr
        gcar[:, HW:] = ci

        gr = gbuf[:, :HW].reshape(TC, nch, HW)
        gi = gbuf[:, HW:].reshape(TC, nch, HW)
        hpr = hbuf[0:rows, :HW].reshape(TC, nch, HW)
        hpi = hbuf[0:rows, HW:].reshape(TC, nch, HW)
        da_ref[:, :HW] += jnp.sum(gr * hpr + gi * hpi, axis=0)
        da_ref[:, HW:] += jnp.sum(gi * hpr - gr * hpi, axis=0)

        g = gbuf[...]
        gm = g.astype(MXU)
        dd4 = jnp.concatenate([jnp.where(fwd, g, 0.0), jnp.where(fwd, 0.0, g)], axis=1).astype(MXU)
        db4_ref[...] += _dot_tn(u_tm, dd4)
        g_tok = _dot(pmt_ref[...], gm).astype(MXU)
        duf_ref[...] = _dot_nt(g_tok[:n], b4_ref[:, :2 * HW]).reshape(B, TC, SW)
        dub_ref[...] = _dot_nt(g_tok[n:], b4_ref[:, 2 * HW:]).reshape(B, TC, SW)

    fwd_blk = lambda col: BS((B, TC, SW), lambda k: (0, nk - 1 - k, col))
    bwd_blk = lambda col: BS((B, TC, SW), lambda k: (0, _rev_chunk(nk - 1 - k, nk, ncc), col))
    return _call_with(
        ex, body, "scan_bwd", (nk,),
        [fwd_blk(ucol), bwd_blk(ucol), fwd_blk(0), bwd_blk(0),
         BS((None, nch, 2 * HW), lambda k: (nk - 1 - k, 0, 0)),
         _full((rows, rows)), _full((rows, rows)),
         _full((SW, 4 * HW)), _full((nch, 2 * HW)), _full((2 * HW, 2 * SW))],
        [fwd_blk(0), bwd_blk(0), _full((SW, 4 * HW)), _full((nch, 2 * HW)), _full((2 * HW, 2 * SW))],
        [SDS((B, S, SW), F32), SDS((B, S, SW), F32), SDS((SW, 4 * HW), F32), SDS((nch, 2 * HW), F32),
         SDS((2 * HW, 2 * SW), F32)],
        [p, p, dy, dy, hb, pm, pmt, b4, a, c2],
        [pltpu.VMEM((nch, 2 * HW), F32), pltpu.VMEM((rows, 2 * HW), F32),
         pltpu.VMEM((rows + nch, 2 * HW), F32), pltpu.VMEM((rows, 2 * HW), F32)],
        _cp(("arbitrary",), 56))


def _post_mix(o, cv, yf, yb, p, h, modtok, ng, dsk, wg, bg, wo, nct):
    B, S, _ = h.shape

    def body(o_ref, cv_ref, yf_ref, yb_ref, u_ref, h_ref, mod_ref, g_ref, dsk_ref, wg_ref, bg_ref, wo_ref,
             h1_ref, mix_ref, m_ref, y_ref):
        y = yf_ref[...] + yb_ref[...] + dsk_ref[...] * u_ref[...]
        g = _gelu(y)
        s = g * _sigmoid(_dot(g.astype(MXU), wg_ref[...]) + bg_ref[...])
        mix_ref[:, 0:AW] = o_ref[...]
        mix_ref[:, AW:AW + CW] = cv_ref[...].astype(MXU)
        mix_ref[:, AW + CW:] = s.astype(MXU)
        m = _dot(mix_ref[...], wo_ref[...])
        n, _ = _rms(m)
        h1_ref[...] = h_ref[...] + mod_ref[2:3, :] * (n * g_ref[1:2, :])
        m_ref[...] = m
        y_ref[...] = y

    return pl.pallas_call(
        body, name="post_mix", grid=(B, S // TB),
        in_specs=[_tok(AW), _tok(CW), _tok(SW), _tok(SW), _tok(SW, (INW - SW) // SW), _tok(D), _mod_spec(nct),
                  _full((8, D)), _full((1, SW)), _full((SW, SW)), _full((1, SW)), _full((D, D))],
        out_specs=[_tok(D), _tok(D), _tok(D), _tok(SW)],
        out_shape=[SDS((B, S, D), F32), SDS((B, S, D), MXU), SDS((B, S, D), F32), SDS((B, S, SW), F32)],
        compiler_params=_cp(("arbitrary", "arbitrary"), 40),
    )(o, cv, yf, yb, p, h, modtok, ng, dsk, wg, bg, wo)


def _post_bwd(dh1, m, y, p, modtok, ng, dsk, wg, bg, wo, nct):
    B, S, _ = m.shape

    def body(dh_ref, m_ref, y_ref, u_ref, mod_ref, g_ref, dsk_ref, wg_ref, bg_ref, wo_ref,
             dm_ref, da_ref, dc_ref, dy_ref, du_ref, vec_ref, vec2_ref, dwg_ref):
        @pl.when(jnp.logical_and(pl.program_id(0) == 0, pl.program_id(1) == 0))
        def _():
            dwg_ref[...] = jnp.zeros_like(dwg_ref)

        n, r = _rms(m_ref[...])
        dh1 = dh_ref[...]
        gpm = g_ref[1:2, :]
        dr = dh1 * mod_ref[2:3, :]
        vec_ref[...] = jnp.zeros_like(vec_ref)
        vec_ref[0:1, :] = _colsum(dh1 * (n * gpm))
        vec_ref[1:2, :] = _colsum(dr * n)
        dm = _rms_bwd(dr * gpm, n, r).astype(MXU)
        dm_ref[...] = dm
        dmix = _dot_nt(dm, wo_ref[...])
        for g, t in enumerate(_to_groups([dmix[:, k * 128:(k + 1) * 128] for k in range(AW // 128)])):
            da_ref[g] = t
        dc_ref[...] = dmix[:, AW:AW + CW]
        ds = dmix[:, AW + CW:]
        yv = y_ref[...]
        g = _gelu(yv)
        gb = g.astype(MXU)
        sg = _sigmoid(_dot(gb, wg_ref[...]) + bg_ref[...])
        dv = ds * g * sg * (1.0 - sg)
        dvb = dv.astype(MXU)
        dg = ds * sg + _dot_nt(dvb, wg_ref[...])
        dwg_ref[...] += _dot_tn(gb, dvb)
        dy = dg * _gelu_grad(yv)
        dy_ref[...] = dy
        du_ref[...] = dy * dsk_ref[...]
        vec2_ref[...] = jnp.zeros_like(vec2_ref)
        vec2_ref[0:1, :] = _colsum(dy * u_ref[...])
        vec2_ref[1:2, :] = _colsum(dv)

    nt = S // TB
    return pl.pallas_call(
        body, name="post_bwd", grid=(B, nt),
        in_specs=[_tok(D), _tok(D), _tok(SW), _tok(SW, (INW - SW) // SW), _mod_spec(nct), _full((8, D)),
                  _full((1, SW)), _full((SW, SW)), _full((1, SW)), _full((D, D))],
        out_specs=[_tok(D), _grp(), _tok(CW), _tok(SW), _tok(SW),
                   BS((None, None, 8, D), lambda b, j: (b, j, 0, 0)),
                   BS((None, None, 8, SW), lambda b, j: (b, j, 0, 0)), _full((SW, SW))],
        out_shape=[SDS((B, S, D), MXU), SDS((B, QPK, S, 128), F32), SDS((B, S, CW), F32), SDS((B, S, SW), F32),
                   SDS((B, S, SW), F32), SDS((B, nt, 8, D), F32), SDS((B, nt, 8, SW), F32), SDS((SW, SW), F32)],
        compiler_params=_cp(("arbitrary", "arbitrary"), 40),
    )(dh1, m, y, p, modtok, ng, dsk, wg, bg, wo)


def _mlp_fwd(h1, modtok, ng, w1, w2, nct, ex=None):
    B, S, _ = h1.shape
    assert w1.shape == (DFF // TF, D, TF)

    def body(h_ref, mod_ref, g_ref, w1_ref, w2_ref, h2_ref, a2_ref, z_ref, f_ref):
        n, _ = _rms(h_ref[...])
        a2 = ((n * g_ref[2:3, :]) * (1.0 + mod_ref[4:5, :]) + mod_ref[3:4, :]).astype(MXU)
        a2_ref[...] = a2
        ff = jnp.zeros((TB, D), F32)
        for c in range(DFF // TF):
            cs = slice(c * TF, (c + 1) * TF)
            z = _dot(a2, w1_ref[c])
            z_ref[:, cs] = z.astype(MXU)
            r = jnp.maximum(z, 0.0)
            ff = ff + _dot((r * r).astype(MXU), w2_ref[cs, :])
        f_ref[...] = ff
        n, _ = _rms(ff)
        h2_ref[...] = h_ref[...] + mod_ref[5:6, :] * (n * g_ref[3:4, :])

    return _call_with(
        ex, body, "mlp_fwd", (B, S // TB),
        [_tok(D), _mod_spec(nct), _full((8, D)), _resident((DFF // TF, D, TF)), _resident((DFF, D))],
        [_tok(D), _tok(D), _tok(DFF), _tok(D)],
        [SDS((B, S, D), F32), SDS((B, S, D), MXU), SDS((B, S, DFF), MXU), SDS((B, S, D), F32)],
        [h1, modtok, ng, w1, w2], [], _cp(("arbitrary", "arbitrary"), 48))


def _mlp_bwd(dh2, h1, f, z, modtok, ng, w1, w2, nct, ex=None):
    B, S, _ = h1.shape

    def body(dh_ref, h_ref, f_ref, z_ref, mod_ref, g_ref, w1_ref, w2_ref, dh1_ref, df_ref, dz_ref, vec_ref):
        n, r = _rms(f_ref[...])
        dh2 = dh_ref[...]
        gp = g_ref[3:4, :]
        dr = dh2 * mod_ref[5:6, :]
        vec_ref[...] = jnp.zeros_like(vec_ref)
        vec_ref[3:4, :] = _colsum(dh2 * (n * gp))
        vec_ref[4:5, :] = _colsum(dr * n)
        df = _rms_bwd(dr * gp, n, r).astype(MXU)
        df_ref[...] = df
        da = jnp.zeros((TB, D), F32)
        for c in range(DFF // TF):
            cs = slice(c * TF, (c + 1) * TF)
            dr2 = _dot_nt(df, w2_ref[cs, :])
            dz = (dr2 * 2.0 * jnp.maximum(z_ref[:, cs].astype(F32), 0.0)).astype(MXU)
            dz_ref[:, cs] = dz
            da = da + _dot_nt(dz, w1_ref[c])
        n, r = _rms(h_ref[...])
        g = g_ref[2:3, :]
        sc1 = 1.0 + mod_ref[4:5, :]
        vec_ref[0:1, :] = _colsum(da)
        vec_ref[1:2, :] = _colsum(da * (n * g))
        vec_ref[2:3, :] = _colsum(da * sc1 * n)
        dh1_ref[...] = dh2 + _rms_bwd(da * sc1 * g, n, r)

    return _call_with(
        ex, body, "mlp_bwd", (B, S // TB),
        [_tok(D), _tok(D), _tok(D), _tok(DFF), _mod_spec(nct), _full((8, D)),
         _resident((DFF // TF, D, TF)), _resident((DFF, D))],
        [_tok(D), _tok(D), _tok(DFF), BS((None, None, 8, D), lambda b, j: (b, j, 0, 0))],
        [SDS((B, S, D), F32), SDS((B, S, D), MXU), SDS((B, S, DFF), MXU), SDS((B, S // TB, 8, D), F32)],
        [dh2, h1, f, z, modtok, ng, w1, w2], [], _cp(("arbitrary", "arbitrary"), 56))


def _loss(h, target, Lc):
    B, S, _ = h.shape
    nt, nct = S // TB, Lc // TB

    def body(h_ref, t_ref, dh_ref, l_ref):
        lat = pl.program_id(1) >= nct
        err = jnp.where(lat, h_ref[...] - t_ref[...], 0.0)
        dh_ref[...] = err * (1.0 / D)
        l_ref[...] = jnp.broadcast_to(jnp.sum(err * err, keepdims=True), (8, 128))

    return pl.pallas_call(
        body, name="loss", grid=(B, nt),
        in_specs=[_tok(D), BS((None, TB, D), lambda b, j: (b, jnp.maximum(j - nct, 0), 0))],
        out_specs=[_tok(D), BS((None, None, 8, 128), lambda b, j: (b, j, 0, 0))],
        out_shape=[SDS((B, S, D), F32), SDS((B, nt, 8, 128), F32)],
        compiler_params=_cp(("arbitrary", "arbitrary")),
    )(h, target)


def _mm_tn(a, b, name, relu2=False):
    T, M = a.shape
    N = b.shape[1]
    def blk(n):
        return max(b for b in range(128, 1025, 128) if n % b == 0)

    bm, bn, tk = blk(M), blk(N), min(T, 512)
    assert T % tk == 0
    nkk = T // tk

    def body(a_ref, b_ref, o_ref, om_ref):
        k = pl.program_id(2)

        @pl.when(k == 0)
        def _():
            o_ref[...] = jnp.zeros_like(o_ref)

        av = a_ref[...]
        if relu2:
            r = jnp.maximum(av.astype(F32), 0.0)
            av = (r * r).astype(MXU)
        o_ref[...] += _dot_tn(av, b_ref[...])

        @pl.when(k == nkk - 1)
        def _():
            om_ref[...] = o_ref[...].astype(MXU)

    out = BS((bm, bn), lambda i, j, k: (i, j))
    return pl.pallas_call(
        body, name=name, grid=(M // bm, N // bn, nkk),
        in_specs=[BS((tk, bm), lambda i, j, k: (k, i)), BS((tk, bn), lambda i, j, k: (k, j))],
        out_specs=[out, out], out_shape=[SDS((M, N), F32), SDS((M, N), MXU)],
        compiler_params=_cp(("arbitrary", "arbitrary", "arbitrary"), 40),
    )(a, b)


_PAD_ROWS = 64


def _block_rows(R):
    for br in (256, _PAD_ROWS):
        if R % br == 0:
            return br
    raise ValueError(f"row count {R} is not a multiple of {_PAD_ROWS}")


def _adamw(w, ga, gb, m, v, name):
    R, C = w.shape
    br = _block_rows(R) if C <= _LANES else _PAD_ROWS

    def body(w_ref, ga_ref, gb_ref, m_ref, v_ref, g_out, d_out, m_out, v_out):
        g = ga_ref[...] + gb_ref[...]
        m2 = B1 * m_ref[...] + (1.0 - B1) * g
        v2 = B2 * v_ref[...] + (1.0 - B2) * (g * g)
        m_hat = m2 / (1.0 - B1 ** STEP)
        v_hat = v2 / (1.0 - B2 ** STEP)
        g_out[...] = g
        d_out[...] = -LR * (m_hat / (jnp.sqrt(v_hat) + AEPS) + WD * w_ref[...])
        m_out[...] = m2
        v_out[...] = v2

    own = BS((br, C), lambda i: (i, 0))
    return pl.pallas_call(
        body, name=name, grid=(R // br,),
        in_specs=[own] * 5, out_specs=[own] * 4, out_shape=[SDS((R, C), F32)] * 4,
        compiler_params=_cp(("arbitrary",)),
    )(w, ga, gb, m, v)


def _my_xyc():
    return lax.axis_index("x"), lax.axis_index("y"), lax.axis_index("c")


def _chip_peers(x, y):
    return [(1 - x, y), (x, 1 - y), (1 - x, 1 - y)]


class _Gather:
    def __init__(self, shards):
        self.args = list(shards)
        self.n = len(self.args)
        self.specs = [ANY] * self.n
        self.out_shape = [SDS((NCHIP,) + a.shape, a.dtype) for a in self.args]
        self.sems = [pltpu.SemaphoreType.DMA((3 * self.n,)), pltpu.SemaphoreType.DMA((3 * self.n,)),
                     pltpu.SemaphoreType.DMA((self.n,))]

    def _copies(self, ins, outs, ssem, rsem, lsem):
        x, y, c = _my_xyc()
        mine, sends, recvs = [], [], []
        for k, (s_ref, o_ref) in enumerate(zip(ins, outs)):
            mine.append(pltpu.make_async_copy(s_ref, o_ref.at[2 * x + y], lsem.at[k]))
            for j, (px, py) in enumerate(_chip_peers(x, y)):
                sems = dict(send_sem=ssem.at[3 * k + j], recv_sem=rsem.at[3 * k + j], device_id=(px, py, c),
                            device_id_type=MESH)
                sends.append(pltpu.make_async_remote_copy(src_ref=s_ref, dst_ref=o_ref.at[2 * x + y], **sems))
                recvs.append(pltpu.make_async_remote_copy(src_ref=s_ref, dst_ref=o_ref.at[2 * px + py], **sems))
        return mine, sends, recvs

    def start(self, ins, outs, sems):
        mine, sends, _ = self._copies(ins, outs, *sems)
        for cp in mine + sends:
            cp.start()

    def wait(self, ins, outs, sems):
        mine, sends, recvs = self._copies(ins, outs, *sems)
        for cp in recvs:
            cp.wait_recv()
        for cp in sends:
            cp.wait_send()
        for cp in mine:
            cp.wait()


class _Scatter:
    def __init__(self, sends):
        self.args = list(sends)
        self.n = len(self.args)
        self.specs = [ANY] * self.n
        self.out_shape = [SDS((3,) + a.shape[1:], a.dtype) for a in self.args]
        self.sems = [pltpu.SemaphoreType.DMA((3 * self.n,)), pltpu.SemaphoreType.DMA((3 * self.n,))]

    def _copies(self, ins, outs, ssem, rsem):
        x, y, c = _my_xyc()
        return [pltpu.make_async_remote_copy(
            src_ref=s_ref.at[2 * px + py], dst_ref=o_ref.at[j], send_sem=ssem.at[3 * k + j],
            recv_sem=rsem.at[3 * k + j], device_id=(px, py, c), device_id_type=MESH)
            for k, (s_ref, o_ref) in enumerate(zip(ins, outs)) for j, (px, py) in enumerate(_chip_peers(x, y))]

    def start(self, ins, outs, sems):
        for cp in self._copies(ins, outs, *sems):
            cp.start()

    def wait(self, ins, outs, sems):
        for cp in self._copies(ins, outs, *sems):
            cp.wait()


class _Swap(_Scatter):
    def __init__(self, arrays):
        self.args = list(arrays)
        self.n = len(self.args)
        self.specs = [ANY] * self.n
        self.out_shape = [SDS(a.shape, a.dtype) for a in self.args]
        self.sems = [pltpu.SemaphoreType.DMA((self.n,)), pltpu.SemaphoreType.DMA((self.n,))]

    def _copies(self, ins, outs, ssem, rsem):
        x, y, c = _my_xyc()
        return [pltpu.make_async_remote_copy(src_ref=s_ref, dst_ref=o_ref, send_sem=ssem.at[k], recv_sem=rsem.at[k],
                                             device_id=(x, y, 1 - c), device_id_type=MESH)
                for k, (s_ref, o_ref) in enumerate(zip(ins, outs))]


def _exchange_call(ex, name):
    n = ex.n

    def body(*refs):
        ins, outs, sems = refs[:n], refs[n:2 * n], refs[2 * n:]
        ex.start(ins, outs, sems)
        ex.wait(ins, outs, sems)

    return pl.pallas_call(body, name=name, in_specs=ex.specs, out_specs=ex.specs, out_shape=ex.out_shape,
                          scratch_shapes=ex.sems)(*ex.args)


def _carried(ex, refs, n_in, n_out):
    n = ex.n
    ins, cin = refs[:n_in], refs[n_in:n_in + n]
    outs, cout = refs[n_in + n:n_in + n + n_out], refs[n_in + n + n_out:n_in + 2 * n + n_out]
    rest = refs[n_in + 2 * n + n_out:]
    nsem = len(ex.sems)
    return list(ins) + list(outs) + list(rest[:len(rest) - nsem]), (cin, cout, rest[len(rest) - nsem:])


def _gather_all(v, name):
    R, C = v.shape

    def body(s_ref, o_ref, ssem, rsem, lsem):
        x, y, c = _my_xyc()
        me = 4 * x + 2 * y + c
        mine = pltpu.make_async_copy(s_ref, o_ref.at[me], lsem)
        mine.start()

        def peer(j):
            fx, fy, fc = (j >> 2) & 1, (j >> 1) & 1, j & 1
            return (x ^ fx, y ^ fy, c ^ fc)

        cps = []
        for j in range(1, 8):
            cps.append(pltpu.make_async_remote_copy(
                src_ref=s_ref, dst_ref=o_ref.at[me], send_sem=ssem.at[j - 1], recv_sem=rsem.at[j - 1],
                device_id=peer(j), device_id_type=MESH))
            cps[-1].start()
        for j in range(1, 8):
            px, py, pc = peer(j)
            pltpu.make_async_remote_copy(
                src_ref=s_ref, dst_ref=o_ref.at[4 * px + 2 * py + pc], send_sem=ssem.at[j - 1],
                recv_sem=rsem.at[j - 1], device_id=peer(j), device_id_type=MESH).wait_recv()
        for cp in cps:
            cp.wait_send()
        mine.wait()

    return pl.pallas_call(
        body, name=name, in_specs=[ANY], out_specs=ANY, out_shape=SDS((8, R, C), v.dtype),
        scratch_shapes=[pltpu.SemaphoreType.DMA((7,)), pltpu.SemaphoreType.DMA((7,)), pltpu.SemaphoreType.DMA],
    )(v)


def _sum_slots(own, slots, name):
    n, R, C = slots.shape
    br = _block_rows(R)

    def body(*refs):
        o_ref = refs[-1]
        if own is None:
            acc = refs[0][0].astype(F32)
            first = 1
            s_ref = refs[0]
        else:
            acc = refs[0][...]
            first = 0
            s_ref = refs[1]
        for k in range(first, n):
            acc = acc + s_ref[k].astype(F32)
        o_ref[...] = acc

    row = BS((br, C), lambda i: (i, 0))
    slab = BS((n, br, C), lambda i: (0, i, 0))
    ins, args = ([slab], [slots]) if own is None else ([row, slab], [own, slots])
    return pl.pallas_call(
        body, name=name, grid=(R // br,), in_specs=ins, out_specs=row, out_shape=SDS((R, C), F32),
        compiler_params=_cp(("arbitrary",)),
    )(*args)


def _rope_tables(L, Lc):
    n = jnp.arange(L)
    row = (n // GRID_W).astype(F32)
    col = (n % GRID_W).astype(F32)
    freqs = ROPE_BASE ** (-jnp.arange(16, dtype=F32) / 16)
    lane = jnp.arange(128)
    dd = lane % HD
    fr = freqs[dd % 16]
    ang = jnp.where(dd < 32, row[:, None], col[:, None]) * fr[None, :]
    sign = jnp.where((dd % 32) < 16, -1.0, 1.0)
    cos = jnp.concatenate([jnp.ones((Lc, 128), F32), jnp.cos(ang)], axis=0)
    sin = jnp.concatenate([jnp.zeros((Lc, 128), F32), jnp.sin(ang) * sign[None, :]], axis=0)
    return cos, sin


def _ssm_prep(lam_re, lam_im, log_dt, b_re, b_im, c_re, c_im, B):
    dt = jnp.exp(log_dt)[..., None]
    mag = jnp.exp(lam_re * dt)
    ar = mag * jnp.cos(lam_im * dt)
    ai = mag * jnp.sin(lam_im * dt)
    den = lam_re * lam_re + lam_im * lam_im
    kr = ((ar - 1.0) * lam_re + ai * lam_im) / den
    ki = (ai * lam_re - (ar - 1.0) * lam_im) / den
    bbr = kr[..., None] * b_re - ki[..., None] * b_im
    bbi = kr[..., None] * b_im + ki[..., None] * b_re
    eye = jnp.eye(SGRP, dtype=F32)

    def bblk(t):
        return jnp.einsum("gpi,gh->gihp", t, eye).reshape(SGRP * SCH, SGRP * SST)

    def cblk(t):
        return jnp.einsum("gip,gh->gphi", t, eye).reshape(SGRP * SST, SGRP * SCH)

    b4 = jnp.concatenate([bblk(bbr[0]), bblk(bbi[0]), bblk(bbr[1]), bblk(bbi[1])], axis=1)
    c2 = jnp.concatenate([jnp.concatenate([cblk(c_re[0]), -cblk(c_im[0])], axis=0),
                          jnp.concatenate([cblk(c_re[1]), -cblk(c_im[1])], axis=0)], axis=1)
    a2 = jnp.concatenate([ar.reshape(2, -1), ai.reshape(2, -1)], axis=1)
    a = jnp.repeat(a2, B, axis=0)
    return a, b4, c2


def _pad_rows(v, rows=8):
    return jnp.concatenate([v, jnp.zeros((rows - v.shape[0],) + v.shape[1:], v.dtype)], axis=0)


_BIG = ("w_ada", "w_in", "w_out", "w_mlp_in", "w_mlp_out")
_BIG_AXIS = {"w_ada": 2, "w_in": 2, "w_out": 1, "w_mlp_in": 2, "w_mlp_out": 1}


def _split_shards(g, axis):
    L, r, c = g.shape
    if axis == 2:
        return jnp.transpose(g.reshape(L, r, NCHIP, c // NCHIP), (2, 0, 1, 3))
    return jnp.transpose(g.reshape(L, NCHIP, r // NCHIP, c), (1, 0, 2, 3))


def _join_shards(s, axis):
    _, L, r, c = s.shape
    if axis == 2:
        return jnp.transpose(s, (1, 2, 0, 3)).reshape(L, r, NCHIP * c)
    return jnp.transpose(s, (1, 0, 2, 3)).reshape(L, NCHIP * r, c)


def _local_step(x, c, ctx, loss_target, P, ex):
    B, L, _ = x.shape
    Lc = ctx.shape[1]
    S = Lc + L
    nct = Lc // TB
    nt = S // TB
    nch = 2 * B
    cos, sin = _rope_tables(L, Lc)
    perm = _scan_perm(B)
    pm, pmt = jnp.asarray(perm, MXU), jnp.asarray(perm.T, MXU)

    c_act = jax.nn.silu(c)
    cc_act, silu_vjp = jax.vjp(jax.nn.silu, P["c_ctx"])
    cact8 = _pad_rows(jnp.concatenate([c_act, cc_act[None, :]], axis=0))
    h = jnp.concatenate([ctx, x], axis=1)

    saved = []
    for l in range(NLAYER):
        Wl = ex.weights(l)
        mod8 = _ada_fwd(cact8, Wl["w_ada"], P["b_ada"][l][None, :])
        mod6 = mod8.reshape(8, NMOD, D)
        modtok = jnp.stack([jnp.broadcast_to(mod6[B], (B, NMOD, D)), mod6[:B]], axis=1)
        modtok = jnp.concatenate([modtok, jnp.zeros((B, 2, 2, D), F32)], axis=2)
        ng = _pad_rows(P["norm_g"][l])
        cw8 = _pad_rows(P["conv_w"][l])
        (a_s, b4, c2), prep_vjp = jax.vjp(
            lambda *t: _ssm_prep(*t, B), P["ssm_lam_re"][l], P["ssm_lam_im"][l], P["ssm_log_dt"][l],
            P["ssm_b_re"][l], P["ssm_b_im"][l], P["ssm_c_re"][l], P["ssm_c_im"][l])
        b4m, c2m = b4.astype(MXU), c2.astype(MXU)
        wg = P["w_glu"][l].astype(MXU)
        dsk = P["ssm_d"][l][None, :]
        bg = P["b_glu"][l][None, :]

        p, a = _pre_mix(h, modtok, ng, Wl["w_in"], nct)
        q4, kr = _rope_fwd(p, cos, sin)
        (o,) = ex.attn_fwd(l, P["attn_sink"][l], q4, kr, p, Lc)
        cv = _conv_fwd(p, cw8, nct)
        yf, yb, hb = _scan_fwd(p, pm, pmt, b4m, a_s, c2m, Lc)
        h1, mix, m, y = _post_mix(o, cv, yf, yb, p, h, modtok, ng, dsk, wg, bg, Wl["w_out"], nct)
        h2, a2, z, f = ex.mlp_fwd(l, h1, modtok, ng, Wl["w_mlp_in"], Wl["w_mlp_out"], nct)
        saved.append(dict(Wl=Wl, h=h, modtok=modtok, ng=ng, cw8=cw8, a_s=a_s, b4m=b4m, c2m=c2m, wg=wg, dsk=dsk, bg=bg,
                          p=p, a=a, q4=q4, kr=kr, hb=hb, mix=mix, m=m, y=y, h1=h1, a2=a2, z=z, f=f,
                          prep_vjp=prep_vjp))
        h = h2

    dh, lpart = _loss(h, loss_target, Lc)
    loss_local = 0.5 / D * jnp.sum(lpart[:, :, 0, 0])

    G = {k: [None] * NLAYER for k in _SMALL if k != "c_ctx"}
    dcact = jnp.zeros((8, D), F32)
    T = B * S
    for l in reversed(range(NLAYER)):
        sv = saved[l]
        modtok, ng, Wl, GB = sv["modtok"], sv["ng"], sv["Wl"], {}
        dh1, df, dz, vec_m = ex.mlp_bwd(l, dh, sv["h1"], sv["f"], sv["z"], modtok, ng, Wl["w_mlp_in"],
                                        Wl["w_mlp_out"], nct)
        GB["w_mlp_out"] = _mm_tn(sv["z"].reshape(T, DFF), df.reshape(T, D), "dw_mlp_out", relu2=True)
        GB["w_mlp_in"] = _mm_tn(sv["a2"].reshape(T, D), dz.reshape(T, DFF), "dw_mlp_in")
        dm, dattn, dcv, dy, du_skip, vec_p, vec_p2, dwg = _post_bwd(
            dh1, sv["m"], sv["y"], sv["p"], modtok, ng, sv["dsk"], sv["wg"], sv["bg"], Wl["w_out"], nct)
        GB["w_out"] = _mm_tn(sv["mix"].reshape(T, D), dm.reshape(T, D), "dw_out")
        G["w_glu"][l] = dwg
        G["ssm_d"][l] = jnp.sum(vec_p2[:, :, 0, :], axis=(0, 1))
        G["b_glu"][l] = jnp.sum(vec_p2[:, :, 1, :], axis=(0, 1))
        duf, dub, db4, da_s, dc2 = ex.scan_bwd(l, sv["p"], dy, sv["hb"], pm, pmt, sv["b4m"], sv["a_s"], sv["c2m"],
                                               Lc)
        du = du_skip + duf + dub
        gs = sv["prep_vjp"]((da_s, db4, dc2))
        for k, name in enumerate(("ssm_lam_re", "ssm_lam_im", "ssm_log_dt", "ssm_b_re", "ssm_b_im", "ssm_c_re",
                                  "ssm_c_im")):
            G[name][l] = gs[k]
        dconv, dcw = _conv_bwd(sv["p"], dcv, sv["cw8"], nct)
        G["conv_w"][l] = jnp.sum(dcw[:, :, 0:3, :], axis=(0, 1))
        dq_r, dk_r, dv, dsink = _attn_bwd(P["attn_sink"][l], sv["q4"], sv["kr"], sv["p"], dattn, Lc)
        G["attn_sink"][l] = jnp.sum(dsink[:, :, 0], axis=0)
        dqk = _rope_bwd(dq_r, dk_r, cos, -sin)
        dp = jnp.concatenate([dqk, dv.astype(MXU), dconv.astype(MXU), du.astype(MXU)], axis=-1)
        GB["w_in"] = _mm_tn(sv["a"].reshape(T, D), dp.reshape(T, INW), "dw_in")
        dh, vec_i = _pre_bwd(dp, dh1, sv["h"], modtok, ng, Wl["w_in"], nct)
        G["norm_g"][l] = jnp.stack([jnp.sum(vec_i[:, :, 2], axis=(0, 1)), jnp.sum(vec_p[:, :, 1], axis=(0, 1)),
                                    jnp.sum(vec_m[:, :, 2], axis=(0, 1)), jnp.sum(vec_m[:, :, 4], axis=(0, 1))])
        dmod_blk = jnp.stack([vec_i[:, :, 0], vec_i[:, :, 1], vec_p[:, :, 0],
                              vec_m[:, :, 0], vec_m[:, :, 1], vec_m[:, :, 3]], axis=2)
        dmod_lat = jnp.sum(dmod_blk[:, nct:], axis=1).reshape(B, NMOD * D)
        dmod_ctx = jnp.sum(dmod_blk[:, :nct], axis=(0, 1)).reshape(1, NMOD * D)
        dmod8 = _pad_rows(jnp.concatenate([dmod_lat, dmod_ctx], axis=0))
        G["b_ada"][l] = jnp.sum(dmod8, axis=0)
        dmod8m = dmod8.astype(MXU)
        GB["w_ada"] = _mm_tn(cact8.astype(MXU), dmod8m, "dw_ada")
        dcact = dcact + _ada_bwd_c(dmod8m, Wl["w_ada"])
        ex.submit(l, GB)

    grads = {k: jnp.stack(v) for k, v in G.items()}
    grads["c_ctx"] = silu_vjp(dcact[B])[0]
    return loss_local, dh[:, Lc:, :], grads


_WEIGHTS = ["c_ctx", "w_ada", "b_ada", "norm_g", "w_in", "conv_w", "attn_sink", "ssm_lam_re", "ssm_lam_im",
            "ssm_log_dt", "ssm_b_re", "ssm_b_im", "ssm_c_re", "ssm_c_im", "ssm_d", "w_glu", "b_glu", "w_out",
            "w_mlp_in", "w_mlp_out"]
_SMALL = [n for n in _WEIGHTS if n not in _BIG]
_SMALL_SHARDED = {"norm_g": 2, "conv_w": 2, "w_glu": 1}
_LANES = 1024


_GROUPS = (("w_ada", "w_in", "w_out"), ("w_mlp_in", "w_mlp_out"))


class _Exchange:
    def __init__(self, W, q_me):
        self.W, self.q_me = W, q_me
        self.wall, self.own, self.recv, self.pending = {}, {}, {}, {}
        self._keep(0, _BIG, _exchange_call(_Gather([self._shard(0, n) for n in _BIG]), "gather_big"))

    def _shard(self, l, n):
        return self.W[n][l].astype(MXU)

    def _keep(self, l, names, walls):
        for n, w in zip(names, walls):
            self.wall[l, n] = w

    def weights(self, l):
        g = {n: self.wall[l, n] for n in _BIG}
        return dict(w_ada=g["w_ada"], w_mlp_in=g["w_mlp_in"], w_in=_join_shards(g["w_in"][:, None], 2)[0],
                    w_out=g["w_out"].reshape(-1, g["w_out"].shape[-1]),
                    w_mlp_out=g["w_mlp_out"].reshape(-1, g["w_mlp_out"].shape[-1]))

    def _gathering(self, l, grp, fn, *args):
        if l + 1 == NLAYER:
            return fn(*args)[0]
        res, walls = fn(*args, ex=_Gather([self._shard(l + 1, n) for n in _GROUPS[grp]]))
        self._keep(l + 1, _GROUPS[grp], walls)
        return res

    def attn_fwd(self, l, *args):
        return self._gathering(l, 0, _attn_fwd, *args)

    def mlp_fwd(self, l, *args):
        return self._gathering(l, 1, _mlp_fwd, *args)

    def submit(self, l, G):
        for grp, names in enumerate(_GROUPS):
            sends = []
            for n in names:
                g, gm = G[n]
                r, c = g.shape
                if _BIG_AXIS[n] == 2:
                    sends.append(jnp.transpose(gm.reshape(r, NCHIP, c // NCHIP), (1, 0, 2)))
                    self.own[l, n] = lax.dynamic_slice_in_dim(g, self.q_me * (c // NCHIP), c // NCHIP, axis=1)
                else:
                    sends.append(gm.reshape(NCHIP, r // NCHIP, c))
                    self.own[l, n] = lax.dynamic_slice_in_dim(g, self.q_me * (r // NCHIP), r // NCHIP, axis=0)
            self.pending[grp] = (l, sends)

    def _scattering(self, grp, fn, *args):
        if grp not in self.pending:
            return fn(*args)[0]
        lp, sends = self.pending.pop(grp)
        res, recvs = fn(*args, ex=_Scatter(sends))
        for n, r in zip(_GROUPS[grp], recvs):
            self.recv[lp, n] = r
        return res

    def mlp_bwd(self, l, *args):
        return self._scattering(0, _mlp_bwd, *args)

    def scan_bwd(self, l, *args):
        return self._scattering(1, _scan_bwd, *args)

    def finish(self):
        (lp, s0), (_, s1) = self.pending.pop(0), self.pending.pop(1)
        for n, r in zip(_BIG, _exchange_call(_Scatter(s0 + s1), "scatter_big")):
            self.recv[lp, n] = r
        part = [jnp.concatenate([_sum_slots(self.own[l, n], self.recv[l, n], "sum_chips") for l in range(NLAYER)],
                                axis=0) for n in _BIG]
        sib = _exchange_call(_Swap(part), "swap_big")
        return {n: (p, s) for n, p, s in zip(_BIG, part, sib)}


def _flat_pad(parts):
    v = jnp.concatenate([p.reshape(-1) for p in parts])
    rows = -(-v.shape[0] // (_PAD_ROWS * _LANES)) * _PAD_ROWS
    return jnp.concatenate([v, jnp.zeros((rows * _LANES - v.shape[0],), v.dtype)]).reshape(rows, _LANES)


def _unflat(flat, shapes):
    v = flat.reshape(-1)
    out, off = [], 0
    for s in shapes:
        n = math.prod(s)
        out.append(v[off:off + n].reshape(s))
        off += n
    return out


def kernel(x, c, ctx, c_ctx, w_ada, b_ada, norm_g, w_in, conv_w, attn_sink, ssm_lam_re, ssm_lam_im, ssm_log_dt, ssm_b_re, ssm_b_im, ssm_c_re, ssm_c_im, ssm_d, w_glu, b_glu, w_out, w_mlp_in, w_mlp_out, loss_target, m_c_ctx, m_w_ada, m_b_ada, m_norm_g, m_w_in, m_conv_w, m_attn_sink, m_ssm_lam_re, m_ssm_lam_im, m_ssm_log_dt, m_ssm_b_re, m_ssm_b_im, m_ssm_c_re, m_ssm_c_im, m_ssm_d, m_w_glu, m_b_glu, m_w_out, m_w_mlp_in, m_w_mlp_out, v_c_ctx, v_w_ada, v_b_ada, v_norm_g, v_w_in, v_conv_w, v_attn_sink, v_ssm_lam_re, v_ssm_lam_im, v_ssm_log_dt, v_ssm_b_re, v_ssm_b_im, v_ssm_c_re, v_ssm_c_im, v_ssm_d, v_w_glu, v_b_glu, v_w_out, v_w_mlp_in, v_w_mlp_out):
    W = dict(c_ctx=c_ctx, w_ada=w_ada, b_ada=b_ada, norm_g=norm_g, w_in=w_in, conv_w=conv_w, attn_sink=attn_sink,
             ssm_lam_re=ssm_lam_re, ssm_lam_im=ssm_lam_im, ssm_log_dt=ssm_log_dt, ssm_b_re=ssm_b_re,
             ssm_b_im=ssm_b_im, ssm_c_re=ssm_c_re, ssm_c_im=ssm_c_im, ssm_d=ssm_d, w_glu=w_glu, b_glu=b_glu,
             w_out=w_out, w_mlp_in=w_mlp_in, w_mlp_out=w_mlp_out)
    M = dict(c_ctx=m_c_ctx, w_ada=m_w_ada, b_ada=m_b_ada, norm_g=m_norm_g, w_in=m_w_in, conv_w=m_conv_w,
             attn_sink=m_attn_sink, ssm_lam_re=m_ssm_lam_re, ssm_lam_im=m_ssm_lam_im, ssm_log_dt=m_ssm_log_dt,
             ssm_b_re=m_ssm_b_re, ssm_b_im=m_ssm_b_im, ssm_c_re=m_ssm_c_re, ssm_c_im=m_ssm_c_im, ssm_d=m_ssm_d,
             w_glu=m_w_glu, b_glu=m_b_glu, w_out=m_w_out, w_mlp_in=m_w_mlp_in, w_mlp_out=m_w_mlp_out)
    V = dict(c_ctx=v_c_ctx, w_ada=v_w_ada, b_ada=v_b_ada, norm_g=v_norm_g, w_in=v_w_in, conv_w=v_conv_w,
             attn_sink=v_attn_sink, ssm_lam_re=v_ssm_lam_re, ssm_lam_im=v_ssm_lam_im, ssm_log_dt=v_ssm_log_dt,
             ssm_b_re=v_ssm_b_re, ssm_b_im=v_ssm_b_im, ssm_c_re=v_ssm_c_re, ssm_c_im=v_ssm_c_im, ssm_d=v_ssm_d,
             w_glu=v_w_glu, b_glu=v_b_glu, w_out=v_w_out, w_mlp_in=v_w_mlp_in, w_mlp_out=v_w_mlp_out)
    q_me = 2 * lax.axis_index("x") + lax.axis_index("y")

    ex = _Exchange(W, q_me)
    P = {n: W[n] for n in _SMALL}
    ssh_names = list(_SMALL_SHARDED)
    ssh = _flat_pad([W[n] for n in ssh_names])
    (sall,) = _exchange_call(_Gather([ssh]), "gather_small")
    parts = [_unflat(sall[q], [W[n].shape for n in ssh_names]) for q in range(NCHIP)]
    for k, n in enumerate(ssh_names):
        P[n] = _join_shards(jnp.stack([parts[q][k] for q in range(NCHIP)]), _SMALL_SHARDED[n])

    loss_local, grad_x, G = _local_step(x, c, ctx, loss_target, P, ex)
    loss = lax.psum(loss_local, ("x", "y", "c"))

    sums = ex.finish()
    out = {}
    for n in _BIG:
        flat = lambda t: t.reshape(-1, t.shape[-1])
        res = _adamw(flat(W[n]), *sums[n], flat(M[n]), flat(V[n]), "adamw_" + n)
        out[n] = [t.reshape(W[n].shape) for t in res]

    gsmall = _flat_pad([G[n] for n in _SMALL])
    gsum = _sum_slots(None, _gather_all(gsmall, "gather_grads"), "sum_devices")
    gfull = dict(zip(_SMALL, _unflat(gsum, [G[n].shape for n in _SMALL])))
    for n, ax in _SMALL_SHARDED.items():
        width = W[n].shape[ax]
        gfull[n] = lax.dynamic_slice_in_dim(gfull[n], q_me * width, width, axis=ax)
    shapes = [W[n].shape for n in _SMALL]
    gflat = _flat_pad([gfull[n] for n in _SMALL])
    res = _adamw(_flat_pad([W[n] for n in _SMALL]), gflat, jnp.zeros_like(gflat),
                 _flat_pad([M[n] for n in _SMALL]), _flat_pad([V[n] for n in _SMALL]), "adamw_small")
    for k, t in enumerate(res):
        for n, piece in zip(_SMALL, _unflat(t, shapes)):
            out.setdefault(n, [None] * 4)[k] = piece

    return (loss, grad_x, *[out[n][0] for n in _WEIGHTS], *[out[n][1] for n in _WEIGHTS],
            *[out[n][2] for n in _WEIGHTS], *[out[n][3] for n in _WEIGHTS])
```

```python
import functools
import math

import jax
import jax.numpy as jnp
import numpy as np
from jax import lax
from jax.experimental import pallas as pl
from jax.experimental.pallas import tpu as pltpu

F32 = jnp.float32
MXU = jnp.bfloat16

D = 1024
DFF = 4096
NMOD = 6
EPS = 1e-6
HD = 64
NQ = 8
NKV = 2
QPK = 4
AW = 512
KVW = 128
WIN = 128
QB = 128
CW = 256
SW = 256
SGRP = 16
SCH = 16
SST = 64
INW = 1792
GRID_W = 64
ROPE_BASE = 10000.0
NEG = -1e30
SCALE = HD ** -0.5
NLAYER = 4

TB = 256
TF = 1024
TC = 32
NCHIP = 4

LR, B1, B2, AEPS, WD, STEP = 0.001, 0.9, 0.999, 1e-08, 0.01, 10

MESH = pl.DeviceIdType.MESH
SDS = jax.ShapeDtypeStruct
BS = pl.BlockSpec
ANY = pl.BlockSpec(memory_space=pl.ANY)


def _cp(sem, vmem_mb=None):
    kw = dict(dimension_semantics=sem)
    if vmem_mb is not None:
        kw["vmem_limit_bytes"] = vmem_mb * 1024 * 1024
    return pltpu.CompilerParams(**kw)


def _dot(a, b):
    return jnp.dot(a, b, preferred_element_type=F32)


def _dot_nt(a, b):
    return lax.dot_general(a, b, (((1,), (1,)), ((), ())), preferred_element_type=F32)


def _dot_tn(a, b):
    return lax.dot_general(a, b, (((0,), (0,)), ((), ())), preferred_element_type=F32)


def _rms(x):
    r = lax.rsqrt(jnp.mean(x * x, axis=-1, keepdims=True) + EPS)
    return x * r, r


def _rms_bwd(dn, n, r):
    return r * (dn - n * jnp.mean(dn * n, axis=-1, keepdims=True))


_GC = math.sqrt(2.0 / math.pi)


def _gelu(y):
    return 0.5 * y * (1.0 + jnp.tanh(_GC * (y + 0.044715 * y * y * y)))


def _gelu_grad(y):
    th = jnp.tanh(_GC * (y + 0.044715 * y * y * y))
    return 0.5 * (1.0 + th) + 0.5 * y * (1.0 - th * th) * _GC * (1.0 + 3 * 0.044715 * y * y)


def _sigmoid(v):
    return 1.0 / (1.0 + jnp.exp(-v))


def _colsum(x):
    return jnp.sum(x, axis=0, keepdims=True)


def _tok(width, col=0):
    return BS((None, TB, width), lambda b, j: (b, j, col))


def _mod_spec(nct):
    return BS((None, None, 8, D), lambda b, j: (b, jnp.where(j >= nct, 1, 0), 0, 0))


def _full(shape):
    nd = len(shape)
    return BS(shape, lambda *a: (0,) * nd)


def _resident(shape):
    nd = len(shape)
    return BS(shape, lambda *a: (0,) * nd, pipeline_mode=pl.Buffered(1))


def _ada_fwd(cact8, w4, b):
    wq = w4.shape[2]

    def body(c_ref, w_ref, b_ref, o_ref):
        o_ref[...] = _dot(c_ref[...].astype(MXU), w_ref[...]) + b_ref[...]

    return pl.pallas_call(
        body, name="ada_fwd", grid=(NCHIP,),
        in_specs=[BS((8, D), lambda j: (0, 0)), BS((None, D, wq), lambda j: (j, 0, 0)), BS((1, wq), lambda j: (0, j))],
        out_specs=BS((8, wq), lambda j: (0, j)),
        out_shape=SDS((8, NMOD * D), F32), compiler_params=_cp(("arbitrary",)),
    )(cact8, w4, b)


def _ada_bwd_c(dmod8, w4):
    wq = w4.shape[2]

    def body(d_ref, w_ref, o_ref):
        @pl.when(pl.program_id(0) == 0)
        def _():
            o_ref[...] = jnp.zeros_like(o_ref)
        o_ref[...] += _dot_nt(d_ref[...], w_ref[...])

    return pl.pallas_call(
        body, name="ada_bwd_c", grid=(NCHIP,),
        in_specs=[BS((8, wq), lambda j: (0, j)), BS((None, D, wq), lambda j: (j, 0, 0))],
        out_specs=BS((8, D), lambda j: (0, 0)),
        out_shape=SDS((8, D), F32), compiler_params=_cp(("arbitrary",)),
    )(dmod8, w4)


def _pre_mix(h, modtok, ng, w_in, nct):
    B, S, _ = h.shape

    def body(h_ref, mod_ref, g_ref, w_ref, p_ref, a_ref):
        n, _ = _rms(h_ref[...])
        a = ((n * g_ref[0:1, :]) * (1.0 + mod_ref[1:2, :]) + mod_ref[0:1, :]).astype(MXU)
        a_ref[...] = a
        p_ref[...] = _dot(a, w_ref[...])

    return pl.pallas_call(
        body, name="pre_mix", grid=(B, S // TB),
        in_specs=[_tok(D), _mod_spec(nct), _full((8, D)), _full((D, INW))],
        out_specs=[_tok(INW), _tok(D)],
        out_shape=[SDS((B, S, INW), F32), SDS((B, S, D), MXU)],
        compiler_params=_cp(("arbitrary", "arbitrary"), 40),
    )(h, modtok, ng, w_in)


def _pre_bwd(dqk, dv, dcv, du_parts, dh_part, h, modtok, ng, w_in, nct):
    B, S, _ = h.shape

    def body(dqk_ref, dv_ref, dcv_ref, du0_ref, du1_ref, du2_ref, dhp_ref, h_ref, mod_ref, g_ref, w_ref,
             dh_ref, vec_ref, dp_ref):
        du = du0_ref[...] + du1_ref[...] + du2_ref[...]
        dp = jnp.concatenate([dqk_ref[...], dv_ref[...].astype(MXU), dcv_ref[...].astype(MXU), du.astype(MXU)],
                             axis=1)
        dp_ref[...] = dp
        da = _dot_nt(dp, w_ref[...])
        n, r = _rms(h_ref[...])
        g = g_ref[0:1, :]
        sc1 = 1.0 + mod_ref[1:2, :]
        vec_ref[...] = jnp.zeros_like(vec_ref)
        vec_ref[0:1, :] = _colsum(da)
        vec_ref[1:2, :] = _colsum(da * (n * g))
        vec_ref[2:3, :] = _colsum(da * sc1 * n)
        dh_ref[...] = dhp_ref[...] + _rms_bwd(da * sc1 * g, n, r)

    return pl.pallas_call(
        body, name="pre_bwd", grid=(B, S // TB),
        in_specs=[_tok(AW + KVW), _tok(KVW), _tok(3 * CW), _tok(SW), _tok(SW), _tok(SW), _tok(D), _tok(D),
                  _mod_spec(nct), _full((8, D)), _full((D, INW))],
        out_specs=[_tok(D), BS((None, None, 8, D), lambda b, j: (b, j, 0, 0)), _tok(INW)],
        out_shape=[SDS((B, S, D), F32), SDS((B, S // TB, 8, D), F32), SDS((B, S, INW), MXU)],
        compiler_params=_cp(("arbitrary", "arbitrary"), 40),
    )(dqk, dv, dcv, *du_parts, dh_part, h, modtok, ng, w_in)


def _rotate(t, c, s):
    lane = lax.broadcasted_iota(jnp.int32, t.shape, 1)
    partner = jnp.where((lane % 32) < 16, pltpu.roll(t, 112, 1), pltpu.roll(t, 16, 1))
    return t * c + partner * s


def _to_groups(cols):
    lo = lax.broadcasted_iota(jnp.int32, cols[0].shape, 1) < HD
    out = []
    for g in range(QPK):
        a, b = cols[g // 2], cols[2 + g // 2]
        out.append(jnp.where(lo, a, pltpu.roll(b, HD, 1)) if g % 2 == 0 else jnp.where(lo, pltpu.roll(a, HD, 1), b))
    return out


def _from_groups(tiles):
    lo = lax.broadcasted_iota(jnp.int32, tiles[0].shape, 1) < HD
    out = []
    for k in range(NQ // 2):
        a, b = tiles[(2 * k) % QPK], tiles[(2 * k) % QPK + 1]
        out.append(jnp.where(lo, a, pltpu.roll(b, HD, 1)) if k < 2 else jnp.where(lo, pltpu.roll(a, HD, 1), b))
    return out


def _grp(width=128):
    return BS((None, QPK, TB, width), lambda b, j: (b, 0, j, 0))


_TAB = BS((TB, 128), lambda b, j: (j, 0))


def _rope_fwd(p, cos, sin):
    B, S, _ = p.shape

    def body(x_ref, c_ref, s_ref, q_ref, k_ref):
        c, s = c_ref[...], s_ref[...]
        rot = [_rotate(x_ref[:, k * 128:(k + 1) * 128], c, s) for k in range((AW + KVW) // 128)]
        k_ref[...] = rot[AW // 128].astype(MXU)
        for g, t in enumerate(_to_groups(rot[:AW // 128])):
            q_ref[g] = (t * SCALE).astype(MXU)

    return pl.pallas_call(
        body, name="rope_fwd", grid=(B, S // TB),
        in_specs=[_tok(AW + KVW), _TAB, _TAB],
        out_specs=[_grp(), _tok(KVW)],
        out_shape=[SDS((B, QPK, S, 128), MXU), SDS((B, S, KVW), MXU)],
        compiler_params=_cp(("arbitrary", "arbitrary")),
    )(p, cos, sin)


def _rope_bwd(dq4, dk, cos, nsin):
    B, _, S, _ = dq4.shape

    def body(q_ref, k_ref, c_ref, s_ref, o_ref):
        c, s = c_ref[...], s_ref[...]
        cols = _from_groups([q_ref[g] * SCALE for g in range(QPK)]) + [k_ref[...]]
        for k, t in enumerate(cols):
            o_ref[:, k * 128:(k + 1) * 128] = _rotate(t, c, s).astype(MXU)

    return pl.pallas_call(
        body, name="rope_bwd", grid=(B, S // TB),
        in_specs=[_grp(), _tok(KVW), _TAB, _TAB],
        out_specs=_tok(AW + KVW),
        out_shape=SDS((B, S, AW + KVW), MXU),
        compiler_params=_cp(("arbitrary", "arbitrary")),
    )(dq4, dk, cos, nsin)


def _attn_masks(i, S, Lc):
    ncb = Lc // QB
    is_lat = i >= ncb
    w0 = pl.multiple_of(jnp.clip((i - 1) * QB, 0, S - 3 * QB), QB)
    shape = (QPK * QB, 3 * QB)
    qpos = i * QB + jnp.bitwise_and(lax.broadcasted_iota(jnp.int32, shape, 0), QB - 1)
    kpos = w0 + lax.broadcasted_iota(jnp.int32, shape, 1)
    mask = jnp.logical_and(jnp.logical_and(kpos >= Lc, jnp.abs(qpos - kpos) <= WIN), is_lat)
    return w0, mask


def _attn_probs(qh, kch, kwh, mask, sk):
    sc = _dot_nt(qh, kch)
    sw = jnp.where(mask, _dot_nt(qh, kwh), NEG)
    m = jnp.maximum(jnp.maximum(jnp.max(sc, axis=-1, keepdims=True), jnp.max(sw, axis=-1, keepdims=True)), sk)
    ec = jnp.exp(sc - m)
    ew = jnp.exp(sw - m)
    es = jnp.exp(sk - m)
    inv = 1.0 / (jnp.sum(ec, axis=-1, keepdims=True) + jnp.sum(ew, axis=-1, keepdims=True) + es)
    return ec * inv, ew * inv, es * inv


def _sink_col(sink_ref, hk):
    return jnp.concatenate([jnp.full((QB, 1), sink_ref[hk * QPK + g], F32) for g in range(QPK)], axis=0)


_QGRP = BS((None, QPK, QB, 128), lambda b, i: (b, 0, i, 0))


def _first_last(grid):
    first = last = None
    for ax, n in enumerate(grid):
        i = pl.program_id(ax)
        first = (i == 0) if first is None else jnp.logical_and(first, i == 0)
        last = (i == n - 1) if last is None else jnp.logical_and(last, i == n - 1)
    return first, last


def _call_with(ex, body, name, grid, in_specs, out_specs, out_shape, args, scratch, params):
    n_in, n_out = len(in_specs), len(out_specs)
    if ex is None:
        res = pl.pallas_call(body, name=name, grid=grid, in_specs=in_specs, out_specs=out_specs,
                             out_shape=out_shape, scratch_shapes=scratch, compiler_params=params)(*args)
        return list(res), []

    def carrying(*refs):
        own, comm = _carried(ex, refs, n_in, n_out)
        first, last = _first_last(grid)
        pl.when(first)(lambda: ex.start(*comm))
        body(*own)
        pl.when(last)(lambda: ex.wait(*comm))

    res = pl.pallas_call(
        carrying, name=f"{name}_{type(ex).__name__.strip('_').lower()}", grid=grid,
        in_specs=in_specs + ex.specs, out_specs=out_specs + ex.specs, out_shape=out_shape + ex.out_shape,
        scratch_shapes=scratch + ex.sems, compiler_params=params)(*args, *ex.args)
    return list(res[:n_out]), list(res[n_out:])


def _attn_fwd(sink, q4, k, p, Lc, ex=None):
    B, S, _ = k.shape

    def body(sink_ref, q_ref, k_ref, v_ref, o_ref):
        w0, mask = _attn_masks(pl.program_id(1), S, Lc)
        kc = k_ref[0:Lc, :]
        vc = v_ref[0:Lc, :].astype(MXU)
        kw = k_ref[pl.ds(w0, 3 * QB), :]
        vw = v_ref[pl.ds(w0, 3 * QB), :].astype(MXU)
        q = q_ref[...].reshape(QPK * QB, 128)
        outs = []
        for hk in range(NKV):
            cs = slice(hk * HD, (hk + 1) * HD)
            pc, pw, _ = _attn_probs(q[:, cs], kc[:, cs], kw[:, cs], mask, _sink_col(sink_ref, hk))
            o = _dot(pc.astype(MXU), vc[:, cs]) + _dot(pw.astype(MXU), vw[:, cs])
            outs += [o[g * QB:(g + 1) * QB] for g in range(QPK)]
        o_ref[...] = jnp.concatenate(outs, axis=1).astype(o_ref.dtype)

    return _call_with(
        ex, body, "attn_fwd", (B, S // QB),
        [BS(memory_space=pltpu.SMEM), _QGRP, BS((None, S, KVW), lambda b, i: (b, 0, 0)),
         BS((None, S, KVW), lambda b, i: (b, 0, (AW + KVW) // KVW))],
        [BS((None, QB, AW), lambda b, i: (b, i, 0))], [SDS((B, S, AW), MXU)], [sink, q4, k, p], [],
        _cp(("arbitrary", "arbitrary"), 40))


def _attn_bwd(sink, q4, k, p, do4, Lc):
    B, S, _ = k.shape

    def body(sink_ref, q_ref, k_ref, v_ref, do_ref, dq_ref, dk_ref, dv_ref, ds_ref):
        i = pl.program_id(1)

        @pl.when(i == 0)
        def _():
            dk_ref[...] = jnp.zeros_like(dk_ref)
            dv_ref[...] = jnp.zeros_like(dv_ref)
            ds_ref[...] = jnp.zeros_like(ds_ref)

        w0, mask = _attn_masks(i, S, Lc)
        kc = k_ref[0:Lc, :]
        vc = v_ref[0:Lc, :].astype(MXU)
        kw = k_ref[pl.ds(w0, 3 * QB), :]
        vw = v_ref[pl.ds(w0, 3 * QB), :].astype(MXU)
        q = q_ref[...].reshape(QPK * QB, 128)
        do = do_ref[...].reshape(QPK * QB, 128)
        dqs, dsinks, dkcs, dkws, dvcs, dvws = [], [], [], [], [], []
        for hk in range(NKV):
            cs = slice(hk * HD, (hk + 1) * HD)
            qh = q[:, cs]
            pc, pw, ps = _attn_probs(qh, kc[:, cs], kw[:, cs], mask, _sink_col(sink_ref, hk))
            dob = do[:, cs].astype(MXU)
            dpc = _dot_nt(dob, vc[:, cs])
            dpw = _dot_nt(dob, vw[:, cs])
            delta = jnp.sum(pc * dpc, axis=-1, keepdims=True) + jnp.sum(pw * dpw, axis=-1, keepdims=True)
            dsc = (pc * (dpc - delta)).astype(MXU)
            dsw = (pw * (dpw - delta)).astype(MXU)
            dqs.append(_dot(dsc, kc[:, cs]) + _dot(dsw, kw[:, cs]))
            dkcs.append(_dot_tn(dsc, qh))
            dkws.append(_dot_tn(dsw, qh))
            dvcs.append(_dot_tn(pc.astype(MXU), dob))
            dvws.append(_dot_tn(pw.astype(MXU), dob))
            psd = ps * delta
            dsinks += [jnp.broadcast_to(-jnp.sum(psd[g * QB:(g + 1) * QB], axis=0, keepdims=True), (1, 128))
                       for g in range(QPK)]
        for g in range(QPK):
            dq_ref[g] = jnp.concatenate([dq[g * QB:(g + 1) * QB] for dq in dqs], axis=1)
        ds_ref[...] += jnp.concatenate(dsinks, axis=0)
        dk_ref[0:Lc, :] += jnp.concatenate(dkcs, axis=1)
        dv_ref[0:Lc, :] += jnp.concatenate(dvcs, axis=1)
        dk_ref[pl.ds(w0, 3 * QB), :] += jnp.concatenate(dkws, axis=1)
        dv_ref[pl.ds(w0, 3 * QB), :] += jnp.concatenate(dvws, axis=1)

    kv_acc = BS((None, S, KVW), lambda b, i: (b, 0, 0))
    return pl.pallas_call(
        body, name="attn_bwd", grid=(B, S // QB),
        in_specs=[BS(memory_space=pltpu.SMEM), _QGRP, BS((None, S, KVW), lambda b, i: (b, 0, 0)),
                  BS((None, S, KVW), lambda b, i: (b, 0, (AW + KVW) // KVW)), _QGRP],
        out_specs=[_QGRP, kv_acc, kv_acc, BS((None, 8, 128), lambda b, i: (b, 0, 0))],
        out_shape=[SDS((B, QPK, S, 128), F32), SDS((B, S, KVW), F32), SDS((B, S, KVW), F32), SDS((B, 8, 128), F32)],
        compiler_params=_cp(("arbitrary", "arbitrary"), 48),
    )(sink, q4, k, p, do4)


def _halo_specs(width, col, S):
    nb8 = S // 8
    per = TB // 8
    prev = BS((None, 8, width), lambda b, j: (b, jnp.maximum(j * per - 1, 0), col))
    nxt = BS((None, 8, width), lambda b, j: (b, jnp.minimum((j + 1) * per, nb8 - 1), col))
    return prev, nxt


def _shift_dn(z, prev_row, at_start):
    row = lax.broadcasted_iota(jnp.int32, z.shape, 0)
    first = jnp.where(at_start, 0.0, 1.0) * prev_row
    return jnp.where(row == 0, first, pltpu.roll(z, 1, 0))


def _shift_up(z, next_row, at_end):
    row = lax.broadcasted_iota(jnp.int32, z.shape, 0)
    last = jnp.where(at_end, 0.0, 1.0) * next_row
    return jnp.where(row == z.shape[0] - 1, last, pltpu.roll(z, z.shape[0] - 1, 0))


def _conv_fwd(p, cw8, nct):
    B, S, _ = p.shape
    nt = S // TB

    def body(p_ref, pp_ref, pn_ref, w_ref, o_ref):
        j = pl.program_id(1)
        at_start = jnp.logical_or(j == 0, j == nct)
        at_end = jnp.logical_or(j == nct - 1, j == nt - 1)
        z = p_ref[:, 256:512] * p_ref[:, 512:768]
        zprev = pp_ref[7:8, 256:512] * pp_ref[7:8, 512:768]
        znext = pn_ref[0:1, 256:512] * pn_ref[0:1, 512:768]
        c3 = (_shift_dn(z, zprev, at_start) * w_ref[0:1, :] + z * w_ref[1:2, :]
              + _shift_up(z, znext, at_end) * w_ref[2:3, :])
        o_ref[...] = p_ref[:, 0:256] * c3

    prev, nxt = _halo_specs(3 * CW, 1, S)
    return pl.pallas_call(
        body, name="conv_fwd", grid=(B, nt),
        in_specs=[_tok(3 * CW, 1), prev, nxt, _full((8, CW))],
        out_specs=_tok(CW),
        out_shape=SDS((B, S, CW), F32),
        compiler_params=_cp(("arbitrary", "arbitrary")),
    )(p, p, p, cw8)


def _conv_bwd(p, dcv, cw8, nct):
    B, S, _ = p.shape
    nt = S // TB

    def body(p_ref, pp_ref, pn_ref, d_ref, dp_ref, dn_ref, w_ref, o_ref, dw_ref):
        j = pl.program_id(1)
        at_start = jnp.logical_or(j == 0, j == nct)
        at_end = jnp.logical_or(j == nct - 1, j == nt - 1)
        cb, cc, cx = p_ref[:, 0:256], p_ref[:, 256:512], p_ref[:, 512:768]
        z = cc * cx
        zm = _shift_dn(z, pp_ref[7:8, 256:512] * pp_ref[7:8, 512:768], at_start)
        zp = _shift_up(z, pn_ref[0:1, 256:512] * pn_ref[0:1, 512:768], at_end)
        d = d_ref[...]
        e = d * cb
        em = _shift_dn(e, dp_ref[7:8, :] * pp_ref[7:8, 0:256], at_start)
        ep = _shift_up(e, dn_ref[0:1, :] * pn_ref[0:1, 0:256], at_end)
        dz = e * w_ref[1:2, :] + ep * w_ref[0:1, :] + em * w_ref[2:3, :]
        o_ref[:, 0:256] = d * (zm * w_ref[0:1, :] + z * w_ref[1:2, :] + zp * w_ref[2:3, :])
        o_ref[:, 256:512] = dz * cx
        o_ref[:, 512:768] = dz * cc
        dw_ref[...] = jnp.zeros_like(dw_ref)
        dw_ref[0:1, :] = _colsum(e * zm)
        dw_ref[1:2, :] = _colsum(e * z)
        dw_ref[2:3, :] = _colsum(e * zp)

    prev, nxt = _halo_specs(3 * CW, 1, S)
    dprev, dnxt = _halo_specs(CW, 0, S)
    return pl.pallas_call(
        body, name="conv_bwd", grid=(B, nt),
        in_specs=[_tok(3 * CW, 1), prev, nxt, _tok(CW), dprev, dnxt, _full((8, CW))],
        out_specs=[_tok(3 * CW), BS((None, None, 8, CW), lambda b, j: (b, j, 0, 0))],
        out_shape=[SDS((B, S, 3 * CW), F32), SDS((B, nt, 8, CW), F32)],
        compiler_params=_cp(("arbitrary", "arbitrary")),
    )(p, p, p, dcv, dcv, dcv, cw8)


HW = SGRP * SST


def _rev_chunk(k, nk, ncc):
    return jnp.where(k < ncc, ncc - 1 - k, nk - 1 - k + ncc)


def _scan_perm(B):
    n = B * TC
    pm = np.zeros((2 * n, 2 * n), np.float32)
    for t in range(TC):
        for e in range(B):
            pm[t * 2 * B + e, e * TC + t] = 1.0
            pm[t * 2 * B + B + e, n + e * TC + (TC - 1 - t)] = 1.0
    return pm


def _scan_drive(uf_ref, ub_ref, pm_ref, b4_ref, fwd, dbuf, n):
    u_tok = jnp.concatenate([uf_ref[...].reshape(n, SW), ub_ref[...].reshape(n, SW)], axis=0).astype(MXU)
    u_tm = _dot(pm_ref[...], u_tok).astype(MXU)
    d4 = _dot(u_tm, b4_ref[...])
    dbuf[...] = jnp.where(fwd, d4[:, :2 * HW], d4[:, 2 * HW:])
    return u_tm


def _scan_fwd(p, pm, pmt, b4, a, c2, Lc):
    B, S, _ = p.shape
    nch, n = 2 * B, B * TC
    nk, ncc = S // TC, Lc // TC
    ucol = (INW - SW) // SW

    def body(uf_ref, ub_ref, pm_ref, pmt_ref, b4_ref, a_ref, c2_ref, yf_ref, yb_ref, hb_ref, hcar, dbuf, hbuf):
        @pl.when(pl.program_id(0) == 0)
        def _():
            hcar[...] = jnp.zeros_like(hcar)

        fwd = (lax.broadcasted_iota(jnp.int32, (2 * n, 1), 0) % nch) < B
        _scan_drive(uf_ref, ub_ref, pm_ref, b4_ref, fwd, dbuf, n)
        hb_ref[...] = hcar[...]
        ar, ai = a_ref[:, :HW], a_ref[:, HW:]

        def step(t, h):
            hr, hi = h
            r0 = pl.multiple_of(t * nch, nch)
            nr = ar * hr - ai * hi + dbuf[pl.ds(r0, nch), :HW]
            ni = ar * hi + ai * hr + dbuf[pl.ds(r0, nch), HW:]
            hbuf[pl.ds(r0, nch), :HW] = nr
            hbuf[pl.ds(r0, nch), HW:] = ni
            return nr, ni

        hr, hi = lax.fori_loop(0, TC, step, (hcar[:, :HW], hcar[:, HW:]))
        hcar[:, :HW] = hr
        hcar[:, HW:] = hi
        h_tok = _dot(pmt_ref[...], hbuf[...].astype(MXU)).astype(MXU)
        yf_ref[...] = _dot(h_tok[:n], c2_ref[:, :SW]).reshape(B, TC, SW)
        yb_ref[...] = _dot(h_tok[n:], c2_ref[:, SW:]).reshape(B, TC, SW)

    return pl.pallas_call(
        body, name="scan_fwd", grid=(nk,),
        in_specs=[BS((B, TC, SW), lambda k: (0, k, ucol)),
                  BS((B, TC, SW), lambda k: (0, _rev_chunk(k, nk, ncc), ucol)),
                  _full((2 * n, 2 * n)), _full((2 * n, 2 * n)),
                  _full((SW, 4 * HW)), _full((nch, 2 * HW)), _full((2 * HW, 2 * SW))],
        out_specs=[BS((B, TC, SW), lambda k: (0, k, 0)),
                   BS((B, TC, SW), lambda k: (0, _rev_chunk(k, nk, ncc), 0)),
                   BS((None, nch, 2 * HW), lambda k: (k, 0, 0))],
        out_shape=[SDS((B, S, SW), F32), SDS((B, S, SW), F32), SDS((nk, nch, 2 * HW), F32)],
        scratch_shapes=[pltpu.VMEM((nch, 2 * HW), F32), pltpu.VMEM((2 * n, 2 * HW), F32),
                        pltpu.VMEM((2 * n, 2 * HW), F32)],
        compiler_params=_cp(("arbitrary",), 48),
    )(p, p, pm, pmt, b4, a, c2)


def _scan_bwd(p, dy, hb, pm, pmt, b4, a, c2, Lc, ex=None):
    B, S, _ = p.shape
    nch, n = 2 * B, B * TC
    nk, ncc = S // TC, Lc // TC
    ucol = (INW - SW) // SW
    rows = 2 * n

    def body(uf_ref, ub_ref, dyf_ref, dyb_ref, hb_ref, pm_ref, pmt_ref, b4_ref, a_ref, c2_ref,
             duf_ref, dub_ref, db4_ref, da_ref, dc2_ref, gcar, dbuf, hbuf, gbuf):
        @pl.when(pl.program_id(0) == 0)
        def _():
            gcar[...] = jnp.zeros_like(gcar)
            db4_ref[...] = jnp.zeros_like(db4_ref)
            da_ref[...] = jnp.zeros_like(da_ref)
            dc2_ref[...] = jnp.zeros_like(dc2_ref)

        fwd = (lax.broadcasted_iota(jnp.int32, (rows, 1), 0) % nch) < B
        u_tm = _scan_drive(uf_ref, ub_ref, pm_ref, b4_ref, fwd, dbuf, n)
        ar, ai = a_ref[:, :HW], a_ref[:, HW:]
        hbuf[0:nch, :] = hb_ref[...]

        def fstep(t, h):
            hr, hi = h
            r0 = pl.multiple_of(t * nch, nch)
            nr = ar * hr - ai * hi + dbuf[pl.ds(r0, nch), :HW]
            ni = ar * hi + ai * hr + dbuf[pl.ds(r0, nch), HW:]
            hbuf[pl.ds(r0 + nch, nch), :HW] = nr
            hbuf[pl.ds(r0 + nch, nch), HW:] = ni
            return nr, ni

        lax.fori_loop(0, TC, fstep, (hb_ref[:, :HW], hb_ref[:, HW:]))

        dy_tok = jnp.concatenate([dyf_ref[...].reshape(n, SW), dyb_ref[...].reshape(n, SW)], axis=0).astype(MXU)
        dy_tm = _dot(pm_ref[...], dy_tok)
        dy2 = jnp.concatenate([jnp.where(fwd, dy_tm, 0.0), jnp.where(fwd, 0.0, dy_tm)], axis=1).astype(MXU)
        gbuf[...] = _dot_nt(dy2, c2_ref[...])
        dc2_ref[...] += _dot_tn(hbuf[nch:, :].astype(MXU), dy2)

        def bstep(i, cary):
            cr, ci = cary
            r0 = pl.multiple_of((TC - 1 - i) * nch, nch)
            gr = gbuf[pl.ds(r0, nch), :HW] + cr
            gi = gbuf[pl.ds(r0, nch), HW:] + ci
            gbuf[pl.ds(r0, nch), :HW] = gr
            gbuf[pl.ds(r0, nch), HW:] = gi
            return ar * gr + ai * gi, ar * gi - ai * gr

        cr, ci = lax.fori_loop(0, TC, bstep, (gcar[:, :HW], gcar[:, HW:]))
        gcar[:, :HW] = cr
        gcar[:, HW:] = ci

        gr = gbuf[:, :HW].reshape(TC, nch, HW)
        gi = gbuf[:, HW:].reshape(TC, nch, HW)
        hpr = hbuf[0:rows, :HW].reshape(TC, nch, HW)
        hpi = hbuf[0:rows, HW:].reshape(TC, nch, HW)
        da_ref[:, :HW] += jnp.sum(gr * hpr + gi * hpi, axis=0)
        da_ref[:, HW:] += jnp.sum(gi * hpr - gr * hpi, axis=0)

        g = gbuf[...]
        gm = g.astype(MXU)
        dd4 = jnp.concatenate([jnp.where(fwd, g, 0.0), jnp.where(fwd, 0.0, g)], axis=1).astype(MXU)
        db4_ref[...] += _dot_tn(u_tm, dd4)
        g_tok = _dot(pmt_ref[...], gm).astype(MXU)
        duf_ref[...] = _dot_nt(g_tok[:n], b4_ref[:, :2 * HW]).reshape(B, TC, SW)
        dub_ref[...] = _dot_nt(g_tok[n:], b4_ref[:, 2 * HW:]).reshape(B, TC, SW)

    fwd_blk = lambda col: BS((B, TC, SW), lambda k: (0, nk - 1 - k, col))
    bwd_blk = lambda col: BS((B, TC, SW), lambda k: (0, _rev_chunk(nk - 1 - k, nk, ncc), col))
    return _call_with(
        ex, body, "scan_bwd", (nk,),
        [fwd_blk(ucol), bwd_blk(ucol), fwd_blk(0), bwd_blk(0),
         BS((None, nch, 2 * HW), lambda k: (nk - 1 - k, 0, 0)),
         _full((rows, rows)), _full((rows, rows)),
         _full((SW, 4 * HW)), _full((nch, 2 * HW)), _full((2 * HW, 2 * SW))],
        [fwd_blk(0), bwd_blk(0), _full((SW, 4 * HW)), _full((nch, 2 * HW)), _full((2 * HW, 2 * SW))],
        [SDS((B, S, SW), F32), SDS((B, S, SW), F32), SDS((SW, 4 * HW), F32), SDS((nch, 2 * HW), F32),
         SDS((2 * HW, 2 * SW), F32)],
        [p, p, dy, dy, hb, pm, pmt, b4, a, c2],
        [pltpu.VMEM((nch, 2 * HW), F32), pltpu.VMEM((rows, 2 * HW), F32),
         pltpu.VMEM((rows + nch, 2 * HW), F32), pltpu.VMEM((rows, 2 * HW), F32)],
        _cp(("arbitrary",), 56))


def _post_mix(o, cv, yf, yb, p, h, modtok, ng, dsk, wg, bg, wo, nct):
    B, S, _ = h.shape

    def body(o_ref, cv_ref, yf_ref, yb_ref, u_ref, h_ref, mod_ref, g_ref, dsk_ref, wg_ref, bg_ref, wo_ref,
             h1_ref, mix_ref, m_ref, y_ref):
        y = yf_ref[...] + yb_ref[...] + dsk_ref[...] * u_ref[...]
        g = _gelu(y)
        s = g * _sigmoid(_dot(g.astype(MXU), wg_ref[...]) + bg_ref[...])
        mix_ref[:, 0:AW] = o_ref[...]
        mix_ref[:, AW:AW + CW] = cv_ref[...].astype(MXU)
        mix_ref[:, AW + CW:] = s.astype(MXU)
        m = _dot(mix_ref[...], wo_ref[...])
        n, _ = _rms(m)
        h1_ref[...] = h_ref[...] + mod_ref[2:3, :] * (n * g_ref[1:2, :])
        m_ref[...] = m
        y_ref[...] = y

    return pl.pallas_call(
        body, name="post_mix", grid=(B, S // TB),
        in_specs=[_tok(AW), _tok(CW), _tok(SW), _tok(SW), _tok(SW, (INW - SW) // SW), _tok(D), _mod_spec(nct),
                  _full((8, D)), _full((1, SW)), _full((SW, SW)), _full((1, SW)), _full((D, D))],
        out_specs=[_tok(D), _tok(D), _tok(D), _tok(SW)],
        out_shape=[SDS((B, S, D), F32), SDS((B, S, D), MXU), SDS((B, S, D), F32), SDS((B, S, SW), F32)],
        compiler_params=_cp(("arbitrary", "arbitrary"), 40),
    )(o, cv, yf, yb, p, h, modtok, ng, dsk, wg, bg, wo)


def _post_bwd(dh1, m, y, p, modtok, ng, dsk, wg, bg, wo, nct):
    B, S, _ = m.shape

    def body(dh_ref, m_ref, y_ref, u_ref, mod_ref, g_ref, dsk_ref, wg_ref, bg_ref, wo_ref,
             dm_ref, da_ref, dc_ref, dy_ref, du_ref, vec_ref, vec2_ref, dwg_ref):
        @pl.when(jnp.logical_and(pl.program_id(0) == 0, pl.program_id(1) == 0))
        def _():
            dwg_ref[...] = jnp.zeros_like(dwg_ref)

        n, r = _rms(m_ref[...])
        dh1 = dh_ref[...]
        gpm = g_ref[1:2, :]
        dr = dh1 * mod_ref[2:3, :]
        vec_ref[...] = jnp.zeros_like(vec_ref)
        vec_ref[0:1, :] = _colsum(dh1 * (n * gpm))
        vec_ref[1:2, :] = _colsum(dr * n)
        dm = _rms_bwd(dr * gpm, n, r).astype(MXU)
        dm_ref[...] = dm
        dmix = _dot_nt(dm, wo_ref[...])
        for g, t in enumerate(_to_groups([dmix[:, k * 128:(k + 1) * 128] for k in range(AW // 128)])):
            da_ref[g] = t
        dc_ref[...] = dmix[:, AW:AW + CW]
        ds = dmix[:, AW + CW:]
        yv = y_ref[...]
        g = _gelu(yv)
        gb = g.astype(MXU)
        sg = _sigmoid(_dot(gb, wg_ref[...]) + bg_ref[...])
        dv = ds * g * sg * (1.0 - sg)
        dvb = dv.astype(MXU)
        dg = ds * sg + _dot_nt(dvb, wg_ref[...])
        dwg_ref[...] += _dot_tn(gb, dvb)
        dy = dg * _gelu_grad(yv)
        dy_ref[...] = dy
        du_ref[...] = dy * dsk_ref[...]
        vec2_ref[...] = jnp.zeros_like(vec2_ref)
        vec2_ref[0:1, :] = _colsum(dy * u_ref[...])
        vec2_ref[1:2, :] = _colsum(dv)

    nt = S // TB
    return pl.pallas_call(
        body, name="post_bwd", grid=(B, nt),
        in_specs=[_tok(D), _tok(D), _tok(SW), _tok(SW, (INW - SW) // SW), _mod_spec(nct), _full((8, D)),
                  _full((1, SW)), _full((SW, SW)), _full((1, SW)), _full((D, D))],
        out_specs=[_tok(D), _grp(), _tok(CW), _tok(SW), _tok(SW),
                   BS((None, None, 8, D), lambda b, j: (b, j, 0, 0)),
                   BS((None, None, 8, SW), lambda b, j: (b, j, 0, 0)), _full((SW, SW))],
        out_shape=[SDS((B, S, D), MXU), SDS((B, QPK, S, 128), F32), SDS((B, S, CW), F32), SDS((B, S, SW), F32),
                   SDS((B, S, SW), F32), SDS((B, nt, 8, D), F32), SDS((B, nt, 8, SW), F32), SDS((SW, SW), F32)],
        compiler_params=_cp(("arbitrary", "arbitrary"), 40),
    )(dh1, m, y, p, modtok, ng, dsk, wg, bg, wo)


def _mlp_fwd(h1, modtok, ng, w1, w2, nct, ex=None):
    B, S, _ = h1.shape
    assert w1.shape == (DFF // TF, D, TF)

    def body(h_ref, mod_ref, g_ref, w1_ref, w2_ref, h2_ref, a2_ref, z_ref, f_ref):
        n, _ = _rms(h_ref[...])
        a2 = ((n * g_ref[2:3, :]) * (1.0 + mod_ref[4:5, :]) + mod_ref[3:4, :]).astype(MXU)
        a2_ref[...] = a2
        ff = jnp.zeros((TB, D), F32)
        for c in range(DFF // TF):
            cs = slice(c * TF, (c + 1) * TF)
            z = _dot(a2, w1_ref[c])
            z_ref[:, cs] = z.astype(MXU)
            r = jnp.maximum(z, 0.0)
            ff = ff + _dot((r * r).astype(MXU), w2_ref[cs, :])
        f_ref[...] = ff
        n, _ = _rms(ff)
        h2_ref[...] = h_ref[...] + mod_ref[5:6, :] * (n * g_ref[3:4, :])

    return _call_with(
        ex, body, "mlp_fwd", (B, S // TB),
        [_tok(D), _mod_spec(nct), _full((8, D)), _resident((DFF // TF, D, TF)), _resident((DFF, D))],
        [_tok(D), _tok(D), _tok(DFF), _tok(D)],
        [SDS((B, S, D), F32), SDS((B, S, D), MXU), SDS((B, S, DFF), MXU), SDS((B, S, D), F32)],
        [h1, modtok, ng, w1, w2], [], _cp(("arbitrary", "arbitrary"), 48))


def _mlp_bwd(dh2, h1, f, z, modtok, ng, w1, w2, nct, ex=None):
    B, S, _ = h1.shape

    def body(dh_ref, h_ref, f_ref, z_ref, mod_ref, g_ref, w1_ref, w2_ref, dh1_ref, df_ref, dz_ref, vec_ref):
        n, r = _rms(f_ref[...])
        dh2 = dh_ref[...]
        gp = g_ref[3:4, :]
        dr = dh2 * mod_ref[5:6, :]
        vec_ref[...] = jnp.zeros_like(vec_ref)
        vec_ref[3:4, :] = _colsum(dh2 * (n * gp))
        vec_ref[4:5, :] = _colsum(dr * n)
        df = _rms_bwd(dr * gp, n, r).astype(MXU)
        df_ref[...] = df
        da = jnp.zeros((TB, D), F32)
        for c in range(DFF // TF):
            cs = slice(c * TF, (c + 1) * TF)
            dr2 = _dot_nt(df, w2_ref[cs, :])
            dz = (dr2 * 2.0 * jnp.maximum(z_ref[:, cs].astype(F32), 0.0)).astype(MXU)
            dz_ref[:, cs] = dz
            da = da + _dot_nt(dz, w1_ref[c])
        n, r = _rms(h_ref[...])
        g = g_ref[2:3, :]
        sc1 = 1.0 + mod_ref[4:5, :]
        vec_ref[0:1, :] = _colsum(da)
        vec_ref[1:2, :] = _colsum(da * (n * g))
        vec_ref[2:3, :] = _colsum(da * sc1 * n)
        dh1_ref[...] = dh2 + _rms_bwd(da * sc1 * g, n, r)

    return _call_with(
        ex, body, "mlp_bwd", (B, S // TB),
        [_tok(D), _tok(D), _tok(D), _tok(DFF), _mod_spec(nct), _full((8, D)),
         _resident((DFF // TF, D, TF)), _resident((DFF, D))],
        [_tok(D), _tok(D), _tok(DFF), BS((None, None, 8, D), lambda b, j: (b, j, 0, 0))],
        [SDS((B, S, D), F32), SDS((B, S, D), MXU), SDS((B, S, DFF), MXU), SDS((B, S // TB, 8, D), F32)],
        [dh2, h1, f, z, modtok, ng, w1, w2], [], _cp(("arbitrary", "arbitrary"), 56))


def _loss(h, target, Lc):
    B, S, _ = h.shape
    nt, nct = S // TB, Lc // TB

    def body(h_ref, t_ref, dh_ref, l_ref):
        lat = pl.program_id(1) >= nct
        err = jnp.where(lat, h_ref[...] - t_ref[...], 0.0)
        dh_ref[...] = err * (1.0 / D)
        l_ref[...] = jnp.broadcast_to(jnp.sum(err * err, keepdims=True), (8, 128))

    return pl.pallas_call(
        body, name="loss", grid=(B, nt),
        in_specs=[_tok(D), BS((None, TB, D), lambda b, j: (b, jnp.maximum(j - nct, 0), 0))],
        out_specs=[_tok(D), BS((None, None, 8, 128), lambda b, j: (b, j, 0, 0))],
        out_shape=[SDS((B, S, D), F32), SDS((B, nt, 8, 128), F32)],
        compiler_params=_cp(("arbitrary", "arbitrary")),
    )(h, target)


def _mm_tn(a, b, name, relu2=False):
    T, M = a.shape
    N = b.shape[1]
    def blk(n):
        return max(b for b in range(128, 1025, 128) if n % b == 0)

    bm, bn = blk(M), blk(N)
    tk = next(t for t in (1536, 1024, 512, 256, T) if T % t == 0)
    nkk = T // tk

    def body(a_ref, b_ref, o_ref, om_ref):
        k = pl.program_id(2)

        @pl.when(k == 0)
        def _():
            o_ref[...] = jnp.zeros_like(o_ref)

        av = a_ref[...]
        if relu2:
            r = jnp.maximum(av.astype(F32), 0.0)
            av = (r * r).astype(MXU)
        o_ref[...] += _dot_tn(av, b_ref[...])

        @pl.when(k == nkk - 1)
        def _():
            om_ref[...] = o_ref[...].astype(MXU)

    out = BS((bm, bn), lambda i, j, k: (i, j))
    return pl.pallas_call(
        body, name=name, grid=(M // bm, N // bn, nkk),
        in_specs=[BS((tk, bm), lambda i, j, k: (k, i)), BS((tk, bn), lambda i, j, k: (k, j))],
        out_specs=[out, out], out_shape=[SDS((M, N), F32), SDS((M, N), MXU)],
        compiler_params=_cp(("arbitrary", "arbitrary", "arbitrary"), 56),
    )(a, b)


_PAD_ROWS = 64


def _block_rows(R):
    for br in (256, _PAD_ROWS):
        if R % br == 0:
            return br
    raise ValueError(f"row count {R} is not a multiple of {_PAD_ROWS}")


def _adamw(w, ga, gb, m, v, name):
    R, C = w.shape
    br = _block_rows(R) if C <= _LANES else _PAD_ROWS

    def body(w_ref, ga_ref, gb_ref, m_ref, v_ref, g_out, d_out, m_out, v_out):
        g = ga_ref[...] + gb_ref[...]
        m2 = B1 * m_ref[...] + (1.0 - B1) * g
        v2 = B2 * v_ref[...] + (1.0 - B2) * (g * g)
        m_hat = m2 / (1.0 - B1 ** STEP)
        v_hat = v2 / (1.0 - B2 ** STEP)
        g_out[...] = g
        d_out[...] = -LR * (m_hat / (jnp.sqrt(v_hat) + AEPS) + WD * w_ref[...])
        m_out[...] = m2
        v_out[...] = v2

    own = BS((br, C), lambda i: (i, 0))
    return pl.pallas_call(
        body, name=name, grid=(R // br,),
        in_specs=[own] * 5, out_specs=[own] * 4, out_shape=[SDS((R, C), F32)] * 4,
        compiler_params=_cp(("arbitrary",)),
    )(w, ga, gb, m, v)


def _my_xyc():
    return lax.axis_index("x"), lax.axis_index("y"), lax.axis_index("c")


def _chip_peers(x, y):
    return [(1 - x, y), (x, 1 - y), (1 - x, 1 - y)]


class _Gather:
    def __init__(self, shards):
        self.args = list(shards)
        self.n = len(self.args)
        self.specs = [ANY] * self.n
        self.out_shape = [SDS((NCHIP,) + a.shape, a.dtype) for a in self.args]
        self.sems = [pltpu.SemaphoreType.DMA((3 * self.n,)), pltpu.SemaphoreType.DMA((3 * self.n,)),
                     pltpu.SemaphoreType.DMA((self.n,))]

    def _copies(self, ins, outs, ssem, rsem, lsem):
        x, y, c = _my_xyc()
        mine, sends, recvs = [], [], []
        for k, (s_ref, o_ref) in enumerate(zip(ins, outs)):
            mine.append(pltpu.make_async_copy(s_ref, o_ref.at[2 * x + y], lsem.at[k]))
            for j, (px, py) in enumerate(_chip_peers(x, y)):
                sems = dict(send_sem=ssem.at[3 * k + j], recv_sem=rsem.at[3 * k + j], device_id=(px, py, c),
                            device_id_type=MESH)
                sends.append(pltpu.make_async_remote_copy(src_ref=s_ref, dst_ref=o_ref.at[2 * x + y], **sems))
                recvs.append(pltpu.make_async_remote_copy(src_ref=s_ref, dst_ref=o_ref.at[2 * px + py], **sems))
        return mine, sends, recvs

    def start(self, ins, outs, sems):
        mine, sends, _ = self._copies(ins, outs, *sems)
        for cp in mine + sends:
            cp.start()

    def wait(self, ins, outs, sems):
        mine, sends, recvs = self._copies(ins, outs, *sems)
        for cp in recvs:
            cp.wait_recv()
        for cp in sends:
            cp.wait_send()
        for cp in mine:
            cp.wait()


class _Scatter:
    def __init__(self, sends):
        self.args = list(sends)
        self.n = len(self.args)
        self.specs = [ANY] * self.n
        self.out_shape = [SDS((3,) + a.shape[1:], a.dtype) for a in self.args]
        self.sems = [pltpu.SemaphoreType.DMA((3 * self.n,)), pltpu.SemaphoreType.DMA((3 * self.n,))]

    def _copies(self, ins, outs, ssem, rsem):
        x, y, c = _my_xyc()
        return [pltpu.make_async_remote_copy(
            src_ref=s_ref.at[2 * px + py], dst_ref=o_ref.at[j], send_sem=ssem.at[3 * k + j],
            recv_sem=rsem.at[3 * k + j], device_id=(px, py, c), device_id_type=MESH)
            for k, (s_ref, o_ref) in enumerate(zip(ins, outs)) for j, (px, py) in enumerate(_chip_peers(x, y))]

    def start(self, ins, outs, sems):
        for cp in self._copies(ins, outs, *sems):
            cp.start()

    def wait(self, ins, outs, sems):
        for cp in self._copies(ins, outs, *sems):
            cp.wait()


class _Swap(_Scatter):
    def __init__(self, arrays):
        self.args = list(arrays)
        self.n = len(self.args)
        self.specs = [ANY] * self.n
        self.out_shape = [SDS(a.shape, a.dtype) for a in self.args]
        self.sems = [pltpu.SemaphoreType.DMA((self.n,)), pltpu.SemaphoreType.DMA((self.n,))]

    def _copies(self, ins, outs, ssem, rsem):
        x, y, c = _my_xyc()
        return [pltpu.make_async_remote_copy(src_ref=s_ref, dst_ref=o_ref, send_sem=ssem.at[k], recv_sem=rsem.at[k],
                                             device_id=(x, y, 1 - c), device_id_type=MESH)
                for k, (s_ref, o_ref) in enumerate(zip(ins, outs))]


def _exchange_call(ex, name):
    n = ex.n

    def body(*refs):
        ins, outs, sems = refs[:n], refs[n:2 * n], refs[2 * n:]
        ex.start(ins, outs, sems)
        ex.wait(ins, outs, sems)

    return pl.pallas_call(body, name=name, in_specs=ex.specs, out_specs=ex.specs, out_shape=ex.out_shape,
                          scratch_shapes=ex.sems)(*ex.args)


def _carried(ex, refs, n_in, n_out):
    n = ex.n
    ins, cin = refs[:n_in], refs[n_in:n_in + n]
    outs, cout = refs[n_in + n:n_in + n + n_out], refs[n_in + n + n_out:n_in + 2 * n + n_out]
    rest = refs[n_in + 2 * n + n_out:]
    nsem = len(ex.sems)
    return list(ins) + list(outs) + list(rest[:len(rest) - nsem]), (cin, cout, rest[len(rest) - nsem:])


def _gather_all(v, name):
    R, C = v.shape

    def body(s_ref, o_ref, ssem, rsem, lsem):
        x, y, c = _my_xyc()
        me = 4 * x + 2 * y + c
        mine = pltpu.make_async_copy(s_ref, o_ref.at[me], lsem)
        mine.start()

        def peer(j):
            fx, fy, fc = (j >> 2) & 1, (j >> 1) & 1, j & 1
            return (x ^ fx, y ^ fy, c ^ fc)

        cps = []
        for j in range(1, 8):
            cps.append(pltpu.make_async_remote_copy(
                src_ref=s_ref, dst_ref=o_ref.at[me], send_sem=ssem.at[j - 1], recv_sem=rsem.at[j - 1],
                device_id=peer(j), device_id_type=MESH))
            cps[-1].start()
        for j in range(1, 8):
            px, py, pc = peer(j)
            pltpu.make_async_remote_copy(
                src_ref=s_ref, dst_ref=o_ref.at[4 * px + 2 * py + pc], send_sem=ssem.at[j - 1],
                recv_sem=rsem.at[j - 1], device_id=peer(j), device_id_type=MESH).wait_recv()
        for cp in cps:
            cp.wait_send()
        mine.wait()

    return pl.pallas_call(
        body, name=name, in_specs=[ANY], out_specs=ANY, out_shape=SDS((8, R, C), v.dtype),
        scratch_shapes=[pltpu.SemaphoreType.DMA((7,)), pltpu.SemaphoreType.DMA((7,)), pltpu.SemaphoreType.DMA],
    )(v)


def _sum_slots(own, slots, name):
    n, R, C = slots.shape
    br = _block_rows(R)

    def body(*refs):
        o_ref = refs[-1]
        if own is None:
            acc = refs[0][0].astype(F32)
            first = 1
            s_ref = refs[0]
        else:
            acc = refs[0][...]
            first = 0
            s_ref = refs[1]
        for k in range(first, n):
            acc = acc + s_ref[k].astype(F32)
        o_ref[...] = acc

    row = BS((br, C), lambda i: (i, 0))
    slab = BS((n, br, C), lambda i: (0, i, 0))
    ins, args = ([slab], [slots]) if own is None else ([row, slab], [own, slots])
    return pl.pallas_call(
        body, name=name, grid=(R // br,), in_specs=ins, out_specs=row, out_shape=SDS((R, C), F32),
        compiler_params=_cp(("arbitrary",)),
    )(*args)


def _rope_tables(L, Lc):
    n = jnp.arange(L)
    row = (n // GRID_W).astype(F32)
    col = (n % GRID_W).astype(F32)
    freqs = ROPE_BASE ** (-jnp.arange(16, dtype=F32) / 16)
    lane = jnp.arange(128)
    dd = lane % HD
    fr = freqs[dd % 16]
    ang = jnp.where(dd < 32, row[:, None], col[:, None]) * fr[None, :]
    sign = jnp.where((dd % 32) < 16, -1.0, 1.0)
    cos = jnp.concatenate([jnp.ones((Lc, 128), F32), jnp.cos(ang)], axis=0)
    sin = jnp.concatenate([jnp.zeros((Lc, 128), F32), jnp.sin(ang) * sign[None, :]], axis=0)
    return cos, sin


def _ssm_prep(lam_re, lam_im, log_dt, b_re, b_im, c_re, c_im, B):
    dt = jnp.exp(log_dt)[..., None]
    mag = jnp.exp(lam_re * dt)
    ar = mag * jnp.cos(lam_im * dt)
    ai = mag * jnp.sin(lam_im * dt)
    den = lam_re * lam_re + lam_im * lam_im
    kr = ((ar - 1.0) * lam_re + ai * lam_im) / den
    ki = (ai * lam_re - (ar - 1.0) * lam_im) / den
    bbr = kr[..., None] * b_re - ki[..., None] * b_im
    bbi = kr[..., None] * b_im + ki[..., None] * b_re
    eye = jnp.eye(SGRP, dtype=F32)

    def bblk(t):
        return jnp.einsum("gpi,gh->gihp", t, eye).reshape(SGRP * SCH, SGRP * SST)

    def cblk(t):
        return jnp.einsum("gip,gh->gphi", t, eye).reshape(SGRP * SST, SGRP * SCH)

    b4 = jnp.concatenate([bblk(bbr[0]), bblk(bbi[0]), bblk(bbr[1]), bblk(bbi[1])], axis=1)
    c2 = jnp.concatenate([jnp.concatenate([cblk(c_re[0]), -cblk(c_im[0])], axis=0),
                          jnp.concatenate([cblk(c_re[1]), -cblk(c_im[1])], axis=0)], axis=1)
    a2 = jnp.concatenate([ar.reshape(2, -1), ai.reshape(2, -1)], axis=1)
    a = jnp.repeat(a2, B, axis=0)
    return a, b4, c2


def _pad_rows(v, rows=8):
    return jnp.concatenate([v, jnp.zeros((rows - v.shape[0],) + v.shape[1:], v.dtype)], axis=0)


_BIG = ("w_ada", "w_in", "w_out", "w_mlp_in", "w_mlp_out")
_BIG_AXIS = {"w_ada": 2, "w_in": 2, "w_out": 1, "w_mlp_in": 2, "w_mlp_out": 1}


def _split_shards(g, axis):
    L, r, c = g.shape
    if axis == 2:
        return jnp.transpose(g.reshape(L, r, NCHIP, c // NCHIP), (2, 0, 1, 3))
    return jnp.transpose(g.reshape(L, NCHIP, r // NCHIP, c), (1, 0, 2, 3))


def _join_shards(s, axis):
    _, L, r, c = s.shape
    if axis == 2:
        return jnp.transpose(s, (1, 2, 0, 3)).reshape(L, r, NCHIP * c)
    return jnp.transpose(s, (1, 0, 2, 3)).reshape(L, NCHIP * r, c)


def _local_step(x, c, ctx, loss_target, P, ex):
    B, L, _ = x.shape
    Lc = ctx.shape[1]
    S = Lc + L
    nct = Lc // TB
    nt = S // TB
    nch = 2 * B
    cos, sin = _rope_tables(L, Lc)
    perm = _scan_perm(B)
    pm, pmt = jnp.asarray(perm, MXU), jnp.asarray(perm.T, MXU)

    c_act = jax.nn.silu(c)
    cc_act, silu_vjp = jax.vjp(jax.nn.silu, P["c_ctx"])
    cact8 = _pad_rows(jnp.concatenate([c_act, cc_act[None, :]], axis=0))
    h = jnp.concatenate([ctx, x], axis=1)

    saved = []
    for l in range(NLAYER):
        Wl = ex.weights(l)
        mod8 = _ada_fwd(cact8, Wl["w_ada"], P["b_ada"][l][None, :])
        mod6 = mod8.reshape(8, NMOD, D)
        modtok = jnp.stack([jnp.broadcast_to(mod6[B], (B, NMOD, D)), mod6[:B]], axis=1)
        modtok = jnp.concatenate([modtok, jnp.zeros((B, 2, 2, D), F32)], axis=2)
        ng = _pad_rows(P["norm_g"][l])
        cw8 = _pad_rows(P["conv_w"][l])
        (a_s, b4, c2), prep_vjp = jax.vjp(
            lambda *t: _ssm_prep(*t, B), P["ssm_lam_re"][l], P["ssm_lam_im"][l], P["ssm_log_dt"][l],
            P["ssm_b_re"][l], P["ssm_b_im"][l], P["ssm_c_re"][l], P["ssm_c_im"][l])
        b4m, c2m = b4.astype(MXU), c2.astype(MXU)
        wg = P["w_glu"][l].astype(MXU)
        dsk = P["ssm_d"][l][None, :]
        bg = P["b_glu"][l][None, :]

        p, a = _pre_mix(h, modtok, ng, Wl["w_in"], nct)
        q4, kr = _rope_fwd(p, cos, sin)
        (o,) = ex.attn_fwd(l, P["attn_sink"][l], q4, kr, p, Lc)
        cv = _conv_fwd(p, cw8, nct)
        yf, yb, hb = _scan_fwd(p, pm, pmt, b4m, a_s, c2m, Lc)
        h1, mix, m, y = _post_mix(o, cv, yf, yb, p, h, modtok, ng, dsk, wg, bg, Wl["w_out"], nct)
        h2, a2, z, f = ex.mlp_fwd(l, h1, modtok, ng, Wl["w_mlp_in"], Wl["w_mlp_out"], nct)
        saved.append(dict(Wl=Wl, h=h, modtok=modtok, ng=ng, cw8=cw8, a_s=a_s, b4m=b4m, c2m=c2m, wg=wg, dsk=dsk, bg=bg,
                          p=p, a=a, q4=q4, kr=kr, hb=hb, mix=mix, m=m, y=y, h1=h1, a2=a2, z=z, f=f,
                          prep_vjp=prep_vjp))
        h = h2

    dh, lpart = _loss(h, loss_target, Lc)
    loss_local = 0.5 / D * jnp.sum(lpart[:, :, 0, 0])

    G = {k: [None] * NLAYER for k in _SMALL if k != "c_ctx"}
    dcact = jnp.zeros((8, D), F32)
    T = B * S
    for l in reversed(range(NLAYER)):
        sv = saved[l]
        modtok, ng, Wl, GB = sv["modtok"], sv["ng"], sv["Wl"], {}
        dh1, df, dz, vec_m = ex.mlp_bwd(l, dh, sv["h1"], sv["f"], sv["z"], modtok, ng, Wl["w_mlp_in"],
                                        Wl["w_mlp_out"], nct)
        GB["w_mlp_out"] = _mm_tn(sv["z"].reshape(T, DFF), df.reshape(T, D), "dw_mlp_out", relu2=True)
        GB["w_mlp_in"] = _mm_tn(sv["a2"].reshape(T, D), dz.reshape(T, DFF), "dw_mlp_in")
        dm, dattn, dcv, dy, du_skip, vec_p, vec_p2, dwg = _post_bwd(
            dh1, sv["m"], sv["y"], sv["p"], modtok, ng, sv["dsk"], sv["wg"], sv["bg"], Wl["w_out"], nct)
        GB["w_out"] = _mm_tn(sv["mix"].reshape(T, D), dm.reshape(T, D), "dw_out")
        G["w_glu"][l] = dwg
        G["ssm_d"][l] = jnp.sum(vec_p2[:, :, 0, :], axis=(0, 1))
        G["b_glu"][l] = jnp.sum(vec_p2[:, :, 1, :], axis=(0, 1))
        duf, dub, db4, da_s, dc2 = ex.scan_bwd(l, sv["p"], dy, sv["hb"], pm, pmt, sv["b4m"], sv["a_s"], sv["c2m"],
                                               Lc)
        gs = sv["prep_vjp"]((da_s, db4, dc2))
        for k, name in enumerate(("ssm_lam_re", "ssm_lam_im", "ssm_log_dt", "ssm_b_re", "ssm_b_im", "ssm_c_re",
                                  "ssm_c_im")):
            G[name][l] = gs[k]
        dconv, dcw = _conv_bwd(sv["p"], dcv, sv["cw8"], nct)
        G["conv_w"][l] = jnp.sum(dcw[:, :, 0:3, :], axis=(0, 1))
        dq_r, dk_r, dv, dsink = _attn_bwd(P["attn_sink"][l], sv["q4"], sv["kr"], sv["p"], dattn, Lc)
        G["attn_sink"][l] = jnp.sum(dsink[:, :, 0], axis=0)
        dqk = _rope_bwd(dq_r, dk_r, cos, -sin)
        dh, vec_i, dp = _pre_bwd(dqk, dv, dconv, (du_skip, duf, dub), dh1, sv["h"], modtok, ng, Wl["w_in"], nct)
        GB["w_in"] = _mm_tn(sv["a"].reshape(T, D), dp.reshape(T, INW), "dw_in")
        G["norm_g"][l] = jnp.stack([jnp.sum(vec_i[:, :, 2], axis=(0, 1)), jnp.sum(vec_p[:, :, 1], axis=(0, 1)),
                                    jnp.sum(vec_m[:, :, 2], axis=(0, 1)), jnp.sum(vec_m[:, :, 4], axis=(0, 1))])
        dmod_blk = jnp.stack([vec_i[:, :, 0], vec_i[:, :, 1], vec_p[:, :, 0],
                              vec_m[:, :, 0], vec_m[:, :, 1], vec_m[:, :, 3]], axis=2)
        dmod_lat = jnp.sum(dmod_blk[:, nct:], axis=1).reshape(B, NMOD * D)
        dmod_ctx = jnp.sum(dmod_blk[:, :nct], axis=(0, 1)).reshape(1, NMOD * D)
        dmod8 = _pad_rows(jnp.concatenate([dmod_lat, dmod_ctx], axis=0))
        G["b_ada"][l] = jnp.sum(dmod8, axis=0)
        dmod8m = dmod8.astype(MXU)
        GB["w_ada"] = _mm_tn(cact8.astype(MXU), dmod8m, "dw_ada")
        dcact = dcact + _ada_bwd_c(dmod8m, Wl["w_ada"])
        ex.submit(l, GB)

    grads = {k: jnp.stack(v) for k, v in G.items()}
    grads["c_ctx"] = silu_vjp(dcact[B])[0]
    return loss_local, dh[:, Lc:, :], grads


_WEIGHTS = ["c_ctx", "w_ada", "b_ada", "norm_g", "w_in", "conv_w", "attn_sink", "ssm_lam_re", "ssm_lam_im",
            "ssm_log_dt", "ssm_b_re", "ssm_b_im", "ssm_c_re", "ssm_c_im", "ssm_d", "w_glu", "b_glu", "w_out",
            "w_mlp_in", "w_mlp_out"]
_SMALL = [n for n in _WEIGHTS if n not in _BIG]
_SMALL_SHARDED = {"norm_g": 2, "conv_w": 2, "w_glu": 1}
_LANES = 1024


_GROUPS = (("w_ada", "w_in", "w_out"), ("w_mlp_in", "w_mlp_out"))


class _Exchange:
    def __init__(self, W, q_me):
        self.W, self.q_me = W, q_me
        self.wall, self.own, self.recv, self.pending = {}, {}, {}, {}
        self._keep(0, _BIG, _exchange_call(_Gather([self._shard(0, n) for n in _BIG]), "gather_big"))

    def _shard(self, l, n):
        return self.W[n][l].astype(MXU)

    def _keep(self, l, names, walls):
        for n, w in zip(names, walls):
            self.wall[l, n] = w

    def weights(self, l):
        g = {n: self.wall[l, n] for n in _BIG}
        return dict(w_ada=g["w_ada"], w_mlp_in=g["w_mlp_in"], w_in=_join_shards(g["w_in"][:, None], 2)[0],
                    w_out=g["w_out"].reshape(-1, g["w_out"].shape[-1]),
                    w_mlp_out=g["w_mlp_out"].reshape(-1, g["w_mlp_out"].shape[-1]))

    def _gathering(self, l, grp, fn, *args):
        if l + 1 == NLAYER:
            return fn(*args)[0]
        res, walls = fn(*args, ex=_Gather([self._shard(l + 1, n) for n in _GROUPS[grp]]))
        self._keep(l + 1, _GROUPS[grp], walls)
        return res

    def attn_fwd(self, l, *args):
        return self._gathering(l, 0, _attn_fwd, *args)

    def mlp_fwd(self, l, *args):
        return self._gathering(l, 1, _mlp_fwd, *args)

    def submit(self, l, G):
        for grp, names in enumerate(_GROUPS):
            sends = []
            for n in names:
                g, gm = G[n]
                r, c = g.shape
                if _BIG_AXIS[n] == 2:
                    sends.append(jnp.transpose(gm.reshape(r, NCHIP, c // NCHIP), (1, 0, 2)))
                    self.own[l, n] = lax.dynamic_slice_in_dim(g, self.q_me * (c // NCHIP), c // NCHIP, axis=1)
                else:
                    sends.append(gm.reshape(NCHIP, r // NCHIP, c))
                    self.own[l, n] = lax.dynamic_slice_in_dim(g, self.q_me * (r // NCHIP), r // NCHIP, axis=0)
            self.pending[grp] = (l, sends)

    def _scattering(self, grp, fn, *args):
        if grp not in self.pending:
            return fn(*args)[0]
        lp, sends = self.pending.pop(grp)
        res, recvs = fn(*args, ex=_Scatter(sends))
        for n, r in zip(_GROUPS[grp], recvs):
            self.recv[lp, n] = r
        return res

    def mlp_bwd(self, l, *args):
        return self._scattering(0, _mlp_bwd, *args)

    def scan_bwd(self, l, *args):
        return self._scattering(1, _scan_bwd, *args)

    def finish(self):
        (lp, s0), (_, s1) = self.pending.pop(0), self.pending.pop(1)
        for n, r in zip(_BIG, _exchange_call(_Scatter(s0 + s1), "scatter_big")):
            self.recv[lp, n] = r
        part = [jnp.concatenate([_sum_slots(self.own[l, n], self.recv[l, n], "sum_chips") for l in range(NLAYER)],
                                axis=0) for n in _BIG]
        sib = _exchange_call(_Swap(part), "swap_big")
        return {n: (p, s) for n, p, s in zip(_BIG, part, sib)}


def _part_rows(shape):
    return -(-math.prod(shape) // _LANES)


def _flat_pad(parts):
    rows = []
    for p in parts:
        v = p.reshape(-1)
        pad = _part_rows(p.shape) * _LANES - v.shape[0]
        rows.append(jnp.concatenate([v, jnp.zeros((pad,), v.dtype)]).reshape(-1, _LANES))
    used = sum(r.shape[0] for r in rows)
    rows.append(jnp.zeros((-used % _PAD_ROWS, _LANES), rows[0].dtype))
    return jnp.concatenate(rows, axis=0)


def _unflat(flat, shapes):
    out, r0 = [], 0
    for s in shapes:
        nr = _part_rows(s)
        out.append(flat[r0:r0 + nr].reshape(-1)[:math.prod(s)].reshape(s))
        r0 += nr
    return out


def kernel(x, c, ctx, c_ctx, w_ada, b_ada, norm_g, w_in, conv_w, attn_sink, ssm_lam_re, ssm_lam_im, ssm_log_dt, ssm_b_re, ssm_b_im, ssm_c_re, ssm_c_im, ssm_d, w_glu, b_glu, w_out, w_mlp_in, w_mlp_out, loss_target, m_c_ctx, m_w_ada, m_b_ada, m_norm_g, m_w_in, m_conv_w, m_attn_sink, m_ssm_lam_re, m_ssm_lam_im, m_ssm_log_dt, m_ssm_b_re, m_ssm_b_im, m_ssm_c_re, m_ssm_c_im, m_ssm_d, m_w_glu, m_b_glu, m_w_out, m_w_mlp_in, m_w_mlp_out, v_c_ctx, v_w_ada, v_b_ada, v_norm_g, v_w_in, v_conv_w, v_attn_sink, v_ssm_lam_re, v_ssm_lam_im, v_ssm_log_dt, v_ssm_b_re, v_ssm_b_im, v_ssm_c_re, v_ssm_c_im, v_ssm_d, v_w_glu, v_b_glu, v_w_out, v_w_mlp_in, v_w_mlp_out):
    W = dict(c_ctx=c_ctx, w_ada=w_ada, b_ada=b_ada, norm_g=norm_g, w_in=w_in, conv_w=conv_w, attn_sink=attn_sink,
             ssm_lam_re=ssm_lam_re, ssm_lam_im=ssm_lam_im, ssm_log_dt=ssm_log_dt, ssm_b_re=ssm_b_re,
             ssm_b_im=ssm_b_im, ssm_c_re=ssm_c_re, ssm_c_im=ssm_c_im, ssm_d=ssm_d, w_glu=w_glu, b_glu=b_glu,
             w_out=w_out, w_mlp_in=w_mlp_in, w_mlp_out=w_mlp_out)
    M = dict(c_ctx=m_c_ctx, w_ada=m_w_ada, b_ada=m_b_ada, norm_g=m_norm_g, w_in=m_w_in, conv_w=m_conv_w,
             attn_sink=m_attn_sink, ssm_lam_re=m_ssm_lam_re, ssm_lam_im=m_ssm_lam_im, ssm_log_dt=m_ssm_log_dt,
             ssm_b_re=m_ssm_b_re, ssm_b_im=m_ssm_b_im, ssm_c_re=m_ssm_c_re, ssm_c_im=m_ssm_c_im, ssm_d=m_ssm_d,
             w_glu=m_w_glu, b_glu=m_b_glu, w_out=m_w_out, w_mlp_in=m_w_mlp_in, w_mlp_out=m_w_mlp_out)
    V = dict(c_ctx=v_c_ctx, w_ada=v_w_ada, b_ada=v_b_ada, norm_g=v_norm_g, w_in=v_w_in, conv_w=v_conv_w,
             attn_sink=v_attn_sink, ssm_lam_re=v_ssm_lam_re, ssm_lam_im=v_ssm_lam_im, ssm_log_dt=v_ssm_log_dt,
             ssm_b_re=v_ssm_b_re, ssm_b_im=v_ssm_b_im, ssm_c_re=v_ssm_c_re, ssm_c_im=v_ssm_c_im, ssm_d=v_ssm_d,
             w_glu=v_w_glu, b_glu=v_b_glu, w_out=v_w_out, w_mlp_in=v_w_mlp_in, w_mlp_out=v_w_mlp_out)
    q_me = 2 * lax.axis_index("x") + lax.axis_index("y")

    ex = _Exchange(W, q_me)
    P = {n: W[n] for n in _SMALL}
    ssh_names = list(_SMALL_SHARDED)
    ssh = _flat_pad([W[n] for n in ssh_names])
    (sall,) = _exchange_call(_Gather([ssh]), "gather_small")
    parts = [_unflat(sall[q], [W[n].shape for n in ssh_names]) for q in range(NCHIP)]
    for k, n in enumerate(ssh_names):
        P[n] = _join_shards(jnp.stack([parts[q][k] for q in range(NCHIP)]), _SMALL_SHARDED[n])

    loss_local, grad_x, G = _local_step(x, c, ctx, loss_target, P, ex)
    loss = lax.psum(loss_local, ("x", "y", "c"))

    sums = ex.finish()
    out = {}
    for n in _BIG:
        flat = lambda t: t.reshape(-1, t.shape[-1])
        res = _adamw(flat(W[n]), *sums[n], flat(M[n]), flat(V[n]), "adamw_" + n)
        out[n] = [t.reshape(W[n].shape) for t in res]

    gsmall = _flat_pad([G[n] for n in _SMALL]).astype(MXU)
    gsum = _sum_slots(None, _gather_all(gsmall, "gather_grads"), "sum_devices")
    gfull = dict(zip(_SMALL, _unflat(gsum, [G[n].shape for n in _SMALL])))
    for n, ax in _SMALL_SHARDED.items():
        width = W[n].shape[ax]
        gfull[n] = lax.dynamic_slice_in_dim(gfull[n], q_me * width, width, axis=ax)
    shapes = [W[n].shape for n in _SMALL]
    gflat = _flat_pad([gfull[n] for n in _SMALL])
    res = _adamw(_flat_pad([W[n] for n in _SMALL]), gflat, jnp.zeros_like(gflat),
                 _flat_pad([M[n] for n in _SMALL]), _flat_pad([V[n] for n in _SMALL]), "adamw_small")
    for k, t in enumerate(res):
        for n, piece in zip(_SMALL, _unflat(t, shapes)):
            out.setdefault(n, [None] * 4)[k] = piece

    return (loss, grad_x, *[out[n][0] for n in _WEIGHTS], *[out[n][1] for n in _WEIGHTS],
            *[out[n][2] for n in _WEIGHTS], *[out[n][3] for n in _WEIGHTS])
```

```python
import functools
import math

import jax
import jax.numpy as jnp
import numpy as np
from jax import lax
from jax.experimental import pallas as pl
from jax.experimental.pallas import tpu as pltpu

F32 = jnp.float32
MXU = jnp.bfloat16

D = 1024
DFF = 4096
NMOD = 6
EPS = 1e-6
HD = 64
NQ = 8
NKV = 2
QPK = 4
AW = 512
KVW = 128
WIN = 128
QB = 128
CW = 256
SW = 256
SGRP = 16
SCH = 16
SST = 64
INW = 1792
GRID_W = 64
ROPE_BASE = 10000.0
NEG = -1e30
SCALE = HD ** -0.5
NLAYER = 4

TB = 256
TF = 1024
TC = 32
NCHIP = 4

LR, B1, B2, AEPS, WD, STEP = 0.001, 0.9, 0.999, 1e-08, 0.01, 10

MESH = pl.DeviceIdType.MESH
SDS = jax.ShapeDtypeStruct
BS = pl.BlockSpec
ANY = pl.BlockSpec(memory_space=pl.ANY)


def _cp(sem, vmem_mb=None):
    kw = dict(dimension_semantics=sem)
    if vmem_mb is not None:
        kw["vmem_limit_bytes"] = vmem_mb * 1024 * 1024
    return pltpu.CompilerParams(**kw)


def _dot(a, b):
    return jnp.dot(a, b, preferred_element_type=F32)


def _dot_nt(a, b):
    return lax.dot_general(a, b, (((1,), (1,)), ((), ())), preferred_element_type=F32)


def _dot_tn(a, b):
    return lax.dot_general(a, b, (((0,), (0,)), ((), ())), preferred_element_type=F32)


def _rms(x):
    r = lax.rsqrt(jnp.mean(x * x, axis=-1, keepdims=True) + EPS)
    return x * r, r


def _rms_bwd(dn, n, r):
    return r * (dn - n * jnp.mean(dn * n, axis=-1, keepdims=True))


_GC = math.sqrt(2.0 / math.pi)


def _gelu(y):
    return 0.5 * y * (1.0 + jnp.tanh(_GC * (y + 0.044715 * y * y * y)))


def _gelu_grad(y):
    th = jnp.tanh(_GC * (y + 0.044715 * y * y * y))
    return 0.5 * (1.0 + th) + 0.5 * y * (1.0 - th * th) * _GC * (1.0 + 3 * 0.044715 * y * y)


def _sigmoid(v):
    return 1.0 / (1.0 + jnp.exp(-v))


def _colsum(x):
    return jnp.sum(x, axis=0, keepdims=True)


def _tok(width, col=0):
    return BS((None, TB, width), lambda b, j: (b, j, col))


def _mod_spec(nct):
    return BS((None, None, 8, D), lambda b, j: (b, jnp.where(j >= nct, 1, 0), 0, 0))


def _full(shape):
    nd = len(shape)
    return BS(shape, lambda *a: (0,) * nd)


def _resident(shape):
    nd = len(shape)
    return BS(shape, lambda *a: (0,) * nd, pipeline_mode=pl.Buffered(1))


def _ada_fwd(cact8, w4, b):
    wq = w4.shape[2]

    def body(c_ref, w_ref, b_ref, o_ref):
        o_ref[...] = _dot(c_ref[...].astype(MXU), w_ref[...]) + b_ref[...]

    return pl.pallas_call(
        body, name="ada_fwd", grid=(NCHIP,),
        in_specs=[BS((8, D), lambda j: (0, 0)), BS((None, D, wq), lambda j: (j, 0, 0)), BS((1, wq), lambda j: (0, j))],
        out_specs=BS((8, wq), lambda j: (0, j)),
        out_shape=SDS((8, NMOD * D), F32), compiler_params=_cp(("arbitrary",)),
    )(cact8, w4, b)


def _ada_bwd_c(dmod8, w4):
    wq = w4.shape[2]

    def body(d_ref, w_ref, o_ref):
        @pl.when(pl.program_id(0) == 0)
        def _():
            o_ref[...] = jnp.zeros_like(o_ref)
        o_ref[...] += _dot_nt(d_ref[...], w_ref[...])

    return pl.pallas_call(
        body, name="ada_bwd_c", grid=(NCHIP,),
        in_specs=[BS((8, wq), lambda j: (0, j)), BS((None, D, wq), lambda j: (j, 0, 0))],
        out_specs=BS((8, D), lambda j: (0, 0)),
        out_shape=SDS((8, D), F32), compiler_params=_cp(("arbitrary",)),
    )(dmod8, w4)


def _pre_mix(h, modtok, ng, w_in, nct):
    B, S, _ = h.shape

    def body(h_ref, mod_ref, g_ref, w_ref, p_ref, a_ref):
        n, _ = _rms(h_ref[...])
        a = ((n * g_ref[0:1, :]) * (1.0 + mod_ref[1:2, :]) + mod_ref[0:1, :]).astype(MXU)
        a_ref[...] = a
        p_ref[...] = _dot(a, w_ref[...])

    return pl.pallas_call(
        body, name="pre_mix", grid=(B, S // TB),
        in_specs=[_tok(D), _mod_spec(nct), _full((8, D)), _full((D, INW))],
        out_specs=[_tok(INW), _tok(D)],
        out_shape=[SDS((B, S, INW), F32), SDS((B, S, D), MXU)],
        compiler_params=_cp(("arbitrary", "arbitrary"), 40),
    )(h, modtok, ng, w_in)


def _pre_bwd(dqk, dv, dcv, du_parts, dh_part, h, modtok, ng, w_in, nct):
    B, S, _ = h.shape

    def body(dqk_ref, dv_ref, dcv_ref, du0_ref, du1_ref, du2_ref, dhp_ref, h_ref, mod_ref, g_ref, w_ref,
             dh_ref, vec_ref, dp_ref):
        du = du0_ref[...] + du1_ref[...] + du2_ref[...]
        dp = jnp.concatenate([dqk_ref[...], dv_ref[...].astype(MXU), dcv_ref[...].astype(MXU), du.astype(MXU)],
                             axis=1)
        dp_ref[...] = dp
        da = _dot_nt(dp, w_ref[...])
        n, r = _rms(h_ref[...])
        g = g_ref[0:1, :]
        sc1 = 1.0 + mod_ref[1:2, :]
        vec_ref[...] = jnp.zeros_like(vec_ref)
        vec_ref[0:1, :] = _colsum(da)
        vec_ref[1:2, :] = _colsum(da * (n * g))
        vec_ref[2:3, :] = _colsum(da * sc1 * n)
        dh_ref[...] = dhp_ref[...] + _rms_bwd(da * sc1 * g, n, r)

    return pl.pallas_call(
        body, name="pre_bwd", grid=(B, S // TB),
        in_specs=[_tok(AW + KVW), _tok(KVW), _tok(3 * CW), _tok(SW), _tok(SW), _tok(SW), _tok(D), _tok(D),
                  _mod_spec(nct), _full((8, D)), _full((D, INW))],
        out_specs=[_tok(D), BS((None, None, 8, D), lambda b, j: (b, j, 0, 0)), _tok(INW)],
        out_shape=[SDS((B, S, D), F32), SDS((B, S // TB, 8, D), F32), SDS((B, S, INW), MXU)],
        compiler_params=_cp(("arbitrary", "arbitrary"), 40),
    )(dqk, dv, dcv, *du_parts, dh_part, h, modtok, ng, w_in)


def _rotate(t, c, s):
    lane = lax.broadcasted_iota(jnp.int32, t.shape, 1)
    partner = jnp.where((lane % 32) < 16, pltpu.roll(t, 112, 1), pltpu.roll(t, 16, 1))
    return t * c + partner * s


def _to_groups(cols):
    lo = lax.broadcasted_iota(jnp.int32, cols[0].shape, 1) < HD
    out = []
    for g in range(QPK):
        a, b = cols[g // 2], cols[2 + g // 2]
        out.append(jnp.where(lo, a, pltpu.roll(b, HD, 1)) if g % 2 == 0 else jnp.where(lo, pltpu.roll(a, HD, 1), b))
    return out


def _from_groups(tiles):
    lo = lax.broadcasted_iota(jnp.int32, tiles[0].shape, 1) < HD
    out = []
    for k in range(NQ // 2):
        a, b = tiles[(2 * k) % QPK], tiles[(2 * k) % QPK + 1]
        out.append(jnp.where(lo, a, pltpu.roll(b, HD, 1)) if k < 2 else jnp.where(lo, pltpu.roll(a, HD, 1), b))
    return out


def _grp(width=128):
    return BS((None, QPK, TB, width), lambda b, j: (b, 0, j, 0))


_TAB = BS((TB, 128), lambda b, j: (j, 0))


def _rope_fwd(p, cos, sin):
    B, S, _ = p.shape

    def body(x_ref, c_ref, s_ref, q_ref, k_ref):
        c, s = c_ref[...], s_ref[...]
        rot = [_rotate(x_ref[:, k * 128:(k + 1) * 128], c, s) for k in range((AW + KVW) // 128)]
        k_ref[...] = rot[AW // 128].astype(MXU)
        for g, t in enumerate(_to_groups(rot[:AW // 128])):
            q_ref[g] = (t * SCALE).astype(MXU)

    return pl.pallas_call(
        body, name="rope_fwd", grid=(B, S // TB),
        in_specs=[_tok(AW + KVW), _TAB, _TAB],
        out_specs=[_grp(), _tok(KVW)],
        out_shape=[SDS((B, QPK, S, 128), MXU), SDS((B, S, KVW), MXU)],
        compiler_params=_cp(("arbitrary", "arbitrary")),
    )(p, cos, sin)


def _rope_bwd(dq4, dk, cos, nsin):
    B, _, S, _ = dq4.shape

    def body(q_ref, k_ref, c_ref, s_ref, o_ref):
        c, s = c_ref[...], s_ref[...]
        cols = _from_groups([q_ref[g] * SCALE for g in range(QPK)]) + [k_ref[...]]
        for k, t in enumerate(cols):
            o_ref[:, k * 128:(k + 1) * 128] = _rotate(t, c, s).astype(MXU)

    return pl.pallas_call(
        body, name="rope_bwd", grid=(B, S // TB),
        in_specs=[_grp(), _tok(KVW), _TAB, _TAB],
        out_specs=_tok(AW + KVW),
        out_shape=SDS((B, S, AW + KVW), MXU),
        compiler_params=_cp(("arbitrary", "arbitrary")),
    )(dq4, dk, cos, nsin)


def _attn_masks(i, S, Lc):
    ncb = Lc // QB
    is_lat = i >= ncb
    w0 = pl.multiple_of(jnp.clip((i - 1) * QB, 0, S - 3 * QB), QB)
    shape = (QPK * QB, 3 * QB)
    qpos = i * QB + jnp.bitwise_and(lax.broadcasted_iota(jnp.int32, shape, 0), QB - 1)
    kpos = w0 + lax.broadcasted_iota(jnp.int32, shape, 1)
    mask = jnp.logical_and(jnp.logical_and(kpos >= Lc, jnp.abs(qpos - kpos) <= WIN), is_lat)
    return w0, mask


def _attn_probs(qh, kch, kwh, mask, sk):
    sc = _dot_nt(qh, kch)
    sw = jnp.where(mask, _dot_nt(qh, kwh), NEG)
    m = jnp.maximum(jnp.maximum(jnp.max(sc, axis=-1, keepdims=True), jnp.max(sw, axis=-1, keepdims=True)), sk)
    ec = jnp.exp(sc - m)
    ew = jnp.exp(sw - m)
    es = jnp.exp(sk - m)
    inv = 1.0 / (jnp.sum(ec, axis=-1, keepdims=True) + jnp.sum(ew, axis=-1, keepdims=True) + es)
    return ec * inv, ew * inv, es * inv


def _sink_col(sink_ref, hk):
    return jnp.concatenate([jnp.full((QB, 1), sink_ref[hk * QPK + g], F32) for g in range(QPK)], axis=0)


_QGRP = BS((None, QPK, QB, 128), lambda b, i: (b, 0, i, 0))


def _first_last(grid):
    first = last = None
    for ax, n in enumerate(grid):
        i = pl.program_id(ax)
        first = (i == 0) if first is None else jnp.logical_and(first, i == 0)
        last = (i == n - 1) if last is None else jnp.logical_and(last, i == n - 1)
    return first, last


def _call_with(ex, body, name, grid, in_specs, out_specs, out_shape, args, scratch, params):
    n_in, n_out = len(in_specs), len(out_specs)
    if ex is None:
        res = pl.pallas_call(body, name=name, grid=grid, in_specs=in_specs, out_specs=out_specs,
                             out_shape=out_shape, scratch_shapes=scratch, compiler_params=params)(*args)
        return list(res), []

    def carrying(*refs):
        own, comm = _carried(ex, refs, n_in, n_out)
        first, last = _first_last(grid)
        pl.when(first)(lambda: ex.start(*comm))
        body(*own)
        pl.when(last)(lambda: ex.wait(*comm))

    res = pl.pallas_call(
        carrying, name=f"{name}_{type(ex).__name__.strip('_').lower()}", grid=grid,
        in_specs=in_specs + ex.specs, out_specs=out_specs + ex.specs, out_shape=out_shape + ex.out_shape,
        scratch_shapes=scratch + ex.sems, compiler_params=params)(*args, *ex.args)
    return list(res[:n_out]), list(res[n_out:])


def _attn_fwd(sink, q4, k, p, Lc, ex=None):
    B, S, _ = k.shape

    def body(sink_ref, q_ref, k_ref, v_ref, o_ref):
        w0, mask = _attn_masks(pl.program_id(1), S, Lc)
        kc = k_ref[0:Lc, :]
        vc = v_ref[0:Lc, :].astype(MXU)
        kw = k_ref[pl.ds(w0, 3 * QB), :]
        vw = v_ref[pl.ds(w0, 3 * QB), :].astype(MXU)
        q = q_ref[...].reshape(QPK * QB, 128)
        outs = []
        for hk in range(NKV):
            cs = slice(hk * HD, (hk + 1) * HD)
            pc, pw, _ = _attn_probs(q[:, cs], kc[:, cs], kw[:, cs], mask, _sink_col(sink_ref, hk))
            o = _dot(pc.astype(MXU), vc[:, cs]) + _dot(pw.astype(MXU), vw[:, cs])
            outs += [o[g * QB:(g + 1) * QB] for g in range(QPK)]
        o_ref[...] = jnp.concatenate(outs, axis=1).astype(o_ref.dtype)

    return _call_with(
        ex, body, "attn_fwd", (B, S // QB),
        [BS(memory_space=pltpu.SMEM), _QGRP, BS((None, S, KVW), lambda b, i: (b, 0, 0)),
         BS((None, S, KVW), lambda b, i: (b, 0, (AW + KVW) // KVW))],
        [BS((None, QB, AW), lambda b, i: (b, i, 0))], [SDS((B, S, AW), MXU)], [sink, q4, k, p], [],
        _cp(("arbitrary", "arbitrary"), 40))


def _attn_bwd(sink, q4, k, p, do4, Lc, ex=None):
    B, S, _ = k.shape

    def body(sink_ref, q_ref, k_ref, v_ref, do_ref, dq_ref, dk_ref, dv_ref, ds_ref):
        i = pl.program_id(1)

        @pl.when(i == 0)
        def _():
            dk_ref[...] = jnp.zeros_like(dk_ref)
            dv_ref[...] = jnp.zeros_like(dv_ref)
            ds_ref[...] = jnp.zeros_like(ds_ref)

        w0, mask = _attn_masks(i, S, Lc)
        kc = k_ref[0:Lc, :]
        vc = v_ref[0:Lc, :].astype(MXU)
        kw = k_ref[pl.ds(w0, 3 * QB), :]
        vw = v_ref[pl.ds(w0, 3 * QB), :].astype(MXU)
        q = q_ref[...].reshape(QPK * QB, 128)
        do = do_ref[...].reshape(QPK * QB, 128)
        dqs, dsinks, dkcs, dkws, dvcs, dvws = [], [], [], [], [], []
        for hk in range(NKV):
            cs = slice(hk * HD, (hk + 1) * HD)
            qh = q[:, cs]
            pc, pw, ps = _attn_probs(qh, kc[:, cs], kw[:, cs], mask, _sink_col(sink_ref, hk))
            dob = do[:, cs].astype(MXU)
            dpc = _dot_nt(dob, vc[:, cs])
            dpw = _dot_nt(dob, vw[:, cs])
            delta = jnp.sum(pc * dpc, axis=-1, keepdims=True) + jnp.sum(pw * dpw, axis=-1, keepdims=True)
            dsc = (pc * (dpc - delta)).astype(MXU)
            dsw = (pw * (dpw - delta)).astype(MXU)
            dqs.append(_dot(dsc, kc[:, cs]) + _dot(dsw, kw[:, cs]))
            dkcs.append(_dot_tn(dsc, qh))
            dkws.append(_dot_tn(dsw, qh))
            dvcs.append(_dot_tn(pc.astype(MXU), dob))
            dvws.append(_dot_tn(pw.astype(MXU), dob))
            psd = ps * delta
            dsinks += [jnp.broadcast_to(-jnp.sum(psd[g * QB:(g + 1) * QB], axis=0, keepdims=True), (1, 128))
                       for g in range(QPK)]
        for g in range(QPK):
            dq_ref[g] = jnp.concatenate([dq[g * QB:(g + 1) * QB] for dq in dqs], axis=1)
        ds_ref[...] += jnp.concatenate(dsinks, axis=0)
        dk_ref[0:Lc, :] += jnp.concatenate(dkcs, axis=1)
        dv_ref[0:Lc, :] += jnp.concatenate(dvcs, axis=1)
        dk_ref[pl.ds(w0, 3 * QB), :] += jnp.concatenate(dkws, axis=1)
        dv_ref[pl.ds(w0, 3 * QB), :] += jnp.concatenate(dvws, axis=1)

    kv_acc = BS((None, S, KVW), lambda b, i: (b, 0, 0))
    return _call_with(
        ex, body, "attn_bwd", (B, S // QB),
        [BS(memory_space=pltpu.SMEM), _QGRP, BS((None, S, KVW), lambda b, i: (b, 0, 0)),
         BS((None, S, KVW), lambda b, i: (b, 0, (AW + KVW) // KVW)), _QGRP],
        [_QGRP, kv_acc, kv_acc, BS((None, 8, 128), lambda b, i: (b, 0, 0))],
        [SDS((B, QPK, S, 128), F32), SDS((B, S, KVW), F32), SDS((B, S, KVW), F32), SDS((B, 8, 128), F32)],
        [sink, q4, k, p, do4], [], _cp(("arbitrary", "arbitrary"), 48))


def _halo_specs(width, col, S):
    nb8 = S // 8
    per = TB // 8
    prev = BS((None, 8, width), lambda b, j: (b, jnp.maximum(j * per - 1, 0), col))
    nxt = BS((None, 8, width), lambda b, j: (b, jnp.minimum((j + 1) * per, nb8 - 1), col))
    return prev, nxt


def _shift_dn(z, prev_row, at_start):
    row = lax.broadcasted_iota(jnp.int32, z.shape, 0)
    first = jnp.where(at_start, 0.0, 1.0) * prev_row
    return jnp.where(row == 0, first, pltpu.roll(z, 1, 0))


def _shift_up(z, next_row, at_end):
    row = lax.broadcasted_iota(jnp.int32, z.shape, 0)
    last = jnp.where(at_end, 0.0, 1.0) * next_row
    return jnp.where(row == z.shape[0] - 1, last, pltpu.roll(z, z.shape[0] - 1, 0))


def _conv_fwd(p, cw8, nct):
    B, S, _ = p.shape
    nt = S // TB

    def body(p_ref, pp_ref, pn_ref, w_ref, o_ref):
        j = pl.program_id(1)
        at_start = jnp.logical_or(j == 0, j == nct)
        at_end = jnp.logical_or(j == nct - 1, j == nt - 1)
        z = p_ref[:, 256:512] * p_ref[:, 512:768]
        zprev = pp_ref[7:8, 256:512] * pp_ref[7:8, 512:768]
        znext = pn_ref[0:1, 256:512] * pn_ref[0:1, 512:768]
        c3 = (_shift_dn(z, zprev, at_start) * w_ref[0:1, :] + z * w_ref[1:2, :]
              + _shift_up(z, znext, at_end) * w_ref[2:3, :])
        o_ref[...] = p_ref[:, 0:256] * c3

    prev, nxt = _halo_specs(3 * CW, 1, S)
    return pl.pallas_call(
        body, name="conv_fwd", grid=(B, nt),
        in_specs=[_tok(3 * CW, 1), prev, nxt, _full((8, CW))],
        out_specs=_tok(CW),
        out_shape=SDS((B, S, CW), F32),
        compiler_params=_cp(("arbitrary", "arbitrary")),
    )(p, p, p, cw8)


def _conv_bwd(p, dcv, cw8, nct):
    B, S, _ = p.shape
    nt = S // TB

    def body(p_ref, pp_ref, pn_ref, d_ref, dp_ref, dn_ref, w_ref, o_ref, dw_ref):
        j = pl.program_id(1)
        at_start = jnp.logical_or(j == 0, j == nct)
        at_end = jnp.logical_or(j == nct - 1, j == nt - 1)
        cb, cc, cx = p_ref[:, 0:256], p_ref[:, 256:512], p_ref[:, 512:768]
        z = cc * cx
        zm = _shift_dn(z, pp_ref[7:8, 256:512] * pp_ref[7:8, 512:768], at_start)
        zp = _shift_up(z, pn_ref[0:1, 256:512] * pn_ref[0:1, 512:768], at_end)
        d = d_ref[...]
        e = d * cb
        em = _shift_dn(e, dp_ref[7:8, :] * pp_ref[7:8, 0:256], at_start)
        ep = _shift_up(e, dn_ref[0:1, :] * pn_ref[0:1, 0:256], at_end)
        dz = e * w_ref[1:2, :] + ep * w_ref[0:1, :] + em * w_ref[2:3, :]
        o_ref[:, 0:256] = d * (zm * w_ref[0:1, :] + z * w_ref[1:2, :] + zp * w_ref[2:3, :])
        o_ref[:, 256:512] = dz * cx
        o_ref[:, 512:768] = dz * cc
        dw_ref[...] = jnp.zeros_like(dw_ref)
        dw_ref[0:1, :] = _colsum(e * zm)
        dw_ref[1:2, :] = _colsum(e * z)
        dw_ref[2:3, :] = _colsum(e * zp)

    prev, nxt = _halo_specs(3 * CW, 1, S)
    dprev, dnxt = _halo_specs(CW, 0, S)
    return pl.pallas_call(
        body, name="conv_bwd", grid=(B, nt),
        in_specs=[_tok(3 * CW, 1), prev, nxt, _tok(CW), dprev, dnxt, _full((8, CW))],
        out_specs=[_tok(3 * CW), BS((None, None, 8, CW), lambda b, j: (b, j, 0, 0))],
        out_shape=[SDS((B, S, 3 * CW), F32), SDS((B, nt, 8, CW), F32)],
        compiler_params=_cp(("arbitrary", "arbitrary")),
    )(p, p, p, dcv, dcv, dcv, cw8)


HW = SGRP * SST


def _rev_chunk(k, nk, ncc):
    return jnp.where(k < ncc, ncc - 1 - k, nk - 1 - k + ncc)


def _scan_perm(B):
    n = B * TC
    pm = np.zeros((2 * n, 2 * n), np.float32)
    for t in range(TC):
        for e in range(B):
            pm[t * 2 * B + e, e * TC + t] = 1.0
            pm[t * 2 * B + B + e, n + e * TC + (TC - 1 - t)] = 1.0
    return pm


def _scan_drive(uf_ref, ub_ref, pm_ref, b4_ref, fwd, dbuf, n):
    u_tok = jnp.concatenate([uf_ref[...].reshape(n, SW), ub_ref[...].reshape(n, SW)], axis=0).astype(MXU)
    u_tm = _dot(pm_ref[...], u_tok).astype(MXU)
    d4 = _dot(u_tm, b4_ref[...])
    dbuf[...] = jnp.where(fwd, d4[:, :2 * HW], d4[:, 2 * HW:])
    return u_tm


def _scan_fwd(p, pm, pmt, b4, a, c2, Lc):
    B, S, _ = p.shape
    nch, n = 2 * B, B * TC
    nk, ncc = S // TC, Lc // TC
    ucol = (INW - SW) // SW

    def body(uf_ref, ub_ref, pm_ref, pmt_ref, b4_ref, a_ref, c2_ref, yf_ref, yb_ref, hb_ref, hcar, dbuf, hbuf):
        @pl.when(pl.program_id(0) == 0)
        def _():
            hcar[...] = jnp.zeros_like(hcar)

        fwd = (lax.broadcasted_iota(jnp.int32, (2 * n, 1), 0) % nch) < B
        _scan_drive(uf_ref, ub_ref, pm_ref, b4_ref, fwd, dbuf, n)
        hb_ref[...] = hcar[...]
        ar, ai = a_ref[:, :HW], a_ref[:, HW:]

        def step(t, h):
            hr, hi = h
            r0 = pl.multiple_of(t * nch, nch)
            nr = ar * hr - ai * hi + dbuf[pl.ds(r0, nch), :HW]
            ni = ar * hi + ai * hr + dbuf[pl.ds(r0, nch), HW:]
            hbuf[pl.ds(r0, nch), :HW] = nr
            hbuf[pl.ds(r0, nch), HW:] = ni
            return nr, ni

        hr, hi = lax.fori_loop(0, TC, step, (hcar[:, :HW], hcar[:, HW:]))
        hcar[:, :HW] = hr
        hcar[:, HW:] = hi
        h_tok = _dot(pmt_ref[...], hbuf[...].astype(MXU)).astype(MXU)
        yf_ref[...] = _dot(h_tok[:n], c2_ref[:, :SW]).reshape(B, TC, SW)
        yb_ref[...] = _dot(h_tok[n:], c2_ref[:, SW:]).reshape(B, TC, SW)

    return pl.pallas_call(
        body, name="scan_fwd", grid=(nk,),
        in_specs=[BS((B, TC, SW), lambda k: (0, k, ucol)),
                  BS((B, TC, SW), lambda k: (0, _rev_chunk(k, nk, ncc), ucol)),
                  _full((2 * n, 2 * n)), _full((2 * n, 2 * n)),
                  _full((SW, 4 * HW)), _full((nch, 2 * HW)), _full((2 * HW, 2 * SW))],
        out_specs=[BS((B, TC, SW), lambda k: (0, k, 0)),
                   BS((B, TC, SW), lambda k: (0, _rev_chunk(k, nk, ncc), 0)),
                   BS((None, nch, 2 * HW), lambda k: (k, 0, 0))],
        out_shape=[SDS((B, S, SW), F32), SDS((B, S, SW), F32), SDS((nk, nch, 2 * HW), F32)],
        scratch_shapes=[pltpu.VMEM((nch, 2 * HW), F32), pltpu.VMEM((2 * n, 2 * HW), F32),
                        pltpu.VMEM((2 * n, 2 * HW), F32)],
        compiler_params=_cp(("arbitrary",), 48),
    )(p, p, pm, pmt, b4, a, c2)


def _scan_bwd(p, dy, hb, pm, pmt, b4, a, c2, Lc, ex=None):
    B, S, _ = p.shape
    nch, n = 2 * B, B * TC
    nk, ncc = S // TC, Lc // TC
    ucol = (INW - SW) // SW
    rows = 2 * n

    def body(uf_ref, ub_ref, dyf_ref, dyb_ref, hb_ref, pm_ref, pmt_ref, b4_ref, a_ref, c2_ref,
             duf_ref, dub_ref, db4_ref, da_ref, dc2_ref, gcar, dbuf, hbuf, gbuf):
        @pl.when(pl.program_id(0) == 0)
        def _():
            gcar[...] = jnp.zeros_like(gcar)
            db4_ref[...] = jnp.zeros_like(db4_ref)
            da_ref[...] = jnp.zeros_like(da_ref)
            dc2_ref[...] = jnp.zeros_like(dc2_ref)

        fwd = (lax.broadcasted_iota(jnp.int32, (rows, 1), 0) % nch) < B
        u_tm = _scan_drive(uf_ref, ub_ref, pm_ref, b4_ref, fwd, dbuf, n)
        ar, ai = a_ref[:, :HW], a_ref[:, HW:]
        hbuf[0:nch, :] = hb_ref[...]

        def fstep(t, h):
            hr, hi = h
            r0 = pl.multiple_of(t * nch, nch)
            nr = ar * hr - ai * hi + dbuf[pl.ds(r0, nch), :HW]
            ni = ar * hi + ai * hr + dbuf[pl.ds(r0, nch), HW:]
            hbuf[pl.ds(r0 + nch, nch), :HW] = nr
            hbuf[pl.ds(r0 + nch, nch), HW:] = ni
            return nr, ni

        lax.fori_loop(0, TC, fstep, (hb_ref[:, :HW], hb_ref[:, HW:]))

        dy_tok = jnp.concatenate([dyf_ref[...].reshape(n, SW), dyb_ref[...].reshape(n, SW)], axis=0).astype(MXU)
        dy_tm = _dot(pm_ref[...], dy_tok)
        dy2 = jnp.concatenate([jnp.where(fwd, dy_tm, 0.0), jnp.where(fwd, 0.0, dy_tm)], axis=1).astype(MXU)
        gbuf[...] = _dot_nt(dy2, c2_ref[...])
        dc2_ref[...] += _dot_tn(hbuf[nch:, :].astype(MXU), dy2)

        def bstep(i, cary):
            cr, ci = cary
            r0 = pl.multiple_of((TC - 1 - i) * nch, nch)
            gr = gbuf[pl.ds(r0, nch), :HW] + cr
            gi = gbuf[pl.ds(r0, nch), HW:] + ci
            gbuf[pl.ds(r0, nch), :HW] = gr
            gbuf[pl.ds(r0, nch), HW:] = gi
            return ar * gr + ai * gi, ar * gi - ai * gr

        cr, ci = lax.fori_loop(0, TC, bstep, (gcar[:, :HW], gcar[:, HW:]))
        gcar[:, :HW] = cr
        gcar[:, HW:] = ci

        gr = gbuf[:, :HW].reshape(TC, nch, HW)
        gi = gbuf[:, HW:].reshape(TC, nch, HW)
        hpr = hbuf[0:rows, :HW].reshape(TC, nch, HW)
        hpi = hbuf[0:rows, HW:].reshape(TC, nch, HW)
        da_ref[:, :HW] += jnp.sum(gr * hpr + gi * hpi, axis=0)
        da_ref[:, HW:] += jnp.sum(gi * hpr - gr * hpi, axis=0)

        g = gbuf[...]
        gm = g.astype(MXU)
        dd4 = jnp.concatenate([jnp.where(fwd, g, 0.0), jnp.where(fwd, 0.0, g)], axis=1).astype(MXU)
        db4_ref[...] += _dot_tn(u_tm, dd4)
        g_tok = _dot(pmt_ref[...], gm).astype(MXU)
        duf_ref[...] = _dot_nt(g_tok[:n], b4_ref[:, :2 * HW]).reshape(B, TC, SW)
        dub_ref[...] = _dot_nt(g_tok[n:], b4_ref[:, 2 * HW:]).reshape(B, TC, SW)

    fwd_blk = lambda col: BS((B, TC, SW), lambda k: (0, nk - 1 - k, col))
    bwd_blk = lambda col: BS((B, TC, SW), lambda k: (0, _rev_chunk(nk - 1 - k, nk, ncc), col))
    return _call_with(
        ex, body, "scan_bwd", (nk,),
        [fwd_blk(ucol), bwd_blk(ucol), fwd_blk(0), bwd_blk(0),
         BS((None, nch, 2 * HW), lambda k: (nk - 1 - k, 0, 0)),
         _full((rows, rows)), _full((rows, rows)),
         _full((SW, 4 * HW)), _full((nch, 2 * HW)), _full((2 * HW, 2 * SW))],
        [fwd_blk(0), bwd_blk(0), _full((SW, 4 * HW)), _full((nch, 2 * HW)), _full((2 * HW, 2 * SW))],
        [SDS((B, S, SW), F32), SDS((B, S, SW), F32), SDS((SW, 4 * HW), F32), SDS((nch, 2 * HW), F32),
         SDS((2 * HW, 2 * SW), F32)],
        [p, p, dy, dy, hb, pm, pmt, b4, a, c2],
        [pltpu.VMEM((nch, 2 * HW), F32), pltpu.VMEM((rows, 2 * HW), F32),
         pltpu.VMEM((rows + nch, 2 * HW), F32), pltpu.VMEM((rows, 2 * HW), F32)],
        _cp(("arbitrary",), 56))


def _post_mix(o, cv, yf, yb, p, h, modtok, ng, dsk, wg, bg, wo, nct):
    B, S, _ = h.shape

    def body(o_ref, cv_ref, yf_ref, yb_ref, u_ref, h_ref, mod_ref, g_ref, dsk_ref, wg_ref, bg_ref, wo_ref,
             h1_ref, mix_ref, m_ref, y_ref):
        y = yf_ref[...] + yb_ref[...] + dsk_ref[...] * u_ref[...]
        g = _gelu(y)
        s = g * _sigmoid(_dot(g.astype(MXU), wg_ref[...]) + bg_ref[...])
        mix_ref[:, 0:AW] = o_ref[...]
        mix_ref[:, AW:AW + CW] = cv_ref[...].astype(MXU)
        mix_ref[:, AW + CW:] = s.astype(MXU)
        m = _dot(mix_ref[...], wo_ref[...])
        n, _ = _rms(m)
        h1_ref[...] = h_ref[...] + mod_ref[2:3, :] * (n * g_ref[1:2, :])
        m_ref[...] = m
        y_ref[...] = y

    return pl.pallas_call(
        body, name="post_mix", grid=(B, S // TB),
        in_specs=[_tok(AW), _tok(CW), _tok(SW), _tok(SW), _tok(SW, (INW - SW) // SW), _tok(D), _mod_spec(nct),
                  _full((8, D)), _full((1, SW)), _full((SW, SW)), _full((1, SW)), _full((D, D))],
        out_specs=[_tok(D), _tok(D), _tok(D), _tok(SW)],
        out_shape=[SDS((B, S, D), F32), SDS((B, S, D), MXU), SDS((B, S, D), F32), SDS((B, S, SW), F32)],
        compiler_params=_cp(("arbitrary", "arbitrary"), 40),
    )(o, cv, yf, yb, p, h, modtok, ng, dsk, wg, bg, wo)


def _post_bwd(dh1, m, y, p, modtok, ng, dsk, wg, bg, wo, nct):
    B, S, _ = m.shape

    def body(dh_ref, m_ref, y_ref, u_ref, mod_ref, g_ref, dsk_ref, wg_ref, bg_ref, wo_ref,
             dm_ref, da_ref, dc_ref, dy_ref, du_ref, vec_ref, vec2_ref, dwg_ref):
        @pl.when(jnp.logical_and(pl.program_id(0) == 0, pl.program_id(1) == 0))
        def _():
            dwg_ref[...] = jnp.zeros_like(dwg_ref)

        n, r = _rms(m_ref[...])
        dh1 = dh_ref[...]
        gpm = g_ref[1:2, :]
        dr = dh1 * mod_ref[2:3, :]
        vec_ref[...] = jnp.zeros_like(vec_ref)
        vec_ref[0:1, :] = _colsum(dh1 * (n * gpm))
        vec_ref[1:2, :] = _colsum(dr * n)
        dm = _rms_bwd(dr * gpm, n, r).astype(MXU)
        dm_ref[...] = dm
        dmix = _dot_nt(dm, wo_ref[...])
        for g, t in enumerate(_to_groups([dmix[:, k * 128:(k + 1) * 128] for k in range(AW // 128)])):
            da_ref[g] = t
        dc_ref[...] = dmix[:, AW:AW + CW]
        ds = dmix[:, AW + CW:]
        yv = y_ref[...]
        g = _gelu(yv)
        gb = g.astype(MXU)
        sg = _sigmoid(_dot(gb, wg_ref[...]) + bg_ref[...])
        dv = ds * g * sg * (1.0 - sg)
        dvb = dv.astype(MXU)
        dg = ds * sg + _dot_nt(dvb, wg_ref[...])
        dwg_ref[...] += _dot_tn(gb, dvb)
        dy = dg * _gelu_grad(yv)
        dy_ref[...] = dy
        du_ref[...] = dy * dsk_ref[...]
        vec2_ref[...] = jnp.zeros_like(vec2_ref)
        vec2_ref[0:1, :] = _colsum(dy * u_ref[...])
        vec2_ref[1:2, :] = _colsum(dv)

    nt = S // TB
    return pl.pallas_call(
        body, name="post_bwd", grid=(B, nt),
        in_specs=[_tok(D), _tok(D), _tok(SW), _tok(SW, (INW - SW) // SW), _mod_spec(nct), _full((8, D)),
                  _full((1, SW)), _full((SW, SW)), _full((1, SW)), _full((D, D))],
        out_specs=[_tok(D), _grp(), _tok(CW), _tok(SW), _tok(SW),
                   BS((None, None, 8, D), lambda b, j: (b, j, 0, 0)),
                   BS((None, None, 8, SW), lambda b, j: (b, j, 0, 0)), _full((SW, SW))],
        out_shape=[SDS((B, S, D), MXU), SDS((B, QPK, S, 128), F32), SDS((B, S, CW), F32), SDS((B, S, SW), F32),
                   SDS((B, S, SW), F32), SDS((B, nt, 8, D), F32), SDS((B, nt, 8, SW), F32), SDS((SW, SW), F32)],
        compiler_params=_cp(("arbitrary", "arbitrary"), 40),
    )(dh1, m, y, p, modtok, ng, dsk, wg, bg, wo)


def _mlp_fwd(h1, modtok, ng, w1, w2, nct, ex=None):
    B, S, _ = h1.shape
    assert w1.shape == (DFF // TF, D, TF)

    def body(h_ref, mod_ref, g_ref, w1_ref, w2_ref, h2_ref, a2_ref, z_ref, f_ref):
        n, _ = _rms(h_ref[...])
        a2 = ((n * g_ref[2:3, :]) * (1.0 + mod_ref[4:5, :]) + mod_ref[3:4, :]).astype(MXU)
        a2_ref[...] = a2
        ff = jnp.zeros((TB, D), F32)
        for c in range(DFF // TF):
            cs = slice(c * TF, (c + 1) * TF)
            z = _dot(a2, w1_ref[c])
            z_ref[:, cs] = z.astype(MXU)
            r = jnp.maximum(z, 0.0)
            ff = ff + _dot((r * r).astype(MXU), w2_ref[cs, :])
        f_ref[...] = ff
        n, _ = _rms(ff)
        h2_ref[...] = h_ref[...] + mod_ref[5:6, :] * (n * g_ref[3:4, :])

    return _call_with(
        ex, body, "mlp_fwd", (B, S // TB),
        [_tok(D), _mod_spec(nct), _full((8, D)), _resident((DFF // TF, D, TF)), _resident((DFF, D))],
        [_tok(D), _tok(D), _tok(DFF), _tok(D)],
        [SDS((B, S, D), F32), SDS((B, S, D), MXU), SDS((B, S, DFF), MXU), SDS((B, S, D), F32)],
        [h1, modtok, ng, w1, w2], [], _cp(("arbitrary", "arbitrary"), 48))


def _mlp_bwd(dh2, h1, f, z, modtok, ng, w1, w2, nct, ex=None):
    B, S, _ = h1.shape

    def body(dh_ref, h_ref, f_ref, z_ref, mod_ref, g_ref, w1_ref, w2_ref, dh1_ref, df_ref, dz_ref, vec_ref):
        n, r = _rms(f_ref[...])
        dh2 = dh_ref[...]
        gp = g_ref[3:4, :]
        dr = dh2 * mod_ref[5:6, :]
        vec_ref[...] = jnp.zeros_like(vec_ref)
        vec_ref[3:4, :] = _colsum(dh2 * (n * gp))
        vec_ref[4:5, :] = _colsum(dr * n)
        df = _rms_bwd(dr * gp, n, r).astype(MXU)
        df_ref[...] = df
        da = jnp.zeros((TB, D), F32)
        for c in range(DFF // TF):
            cs = slice(c * TF, (c + 1) * TF)
            dr2 = _dot_nt(df, w2_ref[cs, :])
            dz = (dr2 * 2.0 * jnp.maximum(z_ref[:, cs].astype(F32), 0.0)).astype(MXU)
            dz_ref[:, cs] = dz
            da = da + _dot_nt(dz, w1_ref[c])
        n, r = _rms(h_ref[...])
        g = g_ref[2:3, :]
        sc1 = 1.0 + mod_ref[4:5, :]
        vec_ref[0:1, :] = _colsum(da)
        vec_ref[1:2, :] = _colsum(da * (n * g))
        vec_ref[2:3, :] = _colsum(da * sc1 * n)
        dh1_ref[...] = dh2 + _rms_bwd(da * sc1 * g, n, r)

    return _call_with(
        ex, body, "mlp_bwd", (B, S // TB),
        [_tok(D), _tok(D), _tok(D), _tok(DFF), _mod_spec(nct), _full((8, D)),
         _resident((DFF // TF, D, TF)), _resident((DFF, D))],
        [_tok(D), _tok(D), _tok(DFF), BS((None, None, 8, D), lambda b, j: (b, j, 0, 0))],
        [SDS((B, S, D), F32), SDS((B, S, D), MXU), SDS((B, S, DFF), MXU), SDS((B, S // TB, 8, D), F32)],
        [dh2, h1, f, z, modtok, ng, w1, w2], [], _cp(("arbitrary", "arbitrary"), 56))


def _loss(h, target, Lc):
    B, S, _ = h.shape
    nt, nct = S // TB, Lc // TB

    def body(h_ref, t_ref, dh_ref, l_ref):
        lat = pl.program_id(1) >= nct
        err = jnp.where(lat, h_ref[...] - t_ref[...], 0.0)
        dh_ref[...] = err * (1.0 / D)
        l_ref[...] = jnp.broadcast_to(jnp.sum(err * err, keepdims=True), (8, 128))

    return pl.pallas_call(
        body, name="loss", grid=(B, nt),
        in_specs=[_tok(D), BS((None, TB, D), lambda b, j: (b, jnp.maximum(j - nct, 0), 0))],
        out_specs=[_tok(D), BS((None, None, 8, 128), lambda b, j: (b, j, 0, 0))],
        out_shape=[SDS((B, S, D), F32), SDS((B, nt, 8, 128), F32)],
        compiler_params=_cp(("arbitrary", "arbitrary")),
    )(h, target)


def _mm_tn(a, b, name, relu2=False):
    T, M = a.shape
    N = b.shape[1]
    def blk(n):
        return max(b for b in range(128, 1025, 128) if n % b == 0)

    bm, bn = blk(M), blk(N)
    tk = next(t for t in (1536, 1024, 512, 256, T) if T % t == 0)
    nkk = T // tk

    def body(a_ref, b_ref, o_ref, om_ref):
        k = pl.program_id(2)

        @pl.when(k == 0)
        def _():
            o_ref[...] = jnp.zeros_like(o_ref)

        av = a_ref[...]
        if relu2:
            r = jnp.maximum(av.astype(F32), 0.0)
            av = (r * r).astype(MXU)
        o_ref[...] += _dot_tn(av, b_ref[...])

        @pl.when(k == nkk - 1)
        def _():
            om_ref[...] = o_ref[...].astype(MXU)

    out = BS((bm, bn), lambda i, j, k: (i, j))
    return pl.pallas_call(
        body, name=name, grid=(M // bm, N // bn, nkk),
        in_specs=[BS((tk, bm), lambda i, j, k: (k, i)), BS((tk, bn), lambda i, j, k: (k, j))],
        out_specs=[out, out], out_shape=[SDS((M, N), F32), SDS((M, N), MXU)],
        compiler_params=_cp(("arbitrary", "arbitrary", "arbitrary"), 56),
    )(a, b)


_PAD_ROWS = 64


def _block_rows(R):
    for br in (256, _PAD_ROWS):
        if R % br == 0:
            return br
    raise ValueError(f"row count {R} is not a multiple of {_PAD_ROWS}")


def _adamw(w, ga, gb, m, v, name):
    R, C = w.shape
    br = _block_rows(R) if C <= _LANES else _PAD_ROWS

    def body(w_ref, ga_ref, gb_ref, m_ref, v_ref, g_out, d_out, m_out, v_out):
        g = ga_ref[...] + gb_ref[...]
        m2 = B1 * m_ref[...] + (1.0 - B1) * g
        v2 = B2 * v_ref[...] + (1.0 - B2) * (g * g)
        m_hat = m2 / (1.0 - B1 ** STEP)
        v_hat = v2 / (1.0 - B2 ** STEP)
        g_out[...] = g
        d_out[...] = -LR * (m_hat / (jnp.sqrt(v_hat) + AEPS) + WD * w_ref[...])
        m_out[...] = m2
        v_out[...] = v2

    own = BS((br, C), lambda i: (i, 0))
    return pl.pallas_call(
        body, name=name, grid=(R // br,),
        in_specs=[own] * 5, out_specs=[own] * 4, out_shape=[SDS((R, C), F32)] * 4,
        compiler_params=_cp(("arbitrary",)),
    )(w, ga, gb, m, v)


def _my_xyc():
    return lax.axis_index("x"), lax.axis_index("y"), lax.axis_index("c")


def _chip_peers(x, y):
    return [(1 - x, y), (x, 1 - y), (1 - x, 1 - y)]


class _Gather:
    def __init__(self, shards):
        self.args = list(shards)
        self.n = len(self.args)
        self.specs = [ANY] * self.n
        self.out_shape = [SDS((NCHIP,) + a.shape, a.dtype) for a in self.args]
        self.sems = [pltpu.SemaphoreType.DMA((3 * self.n,)), pltpu.SemaphoreType.DMA((3 * self.n,)),
                     pltpu.SemaphoreType.DMA((self.n,))]

    def _copies(self, ins, outs, ssem, rsem, lsem):
        x, y, c = _my_xyc()
        mine, sends, recvs = [], [], []
        for k, (s_ref, o_ref) in enumerate(zip(ins, outs)):
            mine.append(pltpu.make_async_copy(s_ref, o_ref.at[2 * x + y], lsem.at[k]))
            for j, (px, py) in enumerate(_chip_peers(x, y)):
                sems = dict(send_sem=ssem.at[3 * k + j], recv_sem=rsem.at[3 * k + j], device_id=(px, py, c),
                            device_id_type=MESH)
                sends.append(pltpu.make_async_remote_copy(src_ref=s_ref, dst_ref=o_ref.at[2 * x + y], **sems))
                recvs.append(pltpu.make_async_remote_copy(src_ref=s_ref, dst_ref=o_ref.at[2 * px + py], **sems))
        return mine, sends, recvs

    def start(self, ins, outs, sems):
        mine, sends, _ = self._copies(ins, outs, *sems)
        for cp in mine + sends:
            cp.start()

    def wait(self, ins, outs, sems):
        mine, sends, recvs = self._copies(ins, outs, *sems)
        for cp in recvs:
            cp.wait_recv()
        for cp in sends:
            cp.wait_send()
        for cp in mine:
            cp.wait()


class _Gather2(_Gather):
    def __init__(self, shards):
        super().__init__(shards)
        self.sems = self.sems + [pltpu.SemaphoreType.DMA((3 * self.n,)), pltpu.SemaphoreType.DMA((3 * self.n,))]

    def _copies2(self, ins, outs, ssem, rsem, lsem, fsem, gsem):
        x, y, c = _my_xyc()
        mine, sends, recvs, passes, got = [], [], [], [], []
        for k, (s_ref, o_ref) in enumerate(zip(ins, outs)):
            half = s_ref.shape[0] // 2
            my_rows, sib_rows = pl.ds(c * half, half), pl.ds((1 - c) * half, half)
            mine.append(pltpu.make_async_copy(s_ref, o_ref.at[2 * x + y], lsem.at[k]))
            for j, (px, py) in enumerate(_chip_peers(x, y)):
                ici = dict(send_sem=ssem.at[3 * k + j], recv_sem=rsem.at[3 * k + j], device_id=(px, py, c),
                           device_id_type=MESH)
                d2d = dict(send_sem=fsem.at[3 * k + j], recv_sem=gsem.at[3 * k + j], device_id=(x, y, 1 - c),
                           device_id_type=MESH)
                landed = o_ref.at[2 * px + py, my_rows]
                sends.append(pltpu.make_async_remote_copy(src_ref=s_ref.at[my_rows],
                                                          dst_ref=o_ref.at[2 * x + y, my_rows], **ici))
                recvs.append(pltpu.make_async_remote_copy(src_ref=s_ref.at[my_rows], dst_ref=landed, **ici))
                passes.append(pltpu.make_async_remote_copy(src_ref=landed, dst_ref=landed, **d2d))
                got.append(pltpu.make_async_remote_copy(src_ref=landed, dst_ref=o_ref.at[2 * px + py, sib_rows],
                                                        **d2d))
        return mine, sends, recvs, passes, got

    def start(self, ins, outs, sems):
        mine, sends, _, _, _ = self._copies2(ins, outs, *sems)
        for cp in mine + sends:
            cp.start()

    def wait(self, ins, outs, sems):
        mine, sends, recvs, passes, got = self._copies2(ins, outs, *sems)
        for landed, onward in zip(recvs, passes):
            landed.wait_recv()
            onward.start()
        for cp in got:
            cp.wait_recv()
        for cp in sends + passes:
            cp.wait_send()
        for cp in mine:
            cp.wait()


class _Scatter:
    def __init__(self, sends, cols=None):
        self.args = list(sends)
        self.n = len(self.args)
        self.cols = list(cols) if cols is not None else [None] * self.n
        self.specs = [ANY] * self.n
        self.out_shape = [SDS((3,) + a.shape[1:] if cw is None else (3, a.shape[0], cw), a.dtype)
                          for a, cw in zip(self.args, self.cols)]
        self.sems = [pltpu.SemaphoreType.DMA((3 * self.n,)), pltpu.SemaphoreType.DMA((3 * self.n,))]

    def _copies(self, ins, outs, ssem, rsem):
        x, y, c = _my_xyc()

        def piece(k, q):
            cw = self.cols[k]
            return ins[k].at[q] if cw is None else ins[k].at[:, pl.ds(pl.multiple_of(q * cw, 128), cw)]

        return [pltpu.make_async_remote_copy(
            src_ref=piece(k, 2 * px + py), dst_ref=outs[k].at[j], send_sem=ssem.at[3 * k + j],
            recv_sem=rsem.at[3 * k + j], device_id=(px, py, c), device_id_type=MESH)
            for k in range(self.n) for j, (px, py) in enumerate(_chip_peers(x, y))]

    def start(self, ins, outs, sems):
        for cp in self._copies(ins, outs, *sems):
            cp.start()

    def wait(self, ins, outs, sems):
        for cp in self._copies(ins, outs, *sems):
            cp.wait()


class _Swap(_Scatter):
    def __init__(self, arrays):
        self.args = list(arrays)
        self.n = len(self.args)
        self.specs = [ANY] * self.n
        self.out_shape = [SDS(a.shape, a.dtype) for a in self.args]
        self.sems = [pltpu.SemaphoreType.DMA((self.n,)), pltpu.SemaphoreType.DMA((self.n,))]

    def _copies(self, ins, outs, ssem, rsem):
        x, y, c = _my_xyc()
        return [pltpu.make_async_remote_copy(src_ref=s_ref, dst_ref=o_ref, send_sem=ssem.at[k], recv_sem=rsem.at[k],
                                             device_id=(x, y, 1 - c), device_id_type=MESH)
                for k, (s_ref, o_ref) in enumerate(zip(ins, outs))]


def _exchange_call(ex, name):
    n = ex.n

    def body(*refs):
        ins, outs, sems = refs[:n], refs[n:2 * n], refs[2 * n:]
        ex.start(ins, outs, sems)
        ex.wait(ins, outs, sems)

    return pl.pallas_call(body, name=name, in_specs=ex.specs, out_specs=ex.specs, out_shape=ex.out_shape,
                          scratch_shapes=ex.sems)(*ex.args)


def _carried(ex, refs, n_in, n_out):
    n = ex.n
    ins, cin = refs[:n_in], refs[n_in:n_in + n]
    outs, cout = refs[n_in + n:n_in + n + n_out], refs[n_in + n + n_out:n_in + 2 * n + n_out]
    rest = refs[n_in + 2 * n + n_out:]
    nsem = len(ex.sems)
    return list(ins) + list(outs) + list(rest[:len(rest) - nsem]), (cin, cout, rest[len(rest) - nsem:])


def _gather_all(v, name):
    R, C = v.shape

    def body(s_ref, o_ref, ssem, rsem, lsem):
        x, y, c = _my_xyc()
        me = 4 * x + 2 * y + c
        mine = pltpu.make_async_copy(s_ref, o_ref.at[me], lsem)
        mine.start()

        def peer(j):
            fx, fy, fc = (j >> 2) & 1, (j >> 1) & 1, j & 1
            return (x ^ fx, y ^ fy, c ^ fc)

        cps = []
        for j in range(1, 8):
            cps.append(pltpu.make_async_remote_copy(
                src_ref=s_ref, dst_ref=o_ref.at[me], send_sem=ssem.at[j - 1], recv_sem=rsem.at[j - 1],
                device_id=peer(j), device_id_type=MESH))
            cps[-1].start()
        for j in range(1, 8):
            px, py, pc = peer(j)
            pltpu.make_async_remote_copy(
                src_ref=s_ref, dst_ref=o_ref.at[4 * px + 2 * py + pc], send_sem=ssem.at[j - 1],
                recv_sem=rsem.at[j - 1], device_id=peer(j), device_id_type=MESH).wait_recv()
        for cp in cps:
            cp.wait_send()
        mine.wait()

    return pl.pallas_call(
        body, name=name, in_specs=[ANY], out_specs=ANY, out_shape=SDS((8, R, C), v.dtype),
        scratch_shapes=[pltpu.SemaphoreType.DMA((7,)), pltpu.SemaphoreType.DMA((7,)), pltpu.SemaphoreType.DMA],
    )(v)


def _sum_slots(own, slots, name):
    n, R, C = slots.shape
    br = _block_rows(R)

    def body(*refs):
        o_ref = refs[-1]
        if own is None:
            acc = refs[0][0].astype(F32)
            first = 1
            s_ref = refs[0]
        else:
            acc = refs[0][...]
            first = 0
            s_ref = refs[1]
        for k in range(first, n):
            acc = acc + s_ref[k].astype(F32)
        o_ref[...] = acc

    row = BS((br, C), lambda i: (i, 0))
    slab = BS((n, br, C), lambda i: (0, i, 0))
    ins, args = ([slab], [slots]) if own is None else ([row, slab], [own, slots])
    return pl.pallas_call(
        body, name=name, grid=(R // br,), in_specs=ins, out_specs=row, out_shape=SDS((R, C), F32),
        compiler_params=_cp(("arbitrary",)),
    )(*args)


def _rope_tables(L, Lc):
    n = jnp.arange(L)
    row = (n // GRID_W).astype(F32)
    col = (n % GRID_W).astype(F32)
    freqs = ROPE_BASE ** (-jnp.arange(16, dtype=F32) / 16)
    lane = jnp.arange(128)
    dd = lane % HD
    fr = freqs[dd % 16]
    ang = jnp.where(dd < 32, row[:, None], col[:, None]) * fr[None, :]
    sign = jnp.where((dd % 32) < 16, -1.0, 1.0)
    cos = jnp.concatenate([jnp.ones((Lc, 128), F32), jnp.cos(ang)], axis=0)
    sin = jnp.concatenate([jnp.zeros((Lc, 128), F32), jnp.sin(ang) * sign[None, :]], axis=0)
    return cos, sin


def _ssm_prep(lam_re, lam_im, log_dt, b_re, b_im, c_re, c_im, B):
    dt = jnp.exp(log_dt)[..., None]
    mag = jnp.exp(lam_re * dt)
    ar = mag * jnp.cos(lam_im * dt)
    ai = mag * jnp.sin(lam_im * dt)
    den = lam_re * lam_re + lam_im * lam_im
    kr = ((ar - 1.0) * lam_re + ai * lam_im) / den
    ki = (ai * lam_re - (ar - 1.0) * lam_im) / den
    bbr = kr[..., None] * b_re - ki[..., None] * b_im
    bbi = kr[..., None] * b_im + ki[..., None] * b_re
    eye = jnp.eye(SGRP, dtype=F32)

    def bblk(t):
        return jnp.einsum("gpi,gh->gihp", t, eye).reshape(SGRP * SCH, SGRP * SST)

    def cblk(t):
        return jnp.einsum("gip,gh->gphi", t, eye).reshape(SGRP * SST, SGRP * SCH)

    b4 = jnp.concatenate([bblk(bbr[0]), bblk(bbi[0]), bblk(bbr[1]), bblk(bbi[1])], axis=1)
    c2 = jnp.concatenate([jnp.concatenate([cblk(c_re[0]), -cblk(c_im[0])], axis=0),
                          jnp.concatenate([cblk(c_re[1]), -cblk(c_im[1])], axis=0)], axis=1)
    a2 = jnp.concatenate([ar.reshape(2, -1), ai.reshape(2, -1)], axis=1)
    a = jnp.repeat(a2, B, axis=0)
    return a, b4, c2


def _pad_rows(v, rows=8):
    return jnp.concatenate([v, jnp.zeros((rows - v.shape[0],) + v.shape[1:], v.dtype)], axis=0)


_BIG = ("w_ada", "w_in", "w_out", "w_mlp_in", "w_mlp_out")
_BIG_AXIS = {"w_ada": 2, "w_in": 2, "w_out": 1, "w_mlp_in": 2, "w_mlp_out": 1}


def _split_shards(g, axis):
    L, r, c = g.shape
    if axis == 2:
        return jnp.transpose(g.reshape(L, r, NCHIP, c // NCHIP), (2, 0, 1, 3))
    return jnp.transpose(g.reshape(L, NCHIP, r // NCHIP, c), (1, 0, 2, 3))


def _join_shards(s, axis):
    _, L, r, c = s.shape
    if axis == 2:
        return jnp.transpose(s, (1, 2, 0, 3)).reshape(L, r, NCHIP * c)
    return jnp.transpose(s, (1, 0, 2, 3)).reshape(L, NCHIP * r, c)


def _local_step(x, c, ctx, loss_target, P, ex):
    B, L, _ = x.shape
    Lc = ctx.shape[1]
    S = Lc + L
    nct = Lc // TB
    nt = S // TB
    nch = 2 * B
    cos, sin = _rope_tables(L, Lc)
    perm = _scan_perm(B)
    pm, pmt = jnp.asarray(perm, MXU), jnp.asarray(perm.T, MXU)

    c_act = jax.nn.silu(c)
    cc_act, silu_vjp = jax.vjp(jax.nn.silu, P["c_ctx"])
    cact8 = _pad_rows(jnp.concatenate([c_act, cc_act[None, :]], axis=0))
    h = jnp.concatenate([ctx, x], axis=1)

    saved = []
    for l in range(NLAYER):
        Wl = ex.weights(l)
        mod8 = _ada_fwd(cact8, Wl["w_ada"], P["b_ada"][l][None, :])
        mod6 = mod8.reshape(8, NMOD, D)
        modtok = jnp.stack([jnp.broadcast_to(mod6[B], (B, NMOD, D)), mod6[:B]], axis=1)
        modtok = jnp.concatenate([modtok, jnp.zeros((B, 2, 2, D), F32)], axis=2)
        ng = _pad_rows(P["norm_g"][l])
        cw8 = _pad_rows(P["conv_w"][l])
        (a_s, b4, c2), prep_vjp = jax.vjp(
            lambda *t: _ssm_prep(*t, B), P["ssm_lam_re"][l], P["ssm_lam_im"][l], P["ssm_log_dt"][l],
            P["ssm_b_re"][l], P["ssm_b_im"][l], P["ssm_c_re"][l], P["ssm_c_im"][l])
        b4m, c2m = b4.astype(MXU), c2.astype(MXU)
        wg = P["w_glu"][l].astype(MXU)
        dsk = P["ssm_d"][l][None, :]
        bg = P["b_glu"][l][None, :]

        p, a = _pre_mix(h, modtok, ng, Wl["w_in"], nct)
        q4, kr = _rope_fwd(p, cos, sin)
        (o,) = ex.attn_fwd(l, P["attn_sink"][l], q4, kr, p, Lc)
        cv = _conv_fwd(p, cw8, nct)
        yf, yb, hb = _scan_fwd(p, pm, pmt, b4m, a_s, c2m, Lc)
        h1, mix, m, y = _post_mix(o, cv, yf, yb, p, h, modtok, ng, dsk, wg, bg, Wl["w_out"], nct)
        h2, a2, z, f = ex.mlp_fwd(l, h1, modtok, ng, Wl["w_mlp_in"], Wl["w_mlp_out"], nct)
        saved.append(dict(Wl=Wl, h=h, modtok=modtok, ng=ng, cw8=cw8, a_s=a_s, b4m=b4m, c2m=c2m, wg=wg, dsk=dsk, bg=bg,
                          p=p, a=a, q4=q4, kr=kr, hb=hb, mix=mix, m=m, y=y, h1=h1, a2=a2, z=z, f=f,
                          prep_vjp=prep_vjp))
        h = h2

    dh, lpart = _loss(h, loss_target, Lc)
    loss_local = 0.5 / D * jnp.sum(lpart[:, :, 0, 0])

    G = {k: [None] * NLAYER for k in _SMALL if k != "c_ctx"}
    dcact = jnp.zeros((8, D), F32)
    T = B * S
    for l in reversed(range(NLAYER)):
        sv = saved[l]
        modtok, ng, Wl = sv["modtok"], sv["ng"], sv["Wl"]
        dh1, df, dz, vec_m = ex.mlp_bwd(l, dh, sv["h1"], sv["f"], sv["z"], modtok, ng, Wl["w_mlp_in"],
                                        Wl["w_mlp_out"], nct)
        ex.submit(l, "w_mlp_out", *_mm_tn(sv["z"].reshape(T, DFF), df.reshape(T, D), "dw_mlp_out", relu2=True))
        ex.submit(l, "w_mlp_in", *_mm_tn(sv["a2"].reshape(T, D), dz.reshape(T, DFF), "dw_mlp_in"))
        dm, dattn, dcv, dy, du_skip, vec_p, vec_p2, dwg = _post_bwd(
            dh1, sv["m"], sv["y"], sv["p"], modtok, ng, sv["dsk"], sv["wg"], sv["bg"], Wl["w_out"], nct)
        ex.submit(l, "w_out", *_mm_tn(sv["mix"].reshape(T, D), dm.reshape(T, D), "dw_out"))
        G["w_glu"][l] = dwg
        G["ssm_d"][l] = jnp.sum(vec_p2[:, :, 0, :], axis=(0, 1))
        G["b_glu"][l] = jnp.sum(vec_p2[:, :, 1, :], axis=(0, 1))
        duf, dub, db4, da_s, dc2 = ex.scan_bwd(l, sv["p"], dy, sv["hb"], pm, pmt, sv["b4m"], sv["a_s"], sv["c2m"],
                                               Lc)
        gs = sv["prep_vjp"]((da_s, db4, dc2))
        for k, name in enumerate(("ssm_lam_re", "ssm_lam_im", "ssm_log_dt", "ssm_b_re", "ssm_b_im", "ssm_c_re",
                                  "ssm_c_im")):
            G[name][l] = gs[k]
        dconv, dcw = _conv_bwd(sv["p"], dcv, sv["cw8"], nct)
        G["conv_w"][l] = jnp.sum(dcw[:, :, 0:3, :], axis=(0, 1))
        dq_r, dk_r, dv, dsink = ex.attn_bwd(l, P["attn_sink"][l], sv["q4"], sv["kr"], sv["p"], dattn, Lc)
        G["attn_sink"][l] = jnp.sum(dsink[:, :, 0], axis=0)
        dqk = _rope_bwd(dq_r, dk_r, cos, -sin)
        dh, vec_i, dp = _pre_bwd(dqk, dv, dconv, (du_skip, duf, dub), dh1, sv["h"], modtok, ng, Wl["w_in"], nct)
        ex.submit(l, "w_in", *_mm_tn(sv["a"].reshape(T, D), dp.reshape(T, INW), "dw_in"))
        G["norm_g"][l] = jnp.stack([jnp.sum(vec_i[:, :, 2], axis=(0, 1)), jnp.sum(vec_p[:, :, 1], axis=(0, 1)),
                                    jnp.sum(vec_m[:, :, 2], axis=(0, 1)), jnp.sum(vec_m[:, :, 4], axis=(0, 1))])
        dmod_blk = jnp.stack([vec_i[:, :, 0], vec_i[:, :, 1], vec_p[:, :, 0],
                              vec_m[:, :, 0], vec_m[:, :, 1], vec_m[:, :, 3]], axis=2)
        dmod_lat = jnp.sum(dmod_blk[:, nct:], axis=1).reshape(B, NMOD * D)
        dmod_ctx = jnp.sum(dmod_blk[:, :nct], axis=(0, 1)).reshape(1, NMOD * D)
        dmod8 = _pad_rows(jnp.concatenate([dmod_lat, dmod_ctx], axis=0))
        G["b_ada"][l] = jnp.sum(dmod8, axis=0)
        dmod8m = dmod8.astype(MXU)
        ex.submit(l, "w_ada", *_mm_tn(cact8.astype(MXU), dmod8m, "dw_ada"))
        dcact = dcact + _ada_bwd_c(dmod8m, Wl["w_ada"])

    grads = {k: jnp.stack(v) for k, v in G.items()}
    grads["c_ctx"] = silu_vjp(dcact[B])[0]
    return loss_local, dh[:, Lc:, :], grads


_WEIGHTS = ["c_ctx", "w_ada", "b_ada", "norm_g", "w_in", "conv_w", "attn_sink", "ssm_lam_re", "ssm_lam_im",
            "ssm_log_dt", "ssm_b_re", "ssm_b_im", "ssm_c_re", "ssm_c_im", "ssm_d", "w_glu", "b_glu", "w_out",
            "w_mlp_in", "w_mlp_out"]
_SMALL = [n for n in _WEIGHTS if n not in _BIG]
_SMALL_SHARDED = {"norm_g": 2, "conv_w": 2, "w_glu": 1}
_LANES = 1024


_GROUPS = (("w_ada", "w_in", "w_out"), ("w_mlp_in", "w_mlp_out"))
_QUEUE_OF = {"w_ada": "narrow", "w_in": "narrow", "w_out": "out", "w_mlp_in": "mlp", "w_mlp_out": "mlp"}


class _Exchange:
    def __init__(self, W, q_me):
        self.W, self.q_me = W, q_me
        self.wall, self.own, self.recv, self.pending = {}, {}, {}, {}
        self._keep(0, _BIG, _exchange_call(_Gather2([self._shard(0, n) for n in _BIG]), "gather_big"))

    def _shard(self, l, n):
        return self.W[n][l].astype(MXU)

    def _keep(self, l, names, walls):
        for n, w in zip(names, walls):
            self.wall[l, n] = w

    def weights(self, l):
        g = {n: self.wall[l, n] for n in _BIG}
        return dict(w_ada=g["w_ada"], w_mlp_in=g["w_mlp_in"], w_in=_join_shards(g["w_in"][:, None], 2)[0],
                    w_out=g["w_out"].reshape(-1, g["w_out"].shape[-1]),
                    w_mlp_out=g["w_mlp_out"].reshape(-1, g["w_mlp_out"].shape[-1]))

    def _gathering(self, l, grp, fn, *args):
        if l + 1 == NLAYER:
            return fn(*args)[0]
        res, walls = fn(*args, ex=_Gather([self._shard(l + 1, n) for n in _GROUPS[grp]]))
        self._keep(l + 1, _GROUPS[grp], walls)
        return res

    def attn_fwd(self, l, *args):
        return self._gathering(l, 0, _attn_fwd, *args)

    def mlp_fwd(self, l, *args):
        return self._gathering(l, 1, _mlp_fwd, *args)

    def submit(self, l, n, g, gm):
        r, c = g.shape
        if _BIG_AXIS[n] == 2:
            cw = c // NCHIP
            self.own[l, n] = lax.dynamic_slice_in_dim(g, self.q_me * cw, cw, axis=1)
            if cw % 128 == 0:
                item = (gm, cw)
            else:
                item = (jnp.transpose(gm.reshape(r, NCHIP, cw), (1, 0, 2)), None)
        else:
            self.own[l, n] = lax.dynamic_slice_in_dim(g, self.q_me * (r // NCHIP), r // NCHIP, axis=0)
            item = (gm.reshape(NCHIP, r // NCHIP, c), None)
        self.pending.setdefault(_QUEUE_OF[n], []).append((l, n) + item)

    def _scatter_of(self, items):
        return _Scatter([it[2] for it in items], [it[3] for it in items])

    def _scattering(self, queue, fn, *args):
        items = self.pending.pop(queue, [])
        if not items:
            return fn(*args)[0]
        res, recvs = fn(*args, ex=self._scatter_of(items))
        for it, r in zip(items, recvs):
            self.recv[it[0], it[1]] = r
        return res

    def mlp_bwd(self, l, *args):
        return self._scattering("narrow", _mlp_bwd, *args)

    def scan_bwd(self, l, *args):
        return self._scattering("mlp", _scan_bwd, *args)

    def attn_bwd(self, l, *args):
        return self._scattering("out", _attn_bwd, *args)

    def finish(self):
        items = [it for queue in sorted(self.pending) for it in self.pending[queue]]
        self.pending = {}
        for it, r in zip(items, _exchange_call(self._scatter_of(items), "scatter_big")):
            self.recv[it[0], it[1]] = r
        part = [jnp.concatenate([_sum_slots(self.own[l, n], self.recv[l, n], "sum_chips") for l in range(NLAYER)],
                                axis=0) for n in _BIG]
        sib = _exchange_call(_Swap(part), "swap_big")
        return {n: (p, s) for n, p, s in zip(_BIG, part, sib)}


def _part_rows(shape):
    return -(-math.prod(shape) // _LANES)


def _flat_pad(parts):
    rows = []
    for p in parts:
        v = p.reshape(-1)
        pad = _part_rows(p.shape) * _LANES - v.shape[0]
        rows.append(jnp.concatenate([v, jnp.zeros((pad,), v.dtype)]).reshape(-1, _LANES))
    used = sum(r.shape[0] for r in rows)
    rows.append(jnp.zeros((-used % _PAD_ROWS, _LANES), rows[0].dtype))
    return jnp.concatenate(rows, axis=0)


def _unflat(flat, shapes):
    out, r0 = [], 0
    for s in shapes:
        nr = _part_rows(s)
        out.append(flat[r0:r0 + nr].reshape(-1)[:math.prod(s)].reshape(s))
        r0 += nr
    return out


def kernel(x, c, ctx, c_ctx, w_ada, b_ada, norm_g, w_in, conv_w, attn_sink, ssm_lam_re, ssm_lam_im, ssm_log_dt, ssm_b_re, ssm_b_im, ssm_c_re, ssm_c_im, ssm_d, w_glu, b_glu, w_out, w_mlp_in, w_mlp_out, loss_target, m_c_ctx, m_w_ada, m_b_ada, m_norm_g, m_w_in, m_conv_w, m_attn_sink, m_ssm_lam_re, m_ssm_lam_im, m_ssm_log_dt, m_ssm_b_re, m_ssm_b_im, m_ssm_c_re, m_ssm_c_im, m_ssm_d, m_w_glu, m_b_glu, m_w_out, m_w_mlp_in, m_w_mlp_out, v_c_ctx, v_w_ada, v_b_ada, v_norm_g, v_w_in, v_conv_w, v_attn_sink, v_ssm_lam_re, v_ssm_lam_im, v_ssm_log_dt, v_ssm_b_re, v_ssm_b_im, v_ssm_c_re, v_ssm_c_im, v_ssm_d, v_w_glu, v_b_glu, v_w_out, v_w_mlp_in, v_w_mlp_out):
    W = dict(c_ctx=c_ctx, w_ada=w_ada, b_ada=b_ada, norm_g=norm_g, w_in=w_in, conv_w=conv_w, attn_sink=attn_sink,
             ssm_lam_re=ssm_lam_re, ssm_lam_im=ssm_lam_im, ssm_log_dt=ssm_log_dt, ssm_b_re=ssm_b_re,
             ssm_b_im=ssm_b_im, ssm_c_re=ssm_c_re, ssm_c_im=ssm_c_im, ssm_d=ssm_d, w_glu=w_glu, b_glu=b_glu,
             w_out=w_out, w_mlp_in=w_mlp_in, w_mlp_out=w_mlp_out)
    M = dict(c_ctx=m_c_ctx, w_ada=m_w_ada, b_ada=m_b_ada, norm_g=m_norm_g, w_in=m_w_in, conv_w=m_conv_w,
             attn_sink=m_attn_sink, ssm_lam_re=m_ssm_lam_re, ssm_lam_im=m_ssm_lam_im, ssm_log_dt=m_ssm_log_dt,
             ssm_b_re=m_ssm_b_re, ssm_b_im=m_ssm_b_im, ssm_c_re=m_ssm_c_re, ssm_c_im=m_ssm_c_im, ssm_d=m_ssm_d,
             w_glu=m_w_glu, b_glu=m_b_glu, w_out=m_w_out, w_mlp_in=m_w_mlp_in, w_mlp_out=m_w_mlp_out)
    V = dict(c_ctx=v_c_ctx, w_ada=v_w_ada, b_ada=v_b_ada, norm_g=v_norm_g, w_in=v_w_in, conv_w=v_conv_w,
             attn_sink=v_attn_sink, ssm_lam_re=v_ssm_lam_re, ssm_lam_im=v_ssm_lam_im, ssm_log_dt=v_ssm_log_dt,
             ssm_b_re=v_ssm_b_re, ssm_b_im=v_ssm_b_im, ssm_c_re=v_ssm_c_re, ssm_c_im=v_ssm_c_im, ssm_d=v_ssm_d,
             w_glu=v_w_glu, b_glu=v_b_glu, w_out=v_w_out, w_mlp_in=v_w_mlp_in, w_mlp_out=v_w_mlp_out)
    q_me = 2 * lax.axis_index("x") + lax.axis_index("y")

    ex = _Exchange(W, q_me)
    P = {n: W[n] for n in _SMALL}
    ssh_names = list(_SMALL_SHARDED)
    ssh = _flat_pad([W[n] for n in ssh_names])
    (sall,) = _exchange_call(_Gather([ssh]), "gather_small")
    parts = [_unflat(sall[q], [W[n].shape for n in ssh_names]) for q in range(NCHIP)]
    for k, n in enumerate(ssh_names):
        P[n] = _join_shards(jnp.stack([parts[q][k] for q in range(NCHIP)]), _SMALL_SHARDED[n])

    loss_local, grad_x, G = _local_step(x, c, ctx, loss_target, P, ex)
    loss = lax.psum(loss_local, ("x", "y", "c"))

    sums = ex.finish()
    out = {}
    for n in _BIG:
        flat = lambda t: t.reshape(-1, t.shape[-1])
        res = _adamw(flat(W[n]), *sums[n], flat(M[n]), flat(V[n]), "adamw_" + n)
        out[n] = [t.reshape(W[n].shape) for t in res]

    gsmall = _flat_pad([G[n] for n in _SMALL]).astype(MXU)
    gsum = _sum_slots(None, _gather_all(gsmall, "gather_grads"), "sum_devices")
    gfull = dict(zip(_SMALL, _unflat(gsum, [G[n].shape for n in _SMALL])))
    for n, ax in _SMALL_SHARDED.items():
        width = W[n].shape[ax]
        gfull[n] = lax.dynamic_slice_in_dim(gfull[n], q_me * width, width, axis=ax)
    shapes = [W[n].shape for n in _SMALL]
    gflat = _flat_pad([gfull[n] for n in _SMALL])
    res = _adamw(_flat_pad([W[n] for n in _SMALL]), gflat, jnp.zeros_like(gflat),
                 _flat_pad([M[n] for n in _SMALL]), _flat_pad([V[n] for n in _SMALL]), "adamw_small")
    for k, t in enumerate(res):
        for n, piece in zip(_SMALL, _unflat(t, shapes)):
            out.setdefault(n, [None] * 4)[k] = piece

    return (loss, grad_x, *[out[n][0] for n in _WEIGHTS], *[out[n][1] for n in _WEIGHTS],
            *[out[n][2] for n in _WEIGHTS], *[out[n][3] for n in _WEIGHTS])
```

```python
import functools
import math

import jax
import jax.numpy as jnp
import numpy as np
from jax import lax
from jax.experimental import pallas as pl
from jax.experimental.pallas import tpu as pltpu

F32 = jnp.float32
MXU = jnp.bfloat16

D = 1024
DFF = 4096
NMOD = 6
EPS = 1e-6
HD = 64
NQ = 8
NKV = 2
QPK = 4
AW = 512
KVW = 128
WIN = 128
QB = 128
CW = 256
SW = 256
SGRP = 16
SCH = 16
SST = 64
INW = 1792
GRID_W = 64
ROPE_BASE = 10000.0
NEG = -1e30
SCALE = HD ** -0.5
NLAYER = 4

TB = 256
TF = 1024
TC = 32
NCHIP = 4

LR, B1, B2, AEPS, WD, STEP = 0.001, 0.9, 0.999, 1e-08, 0.01, 10

MESH = pl.DeviceIdType.MESH
SDS = jax.ShapeDtypeStruct
BS = pl.BlockSpec
ANY = pl.BlockSpec(memory_space=pl.ANY)


def _cp(sem, vmem_mb=None):
    kw = dict(dimension_semantics=sem)
    if vmem_mb is not None:
        kw["vmem_limit_bytes"] = vmem_mb * 1024 * 1024
    return pltpu.CompilerParams(**kw)


def _dot(a, b):
    return jnp.dot(a, b, preferred_element_type=F32)


def _dot_nt(a, b):
    return lax.dot_general(a, b, (((1,), (1,)), ((), ())), preferred_element_type=F32)


def _dot_tn(a, b):
    return lax.dot_general(a, b, (((0,), (0,)), ((), ())), preferred_element_type=F32)


def _rms(x):
    r = lax.rsqrt(jnp.mean(x * x, axis=-1, keepdims=True) + EPS)
    return x * r, r


def _rms_bwd(dn, n, r):
    return r * (dn - n * jnp.mean(dn * n, axis=-1, keepdims=True))


_GC = math.sqrt(2.0 / math.pi)


def _gelu(y):
    return 0.5 * y * (1.0 + jnp.tanh(_GC * (y + 0.044715 * y * y * y)))


def _gelu_grad(y):
    th = jnp.tanh(_GC * (y + 0.044715 * y * y * y))
    return 0.5 * (1.0 + th) + 0.5 * y * (1.0 - th * th) * _GC * (1.0 + 3 * 0.044715 * y * y)


def _sigmoid(v):
    return 1.0 / (1.0 + jnp.exp(-v))


def _colsum(x):
    return jnp.sum(x, axis=0, keepdims=True)


def _tok(width, col=0):
    return BS((None, TB, width), lambda b, j: (b, j, col))


def _mod_spec(nct):
    return BS((None, None, 8, D), lambda b, j: (b, jnp.where(j >= nct, 1, 0), 0, 0))


def _full(shape):
    nd = len(shape)
    return BS(shape, lambda *a: (0,) * nd)


def _resident(shape):
    nd = len(shape)
    return BS(shape, lambda *a: (0,) * nd, pipeline_mode=pl.Buffered(1))


def _ada_fwd(cact8, w4, b):
    wq = w4.shape[2]

    def body(c_ref, w_ref, b_ref, o_ref):
        o_ref[...] = _dot(c_ref[...].astype(MXU), w_ref[...]) + b_ref[...]

    return pl.pallas_call(
        body, name="ada_fwd", grid=(NCHIP,),
        in_specs=[BS((8, D), lambda j: (0, 0)), BS((None, D, wq), lambda j: (j, 0, 0)), BS((1, wq), lambda j: (0, j))],
        out_specs=BS((8, wq), lambda j: (0, j)),
        out_shape=SDS((8, NMOD * D), F32), compiler_params=_cp(("arbitrary",)),
    )(cact8, w4, b)


def _ada_bwd_c(dmod8, w4):
    wq = w4.shape[2]

    def body(d_ref, w_ref, o_ref):
        @pl.when(pl.program_id(0) == 0)
        def _():
            o_ref[...] = jnp.zeros_like(o_ref)
        o_ref[...] += _dot_nt(d_ref[...], w_ref[...])

    return pl.pallas_call(
        body, name="ada_bwd_c", grid=(NCHIP,),
        in_specs=[BS((8, wq), lambda j: (0, j)), BS((None, D, wq), lambda j: (j, 0, 0))],
        out_specs=BS((8, D), lambda j: (0, 0)),
        out_shape=SDS((8, D), F32), compiler_params=_cp(("arbitrary",)),
    )(dmod8, w4)


def _pre_mix(h, modtok, ng, w_in, nct):
    B, S, _ = h.shape

    def body(h_ref, mod_ref, g_ref, w_ref, p_ref, a_ref):
        n, _ = _rms(h_ref[...])
        a = ((n * g_ref[0:1, :]) * (1.0 + mod_ref[1:2, :]) + mod_ref[0:1, :]).astype(MXU)
        a_ref[...] = a
        p_ref[...] = _dot(a, w_ref[...])

    return pl.pallas_call(
        body, name="pre_mix", grid=(B, S // TB),
        in_specs=[_tok(D), _mod_spec(nct), _full((8, D)), _full((D, INW))],
        out_specs=[_tok(INW), _tok(D)],
        out_shape=[SDS((B, S, INW), F32), SDS((B, S, D), MXU)],
        compiler_params=_cp(("arbitrary", "arbitrary"), 40),
    )(h, modtok, ng, w_in)


def _pre_bwd(dqk, dv, dcv, du_parts, dh_part, h, modtok, ng, w_in, nct):
    B, S, _ = h.shape

    def body(dqk_ref, dv_ref, dcv_ref, du0_ref, du1_ref, du2_ref, dhp_ref, h_ref, mod_ref, g_ref, w_ref,
             dh_ref, vec_ref, dp_ref):
        du = du0_ref[...] + du1_ref[...] + du2_ref[...]
        dp = jnp.concatenate([dqk_ref[...], dv_ref[...].astype(MXU), dcv_ref[...].astype(MXU), du.astype(MXU)],
                             axis=1)
        dp_ref[...] = dp
        da = _dot_nt(dp, w_ref[...])
        n, r = _rms(h_ref[...])
        g = g_ref[0:1, :]
        sc1 = 1.0 + mod_ref[1:2, :]
        vec_ref[...] = jnp.zeros_like(vec_ref)
        vec_ref[0:1, :] = _colsum(da)
        vec_ref[1:2, :] = _colsum(da * (n * g))
        vec_ref[2:3, :] = _colsum(da * sc1 * n)
        dh_ref[...] = dhp_ref[...] + _rms_bwd(da * sc1 * g, n, r)

    return pl.pallas_call(
        body, name="pre_bwd", grid=(B, S // TB),
        in_specs=[_tok(AW + KVW), _tok(KVW), _tok(3 * CW), _tok(SW), _tok(SW), _tok(SW), _tok(D), _tok(D),
                  _mod_spec(nct), _full((8, D)), _full((D, INW))],
        out_specs=[_tok(D), BS((None, None, 8, D), lambda b, j: (b, j, 0, 0)), _tok(INW)],
        out_shape=[SDS((B, S, D), F32), SDS((B, S // TB, 8, D), F32), SDS((B, S, INW), MXU)],
        compiler_params=_cp(("arbitrary", "arbitrary"), 40),
    )(dqk, dv, dcv, *du_parts, dh_part, h, modtok, ng, w_in)


def _rotate(t, c, s):
    lane = lax.broadcasted_iota(jnp.int32, t.shape, 1)
    partner = jnp.where((lane % 32) < 16, pltpu.roll(t, 112, 1), pltpu.roll(t, 16, 1))
    return t * c + partner * s


def _to_groups(cols):
    lo = lax.broadcasted_iota(jnp.int32, cols[0].shape, 1) < HD
    out = []
    for g in range(QPK):
        a, b = cols[g // 2], cols[2 + g // 2]
        out.append(jnp.where(lo, a, pltpu.roll(b, HD, 1)) if g % 2 == 0 else jnp.where(lo, pltpu.roll(a, HD, 1), b))
    return out


def _from_groups(tiles):
    lo = lax.broadcasted_iota(jnp.int32, tiles[0].shape, 1) < HD
    out = []
    for k in range(NQ // 2):
        a, b = tiles[(2 * k) % QPK], tiles[(2 * k) % QPK + 1]
        out.append(jnp.where(lo, a, pltpu.roll(b, HD, 1)) if k < 2 else jnp.where(lo, pltpu.roll(a, HD, 1), b))
    return out


def _grp(width=128):
    return BS((None, QPK, TB, width), lambda b, j: (b, 0, j, 0))


_TAB = BS((TB, 128), lambda b, j: (j, 0))


def _rope_fwd(p, cos, sin):
    B, S, _ = p.shape

    def body(x_ref, c_ref, s_ref, q_ref, k_ref):
        c, s = c_ref[...], s_ref[...]
        rot = [_rotate(x_ref[:, k * 128:(k + 1) * 128], c, s) for k in range((AW + KVW) // 128)]
        k_ref[...] = rot[AW // 128].astype(MXU)
        for g, t in enumerate(_to_groups(rot[:AW // 128])):
            q_ref[g] = (t * SCALE).astype(MXU)

    return pl.pallas_call(
        body, name="rope_fwd", grid=(B, S // TB),
        in_specs=[_tok(AW + KVW), _TAB, _TAB],
        out_specs=[_grp(), _tok(KVW)],
        out_shape=[SDS((B, QPK, S, 128), MXU), SDS((B, S, KVW), MXU)],
        compiler_params=_cp(("arbitrary", "arbitrary")),
    )(p, cos, sin)


def _rope_bwd(dq4, dk, cos, nsin):
    B, _, S, _ = dq4.shape

    def body(q_ref, k_ref, c_ref, s_ref, o_ref):
        c, s = c_ref[...], s_ref[...]
        cols = _from_groups([q_ref[g] * SCALE for g in range(QPK)]) + [k_ref[...]]
        for k, t in enumerate(cols):
            o_ref[:, k * 128:(k + 1) * 128] = _rotate(t, c, s).astype(MXU)

    return pl.pallas_call(
        body, name="rope_bwd", grid=(B, S // TB),
        in_specs=[_grp(), _tok(KVW), _TAB, _TAB],
        out_specs=_tok(AW + KVW),
        out_shape=SDS((B, S, AW + KVW), MXU),
        compiler_params=_cp(("arbitrary", "arbitrary")),
    )(dq4, dk, cos, nsin)


def _attn_masks(i, S, Lc):
    ncb = Lc // QB
    is_lat = i >= ncb
    w0 = pl.multiple_of(jnp.clip((i - 1) * QB, 0, S - 3 * QB), QB)
    shape = (QPK * QB, 3 * QB)
    qpos = i * QB + jnp.bitwise_and(lax.broadcasted_iota(jnp.int32, shape, 0), QB - 1)
    kpos = w0 + lax.broadcasted_iota(jnp.int32, shape, 1)
    mask = jnp.logical_and(jnp.logical_and(kpos >= Lc, jnp.abs(qpos - kpos) <= WIN), is_lat)
    return w0, mask


def _attn_probs(qh, kch, kwh, mask, sk):
    sc = _dot_nt(qh, kch)
    sw = jnp.where(mask, _dot_nt(qh, kwh), NEG)
    m = jnp.maximum(jnp.maximum(jnp.max(sc, axis=-1, keepdims=True), jnp.max(sw, axis=-1, keepdims=True)), sk)
    ec = jnp.exp(sc - m)
    ew = jnp.exp(sw - m)
    es = jnp.exp(sk - m)
    inv = 1.0 / (jnp.sum(ec, axis=-1, keepdims=True) + jnp.sum(ew, axis=-1, keepdims=True) + es)
    return ec * inv, ew * inv, es * inv


def _sink_col(sink_ref, hk):
    return jnp.concatenate([jnp.full((QB, 1), sink_ref[hk * QPK + g], F32) for g in range(QPK)], axis=0)


_QGRP = BS((None, QPK, QB, 128), lambda b, i: (b, 0, i, 0))


def _first_last(grid):
    first = last = None
    for ax, n in enumerate(grid):
        i = pl.program_id(ax)
        first = (i == 0) if first is None else jnp.logical_and(first, i == 0)
        last = (i == n - 1) if last is None else jnp.logical_and(last, i == n - 1)
    return first, last


def _call_with(ex, body, name, grid, in_specs, out_specs, out_shape, args, scratch, params):
    n_in, n_out = len(in_specs), len(out_specs)
    if ex is None:
        res = pl.pallas_call(body, name=name, grid=grid, in_specs=in_specs, out_specs=out_specs,
                             out_shape=out_shape, scratch_shapes=scratch, compiler_params=params)(*args)
        return list(res), []

    def carrying(*refs):
        own, comm = _carried(ex, refs, n_in, n_out)
        first, last = _first_last(grid)
        pl.when(first)(lambda: ex.start(*comm))
        body(*own)
        pl.when(last)(lambda: ex.wait(*comm))

    res = pl.pallas_call(
        carrying, name=f"{name}_{type(ex).__name__.strip('_').lower()}", grid=grid,
        in_specs=in_specs + ex.specs, out_specs=out_specs + ex.specs, out_shape=out_shape + ex.out_shape,
        scratch_shapes=scratch + ex.sems, compiler_params=params)(*args, *ex.args)
    return list(res[:n_out]), list(res[n_out:])


def _attn_fwd(sink, q4, k, p, Lc, ex=None):
    B, S, _ = k.shape

    def body(sink_ref, q_ref, k_ref, v_ref, o_ref):
        w0, mask = _attn_masks(pl.program_id(1), S, Lc)
        kc = k_ref[0:Lc, :]
        vc = v_ref[0:Lc, :].astype(MXU)
        kw = k_ref[pl.ds(w0, 3 * QB), :]
        vw = v_ref[pl.ds(w0, 3 * QB), :].astype(MXU)
        q = q_ref[...].reshape(QPK * QB, 128)
        outs = []
        for hk in range(NKV):
            cs = slice(hk * HD, (hk + 1) * HD)
            pc, pw, _ = _attn_probs(q[:, cs], kc[:, cs], kw[:, cs], mask, _sink_col(sink_ref, hk))
            o = _dot(pc.astype(MXU), vc[:, cs]) + _dot(pw.astype(MXU), vw[:, cs])
            outs += [o[g * QB:(g + 1) * QB] for g in range(QPK)]
        o_ref[...] = jnp.concatenate(outs, axis=1).astype(o_ref.dtype)

    return _call_with(
        ex, body, "attn_fwd", (B, S // QB),
        [BS(memory_space=pltpu.SMEM), _QGRP, BS((None, S, KVW), lambda b, i: (b, 0, 0)),
         BS((None, S, KVW), lambda b, i: (b, 0, (AW + KVW) // KVW))],
        [BS((None, QB, AW), lambda b, i: (b, i, 0))], [SDS((B, S, AW), MXU)], [sink, q4, k, p], [],
        _cp(("arbitrary", "arbitrary"), 40))


def _attn_bwd(sink, q4, k, p, do4, Lc, ex=None):
    B, S, _ = k.shape

    def body(sink_ref, q_ref, k_ref, v_ref, do_ref, dq_ref, dk_ref, dv_ref, ds_ref):
        i = pl.program_id(1)

        @pl.when(i == 0)
        def _():
            dk_ref[...] = jnp.zeros_like(dk_ref)
            dv_ref[...] = jnp.zeros_like(dv_ref)
            ds_ref[...] = jnp.zeros_like(ds_ref)

        w0, mask = _attn_masks(i, S, Lc)
        kc = k_ref[0:Lc, :]
        vc = v_ref[0:Lc, :].astype(MXU)
        kw = k_ref[pl.ds(w0, 3 * QB), :]
        vw = v_ref[pl.ds(w0, 3 * QB), :].astype(MXU)
        q = q_ref[...].reshape(QPK * QB, 128)
        do = do_ref[...].reshape(QPK * QB, 128)
        dqs, dsinks, dkcs, dkws, dvcs, dvws = [], [], [], [], [], []
        for hk in range(NKV):
            cs = slice(hk * HD, (hk + 1) * HD)
            qh = q[:, cs]
            pc, pw, ps = _attn_probs(qh, kc[:, cs], kw[:, cs], mask, _sink_col(sink_ref, hk))
            dob = do[:, cs].astype(MXU)
            dpc = _dot_nt(dob, vc[:, cs])
            dpw = _dot_nt(dob, vw[:, cs])
            delta = jnp.sum(pc * dpc, axis=-1, keepdims=True) + jnp.sum(pw * dpw, axis=-1, keepdims=True)
            dsc = (pc * (dpc - delta)).astype(MXU)
            dsw = (pw * (dpw - delta)).astype(MXU)
            dqs.append(_dot(dsc, kc[:, cs]) + _dot(dsw, kw[:, cs]))
            dkcs.append(_dot_tn(dsc, qh))
            dkws.append(_dot_tn(dsw, qh))
            dvcs.append(_dot_tn(pc.astype(MXU), dob))
            dvws.append(_dot_tn(pw.astype(MXU), dob))
            psd = ps * delta
            dsinks += [jnp.broadcast_to(-jnp.sum(psd[g * QB:(g + 1) * QB], axis=0, keepdims=True), (1, 128))
                       for g in range(QPK)]
        for g in range(QPK):
            dq_ref[g] = jnp.concatenate([dq[g * QB:(g + 1) * QB] for dq in dqs], axis=1)
        ds_ref[...] += jnp.concatenate(dsinks, axis=0)
        dk_ref[0:Lc, :] += jnp.concatenate(dkcs, axis=1)
        dv_ref[0:Lc, :] += jnp.concatenate(dvcs, axis=1)
        dk_ref[pl.ds(w0, 3 * QB), :] += jnp.concatenate(dkws, axis=1)
        dv_ref[pl.ds(w0, 3 * QB), :] += jnp.concatenate(dvws, axis=1)

    kv_acc = BS((None, S, KVW), lambda b, i: (b, 0, 0))
    return _call_with(
        ex, body, "attn_bwd", (B, S // QB),
        [BS(memory_space=pltpu.SMEM), _QGRP, BS((None, S, KVW), lambda b, i: (b, 0, 0)),
         BS((None, S, KVW), lambda b, i: (b, 0, (AW + KVW) // KVW)), _QGRP],
        [_QGRP, kv_acc, kv_acc, BS((None, 8, 128), lambda b, i: (b, 0, 0))],
        [SDS((B, QPK, S, 128), F32), SDS((B, S, KVW), F32), SDS((B, S, KVW), F32), SDS((B, 8, 128), F32)],
        [sink, q4, k, p, do4], [], _cp(("arbitrary", "arbitrary"), 48))


def _halo_specs(width, col, S):
    nb8 = S // 8
    per = TB // 8
    prev = BS((None, 8, width), lambda b, j: (b, jnp.maximum(j * per - 1, 0), col))
    nxt = BS((None, 8, width), lambda b, j: (b, jnp.minimum((j + 1) * per, nb8 - 1), col))
    return prev, nxt


def _shift_dn(z, prev_row, at_start):
    row = lax.broadcasted_iota(jnp.int32, z.shape, 0)
    first = jnp.where(at_start, 0.0, 1.0) * prev_row
    return jnp.where(row == 0, first, pltpu.roll(z, 1, 0))


def _shift_up(z, next_row, at_end):
    row = lax.broadcasted_iota(jnp.int32, z.shape, 0)
    last = jnp.where(at_end, 0.0, 1.0) * next_row
    return jnp.where(row == z.shape[0] - 1, last, pltpu.roll(z, z.shape[0] - 1, 0))


def _conv_fwd(p, cw8, nct):
    B, S, _ = p.shape
    nt = S // TB

    def body(p_ref, pp_ref, pn_ref, w_ref, o_ref):
        j = pl.program_id(1)
        at_start = jnp.logical_or(j == 0, j == nct)
        at_end = jnp.logical_or(j == nct - 1, j == nt - 1)
        z = p_ref[:, 256:512] * p_ref[:, 512:768]
        zprev = pp_ref[7:8, 256:512] * pp_ref[7:8, 512:768]
        znext = pn_ref[0:1, 256:512] * pn_ref[0:1, 512:768]
        c3 = (_shift_dn(z, zprev, at_start) * w_ref[0:1, :] + z * w_ref[1:2, :]
              + _shift_up(z, znext, at_end) * w_ref[2:3, :])
        o_ref[...] = p_ref[:, 0:256] * c3

    prev, nxt = _halo_specs(3 * CW, 1, S)
    return pl.pallas_call(
        body, name="conv_fwd", grid=(B, nt),
        in_specs=[_tok(3 * CW, 1), prev, nxt, _full((8, CW))],
        out_specs=_tok(CW),
        out_shape=SDS((B, S, CW), F32),
        compiler_params=_cp(("arbitrary", "arbitrary")),
    )(p, p, p, cw8)


def _conv_bwd(p, dcv, cw8, nct):
    B, S, _ = p.shape
    nt = S // TB

    def body(p_ref, pp_ref, pn_ref, d_ref, dp_ref, dn_ref, w_ref, o_ref, dw_ref):
        j = pl.program_id(1)
        at_start = jnp.logical_or(j == 0, j == nct)
        at_end = jnp.logical_or(j == nct - 1, j == nt - 1)
        cb, cc, cx = p_ref[:, 0:256], p_ref[:, 256:512], p_ref[:, 512:768]
        z = cc * cx
        zm = _shift_dn(z, pp_ref[7:8, 256:512] * pp_ref[7:8, 512:768], at_start)
        zp = _shift_up(z, pn_ref[0:1, 256:512] * pn_ref[0:1, 512:768], at_end)
        d = d_ref[...]
        e = d * cb
        em = _shift_dn(e, dp_ref[7:8, :] * pp_ref[7:8, 0:256], at_start)
        ep = _shift_up(e, dn_ref[0:1, :] * pn_ref[0:1, 0:256], at_end)
        dz = e * w_ref[1:2, :] + ep * w_ref[0:1, :] + em * w_ref[2:3, :]
        o_ref[:, 0:256] = d * (zm * w_ref[0:1, :] + z * w_ref[1:2, :] + zp * w_ref[2:3, :])
        o_ref[:, 256:512] = dz * cx
        o_ref[:, 512:768] = dz * cc
        dw_ref[...] = jnp.zeros_like(dw_ref)
        dw_ref[0:1, :] = _colsum(e * zm)
        dw_ref[1:2, :] = _colsum(e * z)
        dw_ref[2:3, :] = _colsum(e * zp)

    prev, nxt = _halo_specs(3 * CW, 1, S)
    dprev, dnxt = _halo_specs(CW, 0, S)
    return pl.pallas_call(
        body, name="conv_bwd", grid=(B, nt),
        in_specs=[_tok(3 * CW, 1), prev, nxt, _tok(CW), dprev, dnxt, _full((8, CW))],
        out_specs=[_tok(3 * CW), BS((None, None, 8, CW), lambda b, j: (b, j, 0, 0))],
        out_shape=[SDS((B, S, 3 * CW), F32), SDS((B, nt, 8, CW), F32)],
        compiler_params=_cp(("arbitrary", "arbitrary")),
    )(p, p, p, dcv, dcv, dcv, cw8)


HW = SGRP * SST


def _rev_chunk(k, nk, ncc):
    return jnp.where(k < ncc, ncc - 1 - k, nk - 1 - k + ncc)


def _scan_perm(B):
    n = B * TC
    pm = np.zeros((2 * n, 2 * n), np.float32)
    for t in range(TC):
        for e in range(B):
            pm[t * 2 * B + e, e * TC + t] = 1.0
            pm[t * 2 * B + B + e, n + e * TC + (TC - 1 - t)] = 1.0
    return pm


def _scan_drive(uf_ref, ub_ref, pm_ref, b4_ref, fwd, dbuf, n):
    u_tok = jnp.concatenate([uf_ref[...].reshape(n, SW), ub_ref[...].reshape(n, SW)], axis=0).astype(MXU)
    u_tm = _dot(pm_ref[...], u_tok).astype(MXU)
    d4 = _dot(u_tm, b4_ref[...])
    dbuf[...] = jnp.where(fwd, d4[:, :2 * HW], d4[:, 2 * HW:])
    return u_tm


def _scan_fwd(p, pm, pmt, b4, a, c2, Lc):
    B, S, _ = p.shape
    nch, n = 2 * B, B * TC
    nk, ncc = S // TC, Lc // TC
    ucol = (INW - SW) // SW

    def body(uf_ref, ub_ref, pm_ref, pmt_ref, b4_ref, a_ref, c2_ref, yf_ref, yb_ref, hb_ref, hcar, dbuf, hbuf):
        @pl.when(pl.program_id(0) == 0)
        def _():
            hcar[...] = jnp.zeros_like(hcar)

        fwd = (lax.broadcasted_iota(jnp.int32, (2 * n, 1), 0) % nch) < B
        _scan_drive(uf_ref, ub_ref, pm_ref, b4_ref, fwd, dbuf, n)
        hb_ref[...] = hcar[...]
        ar, ai = a_ref[:, :HW], a_ref[:, HW:]

        def step(t, h):
            hr, hi = h
            r0 = pl.multiple_of(t * nch, nch)
            nr = ar * hr - ai * hi + dbuf[pl.ds(r0, nch), :HW]
            ni = ar * hi + ai * hr + dbuf[pl.ds(r0, nch), HW:]
            hbuf[pl.ds(r0, nch), :HW] = nr
            hbuf[pl.ds(r0, nch), HW:] = ni
            return nr, ni

        hr, hi = lax.fori_loop(0, TC, step, (hcar[:, :HW], hcar[:, HW:]))
        hcar[:, :HW] = hr
        hcar[:, HW:] = hi
        h_tok = _dot(pmt_ref[...], hbuf[...].astype(MXU)).astype(MXU)
        yf_ref[...] = _dot(h_tok[:n], c2_ref[:, :SW]).reshape(B, TC, SW)
        yb_ref[...] = _dot(h_tok[n:], c2_ref[:, SW:]).reshape(B, TC, SW)

    return pl.pallas_call(
        body, name="scan_fwd", grid=(nk,),
        in_specs=[BS((B, TC, SW), lambda k: (0, k, ucol)),
                  BS((B, TC, SW), lambda k: (0, _rev_chunk(k, nk, ncc), ucol)),
                  _full((2 * n, 2 * n)), _full((2 * n, 2 * n)),
                  _full((SW, 4 * HW)), _full((nch, 2 * HW)), _full((2 * HW, 2 * SW))],
        out_specs=[BS((B, TC, SW), lambda k: (0, k, 0)),
                   BS((B, TC, SW), lambda k: (0, _rev_chunk(k, nk, ncc), 0)),
                   BS((None, nch, 2 * HW), lambda k: (k, 0, 0))],
        out_shape=[SDS((B, S, SW), F32), SDS((B, S, SW), F32), SDS((nk, nch, 2 * HW), F32)],
        scratch_shapes=[pltpu.VMEM((nch, 2 * HW), F32), pltpu.VMEM((2 * n, 2 * HW), F32),
                        pltpu.VMEM((2 * n, 2 * HW), F32)],
        compiler_params=_cp(("arbitrary",), 48),
    )(p, p, pm, pmt, b4, a, c2)


def _scan_bwd(p, dy, hb, pm, pmt, b4, a, c2, Lc, ex=None):
    B, S, _ = p.shape
    nch, n = 2 * B, B * TC
    nk, ncc = S // TC, Lc // TC
    ucol = (INW - SW) // SW
    rows = 2 * n

    def body(uf_ref, ub_ref, dyf_ref, dyb_ref, hb_ref, pm_ref, pmt_ref, b4_ref, a_ref, c2_ref,
             duf_ref, dub_ref, db4_ref, da_ref, dc2_ref, gcar, dbuf, hbuf, gbuf):
        @pl.when(pl.program_id(0) == 0)
        def _():
            gcar[...] = jnp.zeros_like(gcar)
            db4_ref[...] = jnp.zeros_like(db4_ref)
            da_ref[...] = jnp.zeros_like(da_ref)
            dc2_ref[...] = jnp.zeros_like(dc2_ref)

        fwd = (lax.broadcasted_iota(jnp.int32, (rows, 1), 0) % nch) < B
        u_tm = _scan_drive(uf_ref, ub_ref, pm_ref, b4_ref, fwd, dbuf, n)
        ar, ai = a_ref[:, :HW], a_ref[:, HW:]
        hbuf[0:nch, :] = hb_ref[...]

        def fstep(t, h):
            hr, hi = h
            r0 = pl.multiple_of(t * nch, nch)
            nr = ar * hr - ai * hi + dbuf[pl.ds(r0, nch), :HW]
            ni = ar * hi + ai * hr + dbuf[pl.ds(r0, nch), HW:]
            hbuf[pl.ds(r0 + nch, nch), :HW] = nr
            hbuf[pl.ds(r0 + nch, nch), HW:] = ni
            return nr, ni

        lax.fori_loop(0, TC, fstep, (hb_ref[:, :HW], hb_ref[:, HW:]))

        dy_tok = jnp.concatenate([dyf_ref[...].reshape(n, SW), dyb_ref[...].reshape(n, SW)], axis=0).astype(MXU)
        dy_tm = _dot(pm_ref[...], dy_tok)
        dy2 = jnp.concatenate([jnp.where(fwd, dy_tm, 0.0), jnp.where(fwd, 0.0, dy_tm)], axis=1).astype(MXU)
        gbuf[...] = _dot_nt(dy2, c2_ref[...])
        dc2_ref[...] += _dot_tn(hbuf[nch:, :].astype(MXU), dy2)

        def bstep(i, cary):
            cr, ci = cary
            r0 = pl.multiple_of((TC - 1 - i) * nch, nch)
            gr = gbuf[pl.ds(r0, nch), :HW] + cr
            gi = gbuf[pl.ds(r0, nch), HW:] + ci
            gbuf[pl.ds(r0, nch), :HW] = gr
            gbuf[pl.ds(r0, nch), HW:] = gi
            return ar * gr + ai * gi, ar * gi - ai * gr

        cr, ci = lax.fori_loop(0, TC, bstep, (gcar[:, :HW], gcar[:, HW:]))
        gcar[:, :HW] = cr
        gcar[:, HW:] = ci

        gr = gbuf[:, :HW].reshape(TC, nch, HW)
        gi = gbuf[:, HW:].reshape(TC, nch, HW)
        hpr = hbuf[0:rows, :HW].reshape(TC, nch, HW)
        hpi = hbuf[0:rows, HW:].reshape(TC, nch, HW)
        da_ref[:, :HW] += jnp.sum(gr * hpr + gi * hpi, axis=0)
        da_ref[:, HW:] += jnp.sum(gi * hpr - gr * hpi, axis=0)

        g = gbuf[...]
        gm = g.astype(MXU)
        dd4 = jnp.concatenate([jnp.where(fwd, g, 0.0), jnp.where(fwd, 0.0, g)], axis=1).astype(MXU)
        db4_ref[...] += _dot_tn(u_tm, dd4)
        g_tok = _dot(pmt_ref[...], gm).astype(MXU)
        duf_ref[...] = _dot_nt(g_tok[:n], b4_ref[:, :2 * HW]).reshape(B, TC, SW)
        dub_ref[...] = _dot_nt(g_tok[n:], b4_ref[:, 2 * HW:]).reshape(B, TC, SW)

    fwd_blk = lambda col: BS((B, TC, SW), lambda k: (0, nk - 1 - k, col))
    bwd_blk = lambda col: BS((B, TC, SW), lambda k: (0, _rev_chunk(nk - 1 - k, nk, ncc), col))
    return _call_with(
        ex, body, "scan_bwd", (nk,),
        [fwd_blk(ucol), bwd_blk(ucol), fwd_blk(0), bwd_blk(0),
         BS((None, nch, 2 * HW), lambda k: (nk - 1 - k, 0, 0)),
         _full((rows, rows)), _full((rows, rows)),
         _full((SW, 4 * HW)), _full((nch, 2 * HW)), _full((2 * HW, 2 * SW))],
        [fwd_blk(0), bwd_blk(0), _full((SW, 4 * HW)), _full((nch, 2 * HW)), _full((2 * HW, 2 * SW))],
        [SDS((B, S, SW), F32), SDS((B, S, SW), F32), SDS((SW, 4 * HW), F32), SDS((nch, 2 * HW), F32),
         SDS((2 * HW, 2 * SW), F32)],
        [p, p, dy, dy, hb, pm, pmt, b4, a, c2],
        [pltpu.VMEM((nch, 2 * HW), F32), pltpu.VMEM((rows, 2 * HW), F32),
         pltpu.VMEM((rows + nch, 2 * HW), F32), pltpu.VMEM((rows, 2 * HW), F32)],
        _cp(("arbitrary",), 56))


def _post_mix(o, cv, yf, yb, p, h, modtok, ng, dsk, wg, bg, wo, nct):
    B, S, _ = h.shape

    def body(o_ref, cv_ref, yf_ref, yb_ref, u_ref, h_ref, mod_ref, g_ref, dsk_ref, wg_ref, bg_ref, wo_ref,
             h1_ref, mix_ref, m_ref, y_ref):
        y = yf_ref[...] + yb_ref[...] + dsk_ref[...] * u_ref[...]
        g = _gelu(y)
        s = g * _sigmoid(_dot(g.astype(MXU), wg_ref[...]) + bg_ref[...])
        mix_ref[:, 0:AW] = o_ref[...]
        mix_ref[:, AW:AW + CW] = cv_ref[...].astype(MXU)
        mix_ref[:, AW + CW:] = s.astype(MXU)
        m = _dot(mix_ref[...], wo_ref[...])
        n, _ = _rms(m)
        h1_ref[...] = h_ref[...] + mod_ref[2:3, :] * (n * g_ref[1:2, :])
        m_ref[...] = m
        y_ref[...] = y

    return pl.pallas_call(
        body, name="post_mix", grid=(B, S // TB),
        in_specs=[_tok(AW), _tok(CW), _tok(SW), _tok(SW), _tok(SW, (INW - SW) // SW), _tok(D), _mod_spec(nct),
                  _full((8, D)), _full((1, SW)), _full((SW, SW)), _full((1, SW)), _full((D, D))],
        out_specs=[_tok(D), _tok(D), _tok(D), _tok(SW)],
        out_shape=[SDS((B, S, D), F32), SDS((B, S, D), MXU), SDS((B, S, D), F32), SDS((B, S, SW), F32)],
        compiler_params=_cp(("arbitrary", "arbitrary"), 40),
    )(o, cv, yf, yb, p, h, modtok, ng, dsk, wg, bg, wo)


def _post_bwd(dh1, m, y, p, modtok, ng, dsk, wg, bg, wo, nct):
    B, S, _ = m.shape

    def body(dh_ref, m_ref, y_ref, u_ref, mod_ref, g_ref, dsk_ref, wg_ref, bg_ref, wo_ref,
             dm_ref, da_ref, dc_ref, dy_ref, du_ref, vec_ref, vec2_ref, dwg_ref):
        @pl.when(jnp.logical_and(pl.program_id(0) == 0, pl.program_id(1) == 0))
        def _():
            dwg_ref[...] = jnp.zeros_like(dwg_ref)

        n, r = _rms(m_ref[...])
        dh1 = dh_ref[...]
        gpm = g_ref[1:2, :]
        dr = dh1 * mod_ref[2:3, :]
        vec_ref[...] = jnp.zeros_like(vec_ref)
        vec_ref[0:1, :] = _colsum(dh1 * (n * gpm))
        vec_ref[1:2, :] = _colsum(dr * n)
        dm = _rms_bwd(dr * gpm, n, r).astype(MXU)
        dm_ref[...] = dm
        dmix = _dot_nt(dm, wo_ref[...])
        for g, t in enumerate(_to_groups([dmix[:, k * 128:(k + 1) * 128] for k in range(AW // 128)])):
            da_ref[g] = t
        dc_ref[...] = dmix[:, AW:AW + CW]
        ds = dmix[:, AW + CW:]
        yv = y_ref[...]
        g = _gelu(yv)
        gb = g.astype(MXU)
        sg = _sigmoid(_dot(gb, wg_ref[...]) + bg_ref[...])
        dv = ds * g * sg * (1.0 - sg)
        dvb = dv.astype(MXU)
        dg = ds * sg + _dot_nt(dvb, wg_ref[...])
        dwg_ref[...] += _dot_tn(gb, dvb)
        dy = dg * _gelu_grad(yv)
        dy_ref[...] = dy
        du_ref[...] = dy * dsk_ref[...]
        vec2_ref[...] = jnp.zeros_like(vec2_ref)
        vec2_ref[0:1, :] = _colsum(dy * u_ref[...])
        vec2_ref[1:2, :] = _colsum(dv)

    nt = S // TB
    return pl.pallas_call(
        body, name="post_bwd", grid=(B, nt),
        in_specs=[_tok(D), _tok(D), _tok(SW), _tok(SW, (INW - SW) // SW), _mod_spec(nct), _full((8, D)),
                  _full((1, SW)), _full((SW, SW)), _full((1, SW)), _full((D, D))],
        out_specs=[_tok(D), _grp(), _tok(CW), _tok(SW), _tok(SW),
                   BS((None, None, 8, D), lambda b, j: (b, j, 0, 0)),
                   BS((None, None, 8, SW), lambda b, j: (b, j, 0, 0)), _full((SW, SW))],
        out_shape=[SDS((B, S, D), MXU), SDS((B, QPK, S, 128), F32), SDS((B, S, CW), F32), SDS((B, S, SW), F32),
                   SDS((B, S, SW), F32), SDS((B, nt, 8, D), F32), SDS((B, nt, 8, SW), F32), SDS((SW, SW), F32)],
        compiler_params=_cp(("arbitrary", "arbitrary"), 40),
    )(dh1, m, y, p, modtok, ng, dsk, wg, bg, wo)


def _mlp_fwd(h1, modtok, ng, w1, w2, nct, ex=None):
    B, S, _ = h1.shape
    assert w1.shape == (DFF // TF, D, TF)

    def body(h_ref, mod_ref, g_ref, w1_ref, w2_ref, h2_ref, a2_ref, z_ref, f_ref):
        n, _ = _rms(h_ref[...])
        a2 = ((n * g_ref[2:3, :]) * (1.0 + mod_ref[4:5, :]) + mod_ref[3:4, :]).astype(MXU)
        a2_ref[...] = a2
        ff = jnp.zeros((TB, D), F32)
        for c in range(DFF // TF):
            cs = slice(c * TF, (c + 1) * TF)
            z = _dot(a2, w1_ref[c])
            z_ref[:, cs] = z.astype(MXU)
            r = jnp.maximum(z, 0.0)
            ff = ff + _dot((r * r).astype(MXU), w2_ref[cs, :])
        f_ref[...] = ff
        n, _ = _rms(ff)
        h2_ref[...] = h_ref[...] + mod_ref[5:6, :] * (n * g_ref[3:4, :])

    return _call_with(
        ex, body, "mlp_fwd", (B, S // TB),
        [_tok(D), _mod_spec(nct), _full((8, D)), _resident((DFF // TF, D, TF)), _resident((DFF, D))],
        [_tok(D), _tok(D), _tok(DFF), _tok(D)],
        [SDS((B, S, D), F32), SDS((B, S, D), MXU), SDS((B, S, DFF), MXU), SDS((B, S, D), F32)],
        [h1, modtok, ng, w1, w2], [], _cp(("arbitrary", "arbitrary"), 48))


def _mlp_bwd(dh2, h1, f, z, modtok, ng, w1, w2, nct, ex=None):
    B, S, _ = h1.shape

    def body(dh_ref, h_ref, f_ref, z_ref, mod_ref, g_ref, w1_ref, w2_ref, dh1_ref, df_ref, dz_ref, vec_ref):
        n, r = _rms(f_ref[...])
        dh2 = dh_ref[...]
        gp = g_ref[3:4, :]
        dr = dh2 * mod_ref[5:6, :]
        vec_ref[...] = jnp.zeros_like(vec_ref)
        vec_ref[3:4, :] = _colsum(dh2 * (n * gp))
        vec_ref[4:5, :] = _colsum(dr * n)
        df = _rms_bwd(dr * gp, n, r).astype(MXU)
        df_ref[...] = df
        da = jnp.zeros((TB, D), F32)
        for c in range(DFF // TF):
            cs = slice(c * TF, (c + 1) * TF)
            dr2 = _dot_nt(df, w2_ref[cs, :])
            dz = (dr2 * 2.0 * jnp.maximum(z_ref[:, cs].astype(F32), 0.0)).astype(MXU)
            dz_ref[:, cs] = dz
            da = da + _dot_nt(dz, w1_ref[c])
        n, r = _rms(h_ref[...])
        g = g_ref[2:3, :]
        sc1 = 1.0 + mod_ref[4:5, :]
        vec_ref[0:1, :] = _colsum(da)
        vec_ref[1:2, :] = _colsum(da * (n * g))
        vec_ref[2:3, :] = _colsum(da * sc1 * n)
        dh1_ref[...] = dh2 + _rms_bwd(da * sc1 * g, n, r)

    return _call_with(
        ex, body, "mlp_bwd", (B, S // TB),
        [_tok(D), _tok(D), _tok(D), _tok(DFF), _mod_spec(nct), _full((8, D)),
         _resident((DFF // TF, D, TF)), _resident((DFF, D))],
        [_tok(D), _tok(D), _tok(DFF), BS((None, None, 8, D), lambda b, j: (b, j, 0, 0))],
        [SDS((B, S, D), F32), SDS((B, S, D), MXU), SDS((B, S, DFF), MXU), SDS((B, S // TB, 8, D), F32)],
        [dh2, h1, f, z, modtok, ng, w1, w2], [], _cp(("arbitrary", "arbitrary"), 56))


def _loss(h, target, Lc):
    B, S, _ = h.shape
    nt, nct = S // TB, Lc // TB

    def body(h_ref, t_ref, dh_ref, l_ref):
        lat = pl.program_id(1) >= nct
        err = jnp.where(lat, h_ref[...] - t_ref[...], 0.0)
        dh_ref[...] = err * (1.0 / D)
        l_ref[...] = jnp.broadcast_to(jnp.sum(err * err, keepdims=True), (8, 128))

    return pl.pallas_call(
        body, name="loss", grid=(B, nt),
        in_specs=[_tok(D), BS((None, TB, D), lambda b, j: (b, jnp.maximum(j - nct, 0), 0))],
        out_specs=[_tok(D), BS((None, None, 8, 128), lambda b, j: (b, j, 0, 0))],
        out_shape=[SDS((B, S, D), F32), SDS((B, nt, 8, 128), F32)],
        compiler_params=_cp(("arbitrary", "arbitrary")),
    )(h, target)


def _mm_tn(a, b, name, relu2=False):
    T, M = a.shape
    N = b.shape[1]
    def blk(n):
        return max(b for b in range(128, 1025, 128) if n % b == 0)

    bm, bn = blk(M), blk(N)
    tk = next(t for t in (1536, 1024, 512, 256, T) if T % t == 0)
    nkk = T // tk

    def body(a_ref, b_ref, o_ref, om_ref):
        k = pl.program_id(2)

        @pl.when(k == 0)
        def _():
            o_ref[...] = jnp.zeros_like(o_ref)

        av = a_ref[...]
        if relu2:
            r = jnp.maximum(av.astype(F32), 0.0)
            av = (r * r).astype(MXU)
        o_ref[...] += _dot_tn(av, b_ref[...])

        @pl.when(k == nkk - 1)
        def _():
            om_ref[...] = o_ref[...].astype(MXU)

    out = BS((bm, bn), lambda i, j, k: (i, j))
    return pl.pallas_call(
        body, name=name, grid=(M // bm, N // bn, nkk),
        in_specs=[BS((tk, bm), lambda i, j, k: (k, i)), BS((tk, bn), lambda i, j, k: (k, j))],
        out_specs=[out, out], out_shape=[SDS((M, N), F32), SDS((M, N), MXU)],
        compiler_params=_cp(("arbitrary", "arbitrary", "arbitrary"), 56),
    )(a, b)


_PAD_ROWS = 64


def _block_rows(R):
    for br in (256, _PAD_ROWS):
        if R % br == 0:
            return br
    raise ValueError(f"row count {R} is not a multiple of {_PAD_ROWS}")


def _adamw(w, ga, gb, m, v, name):
    R, C = w.shape
    br = _block_rows(R) if C <= _LANES else _PAD_ROWS

    def body(w_ref, ga_ref, gb_ref, m_ref, v_ref, g_out, d_out, m_out, v_out):
        g = ga_ref[...] + gb_ref[...]
        m2 = B1 * m_ref[...] + (1.0 - B1) * g
        v2 = B2 * v_ref[...] + (1.0 - B2) * (g * g)
        m_hat = m2 / (1.0 - B1 ** STEP)
        v_hat = v2 / (1.0 - B2 ** STEP)
        g_out[...] = g
        d_out[...] = -LR * (m_hat / (jnp.sqrt(v_hat) + AEPS) + WD * w_ref[...])
        m_out[...] = m2
        v_out[...] = v2

    own = BS((br, C), lambda i: (i, 0))
    return pl.pallas_call(
        body, name=name, grid=(R // br,),
        in_specs=[own] * 5, out_specs=[own] * 4, out_shape=[SDS((R, C), F32)] * 4,
        compiler_params=_cp(("arbitrary",)),
    )(w, ga, gb, m, v)


def _my_xyc():
    return lax.axis_index("x"), lax.axis_index("y"), lax.axis_index("c")


def _chip_peers(x, y):
    return [(1 - x, y), (x, 1 - y), (1 - x, 1 - y)]


class _Gather:
    def __init__(self, shards):
        self.args = list(shards)
        self.n = len(self.args)
        self.specs = [ANY] * self.n
        self.out_shape = [SDS((NCHIP,) + a.shape, a.dtype) for a in self.args]
        self.sems = [pltpu.SemaphoreType.DMA((3 * self.n,)), pltpu.SemaphoreType.DMA((3 * self.n,)),
                     pltpu.SemaphoreType.DMA((self.n,))]

    def _copies(self, ins, outs, ssem, rsem, lsem):
        x, y, c = _my_xyc()
        mine, sends, recvs = [], [], []
        for k, (s_ref, o_ref) in enumerate(zip(ins, outs)):
            mine.append(pltpu.make_async_copy(s_ref, o_ref.at[2 * x + y], lsem.at[k]))
            for j, (px, py) in enumerate(_chip_peers(x, y)):
                sems = dict(send_sem=ssem.at[3 * k + j], recv_sem=rsem.at[3 * k + j], device_id=(px, py, c),
                            device_id_type=MESH)
                sends.append(pltpu.make_async_remote_copy(src_ref=s_ref, dst_ref=o_ref.at[2 * x + y], **sems))
                recvs.append(pltpu.make_async_remote_copy(src_ref=s_ref, dst_ref=o_ref.at[2 * px + py], **sems))
        return mine, sends, recvs

    def start(self, ins, outs, sems):
        mine, sends, _ = self._copies(ins, outs, *sems)
        for cp in mine + sends:
            cp.start()

    def wait(self, ins, outs, sems):
        mine, sends, recvs = self._copies(ins, outs, *sems)
        for cp in recvs:
            cp.wait_recv()
        for cp in sends:
            cp.wait_send()
        for cp in mine:
            cp.wait()


class _Gather2(_Gather):
    def __init__(self, shards):
        super().__init__(shards)
        self.sems = self.sems + [pltpu.SemaphoreType.DMA((3 * self.n,)), pltpu.SemaphoreType.DMA((3 * self.n,))]

    def _copies2(self, ins, outs, ssem, rsem, lsem, fsem, gsem):
        x, y, c = _my_xyc()
        mine, sends, recvs, passes, got = [], [], [], [], []
        for k, (s_ref, o_ref) in enumerate(zip(ins, outs)):
            half = s_ref.shape[0] // 2
            my_rows, sib_rows = pl.ds(c * half, half), pl.ds((1 - c) * half, half)
            mine.append(pltpu.make_async_copy(s_ref, o_ref.at[2 * x + y], lsem.at[k]))
            for j, (px, py) in enumerate(_chip_peers(x, y)):
                ici = dict(send_sem=ssem.at[3 * k + j], recv_sem=rsem.at[3 * k + j], device_id=(px, py, c),
                           device_id_type=MESH)
                d2d = dict(send_sem=fsem.at[3 * k + j], recv_sem=gsem.at[3 * k + j], device_id=(x, y, 1 - c),
                           device_id_type=MESH)
                landed = o_ref.at[2 * px + py, my_rows]
                sends.append(pltpu.make_async_remote_copy(src_ref=s_ref.at[my_rows],
                                                          dst_ref=o_ref.at[2 * x + y, my_rows], **ici))
                recvs.append(pltpu.make_async_remote_copy(src_ref=s_ref.at[my_rows], dst_ref=landed, **ici))
                passes.append(pltpu.make_async_remote_copy(src_ref=landed, dst_ref=landed, **d2d))
                got.append(pltpu.make_async_remote_copy(src_ref=landed, dst_ref=o_ref.at[2 * px + py, sib_rows],
                                                        **d2d))
        return mine, sends, recvs, passes, got

    def start(self, ins, outs, sems):
        mine, sends, _, _, _ = self._copies2(ins, outs, *sems)
        for cp in mine + sends:
            cp.start()

    def wait(self, ins, outs, sems):
        mine, sends, recvs, passes, got = self._copies2(ins, outs, *sems)
        for landed, onward in zip(recvs, passes):
            landed.wait_recv()
            onward.start()
        for cp in got:
            cp.wait_recv()
        for cp in sends + passes:
            cp.wait_send()
        for cp in mine:
            cp.wait()


class _Scatter:
    def __init__(self, sends, cols=None):
        self.args = list(sends)
        self.n = len(self.args)
        self.cols = list(cols) if cols is not None else [None] * self.n
        self.specs = [ANY] * self.n
        self.out_shape = [SDS((3,) + a.shape[1:] if cw is None else (3, a.shape[0], cw), a.dtype)
                          for a, cw in zip(self.args, self.cols)]
        self.sems = [pltpu.SemaphoreType.DMA((3 * self.n,)), pltpu.SemaphoreType.DMA((3 * self.n,))]

    def _copies(self, ins, outs, ssem, rsem):
        x, y, c = _my_xyc()

        def piece(k, q):
            cw = self.cols[k]
            return ins[k].at[q] if cw is None else ins[k].at[:, pl.ds(pl.multiple_of(q * cw, 128), cw)]

        return [pltpu.make_async_remote_copy(
            src_ref=piece(k, 2 * px + py), dst_ref=outs[k].at[j], send_sem=ssem.at[3 * k + j],
            recv_sem=rsem.at[3 * k + j], device_id=(px, py, c), device_id_type=MESH)
            for k in range(self.n) for j, (px, py) in enumerate(_chip_peers(x, y))]

    def start(self, ins, outs, sems):
        for cp in self._copies(ins, outs, *sems):
            cp.start()

    def wait(self, ins, outs, sems):
        for cp in self._copies(ins, outs, *sems):
            cp.wait()


class _Swap(_Scatter):
    def __init__(self, arrays):
        self.args = list(arrays)
        self.n = len(self.args)
        self.specs = [ANY] * self.n
        self.out_shape = [SDS(a.shape, a.dtype) for a in self.args]
        self.sems = [pltpu.SemaphoreType.DMA((self.n,)), pltpu.SemaphoreType.DMA((self.n,))]

    def _copies(self, ins, outs, ssem, rsem):
        x, y, c = _my_xyc()
        return [pltpu.make_async_remote_copy(src_ref=s_ref, dst_ref=o_ref, send_sem=ssem.at[k], recv_sem=rsem.at[k],
                                             device_id=(x, y, 1 - c), device_id_type=MESH)
                for k, (s_ref, o_ref) in enumerate(zip(ins, outs))]


def _exchange_call(ex, name):
    n = ex.n

    def body(*refs):
        ins, outs, sems = refs[:n], refs[n:2 * n], refs[2 * n:]
        ex.start(ins, outs, sems)
        ex.wait(ins, outs, sems)

    return pl.pallas_call(body, name=name, in_specs=ex.specs, out_specs=ex.specs, out_shape=ex.out_shape,
                          scratch_shapes=ex.sems)(*ex.args)


def _carried(ex, refs, n_in, n_out):
    n = ex.n
    ins, cin = refs[:n_in], refs[n_in:n_in + n]
    outs, cout = refs[n_in + n:n_in + n + n_out], refs[n_in + n + n_out:n_in + 2 * n + n_out]
    rest = refs[n_in + 2 * n + n_out:]
    nsem = len(ex.sems)
    return list(ins) + list(outs) + list(rest[:len(rest) - nsem]), (cin, cout, rest[len(rest) - nsem:])


def _gather_all(v, name):
    R, C = v.shape

    def body(s_ref, o_ref, ssem, rsem, lsem):
        x, y, c = _my_xyc()
        me = 4 * x + 2 * y + c
        mine = pltpu.make_async_copy(s_ref, o_ref.at[me], lsem)
        mine.start()

        def peer(j):
            fx, fy, fc = (j >> 2) & 1, (j >> 1) & 1, j & 1
            return (x ^ fx, y ^ fy, c ^ fc)

        cps = []
        for j in range(1, 8):
            cps.append(pltpu.make_async_remote_copy(
                src_ref=s_ref, dst_ref=o_ref.at[me], send_sem=ssem.at[j - 1], recv_sem=rsem.at[j - 1],
                device_id=peer(j), device_id_type=MESH))
            cps[-1].start()
        for j in range(1, 8):
            px, py, pc = peer(j)
            pltpu.make_async_remote_copy(
                src_ref=s_ref, dst_ref=o_ref.at[4 * px + 2 * py + pc], send_sem=ssem.at[j - 1],
                recv_sem=rsem.at[j - 1], device_id=peer(j), device_id_type=MESH).wait_recv()
        for cp in cps:
            cp.wait_send()
        mine.wait()

    return pl.pallas_call(
        body, name=name, in_specs=[ANY], out_specs=ANY, out_shape=SDS((8, R, C), v.dtype),
        scratch_shapes=[pltpu.SemaphoreType.DMA((7,)), pltpu.SemaphoreType.DMA((7,)), pltpu.SemaphoreType.DMA],
    )(v)


def _sum_slots(own, slots, name):
    n, R, C = slots.shape
    br = _block_rows(R)

    def body(*refs):
        o_ref = refs[-1]
        if own is None:
            acc = refs[0][0].astype(F32)
            first = 1
            s_ref = refs[0]
        else:
            acc = refs[0][...]
            first = 0
            s_ref = refs[1]
        for k in range(first, n):
            acc = acc + s_ref[k].astype(F32)
        o_ref[...] = acc

    row = BS((br, C), lambda i: (i, 0))
    slab = BS((n, br, C), lambda i: (0, i, 0))
    ins, args = ([slab], [slots]) if own is None else ([row, slab], [own, slots])
    return pl.pallas_call(
        body, name=name, grid=(R // br,), in_specs=ins, out_specs=row, out_shape=SDS((R, C), F32),
        compiler_params=_cp(("arbitrary",)),
    )(*args)


def _rope_tables(L, Lc):
    n = jnp.arange(L)
    row = (n // GRID_W).astype(F32)
    col = (n % GRID_W).astype(F32)
    freqs = ROPE_BASE ** (-jnp.arange(16, dtype=F32) / 16)
    lane = jnp.arange(128)
    dd = lane % HD
    fr = freqs[dd % 16]
    ang = jnp.where(dd < 32, row[:, None], col[:, None]) * fr[None, :]
    sign = jnp.where((dd % 32) < 16, -1.0, 1.0)
    cos = jnp.concatenate([jnp.ones((Lc, 128), F32), jnp.cos(ang)], axis=0)
    sin = jnp.concatenate([jnp.zeros((Lc, 128), F32), jnp.sin(ang) * sign[None, :]], axis=0)
    return cos, sin


def _ssm_prep(lam_re, lam_im, log_dt, b_re, b_im, c_re, c_im, B):
    dt = jnp.exp(log_dt)[..., None]
    mag = jnp.exp(lam_re * dt)
    ar = mag * jnp.cos(lam_im * dt)
    ai = mag * jnp.sin(lam_im * dt)
    den = lam_re * lam_re + lam_im * lam_im
    kr = ((ar - 1.0) * lam_re + ai * lam_im) / den
    ki = (ai * lam_re - (ar - 1.0) * lam_im) / den
    bbr = kr[..., None] * b_re - ki[..., None] * b_im
    bbi = kr[..., None] * b_im + ki[..., None] * b_re
    eye = jnp.eye(SGRP, dtype=F32)

    def bblk(t):
        return jnp.einsum("gpi,gh->gihp", t, eye).reshape(SGRP * SCH, SGRP * SST)

    def cblk(t):
        return jnp.einsum("gip,gh->gphi", t, eye).reshape(SGRP * SST, SGRP * SCH)

    b4 = jnp.concatenate([bblk(bbr[0]), bblk(bbi[0]), bblk(bbr[1]), bblk(bbi[1])], axis=1)
    c2 = jnp.concatenate([jnp.concatenate([cblk(c_re[0]), -cblk(c_im[0])], axis=0),
                          jnp.concatenate([cblk(c_re[1]), -cblk(c_im[1])], axis=0)], axis=1)
    a2 = jnp.concatenate([ar.reshape(2, -1), ai.reshape(2, -1)], axis=1)
    a = jnp.repeat(a2, B, axis=0)
    return a, b4, c2


def _pad_rows(v, rows=8):
    return jnp.concatenate([v, jnp.zeros((rows - v.shape[0],) + v.shape[1:], v.dtype)], axis=0)


_BIG = ("w_ada", "w_in", "w_out", "w_mlp_in", "w_mlp_out")
_BIG_AXIS = {"w_ada": 2, "w_in": 2, "w_out": 1, "w_mlp_in": 2, "w_mlp_out": 1}


def _split_shards(g, axis):
    L, r, c = g.shape
    if axis == 2:
        return jnp.transpose(g.reshape(L, r, NCHIP, c // NCHIP), (2, 0, 1, 3))
    return jnp.transpose(g.reshape(L, NCHIP, r // NCHIP, c), (1, 0, 2, 3))


def _join_shards(s, axis):
    _, L, r, c = s.shape
    if axis == 2:
        return jnp.transpose(s, (1, 2, 0, 3)).reshape(L, r, NCHIP * c)
    return jnp.transpose(s, (1, 0, 2, 3)).reshape(L, NCHIP * r, c)


def _local_step(x, c, ctx, loss_target, P, ex):
    B, L, _ = x.shape
    Lc = ctx.shape[1]
    S = Lc + L
    nct = Lc // TB
    nt = S // TB
    nch = 2 * B
    cos, sin = _rope_tables(L, Lc)
    perm = _scan_perm(B)
    pm, pmt = jnp.asarray(perm, MXU), jnp.asarray(perm.T, MXU)

    c_act = jax.nn.silu(c)
    cc_act, silu_vjp = jax.vjp(jax.nn.silu, P["c_ctx"])
    cact8 = _pad_rows(jnp.concatenate([c_act, cc_act[None, :]], axis=0))
    h = jnp.concatenate([ctx, x], axis=1)

    ssm_names = ("ssm_lam_re", "ssm_lam_im", "ssm_log_dt", "ssm_b_re", "ssm_b_im", "ssm_c_re", "ssm_c_im")
    (a_all, b4_all, c2_all), prep_vjp = jax.vjp(jax.vmap(lambda *t: _ssm_prep(*t, B)), *[P[n] for n in ssm_names])
    b4m_all, c2m_all = b4_all.astype(MXU), c2_all.astype(MXU)
    ng_all = jnp.pad(P["norm_g"], ((0, 0), (0, 4), (0, 0)))
    cw8_all = jnp.pad(P["conv_w"], ((0, 0), (0, 8 - P["conv_w"].shape[1]), (0, 0)))
    wg_all = P["w_glu"].astype(MXU)

    saved = []
    for l in range(NLAYER):
        Wl = ex.weights(l)
        mod8 = _ada_fwd(cact8, Wl["w_ada"], P["b_ada"][l][None, :])
        mod6 = mod8.reshape(8, NMOD, D)
        modtok = jnp.stack([jnp.broadcast_to(mod6[B], (B, NMOD, D)), mod6[:B]], axis=1)
        modtok = jnp.pad(modtok, ((0, 0), (0, 0), (0, 2), (0, 0)))
        ng, cw8, a_s, b4m, c2m, wg = ng_all[l], cw8_all[l], a_all[l], b4m_all[l], c2m_all[l], wg_all[l]
        dsk = P["ssm_d"][l][None, :]
        bg = P["b_glu"][l][None, :]

        p, a = _pre_mix(h, modtok, ng, Wl["w_in"], nct)
        q4, kr = _rope_fwd(p, cos, sin)
        (o,) = ex.attn_fwd(l, P["attn_sink"][l], q4, kr, p, Lc)
        cv = _conv_fwd(p, cw8, nct)
        yf, yb, hb = _scan_fwd(p, pm, pmt, b4m, a_s, c2m, Lc)
        h1, mix, m, y = _post_mix(o, cv, yf, yb, p, h, modtok, ng, dsk, wg, bg, Wl["w_out"], nct)
        h2, a2, z, f = ex.mlp_fwd(l, h1, modtok, ng, Wl["w_mlp_in"], Wl["w_mlp_out"], nct)
        saved.append(dict(Wl=Wl, h=h, modtok=modtok, ng=ng, cw8=cw8, a_s=a_s, b4m=b4m, c2m=c2m, wg=wg, dsk=dsk, bg=bg,
                          p=p, a=a, q4=q4, kr=kr, hb=hb, mix=mix, m=m, y=y, h1=h1, a2=a2, z=z, f=f))
        h = h2

    dh, lpart = _loss(h, loss_target, Lc)
    loss_local = 0.5 / D * jnp.sum(lpart[:, :, 0, 0])

    R = {k: [None] * NLAYER for k in ("w_glu", "vec_p2", "scan", "dcw", "dsink", "vec", "b_ada")}
    dcact = jnp.zeros((8, D), F32)
    T = B * S
    for l in reversed(range(NLAYER)):
        sv = saved[l]
        modtok, ng, Wl = sv["modtok"], sv["ng"], sv["Wl"]
        dh1, df, dz, vec_m = ex.mlp_bwd(l, dh, sv["h1"], sv["f"], sv["z"], modtok, ng, Wl["w_mlp_in"],
                                        Wl["w_mlp_out"], nct)
        ex.submit(l, "w_mlp_out", *_mm_tn(sv["z"].reshape(T, DFF), df.reshape(T, D), "dw_mlp_out", relu2=True))
        ex.submit(l, "w_mlp_in", *_mm_tn(sv["a2"].reshape(T, D), dz.reshape(T, DFF), "dw_mlp_in"))
        dm, dattn, dcv, dy, du_skip, vec_p, vec_p2, dwg = _post_bwd(
            dh1, sv["m"], sv["y"], sv["p"], modtok, ng, sv["dsk"], sv["wg"], sv["bg"], Wl["w_out"], nct)
        ex.submit(l, "w_out", *_mm_tn(sv["mix"].reshape(T, D), dm.reshape(T, D), "dw_out"))
        R["w_glu"][l], R["vec_p2"][l] = dwg, vec_p2
        duf, dub, db4, da_s, dc2 = ex.scan_bwd(l, sv["p"], dy, sv["hb"], pm, pmt, sv["b4m"], sv["a_s"], sv["c2m"],
                                               Lc)
        R["scan"][l] = (da_s, db4, dc2)
        dconv, R["dcw"][l] = _conv_bwd(sv["p"], dcv, sv["cw8"], nct)
        dq_r, dk_r, dv, R["dsink"][l] = ex.attn_bwd(l, P["attn_sink"][l], sv["q4"], sv["kr"], sv["p"], dattn, Lc)
        dqk = _rope_bwd(dq_r, dk_r, cos, -sin)
        dh, vec_i, dp = _pre_bwd(dqk, dv, dconv, (du_skip, duf, dub), dh1, sv["h"], modtok, ng, Wl["w_in"], nct)
        ex.submit(l, "w_in", *_mm_tn(sv["a"].reshape(T, D), dp.reshape(T, INW), "dw_in"))
        vec = jnp.concatenate([vec_i, vec_p, vec_m], axis=2)
        R["vec"][l] = vec
        mod_rows = jnp.concatenate([vec[:, :, 0:2], vec[:, :, 8:9], vec[:, :, 16:18], vec[:, :, 19:20]], axis=2)
        dmod8 = jnp.concatenate([jnp.sum(mod_rows[:, nct:], axis=1).reshape(B, NMOD * D),
                                 jnp.sum(mod_rows[:, :nct], axis=(0, 1)).reshape(1, NMOD * D),
                                 jnp.zeros((7 - B, NMOD * D), F32)], axis=0)
        R["b_ada"][l] = dmod8
        dmod8m = dmod8.astype(MXU)
        ex.submit(l, "w_ada", *_mm_tn(cact8.astype(MXU), dmod8m, "dw_ada"))
        dcact = dcact + _ada_bwd_c(dmod8m, Wl["w_ada"])

    stk = lambda k: jnp.stack(R[k])
    vec, vec_p2 = jnp.sum(stk("vec"), axis=(1, 2)), jnp.sum(stk("vec_p2"), axis=(1, 2))
    grads = dict(zip(ssm_names, prep_vjp(tuple(jnp.stack([R["scan"][l][k] for l in range(NLAYER)])
                                               for k in range(3)))))
    norm_rows = jnp.concatenate([vec[:, 2:3], vec[:, 9:10], vec[:, 18:19], vec[:, 20:21]], axis=1)
    grads.update(w_glu=stk("w_glu"), ssm_d=vec_p2[:, 0], b_glu=vec_p2[:, 1], norm_g=norm_rows,
                 conv_w=jnp.sum(stk("dcw"), axis=(1, 2))[:, 0:P["conv_w"].shape[1]],
                 attn_sink=jnp.sum(stk("dsink")[..., 0], axis=1), b_ada=jnp.sum(stk("b_ada"), axis=1),
                 c_ctx=silu_vjp(dcact[B])[0])
    return loss_local, dh[:, Lc:, :], grads


_WEIGHTS = ["c_ctx", "w_ada", "b_ada", "norm_g", "w_in", "conv_w", "attn_sink", "ssm_lam_re", "ssm_lam_im",
            "ssm_log_dt", "ssm_b_re", "ssm_b_im", "ssm_c_re", "ssm_c_im", "ssm_d", "w_glu", "b_glu", "w_out",
            "w_mlp_in", "w_mlp_out"]
_SMALL = [n for n in _WEIGHTS if n not in _BIG]
_SMALL_SHARDED = {"norm_g": 2, "conv_w": 2, "w_glu": 1}
_LANES = 1024


_GROUPS = (("w_ada", "w_in", "w_out"), ("w_mlp_in", "w_mlp_out"))
_QUEUE_OF = {"w_ada": "narrow", "w_in": "narrow", "w_out": "out", "w_mlp_in": "mlp", "w_mlp_out": "mlp"}


class _Exchange:
    def __init__(self, W, q_me):
        self.W, self.q_me = W, q_me
        self.wall, self.own, self.recv, self.pending = {}, {}, {}, {}
        self._keep(0, _BIG, _exchange_call(_Gather2([self._shard(0, n) for n in _BIG]), "gather_big"))

    def _shard(self, l, n):
        return self.W[n][l].astype(MXU)

    def _keep(self, l, names, walls):
        for n, w in zip(names, walls):
            self.wall[l, n] = w

    def weights(self, l):
        g = {n: self.wall[l, n] for n in _BIG}
        return dict(w_ada=g["w_ada"], w_mlp_in=g["w_mlp_in"], w_in=_join_shards(g["w_in"][:, None], 2)[0],
                    w_out=g["w_out"].reshape(-1, g["w_out"].shape[-1]),
                    w_mlp_out=g["w_mlp_out"].reshape(-1, g["w_mlp_out"].shape[-1]))

    def _gathering(self, l, grp, fn, *args):
        if l + 1 == NLAYER:
            return fn(*args)[0]
        res, walls = fn(*args, ex=_Gather([self._shard(l + 1, n) for n in _GROUPS[grp]]))
        self._keep(l + 1, _GROUPS[grp], walls)
        return res

    def attn_fwd(self, l, *args):
        return self._gathering(l, 0, _attn_fwd, *args)

    def mlp_fwd(self, l, *args):
        return self._gathering(l, 1, _mlp_fwd, *args)

    def submit(self, l, n, g, gm):
        r, c = g.shape
        if _BIG_AXIS[n] == 2:
            cw = c // NCHIP
            self.own[l, n] = lax.dynamic_slice_in_dim(g, self.q_me * cw, cw, axis=1)
            if cw % 128 == 0:
                item = (gm, cw)
            else:
                item = (jnp.transpose(gm.reshape(r, NCHIP, cw), (1, 0, 2)), None)
        else:
            self.own[l, n] = lax.dynamic_slice_in_dim(g, self.q_me * (r // NCHIP), r // NCHIP, axis=0)
            item = (gm.reshape(NCHIP, r // NCHIP, c), None)
        self.pending.setdefault(_QUEUE_OF[n], []).append((l, n) + item)

    def _scatter_of(self, items):
        return _Scatter([it[2] for it in items], [it[3] for it in items])

    def _scattering(self, queue, fn, *args):
        items = self.pending.pop(queue, [])
        if not items:
            return fn(*args)[0]
        res, recvs = fn(*args, ex=self._scatter_of(items))
        for it, r in zip(items, recvs):
            self.recv[it[0], it[1]] = r
        return res

    def mlp_bwd(self, l, *args):
        return self._scattering("narrow", _mlp_bwd, *args)

    def scan_bwd(self, l, *args):
        return self._scattering("mlp", _scan_bwd, *args)

    def attn_bwd(self, l, *args):
        return self._scattering("out", _attn_bwd, *args)

    def finish(self):
        items = [it for queue in sorted(self.pending) for it in self.pending[queue]]
        self.pending = {}
        for it, r in zip(items, _exchange_call(self._scatter_of(items), "scatter_big")):
            self.recv[it[0], it[1]] = r
        part = [jnp.concatenate([_sum_slots(self.own[l, n], self.recv[l, n], "sum_chips") for l in range(NLAYER)],
                                axis=0) for n in _BIG]
        sib = _exchange_call(_Swap(part), "swap_big")
        return {n: (p, s) for n, p, s in zip(_BIG, part, sib)}


def _part_rows(shape):
    return -(-math.prod(shape) // (16 * _LANES)) * 16


def _flat_pad(parts):
    rows = []
    for p in parts:
        v = p.reshape(-1)
        pad = _part_rows(p.shape) * _LANES - v.shape[0]
        rows.append(jnp.concatenate([v, jnp.zeros((pad,), v.dtype)]).reshape(-1, _LANES))
    used = sum(r.shape[0] for r in rows)
    rows.append(jnp.zeros((-used % _PAD_ROWS, _LANES), rows[0].dtype))
    return jnp.concatenate(rows, axis=0)


def _unflat(flat, shapes):
    out, r0 = [], 0
    for s in shapes:
        nr = _part_rows(s)
        out.append(flat[r0:r0 + nr].reshape(-1)[:math.prod(s)].reshape(s))
        r0 += nr
    return out


def kernel(x, c, ctx, c_ctx, w_ada, b_ada, norm_g, w_in, conv_w, attn_sink, ssm_lam_re, ssm_lam_im, ssm_log_dt, ssm_b_re, ssm_b_im, ssm_c_re, ssm_c_im, ssm_d, w_glu, b_glu, w_out, w_mlp_in, w_mlp_out, loss_target, m_c_ctx, m_w_ada, m_b_ada, m_norm_g, m_w_in, m_conv_w, m_attn_sink, m_ssm_lam_re, m_ssm_lam_im, m_ssm_log_dt, m_ssm_b_re, m_ssm_b_im, m_ssm_c_re, m_ssm_c_im, m_ssm_d, m_w_glu, m_b_glu, m_w_out, m_w_mlp_in, m_w_mlp_out, v_c_ctx, v_w_ada, v_b_ada, v_norm_g, v_w_in, v_conv_w, v_attn_sink, v_ssm_lam_re, v_ssm_lam_im, v_ssm_log_dt, v_ssm_b_re, v_ssm_b_im, v_ssm_c_re, v_ssm_c_im, v_ssm_d, v_w_glu, v_b_glu, v_w_out, v_w_mlp_in, v_w_mlp_out):
    W = dict(c_ctx=c_ctx, w_ada=w_ada, b_ada=b_ada, norm_g=norm_g, w_in=w_in, conv_w=conv_w, attn_sink=attn_sink,
             ssm_lam_re=ssm_lam_re, ssm_lam_im=ssm_lam_im, ssm_log_dt=ssm_log_dt, ssm_b_re=ssm_b_re,
             ssm_b_im=ssm_b_im, ssm_c_re=ssm_c_re, ssm_c_im=ssm_c_im, ssm_d=ssm_d, w_glu=w_glu, b_glu=b_glu,
             w_out=w_out, w_mlp_in=w_mlp_in, w_mlp_out=w_mlp_out)
    M = dict(c_ctx=m_c_ctx, w_ada=m_w_ada, b_ada=m_b_ada, norm_g=m_norm_g, w_in=m_w_in, conv_w=m_conv_w,
             attn_sink=m_attn_sink, ssm_lam_re=m_ssm_lam_re, ssm_lam_im=m_ssm_lam_im, ssm_log_dt=m_ssm_log_dt,
             ssm_b_re=m_ssm_b_re, ssm_b_im=m_ssm_b_im, ssm_c_re=m_ssm_c_re, ssm_c_im=m_ssm_c_im, ssm_d=m_ssm_d,
             w_glu=m_w_glu, b_glu=m_b_glu, w_out=m_w_out, w_mlp_in=m_w_mlp_in, w_mlp_out=m_w_mlp_out)
    V = dict(c_ctx=v_c_ctx, w_ada=v_w_ada, b_ada=v_b_ada, norm_g=v_norm_g, w_in=v_w_in, conv_w=v_conv_w,
             attn_sink=v_attn_sink, ssm_lam_re=v_ssm_lam_re, ssm_lam_im=v_ssm_lam_im, ssm_log_dt=v_ssm_log_dt,
             ssm_b_re=v_ssm_b_re, ssm_b_im=v_ssm_b_im, ssm_c_re=v_ssm_c_re, ssm_c_im=v_ssm_c_im, ssm_d=v_ssm_d,
             w_glu=v_w_glu, b_glu=v_b_glu, w_out=v_w_out, w_mlp_in=v_w_mlp_in, w_mlp_out=v_w_mlp_out)
    q_me = 2 * lax.axis_index("x") + lax.axis_index("y")

    ex = _Exchange(W, q_me)
    P = {n: W[n] for n in _SMALL}
    ssh_names = list(_SMALL_SHARDED)
    ssh = _flat_pad([W[n] for n in ssh_names])
    (sall,) = _exchange_call(_Gather([ssh]), "gather_small")
    parts = [_unflat(sall[q], [W[n].shape for n in ssh_names]) for q in range(NCHIP)]
    for k, n in enumerate(ssh_names):
        P[n] = _join_shards(jnp.stack([parts[q][k] for q in range(NCHIP)]), _SMALL_SHARDED[n])

    loss_local, grad_x, G = _local_step(x, c, ctx, loss_target, P, ex)
    loss = lax.psum(loss_local, ("x", "y", "c"))

    sums = ex.finish()
    out = {}
    for n in _BIG:
        flat = lambda t: t.reshape(-1, t.shape[-1])
        res = _adamw(flat(W[n]), *sums[n], flat(M[n]), flat(V[n]), "adamw_" + n)
        out[n] = [t.reshape(W[n].shape) for t in res]

    gsmall = _flat_pad([G[n] for n in _SMALL]).astype(MXU)
    gsum = _sum_slots(None, _gather_all(gsmall, "gather_grads"), "sum_devices")
    gfull = dict(zip(_SMALL, _unflat(gsum, [G[n].shape for n in _SMALL])))
    for n, ax in _SMALL_SHARDED.items():
        width = W[n].shape[ax]
        gfull[n] = lax.dynamic_slice_in_dim(gfull[n], q_me * width, width, axis=ax)
    shapes = [W[n].shape for n in _SMALL]
    gflat = _flat_pad([gfull[n] for n in _SMALL])
    res = _adamw(_flat_pad([W[n] for n in _SMALL]), gflat, jnp.zeros_like(gflat),
                 _flat_pad([M[n] for n in _SMALL]), _flat_pad([V[n] for n in _SMALL]), "adamw_small")
    for k, t in enumerate(res):
        for n, piece in zip(_SMALL, _unflat(t, shapes)):
            out.setdefault(n, [None] * 4)[k] = piece

    return (loss, grad_x, *[out[n][0] for n in _WEIGHTS], *[out[n][1] for n in _WEIGHTS],
            *[out[n][2] for n in _WEIGHTS], *[out[n][3] for n in _WEIGHTS])
```

```python
import functools
import math

import jax
import jax.numpy as jnp
import numpy as np
from jax import lax
from jax.experimental import pallas as pl
from jax.experimental.pallas import tpu as pltpu

F32 = jnp.float32
MXU = jnp.bfloat16

D = 1024
DFF = 4096
NMOD = 6
EPS = 1e-6
HD = 64
NQ = 8
NKV = 2
QPK = 4
AW = 512
KVW = 128
WIN = 128
QB = 128
CW = 256
SW = 256
SGRP = 16
SCH = 16
SST = 64
INW = 1792
GRID_W = 64
ROPE_BASE = 10000.0
NEG = -1e30
SCALE = HD ** -0.5
NLAYER = 4

TB = 256
TF = 1024
TC = 32
NCHIP = 4

LR, B1, B2, AEPS, WD, STEP = 0.001, 0.9, 0.999, 1e-08, 0.01, 10

MESH = pl.DeviceIdType.MESH
SDS = jax.ShapeDtypeStruct
BS = pl.BlockSpec
ANY = pl.BlockSpec(memory_space=pl.ANY)


def _cp(sem, vmem_mb=None):
    kw = dict(dimension_semantics=sem)
    if vmem_mb is not None:
        kw["vmem_limit_bytes"] = vmem_mb * 1024 * 1024
    return pltpu.CompilerParams(**kw)


def _dot(a, b):
    return jnp.dot(a, b, preferred_element_type=F32)


def _dot_nt(a, b):
    return lax.dot_general(a, b, (((1,), (1,)), ((), ())), preferred_element_type=F32)


def _dot_tn(a, b):
    return lax.dot_general(a, b, (((0,), (0,)), ((), ())), preferred_element_type=F32)


def _rms(x):
    r = lax.rsqrt(jnp.mean(x * x, axis=-1, keepdims=True) + EPS)
    return x * r, r


def _rms_bwd(dn, n, r):
    return r * (dn - n * jnp.mean(dn * n, axis=-1, keepdims=True))


_GC = math.sqrt(2.0 / math.pi)


def _gelu(y):
    return 0.5 * y * (1.0 + jnp.tanh(_GC * (y + 0.044715 * y * y * y)))


def _gelu_grad(y):
    th = jnp.tanh(_GC * (y + 0.044715 * y * y * y))
    return 0.5 * (1.0 + th) + 0.5 * y * (1.0 - th * th) * _GC * (1.0 + 3 * 0.044715 * y * y)


def _sigmoid(v):
    return 1.0 / (1.0 + jnp.exp(-v))


def _colsum(x):
    return jnp.sum(x, axis=0, keepdims=True)


def _tok(width, col=0):
    return BS((None, TB, width), lambda b, j: (b, j, col))


def _mod_spec(nct):
    return BS((None, None, 8, D), lambda b, j: (b, jnp.where(j >= nct, 1, 0), 0, 0))


def _full(shape):
    nd = len(shape)
    return BS(shape, lambda *a: (0,) * nd)


def _resident(shape):
    nd = len(shape)
    return BS(shape, lambda *a: (0,) * nd, pipeline_mode=pl.Buffered(1))


def _ada_fwd(cact8, w4, b):
    wq = w4.shape[2]

    def body(c_ref, w_ref, b_ref, o_ref):
        o_ref[...] = _dot(c_ref[...].astype(MXU), w_ref[...]) + b_ref[...]

    return pl.pallas_call(
        body, name="ada_fwd", grid=(NCHIP,),
        in_specs=[BS((8, D), lambda j: (0, 0)), BS((None, D, wq), lambda j: (j, 0, 0)), BS((1, wq), lambda j: (0, j))],
        out_specs=BS((8, wq), lambda j: (0, j)),
        out_shape=SDS((8, NMOD * D), F32), compiler_params=_cp(("arbitrary",)),
    )(cact8, w4, b)


def _ada_bwd_c(dmod8, w4):
    wq = w4.shape[2]

    def body(d_ref, w_ref, o_ref):
        @pl.when(pl.program_id(0) == 0)
        def _():
            o_ref[...] = jnp.zeros_like(o_ref)
        o_ref[...] += _dot_nt(d_ref[...], w_ref[...])

    return pl.pallas_call(
        body, name="ada_bwd_c", grid=(NCHIP,),
        in_specs=[BS((8, wq), lambda j: (0, j)), BS((None, D, wq), lambda j: (j, 0, 0))],
        out_specs=BS((8, D), lambda j: (0, 0)),
        out_shape=SDS((8, D), F32), compiler_params=_cp(("arbitrary",)),
    )(dmod8, w4)


def _pre_mix(h, modtok, ng, w_in, nct):
    B, S, _ = h.shape

    def body(h_ref, mod_ref, g_ref, w_ref, p_ref, a_ref):
        n, _ = _rms(h_ref[...])
        a = ((n * g_ref[0:1, :]) * (1.0 + mod_ref[1:2, :]) + mod_ref[0:1, :]).astype(MXU)
        a_ref[...] = a
        p_ref[...] = _dot(a, w_ref[...])

    return pl.pallas_call(
        body, name="pre_mix", grid=(B, S // TB),
        in_specs=[_tok(D), _mod_spec(nct), _full((8, D)), _full((D, INW))],
        out_specs=[_tok(INW), _tok(D)],
        out_shape=[SDS((B, S, INW), F32), SDS((B, S, D), MXU)],
        compiler_params=_cp(("arbitrary", "arbitrary"), 40),
    )(h, modtok, ng, w_in)


def _pre_bwd(dqk, dv, dcv, du_parts, dh_part, h, modtok, ng, w_in, nct):
    B, S, _ = h.shape

    def body(dqk_ref, dv_ref, dcv_ref, du0_ref, du1_ref, du2_ref, dhp_ref, h_ref, mod_ref, g_ref, w_ref,
             dh_ref, vec_ref, dp_ref):
        du = du0_ref[...] + du1_ref[...] + du2_ref[...]
        dp = jnp.concatenate([dqk_ref[...], dv_ref[...].astype(MXU), dcv_ref[...].astype(MXU), du.astype(MXU)],
                             axis=1)
        dp_ref[...] = dp
        da = _dot_nt(dp, w_ref[...])
        n, r = _rms(h_ref[...])
        g = g_ref[0:1, :]
        sc1 = 1.0 + mod_ref[1:2, :]
        vec_ref[...] = jnp.zeros_like(vec_ref)
        vec_ref[0:1, :] = _colsum(da)
        vec_ref[1:2, :] = _colsum(da * (n * g))
        vec_ref[2:3, :] = _colsum(da * sc1 * n)
        dh_ref[...] = dhp_ref[...] + _rms_bwd(da * sc1 * g, n, r)

    return pl.pallas_call(
        body, name="pre_bwd", grid=(B, S // TB),
        in_specs=[_tok(AW + KVW), _tok(KVW), _tok(3 * CW), _tok(SW), _tok(SW), _tok(SW), _tok(D), _tok(D),
                  _mod_spec(nct), _full((8, D)), _full((D, INW))],
        out_specs=[_tok(D), BS((None, None, 8, D), lambda b, j: (b, j, 0, 0)), _tok(INW)],
        out_shape=[SDS((B, S, D), F32), SDS((B, S // TB, 8, D), F32), SDS((B, S, INW), MXU)],
        compiler_params=_cp(("arbitrary", "arbitrary"), 40),
    )(dqk, dv, dcv, *du_parts, dh_part, h, modtok, ng, w_in)


def _rotate(t, c, s):
    lane = lax.broadcasted_iota(jnp.int32, t.shape, 1)
    partner = jnp.where((lane % 32) < 16, pltpu.roll(t, 112, 1), pltpu.roll(t, 16, 1))
    return t * c + partner * s


def _to_groups(cols):
    lo = lax.broadcasted_iota(jnp.int32, cols[0].shape, 1) < HD
    out = []
    for g in range(QPK):
        a, b = cols[g // 2], cols[2 + g // 2]
        out.append(jnp.where(lo, a, pltpu.roll(b, HD, 1)) if g % 2 == 0 else jnp.where(lo, pltpu.roll(a, HD, 1), b))
    return out


def _from_groups(tiles):
    lo = lax.broadcasted_iota(jnp.int32, tiles[0].shape, 1) < HD
    out = []
    for k in range(NQ // 2):
        a, b = tiles[(2 * k) % QPK], tiles[(2 * k) % QPK + 1]
        out.append(jnp.where(lo, a, pltpu.roll(b, HD, 1)) if k < 2 else jnp.where(lo, pltpu.roll(a, HD, 1), b))
    return out


def _grp(width=128):
    return BS((None, QPK, TB, width), lambda b, j: (b, 0, j, 0))


_TAB = BS((TB, 128), lambda b, j: (j, 0))


def _rope_fwd(p, cos, sin):
    B, S, _ = p.shape

    def body(x_ref, c_ref, s_ref, q_ref, k_ref):
        c, s = c_ref[...], s_ref[...]
        rot = [_rotate(x_ref[:, k * 128:(k + 1) * 128], c, s) for k in range((AW + KVW) // 128)]
        k_ref[...] = rot[AW // 128].astype(MXU)
        for g, t in enumerate(_to_groups(rot[:AW // 128])):
            q_ref[g] = (t * SCALE).astype(MXU)

    return pl.pallas_call(
        body, name="rope_fwd", grid=(B, S // TB),
        in_specs=[_tok(AW + KVW), _TAB, _TAB],
        out_specs=[_grp(), _tok(KVW)],
        out_shape=[SDS((B, QPK, S, 128), MXU), SDS((B, S, KVW), MXU)],
        compiler_params=_cp(("arbitrary", "arbitrary")),
    )(p, cos, sin)


def _rope_bwd(dq4, dk, cos, nsin):
    B, _, S, _ = dq4.shape

    def body(q_ref, k_ref, c_ref, s_ref, o_ref):
        c, s = c_ref[...], s_ref[...]
        cols = _from_groups([q_ref[g] * SCALE for g in range(QPK)]) + [k_ref[...]]
        for k, t in enumerate(cols):
            o_ref[:, k * 128:(k + 1) * 128] = _rotate(t, c, s).astype(MXU)

    return pl.pallas_call(
        body, name="rope_bwd", grid=(B, S // TB),
        in_specs=[_grp(), _tok(KVW), _TAB, _TAB],
        out_specs=_tok(AW + KVW),
        out_shape=SDS((B, S, AW + KVW), MXU),
        compiler_params=_cp(("arbitrary", "arbitrary")),
    )(dq4, dk, cos, nsin)


def _attn_masks(i, S, Lc):
    ncb = Lc // QB
    is_lat = i >= ncb
    w0 = pl.multiple_of(jnp.clip((i - 1) * QB, 0, S - 3 * QB), QB)
    shape = (QPK * QB, 3 * QB)
    qpos = i * QB + jnp.bitwise_and(lax.broadcasted_iota(jnp.int32, shape, 0), QB - 1)
    kpos = w0 + lax.broadcasted_iota(jnp.int32, shape, 1)
    mask = jnp.logical_and(jnp.logical_and(kpos >= Lc, jnp.abs(qpos - kpos) <= WIN), is_lat)
    return w0, mask


def _attn_probs(qh, kch, kwh, mask, sk):
    sc = _dot_nt(qh, kch)
    sw = jnp.where(mask, _dot_nt(qh, kwh), NEG)
    m = jnp.maximum(jnp.maximum(jnp.max(sc, axis=-1, keepdims=True), jnp.max(sw, axis=-1, keepdims=True)), sk)
    ec = jnp.exp(sc - m)
    ew = jnp.exp(sw - m)
    es = jnp.exp(sk - m)
    inv = 1.0 / (jnp.sum(ec, axis=-1, keepdims=True) + jnp.sum(ew, axis=-1, keepdims=True) + es)
    return ec * inv, ew * inv, es * inv


def _sink_col(sink_ref, hk):
    return jnp.concatenate([jnp.full((QB, 1), sink_ref[hk * QPK + g], F32) for g in range(QPK)], axis=0)


_QGRP = BS((None, QPK, QB, 128), lambda b, i: (b, 0, i, 0))


def _first_last(grid):
    first = last = None
    for ax, n in enumerate(grid):
        i = pl.program_id(ax)
        first = (i == 0) if first is None else jnp.logical_and(first, i == 0)
        last = (i == n - 1) if last is None else jnp.logical_and(last, i == n - 1)
    return first, last


def _call_with(ex, body, name, grid, in_specs, out_specs, out_shape, args, scratch, params):
    n_in, n_out = len(in_specs), len(out_specs)
    if ex is None:
        res = pl.pallas_call(body, name=name, grid=grid, in_specs=in_specs, out_specs=out_specs,
                             out_shape=out_shape, scratch_shapes=scratch, compiler_params=params)(*args)
        return list(res), []

    def carrying(*refs):
        own, comm = _carried(ex, refs, n_in, n_out)
        first, last = _first_last(grid)
        pl.when(first)(lambda: ex.start(*comm))
        body(*own)
        pl.when(last)(lambda: ex.wait(*comm))

    res = pl.pallas_call(
        carrying, name=f"{name}_{type(ex).__name__.strip('_').lower()}", grid=grid,
        in_specs=in_specs + ex.specs, out_specs=out_specs + ex.specs, out_shape=out_shape + ex.out_shape,
        scratch_shapes=scratch + ex.sems, compiler_params=params)(*args, *ex.args)
    return list(res[:n_out]), list(res[n_out:])


def _attn_fwd(sink, q4, k, p, Lc, ex=None):
    B, S, _ = k.shape

    def body(sink_ref, q_ref, k_ref, v_ref, o_ref):
        w0, mask = _attn_masks(pl.program_id(1), S, Lc)
        kc = k_ref[0:Lc, :]
        vc = v_ref[0:Lc, :].astype(MXU)
        kw = k_ref[pl.ds(w0, 3 * QB), :]
        vw = v_ref[pl.ds(w0, 3 * QB), :].astype(MXU)
        q = q_ref[...].reshape(QPK * QB, 128)
        half = QPK * QB // 2
        outs = []
        for hk in range(NKV):
            cs = slice(hk * HD, (hk + 1) * HD)
            sk = _sink_col(sink_ref, hk)
            for r0 in (0, half):
                rows = slice(r0, r0 + half)
                pc, pw, _ = _attn_probs(q[rows, cs], kc[:, cs], kw[:, cs], mask[:half], sk[rows])
                o = _dot(pc.astype(MXU), vc[:, cs]) + _dot(pw.astype(MXU), vw[:, cs])
                outs += [o[g * QB:(g + 1) * QB] for g in range(half // QB)]
        o_ref[...] = jnp.concatenate(outs, axis=1).astype(o_ref.dtype)

    return _call_with(
        ex, body, "attn_fwd", (B, S // QB),
        [BS(memory_space=pltpu.SMEM), _QGRP, BS((None, S, KVW), lambda b, i: (b, 0, 0)),
         BS((None, S, KVW), lambda b, i: (b, 0, (AW + KVW) // KVW))],
        [BS((None, QB, AW), lambda b, i: (b, i, 0))], [SDS((B, S, AW), MXU)], [sink, q4, k, p], [],
        _cp(("arbitrary", "arbitrary"), 40))


def _attn_bwd(sink, q4, k, p, do4, Lc, ex=None):
    B, S, _ = k.shape

    def body(sink_ref, q_ref, k_ref, v_ref, do_ref, dq_ref, dk_ref, dv_ref, ds_ref):
        i = pl.program_id(1)

        @pl.when(i == 0)
        def _():
            dk_ref[...] = jnp.zeros_like(dk_ref)
            dv_ref[...] = jnp.zeros_like(dv_ref)
            ds_ref[...] = jnp.zeros_like(ds_ref)

        w0, mask = _attn_masks(i, S, Lc)
        kc = k_ref[0:Lc, :]
        vc = v_ref[0:Lc, :].astype(MXU)
        kw = k_ref[pl.ds(w0, 3 * QB), :]
        vw = v_ref[pl.ds(w0, 3 * QB), :].astype(MXU)
        q = q_ref[...].reshape(QPK * QB, 128)
        do = do_ref[...].reshape(QPK * QB, 128)
        dqs, dsinks, dkcs, dkws, dvcs, dvws = [], [], [], [], [], []
        for hk in range(NKV):
            cs = slice(hk * HD, (hk + 1) * HD)
            qh = q[:, cs]
            pc, pw, ps = _attn_probs(qh, kc[:, cs], kw[:, cs], mask, _sink_col(sink_ref, hk))
            dob = do[:, cs].astype(MXU)
            dpc = _dot_nt(dob, vc[:, cs])
            dpw = _dot_nt(dob, vw[:, cs])
            delta = jnp.sum(pc * dpc, axis=-1, keepdims=True) + jnp.sum(pw * dpw, axis=-1, keepdims=True)
            dsc = (pc * (dpc - delta)).astype(MXU)
            dsw = (pw * (dpw - delta)).astype(MXU)
            dqs.append(_dot(dsc, kc[:, cs]) + _dot(dsw, kw[:, cs]))
            dkcs.append(_dot_tn(dsc, qh))
            dkws.append(_dot_tn(dsw, qh))
            dvcs.append(_dot_tn(pc.astype(MXU), dob))
            dvws.append(_dot_tn(pw.astype(MXU), dob))
            psd = ps * delta
            dsinks += [jnp.broadcast_to(-jnp.sum(psd[g * QB:(g + 1) * QB], axis=0, keepdims=True), (1, 128))
                       for g in range(QPK)]
        for g in range(QPK):
            dq_ref[g] = jnp.concatenate([dq[g * QB:(g + 1) * QB] for dq in dqs], axis=1)
        ds_ref[...] += jnp.concatenate(dsinks, axis=0)
        dk_ref[0:Lc, :] += jnp.concatenate(dkcs, axis=1)
        dv_ref[0:Lc, :] += jnp.concatenate(dvcs, axis=1)
        dk_ref[pl.ds(w0, 3 * QB), :] += jnp.concatenate(dkws, axis=1)
        dv_ref[pl.ds(w0, 3 * QB), :] += jnp.concatenate(dvws, axis=1)

    kv_acc = BS((None, S, KVW), lambda b, i: (b, 0, 0))
    return _call_with(
        ex, body, "attn_bwd", (B, S // QB),
        [BS(memory_space=pltpu.SMEM), _QGRP, BS((None, S, KVW), lambda b, i: (b, 0, 0)),
         BS((None, S, KVW), lambda b, i: (b, 0, (AW + KVW) // KVW)), _QGRP],
        [_QGRP, kv_acc, kv_acc, BS((None, 8, 128), lambda b, i: (b, 0, 0))],
        [SDS((B, QPK, S, 128), F32), SDS((B, S, KVW), F32), SDS((B, S, KVW), F32), SDS((B, 8, 128), F32)],
        [sink, q4, k, p, do4], [], _cp(("arbitrary", "arbitrary"), 48))


def _halo_specs(width, col, S):
    nb8 = S // 8
    per = TB // 8
    prev = BS((None, 8, width), lambda b, j: (b, jnp.maximum(j * per - 1, 0), col))
    nxt = BS((None, 8, width), lambda b, j: (b, jnp.minimum((j + 1) * per, nb8 - 1), col))
    return prev, nxt


def _shift_dn(z, prev_row, at_start):
    row = lax.broadcasted_iota(jnp.int32, z.shape, 0)
    first = jnp.where(at_start, 0.0, 1.0) * prev_row
    return jnp.where(row == 0, first, pltpu.roll(z, 1, 0))


def _shift_up(z, next_row, at_end):
    row = lax.broadcasted_iota(jnp.int32, z.shape, 0)
    last = jnp.where(at_end, 0.0, 1.0) * next_row
    return jnp.where(row == z.shape[0] - 1, last, pltpu.roll(z, z.shape[0] - 1, 0))


def _conv_fwd(p, cw8, nct):
    B, S, _ = p.shape
    nt = S // TB

    def body(p_ref, pp_ref, pn_ref, w_ref, o_ref):
        j = pl.program_id(1)
        at_start = jnp.logical_or(j == 0, j == nct)
        at_end = jnp.logical_or(j == nct - 1, j == nt - 1)
        z = p_ref[:, 256:512] * p_ref[:, 512:768]
        zprev = pp_ref[7:8, 256:512] * pp_ref[7:8, 512:768]
        znext = pn_ref[0:1, 256:512] * pn_ref[0:1, 512:768]
        c3 = (_shift_dn(z, zprev, at_start) * w_ref[0:1, :] + z * w_ref[1:2, :]
              + _shift_up(z, znext, at_end) * w_ref[2:3, :])
        o_ref[...] = p_ref[:, 0:256] * c3

    prev, nxt = _halo_specs(3 * CW, 1, S)
    return pl.pallas_call(
        body, name="conv_fwd", grid=(B, nt),
        in_specs=[_tok(3 * CW, 1), prev, nxt, _full((8, CW))],
        out_specs=_tok(CW),
        out_shape=SDS((B, S, CW), F32),
        compiler_params=_cp(("arbitrary", "arbitrary")),
    )(p, p, p, cw8)


def _conv_bwd(p, dcv, cw8, nct):
    B, S, _ = p.shape
    nt = S // TB

    def body(p_ref, pp_ref, pn_ref, d_ref, dp_ref, dn_ref, w_ref, o_ref, dw_ref):
        j = pl.program_id(1)
        at_start = jnp.logical_or(j == 0, j == nct)
        at_end = jnp.logical_or(j == nct - 1, j == nt - 1)
        cb, cc, cx = p_ref[:, 0:256], p_ref[:, 256:512], p_ref[:, 512:768]
        z = cc * cx
        zm = _shift_dn(z, pp_ref[7:8, 256:512] * pp_ref[7:8, 512:768], at_start)
        zp = _shift_up(z, pn_ref[0:1, 256:512] * pn_ref[0:1, 512:768], at_end)
        d = d_ref[...]
        e = d * cb
        em = _shift_dn(e, dp_ref[7:8, :] * pp_ref[7:8, 0:256], at_start)
        ep = _shift_up(e, dn_ref[0:1, :] * pn_ref[0:1, 0:256], at_end)
        dz = e * w_ref[1:2, :] + ep * w_ref[0:1, :] + em * w_ref[2:3, :]
        o_ref[:, 0:256] = d * (zm * w_ref[0:1, :] + z * w_ref[1:2, :] + zp * w_ref[2:3, :])
        o_ref[:, 256:512] = dz * cx
        o_ref[:, 512:768] = dz * cc
        dw_ref[...] = jnp.zeros_like(dw_ref)
        dw_ref[0:1, :] = _colsum(e * zm)
        dw_ref[1:2, :] = _colsum(e * z)
        dw_ref[2:3, :] = _colsum(e * zp)

    prev, nxt = _halo_specs(3 * CW, 1, S)
    dprev, dnxt = _halo_specs(CW, 0, S)
    return pl.pallas_call(
        body, name="conv_bwd", grid=(B, nt),
        in_specs=[_tok(3 * CW, 1), prev, nxt, _tok(CW), dprev, dnxt, _full((8, CW))],
        out_specs=[_tok(3 * CW), BS((None, None, 8, CW), lambda b, j: (b, j, 0, 0))],
        out_shape=[SDS((B, S, 3 * CW), F32), SDS((B, nt, 8, CW), F32)],
        compiler_params=_cp(("arbitrary", "arbitrary")),
    )(p, p, p, dcv, dcv, dcv, cw8)


HW = SGRP * SST


def _rev_chunk(k, nk, ncc):
    return jnp.where(k < ncc, ncc - 1 - k, nk - 1 - k + ncc)


def _scan_perm(B):
    n = B * TC
    pm = np.zeros((2 * n, 2 * n), np.float32)
    for t in range(TC):
        for e in range(B):
            pm[t * 2 * B + e, e * TC + t] = 1.0
            pm[t * 2 * B + B + e, n + e * TC + (TC - 1 - t)] = 1.0
    return pm


def _scan_drive(uf_ref, ub_ref, pm_ref, b4_ref, fwd, dbuf, n):
    u_tok = jnp.concatenate([uf_ref[...].reshape(n, SW), ub_ref[...].reshape(n, SW)], axis=0).astype(MXU)
    u_tm = _dot(pm_ref[...], u_tok).astype(MXU)
    d4 = _dot(u_tm, b4_ref[...])
    dbuf[...] = jnp.where(fwd, d4[:, :2 * HW], d4[:, 2 * HW:])
    return u_tm


def _scan_fwd(p, pm, pmt, b4, a, c2, Lc):
    B, S, _ = p.shape
    nch, n = 2 * B, B * TC
    nk, ncc = S // TC, Lc // TC
    ucol = (INW - SW) // SW

    def body(uf_ref, ub_ref, pm_ref, pmt_ref, b4_ref, a_ref, c2_ref, yf_ref, yb_ref, hb_ref, hcar, dbuf, hbuf):
        @pl.when(pl.program_id(0) == 0)
        def _():
            hcar[...] = jnp.zeros_like(hcar)

        fwd = (lax.broadcasted_iota(jnp.int32, (2 * n, 1), 0) % nch) < B
        _scan_drive(uf_ref, ub_ref, pm_ref, b4_ref, fwd, dbuf, n)
        hb_ref[...] = hcar[...]
        ar, ai = a_ref[:, :HW], a_ref[:, HW:]

        def step(t, h):
            hr, hi = h
            r0 = pl.multiple_of(t * nch, nch)
            nr = ar * hr - ai * hi + dbuf[pl.ds(r0, nch), :HW]
            ni = ar * hi + ai * hr + dbuf[pl.ds(r0, nch), HW:]
            hbuf[pl.ds(r0, nch), :HW] = nr
            hbuf[pl.ds(r0, nch), HW:] = ni
            return nr, ni

        hr, hi = lax.fori_loop(0, TC, step, (hcar[:, :HW], hcar[:, HW:]), unroll=4)
        hcar[:, :HW] = hr
        hcar[:, HW:] = hi
        h_tok = _dot(pmt_ref[...], hbuf[...].astype(MXU)).astype(MXU)
        yf_ref[...] = _dot(h_tok[:n], c2_ref[:, :SW]).reshape(B, TC, SW)
        yb_ref[...] = _dot(h_tok[n:], c2_ref[:, SW:]).reshape(B, TC, SW)

    return pl.pallas_call(
        body, name="scan_fwd", grid=(nk,),
        in_specs=[BS((B, TC, SW), lambda k: (0, k, ucol)),
                  BS((B, TC, SW), lambda k: (0, _rev_chunk(k, nk, ncc), ucol)),
                  _full((2 * n, 2 * n)), _full((2 * n, 2 * n)),
                  _full((SW, 4 * HW)), _full((nch, 2 * HW)), _full((2 * HW, 2 * SW))],
        out_specs=[BS((B, TC, SW), lambda k: (0, k, 0)),
                   BS((B, TC, SW), lambda k: (0, _rev_chunk(k, nk, ncc), 0)),
                   BS((None, nch, 2 * HW), lambda k: (k, 0, 0))],
        out_shape=[SDS((B, S, SW), F32), SDS((B, S, SW), F32), SDS((nk, nch, 2 * HW), F32)],
        scratch_shapes=[pltpu.VMEM((nch, 2 * HW), F32), pltpu.VMEM((2 * n, 2 * HW), F32),
                        pltpu.VMEM((2 * n, 2 * HW), F32)],
        compiler_params=_cp(("arbitrary",), 48),
    )(p, p, pm, pmt, b4, a, c2)


def _scan_bwd(p, dy, hb, pm, pmt, b4, a, c2, Lc, ex=None):
    B, S, _ = p.shape
    nch, n = 2 * B, B * TC
    nk, ncc = S // TC, Lc // TC
    ucol = (INW - SW) // SW
    rows = 2 * n

    def body(uf_ref, ub_ref, dyf_ref, dyb_ref, hb_ref, pm_ref, pmt_ref, b4_ref, a_ref, c2_ref,
             duf_ref, dub_ref, db4_ref, da_ref, dc2_ref, gcar, dbuf, hbuf, gbuf):
        @pl.when(pl.program_id(0) == 0)
        def _():
            gcar[...] = jnp.zeros_like(gcar)
            db4_ref[...] = jnp.zeros_like(db4_ref)
            da_ref[...] = jnp.zeros_like(da_ref)
            dc2_ref[...] = jnp.zeros_like(dc2_ref)

        fwd = (lax.broadcasted_iota(jnp.int32, (rows, 1), 0) % nch) < B
        u_tm = _scan_drive(uf_ref, ub_ref, pm_ref, b4_ref, fwd, dbuf, n)
        ar, ai = a_ref[:, :HW], a_ref[:, HW:]
        hbuf[0:nch, :] = hb_ref[...]

        def fstep(t, h):
            hr, hi = h
            r0 = pl.multiple_of(t * nch, nch)
            nr = ar * hr - ai * hi + dbuf[pl.ds(r0, nch), :HW]
            ni = ar * hi + ai * hr + dbuf[pl.ds(r0, nch), HW:]
            hbuf[pl.ds(r0 + nch, nch), :HW] = nr
            hbuf[pl.ds(r0 + nch, nch), HW:] = ni
            return nr, ni

        lax.fori_loop(0, TC, fstep, (hb_ref[:, :HW], hb_ref[:, HW:]), unroll=4)

        dy_tok = jnp.concatenate([dyf_ref[...].reshape(n, SW), dyb_ref[...].reshape(n, SW)], axis=0).astype(MXU)
        dy_tm = _dot(pm_ref[...], dy_tok)
        dy2 = jnp.concatenate([jnp.where(fwd, dy_tm, 0.0), jnp.where(fwd, 0.0, dy_tm)], axis=1).astype(MXU)
        gbuf[...] = _dot_nt(dy2, c2_ref[...])
        dc2_ref[...] += _dot_tn(hbuf[nch:, :].astype(MXU), dy2)

        def bstep(i, cary):
            cr, ci = cary
            r0 = pl.multiple_of((TC - 1 - i) * nch, nch)
            gr = gbuf[pl.ds(r0, nch), :HW] + cr
            gi = gbuf[pl.ds(r0, nch), HW:] + ci
            gbuf[pl.ds(r0, nch), :HW] = gr
            gbuf[pl.ds(r0, nch), HW:] = gi
            return ar * gr + ai * gi, ar * gi - ai * gr

        cr, ci = lax.fori_loop(0, TC, bstep, (gcar[:, :HW], gcar[:, HW:]), unroll=4)
        gcar[:, :HW] = cr
        gcar[:, HW:] = ci

        gr = gbuf[:, :HW].reshape(TC, nch, HW)
        gi = gbuf[:, HW:].reshape(TC, nch, HW)
        hpr = hbuf[0:rows, :HW].reshape(TC, nch, HW)
        hpi = hbuf[0:rows, HW:].reshape(TC, nch, HW)
        da_ref[:, :HW] += jnp.sum(gr * hpr + gi * hpi, axis=0)
        da_ref[:, HW:] += jnp.sum(gi * hpr - gr * hpi, axis=0)

        g = gbuf[...]
        gm = g.astype(MXU)
        dd4 = jnp.concatenate([jnp.where(fwd, g, 0.0), jnp.where(fwd, 0.0, g)], axis=1).astype(MXU)
        db4_ref[...] += _dot_tn(u_tm, dd4)
        g_tok = _dot(pmt_ref[...], gm).astype(MXU)
        duf_ref[...] = _dot_nt(g_tok[:n], b4_ref[:, :2 * HW]).reshape(B, TC, SW)
        dub_ref[...] = _dot_nt(g_tok[n:], b4_ref[:, 2 * HW:]).reshape(B, TC, SW)

    fwd_blk = lambda col: BS((B, TC, SW), lambda k: (0, nk - 1 - k, col))
    bwd_blk = lambda col: BS((B, TC, SW), lambda k: (0, _rev_chunk(nk - 1 - k, nk, ncc), col))
    return _call_with(
        ex, body, "scan_bwd", (nk,),
        [fwd_blk(ucol), bwd_blk(ucol), fwd_blk(0), bwd_blk(0),
         BS((None, nch, 2 * HW), lambda k: (nk - 1 - k, 0, 0)),
         _full((rows, rows)), _full((rows, rows)),
         _full((SW, 4 * HW)), _full((nch, 2 * HW)), _full((2 * HW, 2 * SW))],
        [fwd_blk(0), bwd_blk(0), _full((SW, 4 * HW)), _full((nch, 2 * HW)), _full((2 * HW, 2 * SW))],
        [SDS((B, S, SW), F32), SDS((B, S, SW), F32), SDS((SW, 4 * HW), F32), SDS((nch, 2 * HW), F32),
         SDS((2 * HW, 2 * SW), F32)],
        [p, p, dy, dy, hb, pm, pmt, b4, a, c2],
        [pltpu.VMEM((nch, 2 * HW), F32), pltpu.VMEM((rows, 2 * HW), F32),
         pltpu.VMEM((rows + nch, 2 * HW), F32), pltpu.VMEM((rows, 2 * HW), F32)],
        _cp(("arbitrary",), 56))


def _post_mix(o, cv, yf, yb, p, h, modtok, ng, dsk, wg, bg, wo, nct):
    B, S, _ = h.shape

    def body(o_ref, cv_ref, yf_ref, yb_ref, u_ref, h_ref, mod_ref, g_ref, dsk_ref, wg_ref, bg_ref, wo_ref,
             h1_ref, mix_ref, m_ref, y_ref):
        y = yf_ref[...] + yb_ref[...] + dsk_ref[...] * u_ref[...]
        g = _gelu(y)
        s = g * _sigmoid(_dot(g.astype(MXU), wg_ref[...]) + bg_ref[...])
        mix_ref[:, 0:AW] = o_ref[...]
        mix_ref[:, AW:AW + CW] = cv_ref[...].astype(MXU)
        mix_ref[:, AW + CW:] = s.astype(MXU)
        m = _dot(mix_ref[...], wo_ref[...])
        n, _ = _rms(m)
        h1_ref[...] = h_ref[...] + mod_ref[2:3, :] * (n * g_ref[1:2, :])
        m_ref[...] = m
        y_ref[...] = y

    return pl.pallas_call(
        body, name="post_mix", grid=(B, S // TB),
        in_specs=[_tok(AW), _tok(CW), _tok(SW), _tok(SW), _tok(SW, (INW - SW) // SW), _tok(D), _mod_spec(nct),
                  _full((8, D)), _full((1, SW)), _full((SW, SW)), _full((1, SW)), _full((D, D))],
        out_specs=[_tok(D), _tok(D), _tok(D), _tok(SW)],
        out_shape=[SDS((B, S, D), F32), SDS((B, S, D), MXU), SDS((B, S, D), F32), SDS((B, S, SW), F32)],
        compiler_params=_cp(("arbitrary", "arbitrary"), 40),
    )(o, cv, yf, yb, p, h, modtok, ng, dsk, wg, bg, wo)


def _post_bwd(dh1, m, y, p, modtok, ng, dsk, wg, bg, wo, nct):
    B, S, _ = m.shape

    def body(dh_ref, m_ref, y_ref, u_ref, mod_ref, g_ref, dsk_ref, wg_ref, bg_ref, wo_ref,
             dm_ref, da_ref, dc_ref, dy_ref, du_ref, vec_ref, vec2_ref, dwg_ref):
        @pl.when(jnp.logical_and(pl.program_id(0) == 0, pl.program_id(1) == 0))
        def _():
            dwg_ref[...] = jnp.zeros_like(dwg_ref)

        n, r = _rms(m_ref[...])
        dh1 = dh_ref[...]
        gpm = g_ref[1:2, :]
        dr = dh1 * mod_ref[2:3, :]
        vec_ref[...] = jnp.zeros_like(vec_ref)
        vec_ref[0:1, :] = _colsum(dh1 * (n * gpm))
        vec_ref[1:2, :] = _colsum(dr * n)
        dm = _rms_bwd(dr * gpm, n, r).astype(MXU)
        dm_ref[...] = dm
        dmix = _dot_nt(dm, wo_ref[...])
        for g, t in enumerate(_to_groups([dmix[:, k * 128:(k + 1) * 128] for k in range(AW // 128)])):
            da_ref[g] = t
        dc_ref[...] = dmix[:, AW:AW + CW]
        ds = dmix[:, AW + CW:]
        yv = y_ref[...]
        g = _gelu(yv)
        gb = g.astype(MXU)
        sg = _sigmoid(_dot(gb, wg_ref[...]) + bg_ref[...])
        dv = ds * g * sg * (1.0 - sg)
        dvb = dv.astype(MXU)
        dg = ds * sg + _dot_nt(dvb, wg_ref[...])
        dwg_ref[...] += _dot_tn(gb, dvb)
        dy = dg * _gelu_grad(yv)
        dy_ref[...] = dy
        du_ref[...] = dy * dsk_ref[...]
        vec2_ref[...] = jnp.zeros_like(vec2_ref)
        vec2_ref[0:1, :] = _colsum(dy * u_ref[...])
        vec2_ref[1:2, :] = _colsum(dv)

    nt = S // TB
    return pl.pallas_call(
        body, name="post_bwd", grid=(B, nt),
        in_specs=[_tok(D), _tok(D), _tok(SW), _tok(SW, (INW - SW) // SW), _mod_spec(nct), _full((8, D)),
                  _full((1, SW)), _full((SW, SW)), _full((1, SW)), _full((D, D))],
        out_specs=[_tok(D), _grp(), _tok(CW), _tok(SW), _tok(SW),
                   BS((None, None, 8, D), lambda b, j: (b, j, 0, 0)),
                   BS((None, None, 8, SW), lambda b, j: (b, j, 0, 0)), _full((SW, SW))],
        out_shape=[SDS((B, S, D), MXU), SDS((B, QPK, S, 128), F32), SDS((B, S, CW), F32), SDS((B, S, SW), F32),
                   SDS((B, S, SW), F32), SDS((B, nt, 8, D), F32), SDS((B, nt, 8, SW), F32), SDS((SW, SW), F32)],
        compiler_params=_cp(("arbitrary", "arbitrary"), 40),
    )(dh1, m, y, p, modtok, ng, dsk, wg, bg, wo)


def _mlp_fwd(h1, modtok, ng, w1, w2, nct, ex=None):
    B, S, _ = h1.shape
    assert w1.shape == (DFF // TF, D, TF)

    def body(h_ref, mod_ref, g_ref, w1_ref, w2_ref, h2_ref, a2_ref, z_ref, f_ref):
        n, _ = _rms(h_ref[...])
        a2 = ((n * g_ref[2:3, :]) * (1.0 + mod_ref[4:5, :]) + mod_ref[3:4, :]).astype(MXU)
        a2_ref[...] = a2
        ff = jnp.zeros((TB, D), F32)
        for c in range(DFF // TF):
            cs = slice(c * TF, (c + 1) * TF)
            z = _dot(a2, w1_ref[c])
            z_ref[:, cs] = z.astype(MXU)
            r = jnp.maximum(z, 0.0)
            ff = ff + _dot((r * r).astype(MXU), w2_ref[cs, :])
        f_ref[...] = ff
        n, _ = _rms(ff)
        h2_ref[...] = h_ref[...] + mod_ref[5:6, :] * (n * g_ref[3:4, :])

    return _call_with(
        ex, body, "mlp_fwd", (B, S // TB),
        [_tok(D), _mod_spec(nct), _full((8, D)), _resident((DFF // TF, D, TF)), _resident((DFF, D))],
        [_tok(D), _tok(D), _tok(DFF), _tok(D)],
        [SDS((B, S, D), F32), SDS((B, S, D), MXU), SDS((B, S, DFF), MXU), SDS((B, S, D), F32)],
        [h1, modtok, ng, w1, w2], [], _cp(("arbitrary", "arbitrary"), 48))


def _mlp_bwd(dh2, h1, f, z, modtok, ng, w1, w2, nct, ex=None):
    B, S, _ = h1.shape

    def body(dh_ref, h_ref, f_ref, z_ref, mod_ref, g_ref, w1_ref, w2_ref, dh1_ref, df_ref, dz_ref, vec_ref):
        n, r = _rms(f_ref[...])
        dh2 = dh_ref[...]
        gp = g_ref[3:4, :]
        dr = dh2 * mod_ref[5:6, :]
        vec_ref[...] = jnp.zeros_like(vec_ref)
        vec_ref[3:4, :] = _colsum(dh2 * (n * gp))
        vec_ref[4:5, :] = _colsum(dr * n)
        df = _rms_bwd(dr * gp, n, r).astype(MXU)
        df_ref[...] = df
        da = jnp.zeros((TB, D), F32)
        for c in range(DFF // TF):
            cs = slice(c * TF, (c + 1) * TF)
            dr2 = _dot_nt(df, w2_ref[cs, :])
            dz = (dr2 * 2.0 * jnp.maximum(z_ref[:, cs].astype(F32), 0.0)).astype(MXU)
            dz_ref[:, cs] = dz
            da = da + _dot_nt(dz, w1_ref[c])
        n, r = _rms(h_ref[...])
        g = g_ref[2:3, :]
        sc1 = 1.0 + mod_ref[4:5, :]
        vec_ref[0:1, :] = _colsum(da)
        vec_ref[1:2, :] = _colsum(da * (n * g))
        vec_ref[2:3, :] = _colsum(da * sc1 * n)
        dh1_ref[...] = dh2 + _rms_bwd(da * sc1 * g, n, r)

    return _call_with(
        ex, body, "mlp_bwd", (B, S // TB),
        [_tok(D), _tok(D), _tok(D), _tok(DFF), _mod_spec(nct), _full((8, D)),
         _resident((DFF // TF, D, TF)), _resident((DFF, D))],
        [_tok(D), _tok(D), _tok(DFF), BS((None, None, 8, D), lambda b, j: (b, j, 0, 0))],
        [SDS((B, S, D), F32), SDS((B, S, D), MXU), SDS((B, S, DFF), MXU), SDS((B, S // TB, 8, D), F32)],
        [dh2, h1, f, z, modtok, ng, w1, w2], [], _cp(("arbitrary", "arbitrary"), 56))


def _loss(h, target, Lc):
    B, S, _ = h.shape
    nt, nct = S // TB, Lc // TB

    def body(h_ref, t_ref, dh_ref, l_ref):
        lat = pl.program_id(1) >= nct
        err = jnp.where(lat, h_ref[...] - t_ref[...], 0.0)
        dh_ref[...] = err * (1.0 / D)
        l_ref[...] = jnp.broadcast_to(jnp.sum(err * err, keepdims=True), (8, 128))

    return pl.pallas_call(
        body, name="loss", grid=(B, nt),
        in_specs=[_tok(D), BS((None, TB, D), lambda b, j: (b, jnp.maximum(j - nct, 0), 0))],
        out_specs=[_tok(D), BS((None, None, 8, 128), lambda b, j: (b, j, 0, 0))],
        out_shape=[SDS((B, S, D), F32), SDS((B, nt, 8, 128), F32)],
        compiler_params=_cp(("arbitrary", "arbitrary")),
    )(h, target)


def _mm_tn(a, b, name, relu2=False):
    T, M = a.shape
    N = b.shape[1]
    def blk(n):
        return max(b for b in range(128, 1025, 128) if n % b == 0)

    bm, bn = blk(M), blk(N)
    tk = next(t for t in (1536, 1024, 512, 256, T) if T % t == 0)
    nkk = T // tk

    def body(a_ref, b_ref, o_ref, om_ref):
        k = pl.program_id(2)

        @pl.when(k == 0)
        def _():
            o_ref[...] = jnp.zeros_like(o_ref)

        av = a_ref[...]
        if relu2:
            r = jnp.maximum(av.astype(F32), 0.0)
            av = (r * r).astype(MXU)
        o_ref[...] += _dot_tn(av, b_ref[...])

        @pl.when(k == nkk - 1)
        def _():
            om_ref[...] = o_ref[...].astype(MXU)

    out = BS((bm, bn), lambda i, j, k: (i, j))
    return pl.pallas_call(
        body, name=name, grid=(M // bm, N // bn, nkk),
        in_specs=[BS((tk, bm), lambda i, j, k: (k, i)), BS((tk, bn), lambda i, j, k: (k, j))],
        out_specs=[out, out], out_shape=[SDS((M, N), F32), SDS((M, N), MXU)],
        compiler_params=_cp(("arbitrary", "arbitrary", "arbitrary"), 56),
    )(a, b)


_PAD_ROWS = 64


def _block_rows(R):
    for br in (256, _PAD_ROWS):
        if R % br == 0:
            return br
    raise ValueError(f"row count {R} is not a multiple of {_PAD_ROWS}")


def _adamw(w, ga, gb, m, v, name):
    R, C = w.shape
    br = _block_rows(R) if C <= _LANES else _PAD_ROWS

    def body(w_ref, ga_ref, gb_ref, m_ref, v_ref, g_out, d_out, m_out, v_out):
        g = ga_ref[...] + gb_ref[...]
        m2 = B1 * m_ref[...] + (1.0 - B1) * g
        v2 = B2 * v_ref[...] + (1.0 - B2) * (g * g)
        m_hat = m2 / (1.0 - B1 ** STEP)
        v_hat = v2 / (1.0 - B2 ** STEP)
        g_out[...] = g
        d_out[...] = -LR * (m_hat / (jnp.sqrt(v_hat) + AEPS) + WD * w_ref[...])
        m_out[...] = m2
        v_out[...] = v2

    own = BS((br, C), lambda i: (i, 0))
    return pl.pallas_call(
        body, name=name, grid=(R // br,),
        in_specs=[own] * 5, out_specs=[own] * 4, out_shape=[SDS((R, C), F32)] * 4,
        compiler_params=_cp(("arbitrary",)),
    )(w, ga, gb, m, v)


def _my_xyc():
    return lax.axis_index("x"), lax.axis_index("y"), lax.axis_index("c")


def _chip_peers(x, y):
    return [(1 - x, y), (x, 1 - y), (1 - x, 1 - y)]


class _Gather:
    def __init__(self, shards):
        self.args = list(shards)
        self.n = len(self.args)
        self.specs = [ANY] * self.n
        self.out_shape = [SDS((NCHIP,) + a.shape, a.dtype) for a in self.args]
        self.sems = [pltpu.SemaphoreType.DMA((3 * self.n,)), pltpu.SemaphoreType.DMA((3 * self.n,)),
                     pltpu.SemaphoreType.DMA((self.n,))]

    def _copies(self, ins, outs, ssem, rsem, lsem):
        x, y, c = _my_xyc()
        mine, sends, recvs = [], [], []
        for k, (s_ref, o_ref) in enumerate(zip(ins, outs)):
            mine.append(pltpu.make_async_copy(s_ref, o_ref.at[2 * x + y], lsem.at[k]))
            for j, (px, py) in enumerate(_chip_peers(x, y)):
                sems = dict(send_sem=ssem.at[3 * k + j], recv_sem=rsem.at[3 * k + j], device_id=(px, py, c),
                            device_id_type=MESH)
                sends.append(pltpu.make_async_remote_copy(src_ref=s_ref, dst_ref=o_ref.at[2 * x + y], **sems))
                recvs.append(pltpu.make_async_remote_copy(src_ref=s_ref, dst_ref=o_ref.at[2 * px + py], **sems))
        return mine, sends, recvs

    def start(self, ins, outs, sems):
        mine, sends, _ = self._copies(ins, outs, *sems)
        for cp in mine + sends:
            cp.start()

    def wait(self, ins, outs, sems):
        mine, sends, recvs = self._copies(ins, outs, *sems)
        for cp in recvs:
            cp.wait_recv()
        for cp in sends:
            cp.wait_send()
        for cp in mine:
            cp.wait()


class _Gather2(_Gather):
    def __init__(self, shards):
        super().__init__(shards)
        self.sems = self.sems + [pltpu.SemaphoreType.DMA((3 * self.n,)), pltpu.SemaphoreType.DMA((3 * self.n,))]

    def _copies2(self, ins, outs, ssem, rsem, lsem, fsem, gsem):
        x, y, c = _my_xyc()
        mine, sends, recvs, passes, got = [], [], [], [], []
        for k, (s_ref, o_ref) in enumerate(zip(ins, outs)):
            half = s_ref.shape[0] // 2
            my_rows, sib_rows = pl.ds(c * half, half), pl.ds((1 - c) * half, half)
            mine.append(pltpu.make_async_copy(s_ref, o_ref.at[2 * x + y], lsem.at[k]))
            for j, (px, py) in enumerate(_chip_peers(x, y)):
                ici = dict(send_sem=ssem.at[3 * k + j], recv_sem=rsem.at[3 * k + j], device_id=(px, py, c),
                           device_id_type=MESH)
                d2d = dict(send_sem=fsem.at[3 * k + j], recv_sem=gsem.at[3 * k + j], device_id=(x, y, 1 - c),
                           device_id_type=MESH)
                landed = o_ref.at[2 * px + py, my_rows]
                sends.append(pltpu.make_async_remote_copy(src_ref=s_ref.at[my_rows],
                                                          dst_ref=o_ref.at[2 * x + y, my_rows], **ici))
                recvs.append(pltpu.make_async_remote_copy(src_ref=s_ref.at[my_rows], dst_ref=landed, **ici))
                passes.append(pltpu.make_async_remote_copy(src_ref=landed, dst_ref=landed, **d2d))
                got.append(pltpu.make_async_remote_copy(src_ref=landed, dst_ref=o_ref.at[2 * px + py, sib_rows],
                                                        **d2d))
        return mine, sends, recvs, passes, got

    def start(self, ins, outs, sems):
        mine, sends, _, _, _ = self._copies2(ins, outs, *sems)
        for cp in mine + sends:
            cp.start()

    def wait(self, ins, outs, sems):
        mine, sends, recvs, passes, got = self._copies2(ins, outs, *sems)
        for landed, onward in zip(recvs, passes):
            landed.wait_recv()
            onward.start()
        for cp in got:
            cp.wait_recv()
        for cp in sends + passes:
            cp.wait_send()
        for cp in mine:
            cp.wait()


class _Scatter:
    def __init__(self, sends, cols=None):
        self.args = list(sends)
        self.n = len(self.args)
        self.cols = list(cols) if cols is not None else [None] * self.n
        self.specs = [ANY] * self.n
        self.out_shape = [SDS((3,) + a.shape[1:] if cw is None else (3, a.shape[0], cw), a.dtype)
                          for a, cw in zip(self.args, self.cols)]
        self.sems = [pltpu.SemaphoreType.DMA((3 * self.n,)), pltpu.SemaphoreType.DMA((3 * self.n,))]

    def _copies(self, ins, outs, ssem, rsem):
        x, y, c = _my_xyc()

        def piece(k, q):
            cw = self.cols[k]
            return ins[k].at[q] if cw is None else ins[k].at[:, pl.ds(pl.multiple_of(q * cw, 128), cw)]

        return [pltpu.make_async_remote_copy(
            src_ref=piece(k, 2 * px + py), dst_ref=outs[k].at[j], send_sem=ssem.at[3 * k + j],
            recv_sem=rsem.at[3 * k + j], device_id=(px, py, c), device_id_type=MESH)
            for k in range(self.n) for j, (px, py) in enumerate(_chip_peers(x, y))]

    def start(self, ins, outs, sems):
        for cp in self._copies(ins, outs, *sems):
            cp.start()

    def wait(self, ins, outs, sems):
        for cp in self._copies(ins, outs, *sems):
            cp.wait()


class _Swap(_Scatter):
    def __init__(self, arrays):
        self.args = list(arrays)
        self.n = len(self.args)
        self.specs = [ANY] * self.n
        self.out_shape = [SDS(a.shape, a.dtype) for a in self.args]
        self.sems = [pltpu.SemaphoreType.DMA((self.n,)), pltpu.SemaphoreType.DMA((self.n,))]

    def _copies(self, ins, outs, ssem, rsem):
        x, y, c = _my_xyc()
        return [pltpu.make_async_remote_copy(src_ref=s_ref, dst_ref=o_ref, send_sem=ssem.at[k], recv_sem=rsem.at[k],
                                             device_id=(x, y, 1 - c), device_id_type=MESH)
                for k, (s_ref, o_ref) in enumerate(zip(ins, outs))]


def _exchange_call(ex, name):
    n = ex.n

    def body(*refs):
        ins, outs, sems = refs[:n], refs[n:2 * n], refs[2 * n:]
        ex.start(ins, outs, sems)
        ex.wait(ins, outs, sems)

    return pl.pallas_call(body, name=name, in_specs=ex.specs, out_specs=ex.specs, out_shape=ex.out_shape,
                          scratch_shapes=ex.sems)(*ex.args)


def _carried(ex, refs, n_in, n_out):
    n = ex.n
    ins, cin = refs[:n_in], refs[n_in:n_in + n]
    outs, cout = refs[n_in + n:n_in + n + n_out], refs[n_in + n + n_out:n_in + 2 * n + n_out]
    rest = refs[n_in + 2 * n + n_out:]
    nsem = len(ex.sems)
    return list(ins) + list(outs) + list(rest[:len(rest) - nsem]), (cin, cout, rest[len(rest) - nsem:])


def _gather_all(v, name):
    R, C = v.shape

    def body(s_ref, o_ref, ssem, rsem, lsem):
        x, y, c = _my_xyc()
        me = 4 * x + 2 * y + c
        mine = pltpu.make_async_copy(s_ref, o_ref.at[me], lsem)
        mine.start()

        def peer(j):
            fx, fy, fc = (j >> 2) & 1, (j >> 1) & 1, j & 1
            return (x ^ fx, y ^ fy, c ^ fc)

        cps = []
        for j in range(1, 8):
            cps.append(pltpu.make_async_remote_copy(
                src_ref=s_ref, dst_ref=o_ref.at[me], send_sem=ssem.at[j - 1], recv_sem=rsem.at[j - 1],
                device_id=peer(j), device_id_type=MESH))
            cps[-1].start()
        for j in range(1, 8):
            px, py, pc = peer(j)
            pltpu.make_async_remote_copy(
                src_ref=s_ref, dst_ref=o_ref.at[4 * px + 2 * py + pc], send_sem=ssem.at[j - 1],
                recv_sem=rsem.at[j - 1], device_id=peer(j), device_id_type=MESH).wait_recv()
        for cp in cps:
            cp.wait_send()
        mine.wait()

    return pl.pallas_call(
        body, name=name, in_specs=[ANY], out_specs=ANY, out_shape=SDS((8, R, C), v.dtype),
        scratch_shapes=[pltpu.SemaphoreType.DMA((7,)), pltpu.SemaphoreType.DMA((7,)), pltpu.SemaphoreType.DMA],
    )(v)


def _sum_slots(own, slots, name):
    n, R, C = slots.shape
    br = _block_rows(R)

    def body(*refs):
        o_ref = refs[-1]
        if own is None:
            acc = refs[0][0].astype(F32)
            first = 1
            s_ref = refs[0]
        else:
            acc = refs[0][...]
            first = 0
            s_ref = refs[1]
        for k in range(first, n):
            acc = acc + s_ref[k].astype(F32)
        o_ref[...] = acc

    row = BS((br, C), lambda i: (i, 0))
    slab = BS((n, br, C), lambda i: (0, i, 0))
    ins, args = ([slab], [slots]) if own is None else ([row, slab], [own, slots])
    return pl.pallas_call(
        body, name=name, grid=(R // br,), in_specs=ins, out_specs=row, out_shape=SDS((R, C), F32),
        compiler_params=_cp(("arbitrary",)),
    )(*args)


def _rope_tables(L, Lc):
    n = jnp.arange(L)
    row = (n // GRID_W).astype(F32)
    col = (n % GRID_W).astype(F32)
    freqs = ROPE_BASE ** (-jnp.arange(16, dtype=F32) / 16)
    lane = jnp.arange(128)
    dd = lane % HD
    fr = freqs[dd % 16]
    ang = jnp.where(dd < 32, row[:, None], col[:, None]) * fr[None, :]
    sign = jnp.where((dd % 32) < 16, -1.0, 1.0)
    cos = jnp.concatenate([jnp.ones((Lc, 128), F32), jnp.cos(ang)], axis=0)
    sin = jnp.concatenate([jnp.zeros((Lc, 128), F32), jnp.sin(ang) * sign[None, :]], axis=0)
    return cos, sin


def _ssm_prep(lam_re, lam_im, log_dt, b_re, b_im, c_re, c_im, B):
    dt = jnp.exp(log_dt)[..., None]
    mag = jnp.exp(lam_re * dt)
    ar = mag * jnp.cos(lam_im * dt)
    ai = mag * jnp.sin(lam_im * dt)
    den = lam_re * lam_re + lam_im * lam_im
    kr = ((ar - 1.0) * lam_re + ai * lam_im) / den
    ki = (ai * lam_re - (ar - 1.0) * lam_im) / den
    bbr = kr[..., None] * b_re - ki[..., None] * b_im
    bbi = kr[..., None] * b_im + ki[..., None] * b_re
    eye = jnp.eye(SGRP, dtype=F32)

    def bblk(t):
        return jnp.einsum("gpi,gh->gihp", t, eye).reshape(SGRP * SCH, SGRP * SST)

    def cblk(t):
        return jnp.einsum("gip,gh->gphi", t, eye).reshape(SGRP * SST, SGRP * SCH)

    b4 = jnp.concatenate([bblk(bbr[0]), bblk(bbi[0]), bblk(bbr[1]), bblk(bbi[1])], axis=1)
    c2 = jnp.concatenate([jnp.concatenate([cblk(c_re[0]), -cblk(c_im[0])], axis=0),
                          jnp.concatenate([cblk(c_re[1]), -cblk(c_im[1])], axis=0)], axis=1)
    a2 = jnp.concatenate([ar.reshape(2, -1), ai.reshape(2, -1)], axis=1)
    a = jnp.repeat(a2, B, axis=0)
    return a, b4, c2


def _pad_rows(v, rows=8):
    return jnp.concatenate([v, jnp.zeros((rows - v.shape[0],) + v.shape[1:], v.dtype)], axis=0)


_BIG = ("w_ada", "w_in", "w_out", "w_mlp_in", "w_mlp_out")
_BIG_AXIS = {"w_ada": 2, "w_in": 2, "w_out": 1, "w_mlp_in": 2, "w_mlp_out": 1}


def _split_shards(g, axis):
    L, r, c = g.shape
    if axis == 2:
        return jnp.transpose(g.reshape(L, r, NCHIP, c // NCHIP), (2, 0, 1, 3))
    return jnp.transpose(g.reshape(L, NCHIP, r // NCHIP, c), (1, 0, 2, 3))


def _join_shards(s, axis):
    _, L, r, c = s.shape
    if axis == 2:
        return jnp.transpose(s, (1, 2, 0, 3)).reshape(L, r, NCHIP * c)
    return jnp.transpose(s, (1, 0, 2, 3)).reshape(L, NCHIP * r, c)


def _local_step(x, c, ctx, loss_target, P, ex):
    B, L, _ = x.shape
    Lc = ctx.shape[1]
    S = Lc + L
    nct = Lc // TB
    nt = S // TB
    nch = 2 * B
    cos, sin = _rope_tables(L, Lc)
    perm = _scan_perm(B)
    pm, pmt = jnp.asarray(perm, MXU), jnp.asarray(perm.T, MXU)

    c_act = jax.nn.silu(c)
    cc_act, silu_vjp = jax.vjp(jax.nn.silu, P["c_ctx"])
    cact8 = _pad_rows(jnp.concatenate([c_act, cc_act[None, :]], axis=0))
    h = jnp.concatenate([ctx, x], axis=1)

    ssm_names = ("ssm_lam_re", "ssm_lam_im", "ssm_log_dt", "ssm_b_re", "ssm_b_im", "ssm_c_re", "ssm_c_im")
    (a_all, b4_all, c2_all), prep_vjp = jax.vjp(jax.vmap(lambda *t: _ssm_prep(*t, B)), *[P[n] for n in ssm_names])
    b4m_all, c2m_all = b4_all.astype(MXU), c2_all.astype(MXU)
    ng_all = jnp.pad(P["norm_g"], ((0, 0), (0, 4), (0, 0)))
    cw8_all = jnp.pad(P["conv_w"], ((0, 0), (0, 8 - P["conv_w"].shape[1]), (0, 0)))
    wg_all = P["w_glu"].astype(MXU)

    saved = []
    for l in range(NLAYER):
        Wl = ex.weights(l)
        mod8 = _ada_fwd(cact8, Wl["w_ada"], P["b_ada"][l][None, :])
        mod6 = mod8.reshape(8, NMOD, D)
        modtok = jnp.stack([jnp.broadcast_to(mod6[B], (B, NMOD, D)), mod6[:B]], axis=1)
        modtok = jnp.pad(modtok, ((0, 0), (0, 0), (0, 2), (0, 0)))
        ng, cw8, a_s, b4m, c2m, wg = ng_all[l], cw8_all[l], a_all[l], b4m_all[l], c2m_all[l], wg_all[l]
        dsk = P["ssm_d"][l][None, :]
        bg = P["b_glu"][l][None, :]

        p, a = _pre_mix(h, modtok, ng, Wl["w_in"], nct)
        q4, kr = _rope_fwd(p, cos, sin)
        (o,) = ex.attn_fwd(l, P["attn_sink"][l], q4, kr, p, Lc)
        cv = _conv_fwd(p, cw8, nct)
        yf, yb, hb = _scan_fwd(p, pm, pmt, b4m, a_s, c2m, Lc)
        h1, mix, m, y = _post_mix(o, cv, yf, yb, p, h, modtok, ng, dsk, wg, bg, Wl["w_out"], nct)
        h2, a2, z, f = ex.mlp_fwd(l, h1, modtok, ng, Wl["w_mlp_in"], Wl["w_mlp_out"], nct)
        saved.append(dict(Wl=Wl, h=h, modtok=modtok, ng=ng, cw8=cw8, a_s=a_s, b4m=b4m, c2m=c2m, wg=wg, dsk=dsk, bg=bg,
                          p=p, a=a, q4=q4, kr=kr, hb=hb, mix=mix, m=m, y=y, h1=h1, a2=a2, z=z, f=f))
        h = h2

    dh, lpart = _loss(h, loss_target, Lc)
    loss_local = 0.5 / D * jnp.sum(lpart[:, :, 0, 0])

    R = {k: [None] * NLAYER for k in ("w_glu", "vec_p2", "scan", "dcw", "dsink", "vec", "b_ada")}
    dcact = jnp.zeros((8, D), F32)
    T = B * S
    for l in reversed(range(NLAYER)):
        sv = saved[l]
        modtok, ng, Wl = sv["modtok"], sv["ng"], sv["Wl"]
        dh1, df, dz, vec_m = ex.mlp_bwd(l, dh, sv["h1"], sv["f"], sv["z"], modtok, ng, Wl["w_mlp_in"],
                                        Wl["w_mlp_out"], nct)
        ex.submit(l, "w_mlp_out", *_mm_tn(sv["z"].reshape(T, DFF), df.reshape(T, D), "dw_mlp_out", relu2=True))
        ex.submit(l, "w_mlp_in", *_mm_tn(sv["a2"].reshape(T, D), dz.reshape(T, DFF), "dw_mlp_in"))
        dm, dattn, dcv, dy, du_skip, vec_p, vec_p2, dwg = _post_bwd(
            dh1, sv["m"], sv["y"], sv["p"], modtok, ng, sv["dsk"], sv["wg"], sv["bg"], Wl["w_out"], nct)
        ex.submit(l, "w_out", *_mm_tn(sv["mix"].reshape(T, D), dm.reshape(T, D), "dw_out"))
        R["w_glu"][l], R["vec_p2"][l] = dwg, vec_p2
        duf, dub, db4, da_s, dc2 = ex.scan_bwd(l, sv["p"], dy, sv["hb"], pm, pmt, sv["b4m"], sv["a_s"], sv["c2m"],
                                               Lc)
        R["scan"][l] = (da_s, db4, dc2)
        dconv, R["dcw"][l] = _conv_bwd(sv["p"], dcv, sv["cw8"], nct)
        dq_r, dk_r, dv, R["dsink"][l] = ex.attn_bwd(l, P["attn_sink"][l], sv["q4"], sv["kr"], sv["p"], dattn, Lc)
        dqk = _rope_bwd(dq_r, dk_r, cos, -sin)
        dh, vec_i, dp = _pre_bwd(dqk, dv, dconv, (du_skip, duf, dub), dh1, sv["h"], modtok, ng, Wl["w_in"], nct)
        ex.submit(l, "w_in", *_mm_tn(sv["a"].reshape(T, D), dp.reshape(T, INW), "dw_in"))
        vec = jnp.concatenate([vec_i, vec_p, vec_m], axis=2)
        R["vec"][l] = vec
        mod_rows = jnp.concatenate([vec[:, :, 0:2], vec[:, :, 8:9], vec[:, :, 16:18], vec[:, :, 19:20]], axis=2)
        dmod8 = jnp.concatenate([jnp.sum(mod_rows[:, nct:], axis=1).reshape(B, NMOD * D),
                                 jnp.sum(mod_rows[:, :nct], axis=(0, 1)).reshape(1, NMOD * D),
                                 jnp.zeros((7 - B, NMOD * D), F32)], axis=0)
        R["b_ada"][l] = dmod8
        dmod8m = dmod8.astype(MXU)
        ex.submit(l, "w_ada", *_mm_tn(cact8.astype(MXU), dmod8m, "dw_ada"))
        dcact = dcact + _ada_bwd_c(dmod8m, Wl["w_ada"])

    stk = lambda k: jnp.stack(R[k])
    vec, vec_p2 = jnp.sum(stk("vec"), axis=(1, 2)), jnp.sum(stk("vec_p2"), axis=(1, 2))
    grads = dict(zip(ssm_names, prep_vjp(tuple(jnp.stack([R["scan"][l][k] for l in range(NLAYER)])
                                               for k in range(3)))))
    norm_rows = jnp.concatenate([vec[:, 2:3], vec[:, 9:10], vec[:, 18:19], vec[:, 20:21]], axis=1)
    grads.update(w_glu=stk("w_glu"), ssm_d=vec_p2[:, 0], b_glu=vec_p2[:, 1], norm_g=norm_rows,
                 conv_w=jnp.sum(stk("dcw"), axis=(1, 2))[:, 0:P["conv_w"].shape[1]],
                 attn_sink=jnp.sum(stk("dsink")[..., 0], axis=1), b_ada=jnp.sum(stk("b_ada"), axis=1),
                 c_ctx=silu_vjp(dcact[B])[0])
    return loss_local, dh[:, Lc:, :], grads


_WEIGHTS = ["c_ctx", "w_ada", "b_ada", "norm_g", "w_in", "conv_w", "attn_sink", "ssm_lam_re", "ssm_lam_im",
            "ssm_log_dt", "ssm_b_re", "ssm_b_im", "ssm_c_re", "ssm_c_im", "ssm_d", "w_glu", "b_glu", "w_out",
            "w_mlp_in", "w_mlp_out"]
_SMALL = [n for n in _WEIGHTS if n not in _BIG]
_SMALL_SHARDED = {"norm_g": 2, "conv_w": 2, "w_glu": 1}
_LANES = 1024


_GROUPS = (("w_ada", "w_in", "w_out"), ("w_mlp_in", "w_mlp_out"))
_QUEUE_OF = {"w_ada": "narrow", "w_in": "narrow", "w_out": "out", "w_mlp_in": "mlp", "w_mlp_out": "mlp"}


class _Exchange:
    def __init__(self, W, q_me):
        self.W, self.q_me = W, q_me
        self.wall, self.own, self.recv, self.pending = {}, {}, {}, {}
        self._keep(0, _BIG, _exchange_call(_Gather2([self._shard(0, n) for n in _BIG]), "gather_big"))

    def _shard(self, l, n):
        return self.W[n][l].astype(MXU)

    def _keep(self, l, names, walls):
        for n, w in zip(names, walls):
            self.wall[l, n] = w

    def weights(self, l):
        g = {n: self.wall[l, n] for n in _BIG}
        return dict(w_ada=g["w_ada"], w_mlp_in=g["w_mlp_in"], w_in=_join_shards(g["w_in"][:, None], 2)[0],
                    w_out=g["w_out"].reshape(-1, g["w_out"].shape[-1]),
                    w_mlp_out=g["w_mlp_out"].reshape(-1, g["w_mlp_out"].shape[-1]))

    def _gathering(self, l, grp, fn, *args):
        if l + 1 == NLAYER:
            return fn(*args)[0]
        res, walls = fn(*args, ex=_Gather([self._shard(l + 1, n) for n in _GROUPS[grp]]))
        self._keep(l + 1, _GROUPS[grp], walls)
        return res

    def attn_fwd(self, l, *args):
        return self._gathering(l, 0, _attn_fwd, *args)

    def mlp_fwd(self, l, *args):
        return self._gathering(l, 1, _mlp_fwd, *args)

    def submit(self, l, n, g, gm):
        r, c = g.shape
        if _BIG_AXIS[n] == 2:
            cw = c // NCHIP
            self.own[l, n] = lax.dynamic_slice_in_dim(g, self.q_me * cw, cw, axis=1)
            if cw % 128 == 0:
                item = (gm, cw)
            else:
                item = (jnp.transpose(gm.reshape(r, NCHIP, cw), (1, 0, 2)), None)
        else:
            self.own[l, n] = lax.dynamic_slice_in_dim(g, self.q_me * (r // NCHIP), r // NCHIP, axis=0)
            item = (gm.reshape(NCHIP, r // NCHIP, c), None)
        self.pending.setdefault(_QUEUE_OF[n], []).append((l, n) + item)

    def _scatter_of(self, items):
        return _Scatter([it[2] for it in items], [it[3] for it in items])

    def _scattering(self, queue, fn, *args):
        items = self.pending.pop(queue, [])
        if not items:
            return fn(*args)[0]
        res, recvs = fn(*args, ex=self._scatter_of(items))
        for it, r in zip(items, recvs):
            self.recv[it[0], it[1]] = r
        return res

    def mlp_bwd(self, l, *args):
        return self._scattering("narrow", _mlp_bwd, *args)

    def scan_bwd(self, l, *args):
        return self._scattering("mlp", _scan_bwd, *args)

    def attn_bwd(self, l, *args):
        return self._scattering("out", _attn_bwd, *args)

    def finish(self):
        items = [it for queue in sorted(self.pending) for it in self.pending[queue]]
        self.pending = {}
        for it, r in zip(items, _exchange_call(self._scatter_of(items), "scatter_big")):
            self.recv[it[0], it[1]] = r
        part = [jnp.concatenate([_sum_slots(self.own[l, n], self.recv[l, n], "sum_chips") for l in range(NLAYER)],
                                axis=0) for n in _BIG]
        sib = _exchange_call(_Swap(part), "swap_big")
        return {n: (p, s) for n, p, s in zip(_BIG, part, sib)}


def _part_rows(shape):
    return -(-math.prod(shape) // (16 * _LANES)) * 16


def _flat_pad(parts):
    rows = []
    for p in parts:
        v = p.reshape(-1)
        pad = _part_rows(p.shape) * _LANES - v.shape[0]
        rows.append(jnp.concatenate([v, jnp.zeros((pad,), v.dtype)]).reshape(-1, _LANES))
    used = sum(r.shape[0] for r in rows)
    rows.append(jnp.zeros((-used % _PAD_ROWS, _LANES), rows[0].dtype))
    return jnp.concatenate(rows, axis=0)


def _unflat(flat, shapes):
    out, r0 = [], 0
    for s in shapes:
        nr = _part_rows(s)
        out.append(flat[r0:r0 + nr].reshape(-1)[:math.prod(s)].reshape(s))
        r0 += nr
    return out


def kernel(x, c, ctx, c_ctx, w_ada, b_ada, norm_g, w_in, conv_w, attn_sink, ssm_lam_re, ssm_lam_im, ssm_log_dt, ssm_b_re, ssm_b_im, ssm_c_re, ssm_c_im, ssm_d, w_glu, b_glu, w_out, w_mlp_in, w_mlp_out, loss_target, m_c_ctx, m_w_ada, m_b_ada, m_norm_g, m_w_in, m_conv_w, m_attn_sink, m_ssm_lam_re, m_ssm_lam_im, m_ssm_log_dt, m_ssm_b_re, m_ssm_b_im, m_ssm_c_re, m_ssm_c_im, m_ssm_d, m_w_glu, m_b_glu, m_w_out, m_w_mlp_in, m_w_mlp_out, v_c_ctx, v_w_ada, v_b_ada, v_norm_g, v_w_in, v_conv_w, v_attn_sink, v_ssm_lam_re, v_ssm_lam_im, v_ssm_log_dt, v_ssm_b_re, v_ssm_b_im, v_ssm_c_re, v_ssm_c_im, v_ssm_d, v_w_glu, v_b_glu, v_w_out, v_w_mlp_in, v_w_mlp_out):
    W = dict(c_ctx=c_ctx, w_ada=w_ada, b_ada=b_ada, norm_g=norm_g, w_in=w_in, conv_w=conv_w, attn_sink=attn_sink,
             ssm_lam_re=ssm_lam_re, ssm_lam_im=ssm_lam_im, ssm_log_dt=ssm_log_dt, ssm_b_re=ssm_b_re,
             ssm_b_im=ssm_b_im, ssm_c_re=ssm_c_re, ssm_c_im=ssm_c_im, ssm_d=ssm_d, w_glu=w_glu, b_glu=b_glu,
             w_out=w_out, w_mlp_in=w_mlp_in, w_mlp_out=w_mlp_out)
    M = dict(c_ctx=m_c_ctx, w_ada=m_w_ada, b_ada=m_b_ada, norm_g=m_norm_g, w_in=m_w_in, conv_w=m_conv_w,
             attn_sink=m_attn_sink, ssm_lam_re=m_ssm_lam_re, ssm_lam_im=m_ssm_lam_im, ssm_log_dt=m_ssm_log_dt,
             ssm_b_re=m_ssm_b_re, ssm_b_im=m_ssm_b_im, ssm_c_re=m_ssm_c_re, ssm_c_im=m_ssm_c_im, ssm_d=m_ssm_d,
             w_glu=m_w_glu, b_glu=m_b_glu, w_out=m_w_out, w_mlp_in=m_w_mlp_in, w_mlp_out=m_w_mlp_out)
    V = dict(c_ctx=v_c_ctx, w_ada=v_w_ada, b_ada=v_b_ada, norm_g=v_norm_g, w_in=v_w_in, conv_w=v_conv_w,
             attn_sink=v_attn_sink, ssm_lam_re=v_ssm_lam_re, ssm_lam_im=v_ssm_lam_im, ssm_log_dt=v_ssm_log_dt,
             ssm_b_re=v_ssm_b_re, ssm_b_im=v_ssm_b_im, ssm_c_re=v_ssm_c_re, ssm_c_im=v_ssm_c_im, ssm_d=v_ssm_d,
             w_glu=v_w_glu, b_glu=v_b_glu, w_out=v_w_out, w_mlp_in=v_w_mlp_in, w_mlp_out=v_w_mlp_out)
    q_me = 2 * lax.axis_index("x") + lax.axis_index("y")

    ex = _Exchange(W, q_me)
    P = {n: W[n] for n in _SMALL}
    ssh_names = list(_SMALL_SHARDED)
    ssh = _flat_pad([W[n] for n in ssh_names])
    (sall,) = _exchange_call(_Gather([ssh]), "gather_small")
    parts = [_unflat(sall[q], [W[n].shape for n in ssh_names]) for q in range(NCHIP)]
    for k, n in enumerate(ssh_names):
        P[n] = _join_shards(jnp.stack([parts[q][k] for q in range(NCHIP)]), _SMALL_SHARDED[n])

    loss_local, grad_x, G = _local_step(x, c, ctx, loss_target, P, ex)
    loss = lax.psum(loss_local, ("x", "y", "c"))

    sums = ex.finish()
    out = {}
    for n in _BIG:
        flat = lambda t: t.reshape(-1, t.shape[-1])
        res = _adamw(flat(W[n]), *sums[n], flat(M[n]), flat(V[n]), "adamw_" + n)
        out[n] = [t.reshape(W[n].shape) for t in res]

    gsmall = _flat_pad([G[n] for n in _SMALL]).astype(MXU)
    gsum = _sum_slots(None, _gather_all(gsmall, "gather_grads"), "sum_devices")
    gfull = dict(zip(_SMALL, _unflat(gsum, [G[n].shape for n in _SMALL])))
    for n, ax in _SMALL_SHARDED.items():
        width = W[n].shape[ax]
        gfull[n] = lax.dynamic_slice_in_dim(gfull[n], q_me * width, width, axis=ax)
    shapes = [W[n].shape for n in _SMALL]
    gflat = _flat_pad([gfull[n] for n in _SMALL])
    res = _adamw(_flat_pad([W[n] for n in _SMALL]), gflat, jnp.zeros_like(gflat),
                 _flat_pad([M[n] for n in _SMALL]), _flat_pad([V[n] for n in _SMALL]), "adamw_small")
    for k, t in enumerate(res):
        for n, piece in zip(_SMALL, _unflat(t, shapes)):
            out.setdefault(n, [None] * 4)[k] = piece

    return (loss, grad_x, *[out[n][0] for n in _WEIGHTS], *[out[n][1] for n in _WEIGHTS],
            *[out[n][2] for n in _WEIGHTS], *[out[n][3] for n in _WEIGHTS])
```

```python
import functools
import math

import jax
import jax.numpy as jnp
import numpy as np
from jax import lax
from jax.experimental import pallas as pl
from jax.experimental.pallas import tpu as pltpu

F32 = jnp.float32
MXU = jnp.bfloat16

D = 1024
DFF = 4096
NMOD = 6
EPS = 1e-6
HD = 64
NQ = 8
NKV = 2
QPK = 4
AW = 512
KVW = 128
WIN = 128
QB = 128
CW = 256
SW = 256
SGRP = 16
SCH = 16
SST = 64
INW = 1792
GRID_W = 64
ROPE_BASE = 10000.0
NEG = -1e30
SCALE = HD ** -0.5
NLAYER = 4

TB = 256
TF = 1024
TC = 32
NCHIP = 4

LR, B1, B2, AEPS, WD, STEP = 0.001, 0.9, 0.999, 1e-08, 0.01, 10

MESH = pl.DeviceIdType.MESH
SDS = jax.ShapeDtypeStruct
BS = pl.BlockSpec
ANY = pl.BlockSpec(memory_space=pl.ANY)


def _cp(sem, vmem_mb=None):
    kw = dict(dimension_semantics=sem)
    if vmem_mb is not None:
        kw["vmem_limit_bytes"] = vmem_mb * 1024 * 1024
    return pltpu.CompilerParams(**kw)


def _dot(a, b):
    return jnp.dot(a, b, preferred_element_type=F32)


def _dot_nt(a, b):
    return lax.dot_general(a, b, (((1,), (1,)), ((), ())), preferred_element_type=F32)


def _dot_tn(a, b):
    return lax.dot_general(a, b, (((0,), (0,)), ((), ())), preferred_element_type=F32)


def _rms(x):
    r = lax.rsqrt(jnp.mean(x * x, axis=-1, keepdims=True) + EPS)
    return x * r, r


def _rms_bwd(dn, n, r):
    return r * (dn - n * jnp.mean(dn * n, axis=-1, keepdims=True))


_GC = math.sqrt(2.0 / math.pi)


def _gelu(y):
    return 0.5 * y * (1.0 + jnp.tanh(_GC * (y + 0.044715 * y * y * y)))


def _gelu_grad(y):
    th = jnp.tanh(_GC * (y + 0.044715 * y * y * y))
    return 0.5 * (1.0 + th) + 0.5 * y * (1.0 - th * th) * _GC * (1.0 + 3 * 0.044715 * y * y)


def _sigmoid(v):
    return 1.0 / (1.0 + jnp.exp(-v))


def _colsum(x):
    return jnp.sum(x, axis=0, keepdims=True)


def _tok(width, col=0):
    return BS((None, TB, width), lambda b, j: (b, j, col))


def _mod_spec(nct):
    return BS((None, None, 8, D), lambda b, j: (b, jnp.where(j >= nct, 1, 0), 0, 0))


def _full(shape):
    nd = len(shape)
    return BS(shape, lambda *a: (0,) * nd)


def _resident(shape):
    nd = len(shape)
    return BS(shape, lambda *a: (0,) * nd, pipeline_mode=pl.Buffered(1))


def _ada_fwd(cact8, w4, b):
    wq = w4.shape[2]

    def body(c_ref, w_ref, b_ref, o_ref):
        o_ref[...] = _dot(c_ref[...].astype(MXU), w_ref[...]) + b_ref[...]

    return pl.pallas_call(
        body, name="ada_fwd", grid=(NCHIP,),
        in_specs=[BS((8, D), lambda j: (0, 0)), BS((None, D, wq), lambda j: (j, 0, 0)), BS((1, wq), lambda j: (0, j))],
        out_specs=BS((8, wq), lambda j: (0, j)),
        out_shape=SDS((8, NMOD * D), F32), compiler_params=_cp(("arbitrary",)),
    )(cact8, w4, b)


def _ada_bwd_c(dmod8, w4):
    wq = w4.shape[2]

    def body(d_ref, w_ref, o_ref):
        @pl.when(pl.program_id(0) == 0)
        def _():
            o_ref[...] = jnp.zeros_like(o_ref)
        o_ref[...] += _dot_nt(d_ref[...], w_ref[...])

    return pl.pallas_call(
        body, name="ada_bwd_c", grid=(NCHIP,),
        in_specs=[BS((8, wq), lambda j: (0, j)), BS((None, D, wq), lambda j: (j, 0, 0))],
        out_specs=BS((8, D), lambda j: (0, 0)),
        out_shape=SDS((8, D), F32), compiler_params=_cp(("arbitrary",)),
    )(dmod8, w4)


def _pre_mix(h, modtok, ng, w_in, nct, ex=None):
    B, S, _ = h.shape

    def body(h_ref, mod_ref, g_ref, w_ref, p_ref, a_ref):
        n, _ = _rms(h_ref[...])
        a = ((n * g_ref[0:1, :]) * (1.0 + mod_ref[1:2, :]) + mod_ref[0:1, :]).astype(MXU)
        a_ref[...] = a
        p_ref[...] = _dot(a, w_ref[...])

    return _call_with(
        ex, body, "pre_mix", (B, S // TB),
        [_tok(D), _mod_spec(nct), _full((8, D)), _full((D, INW))], [_tok(INW), _tok(D)],
        [SDS((B, S, INW), F32), SDS((B, S, D), MXU)], [h, modtok, ng, w_in], [],
        _cp(("arbitrary", "arbitrary"), 40))


def _pre_bwd(dqk, dv, dcv, du_parts, dh_part, h, modtok, ng, w_in, nct):
    B, S, _ = h.shape

    def body(dqk_ref, dv_ref, dcv_ref, du0_ref, du1_ref, du2_ref, dhp_ref, h_ref, mod_ref, g_ref, w_ref,
             dh_ref, vec_ref, dp_ref):
        du = du0_ref[...] + du1_ref[...] + du2_ref[...]
        dp = jnp.concatenate([dqk_ref[...], dv_ref[...].astype(MXU), dcv_ref[...].astype(MXU), du.astype(MXU)],
                             axis=1)
        dp_ref[...] = dp
        da = _dot_nt(dp, w_ref[...])
        n, r = _rms(h_ref[...])
        g = g_ref[0:1, :]
        sc1 = 1.0 + mod_ref[1:2, :]
        vec_ref[...] = jnp.zeros_like(vec_ref)
        vec_ref[0:1, :] = _colsum(da)
        vec_ref[1:2, :] = _colsum(da * (n * g))
        vec_ref[2:3, :] = _colsum(da * sc1 * n)
        dh_ref[...] = dhp_ref[...] + _rms_bwd(da * sc1 * g, n, r)

    return pl.pallas_call(
        body, name="pre_bwd", grid=(B, S // TB),
        in_specs=[_tok(AW + KVW), _tok(KVW), _tok(3 * CW), _tok(SW), _tok(SW), _tok(SW), _tok(D), _tok(D),
                  _mod_spec(nct), _full((8, D)), _full((D, INW))],
        out_specs=[_tok(D), BS((None, None, 8, D), lambda b, j: (b, j, 0, 0)), _tok(INW)],
        out_shape=[SDS((B, S, D), F32), SDS((B, S // TB, 8, D), F32), SDS((B, S, INW), MXU)],
        compiler_params=_cp(("arbitrary", "arbitrary"), 40),
    )(dqk, dv, dcv, *du_parts, dh_part, h, modtok, ng, w_in)


def _rotate(t, c, s):
    lane = lax.broadcasted_iota(jnp.int32, t.shape, 1)
    partner = jnp.where((lane % 32) < 16, pltpu.roll(t, 112, 1), pltpu.roll(t, 16, 1))
    return t * c + partner * s


def _to_groups(cols):
    lo = lax.broadcasted_iota(jnp.int32, cols[0].shape, 1) < HD
    out = []
    for g in range(QPK):
        a, b = cols[g // 2], cols[2 + g // 2]
        out.append(jnp.where(lo, a, pltpu.roll(b, HD, 1)) if g % 2 == 0 else jnp.where(lo, pltpu.roll(a, HD, 1), b))
    return out


def _from_groups(tiles):
    lo = lax.broadcasted_iota(jnp.int32, tiles[0].shape, 1) < HD
    out = []
    for k in range(NQ // 2):
        a, b = tiles[(2 * k) % QPK], tiles[(2 * k) % QPK + 1]
        out.append(jnp.where(lo, a, pltpu.roll(b, HD, 1)) if k < 2 else jnp.where(lo, pltpu.roll(a, HD, 1), b))
    return out


def _grp(width=128):
    return BS((None, QPK, TB, width), lambda b, j: (b, 0, j, 0))


_TAB = BS((TB, 128), lambda b, j: (j, 0))


def _rope_fwd(p, cos, sin):
    B, S, _ = p.shape

    def body(x_ref, c_ref, s_ref, q_ref, k_ref):
        c, s = c_ref[...], s_ref[...]
        rot = [_rotate(x_ref[:, k * 128:(k + 1) * 128], c, s) for k in range((AW + KVW) // 128)]
        k_ref[...] = rot[AW // 128].astype(MXU)
        for g, t in enumerate(_to_groups(rot[:AW // 128])):
            q_ref[g] = (t * SCALE).astype(MXU)

    return pl.pallas_call(
        body, name="rope_fwd", grid=(B, S // TB),
        in_specs=[_tok(AW + KVW), _TAB, _TAB],
        out_specs=[_grp(), _tok(KVW)],
        out_shape=[SDS((B, QPK, S, 128), MXU), SDS((B, S, KVW), MXU)],
        compiler_params=_cp(("arbitrary", "arbitrary")),
    )(p, cos, sin)


def _rope_bwd(dq4, dk, cos, nsin):
    B, _, S, _ = dq4.shape

    def body(q_ref, k_ref, c_ref, s_ref, o_ref):
        c, s = c_ref[...], s_ref[...]
        cols = _from_groups([q_ref[g] * SCALE for g in range(QPK)]) + [k_ref[...]]
        for k, t in enumerate(cols):
            o_ref[:, k * 128:(k + 1) * 128] = _rotate(t, c, s).astype(MXU)

    return pl.pallas_call(
        body, name="rope_bwd", grid=(B, S // TB),
        in_specs=[_grp(), _tok(KVW), _TAB, _TAB],
        out_specs=_tok(AW + KVW),
        out_shape=SDS((B, S, AW + KVW), MXU),
        compiler_params=_cp(("arbitrary", "arbitrary")),
    )(dq4, dk, cos, nsin)


def _attn_masks(i, S, Lc):
    ncb = Lc // QB
    is_lat = i >= ncb
    w0 = pl.multiple_of(jnp.clip((i - 1) * QB, 0, S - 3 * QB), QB)
    shape = (QPK * QB, 3 * QB)
    qpos = i * QB + jnp.bitwise_and(lax.broadcasted_iota(jnp.int32, shape, 0), QB - 1)
    kpos = w0 + lax.broadcasted_iota(jnp.int32, shape, 1)
    mask = jnp.logical_and(jnp.logical_and(kpos >= Lc, jnp.abs(qpos - kpos) <= WIN), is_lat)
    return w0, mask


def _attn_probs(qh, kch, kwh, mask, sk):
    sc = _dot_nt(qh, kch)
    sw = jnp.where(mask, _dot_nt(qh, kwh), NEG)
    m = jnp.maximum(jnp.maximum(jnp.max(sc, axis=-1, keepdims=True), jnp.max(sw, axis=-1, keepdims=True)), sk)
    ec = jnp.exp(sc - m)
    ew = jnp.exp(sw - m)
    es = jnp.exp(sk - m)
    inv = 1.0 / (jnp.sum(ec, axis=-1, keepdims=True) + jnp.sum(ew, axis=-1, keepdims=True) + es)
    return ec * inv, ew * inv, es * inv


def _sink_col(sink_ref, hk):
    return jnp.concatenate([jnp.full((QB, 1), sink_ref[hk * QPK + g], F32) for g in range(QPK)], axis=0)


_QGRP = BS((None, QPK, QB, 128), lambda b, i: (b, 0, i, 0))


def _first_last(grid):
    first = last = None
    for ax, n in enumerate(grid):
        i = pl.program_id(ax)
        first = (i == 0) if first is None else jnp.logical_and(first, i == 0)
        last = (i == n - 1) if last is None else jnp.logical_and(last, i == n - 1)
    return first, last


def _call_with(ex, body, name, grid, in_specs, out_specs, out_shape, args, scratch, params):
    n_in, n_out = len(in_specs), len(out_specs)
    if ex is None:
        res = pl.pallas_call(body, name=name, grid=grid, in_specs=in_specs, out_specs=out_specs,
                             out_shape=out_shape, scratch_shapes=scratch, compiler_params=params)(*args)
        return list(res), []

    def carrying(*refs):
        own, comm = _carried(ex, refs, n_in, n_out)
        first, last = _first_last(grid)
        pl.when(first)(lambda: ex.start(*comm))
        body(*own)
        pl.when(last)(lambda: ex.wait(*comm))

    res = pl.pallas_call(
        carrying, name=f"{name}_{type(ex).__name__.strip('_').lower()}", grid=grid,
        in_specs=in_specs + ex.specs, out_specs=out_specs + ex.specs, out_shape=out_shape + ex.out_shape,
        scratch_shapes=scratch + ex.sems, compiler_params=params)(*args, *ex.args)
    return list(res[:n_out]), list(res[n_out:])


def _attn_fwd(sink, q4, k, p, Lc, ex=None):
    B, S, _ = k.shape

    def body(sink_ref, q_ref, k_ref, v_ref, o_ref):
        w0, mask = _attn_masks(pl.program_id(1), S, Lc)
        kc = k_ref[0:Lc, :]
        vc = v_ref[0:Lc, :].astype(MXU)
        kw = k_ref[pl.ds(w0, 3 * QB), :]
        vw = v_ref[pl.ds(w0, 3 * QB), :].astype(MXU)
        q = q_ref[...].reshape(QPK * QB, 128)
        half = QPK * QB // 2
        outs = []
        for hk in range(NKV):
            cs = slice(hk * HD, (hk + 1) * HD)
            sk = _sink_col(sink_ref, hk)
            for r0 in (0, half):
                rows = slice(r0, r0 + half)
                pc, pw, _ = _attn_probs(q[rows, cs], kc[:, cs], kw[:, cs], mask[:half], sk[rows])
                o = _dot(pc.astype(MXU), vc[:, cs]) + _dot(pw.astype(MXU), vw[:, cs])
                outs += [o[g * QB:(g + 1) * QB] for g in range(half // QB)]
        o_ref[...] = jnp.concatenate(outs, axis=1).astype(o_ref.dtype)

    return _call_with(
        ex, body, "attn_fwd", (B, S // QB),
        [BS(memory_space=pltpu.SMEM), _QGRP, BS((None, S, KVW), lambda b, i: (b, 0, 0)),
         BS((None, S, KVW), lambda b, i: (b, 0, (AW + KVW) // KVW))],
        [BS((None, QB, AW), lambda b, i: (b, i, 0))], [SDS((B, S, AW), MXU)], [sink, q4, k, p], [],
        _cp(("arbitrary", "arbitrary"), 40))


def _attn_bwd(sink, q4, k, p, do4, Lc, ex=None):
    B, S, _ = k.shape

    def body(sink_ref, q_ref, k_ref, v_ref, do_ref, dq_ref, dk_ref, dv_ref, ds_ref):
        i = pl.program_id(1)

        @pl.when(i == 0)
        def _():
            dk_ref[...] = jnp.zeros_like(dk_ref)
            dv_ref[...] = jnp.zeros_like(dv_ref)
            ds_ref[...] = jnp.zeros_like(ds_ref)

        w0, mask = _attn_masks(i, S, Lc)
        kc = k_ref[0:Lc, :]
        vc = v_ref[0:Lc, :].astype(MXU)
        kw = k_ref[pl.ds(w0, 3 * QB), :]
        vw = v_ref[pl.ds(w0, 3 * QB), :].astype(MXU)
        q = q_ref[...].reshape(QPK * QB, 128)
        do = do_ref[...].reshape(QPK * QB, 128)
        dqs, dsinks, dkcs, dkws, dvcs, dvws = [], [], [], [], [], []
        for hk in range(NKV):
            cs = slice(hk * HD, (hk + 1) * HD)
            qh = q[:, cs]
            pc, pw, ps = _attn_probs(qh, kc[:, cs], kw[:, cs], mask, _sink_col(sink_ref, hk))
            dob = do[:, cs].astype(MXU)
            dpc = _dot_nt(dob, vc[:, cs])
            dpw = _dot_nt(dob, vw[:, cs])
            delta = jnp.sum(pc * dpc, axis=-1, keepdims=True) + jnp.sum(pw * dpw, axis=-1, keepdims=True)
            dsc = (pc * (dpc - delta)).astype(MXU)
            dsw = (pw * (dpw - delta)).astype(MXU)
            dqs.append(_dot(dsc, kc[:, cs]) + _dot(dsw, kw[:, cs]))
            dkcs.append(_dot_tn(dsc, qh))
            dkws.append(_dot_tn(dsw, qh))
            dvcs.append(_dot_tn(pc.astype(MXU), dob))
            dvws.append(_dot_tn(pw.astype(MXU), dob))
            psd = ps * delta
            dsinks += [jnp.broadcast_to(-jnp.sum(psd[g * QB:(g + 1) * QB], axis=0, keepdims=True), (1, 128))
                       for g in range(QPK)]
        for g in range(QPK):
            dq_ref[g] = jnp.concatenate([dq[g * QB:(g + 1) * QB] for dq in dqs], axis=1)
        ds_ref[...] += jnp.concatenate(dsinks, axis=0)
        dk_ref[0:Lc, :] += jnp.concatenate(dkcs, axis=1)
        dv_ref[0:Lc, :] += jnp.concatenate(dvcs, axis=1)
        dk_ref[pl.ds(w0, 3 * QB), :] += jnp.concatenate(dkws, axis=1)
        dv_ref[pl.ds(w0, 3 * QB), :] += jnp.concatenate(dvws, axis=1)

    kv_acc = BS((None, S, KVW), lambda b, i: (b, 0, 0))
    return _call_with(
        ex, body, "attn_bwd", (B, S // QB),
        [BS(memory_space=pltpu.SMEM), _QGRP, BS((None, S, KVW), lambda b, i: (b, 0, 0)),
         BS((None, S, KVW), lambda b, i: (b, 0, (AW + KVW) // KVW)), _QGRP],
        [_QGRP, kv_acc, kv_acc, BS((None, 8, 128), lambda b, i: (b, 0, 0))],
        [SDS((B, QPK, S, 128), F32), SDS((B, S, KVW), F32), SDS((B, S, KVW), F32), SDS((B, 8, 128), F32)],
        [sink, q4, k, p, do4], [], _cp(("arbitrary", "arbitrary"), 48))


def _halo_specs(width, col, S):
    nb8 = S // 8
    per = TB // 8
    prev = BS((None, 8, width), lambda b, j: (b, jnp.maximum(j * per - 1, 0), col))
    nxt = BS((None, 8, width), lambda b, j: (b, jnp.minimum((j + 1) * per, nb8 - 1), col))
    return prev, nxt


def _shift_dn(z, prev_row, at_start):
    row = lax.broadcasted_iota(jnp.int32, z.shape, 0)
    first = jnp.where(at_start, 0.0, 1.0) * prev_row
    return jnp.where(row == 0, first, pltpu.roll(z, 1, 0))


def _shift_up(z, next_row, at_end):
    row = lax.broadcasted_iota(jnp.int32, z.shape, 0)
    last = jnp.where(at_end, 0.0, 1.0) * next_row
    return jnp.where(row == z.shape[0] - 1, last, pltpu.roll(z, z.shape[0] - 1, 0))


def _conv_fwd(p, cw8, nct):
    B, S, _ = p.shape
    nt = S // TB

    def body(p_ref, pp_ref, pn_ref, w_ref, o_ref):
        j = pl.program_id(1)
        at_start = jnp.logical_or(j == 0, j == nct)
        at_end = jnp.logical_or(j == nct - 1, j == nt - 1)
        z = p_ref[:, 256:512] * p_ref[:, 512:768]
        zprev = pp_ref[7:8, 256:512] * pp_ref[7:8, 512:768]
        znext = pn_ref[0:1, 256:512] * pn_ref[0:1, 512:768]
        c3 = (_shift_dn(z, zprev, at_start) * w_ref[0:1, :] + z * w_ref[1:2, :]
              + _shift_up(z, znext, at_end) * w_ref[2:3, :])
        o_ref[...] = p_ref[:, 0:256] * c3

    prev, nxt = _halo_specs(3 * CW, 1, S)
    return pl.pallas_call(
        body, name="conv_fwd", grid=(B, nt),
        in_specs=[_tok(3 * CW, 1), prev, nxt, _full((8, CW))],
        out_specs=_tok(CW),
        out_shape=SDS((B, S, CW), F32),
        compiler_params=_cp(("arbitrary", "arbitrary")),
    )(p, p, p, cw8)


def _conv_bwd(p, dcv, cw8, nct):
    B, S, _ = p.shape
    nt = S // TB

    def body(p_ref, pp_ref, pn_ref, d_ref, dp_ref, dn_ref, w_ref, o_ref, dw_ref):
        j = pl.program_id(1)
        at_start = jnp.logical_or(j == 0, j == nct)
        at_end = jnp.logical_or(j == nct - 1, j == nt - 1)
        cb, cc, cx = p_ref[:, 0:256], p_ref[:, 256:512], p_ref[:, 512:768]
        z = cc * cx
        zm = _shift_dn(z, pp_ref[7:8, 256:512] * pp_ref[7:8, 512:768], at_start)
        zp = _shift_up(z, pn_ref[0:1, 256:512] * pn_ref[0:1, 512:768], at_end)
        d = d_ref[...]
        e = d * cb
        em = _shift_dn(e, dp_ref[7:8, :] * pp_ref[7:8, 0:256], at_start)
        ep = _shift_up(e, dn_ref[0:1, :] * pn_ref[0:1, 0:256], at_end)
        dz = e * w_ref[1:2, :] + ep * w_ref[0:1, :] + em * w_ref[2:3, :]
        o_ref[:, 0:256] = d * (zm * w_ref[0:1, :] + z * w_ref[1:2, :] + zp * w_ref[2:3, :])
        o_ref[:, 256:512] = dz * cx
        o_ref[:, 512:768] = dz * cc
        dw_ref[...] = jnp.zeros_like(dw_ref)
        dw_ref[0:1, :] = _colsum(e * zm)
        dw_ref[1:2, :] = _colsum(e * z)
        dw_ref[2:3, :] = _colsum(e * zp)

    prev, nxt = _halo_specs(3 * CW, 1, S)
    dprev, dnxt = _halo_specs(CW, 0, S)
    return pl.pallas_call(
        body, name="conv_bwd", grid=(B, nt),
        in_specs=[_tok(3 * CW, 1), prev, nxt, _tok(CW), dprev, dnxt, _full((8, CW))],
        out_specs=[_tok(3 * CW), BS((None, None, 8, CW), lambda b, j: (b, j, 0, 0))],
        out_shape=[SDS((B, S, 3 * CW), F32), SDS((B, nt, 8, CW), F32)],
        compiler_params=_cp(("arbitrary", "arbitrary")),
    )(p, p, p, dcv, dcv, dcv, cw8)


HW = SGRP * SST


def _rev_chunk(k, nk, ncc):
    return jnp.where(k < ncc, ncc - 1 - k, nk - 1 - k + ncc)


def _scan_perm(B):
    n = B * TC
    pm = np.zeros((2 * n, 2 * n), np.float32)
    for t in range(TC):
        for e in range(B):
            pm[t * 2 * B + e, e * TC + t] = 1.0
            pm[t * 2 * B + B + e, n + e * TC + (TC - 1 - t)] = 1.0
    return pm


def _scan_drive(uf_ref, ub_ref, pm_ref, b4_ref, fwd, dbuf, n):
    u_tok = jnp.concatenate([uf_ref[...].reshape(n, SW), ub_ref[...].reshape(n, SW)], axis=0).astype(MXU)
    u_tm = _dot(pm_ref[...], u_tok).astype(MXU)
    d4 = _dot(u_tm, b4_ref[...])
    dbuf[...] = jnp.where(fwd, d4[:, :2 * HW], d4[:, 2 * HW:])
    return u_tm


def _scan_fwd(p, pm, pmt, b4, a, c2, Lc, ex=None):
    B, S, _ = p.shape
    nch, n = 2 * B, B * TC
    nk, ncc = S // TC, Lc // TC
    ucol = (INW - SW) // SW

    def body(uf_ref, ub_ref, pm_ref, pmt_ref, b4_ref, a_ref, c2_ref, yf_ref, yb_ref, hb_ref, hcar, dbuf, hbuf):
        @pl.when(pl.program_id(0) == 0)
        def _():
            hcar[...] = jnp.zeros_like(hcar)

        fwd = (lax.broadcasted_iota(jnp.int32, (2 * n, 1), 0) % nch) < B
        _scan_drive(uf_ref, ub_ref, pm_ref, b4_ref, fwd, dbuf, n)
        hb_ref[...] = hcar[...]
        ar, ai = a_ref[:, :HW], a_ref[:, HW:]

        def step(t, h):
            hr, hi = h
            r0 = pl.multiple_of(t * nch, nch)
            nr = ar * hr - ai * hi + dbuf[pl.ds(r0, nch), :HW]
            ni = ar * hi + ai * hr + dbuf[pl.ds(r0, nch), HW:]
            hbuf[pl.ds(r0, nch), :HW] = nr
            hbuf[pl.ds(r0, nch), HW:] = ni
            return nr, ni

        hr, hi = lax.fori_loop(0, TC, step, (hcar[:, :HW], hcar[:, HW:]), unroll=4)
        hcar[:, :HW] = hr
        hcar[:, HW:] = hi
        h_tok = _dot(pmt_ref[...], hbuf[...].astype(MXU)).astype(MXU)
        yf_ref[...] = _dot(h_tok[:n], c2_ref[:, :SW]).reshape(B, TC, SW)
        yb_ref[...] = _dot(h_tok[n:], c2_ref[:, SW:]).reshape(B, TC, SW)

    return _call_with(
        ex, body, "scan_fwd", (nk,),
        [BS((B, TC, SW), lambda k: (0, k, ucol)),
         BS((B, TC, SW), lambda k: (0, _rev_chunk(k, nk, ncc), ucol)),
         _full((2 * n, 2 * n)), _full((2 * n, 2 * n)),
         _full((SW, 4 * HW)), _full((nch, 2 * HW)), _full((2 * HW, 2 * SW))],
        [BS((B, TC, SW), lambda k: (0, k, 0)),
         BS((B, TC, SW), lambda k: (0, _rev_chunk(k, nk, ncc), 0)),
         BS((None, nch, 2 * HW), lambda k: (k, 0, 0))],
        [SDS((B, S, SW), F32), SDS((B, S, SW), F32), SDS((nk, nch, 2 * HW), F32)],
        [p, p, pm, pmt, b4, a, c2],
        [pltpu.VMEM((nch, 2 * HW), F32), pltpu.VMEM((2 * n, 2 * HW), F32), pltpu.VMEM((2 * n, 2 * HW), F32)],
        _cp(("arbitrary",), 48))


def _scan_bwd(p, dy, hb, pm, pmt, b4, a, c2, Lc, ex=None):
    B, S, _ = p.shape
    nch, n = 2 * B, B * TC
    nk, ncc = S // TC, Lc // TC
    ucol = (INW - SW) // SW
    rows = 2 * n

    def body(uf_ref, ub_ref, dyf_ref, dyb_ref, hb_ref, pm_ref, pmt_ref, b4_ref, a_ref, c2_ref,
             duf_ref, dub_ref, db4_ref, da_ref, dc2_ref, gcar, dbuf, hbuf, gbuf):
        @pl.when(pl.program_id(0) == 0)
        def _():
            gcar[...] = jnp.zeros_like(gcar)
            db4_ref[...] = jnp.zeros_like(db4_ref)
            da_ref[...] = jnp.zeros_like(da_ref)
            dc2_ref[...] = jnp.zeros_like(dc2_ref)

        fwd = (lax.broadcasted_iota(jnp.int32, (rows, 1), 0) % nch) < B
        u_tm = _scan_drive(uf_ref, ub_ref, pm_ref, b4_ref, fwd, dbuf, n)
        ar, ai = a_ref[:, :HW], a_ref[:, HW:]
        hbuf[0:nch, :] = hb_ref[...]

        def fstep(t, h):
            hr, hi = h
            r0 = pl.multiple_of(t * nch, nch)
            nr = ar * hr - ai * hi + dbuf[pl.ds(r0, nch), :HW]
            ni = ar * hi + ai * hr + dbuf[pl.ds(r0, nch), HW:]
            hbuf[pl.ds(r0 + nch, nch), :HW] = nr
            hbuf[pl.ds(r0 + nch, nch), HW:] = ni
            return nr, ni

        lax.fori_loop(0, TC, fstep, (hb_ref[:, :HW], hb_ref[:, HW:]), unroll=4)

        dy_tok = jnp.concatenate([dyf_ref[...].reshape(n, SW), dyb_ref[...].reshape(n, SW)], axis=0).astype(MXU)
        dy_tm = _dot(pm_ref[...], dy_tok)
        dy2 = jnp.concatenate([jnp.where(fwd, dy_tm, 0.0), jnp.where(fwd, 0.0, dy_tm)], axis=1).astype(MXU)
        gbuf[...] = _dot_nt(dy2, c2_ref[...])
        dc2_ref[...] += _dot_tn(hbuf[nch:, :].astype(MXU), dy2)

        def bstep(i, cary):
            cr, ci = cary
            r0 = pl.multiple_of((TC - 1 - i) * nch, nch)
            gr = gbuf[pl.ds(r0, nch), :HW] + cr
            gi = gbuf[pl.ds(r0, nch), HW:] + ci
            gbuf[pl.ds(r0, nch), :HW] = gr
            gbuf[pl.ds(r0, nch), HW:] = gi
            return ar * gr + ai * gi, ar * gi - ai * gr

        cr, ci = lax.fori_loop(0, TC, bstep, (gcar[:, :HW], gcar[:, HW:]), unroll=4)
        gcar[:, :HW] = cr
        gcar[:, HW:] = ci

        gr = gbuf[:, :HW].reshape(TC, nch, HW)
        gi = gbuf[:, HW:].reshape(TC, nch, HW)
        hpr = hbuf[0:rows, :HW].reshape(TC, nch, HW)
        hpi = hbuf[0:rows, HW:].reshape(TC, nch, HW)
        da_ref[:, :HW] += jnp.sum(gr * hpr + gi * hpi, axis=0)
        da_ref[:, HW:] += jnp.sum(gi * hpr - gr * hpi, axis=0)

        g = gbuf[...]
        gm = g.astype(MXU)
        dd4 = jnp.concatenate([jnp.where(fwd, g, 0.0), jnp.where(fwd, 0.0, g)], axis=1).astype(MXU)
        db4_ref[...] += _dot_tn(u_tm, dd4)
        g_tok = _dot(pmt_ref[...], gm).astype(MXU)
        duf_ref[...] = _dot_nt(g_tok[:n], b4_ref[:, :2 * HW]).reshape(B, TC, SW)
        dub_ref[...] = _dot_nt(g_tok[n:], b4_ref[:, 2 * HW:]).reshape(B, TC, SW)

    fwd_blk = lambda col: BS((B, TC, SW), lambda k: (0, nk - 1 - k, col))
    bwd_blk = lambda col: BS((B, TC, SW), lambda k: (0, _rev_chunk(nk - 1 - k, nk, ncc), col))
    return _call_with(
        ex, body, "scan_bwd", (nk,),
        [fwd_blk(ucol), bwd_blk(ucol), fwd_blk(0), bwd_blk(0),
         BS((None, nch, 2 * HW), lambda k: (nk - 1 - k, 0, 0)),
         _full((rows, rows)), _full((rows, rows)),
         _full((SW, 4 * HW)), _full((nch, 2 * HW)), _full((2 * HW, 2 * SW))],
        [fwd_blk(0), bwd_blk(0), _full((SW, 4 * HW)), _full((nch, 2 * HW)), _full((2 * HW, 2 * SW))],
        [SDS((B, S, SW), F32), SDS((B, S, SW), F32), SDS((SW, 4 * HW), F32), SDS((nch, 2 * HW), F32),
         SDS((2 * HW, 2 * SW), F32)],
        [p, p, dy, dy, hb, pm, pmt, b4, a, c2],
        [pltpu.VMEM((nch, 2 * HW), F32), pltpu.VMEM((rows, 2 * HW), F32),
         pltpu.VMEM((rows + nch, 2 * HW), F32), pltpu.VMEM((rows, 2 * HW), F32)],
        _cp(("arbitrary",), 56))


def _post_mix(o, cv, yf, yb, p, h, modtok, ng, dsk, wg, bg, wo, nct, ex=None):
    B, S, _ = h.shape

    def body(o_ref, cv_ref, yf_ref, yb_ref, u_ref, h_ref, mod_ref, g_ref, dsk_ref, wg_ref, bg_ref, wo_ref,
             h1_ref, mix_ref, m_ref, y_ref):
        y = yf_ref[...] + yb_ref[...] + dsk_ref[...] * u_ref[...]
        g = _gelu(y)
        s = g * _sigmoid(_dot(g.astype(MXU), wg_ref[...]) + bg_ref[...])
        mix_ref[:, 0:AW] = o_ref[...]
        mix_ref[:, AW:AW + CW] = cv_ref[...].astype(MXU)
        mix_ref[:, AW + CW:] = s.astype(MXU)
        m = _dot(mix_ref[...], wo_ref[...])
        n, _ = _rms(m)
        h1_ref[...] = h_ref[...] + mod_ref[2:3, :] * (n * g_ref[1:2, :])
        m_ref[...] = m
        y_ref[...] = y

    return _call_with(
        ex, body, "post_mix", (B, S // TB),
        [_tok(AW), _tok(CW), _tok(SW), _tok(SW), _tok(SW, (INW - SW) // SW), _tok(D), _mod_spec(nct),
         _full((8, D)), _full((1, SW)), _full((SW, SW)), _full((1, SW)), _full((D, D))],
        [_tok(D), _tok(D), _tok(D), _tok(SW)],
        [SDS((B, S, D), F32), SDS((B, S, D), MXU), SDS((B, S, D), F32), SDS((B, S, SW), F32)],
        [o, cv, yf, yb, p, h, modtok, ng, dsk, wg, bg, wo], [], _cp(("arbitrary", "arbitrary"), 40))


def _post_bwd(dh1, m, y, p, modtok, ng, dsk, wg, bg, wo, nct):
    B, S, _ = m.shape

    def body(dh_ref, m_ref, y_ref, u_ref, mod_ref, g_ref, dsk_ref, wg_ref, bg_ref, wo_ref,
             dm_ref, da_ref, dc_ref, dy_ref, du_ref, vec_ref, vec2_ref, dwg_ref):
        @pl.when(jnp.logical_and(pl.program_id(0) == 0, pl.program_id(1) == 0))
        def _():
            dwg_ref[...] = jnp.zeros_like(dwg_ref)

        n, r = _rms(m_ref[...])
        dh1 = dh_ref[...]
        gpm = g_ref[1:2, :]
        dr = dh1 * mod_ref[2:3, :]
        vec_ref[...] = jnp.zeros_like(vec_ref)
        vec_ref[0:1, :] = _colsum(dh1 * (n * gpm))
        vec_ref[1:2, :] = _colsum(dr * n)
        dm = _rms_bwd(dr * gpm, n, r).astype(MXU)
        dm_ref[...] = dm
        dmix = _dot_nt(dm, wo_ref[...])
        for g, t in enumerate(_to_groups([dmix[:, k * 128:(k + 1) * 128] for k in range(AW // 128)])):
            da_ref[g] = t
        dc_ref[...] = dmix[:, AW:AW + CW]
        ds = dmix[:, AW + CW:]
        yv = y_ref[...]
        g = _gelu(yv)
        gb = g.astype(MXU)
        sg = _sigmoid(_dot(gb, wg_ref[...]) + bg_ref[...])
        dv = ds * g * sg * (1.0 - sg)
        dvb = dv.astype(MXU)
        dg = ds * sg + _dot_nt(dvb, wg_ref[...])
        dwg_ref[...] += _dot_tn(gb, dvb)
        dy = dg * _gelu_grad(yv)
        dy_ref[...] = dy
        du_ref[...] = dy * dsk_ref[...]
        vec2_ref[...] = jnp.zeros_like(vec2_ref)
        vec2_ref[0:1, :] = _colsum(dy * u_ref[...])
        vec2_ref[1:2, :] = _colsum(dv)

    nt = S // TB
    return pl.pallas_call(
        body, name="post_bwd", grid=(B, nt),
        in_specs=[_tok(D), _tok(D), _tok(SW), _tok(SW, (INW - SW) // SW), _mod_spec(nct), _full((8, D)),
                  _full((1, SW)), _full((SW, SW)), _full((1, SW)), _full((D, D))],
        out_specs=[_tok(D), _grp(), _tok(CW), _tok(SW), _tok(SW),
                   BS((None, None, 8, D), lambda b, j: (b, j, 0, 0)),
                   BS((None, None, 8, SW), lambda b, j: (b, j, 0, 0)), _full((SW, SW))],
        out_shape=[SDS((B, S, D), MXU), SDS((B, QPK, S, 128), F32), SDS((B, S, CW), F32), SDS((B, S, SW), F32),
                   SDS((B, S, SW), F32), SDS((B, nt, 8, D), F32), SDS((B, nt, 8, SW), F32), SDS((SW, SW), F32)],
        compiler_params=_cp(("arbitrary", "arbitrary"), 40),
    )(dh1, m, y, p, modtok, ng, dsk, wg, bg, wo)


def _mlp_fwd(h1, modtok, ng, w1, w2, nct, ex=None):
    B, S, _ = h1.shape
    assert w1.shape == (DFF // TF, D, TF)

    def body(h_ref, mod_ref, g_ref, w1_ref, w2_ref, h2_ref, a2_ref, z_ref, f_ref):
        n, _ = _rms(h_ref[...])
        a2 = ((n * g_ref[2:3, :]) * (1.0 + mod_ref[4:5, :]) + mod_ref[3:4, :]).astype(MXU)
        a2_ref[...] = a2
        ff = jnp.zeros((TB, D), F32)
        for c in range(DFF // TF):
            cs = slice(c * TF, (c + 1) * TF)
            z = _dot(a2, w1_ref[c])
            z_ref[:, cs] = z.astype(MXU)
            r = jnp.maximum(z, 0.0)
            ff = ff + _dot((r * r).astype(MXU), w2_ref[cs, :])
        f_ref[...] = ff
        n, _ = _rms(ff)
        h2_ref[...] = h_ref[...] + mod_ref[5:6, :] * (n * g_ref[3:4, :])

    return _call_with(
        ex, body, "mlp_fwd", (B, S // TB),
        [_tok(D), _mod_spec(nct), _full((8, D)), _resident((DFF // TF, D, TF)), _resident((DFF, D))],
        [_tok(D), _tok(D), _tok(DFF), _tok(D)],
        [SDS((B, S, D), F32), SDS((B, S, D), MXU), SDS((B, S, DFF), MXU), SDS((B, S, D), F32)],
        [h1, modtok, ng, w1, w2], [], _cp(("arbitrary", "arbitrary"), 48))


def _mlp_bwd(dh2, h1, f, z, modtok, ng, w1, w2, nct, ex=None):
    B, S, _ = h1.shape

    def body(dh_ref, h_ref, f_ref, z_ref, mod_ref, g_ref, w1_ref, w2_ref, dh1_ref, df_ref, dz_ref, vec_ref):
        n, r = _rms(f_ref[...])
        dh2 = dh_ref[...]
        gp = g_ref[3:4, :]
        dr = dh2 * mod_ref[5:6, :]
        vec_ref[...] = jnp.zeros_like(vec_ref)
        vec_ref[3:4, :] = _colsum(dh2 * (n * gp))
        vec_ref[4:5, :] = _colsum(dr * n)
        df = _rms_bwd(dr * gp, n, r).astype(MXU)
        df_ref[...] = df
        da = jnp.zeros((TB, D), F32)
        for c in range(DFF // TF):
            cs = slice(c * TF, (c + 1) * TF)
            dr2 = _dot_nt(df, w2_ref[cs, :])
            dz = (dr2 * 2.0 * jnp.maximum(z_ref[:, cs].astype(F32), 0.0)).astype(MXU)
            dz_ref[:, cs] = dz
            da = da + _dot_nt(dz, w1_ref[c])
        n, r = _rms(h_ref[...])
        g = g_ref[2:3, :]
        sc1 = 1.0 + mod_ref[4:5, :]
        vec_ref[0:1, :] = _colsum(da)
        vec_ref[1:2, :] = _colsum(da * (n * g))
        vec_ref[2:3, :] = _colsum(da * sc1 * n)
        dh1_ref[...] = dh2 + _rms_bwd(da * sc1 * g, n, r)

    return _call_with(
        ex, body, "mlp_bwd", (B, S // TB),
        [_tok(D), _tok(D), _tok(D), _tok(DFF), _mod_spec(nct), _full((8, D)),
         _resident((DFF // TF, D, TF)), _resident((DFF, D))],
        [_tok(D), _tok(D), _tok(DFF), BS((None, None, 8, D), lambda b, j: (b, j, 0, 0))],
        [SDS((B, S, D), F32), SDS((B, S, D), MXU), SDS((B, S, DFF), MXU), SDS((B, S // TB, 8, D), F32)],
        [dh2, h1, f, z, modtok, ng, w1, w2], [], _cp(("arbitrary", "arbitrary"), 56))


def _loss(h, target, Lc):
    B, S, _ = h.shape
    nt, nct = S // TB, Lc // TB

    def body(h_ref, t_ref, dh_ref, l_ref):
        lat = pl.program_id(1) >= nct
        err = jnp.where(lat, h_ref[...] - t_ref[...], 0.0)
        dh_ref[...] = err * (1.0 / D)
        l_ref[...] = jnp.broadcast_to(jnp.sum(err * err, keepdims=True), (8, 128))

    return pl.pallas_call(
        body, name="loss", grid=(B, nt),
        in_specs=[_tok(D), BS((None, TB, D), lambda b, j: (b, jnp.maximum(j - nct, 0), 0))],
        out_specs=[_tok(D), BS((None, None, 8, 128), lambda b, j: (b, j, 0, 0))],
        out_shape=[SDS((B, S, D), F32), SDS((B, nt, 8, 128), F32)],
        compiler_params=_cp(("arbitrary", "arbitrary")),
    )(h, target)


def _mm_tn(a, b, name, relu2=False):
    T, M = a.shape
    N = b.shape[1]
    def blk(n):
        return max(b for b in range(128, 1025, 128) if n % b == 0)

    bm, bn = blk(M), blk(N)
    tk = next(t for t in (1536, 1024, 512, 256, T) if T % t == 0)
    nkk = T // tk

    def body(a_ref, b_ref, o_ref, om_ref):
        k = pl.program_id(2)

        @pl.when(k == 0)
        def _():
            o_ref[...] = jnp.zeros_like(o_ref)

        av = a_ref[...]
        if relu2:
            r = jnp.maximum(av.astype(F32), 0.0)
            av = (r * r).astype(MXU)
        o_ref[...] += _dot_tn(av, b_ref[...])

        @pl.when(k == nkk - 1)
        def _():
            om_ref[...] = o_ref[...].astype(MXU)

    out = BS((bm, bn), lambda i, j, k: (i, j))
    return pl.pallas_call(
        body, name=name, grid=(M // bm, N // bn, nkk),
        in_specs=[BS((tk, bm), lambda i, j, k: (k, i)), BS((tk, bn), lambda i, j, k: (k, j))],
        out_specs=[out, out], out_shape=[SDS((M, N), F32), SDS((M, N), MXU)],
        compiler_params=_cp(("arbitrary", "arbitrary", "arbitrary"), 56),
    )(a, b)


_PAD_ROWS = 64


def _block_rows(R):
    for br in (256, _PAD_ROWS):
        if R % br == 0:
            return br
    raise ValueError(f"row count {R} is not a multiple of {_PAD_ROWS}")


def _adamw(w, ga, gb, m, v, name):
    R, C = w.shape
    br = _block_rows(R) if C <= _LANES else _PAD_ROWS

    def body(w_ref, ga_ref, gb_ref, m_ref, v_ref, g_out, d_out, m_out, v_out):
        g = ga_ref[...] + gb_ref[...]
        m2 = B1 * m_ref[...] + (1.0 - B1) * g
        v2 = B2 * v_ref[...] + (1.0 - B2) * (g * g)
        m_hat = m2 / (1.0 - B1 ** STEP)
        v_hat = v2 / (1.0 - B2 ** STEP)
        g_out[...] = g
        d_out[...] = -LR * (m_hat / (jnp.sqrt(v_hat) + AEPS) + WD * w_ref[...])
        m_out[...] = m2
        v_out[...] = v2

    own = BS((br, C), lambda i: (i, 0))
    return pl.pallas_call(
        body, name=name, grid=(R // br,),
        in_specs=[own] * 5, out_specs=[own] * 4, out_shape=[SDS((R, C), F32)] * 4,
        compiler_params=_cp(("arbitrary",)),
    )(w, ga, gb, m, v)


def _my_xyc():
    return lax.axis_index("x"), lax.axis_index("y"), lax.axis_index("c")


def _chip_peers(x, y):
    return [(1 - x, y), (x, 1 - y), (1 - x, 1 - y)]


class _Gather:
    def __init__(self, shards):
        self.args = list(shards)
        self.n = len(self.args)
        self.specs = [ANY] * self.n
        self.out_shape = [SDS((NCHIP,) + a.shape, a.dtype) for a in self.args]
        self.sems = [pltpu.SemaphoreType.DMA((3 * self.n,)), pltpu.SemaphoreType.DMA((3 * self.n,)),
                     pltpu.SemaphoreType.DMA((self.n,))]

    def _copies(self, ins, outs, ssem, rsem, lsem):
        x, y, c = _my_xyc()
        mine, sends, recvs = [], [], []
        for k, (s_ref, o_ref) in enumerate(zip(ins, outs)):
            mine.append(pltpu.make_async_copy(s_ref, o_ref.at[2 * x + y], lsem.at[k]))
            for j, (px, py) in enumerate(_chip_peers(x, y)):
                sems = dict(send_sem=ssem.at[3 * k + j], recv_sem=rsem.at[3 * k + j], device_id=(px, py, c),
                            device_id_type=MESH)
                sends.append(pltpu.make_async_remote_copy(src_ref=s_ref, dst_ref=o_ref.at[2 * x + y], **sems))
                recvs.append(pltpu.make_async_remote_copy(src_ref=s_ref, dst_ref=o_ref.at[2 * px + py], **sems))
        return mine, sends, recvs

    def start(self, ins, outs, sems):
        mine, sends, _ = self._copies(ins, outs, *sems)
        for cp in mine + sends:
            cp.start()

    def wait(self, ins, outs, sems):
        mine, sends, recvs = self._copies(ins, outs, *sems)
        for cp in recvs:
            cp.wait_recv()
        for cp in sends:
            cp.wait_send()
        for cp in mine:
            cp.wait()


class _Gather2(_Gather):
    def __init__(self, shards):
        super().__init__(shards)
        self.sems = self.sems + [pltpu.SemaphoreType.DMA((3 * self.n,)), pltpu.SemaphoreType.DMA((3 * self.n,))]

    def _copies2(self, ins, outs, ssem, rsem, lsem, fsem, gsem):
        x, y, c = _my_xyc()
        mine, sends, recvs, passes, got = [], [], [], [], []
        for k, (s_ref, o_ref) in enumerate(zip(ins, outs)):
            half = s_ref.shape[0] // 2
            my_rows, sib_rows = pl.ds(c * half, half), pl.ds((1 - c) * half, half)
            mine.append(pltpu.make_async_copy(s_ref, o_ref.at[2 * x + y], lsem.at[k]))
            for j, (px, py) in enumerate(_chip_peers(x, y)):
                ici = dict(send_sem=ssem.at[3 * k + j], recv_sem=rsem.at[3 * k + j], device_id=(px, py, c),
                           device_id_type=MESH)
                d2d = dict(send_sem=fsem.at[3 * k + j], recv_sem=gsem.at[3 * k + j], device_id=(x, y, 1 - c),
                           device_id_type=MESH)
                landed = o_ref.at[2 * px + py, my_rows]
                sends.append(pltpu.make_async_remote_copy(src_ref=s_ref.at[my_rows],
                                                          dst_ref=o_ref.at[2 * x + y, my_rows], **ici))
                recvs.append(pltpu.make_async_remote_copy(src_ref=s_ref.at[my_rows], dst_ref=landed, **ici))
                passes.append(pltpu.make_async_remote_copy(src_ref=landed, dst_ref=landed, **d2d))
                got.append(pltpu.make_async_remote_copy(src_ref=landed, dst_ref=o_ref.at[2 * px + py, sib_rows],
                                                        **d2d))
        return mine, sends, recvs, passes, got

    def start(self, ins, outs, sems):
        mine, sends, _, _, _ = self._copies2(ins, outs, *sems)
        for cp in mine + sends:
            cp.start()

    def wait(self, ins, outs, sems):
        mine, sends, recvs, passes, got = self._copies2(ins, outs, *sems)
        for landed, onward in zip(recvs, passes):
            landed.wait_recv()
            onward.start()
        for cp in got:
            cp.wait_recv()
        for cp in sends + passes:
            cp.wait_send()
        for cp in mine:
            cp.wait()


class _Scatter:
    def __init__(self, sends, cols=None):
        self.args = list(sends)
        self.n = len(self.args)
        self.cols = list(cols) if cols is not None else [None] * self.n
        self.specs = [ANY] * self.n
        self.out_shape = [SDS((3,) + a.shape[1:] if cw is None else (3, a.shape[0], cw), a.dtype)
                          for a, cw in zip(self.args, self.cols)]
        self.sems = [pltpu.SemaphoreType.DMA((3 * self.n,)), pltpu.SemaphoreType.DMA((3 * self.n,))]

    def _copies(self, ins, outs, ssem, rsem):
        x, y, c = _my_xyc()

        def piece(k, q):
            cw = self.cols[k]
            return ins[k].at[q] if cw is None else ins[k].at[:, pl.ds(pl.multiple_of(q * cw, 128), cw)]

        return [pltpu.make_async_remote_copy(
            src_ref=piece(k, 2 * px + py), dst_ref=outs[k].at[j], send_sem=ssem.at[3 * k + j],
            recv_sem=rsem.at[3 * k + j], device_id=(px, py, c), device_id_type=MESH)
            for k in range(self.n) for j, (px, py) in enumerate(_chip_peers(x, y))]

    def start(self, ins, outs, sems):
        for cp in self._copies(ins, outs, *sems):
            cp.start()

    def wait(self, ins, outs, sems):
        for cp in self._copies(ins, outs, *sems):
            cp.wait()


class _Swap(_Scatter):
    def __init__(self, arrays):
        self.args = list(arrays)
        self.n = len(self.args)
        self.specs = [ANY] * self.n
        self.out_shape = [SDS(a.shape, a.dtype) for a in self.args]
        self.sems = [pltpu.SemaphoreType.DMA((self.n,)), pltpu.SemaphoreType.DMA((self.n,))]

    def _copies(self, ins, outs, ssem, rsem):
        x, y, c = _my_xyc()
        return [pltpu.make_async_remote_copy(src_ref=s_ref, dst_ref=o_ref, send_sem=ssem.at[k], recv_sem=rsem.at[k],
                                             device_id=(x, y, 1 - c), device_id_type=MESH)
                for k, (s_ref, o_ref) in enumerate(zip(ins, outs))]


def _exchange_call(ex, name):
    n = ex.n

    def body(*refs):
        ins, outs, sems = refs[:n], refs[n:2 * n], refs[2 * n:]
        ex.start(ins, outs, sems)
        ex.wait(ins, outs, sems)

    return pl.pallas_call(body, name=name, in_specs=ex.specs, out_specs=ex.specs, out_shape=ex.out_shape,
                          scratch_shapes=ex.sems)(*ex.args)


def _carried(ex, refs, n_in, n_out):
    n = ex.n
    ins, cin = refs[:n_in], refs[n_in:n_in + n]
    outs, cout = refs[n_in + n:n_in + n + n_out], refs[n_in + n + n_out:n_in + 2 * n + n_out]
    rest = refs[n_in + 2 * n + n_out:]
    nsem = len(ex.sems)
    return list(ins) + list(outs) + list(rest[:len(rest) - nsem]), (cin, cout, rest[len(rest) - nsem:])


def _gather_all(v, name):
    R, C = v.shape

    def body(s_ref, o_ref, ssem, rsem, lsem):
        x, y, c = _my_xyc()
        me = 4 * x + 2 * y + c
        mine = pltpu.make_async_copy(s_ref, o_ref.at[me], lsem)
        mine.start()

        def peer(j):
            fx, fy, fc = (j >> 2) & 1, (j >> 1) & 1, j & 1
            return (x ^ fx, y ^ fy, c ^ fc)

        cps = []
        for j in range(1, 8):
            cps.append(pltpu.make_async_remote_copy(
                src_ref=s_ref, dst_ref=o_ref.at[me], send_sem=ssem.at[j - 1], recv_sem=rsem.at[j - 1],
                device_id=peer(j), device_id_type=MESH))
            cps[-1].start()
        for j in range(1, 8):
            px, py, pc = peer(j)
            pltpu.make_async_remote_copy(
                src_ref=s_ref, dst_ref=o_ref.at[4 * px + 2 * py + pc], send_sem=ssem.at[j - 1],
                recv_sem=rsem.at[j - 1], device_id=peer(j), device_id_type=MESH).wait_recv()
        for cp in cps:
            cp.wait_send()
        mine.wait()

    return pl.pallas_call(
        body, name=name, in_specs=[ANY], out_specs=ANY, out_shape=SDS((8, R, C), v.dtype),
        scratch_shapes=[pltpu.SemaphoreType.DMA((7,)), pltpu.SemaphoreType.DMA((7,)), pltpu.SemaphoreType.DMA],
    )(v)


def _sum_slots(own, slots, name):
    n, R, C = slots.shape
    br = _block_rows(R)

    def body(*refs):
        o_ref = refs[-1]
        if own is None:
            acc = refs[0][0].astype(F32)
            first = 1
            s_ref = refs[0]
        else:
            acc = refs[0][...]
            first = 0
            s_ref = refs[1]
        for k in range(first, n):
            acc = acc + s_ref[k].astype(F32)
        o_ref[...] = acc

    row = BS((br, C), lambda i: (i, 0))
    slab = BS((n, br, C), lambda i: (0, i, 0))
    ins, args = ([slab], [slots]) if own is None else ([row, slab], [own, slots])
    return pl.pallas_call(
        body, name=name, grid=(R // br,), in_specs=ins, out_specs=row, out_shape=SDS((R, C), F32),
        compiler_params=_cp(("arbitrary",)),
    )(*args)


def _rope_tables(L, Lc):
    n = jnp.arange(L)
    row = (n // GRID_W).astype(F32)
    col = (n % GRID_W).astype(F32)
    freqs = ROPE_BASE ** (-jnp.arange(16, dtype=F32) / 16)
    lane = jnp.arange(128)
    dd = lane % HD
    fr = freqs[dd % 16]
    ang = jnp.where(dd < 32, row[:, None], col[:, None]) * fr[None, :]
    sign = jnp.where((dd % 32) < 16, -1.0, 1.0)
    cos = jnp.concatenate([jnp.ones((Lc, 128), F32), jnp.cos(ang)], axis=0)
    sin = jnp.concatenate([jnp.zeros((Lc, 128), F32), jnp.sin(ang) * sign[None, :]], axis=0)
    return cos, sin


def _ssm_prep(lam_re, lam_im, log_dt, b_re, b_im, c_re, c_im, B):
    dt = jnp.exp(log_dt)[..., None]
    mag = jnp.exp(lam_re * dt)
    ar = mag * jnp.cos(lam_im * dt)
    ai = mag * jnp.sin(lam_im * dt)
    den = lam_re * lam_re + lam_im * lam_im
    kr = ((ar - 1.0) * lam_re + ai * lam_im) / den
    ki = (ai * lam_re - (ar - 1.0) * lam_im) / den
    bbr = kr[..., None] * b_re - ki[..., None] * b_im
    bbi = kr[..., None] * b_im + ki[..., None] * b_re
    eye = jnp.eye(SGRP, dtype=F32)

    def bblk(t):
        return jnp.einsum("gpi,gh->gihp", t, eye).reshape(SGRP * SCH, SGRP * SST)

    def cblk(t):
        return jnp.einsum("gip,gh->gphi", t, eye).reshape(SGRP * SST, SGRP * SCH)

    b4 = jnp.concatenate([bblk(bbr[0]), bblk(bbi[0]), bblk(bbr[1]), bblk(bbi[1])], axis=1)
    c2 = jnp.concatenate([jnp.concatenate([cblk(c_re[0]), -cblk(c_im[0])], axis=0),
                          jnp.concatenate([cblk(c_re[1]), -cblk(c_im[1])], axis=0)], axis=1)
    a2 = jnp.concatenate([ar.reshape(2, -1), ai.reshape(2, -1)], axis=1)
    a = jnp.repeat(a2, B, axis=0)
    return a, b4, c2


def _pad_rows(v, rows=8):
    return jnp.concatenate([v, jnp.zeros((rows - v.shape[0],) + v.shape[1:], v.dtype)], axis=0)


_BIG = ("w_ada", "w_in", "w_out", "w_mlp_in", "w_mlp_out")
_BIG_AXIS = {"w_ada": 2, "w_in": 2, "w_out": 1, "w_mlp_in": 2, "w_mlp_out": 1}


def _split_shards(g, axis):
    L, r, c = g.shape
    if axis == 2:
        return jnp.transpose(g.reshape(L, r, NCHIP, c // NCHIP), (2, 0, 1, 3))
    return jnp.transpose(g.reshape(L, NCHIP, r // NCHIP, c), (1, 0, 2, 3))


def _join_shards(s, axis):
    _, L, r, c = s.shape
    if axis == 2:
        return jnp.transpose(s, (1, 2, 0, 3)).reshape(L, r, NCHIP * c)
    return jnp.transpose(s, (1, 0, 2, 3)).reshape(L, NCHIP * r, c)


def _local_step(x, c, ctx, loss_target, P, ex):
    B, L, _ = x.shape
    Lc = ctx.shape[1]
    S = Lc + L
    nct = Lc // TB
    nt = S // TB
    nch = 2 * B
    cos, sin = _rope_tables(L, Lc)
    perm = _scan_perm(B)
    pm, pmt = jnp.asarray(perm, MXU), jnp.asarray(perm.T, MXU)

    c_act = jax.nn.silu(c)
    cc_act, silu_vjp = jax.vjp(jax.nn.silu, P["c_ctx"])
    cact8 = _pad_rows(jnp.concatenate([c_act, cc_act[None, :]], axis=0))
    h = jnp.concatenate([ctx, x], axis=1)

    ssm_names = ("ssm_lam_re", "ssm_lam_im", "ssm_log_dt", "ssm_b_re", "ssm_b_im", "ssm_c_re", "ssm_c_im")
    (a_all, b4_all, c2_all), prep_vjp = jax.vjp(jax.vmap(lambda *t: _ssm_prep(*t, B)), *[P[n] for n in ssm_names])
    b4m_all, c2m_all = b4_all.astype(MXU), c2_all.astype(MXU)
    ng_all = jnp.pad(P["norm_g"], ((0, 0), (0, 4), (0, 0)))
    cw8_all = jnp.pad(P["conv_w"], ((0, 0), (0, 8 - P["conv_w"].shape[1]), (0, 0)))
    wg_all = P["w_glu"].astype(MXU)

    saved = []
    for l in range(NLAYER):
        mod8 = _ada_fwd(cact8, ex.w(l, "w_ada"), P["b_ada"][l][None, :])
        mod6 = mod8.reshape(8, NMOD, D)
        modtok = jnp.stack([jnp.broadcast_to(mod6[B], (B, NMOD, D)), mod6[:B]], axis=1)
        modtok = jnp.pad(modtok, ((0, 0), (0, 0), (0, 2), (0, 0)))
        ng, cw8, a_s, b4m, c2m, wg = ng_all[l], cw8_all[l], a_all[l], b4m_all[l], c2m_all[l], wg_all[l]
        dsk = P["ssm_d"][l][None, :]
        bg = P["b_glu"][l][None, :]

        p, a = ex.pre_mix(l, h, modtok, ng, ex.w(l, "w_in"), nct)
        q4, kr = _rope_fwd(p, cos, sin)
        (o,) = ex.attn_fwd(l, P["attn_sink"][l], q4, kr, p, Lc)
        cv = _conv_fwd(p, cw8, nct)
        yf, yb, hb = ex.scan_fwd(l, p, pm, pmt, b4m, a_s, c2m, Lc)
        h1, mix, m, y = ex.post_mix(l, o, cv, yf, yb, p, h, modtok, ng, dsk, wg, bg, ex.w(l, "w_out"), nct)
        h2, a2, z, f = ex.mlp_fwd(l, h1, modtok, ng, ex.w(l, "w_mlp_in"), ex.w(l, "w_mlp_out"), nct)
        saved.append(dict(h=h, modtok=modtok, ng=ng, cw8=cw8, a_s=a_s, b4m=b4m, c2m=c2m, wg=wg, dsk=dsk, bg=bg,
                          p=p, a=a, q4=q4, kr=kr, hb=hb, mix=mix, m=m, y=y, h1=h1, a2=a2, z=z, f=f))
        h = h2

    dh, lpart = _loss(h, loss_target, Lc)
    loss_local = 0.5 / D * jnp.sum(lpart[:, :, 0, 0])

    R = {k: [None] * NLAYER for k in ("w_glu", "vec_p2", "scan", "dcw", "dsink", "vec", "b_ada")}
    dcact = jnp.zeros((8, D), F32)
    T = B * S
    for l in reversed(range(NLAYER)):
        sv = saved[l]
        modtok, ng = sv["modtok"], sv["ng"]
        dh1, df, dz, vec_m = ex.mlp_bwd(l, dh, sv["h1"], sv["f"], sv["z"], modtok, ng, ex.w(l, "w_mlp_in"),
                                        ex.w(l, "w_mlp_out"), nct)
        ex.submit(l, "w_mlp_out", *_mm_tn(sv["z"].reshape(T, DFF), df.reshape(T, D), "dw_mlp_out", relu2=True))
        ex.submit(l, "w_mlp_in", *_mm_tn(sv["a2"].reshape(T, D), dz.reshape(T, DFF), "dw_mlp_in"))
        dm, dattn, dcv, dy, du_skip, vec_p, vec_p2, dwg = _post_bwd(
            dh1, sv["m"], sv["y"], sv["p"], modtok, ng, sv["dsk"], sv["wg"], sv["bg"], ex.w(l, "w_out"), nct)
        ex.submit(l, "w_out", *_mm_tn(sv["mix"].reshape(T, D), dm.reshape(T, D), "dw_out"))
        R["w_glu"][l], R["vec_p2"][l] = dwg, vec_p2
        duf, dub, db4, da_s, dc2 = ex.scan_bwd(l, sv["p"], dy, sv["hb"], pm, pmt, sv["b4m"], sv["a_s"], sv["c2m"],
                                               Lc)
        R["scan"][l] = (da_s, db4, dc2)
        dconv, R["dcw"][l] = _conv_bwd(sv["p"], dcv, sv["cw8"], nct)
        dq_r, dk_r, dv, R["dsink"][l] = ex.attn_bwd(l, P["attn_sink"][l], sv["q4"], sv["kr"], sv["p"], dattn, Lc)
        dqk = _rope_bwd(dq_r, dk_r, cos, -sin)
        dh, vec_i, dp = _pre_bwd(dqk, dv, dconv, (du_skip, duf, dub), dh1, sv["h"], modtok, ng, ex.w(l, "w_in"),
                                 nct)
        ex.submit(l, "w_in", *_mm_tn(sv["a"].reshape(T, D), dp.reshape(T, INW), "dw_in"))
        vec = jnp.concatenate([vec_i, vec_p, vec_m], axis=2)
        R["vec"][l] = vec
        mod_rows = jnp.concatenate([vec[:, :, 0:2], vec[:, :, 8:9], vec[:, :, 16:18], vec[:, :, 19:20]], axis=2)
        dmod8 = jnp.concatenate([jnp.sum(mod_rows[:, nct:], axis=1).reshape(B, NMOD * D),
                                 jnp.sum(mod_rows[:, :nct], axis=(0, 1)).reshape(1, NMOD * D),
                                 jnp.zeros((7 - B, NMOD * D), F32)], axis=0)
        R["b_ada"][l] = dmod8
        dmod8m = dmod8.astype(MXU)
        ex.submit(l, "w_ada", *_mm_tn(cact8.astype(MXU), dmod8m, "dw_ada"))
        dcact = dcact + _ada_bwd_c(dmod8m, ex.w(l, "w_ada"))

    stk = lambda k: jnp.stack(R[k])
    vec, vec_p2 = jnp.sum(stk("vec"), axis=(1, 2)), jnp.sum(stk("vec_p2"), axis=(1, 2))
    grads = dict(zip(ssm_names, prep_vjp(tuple(jnp.stack([R["scan"][l][k] for l in range(NLAYER)])
                                               for k in range(3)))))
    norm_rows = jnp.concatenate([vec[:, 2:3], vec[:, 9:10], vec[:, 18:19], vec[:, 20:21]], axis=1)
    grads.update(w_glu=stk("w_glu"), ssm_d=vec_p2[:, 0], b_glu=vec_p2[:, 1], norm_g=norm_rows,
                 conv_w=jnp.sum(stk("dcw"), axis=(1, 2))[:, 0:P["conv_w"].shape[1]],
                 attn_sink=jnp.sum(stk("dsink")[..., 0], axis=1), b_ada=jnp.sum(stk("b_ada"), axis=1),
                 c_ctx=silu_vjp(dcact[B])[0])
    return loss_local, dh[:, Lc:, :], grads


_WEIGHTS = ["c_ctx", "w_ada", "b_ada", "norm_g", "w_in", "conv_w", "attn_sink", "ssm_lam_re", "ssm_lam_im",
            "ssm_log_dt", "ssm_b_re", "ssm_b_im", "ssm_c_re", "ssm_c_im", "ssm_d", "w_glu", "b_glu", "w_out",
            "w_mlp_in", "w_mlp_out"]
_SMALL = [n for n in _WEIGHTS if n not in _BIG]
_SMALL_SHARDED = {"norm_g": 2, "conv_w": 2, "w_glu": 1}
_LANES = 1024


_GROUPS = (("w_ada", "w_in", "w_out"), ("w_mlp_in", "w_mlp_out"))
_QUEUE_OF = {"w_ada": "narrow", "w_in": "narrow", "w_out": "out", "w_mlp_in": "mlp", "w_mlp_out": "mlp"}


class _Exchange:
    def __init__(self, W, q_me):
        self.W, self.q_me = W, q_me
        self.wall, self.own, self.recv, self.pending = {}, {}, {}, {}
        first = ("w_ada", "w_in")
        self._keep(0, first, _exchange_call(_Gather2([self._shard(0, n) for n in first]), "gather_big"))

    def _shard(self, l, n):
        return self.W[n][l].astype(MXU)

    def _keep(self, l, names, walls):
        for n, w in zip(names, walls):
            self.wall[l, n] = w

    def w(self, l, n):
        g = self.wall[l, n]
        if n in ("w_ada", "w_mlp_in"):
            return g
        return _join_shards(g[:, None], 2)[0] if n == "w_in" else g.reshape(-1, g.shape[-1])

    def _gathering(self, l, names, fn, *args):
        if l == NLAYER or not names:
            return fn(*args)[0]
        res, walls = fn(*args, ex=_Gather([self._shard(l, n) for n in names]))
        self._keep(l, names, walls)
        return res

    def pre_mix(self, l, *args):
        return self._gathering(l, ("w_out",) if l == 0 else (), _pre_mix, *args)

    def scan_fwd(self, l, *args):
        return self._gathering(l, ("w_mlp_in",) if l == 0 else (), _scan_fwd, *args)

    def post_mix(self, l, *args):
        return self._gathering(l, ("w_mlp_out",) if l == 0 else (), _post_mix, *args)

    def attn_fwd(self, l, *args):
        return self._gathering(l + 1, _GROUPS[0], _attn_fwd, *args)

    def mlp_fwd(self, l, *args):
        return self._gathering(l + 1, _GROUPS[1], _mlp_fwd, *args)

    def submit(self, l, n, g, gm):
        r, c = g.shape
        if _BIG_AXIS[n] == 2:
            cw = c // NCHIP
            self.own[l, n] = lax.dynamic_slice_in_dim(g, self.q_me * cw, cw, axis=1)
            if cw % 128 == 0:
                item = (gm, cw)
            else:
                item = (jnp.transpose(gm.reshape(r, NCHIP, cw), (1, 0, 2)), None)
        else:
            self.own[l, n] = lax.dynamic_slice_in_dim(g, self.q_me * (r // NCHIP), r // NCHIP, axis=0)
            item = (gm.reshape(NCHIP, r // NCHIP, c), None)
        self.pending.setdefault(_QUEUE_OF[n], []).append((l, n) + item)

    def _scatter_of(self, items):
        return _Scatter([it[2] for it in items], [it[3] for it in items])

    def _scattering(self, queue, fn, *args):
        items = self.pending.pop(queue, [])
        if not items:
            return fn(*args)[0]
        res, recvs = fn(*args, ex=self._scatter_of(items))
        for it, r in zip(items, recvs):
            self.recv[it[0], it[1]] = r
        return res

    def mlp_bwd(self, l, *args):
        return self._scattering("narrow", _mlp_bwd, *args)

    def scan_bwd(self, l, *args):
        return self._scattering("mlp", _scan_bwd, *args)

    def attn_bwd(self, l, *args):
        return self._scattering("out", _attn_bwd, *args)

    def finish(self):
        items = [it for queue in sorted(self.pending) for it in self.pending[queue]]
        self.pending = {}
        for it, r in zip(items, _exchange_call(self._scatter_of(items), "scatter_big")):
            self.recv[it[0], it[1]] = r
        part = [jnp.concatenate([_sum_slots(self.own[l, n], self.recv[l, n], "sum_chips") for l in range(NLAYER)],
                                axis=0) for n in _BIG]
        sib = _exchange_call(_Swap(part), "swap_big")
        return {n: (p, s) for n, p, s in zip(_BIG, part, sib)}


def _part_rows(shape):
    return -(-math.prod(shape) // (16 * _LANES)) * 16


def _flat_pad(parts):
    rows = []
    for p in parts:
        v = p.reshape(-1)
        pad = _part_rows(p.shape) * _LANES - v.shape[0]
        rows.append(jnp.concatenate([v, jnp.zeros((pad,), v.dtype)]).reshape(-1, _LANES))
    used = sum(r.shape[0] for r in rows)
    rows.append(jnp.zeros((-used % _PAD_ROWS, _LANES), rows[0].dtype))
    return jnp.concatenate(rows, axis=0)


def _unflat(flat, shapes):
    out, r0 = [], 0
    for s in shapes:
        nr = _part_rows(s)
        out.append(flat[r0:r0 + nr].reshape(-1)[:math.prod(s)].reshape(s))
        r0 += nr
    return out


def kernel(x, c, ctx, c_ctx, w_ada, b_ada, norm_g, w_in, conv_w, attn_sink, ssm_lam_re, ssm_lam_im, ssm_log_dt, ssm_b_re, ssm_b_im, ssm_c_re, ssm_c_im, ssm_d, w_glu, b_glu, w_out, w_mlp_in, w_mlp_out, loss_target, m_c_ctx, m_w_ada, m_b_ada, m_norm_g, m_w_in, m_conv_w, m_attn_sink, m_ssm_lam_re, m_ssm_lam_im, m_ssm_log_dt, m_ssm_b_re, m_ssm_b_im, m_ssm_c_re, m_ssm_c_im, m_ssm_d, m_w_glu, m_b_glu, m_w_out, m_w_mlp_in, m_w_mlp_out, v_c_ctx, v_w_ada, v_b_ada, v_norm_g, v_w_in, v_conv_w, v_attn_sink, v_ssm_lam_re, v_ssm_lam_im, v_ssm_log_dt, v_ssm_b_re, v_ssm_b_im, v_ssm_c_re, v_ssm_c_im, v_ssm_d, v_w_glu, v_b_glu, v_w_out, v_w_mlp_in, v_w_mlp_out):
    W = dict(c_ctx=c_ctx, w_ada=w_ada, b_ada=b_ada, norm_g=norm_g, w_in=w_in, conv_w=conv_w, attn_sink=attn_sink,
             ssm_lam_re=ssm_lam_re, ssm_lam_im=ssm_lam_im, ssm_log_dt=ssm_log_dt, ssm_b_re=ssm_b_re,
             ssm_b_im=ssm_b_im, ssm_c_re=ssm_c_re, ssm_c_im=ssm_c_im, ssm_d=ssm_d, w_glu=w_glu, b_glu=b_glu,
             w_out=w_out, w_mlp_in=w_mlp_in, w_mlp_out=w_mlp_out)
    M = dict(c_ctx=m_c_ctx, w_ada=m_w_ada, b_ada=m_b_ada, norm_g=m_norm_g, w_in=m_w_in, conv_w=m_conv_w,
             attn_sink=m_attn_sink, ssm_lam_re=m_ssm_lam_re, ssm_lam_im=m_ssm_lam_im, ssm_log_dt=m_ssm_log_dt,
             ssm_b_re=m_ssm_b_re, ssm_b_im=m_ssm_b_im, ssm_c_re=m_ssm_c_re, ssm_c_im=m_ssm_c_im, ssm_d=m_ssm_d,
             w_glu=m_w_glu, b_glu=m_b_glu, w_out=m_w_out, w_mlp_in=m_w_mlp_in, w_mlp_out=m_w_mlp_out)
    V = dict(c_ctx=v_c_ctx, w_ada=v_w_ada, b_ada=v_b_ada, norm_g=v_norm_g, w_in=v_w_in, conv_w=v_conv_w,
             attn_sink=v_attn_sink, ssm_lam_re=v_ssm_lam_re, ssm_lam_im=v_ssm_lam_im, ssm_log_dt=v_ssm_log_dt,
             ssm_b_re=v_ssm_b_re, ssm_b_im=v_ssm_b_im, ssm_c_re=v_ssm_c_re, ssm_c_im=v_ssm_c_im, ssm_d=v_ssm_d,
             w_glu=v_w_glu, b_glu=v_b_glu, w_out=v_w_out, w_mlp_in=v_w_mlp_in, w_mlp_out=v_w_mlp_out)
    q_me = 2 * lax.axis_index("x") + lax.axis_index("y")

    ex = _Exchange(W, q_me)
    P = {n: W[n] for n in _SMALL}
    ssh_names = list(_SMALL_SHARDED)
    ssh = _flat_pad([W[n] for n in ssh_names])
    (sall,) = _exchange_call(_Gather([ssh]), "gather_small")
    parts = [_unflat(sall[q], [W[n].shape for n in ssh_names]) for q in range(NCHIP)]
    for k, n in enumerate(ssh_names):
        P[n] = _join_shards(jnp.stack([parts[q][k] for q in range(NCHIP)]), _SMALL_SHARDED[n])

    loss_local, grad_x, G = _local_step(x, c, ctx, loss_target, P, ex)
    loss = lax.psum(loss_local, ("x", "y", "c"))

    sums = ex.finish()
    out = {}
    for n in _BIG:
        flat = lambda t: t.reshape(-1, t.shape[-1])
        res = _adamw(flat(W[n]), *sums[n], flat(M[n]), flat(V[n]), "adamw_" + n)
        out[n] = [t.reshape(W[n].shape) for t in res]

    gsmall = _flat_pad([G[n] for n in _SMALL]).astype(MXU)
    gsum = _sum_slots(None, _gather_all(gsmall, "gather_grads"), "sum_devices")
    gfull = dict(zip(_SMALL, _unflat(gsum, [G[n].shape for n in _SMALL])))
    for n, ax in _SMALL_SHARDED.items():
        width = W[n].shape[ax]
        gfull[n] = lax.dynamic_slice_in_dim(gfull[n], q_me * width, width, axis=ax)
    shapes = [W[n].shape for n in _SMALL]
    gflat = _flat_pad([gfull[n] for n in _SMALL])
    res = _adamw(_flat_pad([W[n] for n in _SMALL]), gflat, jnp.zeros_like(gflat),
                 _flat_pad([M[n] for n in _SMALL]), _flat_pad([V[n] for n in _SMALL]), "adamw_small")
    for k, t in enumerate(res):
        for n, piece in zip(_SMALL, _unflat(t, shapes)):
            out.setdefault(n, [None] * 4)[k] = piece

    return (loss, grad_x, *[out[n][0] for n in _WEIGHTS], *[out[n][1] for n in _WEIGHTS],
            *[out[n][2] for n in _WEIGHTS], *[out[n][3] for n in _WEIGHTS])
```

```python
import math

import jax
import jax.numpy as jnp
import numpy as np
from jax import lax
from jax.experimental import pallas as pl
from jax.experimental.pallas import tpu as pltpu

F32 = jnp.float32
MXU = jnp.bfloat16

D = 1024
DFF = 4096
NMOD = 6
EPS = 1e-6
HD = 64
NQ = 8
NKV = 2
QPK = 4
AW = 512
KVW = 128
WIN = 128
QB = 128
CW = 256
SW = 256
SGRP = 16
SCH = 16
SST = 64
INW = 1792
GRID_W = 64
ROPE_BASE = 10000.0
NEG = -1e30
SCALE = HD ** -0.5
NLAYER = 4

TB = 256
TF = 1024
TC = 32
NCHIP = 4

LR, B1, B2, AEPS, WD, STEP = 0.001, 0.9, 0.999, 1e-08, 0.01, 10

MESH = pl.DeviceIdType.MESH
SDS = jax.ShapeDtypeStruct
BS = pl.BlockSpec
ANY = pl.BlockSpec(memory_space=pl.ANY)


def _cp(sem, vmem_mb=None):
    kw = dict(dimension_semantics=sem)
    if vmem_mb is not None:
        kw["vmem_limit_bytes"] = vmem_mb * 1024 * 1024
    return pltpu.CompilerParams(**kw)


def _dot(a, b):
    return jnp.dot(a, b, preferred_element_type=F32)


def _dot_nt(a, b):
    return lax.dot_general(a, b, (((1,), (1,)), ((), ())), preferred_element_type=F32)


def _dot_tn(a, b):
    return lax.dot_general(a, b, (((0,), (0,)), ((), ())), preferred_element_type=F32)


def _rms(x):
    r = lax.rsqrt(jnp.mean(x * x, axis=-1, keepdims=True) + EPS)
    return x * r, r


def _rms_bwd(dn, n, r):
    return r * (dn - n * jnp.mean(dn * n, axis=-1, keepdims=True))


_GC = math.sqrt(2.0 / math.pi)


def _gelu(y):
    return 0.5 * y * (1.0 + jnp.tanh(_GC * (y + 0.044715 * y * y * y)))


def _gelu_grad(y):
    th = jnp.tanh(_GC * (y + 0.044715 * y * y * y))
    return 0.5 * (1.0 + th) + 0.5 * y * (1.0 - th * th) * _GC * (1.0 + 3 * 0.044715 * y * y)


def _sigmoid(v):
    return 1.0 / (1.0 + jnp.exp(-v))


def _colsum(x):
    return jnp.sum(x, axis=0, keepdims=True)


def _tok(width, col=0):
    return BS((None, TB, width), lambda b, j: (b, j, col))


def _mod_spec(nct):
    return BS((None, None, 8, D), lambda b, j: (b, jnp.where(j >= nct, 1, 0), 0, 0))


def _full(shape):
    nd = len(shape)
    return BS(shape, lambda *a: (0,) * nd)


def _resident(shape):
    nd = len(shape)
    return BS(shape, lambda *a: (0,) * nd, pipeline_mode=pl.Buffered(1))


def _ada_fwd(cact8, w4, b):
    wq = w4.shape[2]

    def body(c_ref, w_ref, b_ref, o_ref):
        o_ref[...] = _dot(c_ref[...].astype(MXU), w_ref[...]) + b_ref[...]

    return pl.pallas_call(
        body, name="ada_fwd", grid=(NCHIP,),
        in_specs=[BS((8, D), lambda j: (0, 0)), BS((None, D, wq), lambda j: (j, 0, 0)), BS((1, wq), lambda j: (0, j))],
        out_specs=BS((8, wq), lambda j: (0, j)),
        out_shape=SDS((8, NMOD * D), F32), compiler_params=_cp(("arbitrary",)),
    )(cact8, w4, b)


def _ada_bwd_c(dmod8, w4):
    wq = w4.shape[2]

    def body(d_ref, w_ref, o_ref):
        @pl.when(pl.program_id(0) == 0)
        def _():
            o_ref[...] = jnp.zeros_like(o_ref)
        o_ref[...] += _dot_nt(d_ref[...], w_ref[...])

    return pl.pallas_call(
        body, name="ada_bwd_c", grid=(NCHIP,),
        in_specs=[BS((8, wq), lambda j: (0, j)), BS((None, D, wq), lambda j: (j, 0, 0))],
        out_specs=BS((8, D), lambda j: (0, 0)),
        out_shape=SDS((8, D), F32), compiler_params=_cp(("arbitrary",)),
    )(dmod8, w4)


def _pre_mix(h, modtok, ng, w_in, nct, ex=None):
    B, S, _ = h.shape

    def body(h_ref, mod_ref, g_ref, w_ref, p_ref, a_ref):
        n, _ = _rms(h_ref[...])
        a = ((n * g_ref[0:1, :]) * (1.0 + mod_ref[1:2, :]) + mod_ref[0:1, :]).astype(MXU)
        a_ref[...] = a
        p_ref[...] = _dot(a, w_ref[...])

    return _call_with(
        ex, body, "pre_mix", (B, S // TB),
        [_tok(D), _mod_spec(nct), _full((8, D)), _full((D, INW))], [_tok(INW), _tok(D)],
        [SDS((B, S, INW), F32), SDS((B, S, D), MXU)], [h, modtok, ng, w_in], [],
        _cp(("arbitrary", "arbitrary"), 40))


def _pre_bwd(dqk, dv, dcv, du_parts, dh_part, h, modtok, ng, w_in, nct):
    B, S, _ = h.shape

    def body(dqk_ref, dv_ref, dcv_ref, du0_ref, du1_ref, du2_ref, dhp_ref, h_ref, mod_ref, g_ref, w_ref,
             dh_ref, vec_ref, dp_ref):
        du = du0_ref[...] + du1_ref[...] + du2_ref[...]
        dp = jnp.concatenate([dqk_ref[...], dv_ref[...].astype(MXU), dcv_ref[...].astype(MXU), du.astype(MXU)],
                             axis=1)
        dp_ref[...] = dp
        da = _dot_nt(dp, w_ref[...])
        n, r = _rms(h_ref[...])
        g = g_ref[0:1, :]
        sc1 = 1.0 + mod_ref[1:2, :]
        vec_ref[...] = jnp.zeros_like(vec_ref)
        vec_ref[0:1, :] = _colsum(da)
        vec_ref[1:2, :] = _colsum(da * (n * g))
        vec_ref[2:3, :] = _colsum(da * sc1 * n)
        dh_ref[...] = dhp_ref[...] + _rms_bwd(da * sc1 * g, n, r)

    return pl.pallas_call(
        body, name="pre_bwd", grid=(B, S // TB),
        in_specs=[_tok(AW + KVW), _tok(KVW), _tok(3 * CW), _tok(SW), _tok(SW), _tok(SW), _tok(D), _tok(D),
                  _mod_spec(nct), _full((8, D)), _full((D, INW))],
        out_specs=[_tok(D), BS((None, None, 8, D), lambda b, j: (b, j, 0, 0)), _tok(INW)],
        out_shape=[SDS((B, S, D), F32), SDS((B, S // TB, 8, D), F32), SDS((B, S, INW), MXU)],
        compiler_params=_cp(("arbitrary", "arbitrary"), 40),
    )(dqk, dv, dcv, *du_parts, dh_part, h, modtok, ng, w_in)


def _rotate(t, c, s):
    lane = lax.broadcasted_iota(jnp.int32, t.shape, 1)
    partner = jnp.where((lane % 32) < 16, pltpu.roll(t, 112, 1), pltpu.roll(t, 16, 1))
    return t * c + partner * s


def _to_groups(cols):
    lo = lax.broadcasted_iota(jnp.int32, cols[0].shape, 1) < HD
    out = []
    for g in range(QPK):
        a, b = cols[g // 2], cols[2 + g // 2]
        out.append(jnp.where(lo, a, pltpu.roll(b, HD, 1)) if g % 2 == 0 else jnp.where(lo, pltpu.roll(a, HD, 1), b))
    return out


def _from_groups(tiles):
    lo = lax.broadcasted_iota(jnp.int32, tiles[0].shape, 1) < HD
    out = []
    for k in range(NQ // 2):
        a, b = tiles[(2 * k) % QPK], tiles[(2 * k) % QPK + 1]
        out.append(jnp.where(lo, a, pltpu.roll(b, HD, 1)) if k < 2 else jnp.where(lo, pltpu.roll(a, HD, 1), b))
    return out


def _grp(width=128):
    return BS((None, QPK, TB, width), lambda b, j: (b, 0, j, 0))


_TAB = BS((TB, 128), lambda b, j: (j, 0))


def _rope_fwd(p, cos, sin):
    B, S, _ = p.shape

    def body(x_ref, c_ref, s_ref, q_ref, k_ref):
        c, s = c_ref[...], s_ref[...]
        rot = [_rotate(x_ref[:, k * 128:(k + 1) * 128], c, s) for k in range((AW + KVW) // 128)]
        k_ref[...] = rot[AW // 128].astype(MXU)
        for g, t in enumerate(_to_groups(rot[:AW // 128])):
            q_ref[g] = (t * SCALE).astype(MXU)

    return pl.pallas_call(
        body, name="rope_fwd", grid=(B, S // TB),
        in_specs=[_tok(AW + KVW), _TAB, _TAB],
        out_specs=[_grp(), _tok(KVW)],
        out_shape=[SDS((B, QPK, S, 128), MXU), SDS((B, S, KVW), MXU)],
        compiler_params=_cp(("arbitrary", "arbitrary")),
    )(p, cos, sin)


def _rope_bwd(dq4, dk, cos, nsin):
    B, _, S, _ = dq4.shape

    def body(q_ref, k_ref, c_ref, s_ref, o_ref):
        c, s = c_ref[...], s_ref[...]
        cols = _from_groups([q_ref[g] * SCALE for g in range(QPK)]) + [k_ref[...]]
        for k, t in enumerate(cols):
            o_ref[:, k * 128:(k + 1) * 128] = _rotate(t, c, s).astype(MXU)

    return pl.pallas_call(
        body, name="rope_bwd", grid=(B, S // TB),
        in_specs=[_grp(), _tok(KVW), _TAB, _TAB],
        out_specs=_tok(AW + KVW),
        out_shape=SDS((B, S, AW + KVW), MXU),
        compiler_params=_cp(("arbitrary", "arbitrary")),
    )(dq4, dk, cos, nsin)


def _attn_masks(i, S, Lc):
    ncb = Lc // QB
    is_lat = i >= ncb
    w0 = pl.multiple_of(jnp.clip((i - 1) * QB, 0, S - 3 * QB), QB)
    shape = (QPK * QB, 3 * QB)
    qpos = i * QB + jnp.bitwise_and(lax.broadcasted_iota(jnp.int32, shape, 0), QB - 1)
    kpos = w0 + lax.broadcasted_iota(jnp.int32, shape, 1)
    mask = jnp.logical_and(jnp.logical_and(kpos >= Lc, jnp.abs(qpos - kpos) <= WIN), is_lat)
    return w0, mask


def _attn_probs(qh, kch, kwh, mask, sk):
    sc = _dot_nt(qh, kch)
    sw = jnp.where(mask, _dot_nt(qh, kwh), NEG)
    m = jnp.maximum(jnp.maximum(jnp.max(sc, axis=-1, keepdims=True), jnp.max(sw, axis=-1, keepdims=True)), sk)
    ec = jnp.exp(sc - m)
    ew = jnp.exp(sw - m)
    es = jnp.exp(sk - m)
    inv = 1.0 / (jnp.sum(ec, axis=-1, keepdims=True) + jnp.sum(ew, axis=-1, keepdims=True) + es)
    return ec * inv, ew * inv, es * inv


def _sink_col(sink_ref, hk):
    return jnp.concatenate([jnp.full((QB, 1), sink_ref[hk * QPK + g], F32) for g in range(QPK)], axis=0)


_QGRP = BS((None, QPK, QB, 128), lambda b, i: (b, 0, i, 0))


def _first_last(grid):
    first = last = None
    for ax, n in enumerate(grid):
        i = pl.program_id(ax)
        first = (i == 0) if first is None else jnp.logical_and(first, i == 0)
        last = (i == n - 1) if last is None else jnp.logical_and(last, i == n - 1)
    return first, last


def _call_with(ex, body, name, grid, in_specs, out_specs, out_shape, args, scratch, params):
    n_in, n_out = len(in_specs), len(out_specs)
    if ex is None:
        res = pl.pallas_call(body, name=name, grid=grid, in_specs=in_specs, out_specs=out_specs,
                             out_shape=out_shape, scratch_shapes=scratch, compiler_params=params)(*args)
        return list(res), []

    def carrying(*refs):
        own, comm = _carried(ex, refs, n_in, n_out)
        first, last = _first_last(grid)
        pl.when(first)(lambda: ex.start(*comm))
        body(*own)
        pl.when(last)(lambda: ex.wait(*comm))

    res = pl.pallas_call(
        carrying, name=f"{name}_{type(ex).__name__.strip('_').lower()}", grid=grid,
        in_specs=in_specs + ex.specs, out_specs=out_specs + ex.specs, out_shape=out_shape + ex.out_shape,
        scratch_shapes=scratch + ex.sems, compiler_params=params)(*args, *ex.args)
    return list(res[:n_out]), list(res[n_out:])


def _attn_fwd(sink, q4, k, p, Lc, ex=None):
    B, S, _ = k.shape

    def body(sink_ref, q_ref, k_ref, v_ref, o_ref):
        w0, mask = _attn_masks(pl.program_id(1), S, Lc)
        kc = k_ref[0:Lc, :]
        vc = v_ref[0:Lc, :].astype(MXU)
        kw = k_ref[pl.ds(w0, 3 * QB), :]
        vw = v_ref[pl.ds(w0, 3 * QB), :].astype(MXU)
        q = q_ref[...].reshape(QPK * QB, 128)
        half = QPK * QB // 2
        outs = []
        for hk in range(NKV):
            cs = slice(hk * HD, (hk + 1) * HD)
            sk = _sink_col(sink_ref, hk)
            for r0 in (0, half):
                rows = slice(r0, r0 + half)
                pc, pw, _ = _attn_probs(q[rows, cs], kc[:, cs], kw[:, cs], mask[:half], sk[rows])
                o = _dot(pc.astype(MXU), vc[:, cs]) + _dot(pw.astype(MXU), vw[:, cs])
                outs += [o[g * QB:(g + 1) * QB] for g in range(half // QB)]
        o_ref[...] = jnp.concatenate(outs, axis=1).astype(o_ref.dtype)

    return _call_with(
        ex, body, "attn_fwd", (B, S // QB),
        [BS(memory_space=pltpu.SMEM), _QGRP, BS((None, S, KVW), lambda b, i: (b, 0, 0)),
         BS((None, S, KVW), lambda b, i: (b, 0, (AW + KVW) // KVW))],
        [BS((None, QB, AW), lambda b, i: (b, i, 0))], [SDS((B, S, AW), MXU)], [sink, q4, k, p], [],
        _cp(("arbitrary", "arbitrary"), 40))


def _attn_bwd(sink, q4, k, p, do4, Lc, ex=None):
    B, S, _ = k.shape

    def body(sink_ref, q_ref, k_ref, v_ref, do_ref, dq_ref, dk_ref, dv_ref, ds_ref):
        i = pl.program_id(1)

        @pl.when(i == 0)
        def _():
            dk_ref[...] = jnp.zeros_like(dk_ref)
            dv_ref[...] = jnp.zeros_like(dv_ref)
            ds_ref[...] = jnp.zeros_like(ds_ref)

        w0, mask = _attn_masks(i, S, Lc)
        kc = k_ref[0:Lc, :]
        vc = v_ref[0:Lc, :].astype(MXU)
        kw = k_ref[pl.ds(w0, 3 * QB), :]
        vw = v_ref[pl.ds(w0, 3 * QB), :].astype(MXU)
        q = q_ref[...].reshape(QPK * QB, 128)
        do = do_ref[...].reshape(QPK * QB, 128)
        dqs, dsinks, dkcs, dkws, dvcs, dvws = [], [], [], [], [], []
        for hk in range(NKV):
            cs = slice(hk * HD, (hk + 1) * HD)
            qh = q[:, cs]
            pc, pw, ps = _attn_probs(qh, kc[:, cs], kw[:, cs], mask, _sink_col(sink_ref, hk))
            dob = do[:, cs].astype(MXU)
            dpc = _dot_nt(dob, vc[:, cs])
            dpw = _dot_nt(dob, vw[:, cs])
            delta = jnp.sum(pc * dpc, axis=-1, keepdims=True) + jnp.sum(pw * dpw, axis=-1, keepdims=True)
            dsc = (pc * (dpc - delta)).astype(MXU)
            dsw = (pw * (dpw - delta)).astype(MXU)
            dqs.append(_dot(dsc, kc[:, cs]) + _dot(dsw, kw[:, cs]))
            dkcs.append(_dot_tn(dsc, qh))
            dkws.append(_dot_tn(dsw, qh))
            dvcs.append(_dot_tn(pc.astype(MXU), dob))
            dvws.append(_dot_tn(pw.astype(MXU), dob))
            psd = ps * delta
            dsinks += [jnp.broadcast_to(-jnp.sum(psd[g * QB:(g + 1) * QB], axis=0, keepdims=True), (1, 128))
                       for g in range(QPK)]
        for g in range(QPK):
            dq_ref[g] = jnp.concatenate([dq[g * QB:(g + 1) * QB] for dq in dqs], axis=1)
        ds_ref[...] += jnp.concatenate(dsinks, axis=0)
        dk_ref[0:Lc, :] += jnp.concatenate(dkcs, axis=1)
        dv_ref[0:Lc, :] += jnp.concatenate(dvcs, axis=1)
        dk_ref[pl.ds(w0, 3 * QB), :] += jnp.concatenate(dkws, axis=1)
        dv_ref[pl.ds(w0, 3 * QB), :] += jnp.concatenate(dvws, axis=1)

    kv_acc = BS((None, S, KVW), lambda b, i: (b, 0, 0))
    return _call_with(
        ex, body, "attn_bwd", (B, S // QB),
        [BS(memory_space=pltpu.SMEM), _QGRP, BS((None, S, KVW), lambda b, i: (b, 0, 0)),
         BS((None, S, KVW), lambda b, i: (b, 0, (AW + KVW) // KVW)), _QGRP],
        [_QGRP, kv_acc, kv_acc, BS((None, 8, 128), lambda b, i: (b, 0, 0))],
        [SDS((B, QPK, S, 128), F32), SDS((B, S, KVW), F32), SDS((B, S, KVW), F32), SDS((B, 8, 128), F32)],
        [sink, q4, k, p, do4], [], _cp(("arbitrary", "arbitrary"), 48))


def _halo_specs(width, col, S):
    nb8 = S // 8
    per = TB // 8
    prev = BS((None, 8, width), lambda b, j: (b, jnp.maximum(j * per - 1, 0), col))
    nxt = BS((None, 8, width), lambda b, j: (b, jnp.minimum((j + 1) * per, nb8 - 1), col))
    return prev, nxt


def _shift_dn(z, prev_row, at_start):
    row = lax.broadcasted_iota(jnp.int32, z.shape, 0)
    first = jnp.where(at_start, 0.0, 1.0) * prev_row
    return jnp.where(row == 0, first, pltpu.roll(z, 1, 0))


def _shift_up(z, next_row, at_end):
    row = lax.broadcasted_iota(jnp.int32, z.shape, 0)
    last = jnp.where(at_end, 0.0, 1.0) * next_row
    return jnp.where(row == z.shape[0] - 1, last, pltpu.roll(z, z.shape[0] - 1, 0))


def _conv_fwd(p, cw8, nct):
    B, S, _ = p.shape
    nt = S // TB

    def body(p_ref, pp_ref, pn_ref, w_ref, o_ref):
        j = pl.program_id(1)
        at_start = jnp.logical_or(j == 0, j == nct)
        at_end = jnp.logical_or(j == nct - 1, j == nt - 1)
        z = p_ref[:, 256:512] * p_ref[:, 512:768]
        zprev = pp_ref[7:8, 256:512] * pp_ref[7:8, 512:768]
        znext = pn_ref[0:1, 256:512] * pn_ref[0:1, 512:768]
        c3 = (_shift_dn(z, zprev, at_start) * w_ref[0:1, :] + z * w_ref[1:2, :]
              + _shift_up(z, znext, at_end) * w_ref[2:3, :])
        o_ref[...] = p_ref[:, 0:256] * c3

    prev, nxt = _halo_specs(3 * CW, 1, S)
    return pl.pallas_call(
        body, name="conv_fwd", grid=(B, nt),
        in_specs=[_tok(3 * CW, 1), prev, nxt, _full((8, CW))],
        out_specs=_tok(CW),
        out_shape=SDS((B, S, CW), F32),
        compiler_params=_cp(("arbitrary", "arbitrary")),
    )(p, p, p, cw8)


def _conv_bwd(p, dcv, cw8, nct):
    B, S, _ = p.shape
    nt = S // TB

    def body(p_ref, pp_ref, pn_ref, d_ref, dp_ref, dn_ref, w_ref, o_ref, dw_ref):
        j = pl.program_id(1)
        at_start = jnp.logical_or(j == 0, j == nct)
        at_end = jnp.logical_or(j == nct - 1, j == nt - 1)
        cb, cc, cx = p_ref[:, 0:256], p_ref[:, 256:512], p_ref[:, 512:768]
        z = cc * cx
        zm = _shift_dn(z, pp_ref[7:8, 256:512] * pp_ref[7:8, 512:768], at_start)
        zp = _shift_up(z, pn_ref[0:1, 256:512] * pn_ref[0:1, 512:768], at_end)
        d = d_ref[...]
        e = d * cb
        em = _shift_dn(e, dp_ref[7:8, :] * pp_ref[7:8, 0:256], at_start)
        ep = _shift_up(e, dn_ref[0:1, :] * pn_ref[0:1, 0:256], at_end)
        dz = e * w_ref[1:2, :] + ep * w_ref[0:1, :] + em * w_ref[2:3, :]
        o_ref[:, 0:256] = d * (zm * w_ref[0:1, :] + z * w_ref[1:2, :] + zp * w_ref[2:3, :])
        o_ref[:, 256:512] = dz * cx
        o_ref[:, 512:768] = dz * cc
        dw_ref[...] = jnp.zeros_like(dw_ref)
        dw_ref[0:1, :] = _colsum(e * zm)
        dw_ref[1:2, :] = _colsum(e * z)
        dw_ref[2:3, :] = _colsum(e * zp)

    prev, nxt = _halo_specs(3 * CW, 1, S)
    dprev, dnxt = _halo_specs(CW, 0, S)
    return pl.pallas_call(
        body, name="conv_bwd", grid=(B, nt),
        in_specs=[_tok(3 * CW, 1), prev, nxt, _tok(CW), dprev, dnxt, _full((8, CW))],
        out_specs=[_tok(3 * CW), BS((None, None, 8, CW), lambda b, j: (b, j, 0, 0))],
        out_shape=[SDS((B, S, 3 * CW), F32), SDS((B, nt, 8, CW), F32)],
        compiler_params=_cp(("arbitrary", "arbitrary")),
    )(p, p, p, dcv, dcv, dcv, cw8)


HW = SGRP * SST


def _rev_chunk(k, nk, ncc):
    return jnp.where(k < ncc, ncc - 1 - k, nk - 1 - k + ncc)


def _scan_perm(B):
    n = B * TC
    pm = np.zeros((2 * n, 2 * n), np.float32)
    for t in range(TC):
        for e in range(B):
            pm[t * 2 * B + e, e * TC + t] = 1.0
            pm[t * 2 * B + B + e, n + e * TC + (TC - 1 - t)] = 1.0
    return pm


def _scan_drive(uf_ref, ub_ref, pm_ref, b4_ref, fwd, dbuf, n):
    u_tok = jnp.concatenate([uf_ref[...].reshape(n, SW), ub_ref[...].reshape(n, SW)], axis=0).astype(MXU)
    u_tm = _dot(pm_ref[...], u_tok).astype(MXU)
    d4 = _dot(u_tm, b4_ref[...])
    dbuf[...] = jnp.where(fwd, d4[:, :2 * HW], d4[:, 2 * HW:])
    return u_tm


def _scan_fwd(p, pm, pmt, b4, a, c2, Lc, ex=None):
    B, S, _ = p.shape
    nch, n = 2 * B, B * TC
    nk, ncc = S // TC, Lc // TC
    ucol = (INW - SW) // SW

    def body(uf_ref, ub_ref, pm_ref, pmt_ref, b4_ref, a_ref, c2_ref, yf_ref, yb_ref, hb_ref, hcar, dbuf, hbuf):
        @pl.when(pl.program_id(0) == 0)
        def _():
            hcar[...] = jnp.zeros_like(hcar)

        fwd = (lax.broadcasted_iota(jnp.int32, (2 * n, 1), 0) % nch) < B
        _scan_drive(uf_ref, ub_ref, pm_ref, b4_ref, fwd, dbuf, n)
        hb_ref[...] = hcar[...]
        ar, ai = a_ref[:, :HW], a_ref[:, HW:]

        def step(t, h):
            hr, hi = h
            r0 = pl.multiple_of(t * nch, nch)
            nr = ar * hr - ai * hi + dbuf[pl.ds(r0, nch), :HW]
            ni = ar * hi + ai * hr + dbuf[pl.ds(r0, nch), HW:]
            hbuf[pl.ds(r0, nch), :HW] = nr
            hbuf[pl.ds(r0, nch), HW:] = ni
            return nr, ni

        hr, hi = lax.fori_loop(0, TC, step, (hcar[:, :HW], hcar[:, HW:]))
        hcar[:, :HW] = hr
        hcar[:, HW:] = hi
        h_tok = _dot(pmt_ref[...], hbuf[...].astype(MXU)).astype(MXU)
        yf_ref[...] = _dot(h_tok[:n], c2_ref[:, :SW]).reshape(B, TC, SW)
        yb_ref[...] = _dot(h_tok[n:], c2_ref[:, SW:]).reshape(B, TC, SW)

    return _call_with(
        ex, body, "scan_fwd", (nk,),
        [BS((B, TC, SW), lambda k: (0, k, ucol)),
         BS((B, TC, SW), lambda k: (0, _rev_chunk(k, nk, ncc), ucol)),
         _full((2 * n, 2 * n)), _full((2 * n, 2 * n)),
         _full((SW, 4 * HW)), _full((nch, 2 * HW)), _full((2 * HW, 2 * SW))],
        [BS((B, TC, SW), lambda k: (0, k, 0)),
         BS((B, TC, SW), lambda k: (0, _rev_chunk(k, nk, ncc), 0)),
         BS((None, nch, 2 * HW), lambda k: (k, 0, 0))],
        [SDS((B, S, SW), F32), SDS((B, S, SW), F32), SDS((nk, nch, 2 * HW), F32)],
        [p, p, pm, pmt, b4, a, c2],
        [pltpu.VMEM((nch, 2 * HW), F32), pltpu.VMEM((2 * n, 2 * HW), F32), pltpu.VMEM((2 * n, 2 * HW), F32)],
        _cp(("arbitrary",), 48))


def _scan_bwd(p, dy, hb, pm, pmt, b4, a, c2, Lc, ex=None):
    B, S, _ = p.shape
    nch, n = 2 * B, B * TC
    nk, ncc = S // TC, Lc // TC
    ucol = (INW - SW) // SW
    rows = 2 * n

    def body(uf_ref, ub_ref, dyf_ref, dyb_ref, hb_ref, pm_ref, pmt_ref, b4_ref, a_ref, c2_ref,
             duf_ref, dub_ref, db4_ref, da_ref, dc2_ref, gcar, dbuf, hbuf, gbuf):
        @pl.when(pl.program_id(0) == 0)
        def _():
            gcar[...] = jnp.zeros_like(gcar)
            db4_ref[...] = jnp.zeros_like(db4_ref)
            da_ref[...] = jnp.zeros_like(da_ref)
            dc2_ref[...] = jnp.zeros_like(dc2_ref)

        fwd = (lax.broadcasted_iota(jnp.int32, (rows, 1), 0) % nch) < B
        u_tm = _scan_drive(uf_ref, ub_ref, pm_ref, b4_ref, fwd, dbuf, n)
        ar, ai = a_ref[:, :HW], a_ref[:, HW:]
        hbuf[0:nch, :] = hb_ref[...]

        def fstep(t, h):
            hr, hi = h
            r0 = pl.multiple_of(t * nch, nch)
            nr = ar * hr - ai * hi + dbuf[pl.ds(r0, nch), :HW]
            ni = ar * hi + ai * hr + dbuf[pl.ds(r0, nch), HW:]
            hbuf[pl.ds(r0 + nch, nch), :HW] = nr
            hbuf[pl.ds(r0 + nch, nch), HW:] = ni
            return nr, ni

        lax.fori_loop(0, TC, fstep, (hb_ref[:, :HW], hb_ref[:, HW:]))

        dy_tok = jnp.concatenate([dyf_ref[...].reshape(n, SW), dyb_ref[...].reshape(n, SW)], axis=0).astype(MXU)
        dy_tm = _dot(pm_ref[...], dy_tok)
        dy2 = jnp.concatenate([jnp.where(fwd, dy_tm, 0.0), jnp.where(fwd, 0.0, dy_tm)], axis=1).astype(MXU)
        gbuf[...] = _dot_nt(dy2, c2_ref[...])
        dc2_ref[...] += _dot_tn(hbuf[nch:, :].astype(MXU), dy2)

        def bstep(i, cary):
            cr, ci = cary
            r0 = pl.multiple_of((TC - 1 - i) * nch, nch)
            gr = gbuf[pl.ds(r0, nch), :HW] + cr
            gi = gbuf[pl.ds(r0, nch), HW:] + ci
            gbuf[pl.ds(r0, nch), :HW] = gr
            gbuf[pl.ds(r0, nch), HW:] = gi
            return ar * gr + ai * gi, ar * gi - ai * gr

        cr, ci = lax.fori_loop(0, TC, bstep, (gcar[:, :HW], gcar[:, HW:]))
        gcar[:, :HW] = cr
        gcar[:, HW:] = ci

        gr = gbuf[:, :HW].reshape(TC, nch, HW)
        gi = gbuf[:, HW:].reshape(TC, nch, HW)
        hpr = hbuf[0:rows, :HW].reshape(TC, nch, HW)
        hpi = hbuf[0:rows, HW:].reshape(TC, nch, HW)
        da_ref[:, :HW] += jnp.sum(gr * hpr + gi * hpi, axis=0)
        da_ref[:, HW:] += jnp.sum(gi * hpr - gr * hpi, axis=0)

        g = gbuf[...]
        gm = g.astype(MXU)
        dd4 = jnp.concatenate([jnp.where(fwd, g, 0.0), jnp.where(fwd, 0.0, g)], axis=1).astype(MXU)
        db4_ref[...] += _dot_tn(u_tm, dd4)
        g_tok = _dot(pmt_ref[...], gm).astype(MXU)
        duf_ref[...] = _dot_nt(g_tok[:n], b4_ref[:, :2 * HW]).reshape(B, TC, SW)
        dub_ref[...] = _dot_nt(g_tok[n:], b4_ref[:, 2 * HW:]).reshape(B, TC, SW)

    fwd_blk = lambda col: BS((B, TC, SW), lambda k: (0, nk - 1 - k, col))
    bwd_blk = lambda col: BS((B, TC, SW), lambda k: (0, _rev_chunk(nk - 1 - k, nk, ncc), col))
    return _call_with(
        ex, body, "scan_bwd", (nk,),
        [fwd_blk(ucol), bwd_blk(ucol), fwd_blk(0), bwd_blk(0),
         BS((None, nch, 2 * HW), lambda k: (nk - 1 - k, 0, 0)),
         _full((rows, rows)), _full((rows, rows)),
         _full((SW, 4 * HW)), _full((nch, 2 * HW)), _full((2 * HW, 2 * SW))],
        [fwd_blk(0), bwd_blk(0), _full((SW, 4 * HW)), _full((nch, 2 * HW)), _full((2 * HW, 2 * SW))],
        [SDS((B, S, SW), F32), SDS((B, S, SW), F32), SDS((SW, 4 * HW), F32), SDS((nch, 2 * HW), F32),
         SDS((2 * HW, 2 * SW), F32)],
        [p, p, dy, dy, hb, pm, pmt, b4, a, c2],
        [pltpu.VMEM((nch, 2 * HW), F32), pltpu.VMEM((rows, 2 * HW), F32),
         pltpu.VMEM((rows + nch, 2 * HW), F32), pltpu.VMEM((rows, 2 * HW), F32)],
        _cp(("arbitrary",), 56))


def _post_mix(o, cv, yf, yb, p, h, modtok, ng, dsk, wg, bg, wo, nct, ex=None):
    B, S, _ = h.shape

    def body(o_ref, cv_ref, yf_ref, yb_ref, u_ref, h_ref, mod_ref, g_ref, dsk_ref, wg_ref, bg_ref, wo_ref,
             h1_ref, mix_ref, m_ref, y_ref):
        y = yf_ref[...] + yb_ref[...] + dsk_ref[...] * u_ref[...]
        g = _gelu(y)
        s = g * _sigmoid(_dot(g.astype(MXU), wg_ref[...]) + bg_ref[...])
        mix_ref[:, 0:AW] = o_ref[...]
        mix_ref[:, AW:AW + CW] = cv_ref[...].astype(MXU)
        mix_ref[:, AW + CW:] = s.astype(MXU)
        m = _dot(mix_ref[...], wo_ref[...])
        n, _ = _rms(m)
        h1_ref[...] = h_ref[...] + mod_ref[2:3, :] * (n * g_ref[1:2, :])
        m_ref[...] = m
        y_ref[...] = y

    return _call_with(
        ex, body, "post_mix", (B, S // TB),
        [_tok(AW), _tok(CW), _tok(SW), _tok(SW), _tok(SW, (INW - SW) // SW), _tok(D), _mod_spec(nct),
         _full((8, D)), _full((1, SW)), _full((SW, SW)), _full((1, SW)), _full((D, D))],
        [_tok(D), _tok(D), _tok(D), _tok(SW)],
        [SDS((B, S, D), F32), SDS((B, S, D), MXU), SDS((B, S, D), F32), SDS((B, S, SW), F32)],
        [o, cv, yf, yb, p, h, modtok, ng, dsk, wg, bg, wo], [], _cp(("arbitrary", "arbitrary"), 40))


def _post_bwd(dh1, m, y, p, modtok, ng, dsk, wg, bg, wo, nct):
    B, S, _ = m.shape

    def body(dh_ref, m_ref, y_ref, u_ref, mod_ref, g_ref, dsk_ref, wg_ref, bg_ref, wo_ref,
             dm_ref, da_ref, dc_ref, dy_ref, du_ref, vec_ref, vec2_ref, dwg_ref):
        @pl.when(jnp.logical_and(pl.program_id(0) == 0, pl.program_id(1) == 0))
        def _():
            dwg_ref[...] = jnp.zeros_like(dwg_ref)

        n, r = _rms(m_ref[...])
        dh1 = dh_ref[...]
        gpm = g_ref[1:2, :]
        dr = dh1 * mod_ref[2:3, :]
        vec_ref[...] = jnp.zeros_like(vec_ref)
        vec_ref[0:1, :] = _colsum(dh1 * (n * gpm))
        vec_ref[1:2, :] = _colsum(dr * n)
        dm = _rms_bwd(dr * gpm, n, r).astype(MXU)
        dm_ref[...] = dm
        dmix = _dot_nt(dm, wo_ref[...])
        for g, t in enumerate(_to_groups([dmix[:, k * 128:(k + 1) * 128] for k in range(AW // 128)])):
            da_ref[g] = t
        dc_ref[...] = dmix[:, AW:AW + CW]
        ds = dmix[:, AW + CW:]
        yv = y_ref[...]
        g = _gelu(yv)
        gb = g.astype(MXU)
        sg = _sigmoid(_dot(gb, wg_ref[...]) + bg_ref[...])
        dv = ds * g * sg * (1.0 - sg)
        dvb = dv.astype(MXU)
        dg = ds * sg + _dot_nt(dvb, wg_ref[...])
        dwg_ref[...] += _dot_tn(gb, dvb)
        dy = dg * _gelu_grad(yv)
        dy_ref[...] = dy
        du_ref[...] = dy * dsk_ref[...]
        vec2_ref[...] = jnp.zeros_like(vec2_ref)
        vec2_ref[0:1, :] = _colsum(dy * u_ref[...])
        vec2_ref[1:2, :] = _colsum(dv)

    nt = S // TB
    return pl.pallas_call(
        body, name="post_bwd", grid=(B, nt),
        in_specs=[_tok(D), _tok(D), _tok(SW), _tok(SW, (INW - SW) // SW), _mod_spec(nct), _full((8, D)),
                  _full((1, SW)), _full((SW, SW)), _full((1, SW)), _full((D, D))],
        out_specs=[_tok(D), _grp(), _tok(CW), _tok(SW), _tok(SW),
                   BS((None, None, 8, D), lambda b, j: (b, j, 0, 0)),
                   BS((None, None, 8, SW), lambda b, j: (b, j, 0, 0)), _full((SW, SW))],
        out_shape=[SDS((B, S, D), MXU), SDS((B, QPK, S, 128), F32), SDS((B, S, CW), F32), SDS((B, S, SW), F32),
                   SDS((B, S, SW), F32), SDS((B, nt, 8, D), F32), SDS((B, nt, 8, SW), F32), SDS((SW, SW), F32)],
        compiler_params=_cp(("arbitrary", "arbitrary"), 40),
    )(dh1, m, y, p, modtok, ng, dsk, wg, bg, wo)


def _mlp_fwd(h1, modtok, ng, w1, w2, nct, ex=None):
    B, S, _ = h1.shape
    assert w1.shape == (DFF // TF, D, TF)

    def body(h_ref, mod_ref, g_ref, w1_ref, w2_ref, h2_ref, a2_ref, z_ref, f_ref):
        n, _ = _rms(h_ref[...])
        a2 = ((n * g_ref[2:3, :]) * (1.0 + mod_ref[4:5, :]) + mod_ref[3:4, :]).astype(MXU)
        a2_ref[...] = a2
        ff = jnp.zeros((TB, D), F32)
        for c in range(DFF // TF):
            cs = slice(c * TF, (c + 1) * TF)
            z = _dot(a2, w1_ref[c])
            z_ref[:, cs] = z.astype(MXU)
            r = jnp.maximum(z, 0.0)
            ff = ff + _dot((r * r).astype(MXU), w2_ref[cs, :])
        f_ref[...] = ff
        n, _ = _rms(ff)
        h2_ref[...] = h_ref[...] + mod_ref[5:6, :] * (n * g_ref[3:4, :])

    return _call_with(
        ex, body, "mlp_fwd", (B, S // TB),
        [_tok(D), _mod_spec(nct), _full((8, D)), _resident((DFF // TF, D, TF)), _resident((DFF, D))],
        [_tok(D), _tok(D), _tok(DFF), _tok(D)],
        [SDS((B, S, D), F32), SDS((B, S, D), MXU), SDS((B, S, DFF), MXU), SDS((B, S, D), F32)],
        [h1, modtok, ng, w1, w2], [], _cp(("arbitrary", "arbitrary"), 48))


def _mlp_bwd(dh2, h1, f, z, modtok, ng, w1, w2, nct, ex=None):
    B, S, _ = h1.shape

    def body(dh_ref, h_ref, f_ref, z_ref, mod_ref, g_ref, w1_ref, w2_ref, dh1_ref, df_ref, dz_ref, vec_ref):
        n, r = _rms(f_ref[...])
        dh2 = dh_ref[...]
        gp = g_ref[3:4, :]
        dr = dh2 * mod_ref[5:6, :]
        vec_ref[...] = jnp.zeros_like(vec_ref)
        vec_ref[3:4, :] = _colsum(dh2 * (n * gp))
        vec_ref[4:5, :] = _colsum(dr * n)
        df = _rms_bwd(dr * gp, n, r).astype(MXU)
        df_ref[...] = df
        da = jnp.zeros((TB, D), F32)
        for c in range(DFF // TF):
            cs = slice(c * TF, (c + 1) * TF)
            dr2 = _dot_nt(df, w2_ref[cs, :])
            dz = (dr2 * 2.0 * jnp.maximum(z_ref[:, cs].astype(F32), 0.0)).astype(MXU)
            dz_ref[:, cs] = dz
            da = da + _dot_nt(dz, w1_ref[c])
        n, r = _rms(h_ref[...])
        g = g_ref[2:3, :]
        sc1 = 1.0 + mod_ref[4:5, :]
        vec_ref[0:1, :] = _colsum(da)
        vec_ref[1:2, :] = _colsum(da * (n * g))
        vec_ref[2:3, :] = _colsum(da * sc1 * n)
        dh1_ref[...] = dh2 + _rms_bwd(da * sc1 * g, n, r)

    return _call_with(
        ex, body, "mlp_bwd", (B, S // TB),
        [_tok(D), _tok(D), _tok(D), _tok(DFF), _mod_spec(nct), _full((8, D)),
         _resident((DFF // TF, D, TF)), _resident((DFF, D))],
        [_tok(D), _tok(D), _tok(DFF), BS((None, None, 8, D), lambda b, j: (b, j, 0, 0))],
        [SDS((B, S, D), F32), SDS((B, S, D), MXU), SDS((B, S, DFF), MXU), SDS((B, S // TB, 8, D), F32)],
        [dh2, h1, f, z, modtok, ng, w1, w2], [], _cp(("arbitrary", "arbitrary"), 56))


def _loss(h, target, Lc):
    B, S, _ = h.shape
    nt, nct = S // TB, Lc // TB

    def body(h_ref, t_ref, dh_ref, l_ref):
        lat = pl.program_id(1) >= nct
        err = jnp.where(lat, h_ref[...] - t_ref[...], 0.0)
        dh_ref[...] = err * (1.0 / D)
        l_ref[...] = jnp.broadcast_to(jnp.sum(err * err, keepdims=True), (8, 128))

    return pl.pallas_call(
        body, name="loss", grid=(B, nt),
        in_specs=[_tok(D), BS((None, TB, D), lambda b, j: (b, jnp.maximum(j - nct, 0), 0))],
        out_specs=[_tok(D), BS((None, None, 8, 128), lambda b, j: (b, j, 0, 0))],
        out_shape=[SDS((B, S, D), F32), SDS((B, nt, 8, 128), F32)],
        compiler_params=_cp(("arbitrary", "arbitrary")),
    )(h, target)


def _mm_tn(a, b, name, relu2=False):
    T, M = a.shape
    N = b.shape[1]
    def blk(n):
        return max(b for b in range(128, 1025, 128) if n % b == 0)

    bm, bn = blk(M), blk(N)
    tk = next(t for t in (1536, 1024, 512, 256, T) if T % t == 0)
    nkk = T // tk

    def body(a_ref, b_ref, o_ref, om_ref):
        k = pl.program_id(2)

        @pl.when(k == 0)
        def _():
            o_ref[...] = jnp.zeros_like(o_ref)

        av = a_ref[...]
        if relu2:
            r = jnp.maximum(av.astype(F32), 0.0)
            av = (r * r).astype(MXU)
        o_ref[...] += _dot_tn(av, b_ref[...])

        @pl.when(k == nkk - 1)
        def _():
            om_ref[...] = o_ref[...].astype(MXU)

    out = BS((bm, bn), lambda i, j, k: (i, j))
    return pl.pallas_call(
        body, name=name, grid=(M // bm, N // bn, nkk),
        in_specs=[BS((tk, bm), lambda i, j, k: (k, i)), BS((tk, bn), lambda i, j, k: (k, j))],
        out_specs=[out, out], out_shape=[SDS((M, N), F32), SDS((M, N), MXU)],
        compiler_params=_cp(("arbitrary", "arbitrary", "arbitrary"), 56),
    )(a, b)


_PAD_ROWS = 64


def _block_rows(R):
    for br in (256, _PAD_ROWS):
        if R % br == 0:
            return br
    raise ValueError(f"row count {R} is not a multiple of {_PAD_ROWS}")


def _adamw(w, ga, gb, m, v, name):
    R, C = w.shape
    br = _block_rows(R) if C <= _LANES else _PAD_ROWS

    def body(w_ref, ga_ref, gb_ref, m_ref, v_ref, g_out, d_out, m_out, v_out):
        g = ga_ref[...] + gb_ref[...]
        m2 = B1 * m_ref[...] + (1.0 - B1) * g
        v2 = B2 * v_ref[...] + (1.0 - B2) * (g * g)
        m_hat = m2 / (1.0 - B1 ** STEP)
        v_hat = v2 / (1.0 - B2 ** STEP)
        g_out[...] = g
        d_out[...] = -LR * (m_hat / (jnp.sqrt(v_hat) + AEPS) + WD * w_ref[...])
        m_out[...] = m2
        v_out[...] = v2

    own = BS((br, C), lambda i: (i, 0))
    return pl.pallas_call(
        body, name=name, grid=(R // br,),
        in_specs=[own] * 5, out_specs=[own] * 4, out_shape=[SDS((R, C), F32)] * 4,
        compiler_params=_cp(("arbitrary",)),
    )(w, ga, gb, m, v)


def _my_xyc():
    return lax.axis_index("x"), lax.axis_index("y"), lax.axis_index("c")


def _chip_peers(x, y):
    return [(1 - x, y), (x, 1 - y), (1 - x, 1 - y)]


class _Gather:
    def __init__(self, shards):
        self.args = list(shards)
        self.n = len(self.args)
        self.specs = [ANY] * self.n
        self.out_shape = [SDS((NCHIP,) + a.shape, a.dtype) for a in self.args]
        self.sems = [pltpu.SemaphoreType.DMA((3 * self.n,)), pltpu.SemaphoreType.DMA((3 * self.n,)),
                     pltpu.SemaphoreType.DMA((self.n,))]

    def _copies(self, ins, outs, ssem, rsem, lsem):
        x, y, c = _my_xyc()
        mine, sends, recvs = [], [], []
        for k, (s_ref, o_ref) in enumerate(zip(ins, outs)):
            mine.append(pltpu.make_async_copy(s_ref, o_ref.at[2 * x + y], lsem.at[k]))
            for j, (px, py) in enumerate(_chip_peers(x, y)):
                sems = dict(send_sem=ssem.at[3 * k + j], recv_sem=rsem.at[3 * k + j], device_id=(px, py, c),
                            device_id_type=MESH)
                sends.append(pltpu.make_async_remote_copy(src_ref=s_ref, dst_ref=o_ref.at[2 * x + y], **sems))
                recvs.append(pltpu.make_async_remote_copy(src_ref=s_ref, dst_ref=o_ref.at[2 * px + py], **sems))
        return mine, sends, recvs

    def start(self, ins, outs, sems):
        mine, sends, _ = self._copies(ins, outs, *sems)
        for cp in mine + sends:
            cp.start()

    def wait(self, ins, outs, sems):
        mine, sends, recvs = self._copies(ins, outs, *sems)
        for cp in recvs:
            cp.wait_recv()
        for cp in sends:
            cp.wait_send()
        for cp in mine:
            cp.wait()


class _Gather2(_Gather):
    def __init__(self, shards):
        super().__init__(shards)
        self.sems = self.sems + [pltpu.SemaphoreType.DMA((3 * self.n,)), pltpu.SemaphoreType.DMA((3 * self.n,))]

    def _copies2(self, ins, outs, ssem, rsem, lsem, fsem, gsem):
        x, y, c = _my_xyc()
        mine, sends, recvs, passes, got = [], [], [], [], []
        for k, (s_ref, o_ref) in enumerate(zip(ins, outs)):
            half = s_ref.shape[0] // 2
            my_rows, sib_rows = pl.ds(c * half, half), pl.ds((1 - c) * half, half)
            mine.append(pltpu.make_async_copy(s_ref, o_ref.at[2 * x + y], lsem.at[k]))
            for j, (px, py) in enumerate(_chip_peers(x, y)):
                ici = dict(send_sem=ssem.at[3 * k + j], recv_sem=rsem.at[3 * k + j], device_id=(px, py, c),
                           device_id_type=MESH)
                d2d = dict(send_sem=fsem.at[3 * k + j], recv_sem=gsem.at[3 * k + j], device_id=(x, y, 1 - c),
                           device_id_type=MESH)
                landed = o_ref.at[2 * px + py, my_rows]
                sends.append(pltpu.make_async_remote_copy(src_ref=s_ref.at[my_rows],
                                                          dst_ref=o_ref.at[2 * x + y, my_rows], **ici))
                recvs.append(pltpu.make_async_remote_copy(src_ref=s_ref.at[my_rows], dst_ref=landed, **ici))
                passes.append(pltpu.make_async_remote_copy(src_ref=landed, dst_ref=landed, **d2d))
                got.append(pltpu.make_async_remote_copy(src_ref=landed, dst_ref=o_ref.at[2 * px + py, sib_rows],
                                                        **d2d))
        return mine, sends, recvs, passes, got

    def start(self, ins, outs, sems):
        mine, sends, _, _, _ = self._copies2(ins, outs, *sems)
        for cp in mine + sends:
            cp.start()

    def wait(self, ins, outs, sems):
        mine, sends, recvs, passes, got = self._copies2(ins, outs, *sems)
        for landed, onward in zip(recvs, passes):
            landed.wait_recv()
            onward.start()
        for cp in got:
            cp.wait_recv()
        for cp in sends + passes:
            cp.wait_send()
        for cp in mine:
            cp.wait()


class _Scatter:
    def __init__(self, sends, cols=None):
        self.args = list(sends)
        self.n = len(self.args)
        self.cols = list(cols) if cols is not None else [None] * self.n
        self.specs = [ANY] * self.n
        self.out_shape = [SDS((3,) + a.shape[1:] if cw is None else (3, a.shape[0], cw), a.dtype)
                          for a, cw in zip(self.args, self.cols)]
        self.sems = [pltpu.SemaphoreType.DMA((3 * self.n,)), pltpu.SemaphoreType.DMA((3 * self.n,))]

    def _copies(self, ins, outs, ssem, rsem):
        x, y, c = _my_xyc()

        def piece(k, q):
            cw = self.cols[k]
            return ins[k].at[q] if cw is None else ins[k].at[:, pl.ds(pl.multiple_of(q * cw, 128), cw)]

        return [pltpu.make_async_remote_copy(
            src_ref=piece(k, 2 * px + py), dst_ref=outs[k].at[j], send_sem=ssem.at[3 * k + j],
            recv_sem=rsem.at[3 * k + j], device_id=(px, py, c), device_id_type=MESH)
            for k in range(self.n) for j, (px, py) in enumerate(_chip_peers(x, y))]

    def start(self, ins, outs, sems):
        for cp in self._copies(ins, outs, *sems):
            cp.start()

    def wait(self, ins, outs, sems):
        for cp in self._copies(ins, outs, *sems):
            cp.wait()


class _Swap(_Scatter):
    def __init__(self, arrays):
        self.args = list(arrays)
        self.n = len(self.args)
        self.specs = [ANY] * self.n
        self.out_shape = [SDS(a.shape, a.dtype) for a in self.args]
        self.sems = [pltpu.SemaphoreType.DMA((self.n,)), pltpu.SemaphoreType.DMA((self.n,))]

    def _copies(self, ins, outs, ssem, rsem):
        x, y, c = _my_xyc()
        return [pltpu.make_async_remote_copy(src_ref=s_ref, dst_ref=o_ref, send_sem=ssem.at[k], recv_sem=rsem.at[k],
                                             device_id=(x, y, 1 - c), device_id_type=MESH)
                for k, (s_ref, o_ref) in enumerate(zip(ins, outs))]


def _exchange_call(ex, name):
    n = ex.n

    def body(*refs):
        ins, outs, sems = refs[:n], refs[n:2 * n], refs[2 * n:]
        ex.start(ins, outs, sems)
        ex.wait(ins, outs, sems)

    return pl.pallas_call(body, name=name, in_specs=ex.specs, out_specs=ex.specs, out_shape=ex.out_shape,
                          scratch_shapes=ex.sems)(*ex.args)


def _carried(ex, refs, n_in, n_out):
    n = ex.n
    ins, cin = refs[:n_in], refs[n_in:n_in + n]
    outs, cout = refs[n_in + n:n_in + n + n_out], refs[n_in + n + n_out:n_in + 2 * n + n_out]
    rest = refs[n_in + 2 * n + n_out:]
    nsem = len(ex.sems)
    return list(ins) + list(outs) + list(rest[:len(rest) - nsem]), (cin, cout, rest[len(rest) - nsem:])


def _gather_all(v, name):
    R, C = v.shape

    def body(s_ref, o_ref, ssem, rsem, lsem):
        x, y, c = _my_xyc()
        me = 4 * x + 2 * y + c
        mine = pltpu.make_async_copy(s_ref, o_ref.at[me], lsem)
        mine.start()

        def peer(j):
            fx, fy, fc = (j >> 2) & 1, (j >> 1) & 1, j & 1
            return (x ^ fx, y ^ fy, c ^ fc)

        cps = []
        for j in range(1, 8):
            cps.append(pltpu.make_async_remote_copy(
                src_ref=s_ref, dst_ref=o_ref.at[me], send_sem=ssem.at[j - 1], recv_sem=rsem.at[j - 1],
                device_id=peer(j), device_id_type=MESH))
            cps[-1].start()
        for j in range(1, 8):
            px, py, pc = peer(j)
            pltpu.make_async_remote_copy(
                src_ref=s_ref, dst_ref=o_ref.at[4 * px + 2 * py + pc], send_sem=ssem.at[j - 1],
                recv_sem=rsem.at[j - 1], device_id=peer(j), device_id_type=MESH).wait_recv()
        for cp in cps:
            cp.wait_send()
        mine.wait()

    return pl.pallas_call(
        body, name=name, in_specs=[ANY], out_specs=ANY, out_shape=SDS((8, R, C), v.dtype),
        scratch_shapes=[pltpu.SemaphoreType.DMA((7,)), pltpu.SemaphoreType.DMA((7,)), pltpu.SemaphoreType.DMA],
    )(v)


def _sum_slots(own, slots, name):
    n, R, C = slots.shape
    br = _block_rows(R)

    def body(*refs):
        o_ref = refs[-1]
        if own is None:
            acc = refs[0][0].astype(F32)
            first = 1
            s_ref = refs[0]
        else:
            acc = refs[0][...]
            first = 0
            s_ref = refs[1]
        for k in range(first, n):
            acc = acc + s_ref[k].astype(F32)
        o_ref[...] = acc

    row = BS((br, C), lambda i: (i, 0))
    slab = BS((n, br, C), lambda i: (0, i, 0))
    ins, args = ([slab], [slots]) if own is None else ([row, slab], [own, slots])
    return pl.pallas_call(
        body, name=name, grid=(R // br,), in_specs=ins, out_specs=row, out_shape=SDS((R, C), F32),
        compiler_params=_cp(("arbitrary",)),
    )(*args)


def _rope_tables(L, Lc):
    n = jnp.arange(L)
    row = (n // GRID_W).astype(F32)
    col = (n % GRID_W).astype(F32)
    freqs = ROPE_BASE ** (-jnp.arange(16, dtype=F32) / 16)
    lane = jnp.arange(128)
    dd = lane % HD
    fr = freqs[dd % 16]
    ang = jnp.where(dd < 32, row[:, None], col[:, None]) * fr[None, :]
    sign = jnp.where((dd % 32) < 16, -1.0, 1.0)
    cos = jnp.concatenate([jnp.ones((Lc, 128), F32), jnp.cos(ang)], axis=0)
    sin = jnp.concatenate([jnp.zeros((Lc, 128), F32), jnp.sin(ang) * sign[None, :]], axis=0)
    return cos, sin


def _ssm_prep(lam_re, lam_im, log_dt, b_re, b_im, c_re, c_im, B):
    dt = jnp.exp(log_dt)[..., None]
    mag = jnp.exp(lam_re * dt)
    ar = mag * jnp.cos(lam_im * dt)
    ai = mag * jnp.sin(lam_im * dt)
    den = lam_re * lam_re + lam_im * lam_im
    kr = ((ar - 1.0) * lam_re + ai * lam_im) / den
    ki = (ai * lam_re - (ar - 1.0) * lam_im) / den
    bbr = kr[..., None] * b_re - ki[..., None] * b_im
    bbi = kr[..., None] * b_im + ki[..., None] * b_re
    eye = jnp.eye(SGRP, dtype=F32)

    def bblk(t):
        return jnp.einsum("gpi,gh->gihp", t, eye).reshape(SGRP * SCH, SGRP * SST)

    def cblk(t):
        return jnp.einsum("gip,gh->gphi", t, eye).reshape(SGRP * SST, SGRP * SCH)

    b4 = jnp.concatenate([bblk(bbr[0]), bblk(bbi[0]), bblk(bbr[1]), bblk(bbi[1])], axis=1)
    c2 = jnp.concatenate([jnp.concatenate([cblk(c_re[0]), -cblk(c_im[0])], axis=0),
                          jnp.concatenate([cblk(c_re[1]), -cblk(c_im[1])], axis=0)], axis=1)
    a2 = jnp.concatenate([ar.reshape(2, -1), ai.reshape(2, -1)], axis=1)
    a = jnp.repeat(a2, B, axis=0)
    return a, b4, c2


def _pad_rows(v, rows=8):
    return jnp.concatenate([v, jnp.zeros((rows - v.shape[0],) + v.shape[1:], v.dtype)], axis=0)


_BIG = ("w_ada", "w_in", "w_out", "w_mlp_in", "w_mlp_out")
_BIG_AXIS = {"w_ada": 2, "w_in": 2, "w_out": 1, "w_mlp_in": 2, "w_mlp_out": 1}


def _join_shards(s, axis):
    _, L, r, c = s.shape
    if axis == 2:
        return jnp.transpose(s, (1, 2, 0, 3)).reshape(L, r, NCHIP * c)
    return jnp.transpose(s, (1, 0, 2, 3)).reshape(L, NCHIP * r, c)


def _local_step(x, c, ctx, loss_target, P, ex):
    B, L, _ = x.shape
    Lc = ctx.shape[1]
    S = Lc + L
    nct = Lc // TB
    nt = S // TB
    cos, sin = _rope_tables(L, Lc)
    perm = _scan_perm(B)
    pm, pmt = jnp.asarray(perm, MXU), jnp.asarray(perm.T, MXU)

    c_act = jax.nn.silu(c)
    cc_act, silu_vjp = jax.vjp(jax.nn.silu, P["c_ctx"])
    cact8 = _pad_rows(jnp.concatenate([c_act, cc_act[None, :]], axis=0))
    h = jnp.concatenate([ctx, x], axis=1)

    ssm_names = ("ssm_lam_re", "ssm_lam_im", "ssm_log_dt", "ssm_b_re", "ssm_b_im", "ssm_c_re", "ssm_c_im")
    (a_all, b4_all, c2_all), prep_vjp = jax.vjp(jax.vmap(lambda *t: _ssm_prep(*t, B)), *[P[n] for n in ssm_names])
    b4m_all, c2m_all = b4_all.astype(MXU), c2_all.astype(MXU)
    ng_all = jnp.pad(P["norm_g"], ((0, 0), (0, 4), (0, 0)))
    cw8_all = jnp.pad(P["conv_w"], ((0, 0), (0, 8 - P["conv_w"].shape[1]), (0, 0)))
    wg_all = P["w_glu"].astype(MXU)

    saved = []
    for l in range(NLAYER):
        mod8 = _ada_fwd(cact8, ex.w(l, "w_ada"), P["b_ada"][l][None, :])
        mod6 = mod8.reshape(8, NMOD, D)
        modtok = jnp.stack([jnp.broadcast_to(mod6[B], (B, NMOD, D)), mod6[:B]], axis=1)
        modtok = jnp.pad(modtok, ((0, 0), (0, 0), (0, 2), (0, 0)))
        ng, cw8, a_s, b4m, c2m, wg = ng_all[l], cw8_all[l], a_all[l], b4m_all[l], c2m_all[l], wg_all[l]
        dsk = P["ssm_d"][l][None, :]
        bg = P["b_glu"][l][None, :]

        p, a = ex.pre_mix(l, h, modtok, ng, ex.w(l, "w_in"), nct)
        q4, kr = _rope_fwd(p, cos, sin)
        (o,) = ex.attn_fwd(l, P["attn_sink"][l], q4, kr, p, Lc)
        cv = _conv_fwd(p, cw8, nct)
        yf, yb, hb = ex.scan_fwd(l, p, pm, pmt, b4m, a_s, c2m, Lc)
        h1, mix, m, y = ex.post_mix(l, o, cv, yf, yb, p, h, modtok, ng, dsk, wg, bg, ex.w(l, "w_out"), nct)
        h2, a2, z, f = ex.mlp_fwd(l, h1, modtok, ng, ex.w(l, "w_mlp_in"), ex.w(l, "w_mlp_out"), nct)
        saved.append(dict(h=h, modtok=modtok, ng=ng, cw8=cw8, a_s=a_s, b4m=b4m, c2m=c2m, wg=wg, dsk=dsk, bg=bg,
                          p=p, a=a, q4=q4, kr=kr, hb=hb, mix=mix, m=m, y=y, h1=h1, a2=a2, z=z, f=f))
        h = h2

    dh, lpart = _loss(h, loss_target, Lc)
    loss_local = 0.5 / D * jnp.sum(lpart[:, :, 0, 0])

    R = {k: [None] * NLAYER for k in ("w_glu", "vec_p2", "scan", "dcw", "dsink", "vec", "b_ada")}
    dcact = jnp.zeros((8, D), F32)
    T = B * S
    for l in reversed(range(NLAYER)):
        sv = saved[l]
        modtok, ng = sv["modtok"], sv["ng"]
        dh1, df, dz, vec_m = ex.mlp_bwd(l, dh, sv["h1"], sv["f"], sv["z"], modtok, ng, ex.w(l, "w_mlp_in"),
                                        ex.w(l, "w_mlp_out"), nct)
        ex.submit(l, "w_mlp_out", *_mm_tn(sv["z"].reshape(T, DFF), df.reshape(T, D), "dw_mlp_out", relu2=True))
        ex.submit(l, "w_mlp_in", *_mm_tn(sv["a2"].reshape(T, D), dz.reshape(T, DFF), "dw_mlp_in"))
        dm, dattn, dcv, dy, du_skip, vec_p, vec_p2, dwg = _post_bwd(
            dh1, sv["m"], sv["y"], sv["p"], modtok, ng, sv["dsk"], sv["wg"], sv["bg"], ex.w(l, "w_out"), nct)
        ex.submit(l, "w_out", *_mm_tn(sv["mix"].reshape(T, D), dm.reshape(T, D), "dw_out"))
        R["w_glu"][l], R["vec_p2"][l] = dwg, vec_p2
        duf, dub, db4, da_s, dc2 = ex.scan_bwd(l, sv["p"], dy, sv["hb"], pm, pmt, sv["b4m"], sv["a_s"], sv["c2m"],
                                               Lc)
        R["scan"][l] = (da_s, db4, dc2)
        dconv, R["dcw"][l] = _conv_bwd(sv["p"], dcv, sv["cw8"], nct)
        dq_r, dk_r, dv, R["dsink"][l] = ex.attn_bwd(l, P["attn_sink"][l], sv["q4"], sv["kr"], sv["p"], dattn, Lc)
        dqk = _rope_bwd(dq_r, dk_r, cos, -sin)
        dh, vec_i, dp = _pre_bwd(dqk, dv, dconv, (du_skip, duf, dub), dh1, sv["h"], modtok, ng, ex.w(l, "w_in"),
                                 nct)
        ex.submit(l, "w_in", *_mm_tn(sv["a"].reshape(T, D), dp.reshape(T, INW), "dw_in"))
        vec = jnp.concatenate([vec_i, vec_p, vec_m], axis=2)
        R["vec"][l] = vec
        mod_rows = jnp.concatenate([vec[:, :, 0:2], vec[:, :, 8:9], vec[:, :, 16:18], vec[:, :, 19:20]], axis=2)
        dmod8 = jnp.concatenate([jnp.sum(mod_rows[:, nct:], axis=1).reshape(B, NMOD * D),
                                 jnp.sum(mod_rows[:, :nct], axis=(0, 1)).reshape(1, NMOD * D),
                                 jnp.zeros((7 - B, NMOD * D), F32)], axis=0)
        R["b_ada"][l] = dmod8
        dmod8m = dmod8.astype(MXU)
        ex.submit(l, "w_ada", *_mm_tn(cact8.astype(MXU), dmod8m, "dw_ada"))
        dcact = dcact + _ada_bwd_c(dmod8m, ex.w(l, "w_ada"))

    stk = lambda k: jnp.stack(R[k])
    vec, vec_p2 = jnp.sum(stk("vec"), axis=(1, 2)), jnp.sum(stk("vec_p2"), axis=(1, 2))
    grads = dict(zip(ssm_names, prep_vjp(tuple(jnp.stack([R["scan"][l][k] for l in range(NLAYER)])
                                               for k in range(3)))))
    norm_rows = jnp.concatenate([vec[:, 2:3], vec[:, 9:10], vec[:, 18:19], vec[:, 20:21]], axis=1)
    grads.update(w_glu=stk("w_glu"), ssm_d=vec_p2[:, 0], b_glu=vec_p2[:, 1], norm_g=norm_rows,
                 conv_w=jnp.sum(stk("dcw"), axis=(1, 2))[:, 0:P["conv_w"].shape[1]],
                 attn_sink=jnp.sum(stk("dsink")[..., 0], axis=1), b_ada=jnp.sum(stk("b_ada"), axis=1),
                 c_ctx=silu_vjp(dcact[B])[0])
    return loss_local, dh[:, Lc:, :], grads


_WEIGHTS = ["c_ctx", "w_ada", "b_ada", "norm_g", "w_in", "conv_w", "attn_sink", "ssm_lam_re", "ssm_lam_im",
            "ssm_log_dt", "ssm_b_re", "ssm_b_im", "ssm_c_re", "ssm_c_im", "ssm_d", "w_glu", "b_glu", "w_out",
            "w_mlp_in", "w_mlp_out"]
_SMALL = [n for n in _WEIGHTS if n not in _BIG]
_SMALL_SHARDED = {"norm_g": 2, "conv_w": 2, "w_glu": 1}
_LANES = 1024


_GROUPS = (("w_ada", "w_in", "w_out"), ("w_mlp_in", "w_mlp_out"))
_QUEUE_OF = {"w_ada": "narrow", "w_in": "narrow", "w_out": "out", "w_mlp_in": "mlp", "w_mlp_out": "mlp"}


class _Exchange:
    def __init__(self, W, q_me):
        self.W, self.q_me = W, q_me
        self.wall, self.own, self.recv, self.pending = {}, {}, {}, {}
        first = ("w_ada", "w_in")
        self._keep(0, first, _exchange_call(_Gather2([self._shard(0, n) for n in first]), "gather_big"))

    def _shard(self, l, n):
        return self.W[n][l].astype(MXU)

    def _keep(self, l, names, walls):
        for n, w in zip(names, walls):
            self.wall[l, n] = w

    def w(self, l, n):
        g = self.wall[l, n]
        if n in ("w_ada", "w_mlp_in"):
            return g
        return _join_shards(g[:, None], 2)[0] if n == "w_in" else g.reshape(-1, g.shape[-1])

    def _gathering(self, l, names, fn, *args):
        if l == NLAYER or not names:
            return fn(*args)[0]
        res, walls = fn(*args, ex=_Gather([self._shard(l, n) for n in names]))
        self._keep(l, names, walls)
        return res

    def pre_mix(self, l, *args):
        return self._gathering(l, ("w_out",) if l == 0 else (), _pre_mix, *args)

    def scan_fwd(self, l, *args):
        return self._gathering(l, ("w_mlp_in",) if l == 0 else (), _scan_fwd, *args)

    def post_mix(self, l, *args):
        return self._gathering(l, ("w_mlp_out",) if l == 0 else (), _post_mix, *args)

    def attn_fwd(self, l, *args):
        return self._gathering(l + 1, _GROUPS[0], _attn_fwd, *args)

    def mlp_fwd(self, l, *args):
        return self._gathering(l + 1, _GROUPS[1], _mlp_fwd, *args)

    def submit(self, l, n, g, gm):
        r, c = g.shape
        if _BIG_AXIS[n] == 2:
            cw = c // NCHIP
            self.own[l, n] = lax.dynamic_slice_in_dim(g, self.q_me * cw, cw, axis=1)
            if cw % 128 == 0:
                item = (gm, cw)
            else:
                item = (jnp.transpose(gm.reshape(r, NCHIP, cw), (1, 0, 2)), None)
        else:
            self.own[l, n] = lax.dynamic_slice_in_dim(g, self.q_me * (r // NCHIP), r // NCHIP, axis=0)
            item = (gm.reshape(NCHIP, r // NCHIP, c), None)
        self.pending.setdefault(_QUEUE_OF[n], []).append((l, n) + item)

    def _scatter_of(self, items):
        return _Scatter([it[2] for it in items], [it[3] for it in items])

    def _scattering(self, queue, fn, *args):
        items = self.pending.pop(queue, [])
        if not items:
            return fn(*args)[0]
        res, recvs = fn(*args, ex=self._scatter_of(items))
        for it, r in zip(items, recvs):
            self.recv[it[0], it[1]] = r
        return res

    def mlp_bwd(self, l, *args):
        return self._scattering("narrow", _mlp_bwd, *args)

    def scan_bwd(self, l, *args):
        return self._scattering("mlp", _scan_bwd, *args)

    def attn_bwd(self, l, *args):
        return self._scattering("out", _attn_bwd, *args)

    def finish(self):
        items = [it for queue in sorted(self.pending) for it in self.pending[queue]]
        self.pending = {}
        for it, r in zip(items, _exchange_call(self._scatter_of(items), "scatter_big")):
            self.recv[it[0], it[1]] = r
        part = [jnp.concatenate([_sum_slots(self.own[l, n], self.recv[l, n], "sum_chips") for l in range(NLAYER)],
                                axis=0) for n in _BIG]
        sib = _exchange_call(_Swap(part), "swap_big")
        return {n: (p, s) for n, p, s in zip(_BIG, part, sib)}


def _part_rows(shape):
    return -(-math.prod(shape) // (16 * _LANES)) * 16


def _flat_pad(parts):
    rows = []
    for p in parts:
        v = p.reshape(-1)
        pad = _part_rows(p.shape) * _LANES - v.shape[0]
        rows.append(jnp.concatenate([v, jnp.zeros((pad,), v.dtype)]).reshape(-1, _LANES))
    used = sum(r.shape[0] for r in rows)
    rows.append(jnp.zeros((-used % _PAD_ROWS, _LANES), rows[0].dtype))
    return jnp.concatenate(rows, axis=0)


def _unflat(flat, shapes):
    out, r0 = [], 0
    for s in shapes:
        nr = _part_rows(s)
        out.append(flat[r0:r0 + nr].reshape(-1)[:math.prod(s)].reshape(s))
        r0 += nr
    return out


def kernel(x, c, ctx, c_ctx, w_ada, b_ada, norm_g, w_in, conv_w, attn_sink, ssm_lam_re, ssm_lam_im, ssm_log_dt, ssm_b_re, ssm_b_im, ssm_c_re, ssm_c_im, ssm_d, w_glu, b_glu, w_out, w_mlp_in, w_mlp_out, loss_target, m_c_ctx, m_w_ada, m_b_ada, m_norm_g, m_w_in, m_conv_w, m_attn_sink, m_ssm_lam_re, m_ssm_lam_im, m_ssm_log_dt, m_ssm_b_re, m_ssm_b_im, m_ssm_c_re, m_ssm_c_im, m_ssm_d, m_w_glu, m_b_glu, m_w_out, m_w_mlp_in, m_w_mlp_out, v_c_ctx, v_w_ada, v_b_ada, v_norm_g, v_w_in, v_conv_w, v_attn_sink, v_ssm_lam_re, v_ssm_lam_im, v_ssm_log_dt, v_ssm_b_re, v_ssm_b_im, v_ssm_c_re, v_ssm_c_im, v_ssm_d, v_w_glu, v_b_glu, v_w_out, v_w_mlp_in, v_w_mlp_out):
    W = dict(c_ctx=c_ctx, w_ada=w_ada, b_ada=b_ada, norm_g=norm_g, w_in=w_in, conv_w=conv_w, attn_sink=attn_sink,
             ssm_lam_re=ssm_lam_re, ssm_lam_im=ssm_lam_im, ssm_log_dt=ssm_log_dt, ssm_b_re=ssm_b_re,
             ssm_b_im=ssm_b_im, ssm_c_re=ssm_c_re, ssm_c_im=ssm_c_im, ssm_d=ssm_d, w_glu=w_glu, b_glu=b_glu,
             w_out=w_out, w_mlp_in=w_mlp_in, w_mlp_out=w_mlp_out)
    M = dict(c_ctx=m_c_ctx, w_ada=m_w_ada, b_ada=m_b_ada, norm_g=m_norm_g, w_in=m_w_in, conv_w=m_conv_w,
             attn_sink=m_attn_sink, ssm_lam_re=m_ssm_lam_re, ssm_lam_im=m_ssm_lam_im, ssm_log_dt=m_ssm_log_dt,
             ssm_b_re=m_ssm_b_re, ssm_b_im=m_ssm_b_im, ssm_c_re=m_ssm_c_re, ssm_c_im=m_ssm_c_im, ssm_d=m_ssm_d,
             w_glu=m_w_glu, b_glu=m_b_glu, w_out=m_w_out, w_mlp_in=m_w_mlp_in, w_mlp_out=m_w_mlp_out)
    V = dict(c_ctx=v_c_ctx, w_ada=v_w_ada, b_ada=v_b_ada, norm_g=v_norm_g, w_in=v_w_in, conv_w=v_conv_w,
             attn_sink=v_attn_sink, ssm_lam_re=v_ssm_lam_re, ssm_lam_im=v_ssm_lam_im, ssm_log_dt=v_ssm_log_dt,
             ssm_b_re=v_ssm_b_re, ssm_b_im=v_ssm_b_im, ssm_c_re=v_ssm_c_re, ssm_c_im=v_ssm_c_im, ssm_d=v_ssm_d,
             w_glu=v_w_glu, b_glu=v_b_glu, w_out=v_w_out, w_mlp_in=v_w_mlp_in, w_mlp_out=v_w_mlp_out)
    q_me = 2 * lax.axis_index("x") + lax.axis_index("y")

    ex = _Exchange(W, q_me)
    P = {n: W[n] for n in _SMALL}
    ssh_names = list(_SMALL_SHARDED)
    ssh = _flat_pad([W[n] for n in ssh_names])
    (sall,) = _exchange_call(_Gather([ssh]), "gather_small")
    parts = [_unflat(sall[q], [W[n].shape for n in ssh_names]) for q in range(NCHIP)]
    for k, n in enumerate(ssh_names):
        P[n] = _join_shards(jnp.stack([parts[q][k] for q in range(NCHIP)]), _SMALL_SHARDED[n])

    loss_local, grad_x, G = _local_step(x, c, ctx, loss_target, P, ex)
    loss = lax.psum(loss_local, ("x", "y", "c"))

    sums = ex.finish()
    out = {}
    for n in _BIG:
        flat = lambda t: t.reshape(-1, t.shape[-1])
        res = _adamw(flat(W[n]), *sums[n], flat(M[n]), flat(V[n]), "adamw_" + n)
        out[n] = [t.reshape(W[n].shape) for t in res]

    gsmall = _flat_pad([G[n] for n in _SMALL]).astype(MXU)
    gsum = _sum_slots(None, _gather_all(gsmall, "gather_grads"), "sum_devices")
    gfull = dict(zip(_SMALL, _unflat(gsum, [G[n].shape for n in _SMALL])))
    for n, ax in _SMALL_SHARDED.items():
        width = W[n].shape[ax]
        gfull[n] = lax.dynamic_slice_in_dim(gfull[n], q_me * width, width, axis=ax)
    shapes = [W[n].shape for n in _SMALL]
    gflat = _flat_pad([gfull[n] for n in _SMALL])
    res = _adamw(_flat_pad([W[n] for n in _SMALL]), gflat, jnp.zeros_like(gflat),
                 _flat_pad([M[n] for n in _SMALL]), _flat_pad([V[n] for n in _SMALL]), "adamw_small")
    for k, t in enumerate(res):
        for n, piece in zip(_SMALL, _unflat(t, shapes)):
            out.setdefault(n, [None] * 4)[k] = piece

    return (loss, grad_x, *[out[n][0] for n in _WEIGHTS], *[out[n][1] for n in _WEIGHTS],
            *[out[n][2] for n in _WEIGHTS], *[out[n][3] for n in _WEIGHTS])
```

```python
import math

import jax
import jax.numpy as jnp
import numpy as np
from jax import lax
from jax.experimental import pallas as pl
from jax.experimental.pallas import tpu as pltpu

F32 = jnp.float32
MXU = jnp.bfloat16

D = 1024
DFF = 4096
NMOD = 6
EPS = 1e-6
HD = 64
NQ = 8
NKV = 2
QPK = 4
AW = 512
KVW = 128
WIN = 128
QB = 128
CW = 256
SW = 256
SGRP = 16
SCH = 16
SST = 64
INW = 1792
GRID_W = 64
ROPE_BASE = 10000.0
NEG = -1e30
SCALE = HD ** -0.5
NLAYER = 4

TB = 256
TF = 1024
TC = 32
NCHIP = 4

LR, B1, B2, AEPS, WD, STEP = 0.001, 0.9, 0.999, 1e-08, 0.01, 10

MESH = pl.DeviceIdType.MESH
SDS = jax.ShapeDtypeStruct
BS = pl.BlockSpec
ANY = pl.BlockSpec(memory_space=pl.ANY)


def _cp(sem, vmem_mb=None):
    kw = dict(dimension_semantics=sem)
    if vmem_mb is not None:
        kw["vmem_limit_bytes"] = vmem_mb * 1024 * 1024
    return pltpu.CompilerParams(**kw)


def _dot(a, b):
    return jnp.dot(a, b, preferred_element_type=F32)


def _dot_nt(a, b):
    return lax.dot_general(a, b, (((1,), (1,)), ((), ())), preferred_element_type=F32)


def _dot_tn(a, b):
    return lax.dot_general(a, b, (((0,), (0,)), ((), ())), preferred_element_type=F32)


def _rms(x):
    r = lax.rsqrt(jnp.mean(x * x, axis=-1, keepdims=True) + EPS)
    return x * r, r


def _rms_bwd(dn, n, r):
    return r * (dn - n * jnp.mean(dn * n, axis=-1, keepdims=True))


_GC = math.sqrt(2.0 / math.pi)


def _gelu(y):
    return 0.5 * y * (1.0 + jnp.tanh(_GC * (y + 0.044715 * y * y * y)))


def _gelu_grad(y):
    th = jnp.tanh(_GC * (y + 0.044715 * y * y * y))
    return 0.5 * (1.0 + th) + 0.5 * y * (1.0 - th * th) * _GC * (1.0 + 3 * 0.044715 * y * y)


def _sigmoid(v):
    return 1.0 / (1.0 + jnp.exp(-v))


def _colsum(x):
    return jnp.sum(x, axis=0, keepdims=True)


def _tok(width, col=0):
    return BS((None, TB, width), lambda b, j: (b, j, col))


def _mod_spec(nct):
    return BS((None, None, 8, D), lambda b, j: (b, jnp.where(j >= nct, 1, 0), 0, 0))


def _full(shape):
    nd = len(shape)
    return BS(shape, lambda *a: (0,) * nd)


def _resident(shape):
    nd = len(shape)
    return BS(shape, lambda *a: (0,) * nd, pipeline_mode=pl.Buffered(1))


def _ada_fwd(cact8, w4, b):
    wq = w4.shape[2]

    def body(c_ref, w_ref, b_ref, o_ref):
        o_ref[...] = _dot(c_ref[...].astype(MXU), w_ref[...]) + b_ref[...]

    return pl.pallas_call(
        body, name="ada_fwd", grid=(NCHIP,),
        in_specs=[BS((8, D), lambda j: (0, 0)), BS((None, D, wq), lambda j: (j, 0, 0)), BS((1, wq), lambda j: (0, j))],
        out_specs=BS((8, wq), lambda j: (0, j)),
        out_shape=SDS((8, NMOD * D), F32), compiler_params=_cp(("arbitrary",)),
    )(cact8, w4, b)


def _ada_bwd_c(dmod8, w4):
    wq = w4.shape[2]

    def body(d_ref, w_ref, o_ref):
        @pl.when(pl.program_id(0) == 0)
        def _():
            o_ref[...] = jnp.zeros_like(o_ref)
        o_ref[...] += _dot_nt(d_ref[...], w_ref[...])

    return pl.pallas_call(
        body, name="ada_bwd_c", grid=(NCHIP,),
        in_specs=[BS((8, wq), lambda j: (0, j)), BS((None, D, wq), lambda j: (j, 0, 0))],
        out_specs=BS((8, D), lambda j: (0, 0)),
        out_shape=SDS((8, D), F32), compiler_params=_cp(("arbitrary",)),
    )(dmod8, w4)


def _pre_mix(h, modtok, ng, w_in, nct, ex=None):
    B, S, _ = h.shape

    def body(h_ref, mod_ref, g_ref, w_ref, p_ref, a_ref):
        n, _ = _rms(h_ref[...])
        a = ((n * g_ref[0:1, :]) * (1.0 + mod_ref[1:2, :]) + mod_ref[0:1, :]).astype(MXU)
        a_ref[...] = a
        p_ref[...] = _dot(a, w_ref[...])

    return _call_with(
        ex, body, "pre_mix", (B, S // TB),
        [_tok(D), _mod_spec(nct), _full((8, D)), _full((D, INW))], [_tok(INW), _tok(D)],
        [SDS((B, S, INW), F32), SDS((B, S, D), MXU)], [h, modtok, ng, w_in], [],
        _cp(("arbitrary", "arbitrary"), 40))


def _pre_bwd(dqk, dv, dcv, du_parts, dh_part, h, modtok, ng, w_in, nct):
    B, S, _ = h.shape

    def body(dqk_ref, dv_ref, dcv_ref, du0_ref, du1_ref, du2_ref, dhp_ref, h_ref, mod_ref, g_ref, w_ref,
             dh_ref, vec_ref, dp_ref):
        du = du0_ref[...] + du1_ref[...] + du2_ref[...]
        dp = jnp.concatenate([dqk_ref[...], dv_ref[...].astype(MXU), dcv_ref[...].astype(MXU), du.astype(MXU)],
                             axis=1)
        dp_ref[...] = dp
        da = _dot_nt(dp, w_ref[...])
        n, r = _rms(h_ref[...])
        g = g_ref[0:1, :]
        sc1 = 1.0 + mod_ref[1:2, :]
        vec_ref[...] = jnp.zeros_like(vec_ref)
        vec_ref[0:1, :] = _colsum(da)
        vec_ref[1:2, :] = _colsum(da * (n * g))
        vec_ref[2:3, :] = _colsum(da * sc1 * n)
        dh_ref[...] = dhp_ref[...] + _rms_bwd(da * sc1 * g, n, r)

    return pl.pallas_call(
        body, name="pre_bwd", grid=(B, S // TB),
        in_specs=[_tok(AW + KVW), _tok(KVW), _tok(3 * CW), _tok(SW), _tok(SW), _tok(SW), _tok(D), _tok(D),
                  _mod_spec(nct), _full((8, D)), _full((D, INW))],
        out_specs=[_tok(D), BS((None, None, 8, D), lambda b, j: (b, j, 0, 0)), _tok(INW)],
        out_shape=[SDS((B, S, D), F32), SDS((B, S // TB, 8, D), F32), SDS((B, S, INW), MXU)],
        compiler_params=_cp(("arbitrary", "arbitrary"), 40),
    )(dqk, dv, dcv, *du_parts, dh_part, h, modtok, ng, w_in)


def _rotate(t, c, s):
    lane = lax.broadcasted_iota(jnp.int32, t.shape, 1)
    partner = jnp.where((lane % 32) < 16, pltpu.roll(t, 112, 1), pltpu.roll(t, 16, 1))
    return t * c + partner * s


def _to_groups(cols):
    lo = lax.broadcasted_iota(jnp.int32, cols[0].shape, 1) < HD
    out = []
    for g in range(QPK):
        a, b = cols[g // 2], cols[2 + g // 2]
        out.append(jnp.where(lo, a, pltpu.roll(b, HD, 1)) if g % 2 == 0 else jnp.where(lo, pltpu.roll(a, HD, 1), b))
    return out


def _from_groups(tiles):
    lo = lax.broadcasted_iota(jnp.int32, tiles[0].shape, 1) < HD
    out = []
    for k in range(NQ // 2):
        a, b = tiles[(2 * k) % QPK], tiles[(2 * k) % QPK + 1]
        out.append(jnp.where(lo, a, pltpu.roll(b, HD, 1)) if k < 2 else jnp.where(lo, pltpu.roll(a, HD, 1), b))
    return out


def _grp(width=128):
    return BS((None, QPK, TB, width), lambda b, j: (b, 0, j, 0))


_TAB = BS((TB, 128), lambda b, j: (j, 0))


def _rope_fwd(p, cos, sin):
    B, S, _ = p.shape

    def body(x_ref, c_ref, s_ref, q_ref, k_ref):
        c, s = c_ref[...], s_ref[...]
        rot = [_rotate(x_ref[:, k * 128:(k + 1) * 128], c, s) for k in range((AW + KVW) // 128)]
        k_ref[...] = rot[AW // 128].astype(MXU)
        for g, t in enumerate(_to_groups(rot[:AW // 128])):
            q_ref[g] = (t * SCALE).astype(MXU)

    return pl.pallas_call(
        body, name="rope_fwd", grid=(B, S // TB),
        in_specs=[_tok(AW + KVW), _TAB, _TAB],
        out_specs=[_grp(), _tok(KVW)],
        out_shape=[SDS((B, QPK, S, 128), MXU), SDS((B, S, KVW), MXU)],
        compiler_params=_cp(("arbitrary", "arbitrary")),
    )(p, cos, sin)


def _rope_bwd(dq4, dk, cos, nsin):
    B, _, S, _ = dq4.shape

    def body(q_ref, k_ref, c_ref, s_ref, o_ref):
        c, s = c_ref[...], s_ref[...]
        cols = _from_groups([q_ref[g] * SCALE for g in range(QPK)]) + [k_ref[...]]
        for k, t in enumerate(cols):
            o_ref[:, k * 128:(k + 1) * 128] = _rotate(t, c, s).astype(MXU)

    return pl.pallas_call(
        body, name="rope_bwd", grid=(B, S // TB),
        in_specs=[_grp(), _tok(KVW), _TAB, _TAB],
        out_specs=_tok(AW + KVW),
        out_shape=SDS((B, S, AW + KVW), MXU),
        compiler_params=_cp(("arbitrary", "arbitrary")),
    )(dq4, dk, cos, nsin)


def _window_start(i, S):
    return pl.multiple_of(jnp.clip((i - 1) * QB, 0, S - 3 * QB), QB)


def _attn_bias(S, Lc):
    nb = S // QB
    i = np.arange(nb)[:, None, None]
    w0 = np.clip((i - 1) * QB, 0, S - 3 * QB)
    qpos = i * QB + np.arange(QB)[None, :, None]
    kpos = w0 + np.arange(3 * QB)[None, None, :]
    ok = (i >= Lc // QB) & (kpos >= Lc) & (np.abs(qpos - kpos) <= WIN)
    return jnp.asarray(np.where(ok, 0.0, NEG), F32)


_BIAS = BS((None, QB, 3 * QB), lambda b, i: (i, 0, 0))


def _attn_probs(qh, kch, kwh, bias, sk):
    sc = _dot_nt(qh, kch)
    sw = _dot_nt(qh, kwh) + jnp.concatenate([bias] * (qh.shape[0] // QB), axis=0)
    m = jnp.maximum(jnp.maximum(jnp.max(sc, axis=-1, keepdims=True), jnp.max(sw, axis=-1, keepdims=True)), sk)
    ec = jnp.exp(sc - m)
    ew = jnp.exp(sw - m)
    es = jnp.exp(sk - m)
    inv = 1.0 / (jnp.sum(ec, axis=-1, keepdims=True) + jnp.sum(ew, axis=-1, keepdims=True) + es)
    return ec * inv, ew * inv, es * inv


def _sink_col(sink_ref, hk):
    return jnp.concatenate([jnp.full((QB, 1), sink_ref[hk * QPK + g], F32) for g in range(QPK)], axis=0)


_QGRP = BS((None, QPK, QB, 128), lambda b, i: (b, 0, i, 0))


def _first_last(grid):
    first = last = None
    for ax, n in enumerate(grid):
        i = pl.program_id(ax)
        first = (i == 0) if first is None else jnp.logical_and(first, i == 0)
        last = (i == n - 1) if last is None else jnp.logical_and(last, i == n - 1)
    return first, last


def _call_with(ex, body, name, grid, in_specs, out_specs, out_shape, args, scratch, params):
    n_in, n_out = len(in_specs), len(out_specs)
    if ex is None:
        res = pl.pallas_call(body, name=name, grid=grid, in_specs=in_specs, out_specs=out_specs,
                             out_shape=out_shape, scratch_shapes=scratch, compiler_params=params)(*args)
        return list(res), []

    def carrying(*refs):
        own, comm = _carried(ex, refs, n_in, n_out)
        first, last = _first_last(grid)
        pl.when(first)(lambda: ex.start(*comm))
        body(*own)
        pl.when(last)(lambda: ex.wait(*comm))

    res = pl.pallas_call(
        carrying, name=f"{name}_{type(ex).__name__.strip('_').lower()}", grid=grid,
        in_specs=in_specs + ex.specs, out_specs=out_specs + ex.specs, out_shape=out_shape + ex.out_shape,
        scratch_shapes=scratch + ex.sems, compiler_params=params)(*args, *ex.args)
    return list(res[:n_out]), list(res[n_out:])


def _attn_fwd(sink, q4, k, p, bias, Lc, ex=None):
    B, S, _ = k.shape

    def body(sink_ref, q_ref, k_ref, v_ref, bias_ref, o_ref):
        w0, bias = _window_start(pl.program_id(1), S), bias_ref[...]
        kc = k_ref[0:Lc, :]
        vc = v_ref[0:Lc, :].astype(MXU)
        kw = k_ref[pl.ds(w0, 3 * QB), :]
        vw = v_ref[pl.ds(w0, 3 * QB), :].astype(MXU)
        q = q_ref[...].reshape(QPK * QB, 128)
        half = QPK * QB // 2
        outs = []
        for hk in range(NKV):
            cs = slice(hk * HD, (hk + 1) * HD)
            sk = _sink_col(sink_ref, hk)
            for r0 in (0, half):
                rows = slice(r0, r0 + half)
                pc, pw, _ = _attn_probs(q[rows, cs], kc[:, cs], kw[:, cs], bias, sk[rows])
                o = _dot(pc.astype(MXU), vc[:, cs]) + _dot(pw.astype(MXU), vw[:, cs])
                outs += [o[g * QB:(g + 1) * QB] for g in range(half // QB)]
        o_ref[...] = jnp.concatenate(outs, axis=1).astype(o_ref.dtype)

    return _call_with(
        ex, body, "attn_fwd", (B, S // QB),
        [BS(memory_space=pltpu.SMEM), _QGRP, BS((None, S, KVW), lambda b, i: (b, 0, 0)),
         BS((None, S, KVW), lambda b, i: (b, 0, (AW + KVW) // KVW)), _BIAS],
        [BS((None, QB, AW), lambda b, i: (b, i, 0))], [SDS((B, S, AW), MXU)], [sink, q4, k, p, bias], [],
        _cp(("arbitrary", "arbitrary"), 40))


def _attn_bwd(sink, q4, k, p, do4, bias, Lc, ex=None):
    B, S, _ = k.shape

    def body(sink_ref, q_ref, k_ref, v_ref, do_ref, bias_ref, dq_ref, dk_ref, dv_ref, ds_ref):
        i = pl.program_id(1)

        @pl.when(i == 0)
        def _():
            dk_ref[...] = jnp.zeros_like(dk_ref)
            dv_ref[...] = jnp.zeros_like(dv_ref)
            ds_ref[...] = jnp.zeros_like(ds_ref)

        w0, bias = _window_start(i, S), bias_ref[...]
        kc = k_ref[0:Lc, :]
        vc = v_ref[0:Lc, :].astype(MXU)
        kw = k_ref[pl.ds(w0, 3 * QB), :]
        vw = v_ref[pl.ds(w0, 3 * QB), :].astype(MXU)
        q = q_ref[...].reshape(QPK * QB, 128)
        do = do_ref[...].reshape(QPK * QB, 128)
        dqs, dsinks, dkcs, dkws, dvcs, dvws = [], [], [], [], [], []
        for hk in range(NKV):
            cs = slice(hk * HD, (hk + 1) * HD)
            qh = q[:, cs]
            pc, pw, ps = _attn_probs(qh, kc[:, cs], kw[:, cs], bias, _sink_col(sink_ref, hk))
            dob = do[:, cs].astype(MXU)
            dpc = _dot_nt(dob, vc[:, cs])
            dpw = _dot_nt(dob, vw[:, cs])
            delta = jnp.sum(pc * dpc, axis=-1, keepdims=True) + jnp.sum(pw * dpw, axis=-1, keepdims=True)
            dsc = (pc * (dpc - delta)).astype(MXU)
            dsw = (pw * (dpw - delta)).astype(MXU)
            dqs.append(_dot(dsc, kc[:, cs]) + _dot(dsw, kw[:, cs]))
            dkcs.append(_dot_tn(dsc, qh))
            dkws.append(_dot_tn(dsw, qh))
            dvcs.append(_dot_tn(pc.astype(MXU), dob))
            dvws.append(_dot_tn(pw.astype(MXU), dob))
            psd = ps * delta
            dsinks += [jnp.broadcast_to(-jnp.sum(psd[g * QB:(g + 1) * QB], axis=0, keepdims=True), (1, 128))
                       for g in range(QPK)]
        for g in range(QPK):
            dq_ref[g] = jnp.concatenate([dq[g * QB:(g + 1) * QB] for dq in dqs], axis=1)
        ds_ref[...] += jnp.concatenate(dsinks, axis=0)
        dk_ref[0:Lc, :] += jnp.concatenate(dkcs, axis=1)
        dv_ref[0:Lc, :] += jnp.concatenate(dvcs, axis=1)
        dk_ref[pl.ds(w0, 3 * QB), :] += jnp.concatenate(dkws, axis=1)
        dv_ref[pl.ds(w0, 3 * QB), :] += jnp.concatenate(dvws, axis=1)

    kv_acc = BS((None, S, KVW), lambda b, i: (b, 0, 0))
    return _call_with(
        ex, body, "attn_bwd", (B, S // QB),
        [BS(memory_space=pltpu.SMEM), _QGRP, BS((None, S, KVW), lambda b, i: (b, 0, 0)),
         BS((None, S, KVW), lambda b, i: (b, 0, (AW + KVW) // KVW)), _QGRP, _BIAS],
        [_QGRP, kv_acc, kv_acc, BS((None, 8, 128), lambda b, i: (b, 0, 0))],
        [SDS((B, QPK, S, 128), F32), SDS((B, S, KVW), F32), SDS((B, S, KVW), F32), SDS((B, 8, 128), F32)],
        [sink, q4, k, p, do4, bias], [], _cp(("arbitrary", "arbitrary"), 48))


def _halo_specs(width, col, S):
    nb8 = S // 8
    per = TB // 8
    prev = BS((None, 8, width), lambda b, j: (b, jnp.maximum(j * per - 1, 0), col))
    nxt = BS((None, 8, width), lambda b, j: (b, jnp.minimum((j + 1) * per, nb8 - 1), col))
    return prev, nxt


def _shift_dn(z, prev_row, at_start):
    row = lax.broadcasted_iota(jnp.int32, z.shape, 0)
    first = jnp.where(at_start, 0.0, 1.0) * prev_row
    return jnp.where(row == 0, first, pltpu.roll(z, 1, 0))


def _shift_up(z, next_row, at_end):
    row = lax.broadcasted_iota(jnp.int32, z.shape, 0)
    last = jnp.where(at_end, 0.0, 1.0) * next_row
    return jnp.where(row == z.shape[0] - 1, last, pltpu.roll(z, z.shape[0] - 1, 0))


def _conv_fwd(p, cw8, nct):
    B, S, _ = p.shape
    nt = S // TB

    def body(p_ref, pp_ref, pn_ref, w_ref, o_ref):
        j = pl.program_id(1)
        at_start = jnp.logical_or(j == 0, j == nct)
        at_end = jnp.logical_or(j == nct - 1, j == nt - 1)
        z = p_ref[:, 256:512] * p_ref[:, 512:768]
        zprev = pp_ref[7:8, 256:512] * pp_ref[7:8, 512:768]
        znext = pn_ref[0:1, 256:512] * pn_ref[0:1, 512:768]
        c3 = (_shift_dn(z, zprev, at_start) * w_ref[0:1, :] + z * w_ref[1:2, :]
              + _shift_up(z, znext, at_end) * w_ref[2:3, :])
        o_ref[...] = p_ref[:, 0:256] * c3

    prev, nxt = _halo_specs(3 * CW, 1, S)
    return pl.pallas_call(
        body, name="conv_fwd", grid=(B, nt),
        in_specs=[_tok(3 * CW, 1), prev, nxt, _full((8, CW))],
        out_specs=_tok(CW),
        out_shape=SDS((B, S, CW), F32),
        compiler_params=_cp(("arbitrary", "arbitrary")),
    )(p, p, p, cw8)


def _conv_bwd(p, dcv, cw8, nct):
    B, S, _ = p.shape
    nt = S // TB

    def body(p_ref, pp_ref, pn_ref, d_ref, dp_ref, dn_ref, w_ref, o_ref, dw_ref):
        j = pl.program_id(1)
        at_start = jnp.logical_or(j == 0, j == nct)
        at_end = jnp.logical_or(j == nct - 1, j == nt - 1)
        cb, cc, cx = p_ref[:, 0:256], p_ref[:, 256:512], p_ref[:, 512:768]
        z = cc * cx
        zm = _shift_dn(z, pp_ref[7:8, 256:512] * pp_ref[7:8, 512:768], at_start)
        zp = _shift_up(z, pn_ref[0:1, 256:512] * pn_ref[0:1, 512:768], at_end)
        d = d_ref[...]
        e = d * cb
        em = _shift_dn(e, dp_ref[7:8, :] * pp_ref[7:8, 0:256], at_start)
        ep = _shift_up(e, dn_ref[0:1, :] * pn_ref[0:1, 0:256], at_end)
        dz = e * w_ref[1:2, :] + ep * w_ref[0:1, :] + em * w_ref[2:3, :]
        o_ref[:, 0:256] = d * (zm * w_ref[0:1, :] + z * w_ref[1:2, :] + zp * w_ref[2:3, :])
        o_ref[:, 256:512] = dz * cx
        o_ref[:, 512:768] = dz * cc
        dw_ref[...] = jnp.zeros_like(dw_ref)
        dw_ref[0:1, :] = _colsum(e * zm)
        dw_ref[1:2, :] = _colsum(e * z)
        dw_ref[2:3, :] = _colsum(e * zp)

    prev, nxt = _halo_specs(3 * CW, 1, S)
    dprev, dnxt = _halo_specs(CW, 0, S)
    return pl.pallas_call(
        body, name="conv_bwd", grid=(B, nt),
        in_specs=[_tok(3 * CW, 1), prev, nxt, _tok(CW), dprev, dnxt, _full((8, CW))],
        out_specs=[_tok(3 * CW), BS((None, None, 8, CW), lambda b, j: (b, j, 0, 0))],
        out_shape=[SDS((B, S, 3 * CW), F32), SDS((B, nt, 8, CW), F32)],
        compiler_params=_cp(("arbitrary", "arbitrary")),
    )(p, p, p, dcv, dcv, dcv, cw8)


HW = SGRP * SST


def _rev_chunk(k, nk, ncc):
    return jnp.where(k < ncc, ncc - 1 - k, nk - 1 - k + ncc)


def _scan_perm(B):
    n = B * TC
    pm = np.zeros((2 * n, 2 * n), np.float32)
    for t in range(TC):
        for e in range(B):
            pm[t * 2 * B + e, e * TC + t] = 1.0
            pm[t * 2 * B + B + e, n + e * TC + (TC - 1 - t)] = 1.0
    return pm


def _scan_drive(uf_ref, ub_ref, pm_ref, b4_ref, fwd, dbuf, n):
    u_tok = jnp.concatenate([uf_ref[...].reshape(n, SW), ub_ref[...].reshape(n, SW)], axis=0).astype(MXU)
    u_tm = _dot(pm_ref[...], u_tok).astype(MXU)
    d4 = _dot(u_tm, b4_ref[...])
    dbuf[...] = jnp.where(fwd, d4[:, :2 * HW], d4[:, 2 * HW:])
    return u_tm


def _scan_fwd(p, pm, pmt, b4, a, c2, Lc, ex=None):
    B, S, _ = p.shape
    nch, n = 2 * B, B * TC
    nk, ncc = S // TC, Lc // TC
    ucol = (INW - SW) // SW

    def body(uf_ref, ub_ref, pm_ref, pmt_ref, b4_ref, a_ref, c2_ref, yf_ref, yb_ref, hb_ref, hcar, dbuf, hbuf):
        @pl.when(pl.program_id(0) == 0)
        def _():
            hcar[...] = jnp.zeros_like(hcar)

        fwd = (lax.broadcasted_iota(jnp.int32, (2 * n, 1), 0) % nch) < B
        _scan_drive(uf_ref, ub_ref, pm_ref, b4_ref, fwd, dbuf, n)
        hb_ref[...] = hcar[...]
        ar, ai = a_ref[:, :HW], a_ref[:, HW:]

        def step(t, h):
            hr, hi = h
            r0 = pl.multiple_of(t * nch, nch)
            nr = ar * hr - ai * hi + dbuf[pl.ds(r0, nch), :HW]
            ni = ar * hi + ai * hr + dbuf[pl.ds(r0, nch), HW:]
            hbuf[pl.ds(r0, nch), :HW] = nr
            hbuf[pl.ds(r0, nch), HW:] = ni
            return nr, ni

        hr, hi = lax.fori_loop(0, TC, step, (hcar[:, :HW], hcar[:, HW:]))
        hcar[:, :HW] = hr
        hcar[:, HW:] = hi
        h_tok = _dot(pmt_ref[...], hbuf[...].astype(MXU)).astype(MXU)
        yf_ref[...] = _dot(h_tok[:n], c2_ref[:, :SW]).reshape(B, TC, SW)
        yb_ref[...] = _dot(h_tok[n:], c2_ref[:, SW:]).reshape(B, TC, SW)

    return _call_with(
        ex, body, "scan_fwd", (nk,),
        [BS((B, TC, SW), lambda k: (0, k, ucol)),
         BS((B, TC, SW), lambda k: (0, _rev_chunk(k, nk, ncc), ucol)),
         _full((2 * n, 2 * n)), _full((2 * n, 2 * n)),
         _full((SW, 4 * HW)), _full((nch, 2 * HW)), _full((2 * HW, 2 * SW))],
        [BS((B, TC, SW), lambda k: (0, k, 0)),
         BS((B, TC, SW), lambda k: (0, _rev_chunk(k, nk, ncc), 0)),
         BS((None, nch, 2 * HW), lambda k: (k, 0, 0))],
        [SDS((B, S, SW), F32), SDS((B, S, SW), F32), SDS((nk, nch, 2 * HW), F32)],
        [p, p, pm, pmt, b4, a, c2],
        [pltpu.VMEM((nch, 2 * HW), F32), pltpu.VMEM((2 * n, 2 * HW), F32), pltpu.VMEM((2 * n, 2 * HW), F32)],
        _cp(("arbitrary",), 48))


def _scan_bwd(p, dy, hb, pm, pmt, b4, a, c2, Lc, ex=None):
    B, S, _ = p.shape
    nch, n = 2 * B, B * TC
    nk, ncc = S // TC, Lc // TC
    ucol = (INW - SW) // SW
    rows = 2 * n

    def body(uf_ref, ub_ref, dyf_ref, dyb_ref, hb_ref, pm_ref, pmt_ref, b4_ref, a_ref, c2_ref,
             duf_ref, dub_ref, db4_ref, da_ref, dc2_ref, gcar, dbuf, hbuf, gbuf):
        @pl.when(pl.program_id(0) == 0)
        def _():
            gcar[...] = jnp.zeros_like(gcar)
            db4_ref[...] = jnp.zeros_like(db4_ref)
            da_ref[...] = jnp.zeros_like(da_ref)
            dc2_ref[...] = jnp.zeros_like(dc2_ref)

        fwd = (lax.broadcasted_iota(jnp.int32, (rows, 1), 0) % nch) < B
        u_tm = _scan_drive(uf_ref, ub_ref, pm_ref, b4_ref, fwd, dbuf, n)
        ar, ai = a_ref[:, :HW], a_ref[:, HW:]
        hbuf[0:nch, :] = hb_ref[...]

        def fstep(t, h):
            hr, hi = h
            r0 = pl.multiple_of(t * nch, nch)
            nr = ar * hr - ai * hi + dbuf[pl.ds(r0, nch), :HW]
            ni = ar * hi + ai * hr + dbuf[pl.ds(r0, nch), HW:]
            hbuf[pl.ds(r0 + nch, nch), :HW] = nr
            hbuf[pl.ds(r0 + nch, nch), HW:] = ni
            return nr, ni

        lax.fori_loop(0, TC, fstep, (hb_ref[:, :HW], hb_ref[:, HW:]))

        dy_tok = jnp.concatenate([dyf_ref[...].reshape(n, SW), dyb_ref[...].reshape(n, SW)], axis=0).astype(MXU)
        dy_tm = _dot(pm_ref[...], dy_tok)
        dy2 = jnp.concatenate([jnp.where(fwd, dy_tm, 0.0), jnp.where(fwd, 0.0, dy_tm)], axis=1).astype(MXU)
        gbuf[...] = _dot_nt(dy2, c2_ref[...])
        dc2_ref[...] += _dot_tn(hbuf[nch:, :].astype(MXU), dy2)

        def bstep(i, cary):
            cr, ci = cary
            r0 = pl.multiple_of((TC - 1 - i) * nch, nch)
            gr = gbuf[pl.ds(r0, nch), :HW] + cr
            gi = gbuf[pl.ds(r0, nch), HW:] + ci
            gbuf[pl.ds(r0, nch), :HW] = gr
            gbuf[pl.ds(r0, nch), HW:] = gi
            return ar * gr + ai * gi, ar * gi - ai * gr

        cr, ci = lax.fori_loop(0, TC, bstep, (gcar[:, :HW], gcar[:, HW:]))
        gcar[:, :HW] = cr
        gcar[:, HW:] = ci

        gr = gbuf[:, :HW].reshape(TC, nch, HW)
        gi = gbuf[:, HW:].reshape(TC, nch, HW)
        hpr = hbuf[0:rows, :HW].reshape(TC, nch, HW)
        hpi = hbuf[0:rows, HW:].reshape(TC, nch, HW)
        da_ref[:, :HW] += jnp.sum(gr * hpr + gi * hpi, axis=0)
        da_ref[:, HW:] += jnp.sum(gi * hpr - gr * hpi, axis=0)

        g = gbuf[...]
        gm = g.astype(MXU)
        dd4 = jnp.concatenate([jnp.where(fwd, g, 0.0), jnp.where(fwd, 0.0, g)], axis=1).astype(MXU)
        db4_ref[...] += _dot_tn(u_tm, dd4)
        g_tok = _dot(pmt_ref[...], gm).astype(MXU)
        duf_ref[...] = _dot_nt(g_tok[:n], b4_ref[:, :2 * HW]).reshape(B, TC, SW)
        dub_ref[...] = _dot_nt(g_tok[n:], b4_ref[:, 2 * HW:]).reshape(B, TC, SW)

    fwd_blk = lambda col: BS((B, TC, SW), lambda k: (0, nk - 1 - k, col))
    bwd_blk = lambda col: BS((B, TC, SW), lambda k: (0, _rev_chunk(nk - 1 - k, nk, ncc), col))
    return _call_with(
        ex, body, "scan_bwd", (nk,),
        [fwd_blk(ucol), bwd_blk(ucol), fwd_blk(0), bwd_blk(0),
         BS((None, nch, 2 * HW), lambda k: (nk - 1 - k, 0, 0)),
         _full((rows, rows)), _full((rows, rows)),
         _full((SW, 4 * HW)), _full((nch, 2 * HW)), _full((2 * HW, 2 * SW))],
        [fwd_blk(0), bwd_blk(0), _full((SW, 4 * HW)), _full((nch, 2 * HW)), _full((2 * HW, 2 * SW))],
        [SDS((B, S, SW), F32), SDS((B, S, SW), F32), SDS((SW, 4 * HW), F32), SDS((nch, 2 * HW), F32),
         SDS((2 * HW, 2 * SW), F32)],
        [p, p, dy, dy, hb, pm, pmt, b4, a, c2],
        [pltpu.VMEM((nch, 2 * HW), F32), pltpu.VMEM((rows, 2 * HW), F32),
         pltpu.VMEM((rows + nch, 2 * HW), F32), pltpu.VMEM((rows, 2 * HW), F32)],
        _cp(("arbitrary",), 56))


def _post_mix(o, cv, yf, yb, p, h, modtok, ng, dsk, wg, bg, wo, nct, ex=None):
    B, S, _ = h.shape

    def body(o_ref, cv_ref, yf_ref, yb_ref, u_ref, h_ref, mod_ref, g_ref, dsk_ref, wg_ref, bg_ref, wo_ref,
             h1_ref, mix_ref, m_ref, y_ref):
        y = yf_ref[...] + yb_ref[...] + dsk_ref[...] * u_ref[...]
        g = _gelu(y)
        s = g * _sigmoid(_dot(g.astype(MXU), wg_ref[...]) + bg_ref[...])
        mix_ref[:, 0:AW] = o_ref[...]
        mix_ref[:, AW:AW + CW] = cv_ref[...].astype(MXU)
        mix_ref[:, AW + CW:] = s.astype(MXU)
        m = _dot(mix_ref[...], wo_ref[...])
        n, _ = _rms(m)
        h1_ref[...] = h_ref[...] + mod_ref[2:3, :] * (n * g_ref[1:2, :])
        m_ref[...] = m
        y_ref[...] = y

    return _call_with(
        ex, body, "post_mix", (B, S // TB),
        [_tok(AW), _tok(CW), _tok(SW), _tok(SW), _tok(SW, (INW - SW) // SW), _tok(D), _mod_spec(nct),
         _full((8, D)), _full((1, SW)), _full((SW, SW)), _full((1, SW)), _full((D, D))],
        [_tok(D), _tok(D), _tok(D), _tok(SW)],
        [SDS((B, S, D), F32), SDS((B, S, D), MXU), SDS((B, S, D), F32), SDS((B, S, SW), F32)],
        [o, cv, yf, yb, p, h, modtok, ng, dsk, wg, bg, wo], [], _cp(("arbitrary", "arbitrary"), 40))


def _post_bwd(dh1, m, y, p, modtok, ng, dsk, wg, bg, wo, nct):
    B, S, _ = m.shape

    def body(dh_ref, m_ref, y_ref, u_ref, mod_ref, g_ref, dsk_ref, wg_ref, bg_ref, wo_ref,
             dm_ref, da_ref, dc_ref, dy_ref, du_ref, vec_ref, vec2_ref, dwg_ref):
        @pl.when(jnp.logical_and(pl.program_id(0) == 0, pl.program_id(1) == 0))
        def _():
            dwg_ref[...] = jnp.zeros_like(dwg_ref)

        n, r = _rms(m_ref[...])
        dh1 = dh_ref[...]
        gpm = g_ref[1:2, :]
        dr = dh1 * mod_ref[2:3, :]
        vec_ref[...] = jnp.zeros_like(vec_ref)
        vec_ref[0:1, :] = _colsum(dh1 * (n * gpm))
        vec_ref[1:2, :] = _colsum(dr * n)
        dm = _rms_bwd(dr * gpm, n, r).astype(MXU)
        dm_ref[...] = dm
        dmix = _dot_nt(dm, wo_ref[...])
        for g, t in enumerate(_to_groups([dmix[:, k * 128:(k + 1) * 128] for k in range(AW // 128)])):
            da_ref[g] = t
        dc_ref[...] = dmix[:, AW:AW + CW]
        ds = dmix[:, AW + CW:]
        yv = y_ref[...]
        g = _gelu(yv)
        gb = g.astype(MXU)
        sg = _sigmoid(_dot(gb, wg_ref[...]) + bg_ref[...])
        dv = ds * g * sg * (1.0 - sg)
        dvb = dv.astype(MXU)
        dg = ds * sg + _dot_nt(dvb, wg_ref[...])
        dwg_ref[...] += _dot_tn(gb, dvb)
        dy = dg * _gelu_grad(yv)
        dy_ref[...] = dy
        du_ref[...] = dy * dsk_ref[...]
        vec2_ref[...] = jnp.zeros_like(vec2_ref)
        vec2_ref[0:1, :] = _colsum(dy * u_ref[...])
        vec2_ref[1:2, :] = _colsum(dv)

    nt = S // TB
    return pl.pallas_call(
        body, name="post_bwd", grid=(B, nt),
        in_specs=[_tok(D), _tok(D), _tok(SW), _tok(SW, (INW - SW) // SW), _mod_spec(nct), _full((8, D)),
                  _full((1, SW)), _full((SW, SW)), _full((1, SW)), _full((D, D))],
        out_specs=[_tok(D), _grp(), _tok(CW), _tok(SW), _tok(SW),
                   BS((None, None, 8, D), lambda b, j: (b, j, 0, 0)),
                   BS((None, None, 8, SW), lambda b, j: (b, j, 0, 0)), _full((SW, SW))],
        out_shape=[SDS((B, S, D), MXU), SDS((B, QPK, S, 128), F32), SDS((B, S, CW), F32), SDS((B, S, SW), F32),
                   SDS((B, S, SW), F32), SDS((B, nt, 8, D), F32), SDS((B, nt, 8, SW), F32), SDS((SW, SW), F32)],
        compiler_params=_cp(("arbitrary", "arbitrary"), 40),
    )(dh1, m, y, p, modtok, ng, dsk, wg, bg, wo)


def _mlp_fwd(h1, modtok, ng, w1, w2, nct, ex=None):
    B, S, _ = h1.shape
    assert w1.shape == (DFF // TF, D, TF)

    def body(h_ref, mod_ref, g_ref, w1_ref, w2_ref, h2_ref, a2_ref, z_ref, f_ref):
        n, _ = _rms(h_ref[...])
        a2 = ((n * g_ref[2:3, :]) * (1.0 + mod_ref[4:5, :]) + mod_ref[3:4, :]).astype(MXU)
        a2_ref[...] = a2
        ff = jnp.zeros((TB, D), F32)
        for c in range(DFF // TF):
            cs = slice(c * TF, (c + 1) * TF)
            z = _dot(a2, w1_ref[c])
            z_ref[:, cs] = z.astype(MXU)
            r = jnp.maximum(z, 0.0)
            ff = ff + _dot((r * r).astype(MXU), w2_ref[cs, :])
        f_ref[...] = ff
        n, _ = _rms(ff)
        h2_ref[...] = h_ref[...] + mod_ref[5:6, :] * (n * g_ref[3:4, :])

    return _call_with(
        ex, body, "mlp_fwd", (B, S // TB),
        [_tok(D), _mod_spec(nct), _full((8, D)), _resident((DFF // TF, D, TF)), _resident((DFF, D))],
        [_tok(D), _tok(D), _tok(DFF), _tok(D)],
        [SDS((B, S, D), F32), SDS((B, S, D), MXU), SDS((B, S, DFF), MXU), SDS((B, S, D), F32)],
        [h1, modtok, ng, w1, w2], [], _cp(("arbitrary", "arbitrary"), 48))


def _mlp_bwd(dh2, h1, f, z, modtok, ng, w1, w2, nct, ex=None):
    B, S, _ = h1.shape

    def body(dh_ref, h_ref, f_ref, z_ref, mod_ref, g_ref, w1_ref, w2_ref, dh1_ref, df_ref, dz_ref, vec_ref):
        n, r = _rms(f_ref[...])
        dh2 = dh_ref[...]
        gp = g_ref[3:4, :]
        dr = dh2 * mod_ref[5:6, :]
        vec_ref[...] = jnp.zeros_like(vec_ref)
        vec_ref[3:4, :] = _colsum(dh2 * (n * gp))
        vec_ref[4:5, :] = _colsum(dr * n)
        df = _rms_bwd(dr * gp, n, r).astype(MXU)
        df_ref[...] = df
        da = jnp.zeros((TB, D), F32)
        for c in range(DFF // TF):
            cs = slice(c * TF, (c + 1) * TF)
            dr2 = _dot_nt(df, w2_ref[cs, :])
            dz = (dr2 * 2.0 * jnp.maximum(z_ref[:, cs].astype(F32), 0.0)).astype(MXU)
            dz_ref[:, cs] = dz
            da = da + _dot_nt(dz, w1_ref[c])
        n, r = _rms(h_ref[...])
        g = g_ref[2:3, :]
        sc1 = 1.0 + mod_ref[4:5, :]
        vec_ref[0:1, :] = _colsum(da)
        vec_ref[1:2, :] = _colsum(da * (n * g))
        vec_ref[2:3, :] = _colsum(da * sc1 * n)
        dh1_ref[...] = dh2 + _rms_bwd(da * sc1 * g, n, r)

    return _call_with(
        ex, body, "mlp_bwd", (B, S // TB),
        [_tok(D), _tok(D), _tok(D), _tok(DFF), _mod_spec(nct), _full((8, D)),
         _resident((DFF // TF, D, TF)), _resident((DFF, D))],
        [_tok(D), _tok(D), _tok(DFF), BS((None, None, 8, D), lambda b, j: (b, j, 0, 0))],
        [SDS((B, S, D), F32), SDS((B, S, D), MXU), SDS((B, S, DFF), MXU), SDS((B, S // TB, 8, D), F32)],
        [dh2, h1, f, z, modtok, ng, w1, w2], [], _cp(("arbitrary", "arbitrary"), 56))


def _loss(h, target, Lc):
    B, S, _ = h.shape
    nt, nct = S // TB, Lc // TB

    def body(h_ref, t_ref, dh_ref, l_ref):
        lat = pl.program_id(1) >= nct
        err = jnp.where(lat, h_ref[...] - t_ref[...], 0.0)
        dh_ref[...] = err * (1.0 / D)
        l_ref[...] = jnp.broadcast_to(jnp.sum(err * err, keepdims=True), (8, 128))

    return pl.pallas_call(
        body, name="loss", grid=(B, nt),
        in_specs=[_tok(D), BS((None, TB, D), lambda b, j: (b, jnp.maximum(j - nct, 0), 0))],
        out_specs=[_tok(D), BS((None, None, 8, 128), lambda b, j: (b, j, 0, 0))],
        out_shape=[SDS((B, S, D), F32), SDS((B, nt, 8, 128), F32)],
        compiler_params=_cp(("arbitrary", "arbitrary")),
    )(h, target)


def _mm_tn(a, b, name, relu2=False):
    T, M = a.shape
    N = b.shape[1]
    def blk(n):
        return max(b for b in range(128, 1025, 128) if n % b == 0)

    bm, bn = blk(M), blk(N)
    tk = next(t for t in (1536, 1024, 512, 256, T) if T % t == 0)
    nkk = T // tk

    def body(a_ref, b_ref, o_ref, om_ref):
        k = pl.program_id(2)

        @pl.when(k == 0)
        def _():
            o_ref[...] = jnp.zeros_like(o_ref)

        av = a_ref[...]
        if relu2:
            r = jnp.maximum(av.astype(F32), 0.0)
            av = (r * r).astype(MXU)
        o_ref[...] += _dot_tn(av, b_ref[...])

        @pl.when(k == nkk - 1)
        def _():
            om_ref[...] = o_ref[...].astype(MXU)

    out = BS((bm, bn), lambda i, j, k: (i, j))
    return pl.pallas_call(
        body, name=name, grid=(M // bm, N // bn, nkk),
        in_specs=[BS((tk, bm), lambda i, j, k: (k, i)), BS((tk, bn), lambda i, j, k: (k, j))],
        out_specs=[out, out], out_shape=[SDS((M, N), F32), SDS((M, N), MXU)],
        compiler_params=_cp(("arbitrary", "arbitrary", "arbitrary"), 56),
    )(a, b)


_PAD_ROWS = 64


def _block_rows(R):
    for br in (256, _PAD_ROWS):
        if R % br == 0:
            return br
    raise ValueError(f"row count {R} is not a multiple of {_PAD_ROWS}")


def _adamw(w, ga, gb, m, v, name):
    R, C = w.shape
    br = _block_rows(R) if C <= _LANES else _PAD_ROWS

    def body(w_ref, ga_ref, gb_ref, m_ref, v_ref, g_out, d_out, m_out, v_out):
        g = ga_ref[...] + gb_ref[...]
        m2 = B1 * m_ref[...] + (1.0 - B1) * g
        v2 = B2 * v_ref[...] + (1.0 - B2) * (g * g)
        m_hat = m2 / (1.0 - B1 ** STEP)
        v_hat = v2 / (1.0 - B2 ** STEP)
        g_out[...] = g
        d_out[...] = -LR * (m_hat / (jnp.sqrt(v_hat) + AEPS) + WD * w_ref[...])
        m_out[...] = m2
        v_out[...] = v2

    own = BS((br, C), lambda i: (i, 0))
    return pl.pallas_call(
        body, name=name, grid=(R // br,),
        in_specs=[own] * 5, out_specs=[own] * 4, out_shape=[SDS((R, C), F32)] * 4,
        compiler_params=_cp(("arbitrary",)),
    )(w, ga, gb, m, v)


def _my_xyc():
    return lax.axis_index("x"), lax.axis_index("y"), lax.axis_index("c")


def _chip_peers(x, y):
    return [(1 - x, y), (x, 1 - y), (1 - x, 1 - y)]


class _Gather:
    def __init__(self, shards):
        self.args = list(shards)
        self.n = len(self.args)
        self.specs = [ANY] * self.n
        self.out_shape = [SDS((NCHIP,) + a.shape, a.dtype) for a in self.args]
        self.sems = [pltpu.SemaphoreType.DMA((3 * self.n,)), pltpu.SemaphoreType.DMA((3 * self.n,)),
                     pltpu.SemaphoreType.DMA((self.n,))]

    def _copies(self, ins, outs, ssem, rsem, lsem):
        x, y, c = _my_xyc()
        mine, sends, recvs = [], [], []
        for k, (s_ref, o_ref) in enumerate(zip(ins, outs)):
            mine.append(pltpu.make_async_copy(s_ref, o_ref.at[2 * x + y], lsem.at[k]))
            for j, (px, py) in enumerate(_chip_peers(x, y)):
                sems = dict(send_sem=ssem.at[3 * k + j], recv_sem=rsem.at[3 * k + j], device_id=(px, py, c),
                            device_id_type=MESH)
                sends.append(pltpu.make_async_remote_copy(src_ref=s_ref, dst_ref=o_ref.at[2 * x + y], **sems))
                recvs.append(pltpu.make_async_remote_copy(src_ref=s_ref, dst_ref=o_ref.at[2 * px + py], **sems))
        return mine, sends, recvs

    def start(self, ins, outs, sems):
        mine, sends, _ = self._copies(ins, outs, *sems)
        for cp in mine + sends:
            cp.start()

    def wait(self, ins, outs, sems):
        mine, sends, recvs = self._copies(ins, outs, *sems)
        for cp in recvs:
            cp.wait_recv()
        for cp in sends:
            cp.wait_send()
        for cp in mine:
            cp.wait()


class _Gather2(_Gather):
    def __init__(self, shards):
        super().__init__(shards)
        self.sems = self.sems + [pltpu.SemaphoreType.DMA((3 * self.n,)), pltpu.SemaphoreType.DMA((3 * self.n,))]

    def _copies2(self, ins, outs, ssem, rsem, lsem, fsem, gsem):
        x, y, c = _my_xyc()
        mine, sends, recvs, passes, got = [], [], [], [], []
        for k, (s_ref, o_ref) in enumerate(zip(ins, outs)):
            half = s_ref.shape[0] // 2
            my_rows, sib_rows = pl.ds(c * half, half), pl.ds((1 - c) * half, half)
            mine.append(pltpu.make_async_copy(s_ref, o_ref.at[2 * x + y], lsem.at[k]))
            for j, (px, py) in enumerate(_chip_peers(x, y)):
                ici = dict(send_sem=ssem.at[3 * k + j], recv_sem=rsem.at[3 * k + j], device_id=(px, py, c),
                           device_id_type=MESH)
                d2d = dict(send_sem=fsem.at[3 * k + j], recv_sem=gsem.at[3 * k + j], device_id=(x, y, 1 - c),
                           device_id_type=MESH)
                landed = o_ref.at[2 * px + py, my_rows]
                sends.append(pltpu.make_async_remote_copy(src_ref=s_ref.at[my_rows],
                                                          dst_ref=o_ref.at[2 * x + y, my_rows], **ici))
                recvs.append(pltpu.make_async_remote_copy(src_ref=s_ref.at[my_rows], dst_ref=landed, **ici))
                passes.append(pltpu.make_async_remote_copy(src_ref=landed, dst_ref=landed, **d2d))
                got.append(pltpu.make_async_remote_copy(src_ref=landed, dst_ref=o_ref.at[2 * px + py, sib_rows],
                                                        **d2d))
        return mine, sends, recvs, passes, got

    def start(self, ins, outs, sems):
        mine, sends, _, _, _ = self._copies2(ins, outs, *sems)
        for cp in mine + sends:
            cp.start()

    def wait(self, ins, outs, sems):
        mine, sends, recvs, passes, got = self._copies2(ins, outs, *sems)
        for landed, onward in zip(recvs, passes):
            landed.wait_recv()
            onward.start()
        for cp in got:
            cp.wait_recv()
        for cp in sends + passes:
            cp.wait_send()
        for cp in mine:
            cp.wait()


class _Scatter:
    def __init__(self, sends, cols=None):
        self.args = list(sends)
        self.n = len(self.args)
        self.cols = list(cols) if cols is not None else [None] * self.n
        self.specs = [ANY] * self.n
        self.out_shape = [SDS((3,) + a.shape[1:] if cw is None else (3, a.shape[0], cw), a.dtype)
                          for a, cw in zip(self.args, self.cols)]
        self.sems = [pltpu.SemaphoreType.DMA((3 * self.n,)), pltpu.SemaphoreType.DMA((3 * self.n,))]

    def _copies(self, ins, outs, ssem, rsem):
        x, y, c = _my_xyc()

        def piece(k, q):
            cw = self.cols[k]
            return ins[k].at[q] if cw is None else ins[k].at[:, pl.ds(pl.multiple_of(q * cw, 128), cw)]

        return [pltpu.make_async_remote_copy(
            src_ref=piece(k, 2 * px + py), dst_ref=outs[k].at[j], send_sem=ssem.at[3 * k + j],
            recv_sem=rsem.at[3 * k + j], device_id=(px, py, c), device_id_type=MESH)
            for k in range(self.n) for j, (px, py) in enumerate(_chip_peers(x, y))]

    def start(self, ins, outs, sems):
        for cp in self._copies(ins, outs, *sems):
            cp.start()

    def wait(self, ins, outs, sems):
        for cp in self._copies(ins, outs, *sems):
            cp.wait()


class _Swap(_Scatter):
    def __init__(self, arrays):
        self.args = list(arrays)
        self.n = len(self.args)
        self.specs = [ANY] * self.n
        self.out_shape = [SDS(a.shape, a.dtype) for a in self.args]
        self.sems = [pltpu.SemaphoreType.DMA((self.n,)), pltpu.SemaphoreType.DMA((self.n,))]

    def _copies(self, ins, outs, ssem, rsem):
        x, y, c = _my_xyc()
        return [pltpu.make_async_remote_copy(src_ref=s_ref, dst_ref=o_ref, send_sem=ssem.at[k], recv_sem=rsem.at[k],
                                             device_id=(x, y, 1 - c), device_id_type=MESH)
                for k, (s_ref, o_ref) in enumerate(zip(ins, outs))]


def _exchange_call(ex, name):
    n = ex.n

    def body(*refs):
        ins, outs, sems = refs[:n], refs[n:2 * n], refs[2 * n:]
        ex.start(ins, outs, sems)
        ex.wait(ins, outs, sems)

    return pl.pallas_call(body, name=name, in_specs=ex.specs, out_specs=ex.specs, out_shape=ex.out_shape,
                          scratch_shapes=ex.sems)(*ex.args)


def _carried(ex, refs, n_in, n_out):
    n = ex.n
    ins, cin = refs[:n_in], refs[n_in:n_in + n]
    outs, cout = refs[n_in + n:n_in + n + n_out], refs[n_in + n + n_out:n_in + 2 * n + n_out]
    rest = refs[n_in + 2 * n + n_out:]
    nsem = len(ex.sems)
    return list(ins) + list(outs) + list(rest[:len(rest) - nsem]), (cin, cout, rest[len(rest) - nsem:])


def _gather_all(v, name):
    R, C = v.shape

    def body(s_ref, o_ref, ssem, rsem, lsem):
        x, y, c = _my_xyc()
        me = 4 * x + 2 * y + c
        mine = pltpu.make_async_copy(s_ref, o_ref.at[me], lsem)
        mine.start()

        def peer(j):
            fx, fy, fc = (j >> 2) & 1, (j >> 1) & 1, j & 1
            return (x ^ fx, y ^ fy, c ^ fc)

        cps = []
        for j in range(1, 8):
            cps.append(pltpu.make_async_remote_copy(
                src_ref=s_ref, dst_ref=o_ref.at[me], send_sem=ssem.at[j - 1], recv_sem=rsem.at[j - 1],
                device_id=peer(j), device_id_type=MESH))
            cps[-1].start()
        for j in range(1, 8):
            px, py, pc = peer(j)
            pltpu.make_async_remote_copy(
                src_ref=s_ref, dst_ref=o_ref.at[4 * px + 2 * py + pc], send_sem=ssem.at[j - 1],
                recv_sem=rsem.at[j - 1], device_id=peer(j), device_id_type=MESH).wait_recv()
        for cp in cps:
            cp.wait_send()
        mine.wait()

    return pl.pallas_call(
        body, name=name, in_specs=[ANY], out_specs=ANY, out_shape=SDS((8, R, C), v.dtype),
        scratch_shapes=[pltpu.SemaphoreType.DMA((7,)), pltpu.SemaphoreType.DMA((7,)), pltpu.SemaphoreType.DMA],
    )(v)


def _sum_slots(own, slots, name):
    n, R, C = slots.shape
    br = _block_rows(R)

    def body(*refs):
        o_ref = refs[-1]
        if own is None:
            acc = refs[0][0].astype(F32)
            first = 1
            s_ref = refs[0]
        else:
            acc = refs[0][...]
            first = 0
            s_ref = refs[1]
        for k in range(first, n):
            acc = acc + s_ref[k].astype(F32)
        o_ref[...] = acc

    row = BS((br, C), lambda i: (i, 0))
    slab = BS((n, br, C), lambda i: (0, i, 0))
    ins, args = ([slab], [slots]) if own is None else ([row, slab], [own, slots])
    return pl.pallas_call(
        body, name=name, grid=(R // br,), in_specs=ins, out_specs=row, out_shape=SDS((R, C), F32),
        compiler_params=_cp(("arbitrary",)),
    )(*args)


def _rope_tables(L, Lc):
    n = jnp.arange(L)
    row = (n // GRID_W).astype(F32)
    col = (n % GRID_W).astype(F32)
    freqs = ROPE_BASE ** (-jnp.arange(16, dtype=F32) / 16)
    lane = jnp.arange(128)
    dd = lane % HD
    fr = freqs[dd % 16]
    ang = jnp.where(dd < 32, row[:, None], col[:, None]) * fr[None, :]
    sign = jnp.where((dd % 32) < 16, -1.0, 1.0)
    cos = jnp.concatenate([jnp.ones((Lc, 128), F32), jnp.cos(ang)], axis=0)
    sin = jnp.concatenate([jnp.zeros((Lc, 128), F32), jnp.sin(ang) * sign[None, :]], axis=0)
    return cos, sin


def _ssm_prep(lam_re, lam_im, log_dt, b_re, b_im, c_re, c_im, B):
    dt = jnp.exp(log_dt)[..., None]
    mag = jnp.exp(lam_re * dt)
    ar = mag * jnp.cos(lam_im * dt)
    ai = mag * jnp.sin(lam_im * dt)
    den = lam_re * lam_re + lam_im * lam_im
    kr = ((ar - 1.0) * lam_re + ai * lam_im) / den
    ki = (ai * lam_re - (ar - 1.0) * lam_im) / den
    bbr = kr[..., None] * b_re - ki[..., None] * b_im
    bbi = kr[..., None] * b_im + ki[..., None] * b_re
    eye = jnp.eye(SGRP, dtype=F32)

    def bblk(t):
        return jnp.einsum("gpi,gh->gihp", t, eye).reshape(SGRP * SCH, SGRP * SST)

    def cblk(t):
        return jnp.einsum("gip,gh->gphi", t, eye).reshape(SGRP * SST, SGRP * SCH)

    b4 = jnp.concatenate([bblk(bbr[0]), bblk(bbi[0]), bblk(bbr[1]), bblk(bbi[1])], axis=1)
    c2 = jnp.concatenate([jnp.concatenate([cblk(c_re[0]), -cblk(c_im[0])], axis=0),
                          jnp.concatenate([cblk(c_re[1]), -cblk(c_im[1])], axis=0)], axis=1)
    a2 = jnp.concatenate([ar.reshape(2, -1), ai.reshape(2, -1)], axis=1)
    a = jnp.repeat(a2, B, axis=0)
    return a, b4, c2


def _pad_rows(v, rows=8):
    return jnp.concatenate([v, jnp.zeros((rows - v.shape[0],) + v.shape[1:], v.dtype)], axis=0)


_BIG = ("w_ada", "w_in", "w_out", "w_mlp_in", "w_mlp_out")
_BIG_AXIS = {"w_ada": 2, "w_in": 2, "w_out": 1, "w_mlp_in": 2, "w_mlp_out": 1}


def _join_shards(s, axis):
    _, L, r, c = s.shape
    if axis == 2:
        return jnp.transpose(s, (1, 2, 0, 3)).reshape(L, r, NCHIP * c)
    return jnp.transpose(s, (1, 0, 2, 3)).reshape(L, NCHIP * r, c)


def _local_step(x, c, ctx, loss_target, P, ex):
    B, L, _ = x.shape
    Lc = ctx.shape[1]
    S = Lc + L
    nct = Lc // TB
    nt = S // TB
    cos, sin = _rope_tables(L, Lc)
    bias = _attn_bias(S, Lc)
    perm = _scan_perm(B)
    pm, pmt = jnp.asarray(perm, MXU), jnp.asarray(perm.T, MXU)

    c_act = jax.nn.silu(c)
    cc_act, silu_vjp = jax.vjp(jax.nn.silu, P["c_ctx"])
    cact8 = _pad_rows(jnp.concatenate([c_act, cc_act[None, :]], axis=0))
    h = jnp.concatenate([ctx, x], axis=1)

    ssm_names = ("ssm_lam_re", "ssm_lam_im", "ssm_log_dt", "ssm_b_re", "ssm_b_im", "ssm_c_re", "ssm_c_im")
    (a_all, b4_all, c2_all), prep_vjp = jax.vjp(jax.vmap(lambda *t: _ssm_prep(*t, B)), *[P[n] for n in ssm_names])
    b4m_all, c2m_all = b4_all.astype(MXU), c2_all.astype(MXU)
    ng_all = jnp.pad(P["norm_g"], ((0, 0), (0, 4), (0, 0)))
    cw8_all = jnp.pad(P["conv_w"], ((0, 0), (0, 8 - P["conv_w"].shape[1]), (0, 0)))
    wg_all = P["w_glu"].astype(MXU)

    saved = []
    for l in range(NLAYER):
        mod8 = _ada_fwd(cact8, ex.w(l, "w_ada"), P["b_ada"][l][None, :])
        mod6 = mod8.reshape(8, NMOD, D)
        modtok = jnp.stack([jnp.broadcast_to(mod6[B], (B, NMOD, D)), mod6[:B]], axis=1)
        modtok = jnp.pad(modtok, ((0, 0), (0, 0), (0, 2), (0, 0)))
        ng, cw8, a_s, b4m, c2m, wg = ng_all[l], cw8_all[l], a_all[l], b4m_all[l], c2m_all[l], wg_all[l]
        dsk = P["ssm_d"][l][None, :]
        bg = P["b_glu"][l][None, :]

        p, a = ex.pre_mix(l, h, modtok, ng, ex.w(l, "w_in"), nct)
        q4, kr = _rope_fwd(p, cos, sin)
        (o,) = ex.attn_fwd(l, P["attn_sink"][l], q4, kr, p, bias, Lc)
        cv = _conv_fwd(p, cw8, nct)
        yf, yb, hb = ex.scan_fwd(l, p, pm, pmt, b4m, a_s, c2m, Lc)
        h1, mix, m, y = ex.post_mix(l, o, cv, yf, yb, p, h, modtok, ng, dsk, wg, bg, ex.w(l, "w_out"), nct)
        h2, a2, z, f = ex.mlp_fwd(l, h1, modtok, ng, ex.w(l, "w_mlp_in"), ex.w(l, "w_mlp_out"), nct)
        saved.append(dict(h=h, modtok=modtok, ng=ng, cw8=cw8, a_s=a_s, b4m=b4m, c2m=c2m, wg=wg, dsk=dsk, bg=bg,
                          p=p, a=a, q4=q4, kr=kr, hb=hb, mix=mix, m=m, y=y, h1=h1, a2=a2, z=z, f=f))
        h = h2

    dh, lpart = _loss(h, loss_target, Lc)
    loss_local = 0.5 / D * jnp.sum(lpart[:, :, 0, 0])

    R = {k: [None] * NLAYER for k in ("w_glu", "vec_p2", "scan", "dcw", "dsink", "vec", "b_ada")}
    dcact = jnp.zeros((8, D), F32)
    T = B * S
    for l in reversed(range(NLAYER)):
        sv = saved[l]
        modtok, ng = sv["modtok"], sv["ng"]
        dh1, df, dz, vec_m = ex.mlp_bwd(l, dh, sv["h1"], sv["f"], sv["z"], modtok, ng, ex.w(l, "w_mlp_in"),
                                        ex.w(l, "w_mlp_out"), nct)
        ex.submit(l, "w_mlp_out", *_mm_tn(sv["z"].reshape(T, DFF), df.reshape(T, D), "dw_mlp_out", relu2=True))
        ex.submit(l, "w_mlp_in", *_mm_tn(sv["a2"].reshape(T, D), dz.reshape(T, DFF), "dw_mlp_in"))
        dm, dattn, dcv, dy, du_skip, vec_p, vec_p2, dwg = _post_bwd(
            dh1, sv["m"], sv["y"], sv["p"], modtok, ng, sv["dsk"], sv["wg"], sv["bg"], ex.w(l, "w_out"), nct)
        ex.submit(l, "w_out", *_mm_tn(sv["mix"].reshape(T, D), dm.reshape(T, D), "dw_out"))
        R["w_glu"][l], R["vec_p2"][l] = dwg, vec_p2
        duf, dub, db4, da_s, dc2 = ex.scan_bwd(l, sv["p"], dy, sv["hb"], pm, pmt, sv["b4m"], sv["a_s"], sv["c2m"],
                                               Lc)
        R["scan"][l] = (da_s, db4, dc2)
        dconv, R["dcw"][l] = _conv_bwd(sv["p"], dcv, sv["cw8"], nct)
        dq_r, dk_r, dv, R["dsink"][l] = ex.attn_bwd(l, P["attn_sink"][l], sv["q4"], sv["kr"], sv["p"], dattn, bias,
                                                      Lc)
        dqk = _rope_bwd(dq_r, dk_r, cos, -sin)
        dh, vec_i, dp = _pre_bwd(dqk, dv, dconv, (du_skip, duf, dub), dh1, sv["h"], modtok, ng, ex.w(l, "w_in"),
                                 nct)
        ex.submit(l, "w_in", *_mm_tn(sv["a"].reshape(T, D), dp.reshape(T, INW), "dw_in"))
        vec = jnp.concatenate([vec_i, vec_p, vec_m], axis=2)
        R["vec"][l] = vec
        mod_rows = jnp.concatenate([vec[:, :, 0:2], vec[:, :, 8:9], vec[:, :, 16:18], vec[:, :, 19:20]], axis=2)
        dmod8 = jnp.concatenate([jnp.sum(mod_rows[:, nct:], axis=1).reshape(B, NMOD * D),
                                 jnp.sum(mod_rows[:, :nct], axis=(0, 1)).reshape(1, NMOD * D),
                                 jnp.zeros((7 - B, NMOD * D), F32)], axis=0)
        R["b_ada"][l] = dmod8
        dmod8m = dmod8.astype(MXU)
        ex.submit(l, "w_ada", *_mm_tn(cact8.astype(MXU), dmod8m, "dw_ada"))
        dcact = dcact + _ada_bwd_c(dmod8m, ex.w(l, "w_ada"))

    stk = lambda k: jnp.stack(R[k])
    vec, vec_p2 = jnp.sum(stk("vec"), axis=(1, 2)), jnp.sum(stk("vec_p2"), axis=(1, 2))
    grads = dict(zip(ssm_names, prep_vjp(tuple(jnp.stack([R["scan"][l][k] for l in range(NLAYER)])
                                               for k in range(3)))))
    norm_rows = jnp.concatenate([vec[:, 2:3], vec[:, 9:10], vec[:, 18:19], vec[:, 20:21]], axis=1)
    grads.update(w_glu=stk("w_glu"), ssm_d=vec_p2[:, 0], b_glu=vec_p2[:, 1], norm_g=norm_rows,
                 conv_w=jnp.sum(stk("dcw"), axis=(1, 2))[:, 0:P["conv_w"].shape[1]],
                 attn_sink=jnp.sum(stk("dsink")[..., 0], axis=1), b_ada=jnp.sum(stk("b_ada"), axis=1),
                 c_ctx=silu_vjp(dcact[B])[0])
    return loss_local, dh[:, Lc:, :], grads


_WEIGHTS = ["c_ctx", "w_ada", "b_ada", "norm_g", "w_in", "conv_w", "attn_sink", "ssm_lam_re", "ssm_lam_im",
            "ssm_log_dt", "ssm_b_re", "ssm_b_im", "ssm_c_re", "ssm_c_im", "ssm_d", "w_glu", "b_glu", "w_out",
            "w_mlp_in", "w_mlp_out"]
_SMALL = [n for n in _WEIGHTS if n not in _BIG]
_SMALL_SHARDED = {"norm_g": 2, "conv_w": 2, "w_glu": 1}
_LANES = 1024


_GROUPS = (("w_ada", "w_in", "w_out"), ("w_mlp_in", "w_mlp_out"))
_QUEUE_OF = {"w_ada": "narrow", "w_in": "narrow", "w_out": "out", "w_mlp_in": "mlp", "w_mlp_out": "mlp"}


class _Exchange:
    def __init__(self, W, q_me):
        self.W, self.q_me = W, q_me
        self.wall, self.own, self.recv, self.pending = {}, {}, {}, {}
        first = ("w_ada", "w_in")
        self._keep(0, first, _exchange_call(_Gather2([self._shard(0, n) for n in first]), "gather_big"))

    def _shard(self, l, n):
        return self.W[n][l].astype(MXU)

    def _keep(self, l, names, walls):
        for n, w in zip(names, walls):
            self.wall[l, n] = w

    def w(self, l, n):
        g = self.wall[l, n]
        if n in ("w_ada", "w_mlp_in"):
            return g
        return _join_shards(g[:, None], 2)[0] if n == "w_in" else g.reshape(-1, g.shape[-1])

    def _gathering(self, l, names, fn, *args):
        if l == NLAYER or not names:
            return fn(*args)[0]
        res, walls = fn(*args, ex=_Gather([self._shard(l, n) for n in names]))
        self._keep(l, names, walls)
        return res

    def pre_mix(self, l, *args):
        return self._gathering(l, ("w_out",) if l == 0 else (), _pre_mix, *args)

    def scan_fwd(self, l, *args):
        return self._gathering(l, ("w_mlp_in",) if l == 0 else (), _scan_fwd, *args)

    def post_mix(self, l, *args):
        return self._gathering(l, ("w_mlp_out",) if l == 0 else (), _post_mix, *args)

    def attn_fwd(self, l, *args):
        return self._gathering(l + 1, _GROUPS[0], _attn_fwd, *args)

    def mlp_fwd(self, l, *args):
        return self._gathering(l + 1, _GROUPS[1], _mlp_fwd, *args)

    def submit(self, l, n, g, gm):
        r, c = g.shape
        if _BIG_AXIS[n] == 2:
            cw = c // NCHIP
            self.own[l, n] = lax.dynamic_slice_in_dim(g, self.q_me * cw, cw, axis=1)
            if cw % 128 == 0:
                item = (gm, cw)
            else:
                item = (jnp.transpose(gm.reshape(r, NCHIP, cw), (1, 0, 2)), None)
        else:
            self.own[l, n] = lax.dynamic_slice_in_dim(g, self.q_me * (r // NCHIP), r // NCHIP, axis=0)
            item = (gm.reshape(NCHIP, r // NCHIP, c), None)
        self.pending.setdefault(_QUEUE_OF[n], []).append((l, n) + item)

    def _scatter_of(self, items):
        return _Scatter([it[2] for it in items], [it[3] for it in items])

    def _scattering(self, queue, fn, *args):
        items = self.pending.pop(queue, [])
        if not items:
            return fn(*args)[0]
        res, recvs = fn(*args, ex=self._scatter_of(items))
        for it, r in zip(items, recvs):
            self.recv[it[0], it[1]] = r
        return res

    def mlp_bwd(self, l, *args):
        return self._scattering("narrow", _mlp_bwd, *args)

    def scan_bwd(self, l, *args):
        return self._scattering("mlp", _scan_bwd, *args)

    def attn_bwd(self, l, *args):
        return self._scattering("out", _attn_bwd, *args)

    def finish(self):
        items = [it for queue in sorted(self.pending) for it in self.pending[queue]]
        self.pending = {}
        for it, r in zip(items, _exchange_call(self._scatter_of(items), "scatter_big")):
            self.recv[it[0], it[1]] = r
        part = [jnp.concatenate([_sum_slots(self.own[l, n], self.recv[l, n], "sum_chips") for l in range(NLAYER)],
                                axis=0) for n in _BIG]
        sib = _exchange_call(_Swap(part), "swap_big")
        return {n: (p, s) for n, p, s in zip(_BIG, part, sib)}


def _part_rows(shape):
    return -(-math.prod(shape) // (16 * _LANES)) * 16


def _flat_pad(parts):
    rows = []
    for p in parts:
        v = p.reshape(-1)
        pad = _part_rows(p.shape) * _LANES - v.shape[0]
        rows.append(jnp.concatenate([v, jnp.zeros((pad,), v.dtype)]).reshape(-1, _LANES))
    used = sum(r.shape[0] for r in rows)
    rows.append(jnp.zeros((-used % _PAD_ROWS, _LANES), rows[0].dtype))
    return jnp.concatenate(rows, axis=0)


def _unflat(flat, shapes):
    out, r0 = [], 0
    for s in shapes:
        nr = _part_rows(s)
        out.append(flat[r0:r0 + nr].reshape(-1)[:math.prod(s)].reshape(s))
        r0 += nr
    return out


def kernel(x, c, ctx, c_ctx, w_ada, b_ada, norm_g, w_in, conv_w, attn_sink, ssm_lam_re, ssm_lam_im, ssm_log_dt, ssm_b_re, ssm_b_im, ssm_c_re, ssm_c_im, ssm_d, w_glu, b_glu, w_out, w_mlp_in, w_mlp_out, loss_target, m_c_ctx, m_w_ada, m_b_ada, m_norm_g, m_w_in, m_conv_w, m_attn_sink, m_ssm_lam_re, m_ssm_lam_im, m_ssm_log_dt, m_ssm_b_re, m_ssm_b_im, m_ssm_c_re, m_ssm_c_im, m_ssm_d, m_w_glu, m_b_glu, m_w_out, m_w_mlp_in, m_w_mlp_out, v_c_ctx, v_w_ada, v_b_ada, v_norm_g, v_w_in, v_conv_w, v_attn_sink, v_ssm_lam_re, v_ssm_lam_im, v_ssm_log_dt, v_ssm_b_re, v_ssm_b_im, v_ssm_c_re, v_ssm_c_im, v_ssm_d, v_w_glu, v_b_glu, v_w_out, v_w_mlp_in, v_w_mlp_out):
    W = dict(c_ctx=c_ctx, w_ada=w_ada, b_ada=b_ada, norm_g=norm_g, w_in=w_in, conv_w=conv_w, attn_sink=attn_sink,
             ssm_lam_re=ssm_lam_re, ssm_lam_im=ssm_lam_im, ssm_log_dt=ssm_log_dt, ssm_b_re=ssm_b_re,
             ssm_b_im=ssm_b_im, ssm_c_re=ssm_c_re, ssm_c_im=ssm_c_im, ssm_d=ssm_d, w_glu=w_glu, b_glu=b_glu,
             w_out=w_out, w_mlp_in=w_mlp_in, w_mlp_out=w_mlp_out)
    M = dict(c_ctx=m_c_ctx, w_ada=m_w_ada, b_ada=m_b_ada, norm_g=m_norm_g, w_in=m_w_in, conv_w=m_conv_w,
             attn_sink=m_attn_sink, ssm_lam_re=m_ssm_lam_re, ssm_lam_im=m_ssm_lam_im, ssm_log_dt=m_ssm_log_dt,
             ssm_b_re=m_ssm_b_re, ssm_b_im=m_ssm_b_im, ssm_c_re=m_ssm_c_re, ssm_c_im=m_ssm_c_im, ssm_d=m_ssm_d,
             w_glu=m_w_glu, b_glu=m_b_glu, w_out=m_w_out, w_mlp_in=m_w_mlp_in, w_mlp_out=m_w_mlp_out)
    V = dict(c_ctx=v_c_ctx, w_ada=v_w_ada, b_ada=v_b_ada, norm_g=v_norm_g, w_in=v_w_in, conv_w=v_conv_w,
             attn_sink=v_attn_sink, ssm_lam_re=v_ssm_lam_re, ssm_lam_im=v_ssm_lam_im, ssm_log_dt=v_ssm_log_dt,
             ssm_b_re=v_ssm_b_re, ssm_b_im=v_ssm_b_im, ssm_c_re=v_ssm_c_re, ssm_c_im=v_ssm_c_im, ssm_d=v_ssm_d,
             w_glu=v_w_glu, b_glu=v_b_glu, w_out=v_w_out, w_mlp_in=v_w_mlp_in, w_mlp_out=v_w_mlp_out)
    q_me = 2 * lax.axis_index("x") + lax.axis_index("y")

    ex = _Exchange(W, q_me)
    P = {n: W[n] for n in _SMALL}
    ssh_names = list(_SMALL_SHARDED)
    ssh = _flat_pad([W[n] for n in ssh_names])
    (sall,) = _exchange_call(_Gather([ssh]), "gather_small")
    parts = [_unflat(sall[q], [W[n].shape for n in ssh_names]) for q in range(NCHIP)]
    for k, n in enumerate(ssh_names):
        P[n] = _join_shards(jnp.stack([parts[q][k] for q in range(NCHIP)]), _SMALL_SHARDED[n])

    loss_local, grad_x, G = _local_step(x, c, ctx, loss_target, P, ex)
    loss = lax.psum(loss_local, ("x", "y", "c"))

    sums = ex.finish()
    out = {}
    for n in _BIG:
        flat = lambda t: t.reshape(-1, t.shape[-1])
        res = _adamw(flat(W[n]), *sums[n], flat(M[n]), flat(V[n]), "adamw_" + n)
        out[n] = [t.reshape(W[n].shape) for t in res]

    gsmall = _flat_pad([G[n] for n in _SMALL]).astype(MXU)
    gsum = _sum_slots(None, _gather_all(gsmall, "gather_grads"), "sum_devices")
    gfull = dict(zip(_SMALL, _unflat(gsum, [G[n].shape for n in _SMALL])))
    for n, ax in _SMALL_SHARDED.items():
        width = W[n].shape[ax]
        gfull[n] = lax.dynamic_slice_in_dim(gfull[n], q_me * width, width, axis=ax)
    shapes = [W[n].shape for n in _SMALL]
    gflat = _flat_pad([gfull[n] for n in _SMALL])
    res = _adamw(_flat_pad([W[n] for n in _SMALL]), gflat, jnp.zeros_like(gflat),
                 _flat_pad([M[n] for n in _SMALL]), _flat_pad([V[n] for n in _SMALL]), "adamw_small")
    for k, t in enumerate(res):
        for n, piece in zip(_SMALL, _unflat(t, shapes)):
            out.setdefault(n, [None] * 4)[k] = piece

    return (loss, grad_x, *[out[n][0] for n in _WEIGHTS], *[out[n][1] for n in _WEIGHTS],
            *[out[n][2] for n in _WEIGHTS], *[out[n][3] for n in _WEIGHTS])
```

```python
import math

import jax
import jax.numpy as jnp
import numpy as np
from jax import lax
from jax.experimental import pallas as pl
from jax.experimental.pallas import tpu as pltpu

F32 = jnp.float32
MXU = jnp.bfloat16

D = 1024
DFF = 4096
NMOD = 6
EPS = 1e-6
HD = 64
NQ = 8
NKV = 2
QPK = 4
AW = 512
KVW = 128
WIN = 128
QB = 128
CW = 256
SW = 256
SGRP = 16
SCH = 16
SST = 64
INW = 1792
GRID_W = 64
ROPE_BASE = 10000.0
NEG = -1e30
SCALE = HD ** -0.5
NLAYER = 4

TB = 256
TF = 1024
TC = 32
NCHIP = 4

LR, B1, B2, AEPS, WD, STEP = 0.001, 0.9, 0.999, 1e-08, 0.01, 10

MESH = pl.DeviceIdType.MESH
SDS = jax.ShapeDtypeStruct
BS = pl.BlockSpec
ANY = pl.BlockSpec(memory_space=pl.ANY)


def _cp(sem, vmem_mb=None):
    kw = dict(dimension_semantics=sem)
    if vmem_mb is not None:
        kw["vmem_limit_bytes"] = vmem_mb * 1024 * 1024
    return pltpu.CompilerParams(**kw)


def _dot(a, b):
    return jnp.dot(a, b, preferred_element_type=F32)


def _dot_nt(a, b):
    return lax.dot_general(a, b, (((1,), (1,)), ((), ())), preferred_element_type=F32)


def _dot_tn(a, b):
    return lax.dot_general(a, b, (((0,), (0,)), ((), ())), preferred_element_type=F32)


def _rms(x):
    r = lax.rsqrt(jnp.mean(x * x, axis=-1, keepdims=True) + EPS)
    return x * r, r


def _rms_bwd(dn, n, r):
    return r * (dn - n * jnp.mean(dn * n, axis=-1, keepdims=True))


_GC = math.sqrt(2.0 / math.pi)


def _gelu(y):
    return 0.5 * y * (1.0 + jnp.tanh(_GC * (y + 0.044715 * y * y * y)))


def _gelu_grad(y):
    th = jnp.tanh(_GC * (y + 0.044715 * y * y * y))
    return 0.5 * (1.0 + th) + 0.5 * y * (1.0 - th * th) * _GC * (1.0 + 3 * 0.044715 * y * y)


def _sigmoid(v):
    return 1.0 / (1.0 + jnp.exp(-v))


def _colsum(x):
    return jnp.sum(x, axis=0, keepdims=True)


def _tok(width, col=0):
    return BS((None, TB, width), lambda b, j: (b, j, col))


def _mod_spec(nct):
    return BS((None, None, 8, D), lambda b, j: (b, jnp.where(j >= nct, 1, 0), 0, 0))


def _full(shape):
    nd = len(shape)
    return BS(shape, lambda *a: (0,) * nd)


def _resident(shape):
    nd = len(shape)
    return BS(shape, lambda *a: (0,) * nd, pipeline_mode=pl.Buffered(1))


def _ada_fwd(cact8, w4, b):
    wq = w4.shape[2]

    def body(c_ref, w_ref, b_ref, o_ref):
        o_ref[...] = _dot(c_ref[...].astype(MXU), w_ref[...]) + b_ref[...]

    return pl.pallas_call(
        body, name="ada_fwd", grid=(NCHIP,),
        in_specs=[BS((8, D), lambda j: (0, 0)), BS((None, D, wq), lambda j: (j, 0, 0)), BS((1, wq), lambda j: (0, j))],
        out_specs=BS((8, wq), lambda j: (0, j)),
        out_shape=SDS((8, NMOD * D), F32), compiler_params=_cp(("arbitrary",)),
    )(cact8, w4, b)


def _ada_bwd_c(dmod8, w4):
    wq = w4.shape[2]

    def body(d_ref, w_ref, o_ref):
        @pl.when(pl.program_id(0) == 0)
        def _():
            o_ref[...] = jnp.zeros_like(o_ref)
        o_ref[...] += _dot_nt(d_ref[...], w_ref[...])

    return pl.pallas_call(
        body, name="ada_bwd_c", grid=(NCHIP,),
        in_specs=[BS((8, wq), lambda j: (0, j)), BS((None, D, wq), lambda j: (j, 0, 0))],
        out_specs=BS((8, D), lambda j: (0, 0)),
        out_shape=SDS((8, D), F32), compiler_params=_cp(("arbitrary",)),
    )(dmod8, w4)


def _pre_mix(h, modtok, ng, w_in, cos, sin, nct, ex=None):
    B, S, _ = h.shape

    def body(h_ref, mod_ref, g_ref, w_ref, c_ref, s_ref, p_ref, a_ref, q_ref, k_ref):
        n, _ = _rms(h_ref[...])
        a = ((n * g_ref[0:1, :]) * (1.0 + mod_ref[1:2, :]) + mod_ref[0:1, :]).astype(MXU)
        a_ref[...] = a
        p = _dot(a, w_ref[...])
        p_ref[...] = p
        c, s = c_ref[...], s_ref[...]
        rot = [_rotate(p[:, k * 128:(k + 1) * 128], c, s) for k in range((AW + KVW) // 128)]
        k_ref[...] = rot[AW // 128].astype(MXU)
        for g, t in enumerate(_to_groups(rot[:AW // 128])):
            q_ref[g] = (t * SCALE).astype(MXU)

    return _call_with(
        ex, body, "pre_mix", (B, S // TB),
        [_tok(D), _mod_spec(nct), _full((8, D)), _full((D, INW)), _TAB, _TAB],
        [_tok(INW), _tok(D), _grp(), _tok(KVW)],
        [SDS((B, S, INW), F32), SDS((B, S, D), MXU), SDS((B, QPK, S, 128), MXU), SDS((B, S, KVW), MXU)],
        [h, modtok, ng, w_in, cos, sin], [], _cp(("arbitrary", "arbitrary"), 40))


def _pre_bwd(dq4, dk, cos, nsin, dv, dcv, du_parts, dh_part, h, modtok, ng, w_in, nct):
    B, S, _ = h.shape

    def body(dq_ref, dk_ref, c_ref, s_ref, dv_ref, dcv_ref, du0_ref, du1_ref, du2_ref, dhp_ref, h_ref, mod_ref,
             g_ref, w_ref, dh_ref, vec_ref, dp_ref):
        c, s = c_ref[...], s_ref[...]
        cols = _from_groups([dq_ref[g] * SCALE for g in range(QPK)]) + [dk_ref[...]]
        dqk = [_rotate(t, c, s).astype(MXU) for t in cols]
        du = du0_ref[...] + du1_ref[...] + du2_ref[...]
        dp = jnp.concatenate(dqk + [dv_ref[...].astype(MXU), dcv_ref[...].astype(MXU), du.astype(MXU)], axis=1)
        dp_ref[...] = dp
        da = _dot_nt(dp, w_ref[...])
        n, r = _rms(h_ref[...])
        g = g_ref[0:1, :]
        sc1 = 1.0 + mod_ref[1:2, :]
        vec_ref[...] = jnp.zeros_like(vec_ref)
        vec_ref[0:1, :] = _colsum(da)
        vec_ref[1:2, :] = _colsum(da * (n * g))
        vec_ref[2:3, :] = _colsum(da * sc1 * n)
        dh_ref[...] = dhp_ref[...] + _rms_bwd(da * sc1 * g, n, r)

    return pl.pallas_call(
        body, name="pre_bwd", grid=(B, S // TB),
        in_specs=[_grp(), _tok(KVW), _TAB, _TAB, _tok(KVW), _tok(3 * CW), _tok(SW), _tok(SW), _tok(SW), _tok(D),
                  _tok(D), _mod_spec(nct), _full((8, D)), _full((D, INW))],
        out_specs=[_tok(D), BS((None, None, 8, D), lambda b, j: (b, j, 0, 0)), _tok(INW)],
        out_shape=[SDS((B, S, D), F32), SDS((B, S // TB, 8, D), F32), SDS((B, S, INW), MXU)],
        compiler_params=_cp(("arbitrary", "arbitrary"), 40),
    )(dq4, dk, cos, nsin, dv, dcv, *du_parts, dh_part, h, modtok, ng, w_in)


def _rotate(t, c, s):
    lane = lax.broadcasted_iota(jnp.int32, t.shape, 1)
    partner = jnp.where((lane % 32) < 16, pltpu.roll(t, 112, 1), pltpu.roll(t, 16, 1))
    return t * c + partner * s


def _to_groups(cols):
    lo = lax.broadcasted_iota(jnp.int32, cols[0].shape, 1) < HD
    out = []
    for g in range(QPK):
        a, b = cols[g // 2], cols[2 + g // 2]
        out.append(jnp.where(lo, a, pltpu.roll(b, HD, 1)) if g % 2 == 0 else jnp.where(lo, pltpu.roll(a, HD, 1), b))
    return out


def _from_groups(tiles):
    lo = lax.broadcasted_iota(jnp.int32, tiles[0].shape, 1) < HD
    out = []
    for k in range(NQ // 2):
        a, b = tiles[(2 * k) % QPK], tiles[(2 * k) % QPK + 1]
        out.append(jnp.where(lo, a, pltpu.roll(b, HD, 1)) if k < 2 else jnp.where(lo, pltpu.roll(a, HD, 1), b))
    return out


def _grp(width=128):
    return BS((None, QPK, TB, width), lambda b, j: (b, 0, j, 0))


_TAB = BS((TB, 128), lambda b, j: (j, 0))


def _window_start(i, S):
    return pl.multiple_of(jnp.clip((i - 1) * QB, 0, S - 3 * QB), QB)


def _attn_bias(S, Lc):
    nb = S // QB
    i = np.arange(nb)[:, None, None]
    w0 = np.clip((i - 1) * QB, 0, S - 3 * QB)
    qpos = i * QB + np.arange(QB)[None, :, None]
    kpos = w0 + np.arange(3 * QB)[None, None, :]
    ok = (i >= Lc // QB) & (kpos >= Lc) & (np.abs(qpos - kpos) <= WIN)
    return jnp.asarray(np.where(ok, 0.0, NEG), F32)


_BIAS = BS((None, QB, 3 * QB), lambda b, i: (i, 0, 0))


def _attn_probs(qh, kch, kwh, bias, sk):
    sc = _dot_nt(qh, kch)
    sw = _dot_nt(qh, kwh) + jnp.concatenate([bias] * (qh.shape[0] // QB), axis=0)
    m = jnp.maximum(jnp.maximum(jnp.max(sc, axis=-1, keepdims=True), jnp.max(sw, axis=-1, keepdims=True)), sk)
    ec = jnp.exp(sc - m)
    ew = jnp.exp(sw - m)
    es = jnp.exp(sk - m)
    inv = 1.0 / (jnp.sum(ec, axis=-1, keepdims=True) + jnp.sum(ew, axis=-1, keepdims=True) + es)
    return ec * inv, ew * inv, es * inv


def _sink_col(sink_ref, hk):
    return jnp.concatenate([jnp.full((QB, 1), sink_ref[hk * QPK + g], F32) for g in range(QPK)], axis=0)


_QGRP = BS((None, QPK, QB, 128), lambda b, i: (b, 0, i, 0))


def _first_last(grid):
    first = last = None
    for ax, n in enumerate(grid):
        i = pl.program_id(ax)
        first = (i == 0) if first is None else jnp.logical_and(first, i == 0)
        last = (i == n - 1) if last is None else jnp.logical_and(last, i == n - 1)
    return first, last


def _call_with(ex, body, name, grid, in_specs, out_specs, out_shape, args, scratch, params):
    n_in, n_out = len(in_specs), len(out_specs)
    if ex is None:
        res = pl.pallas_call(body, name=name, grid=grid, in_specs=in_specs, out_specs=out_specs,
                             out_shape=out_shape, scratch_shapes=scratch, compiler_params=params)(*args)
        return list(res), []

    def carrying(*refs):
        own, comm = _carried(ex, refs, n_in, n_out)
        first, last = _first_last(grid)
        pl.when(first)(lambda: ex.start(*comm))
        body(*own)
        pl.when(last)(lambda: ex.wait(*comm))

    res = pl.pallas_call(
        carrying, name=f"{name}_{type(ex).__name__.strip('_').lower()}", grid=grid,
        in_specs=in_specs + ex.specs, out_specs=out_specs + ex.specs, out_shape=out_shape + ex.out_shape,
        scratch_shapes=scratch + ex.sems, compiler_params=params)(*args, *ex.args)
    return list(res[:n_out]), list(res[n_out:])


def _attn_fwd(sink, q4, k, p, bias, Lc, ex=None):
    B, S, _ = k.shape

    def body(sink_ref, q_ref, k_ref, v_ref, bias_ref, o_ref):
        w0, bias = _window_start(pl.program_id(1), S), bias_ref[...]
        kc = k_ref[0:Lc, :]
        vc = v_ref[0:Lc, :].astype(MXU)
        kw = k_ref[pl.ds(w0, 3 * QB), :]
        vw = v_ref[pl.ds(w0, 3 * QB), :].astype(MXU)
        q = q_ref[...].reshape(QPK * QB, 128)
        half = QPK * QB // 2
        outs = []
        for hk in range(NKV):
            cs = slice(hk * HD, (hk + 1) * HD)
            sk = _sink_col(sink_ref, hk)
            for r0 in (0, half):
                rows = slice(r0, r0 + half)
                pc, pw, _ = _attn_probs(q[rows, cs], kc[:, cs], kw[:, cs], bias, sk[rows])
                o = _dot(pc.astype(MXU), vc[:, cs]) + _dot(pw.astype(MXU), vw[:, cs])
                outs += [o[g * QB:(g + 1) * QB] for g in range(half // QB)]
        o_ref[...] = jnp.concatenate(outs, axis=1).astype(o_ref.dtype)

    return _call_with(
        ex, body, "attn_fwd", (B, S // QB),
        [BS(memory_space=pltpu.SMEM), _QGRP, BS((None, S, KVW), lambda b, i: (b, 0, 0)),
         BS((None, S, KVW), lambda b, i: (b, 0, (AW + KVW) // KVW)), _BIAS],
        [BS((None, QB, AW), lambda b, i: (b, i, 0))], [SDS((B, S, AW), MXU)], [sink, q4, k, p, bias], [],
        _cp(("arbitrary", "arbitrary"), 40))


def _attn_bwd(sink, q4, k, p, do4, bias, Lc, ex=None):
    B, S, _ = k.shape

    def body(sink_ref, q_ref, k_ref, v_ref, do_ref, bias_ref, dq_ref, dk_ref, dv_ref, ds_ref):
        i = pl.program_id(1)

        @pl.when(i == 0)
        def _():
            dk_ref[...] = jnp.zeros_like(dk_ref)
            dv_ref[...] = jnp.zeros_like(dv_ref)
            ds_ref[...] = jnp.zeros_like(ds_ref)

        w0, bias = _window_start(i, S), bias_ref[...]
        kc = k_ref[0:Lc, :]
        vc = v_ref[0:Lc, :].astype(MXU)
        kw = k_ref[pl.ds(w0, 3 * QB), :]
        vw = v_ref[pl.ds(w0, 3 * QB), :].astype(MXU)
        q = q_ref[...].reshape(QPK * QB, 128)
        do = do_ref[...].reshape(QPK * QB, 128)
        dqs, dsinks, dkcs, dkws, dvcs, dvws = [], [], [], [], [], []
        for hk in range(NKV):
            cs = slice(hk * HD, (hk + 1) * HD)
            qh = q[:, cs]
            pc, pw, ps = _attn_probs(qh, kc[:, cs], kw[:, cs], bias, _sink_col(sink_ref, hk))
            dob = do[:, cs].astype(MXU)
            dpc = _dot_nt(dob, vc[:, cs])
            dpw = _dot_nt(dob, vw[:, cs])
            delta = jnp.sum(pc * dpc, axis=-1, keepdims=True) + jnp.sum(pw * dpw, axis=-1, keepdims=True)
            dsc = (pc * (dpc - delta)).astype(MXU)
            dsw = (pw * (dpw - delta)).astype(MXU)
            dqs.append(_dot(dsc, kc[:, cs]) + _dot(dsw, kw[:, cs]))
            dkcs.append(_dot_tn(dsc, qh))
            dkws.append(_dot_tn(dsw, qh))
            dvcs.append(_dot_tn(pc.astype(MXU), dob))
            dvws.append(_dot_tn(pw.astype(MXU), dob))
            psd = ps * delta
            dsinks += [jnp.broadcast_to(-jnp.sum(psd[g * QB:(g + 1) * QB], axis=0, keepdims=True), (1, 128))
                       for g in range(QPK)]
        for g in range(QPK):
            dq_ref[g] = jnp.concatenate([dq[g * QB:(g + 1) * QB] for dq in dqs], axis=1)
        ds_ref[...] += jnp.concatenate(dsinks, axis=0)
        dk_ref[0:Lc, :] += jnp.concatenate(dkcs, axis=1)
        dv_ref[0:Lc, :] += jnp.concatenate(dvcs, axis=1)
        dk_ref[pl.ds(w0, 3 * QB), :] += jnp.concatenate(dkws, axis=1)
        dv_ref[pl.ds(w0, 3 * QB), :] += jnp.concatenate(dvws, axis=1)

    kv_acc = BS((None, S, KVW), lambda b, i: (b, 0, 0))
    return _call_with(
        ex, body, "attn_bwd", (B, S // QB),
        [BS(memory_space=pltpu.SMEM), _QGRP, BS((None, S, KVW), lambda b, i: (b, 0, 0)),
         BS((None, S, KVW), lambda b, i: (b, 0, (AW + KVW) // KVW)), _QGRP, _BIAS],
        [_QGRP, kv_acc, kv_acc, BS((None, 8, 128), lambda b, i: (b, 0, 0))],
        [SDS((B, QPK, S, 128), F32), SDS((B, S, KVW), F32), SDS((B, S, KVW), F32), SDS((B, 8, 128), F32)],
        [sink, q4, k, p, do4, bias], [], _cp(("arbitrary", "arbitrary"), 48))


def _halo_specs(width, col, S):
    nb8 = S // 8
    per = TB // 8
    prev = BS((None, 8, width), lambda b, j: (b, jnp.maximum(j * per - 1, 0), col))
    nxt = BS((None, 8, width), lambda b, j: (b, jnp.minimum((j + 1) * per, nb8 - 1), col))
    return prev, nxt


def _shift_dn(z, prev_row, at_start):
    row = lax.broadcasted_iota(jnp.int32, z.shape, 0)
    first = jnp.where(at_start, 0.0, 1.0) * prev_row
    return jnp.where(row == 0, first, pltpu.roll(z, 1, 0))


def _shift_up(z, next_row, at_end):
    row = lax.broadcasted_iota(jnp.int32, z.shape, 0)
    last = jnp.where(at_end, 0.0, 1.0) * next_row
    return jnp.where(row == z.shape[0] - 1, last, pltpu.roll(z, z.shape[0] - 1, 0))


def _conv_fwd(p, cw8, nct):
    B, S, _ = p.shape
    nt = S // TB

    def body(p_ref, pp_ref, pn_ref, w_ref, o_ref):
        j = pl.program_id(1)
        at_start = jnp.logical_or(j == 0, j == nct)
        at_end = jnp.logical_or(j == nct - 1, j == nt - 1)
        z = p_ref[:, 256:512] * p_ref[:, 512:768]
        zprev = pp_ref[7:8, 256:512] * pp_ref[7:8, 512:768]
        znext = pn_ref[0:1, 256:512] * pn_ref[0:1, 512:768]
        c3 = (_shift_dn(z, zprev, at_start) * w_ref[0:1, :] + z * w_ref[1:2, :]
              + _shift_up(z, znext, at_end) * w_ref[2:3, :])
        o_ref[...] = p_ref[:, 0:256] * c3

    prev, nxt = _halo_specs(3 * CW, 1, S)
    return pl.pallas_call(
        body, name="conv_fwd", grid=(B, nt),
        in_specs=[_tok(3 * CW, 1), prev, nxt, _full((8, CW))],
        out_specs=_tok(CW),
        out_shape=SDS((B, S, CW), F32),
        compiler_params=_cp(("arbitrary", "arbitrary")),
    )(p, p, p, cw8)


def _conv_bwd(p, dcv, cw8, nct):
    B, S, _ = p.shape
    nt = S // TB

    def body(p_ref, pp_ref, pn_ref, d_ref, dp_ref, dn_ref, w_ref, o_ref, dw_ref):
        j = pl.program_id(1)
        at_start = jnp.logical_or(j == 0, j == nct)
        at_end = jnp.logical_or(j == nct - 1, j == nt - 1)
        cb, cc, cx = p_ref[:, 0:256], p_ref[:, 256:512], p_ref[:, 512:768]
        z = cc * cx
        zm = _shift_dn(z, pp_ref[7:8, 256:512] * pp_ref[7:8, 512:768], at_start)
        zp = _shift_up(z, pn_ref[0:1, 256:512] * pn_ref[0:1, 512:768], at_end)
        d = d_ref[...]
        e = d * cb
        em = _shift_dn(e, dp_ref[7:8, :] * pp_ref[7:8, 0:256], at_start)
        ep = _shift_up(e, dn_ref[0:1, :] * pn_ref[0:1, 0:256], at_end)
        dz = e * w_ref[1:2, :] + ep * w_ref[0:1, :] + em * w_ref[2:3, :]
        o_ref[:, 0:256] = d * (zm * w_ref[0:1, :] + z * w_ref[1:2, :] + zp * w_ref[2:3, :])
        o_ref[:, 256:512] = dz * cx
        o_ref[:, 512:768] = dz * cc
        dw_ref[...] = jnp.zeros_like(dw_ref)
        dw_ref[0:1, :] = _colsum(e * zm)
        dw_ref[1:2, :] = _colsum(e * z)
        dw_ref[2:3, :] = _colsum(e * zp)

    prev, nxt = _halo_specs(3 * CW, 1, S)
    dprev, dnxt = _halo_specs(CW, 0, S)
    return pl.pallas_call(
        body, name="conv_bwd", grid=(B, nt),
        in_specs=[_tok(3 * CW, 1), prev, nxt, _tok(CW), dprev, dnxt, _full((8, CW))],
        out_specs=[_tok(3 * CW), BS((None, None, 8, CW), lambda b, j: (b, j, 0, 0))],
        out_shape=[SDS((B, S, 3 * CW), F32), SDS((B, nt, 8, CW), F32)],
        compiler_params=_cp(("arbitrary", "arbitrary")),
    )(p, p, p, dcv, dcv, dcv, cw8)


HW = SGRP * SST


def _rev_chunk(k, nk, ncc):
    return jnp.where(k < ncc, ncc - 1 - k, nk - 1 - k + ncc)


def _scan_perm(B):
    n = B * TC
    pm = np.zeros((2 * n, 2 * n), np.float32)
    for t in range(TC):
        for e in range(B):
            pm[t * 2 * B + e, e * TC + t] = 1.0
            pm[t * 2 * B + B + e, n + e * TC + (TC - 1 - t)] = 1.0
    return pm


def _scan_drive(uf_ref, ub_ref, pm_ref, b4_ref, fwd, dbuf, n):
    u_tok = jnp.concatenate([uf_ref[...].reshape(n, SW), ub_ref[...].reshape(n, SW)], axis=0).astype(MXU)
    u_tm = _dot(pm_ref[...], u_tok).astype(MXU)
    d4 = _dot(u_tm, b4_ref[...])
    dbuf[...] = jnp.where(fwd, d4[:, :2 * HW], d4[:, 2 * HW:])
    return u_tm


def _scan_fwd(p, pm, pmt, b4, a, c2, Lc, ex=None):
    B, S, _ = p.shape
    nch, n = 2 * B, B * TC
    nk, ncc = S // TC, Lc // TC
    ucol = (INW - SW) // SW

    def body(uf_ref, ub_ref, pm_ref, pmt_ref, b4_ref, a_ref, c2_ref, yf_ref, yb_ref, hb_ref, hcar, dbuf, hbuf):
        @pl.when(pl.program_id(0) == 0)
        def _():
            hcar[...] = jnp.zeros_like(hcar)

        fwd = (lax.broadcasted_iota(jnp.int32, (2 * n, 1), 0) % nch) < B
        _scan_drive(uf_ref, ub_ref, pm_ref, b4_ref, fwd, dbuf, n)
        hb_ref[...] = hcar[...]
        ar, ai = a_ref[:, :HW], a_ref[:, HW:]

        def step(t, h):
            hr, hi = h
            r0 = pl.multiple_of(t * nch, nch)
            nr = ar * hr - ai * hi + dbuf[pl.ds(r0, nch), :HW]
            ni = ar * hi + ai * hr + dbuf[pl.ds(r0, nch), HW:]
            hbuf[pl.ds(r0, nch), :HW] = nr
            hbuf[pl.ds(r0, nch), HW:] = ni
            return nr, ni

        hr, hi = lax.fori_loop(0, TC, step, (hcar[:, :HW], hcar[:, HW:]))
        hcar[:, :HW] = hr
        hcar[:, HW:] = hi
        h_tok = _dot(pmt_ref[...], hbuf[...].astype(MXU)).astype(MXU)
        yf_ref[...] = _dot(h_tok[:n], c2_ref[:, :SW]).reshape(B, TC, SW)
        yb_ref[...] = _dot(h_tok[n:], c2_ref[:, SW:]).reshape(B, TC, SW)

    return _call_with(
        ex, body, "scan_fwd", (nk,),
        [BS((B, TC, SW), lambda k: (0, k, ucol)),
         BS((B, TC, SW), lambda k: (0, _rev_chunk(k, nk, ncc), ucol)),
         _full((2 * n, 2 * n)), _full((2 * n, 2 * n)),
         _full((SW, 4 * HW)), _full((nch, 2 * HW)), _full((2 * HW, 2 * SW))],
        [BS((B, TC, SW), lambda k: (0, k, 0)),
         BS((B, TC, SW), lambda k: (0, _rev_chunk(k, nk, ncc), 0)),
         BS((None, nch, 2 * HW), lambda k: (k, 0, 0))],
        [SDS((B, S, SW), F32), SDS((B, S, SW), F32), SDS((nk, nch, 2 * HW), F32)],
        [p, p, pm, pmt, b4, a, c2],
        [pltpu.VMEM((nch, 2 * HW), F32), pltpu.VMEM((2 * n, 2 * HW), F32), pltpu.VMEM((2 * n, 2 * HW), F32)],
        _cp(("arbitrary",), 48))


def _scan_bwd(p, dy, hb, pm, pmt, b4, a, c2, Lc, ex=None):
    B, S, _ = p.shape
    nch, n = 2 * B, B * TC
    nk, ncc = S // TC, Lc // TC
    ucol = (INW - SW) // SW
    rows = 2 * n

    def body(uf_ref, ub_ref, dyf_ref, dyb_ref, hb_ref, pm_ref, pmt_ref, b4_ref, a_ref, c2_ref,
             duf_ref, dub_ref, db4_ref, da_ref, dc2_ref, gcar, dbuf, hbuf, gbuf):
        @pl.when(pl.program_id(0) == 0)
        def _():
            gcar[...] = jnp.zeros_like(gcar)
            db4_ref[...] = jnp.zeros_like(db4_ref)
            da_ref[...] = jnp.zeros_like(da_ref)
            dc2_ref[...] = jnp.zeros_like(dc2_ref)

        fwd = (lax.broadcasted_iota(jnp.int32, (rows, 1), 0) % nch) < B
        u_tm = _scan_drive(uf_ref, ub_ref, pm_ref, b4_ref, fwd, dbuf, n)
        ar, ai = a_ref[:, :HW], a_ref[:, HW:]
        hbuf[0:nch, :] = hb_ref[...]

        def fstep(t, h):
            hr, hi = h
            r0 = pl.multiple_of(t * nch, nch)
            nr = ar * hr - ai * hi + dbuf[pl.ds(r0, nch), :HW]
            ni = ar * hi + ai * hr + dbuf[pl.ds(r0, nch), HW:]
            hbuf[pl.ds(r0 + nch, nch), :HW] = nr
            hbuf[pl.ds(r0 + nch, nch), HW:] = ni
            return nr, ni

        lax.fori_loop(0, TC, fstep, (hb_ref[:, :HW], hb_ref[:, HW:]))

        dy_tok = jnp.concatenate([dyf_ref[...].reshape(n, SW), dyb_ref[...].reshape(n, SW)], axis=0).astype(MXU)
        dy_tm = _dot(pm_ref[...], dy_tok)
        dy2 = jnp.concatenate([jnp.where(fwd, dy_tm, 0.0), jnp.where(fwd, 0.0, dy_tm)], axis=1).astype(MXU)
        gbuf[...] = _dot_nt(dy2, c2_ref[...])
        dc2_ref[...] += _dot_tn(hbuf[nch:, :].astype(MXU), dy2)

        def bstep(i, cary):
            cr, ci = cary
            r0 = pl.multiple_of((TC - 1 - i) * nch, nch)
            gr = gbuf[pl.ds(r0, nch), :HW] + cr
            gi = gbuf[pl.ds(r0, nch), HW:] + ci
            gbuf[pl.ds(r0, nch), :HW] = gr
            gbuf[pl.ds(r0, nch), HW:] = gi
            return ar * gr + ai * gi, ar * gi - ai * gr

        cr, ci = lax.fori_loop(0, TC, bstep, (gcar[:, :HW], gcar[:, HW:]))
        gcar[:, :HW] = cr
        gcar[:, HW:] = ci

        gr = gbuf[:, :HW].reshape(TC, nch, HW)
        gi = gbuf[:, HW:].reshape(TC, nch, HW)
        hpr = hbuf[0:rows, :HW].reshape(TC, nch, HW)
        hpi = hbuf[0:rows, HW:].reshape(TC, nch, HW)
        da_ref[:, :HW] += jnp.sum(gr * hpr + gi * hpi, axis=0)
        da_ref[:, HW:] += jnp.sum(gi * hpr - gr * hpi, axis=0)

        g = gbuf[...]
        gm = g.astype(MXU)
        dd4 = jnp.concatenate([jnp.where(fwd, g, 0.0), jnp.where(fwd, 0.0, g)], axis=1).astype(MXU)
        db4_ref[...] += _dot_tn(u_tm, dd4)
        g_tok = _dot(pmt_ref[...], gm).astype(MXU)
        duf_ref[...] = _dot_nt(g_tok[:n], b4_ref[:, :2 * HW]).reshape(B, TC, SW)
        dub_ref[...] = _dot_nt(g_tok[n:], b4_ref[:, 2 * HW:]).reshape(B, TC, SW)

    fwd_blk = lambda col: BS((B, TC, SW), lambda k: (0, nk - 1 - k, col))
    bwd_blk = lambda col: BS((B, TC, SW), lambda k: (0, _rev_chunk(nk - 1 - k, nk, ncc), col))
    return _call_with(
        ex, body, "scan_bwd", (nk,),
        [fwd_blk(ucol), bwd_blk(ucol), fwd_blk(0), bwd_blk(0),
         BS((None, nch, 2 * HW), lambda k: (nk - 1 - k, 0, 0)),
         _full((rows, rows)), _full((rows, rows)),
         _full((SW, 4 * HW)), _full((nch, 2 * HW)), _full((2 * HW, 2 * SW))],
        [fwd_blk(0), bwd_blk(0), _full((SW, 4 * HW)), _full((nch, 2 * HW)), _full((2 * HW, 2 * SW))],
        [SDS((B, S, SW), F32), SDS((B, S, SW), F32), SDS((SW, 4 * HW), F32), SDS((nch, 2 * HW), F32),
         SDS((2 * HW, 2 * SW), F32)],
        [p, p, dy, dy, hb, pm, pmt, b4, a, c2],
        [pltpu.VMEM((nch, 2 * HW), F32), pltpu.VMEM((rows, 2 * HW), F32),
         pltpu.VMEM((rows + nch, 2 * HW), F32), pltpu.VMEM((rows, 2 * HW), F32)],
        _cp(("arbitrary",), 56))


def _post_mix(o, cv, yf, yb, p, h, modtok, ng, dsk, wg, bg, wo, nct, ex=None):
    B, S, _ = h.shape

    def body(o_ref, cv_ref, yf_ref, yb_ref, u_ref, h_ref, mod_ref, g_ref, dsk_ref, wg_ref, bg_ref, wo_ref,
             h1_ref, mix_ref, m_ref, y_ref):
        y = yf_ref[...] + yb_ref[...] + dsk_ref[...] * u_ref[...]
        g = _gelu(y)
        s = g * _sigmoid(_dot(g.astype(MXU), wg_ref[...]) + bg_ref[...])
        mix_ref[:, 0:AW] = o_ref[...]
        mix_ref[:, AW:AW + CW] = cv_ref[...].astype(MXU)
        mix_ref[:, AW + CW:] = s.astype(MXU)
        m = _dot(mix_ref[...], wo_ref[...])
        n, _ = _rms(m)
        h1_ref[...] = h_ref[...] + mod_ref[2:3, :] * (n * g_ref[1:2, :])
        m_ref[...] = m
        y_ref[...] = y

    return _call_with(
        ex, body, "post_mix", (B, S // TB),
        [_tok(AW), _tok(CW), _tok(SW), _tok(SW), _tok(SW, (INW - SW) // SW), _tok(D), _mod_spec(nct),
         _full((8, D)), _full((1, SW)), _full((SW, SW)), _full((1, SW)), _full((D, D))],
        [_tok(D), _tok(D), _tok(D), _tok(SW)],
        [SDS((B, S, D), F32), SDS((B, S, D), MXU), SDS((B, S, D), F32), SDS((B, S, SW), F32)],
        [o, cv, yf, yb, p, h, modtok, ng, dsk, wg, bg, wo], [], _cp(("arbitrary", "arbitrary"), 40))


def _post_bwd(dh1, m, y, p, modtok, ng, dsk, wg, bg, wo, nct):
    B, S, _ = m.shape

    def body(dh_ref, m_ref, y_ref, u_ref, mod_ref, g_ref, dsk_ref, wg_ref, bg_ref, wo_ref,
             dm_ref, da_ref, dc_ref, dy_ref, du_ref, vec_ref, vec2_ref, dwg_ref):
        @pl.when(jnp.logical_and(pl.program_id(0) == 0, pl.program_id(1) == 0))
        def _():
            dwg_ref[...] = jnp.zeros_like(dwg_ref)

        n, r = _rms(m_ref[...])
        dh1 = dh_ref[...]
        gpm = g_ref[1:2, :]
        dr = dh1 * mod_ref[2:3, :]
        vec_ref[...] = jnp.zeros_like(vec_ref)
        vec_ref[0:1, :] = _colsum(dh1 * (n * gpm))
        vec_ref[1:2, :] = _colsum(dr * n)
        dm = _rms_bwd(dr * gpm, n, r).astype(MXU)
        dm_ref[...] = dm
        dmix = _dot_nt(dm, wo_ref[...])
        for g, t in enumerate(_to_groups([dmix[:, k * 128:(k + 1) * 128] for k in range(AW // 128)])):
            da_ref[g] = t
        dc_ref[...] = dmix[:, AW:AW + CW]
        ds = dmix[:, AW + CW:]
        yv = y_ref[...]
        g = _gelu(yv)
        gb = g.astype(MXU)
        sg = _sigmoid(_dot(gb, wg_ref[...]) + bg_ref[...])
        dv = ds * g * sg * (1.0 - sg)
        dvb = dv.astype(MXU)
        dg = ds * sg + _dot_nt(dvb, wg_ref[...])
        dwg_ref[...] += _dot_tn(gb, dvb)
        dy = dg * _gelu_grad(yv)
        dy_ref[...] = dy
        du_ref[...] = dy * dsk_ref[...]
        vec2_ref[...] = jnp.zeros_like(vec2_ref)
        vec2_ref[0:1, :] = _colsum(dy * u_ref[...])
        vec2_ref[1:2, :] = _colsum(dv)

    nt = S // TB
    return pl.pallas_call(
        body, name="post_bwd", grid=(B, nt),
        in_specs=[_tok(D), _tok(D), _tok(SW), _tok(SW, (INW - SW) // SW), _mod_spec(nct), _full((8, D)),
                  _full((1, SW)), _full((SW, SW)), _full((1, SW)), _full((D, D))],
        out_specs=[_tok(D), _grp(), _tok(CW), _tok(SW), _tok(SW),
                   BS((None, None, 8, D), lambda b, j: (b, j, 0, 0)),
                   BS((None, None, 8, SW), lambda b, j: (b, j, 0, 0)), _full((SW, SW))],
        out_shape=[SDS((B, S, D), MXU), SDS((B, QPK, S, 128), F32), SDS((B, S, CW), F32), SDS((B, S, SW), F32),
                   SDS((B, S, SW), F32), SDS((B, nt, 8, D), F32), SDS((B, nt, 8, SW), F32), SDS((SW, SW), F32)],
        compiler_params=_cp(("arbitrary", "arbitrary"), 40),
    )(dh1, m, y, p, modtok, ng, dsk, wg, bg, wo)


def _mlp_fwd(h1, modtok, ng, w1, w2, nct, ex=None):
    B, S, _ = h1.shape
    assert w1.shape == (DFF // TF, D, TF)

    def body(h_ref, mod_ref, g_ref, w1_ref, w2_ref, h2_ref, a2_ref, z_ref, f_ref):
        n, _ = _rms(h_ref[...])
        a2 = ((n * g_ref[2:3, :]) * (1.0 + mod_ref[4:5, :]) + mod_ref[3:4, :]).astype(MXU)
        a2_ref[...] = a2
        ff = jnp.zeros((TB, D), F32)
        for c in range(DFF // TF):
            cs = slice(c * TF, (c + 1) * TF)
            z = _dot(a2, w1_ref[c])
            z_ref[:, cs] = z.astype(MXU)
            r = jnp.maximum(z, 0.0)
            ff = ff + _dot((r * r).astype(MXU), w2_ref[cs, :])
        f_ref[...] = ff
        n, _ = _rms(ff)
        h2_ref[...] = h_ref[...] + mod_ref[5:6, :] * (n * g_ref[3:4, :])

    return _call_with(
        ex, body, "mlp_fwd", (B, S // TB),
        [_tok(D), _mod_spec(nct), _full((8, D)), _resident((DFF // TF, D, TF)), _resident((DFF, D))],
        [_tok(D), _tok(D), _tok(DFF), _tok(D)],
        [SDS((B, S, D), F32), SDS((B, S, D), MXU), SDS((B, S, DFF), MXU), SDS((B, S, D), F32)],
        [h1, modtok, ng, w1, w2], [], _cp(("arbitrary", "arbitrary"), 48))


def _mlp_bwd(dh2, h1, f, z, modtok, ng, w1, w2, nct, ex=None):
    B, S, _ = h1.shape

    def body(dh_ref, h_ref, f_ref, z_ref, mod_ref, g_ref, w1_ref, w2_ref, dh1_ref, df_ref, dz_ref, vec_ref):
        n, r = _rms(f_ref[...])
        dh2 = dh_ref[...]
        gp = g_ref[3:4, :]
        dr = dh2 * mod_ref[5:6, :]
        vec_ref[...] = jnp.zeros_like(vec_ref)
        vec_ref[3:4, :] = _colsum(dh2 * (n * gp))
        vec_ref[4:5, :] = _colsum(dr * n)
        df = _rms_bwd(dr * gp, n, r).astype(MXU)
        df_ref[...] = df
        da = jnp.zeros((TB, D), F32)
        for c in range(DFF // TF):
            cs = slice(c * TF, (c + 1) * TF)
            dr2 = _dot_nt(df, w2_ref[cs, :])
            dz = (dr2 * 2.0 * jnp.maximum(z_ref[:, cs].astype(F32), 0.0)).astype(MXU)
            dz_ref[:, cs] = dz
            da = da + _dot_nt(dz, w1_ref[c])
        n, r = _rms(h_ref[...])
        g = g_ref[2:3, :]
        sc1 = 1.0 + mod_ref[4:5, :]
        vec_ref[0:1, :] = _colsum(da)
        vec_ref[1:2, :] = _colsum(da * (n * g))
        vec_ref[2:3, :] = _colsum(da * sc1 * n)
        dh1_ref[...] = dh2 + _rms_bwd(da * sc1 * g, n, r)

    return _call_with(
        ex, body, "mlp_bwd", (B, S // TB),
        [_tok(D), _tok(D), _tok(D), _tok(DFF), _mod_spec(nct), _full((8, D)),
         _resident((DFF // TF, D, TF)), _resident((DFF, D))],
        [_tok(D), _tok(D), _tok(DFF), BS((None, None, 8, D), lambda b, j: (b, j, 0, 0))],
        [SDS((B, S, D), F32), SDS((B, S, D), MXU), SDS((B, S, DFF), MXU), SDS((B, S // TB, 8, D), F32)],
        [dh2, h1, f, z, modtok, ng, w1, w2], [], _cp(("arbitrary", "arbitrary"), 56))


def _loss(h, target, Lc):
    B, S, _ = h.shape
    nt, nct = S // TB, Lc // TB

    def body(h_ref, t_ref, dh_ref, l_ref):
        lat = pl.program_id(1) >= nct
        err = jnp.where(lat, h_ref[...] - t_ref[...], 0.0)
        dh_ref[...] = err * (1.0 / D)
        l_ref[...] = jnp.broadcast_to(jnp.sum(err * err, keepdims=True), (8, 128))

    return pl.pallas_call(
        body, name="loss", grid=(B, nt),
        in_specs=[_tok(D), BS((None, TB, D), lambda b, j: (b, jnp.maximum(j - nct, 0), 0))],
        out_specs=[_tok(D), BS((None, None, 8, 128), lambda b, j: (b, j, 0, 0))],
        out_shape=[SDS((B, S, D), F32), SDS((B, nt, 8, 128), F32)],
        compiler_params=_cp(("arbitrary", "arbitrary")),
    )(h, target)


def _mm_tn(a, b, name, relu2=False):
    T, M = a.shape
    N = b.shape[1]
    def blk(n):
        return max(b for b in range(128, 1025, 128) if n % b == 0)

    bm, bn = blk(M), blk(N)
    tk = next(t for t in (1536, 1024, 512, 256, T) if T % t == 0)
    nkk = T // tk

    def body(a_ref, b_ref, o_ref, om_ref):
        k = pl.program_id(2)

        @pl.when(k == 0)
        def _():
            o_ref[...] = jnp.zeros_like(o_ref)

        av = a_ref[...]
        if relu2:
            r = jnp.maximum(av.astype(F32), 0.0)
            av = (r * r).astype(MXU)
        o_ref[...] += _dot_tn(av, b_ref[...])

        @pl.when(k == nkk - 1)
        def _():
            om_ref[...] = o_ref[...].astype(MXU)

    out = BS((bm, bn), lambda i, j, k: (i, j))
    return pl.pallas_call(
        body, name=name, grid=(M // bm, N // bn, nkk),
        in_specs=[BS((tk, bm), lambda i, j, k: (k, i)), BS((tk, bn), lambda i, j, k: (k, j))],
        out_specs=[out, out], out_shape=[SDS((M, N), F32), SDS((M, N), MXU)],
        compiler_params=_cp(("arbitrary", "arbitrary", "arbitrary"), 56),
    )(a, b)


_PAD_ROWS = 64


def _block_rows(R):
    for br in (256, _PAD_ROWS):
        if R % br == 0:
            return br
    raise ValueError(f"row count {R} is not a multiple of {_PAD_ROWS}")


def _adamw(w, ga, gb, m, v, name):
    R, C = w.shape
    br = _block_rows(R) if C <= _LANES else _PAD_ROWS

    def body(w_ref, ga_ref, gb_ref, m_ref, v_ref, g_out, d_out, m_out, v_out):
        g = ga_ref[...] + gb_ref[...]
        m2 = B1 * m_ref[...] + (1.0 - B1) * g
        v2 = B2 * v_ref[...] + (1.0 - B2) * (g * g)
        m_hat = m2 / (1.0 - B1 ** STEP)
        v_hat = v2 / (1.0 - B2 ** STEP)
        g_out[...] = g
        d_out[...] = -LR * (m_hat / (jnp.sqrt(v_hat) + AEPS) + WD * w_ref[...])
        m_out[...] = m2
        v_out[...] = v2

    own = BS((br, C), lambda i: (i, 0))
    return pl.pallas_call(
        body, name=name, grid=(R // br,),
        in_specs=[own] * 5, out_specs=[own] * 4, out_shape=[SDS((R, C), F32)] * 4,
        compiler_params=_cp(("arbitrary",)),
    )(w, ga, gb, m, v)


def _my_xyc():
    return lax.axis_index("x"), lax.axis_index("y"), lax.axis_index("c")


def _chip_peers(x, y):
    return [(1 - x, y), (x, 1 - y), (1 - x, 1 - y)]


class _Gather:
    def __init__(self, shards):
        self.args = list(shards)
        self.n = len(self.args)
        self.specs = [ANY] * self.n
        self.out_shape = [SDS((NCHIP,) + a.shape, a.dtype) for a in self.args]
        self.sems = [pltpu.SemaphoreType.DMA((3 * self.n,)), pltpu.SemaphoreType.DMA((3 * self.n,)),
                     pltpu.SemaphoreType.DMA((self.n,))]

    def _copies(self, ins, outs, ssem, rsem, lsem):
        x, y, c = _my_xyc()
        mine, sends, recvs = [], [], []
        for k, (s_ref, o_ref) in enumerate(zip(ins, outs)):
            mine.append(pltpu.make_async_copy(s_ref, o_ref.at[2 * x + y], lsem.at[k]))
            for j, (px, py) in enumerate(_chip_peers(x, y)):
                sems = dict(send_sem=ssem.at[3 * k + j], recv_sem=rsem.at[3 * k + j], device_id=(px, py, c),
                            device_id_type=MESH)
                sends.append(pltpu.make_async_remote_copy(src_ref=s_ref, dst_ref=o_ref.at[2 * x + y], **sems))
                recvs.append(pltpu.make_async_remote_copy(src_ref=s_ref, dst_ref=o_ref.at[2 * px + py], **sems))
        return mine, sends, recvs

    def start(self, ins, outs, sems):
        mine, sends, _ = self._copies(ins, outs, *sems)
        for cp in mine + sends:
            cp.start()

    def wait(self, ins, outs, sems):
        mine, sends, recvs = self._copies(ins, outs, *sems)
        for cp in recvs:
            cp.wait_recv()
        for cp in sends:
            cp.wait_send()
        for cp in mine:
            cp.wait()


class _Gather2(_Gather):
    def __init__(self, shards):
        super().__init__(shards)
        self.sems = self.sems + [pltpu.SemaphoreType.DMA((3 * self.n,)), pltpu.SemaphoreType.DMA((3 * self.n,))]

    def _copies2(self, ins, outs, ssem, rsem, lsem, fsem, gsem):
        x, y, c = _my_xyc()
        mine, sends, recvs, passes, got = [], [], [], [], []
        for k, (s_ref, o_ref) in enumerate(zip(ins, outs)):
            half = s_ref.shape[0] // 2
            my_rows, sib_rows = pl.ds(c * half, half), pl.ds((1 - c) * half, half)
            mine.append(pltpu.make_async_copy(s_ref, o_ref.at[2 * x + y], lsem.at[k]))
            for j, (px, py) in enumerate(_chip_peers(x, y)):
                ici = dict(send_sem=ssem.at[3 * k + j], recv_sem=rsem.at[3 * k + j], device_id=(px, py, c),
                           device_id_type=MESH)
                d2d = dict(send_sem=fsem.at[3 * k + j], recv_sem=gsem.at[3 * k + j], device_id=(x, y, 1 - c),
                           device_id_type=MESH)
                landed = o_ref.at[2 * px + py, my_rows]
                sends.append(pltpu.make_async_remote_copy(src_ref=s_ref.at[my_rows],
                                                          dst_ref=o_ref.at[2 * x + y, my_rows], **ici))
                recvs.append(pltpu.make_async_remote_copy(src_ref=s_ref.at[my_rows], dst_ref=landed, **ici))
                passes.append(pltpu.make_async_remote_copy(src_ref=landed, dst_ref=landed, **d2d))
                got.append(pltpu.make_async_remote_copy(src_ref=landed, dst_ref=o_ref.at[2 * px + py, sib_rows],
                                                        **d2d))
        return mine, sends, recvs, passes, got

    def start(self, ins, outs, sems):
        mine, sends, _, _, _ = self._copies2(ins, outs, *sems)
        for cp in mine + sends:
            cp.start()

    def wait(self, ins, outs, sems):
        mine, sends, recvs, passes, got = self._copies2(ins, outs, *sems)
        for landed, onward in zip(recvs, passes):
            landed.wait_recv()
            onward.start()
        for cp in got:
            cp.wait_recv()
        for cp in sends + passes:
            cp.wait_send()
        for cp in mine:
            cp.wait()


class _Scatter:
    def __init__(self, sends, cols=None):
        self.args = list(sends)
        self.n = len(self.args)
        self.cols = list(cols) if cols is not None else [None] * self.n
        self.specs = [ANY] * self.n
        self.out_shape = [SDS((3,) + a.shape[1:] if cw is None else (3, a.shape[0], cw), a.dtype)
                          for a, cw in zip(self.args, self.cols)]
        self.sems = [pltpu.SemaphoreType.DMA((3 * self.n,)), pltpu.SemaphoreType.DMA((3 * self.n,))]

    def _copies(self, ins, outs, ssem, rsem):
        x, y, c = _my_xyc()

        def piece(k, q):
            cw = self.cols[k]
            return ins[k].at[q] if cw is None else ins[k].at[:, pl.ds(pl.multiple_of(q * cw, 128), cw)]

        return [pltpu.make_async_remote_copy(
            src_ref=piece(k, 2 * px + py), dst_ref=outs[k].at[j], send_sem=ssem.at[3 * k + j],
            recv_sem=rsem.at[3 * k + j], device_id=(px, py, c), device_id_type=MESH)
            for k in range(self.n) for j, (px, py) in enumerate(_chip_peers(x, y))]

    def start(self, ins, outs, sems):
        for cp in self._copies(ins, outs, *sems):
            cp.start()

    def wait(self, ins, outs, sems):
        for cp in self._copies(ins, outs, *sems):
            cp.wait()


class _Swap(_Scatter):
    def __init__(self, arrays):
        self.args = list(arrays)
        self.n = len(self.args)
        self.specs = [ANY] * self.n
        self.out_shape = [SDS(a.shape, a.dtype) for a in self.args]
        self.sems = [pltpu.SemaphoreType.DMA((self.n,)), pltpu.SemaphoreType.DMA((self.n,))]

    def _copies(self, ins, outs, ssem, rsem):
        x, y, c = _my_xyc()
        return [pltpu.make_async_remote_copy(src_ref=s_ref, dst_ref=o_ref, send_sem=ssem.at[k], recv_sem=rsem.at[k],
                                             device_id=(x, y, 1 - c), device_id_type=MESH)
                for k, (s_ref, o_ref) in enumerate(zip(ins, outs))]


def _exchange_call(ex, name):
    n = ex.n

    def body(*refs):
        ins, outs, sems = refs[:n], refs[n:2 * n], refs[2 * n:]
        ex.start(ins, outs, sems)
        ex.wait(ins, outs, sems)

    return pl.pallas_call(body, name=name, in_specs=ex.specs, out_specs=ex.specs, out_shape=ex.out_shape,
                          scratch_shapes=ex.sems)(*ex.args)


def _carried(ex, refs, n_in, n_out):
    n = ex.n
    ins, cin = refs[:n_in], refs[n_in:n_in + n]
    outs, cout = refs[n_in + n:n_in + n + n_out], refs[n_in + n + n_out:n_in + 2 * n + n_out]
    rest = refs[n_in + 2 * n + n_out:]
    nsem = len(ex.sems)
    return list(ins) + list(outs) + list(rest[:len(rest) - nsem]), (cin, cout, rest[len(rest) - nsem:])


def _gather_all(v, name):
    R, C = v.shape

    def body(s_ref, o_ref, ssem, rsem, lsem):
        x, y, c = _my_xyc()
        me = 4 * x + 2 * y + c
        mine = pltpu.make_async_copy(s_ref, o_ref.at[me], lsem)
        mine.start()

        def peer(j):
            fx, fy, fc = (j >> 2) & 1, (j >> 1) & 1, j & 1
            return (x ^ fx, y ^ fy, c ^ fc)

        cps = []
        for j in range(1, 8):
            cps.append(pltpu.make_async_remote_copy(
                src_ref=s_ref, dst_ref=o_ref.at[me], send_sem=ssem.at[j - 1], recv_sem=rsem.at[j - 1],
                device_id=peer(j), device_id_type=MESH))
            cps[-1].start()
        for j in range(1, 8):
            px, py, pc = peer(j)
            pltpu.make_async_remote_copy(
                src_ref=s_ref, dst_ref=o_ref.at[4 * px + 2 * py + pc], send_sem=ssem.at[j - 1],
                recv_sem=rsem.at[j - 1], device_id=peer(j), device_id_type=MESH).wait_recv()
        for cp in cps:
            cp.wait_send()
        mine.wait()

    return pl.pallas_call(
        body, name=name, in_specs=[ANY], out_specs=ANY, out_shape=SDS((8, R, C), v.dtype),
        scratch_shapes=[pltpu.SemaphoreType.DMA((7,)), pltpu.SemaphoreType.DMA((7,)), pltpu.SemaphoreType.DMA],
    )(v)


def _sum_slots(own, slots, name):
    n, R, C = slots.shape
    br = _block_rows(R)

    def body(*refs):
        o_ref = refs[-1]
        if own is None:
            acc = refs[0][0].astype(F32)
            first = 1
            s_ref = refs[0]
        else:
            acc = refs[0][...]
            first = 0
            s_ref = refs[1]
        for k in range(first, n):
            acc = acc + s_ref[k].astype(F32)
        o_ref[...] = acc

    row = BS((br, C), lambda i: (i, 0))
    slab = BS((n, br, C), lambda i: (0, i, 0))
    ins, args = ([slab], [slots]) if own is None else ([row, slab], [own, slots])
    return pl.pallas_call(
        body, name=name, grid=(R // br,), in_specs=ins, out_specs=row, out_shape=SDS((R, C), F32),
        compiler_params=_cp(("arbitrary",)),
    )(*args)


def _rope_tables(L, Lc):
    n = jnp.arange(L)
    row = (n // GRID_W).astype(F32)
    col = (n % GRID_W).astype(F32)
    freqs = ROPE_BASE ** (-jnp.arange(16, dtype=F32) / 16)
    lane = jnp.arange(128)
    dd = lane % HD
    fr = freqs[dd % 16]
    ang = jnp.where(dd < 32, row[:, None], col[:, None]) * fr[None, :]
    sign = jnp.where((dd % 32) < 16, -1.0, 1.0)
    cos = jnp.concatenate([jnp.ones((Lc, 128), F32), jnp.cos(ang)], axis=0)
    sin = jnp.concatenate([jnp.zeros((Lc, 128), F32), jnp.sin(ang) * sign[None, :]], axis=0)
    return cos, sin


def _ssm_prep(lam_re, lam_im, log_dt, b_re, b_im, c_re, c_im, B):
    dt = jnp.exp(log_dt)[..., None]
    mag = jnp.exp(lam_re * dt)
    ar = mag * jnp.cos(lam_im * dt)
    ai = mag * jnp.sin(lam_im * dt)
    den = lam_re * lam_re + lam_im * lam_im
    kr = ((ar - 1.0) * lam_re + ai * lam_im) / den
    ki = (ai * lam_re - (ar - 1.0) * lam_im) / den
    bbr = kr[..., None] * b_re - ki[..., None] * b_im
    bbi = kr[..., None] * b_im + ki[..., None] * b_re
    eye = jnp.eye(SGRP, dtype=F32)

    def bblk(t):
        return jnp.einsum("gpi,gh->gihp", t, eye).reshape(SGRP * SCH, SGRP * SST)

    def cblk(t):
        return jnp.einsum("gip,gh->gphi", t, eye).reshape(SGRP * SST, SGRP * SCH)

    b4 = jnp.concatenate([bblk(bbr[0]), bblk(bbi[0]), bblk(bbr[1]), bblk(bbi[1])], axis=1)
    c2 = jnp.concatenate([jnp.concatenate([cblk(c_re[0]), -cblk(c_im[0])], axis=0),
                          jnp.concatenate([cblk(c_re[1]), -cblk(c_im[1])], axis=0)], axis=1)
    a2 = jnp.concatenate([ar.reshape(2, -1), ai.reshape(2, -1)], axis=1)
    a = jnp.repeat(a2, B, axis=0)
    return a, b4, c2


def _pad_rows(v, rows=8):
    return jnp.concatenate([v, jnp.zeros((rows - v.shape[0],) + v.shape[1:], v.dtype)], axis=0)


_BIG = ("w_ada", "w_in", "w_out", "w_mlp_in", "w_mlp_out")
_BIG_AXIS = {"w_ada": 2, "w_in": 2, "w_out": 1, "w_mlp_in": 2, "w_mlp_out": 1}


def _join_shards(s, axis):
    _, L, r, c = s.shape
    if axis == 2:
        return jnp.transpose(s, (1, 2, 0, 3)).reshape(L, r, NCHIP * c)
    return jnp.transpose(s, (1, 0, 2, 3)).reshape(L, NCHIP * r, c)


def _local_step(x, c, ctx, loss_target, P, ex):
    B, L, _ = x.shape
    Lc = ctx.shape[1]
    S = Lc + L
    nct = Lc // TB
    nt = S // TB
    cos, sin = _rope_tables(L, Lc)
    nsin = -sin
    bias = _attn_bias(S, Lc)
    perm = _scan_perm(B)
    pm, pmt = jnp.asarray(perm, MXU), jnp.asarray(perm.T, MXU)

    c_act = jax.nn.silu(c)
    cc_act, silu_vjp = jax.vjp(jax.nn.silu, P["c_ctx"])
    cact8 = _pad_rows(jnp.concatenate([c_act, cc_act[None, :]], axis=0))
    h = jnp.concatenate([ctx, x], axis=1)

    ssm_names = ("ssm_lam_re", "ssm_lam_im", "ssm_log_dt", "ssm_b_re", "ssm_b_im", "ssm_c_re", "ssm_c_im")
    (a_all, b4_all, c2_all), prep_vjp = jax.vjp(jax.vmap(lambda *t: _ssm_prep(*t, B)), *[P[n] for n in ssm_names])
    b4m_all, c2m_all = b4_all.astype(MXU), c2_all.astype(MXU)
    ng_all = jnp.pad(P["norm_g"], ((0, 0), (0, 4), (0, 0)))
    cw8_all = jnp.pad(P["conv_w"], ((0, 0), (0, 8 - P["conv_w"].shape[1]), (0, 0)))
    wg_all = P["w_glu"].astype(MXU)

    saved = []
    for l in range(NLAYER):
        mod8 = _ada_fwd(cact8, ex.w(l, "w_ada"), P["b_ada"][l][None, :])
        mod6 = mod8.reshape(8, NMOD, D)
        modtok = jnp.stack([jnp.broadcast_to(mod6[B], (B, NMOD, D)), mod6[:B]], axis=1)
        modtok = jnp.pad(modtok, ((0, 0), (0, 0), (0, 2), (0, 0)))
        ng, cw8, a_s, b4m, c2m, wg = ng_all[l], cw8_all[l], a_all[l], b4m_all[l], c2m_all[l], wg_all[l]
        dsk = P["ssm_d"][l][None, :]
        bg = P["b_glu"][l][None, :]

        p, a, q4, kr = ex.pre_mix(l, h, modtok, ng, ex.w(l, "w_in"), cos, sin, nct)
        (o,) = ex.attn_fwd(l, P["attn_sink"][l], q4, kr, p, bias, Lc)
        cv = _conv_fwd(p, cw8, nct)
        yf, yb, hb = ex.scan_fwd(l, p, pm, pmt, b4m, a_s, c2m, Lc)
        h1, mix, m, y = ex.post_mix(l, o, cv, yf, yb, p, h, modtok, ng, dsk, wg, bg, ex.w(l, "w_out"), nct)
        h2, a2, z, f = ex.mlp_fwd(l, h1, modtok, ng, ex.w(l, "w_mlp_in"), ex.w(l, "w_mlp_out"), nct)
        saved.append(dict(h=h, modtok=modtok, ng=ng, cw8=cw8, a_s=a_s, b4m=b4m, c2m=c2m, wg=wg, dsk=dsk, bg=bg,
                          p=p, a=a, q4=q4, kr=kr, hb=hb, mix=mix, m=m, y=y, h1=h1, a2=a2, z=z, f=f))
        h = h2

    dh, lpart = _loss(h, loss_target, Lc)
    loss_local = 0.5 / D * jnp.sum(lpart[:, :, 0, 0])

    R = {k: [None] * NLAYER for k in ("w_glu", "vec_p2", "scan", "dcw", "dsink", "vec", "b_ada")}
    dcact = jnp.zeros((8, D), F32)
    T = B * S
    for l in reversed(range(NLAYER)):
        sv = saved[l]
        modtok, ng = sv["modtok"], sv["ng"]
        dh1, df, dz, vec_m = ex.mlp_bwd(l, dh, sv["h1"], sv["f"], sv["z"], modtok, ng, ex.w(l, "w_mlp_in"),
                                        ex.w(l, "w_mlp_out"), nct)
        ex.submit(l, "w_mlp_out", *_mm_tn(sv["z"].reshape(T, DFF), df.reshape(T, D), "dw_mlp_out", relu2=True))
        ex.submit(l, "w_mlp_in", *_mm_tn(sv["a2"].reshape(T, D), dz.reshape(T, DFF), "dw_mlp_in"))
        dm, dattn, dcv, dy, du_skip, vec_p, vec_p2, dwg = _post_bwd(
            dh1, sv["m"], sv["y"], sv["p"], modtok, ng, sv["dsk"], sv["wg"], sv["bg"], ex.w(l, "w_out"), nct)
        ex.submit(l, "w_out", *_mm_tn(sv["mix"].reshape(T, D), dm.reshape(T, D), "dw_out"))
        R["w_glu"][l], R["vec_p2"][l] = dwg, vec_p2
        duf, dub, db4, da_s, dc2 = ex.scan_bwd(l, sv["p"], dy, sv["hb"], pm, pmt, sv["b4m"], sv["a_s"], sv["c2m"],
                                               Lc)
        R["scan"][l] = (da_s, db4, dc2)
        dconv, R["dcw"][l] = _conv_bwd(sv["p"], dcv, sv["cw8"], nct)
        dq_r, dk_r, dv, R["dsink"][l] = ex.attn_bwd(l, P["attn_sink"][l], sv["q4"], sv["kr"], sv["p"], dattn, bias,
                                                      Lc)
        dh, vec_i, dp = _pre_bwd(dq_r, dk_r, cos, nsin, dv, dconv, (du_skip, duf, dub), dh1, sv["h"], modtok, ng,
                                 ex.w(l, "w_in"), nct)
        ex.submit(l, "w_in", *_mm_tn(sv["a"].reshape(T, D), dp.reshape(T, INW), "dw_in"))
        vec = jnp.concatenate([vec_i, vec_p, vec_m], axis=2)
        R["vec"][l] = vec
        mod_rows = jnp.concatenate([vec[:, :, 0:2], vec[:, :, 8:9], vec[:, :, 16:18], vec[:, :, 19:20]], axis=2)
        dmod8 = jnp.concatenate([jnp.sum(mod_rows[:, nct:], axis=1).reshape(B, NMOD * D),
                                 jnp.sum(mod_rows[:, :nct], axis=(0, 1)).reshape(1, NMOD * D),
                                 jnp.zeros((7 - B, NMOD * D), F32)], axis=0)
        R["b_ada"][l] = dmod8
        dmod8m = dmod8.astype(MXU)
        ex.submit(l, "w_ada", *_mm_tn(cact8.astype(MXU), dmod8m, "dw_ada"))
        dcact = dcact + _ada_bwd_c(dmod8m, ex.w(l, "w_ada"))

    stk = lambda k: jnp.stack(R[k])
    vec, vec_p2 = jnp.sum(stk("vec"), axis=(1, 2)), jnp.sum(stk("vec_p2"), axis=(1, 2))
    grads = dict(zip(ssm_names, prep_vjp(tuple(jnp.stack([R["scan"][l][k] for l in range(NLAYER)])
                                               for k in range(3)))))
    norm_rows = jnp.concatenate([vec[:, 2:3], vec[:, 9:10], vec[:, 18:19], vec[:, 20:21]], axis=1)
    grads.update(w_glu=stk("w_glu"), ssm_d=vec_p2[:, 0], b_glu=vec_p2[:, 1], norm_g=norm_rows,
                 conv_w=jnp.sum(stk("dcw"), axis=(1, 2))[:, 0:P["conv_w"].shape[1]],
                 attn_sink=jnp.sum(stk("dsink")[..., 0], axis=1), b_ada=jnp.sum(stk("b_ada"), axis=1),
                 c_ctx=silu_vjp(dcact[B])[0])
    return loss_local, dh[:, Lc:, :], grads


_WEIGHTS = ["c_ctx", "w_ada", "b_ada", "norm_g", "w_in", "conv_w", "attn_sink", "ssm_lam_re", "ssm_lam_im",
            "ssm_log_dt", "ssm_b_re", "ssm_b_im", "ssm_c_re", "ssm_c_im", "ssm_d", "w_glu", "b_glu", "w_out",
            "w_mlp_in", "w_mlp_out"]
_SMALL = [n for n in _WEIGHTS if n not in _BIG]
_SMALL_SHARDED = {"norm_g": 2, "conv_w": 2, "w_glu": 1}
_LANES = 1024


_GROUPS = (("w_ada", "w_in", "w_out"), ("w_mlp_in", "w_mlp_out"))
_QUEUE_OF = {"w_ada": "narrow", "w_in": "narrow", "w_out": "out", "w_mlp_in": "mlp", "w_mlp_out": "mlp"}


class _Exchange:
    def __init__(self, W, q_me):
        self.W, self.q_me = W, q_me
        self.wall, self.own, self.recv, self.pending = {}, {}, {}, {}
        first = ("w_ada", "w_in")
        self._keep(0, first, _exchange_call(_Gather2([self._shard(0, n) for n in first]), "gather_big"))

    def _shard(self, l, n):
        return self.W[n][l].astype(MXU)

    def _keep(self, l, names, walls):
        for n, w in zip(names, walls):
            self.wall[l, n] = w

    def w(self, l, n):
        g = self.wall[l, n]
        if n in ("w_ada", "w_mlp_in"):
            return g
        return _join_shards(g[:, None], 2)[0] if n == "w_in" else g.reshape(-1, g.shape[-1])

    def _gathering(self, l, names, fn, *args):
        if l == NLAYER or not names:
            return fn(*args)[0]
        res, walls = fn(*args, ex=_Gather([self._shard(l, n) for n in names]))
        self._keep(l, names, walls)
        return res

    def pre_mix(self, l, *args):
        return self._gathering(l, ("w_out",) if l == 0 else (), _pre_mix, *args)

    def scan_fwd(self, l, *args):
        return self._gathering(l, ("w_mlp_in",) if l == 0 else (), _scan_fwd, *args)

    def post_mix(self, l, *args):
        return self._gathering(l, ("w_mlp_out",) if l == 0 else (), _post_mix, *args)

    def attn_fwd(self, l, *args):
        return self._gathering(l + 1, _GROUPS[0], _attn_fwd, *args)

    def mlp_fwd(self, l, *args):
        return self._gathering(l + 1, _GROUPS[1], _mlp_fwd, *args)

    def submit(self, l, n, g, gm):
        r, c = g.shape
        if _BIG_AXIS[n] == 2:
            cw = c // NCHIP
            self.own[l, n] = lax.dynamic_slice_in_dim(g, self.q_me * cw, cw, axis=1)
            if cw % 128 == 0:
                item = (gm, cw)
            else:
                item = (jnp.transpose(gm.reshape(r, NCHIP, cw), (1, 0, 2)), None)
        else:
            self.own[l, n] = lax.dynamic_slice_in_dim(g, self.q_me * (r // NCHIP), r // NCHIP, axis=0)
            item = (gm.reshape(NCHIP, r // NCHIP, c), None)
        self.pending.setdefault(_QUEUE_OF[n], []).append((l, n) + item)

    def _scatter_of(self, items):
        return _Scatter([it[2] for it in items], [it[3] for it in items])

    def _scattering(self, queue, fn, *args):
        items = self.pending.pop(queue, [])
        if not items:
            return fn(*args)[0]
        res, recvs = fn(*args, ex=self._scatter_of(items))
        for it, r in zip(items, recvs):
            self.recv[it[0], it[1]] = r
        return res

    def mlp_bwd(self, l, *args):
        return self._scattering("narrow", _mlp_bwd, *args)

    def scan_bwd(self, l, *args):
        return self._scattering("mlp", _scan_bwd, *args)

    def attn_bwd(self, l, *args):
        return self._scattering("out", _attn_bwd, *args)

    def finish(self):
        items = [it for queue in sorted(self.pending) for it in self.pending[queue]]
        self.pending = {}
        for it, r in zip(items, _exchange_call(self._scatter_of(items), "scatter_big")):
            self.recv[it[0], it[1]] = r
        part = [jnp.concatenate([_sum_slots(self.own[l, n], self.recv[l, n], "sum_chips") for l in range(NLAYER)],
                                axis=0) for n in _BIG]
        sib = _exchange_call(_Swap(part), "swap_big")
        return {n: (p, s) for n, p, s in zip(_BIG, part, sib)}


def _part_rows(shape):
    return -(-math.prod(shape) // (16 * _LANES)) * 16


def _flat_pad(parts):
    rows = []
    for p in parts:
        v = p.reshape(-1)
        pad = _part_rows(p.shape) * _LANES - v.shape[0]
        rows.append(jnp.concatenate([v, jnp.zeros((pad,), v.dtype)]).reshape(-1, _LANES))
    used = sum(r.shape[0] for r in rows)
    rows.append(jnp.zeros((-used % _PAD_ROWS, _LANES), rows[0].dtype))
    return jnp.concatenate(rows, axis=0)


def _unflat(flat, shapes):
    out, r0 = [], 0
    for s in shapes:
        nr = _part_rows(s)
        out.append(flat[r0:r0 + nr].reshape(-1)[:math.prod(s)].reshape(s))
        r0 += nr
    return out


def kernel(x, c, ctx, c_ctx, w_ada, b_ada, norm_g, w_in, conv_w, attn_sink, ssm_lam_re, ssm_lam_im, ssm_log_dt, ssm_b_re, ssm_b_im, ssm_c_re, ssm_c_im, ssm_d, w_glu, b_glu, w_out, w_mlp_in, w_mlp_out, loss_target, m_c_ctx, m_w_ada, m_b_ada, m_norm_g, m_w_in, m_conv_w, m_attn_sink, m_ssm_lam_re, m_ssm_lam_im, m_ssm_log_dt, m_ssm_b_re, m_ssm_b_im, m_ssm_c_re, m_ssm_c_im, m_ssm_d, m_w_glu, m_b_glu, m_w_out, m_w_mlp_in, m_w_mlp_out, v_c_ctx, v_w_ada, v_b_ada, v_norm_g, v_w_in, v_conv_w, v_attn_sink, v_ssm_lam_re, v_ssm_lam_im, v_ssm_log_dt, v_ssm_b_re, v_ssm_b_im, v_ssm_c_re, v_ssm_c_im, v_ssm_d, v_w_glu, v_b_glu, v_w_out, v_w_mlp_in, v_w_mlp_out):
    W = dict(c_ctx=c_ctx, w_ada=w_ada, b_ada=b_ada, norm_g=norm_g, w_in=w_in, conv_w=conv_w, attn_sink=attn_sink,
             ssm_lam_re=ssm_lam_re, ssm_lam_im=ssm_lam_im, ssm_log_dt=ssm_log_dt, ssm_b_re=ssm_b_re,
             ssm_b_im=ssm_b_im, ssm_c_re=ssm_c_re, ssm_c_im=ssm_c_im, ssm_d=ssm_d, w_glu=w_glu, b_glu=b_glu,
             w_out=w_out, w_mlp_in=w_mlp_in, w_mlp_out=w_mlp_out)
    M = dict(c_ctx=m_c_ctx, w_ada=m_w_ada, b_ada=m_b_ada, norm_g=m_norm_g, w_in=m_w_in, conv_w=m_conv_w,
             attn_sink=m_attn_sink, ssm_lam_re=m_ssm_lam_re, ssm_lam_im=m_ssm_lam_im, ssm_log_dt=m_ssm_log_dt,
             ssm_b_re=m_ssm_b_re, ssm_b_im=m_ssm_b_im, ssm_c_re=m_ssm_c_re, ssm_c_im=m_ssm_c_im, ssm_d=m_ssm_d,
             w_glu=m_w_glu, b_glu=m_b_glu, w_out=m_w_out, w_mlp_in=m_w_mlp_in, w_mlp_out=m_w_mlp_out)
    V = dict(c_ctx=v_c_ctx, w_ada=v_w_ada, b_ada=v_b_ada, norm_g=v_norm_g, w_in=v_w_in, conv_w=v_conv_w,
             attn_sink=v_attn_sink, ssm_lam_re=v_ssm_lam_re, ssm_lam_im=v_ssm_lam_im, ssm_log_dt=v_ssm_log_dt,
             ssm_b_re=v_ssm_b_re, ssm_b_im=v_ssm_b_im, ssm_c_re=v_ssm_c_re, ssm_c_im=v_ssm_c_im, ssm_d=v_ssm_d,
             w_glu=v_w_glu, b_glu=v_b_glu, w_out=v_w_out, w_mlp_in=v_w_mlp_in, w_mlp_out=v_w_mlp_out)
    q_me = 2 * lax.axis_index("x") + lax.axis_index("y")

    ex = _Exchange(W, q_me)
    P = {n: W[n] for n in _SMALL}
    ssh_names = list(_SMALL_SHARDED)
    ssh = _flat_pad([W[n] for n in ssh_names])
    (sall,) = _exchange_call(_Gather([ssh]), "gather_small")
    parts = [_unflat(sall[q], [W[n].shape for n in ssh_names]) for q in range(NCHIP)]
    for k, n in enumerate(ssh_names):
        P[n] = _join_shards(jnp.stack([parts[q][k] for q in range(NCHIP)]), _SMALL_SHARDED[n])

    loss_local, grad_x, G = _local_step(x, c, ctx, loss_target, P, ex)
    loss = lax.psum(loss_local, ("x", "y", "c"))

    sums = ex.finish()
    out = {}
    for n in _BIG:
        flat = lambda t: t.reshape(-1, t.shape[-1])
        res = _adamw(flat(W[n]), *sums[n], flat(M[n]), flat(V[n]), "adamw_" + n)
        out[n] = [t.reshape(W[n].shape) for t in res]

    gsmall = _flat_pad([G[n] for n in _SMALL]).astype(MXU)
    gsum = _sum_slots(None, _gather_all(gsmall, "gather_grads"), "sum_devices")
    gfull = dict(zip(_SMALL, _unflat(gsum, [G[n].shape for n in _SMALL])))
    for n, ax in _SMALL_SHARDED.items():
        width = W[n].shape[ax]
        gfull[n] = lax.dynamic_slice_in_dim(gfull[n], q_me * width, width, axis=ax)
    shapes = [W[n].shape for n in _SMALL]
    gflat = _flat_pad([gfull[n] for n in _SMALL])
    res = _adamw(_flat_pad([W[n] for n in _SMALL]), gflat, jnp.zeros_like(gflat),
                 _flat_pad([M[n] for n in _SMALL]), _flat_pad([V[n] for n in _SMALL]), "adamw_small")
    for k, t in enumerate(res):
        for n, piece in zip(_SMALL, _unflat(t, shapes)):
            out.setdefault(n, [None] * 4)[k] = piece

    return (loss, grad_x, *[out[n][0] for n in _WEIGHTS], *[out[n][1] for n in _WEIGHTS],
            *[out[n][2] for n in _WEIGHTS], *[out[n][3] for n in _WEIGHTS])
```

```python
import math

import jax
import jax.numpy as jnp
import numpy as np
from jax import lax
from jax.experimental import pallas as pl
from jax.experimental.pallas import tpu as pltpu

F32 = jnp.float32
MXU = jnp.bfloat16

D = 1024
DFF = 4096
NMOD = 6
EPS = 1e-6
HD = 64
NQ = 8
NKV = 2
QPK = 4
AW = 512
KVW = 128
WIN = 128
QB = 128
CW = 256
SW = 256
SGRP = 16
SCH = 16
SST = 64
INW = 1792
GRID_W = 64
ROPE_BASE = 10000.0
NEG = -1e30
SCALE = HD ** -0.5
NLAYER = 4

TB = 256
TF = 1024
TC = 32
NCHIP = 4

LR, B1, B2, AEPS, WD, STEP = 0.001, 0.9, 0.999, 1e-08, 0.01, 10

MESH = pl.DeviceIdType.MESH
SDS = jax.ShapeDtypeStruct
BS = pl.BlockSpec
ANY = pl.BlockSpec(memory_space=pl.ANY)


def _cp(sem, vmem_mb=None):
    kw = dict(dimension_semantics=sem)
    if vmem_mb is not None:
        kw["vmem_limit_bytes"] = vmem_mb * 1024 * 1024
    return pltpu.CompilerParams(**kw)


def _dot(a, b):
    return jnp.dot(a, b, preferred_element_type=F32)


def _dot_nt(a, b):
    return lax.dot_general(a, b, (((1,), (1,)), ((), ())), preferred_element_type=F32)


def _dot_tn(a, b):
    return lax.dot_general(a, b, (((0,), (0,)), ((), ())), preferred_element_type=F32)


def _rms(x):
    r = lax.rsqrt(jnp.mean(x * x, axis=-1, keepdims=True) + EPS)
    return x * r, r


def _rms_bwd(dn, n, r):
    return r * (dn - n * jnp.mean(dn * n, axis=-1, keepdims=True))


_GC = math.sqrt(2.0 / math.pi)


def _gelu(y):
    return 0.5 * y * (1.0 + jnp.tanh(_GC * (y + 0.044715 * y * y * y)))


def _gelu_grad(y):
    th = jnp.tanh(_GC * (y + 0.044715 * y * y * y))
    return 0.5 * (1.0 + th) + 0.5 * y * (1.0 - th * th) * _GC * (1.0 + 3 * 0.044715 * y * y)


def _sigmoid(v):
    return 1.0 / (1.0 + jnp.exp(-v))


def _colsum(x):
    return jnp.sum(x, axis=0, keepdims=True)


def _tok(width, col=0):
    return BS((None, TB, width), lambda b, j: (b, j, col))


def _mod_spec(nct):
    return BS((None, None, 8, D), lambda b, j: (b, jnp.where(j >= nct, 1, 0), 0, 0))


def _full(shape):
    nd = len(shape)
    return BS(shape, lambda *a: (0,) * nd)


def _resident(shape):
    nd = len(shape)
    return BS(shape, lambda *a: (0,) * nd, pipeline_mode=pl.Buffered(1))


def _ada_fwd(cact8, w4, b):
    wq = w4.shape[2]

    def body(c_ref, w_ref, b_ref, o_ref):
        o_ref[...] = _dot(c_ref[...].astype(MXU), w_ref[...]) + b_ref[...]

    return pl.pallas_call(
        body, name="ada_fwd", grid=(NCHIP,),
        in_specs=[BS((8, D), lambda j: (0, 0)), BS((None, D, wq), lambda j: (j, 0, 0)), BS((1, wq), lambda j: (0, j))],
        out_specs=BS((8, wq), lambda j: (0, j)),
        out_shape=SDS((8, NMOD * D), F32), compiler_params=_cp(("arbitrary",)),
    )(cact8, w4, b)


def _ada_bwd_c(dmod8, w4):
    wq = w4.shape[2]

    def body(d_ref, w_ref, o_ref):
        @pl.when(pl.program_id(0) == 0)
        def _():
            o_ref[...] = jnp.zeros_like(o_ref)
        o_ref[...] += _dot_nt(d_ref[...], w_ref[...])

    return pl.pallas_call(
        body, name="ada_bwd_c", grid=(NCHIP,),
        in_specs=[BS((8, wq), lambda j: (0, j)), BS((None, D, wq), lambda j: (j, 0, 0))],
        out_specs=BS((8, D), lambda j: (0, 0)),
        out_shape=SDS((8, D), F32), compiler_params=_cp(("arbitrary",)),
    )(dmod8, w4)


def _pre_mix(h, modtok, ng, w_in, cos, sin, nct, ex=None):
    B, S, _ = h.shape

    def body(h_ref, mod_ref, g_ref, w_ref, c_ref, s_ref, p_ref, a_ref, q_ref, k_ref):
        n, _ = _rms(h_ref[...])
        a = ((n * g_ref[0:1, :]) * (1.0 + mod_ref[1:2, :]) + mod_ref[0:1, :]).astype(MXU)
        a_ref[...] = a
        p = _dot(a, w_ref[...])
        p_ref[...] = p
        c, s = c_ref[...], s_ref[...]
        rot = [_rotate(p[:, k * 128:(k + 1) * 128], c, s) for k in range((AW + KVW) // 128)]
        k_ref[...] = rot[AW // 128].astype(MXU)
        for g, t in enumerate(_to_groups(rot[:AW // 128])):
            q_ref[g] = (t * SCALE).astype(MXU)

    return _call_with(
        ex, body, "pre_mix", (B, S // TB),
        [_tok(D), _mod_spec(nct), _full((8, D)), _full((D, INW)), _TAB, _TAB],
        [_tok(INW), _tok(D), _grp(), _tok(KVW)],
        [SDS((B, S, INW), F32), SDS((B, S, D), MXU), SDS((B, QPK, S, 128), MXU), SDS((B, S, KVW), MXU)],
        [h, modtok, ng, w_in, cos, sin], [], _cp(("arbitrary", "arbitrary"), 40))


def _pre_bwd(dq4, dk, cos, nsin, dv, dcv, du_parts, dh_part, h, modtok, ng, w_in, nct):
    B, S, _ = h.shape

    def body(dq_ref, dk_ref, c_ref, s_ref, dv_ref, dcv_ref, du0_ref, du1_ref, du2_ref, dhp_ref, h_ref, mod_ref,
             g_ref, w_ref, dh_ref, vec_ref, dp_ref):
        c, s = c_ref[...], s_ref[...]
        cols = _from_groups([dq_ref[g] * SCALE for g in range(QPK)]) + [dk_ref[...]]
        dqk = [_rotate(t, c, s).astype(MXU) for t in cols]
        du = du0_ref[...] + du1_ref[...] + du2_ref[...]
        dp = jnp.concatenate(dqk + [dv_ref[...].astype(MXU), dcv_ref[...].astype(MXU), du.astype(MXU)], axis=1)
        dp_ref[...] = dp
        da = _dot_nt(dp, w_ref[...])
        n, r = _rms(h_ref[...])
        g = g_ref[0:1, :]
        sc1 = 1.0 + mod_ref[1:2, :]
        vec_ref[...] = jnp.zeros_like(vec_ref)
        vec_ref[0:1, :] = _colsum(da)
        vec_ref[1:2, :] = _colsum(da * (n * g))
        vec_ref[2:3, :] = _colsum(da * sc1 * n)
        dh_ref[...] = dhp_ref[...] + _rms_bwd(da * sc1 * g, n, r)

    return pl.pallas_call(
        body, name="pre_bwd", grid=(B, S // TB),
        in_specs=[_grp(), _tok(KVW), _TAB, _TAB, _tok(KVW), _tok(3 * CW), _tok(SW), _tok(SW), _tok(SW), _tok(D),
                  _tok(D), _mod_spec(nct), _full((8, D)), _full((D, INW))],
        out_specs=[_tok(D), BS((None, None, 8, D), lambda b, j: (b, j, 0, 0)), _tok(INW)],
        out_shape=[SDS((B, S, D), F32), SDS((B, S // TB, 8, D), F32), SDS((B, S, INW), MXU)],
        compiler_params=_cp(("arbitrary", "arbitrary"), 40),
    )(dq4, dk, cos, nsin, dv, dcv, *du_parts, dh_part, h, modtok, ng, w_in)


def _rotate(t, c, s):
    lane = lax.broadcasted_iota(jnp.int32, t.shape, 1)
    partner = jnp.where((lane % 32) < 16, pltpu.roll(t, 112, 1), pltpu.roll(t, 16, 1))
    return t * c + partner * s


def _to_groups(cols):
    lo = lax.broadcasted_iota(jnp.int32, cols[0].shape, 1) < HD
    out = []
    for g in range(QPK):
        a, b = cols[g // 2], cols[2 + g // 2]
        out.append(jnp.where(lo, a, pltpu.roll(b, HD, 1)) if g % 2 == 0 else jnp.where(lo, pltpu.roll(a, HD, 1), b))
    return out


def _from_groups(tiles):
    lo = lax.broadcasted_iota(jnp.int32, tiles[0].shape, 1) < HD
    out = []
    for k in range(NQ // 2):
        a, b = tiles[(2 * k) % QPK], tiles[(2 * k) % QPK + 1]
        out.append(jnp.where(lo, a, pltpu.roll(b, HD, 1)) if k < 2 else jnp.where(lo, pltpu.roll(a, HD, 1), b))
    return out


def _grp(width=128):
    return BS((None, QPK, TB, width), lambda b, j: (b, 0, j, 0))


_TAB = BS((TB, 128), lambda b, j: (j, 0))


def _window_start(i, S):
    return pl.multiple_of(jnp.clip((i - 1) * QB, 0, S - 3 * QB), QB)


def _attn_bias(S, Lc):
    nb = S // QB
    i = np.arange(nb)[:, None, None]
    w0 = np.clip((i - 1) * QB, 0, S - 3 * QB)
    qpos = i * QB + np.arange(QB)[None, :, None]
    kpos = w0 + np.arange(3 * QB)[None, None, :]
    ok = (i >= Lc // QB) & (kpos >= Lc) & (np.abs(qpos - kpos) <= WIN)
    return jnp.asarray(np.where(ok, 0.0, NEG), F32)


_BIAS = BS((None, QB, 3 * QB), lambda b, i: (i, 0, 0))


def _attn_probs(qh, kch, kwh, bias, sk):
    sc = _dot_nt(qh, kch)
    sw = _dot_nt(qh, kwh) + jnp.concatenate([bias] * (qh.shape[0] // QB), axis=0)
    m = jnp.maximum(jnp.maximum(jnp.max(sc, axis=-1, keepdims=True), jnp.max(sw, axis=-1, keepdims=True)), sk)
    ec = jnp.exp(sc - m)
    ew = jnp.exp(sw - m)
    es = jnp.exp(sk - m)
    inv = 1.0 / (jnp.sum(ec, axis=-1, keepdims=True) + jnp.sum(ew, axis=-1, keepdims=True) + es)
    return ec * inv, ew * inv, es * inv


def _sink_col(sink_ref, hk):
    return jnp.concatenate([jnp.full((QB, 1), sink_ref[hk * QPK + g], F32) for g in range(QPK)], axis=0)


_QGRP = BS((None, QPK, QB, 128), lambda b, i: (b, 0, i, 0))


def _first_last(grid):
    first = last = None
    for ax, n in enumerate(grid):
        i = pl.program_id(ax)
        first = (i == 0) if first is None else jnp.logical_and(first, i == 0)
        last = (i == n - 1) if last is None else jnp.logical_and(last, i == n - 1)
    return first, last


def _call_with(ex, body, name, grid, in_specs, out_specs, out_shape, args, scratch, params):
    n_in, n_out = len(in_specs), len(out_specs)
    if ex is None:
        res = pl.pallas_call(body, name=name, grid=grid, in_specs=in_specs, out_specs=out_specs,
                             out_shape=out_shape, scratch_shapes=scratch, compiler_params=params)(*args)
        return list(res), []

    def carrying(*refs):
        own, comm = _carried(ex, refs, n_in, n_out)
        first, last = _first_last(grid)
        pl.when(first)(lambda: ex.start(*comm))
        body(*own)
        pl.when(last)(lambda: ex.wait(*comm))

    res = pl.pallas_call(
        carrying, name=f"{name}_{type(ex).__name__.strip('_').lower()}", grid=grid,
        in_specs=in_specs + ex.specs, out_specs=out_specs + ex.specs, out_shape=out_shape + ex.out_shape,
        scratch_shapes=scratch + ex.sems, compiler_params=params)(*args, *ex.args)
    return list(res[:n_out]), list(res[n_out:])


def _attn_fwd(sink, q4, k, p, bias, Lc, ex=None):
    B, S, _ = k.shape

    def body(sink_ref, q_ref, k_ref, v_ref, bias_ref, o_ref):
        w0, bias = _window_start(pl.program_id(1), S), bias_ref[...]
        kc = k_ref[0:Lc, :]
        vc = v_ref[0:Lc, :].astype(MXU)
        kw = k_ref[pl.ds(w0, 3 * QB), :]
        vw = v_ref[pl.ds(w0, 3 * QB), :].astype(MXU)
        q = q_ref[...].reshape(QPK * QB, 128)
        half = QPK * QB // 2
        outs = []
        for hk in range(NKV):
            cs = slice(hk * HD, (hk + 1) * HD)
            sk = _sink_col(sink_ref, hk)
            for r0 in (0, half):
                rows = slice(r0, r0 + half)
                pc, pw, _ = _attn_probs(q[rows, cs], kc[:, cs], kw[:, cs], bias, sk[rows])
                o = _dot(pc.astype(MXU), vc[:, cs]) + _dot(pw.astype(MXU), vw[:, cs])
                outs += [o[g * QB:(g + 1) * QB] for g in range(half // QB)]
        o_ref[...] = jnp.concatenate(outs, axis=1).astype(o_ref.dtype)

    return _call_with(
        ex, body, "attn_fwd", (B, S // QB),
        [BS(memory_space=pltpu.SMEM), _QGRP, BS((None, S, KVW), lambda b, i: (b, 0, 0)),
         BS((None, S, KVW), lambda b, i: (b, 0, (AW + KVW) // KVW)), _BIAS],
        [BS((None, QB, AW), lambda b, i: (b, i, 0))], [SDS((B, S, AW), MXU)], [sink, q4, k, p, bias], [],
        _cp(("arbitrary", "arbitrary"), 40))


def _attn_bwd(sink, q4, k, p, do4, bias, Lc, ex=None):
    B, S, _ = k.shape

    def body(sink_ref, q_ref, k_ref, v_ref, do_ref, bias_ref, dq_ref, dk_ref, dv_ref, ds_ref):
        i = pl.program_id(1)

        @pl.when(i == 0)
        def _():
            dk_ref[...] = jnp.zeros_like(dk_ref)
            dv_ref[...] = jnp.zeros_like(dv_ref)
            ds_ref[...] = jnp.zeros_like(ds_ref)

        w0, bias = _window_start(i, S), bias_ref[...]
        kc = k_ref[0:Lc, :]
        vc = v_ref[0:Lc, :].astype(MXU)
        kw = k_ref[pl.ds(w0, 3 * QB), :]
        vw = v_ref[pl.ds(w0, 3 * QB), :].astype(MXU)
        q = q_ref[...].reshape(QPK * QB, 128)
        do = do_ref[...].reshape(QPK * QB, 128)
        dqs, dsinks, dkcs, dkws, dvcs, dvws = [], [], [], [], [], []
        for hk in range(NKV):
            cs = slice(hk * HD, (hk + 1) * HD)
            qh = q[:, cs]
            pc, pw, ps = _attn_probs(qh, kc[:, cs], kw[:, cs], bias, _sink_col(sink_ref, hk))
            dob = do[:, cs].astype(MXU)
            dpc = _dot_nt(dob, vc[:, cs])
            dpw = _dot_nt(dob, vw[:, cs])
            delta = jnp.sum(pc * dpc, axis=-1, keepdims=True) + jnp.sum(pw * dpw, axis=-1, keepdims=True)
            dsc = (pc * (dpc - delta)).astype(MXU)
            dsw = (pw * (dpw - delta)).astype(MXU)
            dqs.append(_dot(dsc, kc[:, cs]) + _dot(dsw, kw[:, cs]))
            dkcs.append(_dot_tn(dsc, qh))
            dkws.append(_dot_tn(dsw, qh))
            dvcs.append(_dot_tn(pc.astype(MXU), dob))
            dvws.append(_dot_tn(pw.astype(MXU), dob))
            psd = ps * delta
            dsinks += [jnp.broadcast_to(-jnp.sum(psd[g * QB:(g + 1) * QB], axis=0, keepdims=True), (1, 128))
                       for g in range(QPK)]
        for g in range(QPK):
            dq_ref[g] = jnp.concatenate([dq[g * QB:(g + 1) * QB] for dq in dqs], axis=1)
        ds_ref[...] += jnp.concatenate(dsinks, axis=0)
        dk_ref[0:Lc, :] += jnp.concatenate(dkcs, axis=1)
        dv_ref[0:Lc, :] += jnp.concatenate(dvcs, axis=1)
        dk_ref[pl.ds(w0, 3 * QB), :] += jnp.concatenate(dkws, axis=1)
        dv_ref[pl.ds(w0, 3 * QB), :] += jnp.concatenate(dvws, axis=1)

    kv_acc = BS((None, S, KVW), lambda b, i: (b, 0, 0))
    return _call_with(
        ex, body, "attn_bwd", (B, S // QB),
        [BS(memory_space=pltpu.SMEM), _QGRP, BS((None, S, KVW), lambda b, i: (b, 0, 0)),
         BS((None, S, KVW), lambda b, i: (b, 0, (AW + KVW) // KVW)), _QGRP, _BIAS],
        [_QGRP, kv_acc, kv_acc, BS((None, 8, 128), lambda b, i: (b, 0, 0))],
        [SDS((B, QPK, S, 128), F32), SDS((B, S, KVW), F32), SDS((B, S, KVW), F32), SDS((B, 8, 128), F32)],
        [sink, q4, k, p, do4, bias], [], _cp(("arbitrary", "arbitrary"), 48))


def _halo_specs(width, col, S):
    nb8 = S // 8
    per = TB // 8
    prev = BS((None, 8, width), lambda b, j: (b, jnp.maximum(j * per - 1, 0), col))
    nxt = BS((None, 8, width), lambda b, j: (b, jnp.minimum((j + 1) * per, nb8 - 1), col))
    return prev, nxt


def _shift_dn(z, prev_row, at_start):
    row = lax.broadcasted_iota(jnp.int32, z.shape, 0)
    first = jnp.where(at_start, 0.0, 1.0) * prev_row
    return jnp.where(row == 0, first, pltpu.roll(z, 1, 0))


def _shift_up(z, next_row, at_end):
    row = lax.broadcasted_iota(jnp.int32, z.shape, 0)
    last = jnp.where(at_end, 0.0, 1.0) * next_row
    return jnp.where(row == z.shape[0] - 1, last, pltpu.roll(z, z.shape[0] - 1, 0))


def _conv_bwd(p, dcv, cw8, nct):
    B, S, _ = p.shape
    nt = S // TB

    def body(p_ref, pp_ref, pn_ref, d_ref, dp_ref, dn_ref, w_ref, o_ref, dw_ref):
        j = pl.program_id(1)
        at_start = jnp.logical_or(j == 0, j == nct)
        at_end = jnp.logical_or(j == nct - 1, j == nt - 1)
        cb, cc, cx = p_ref[:, 0:256], p_ref[:, 256:512], p_ref[:, 512:768]
        z = cc * cx
        zm = _shift_dn(z, pp_ref[7:8, 256:512] * pp_ref[7:8, 512:768], at_start)
        zp = _shift_up(z, pn_ref[0:1, 256:512] * pn_ref[0:1, 512:768], at_end)
        d = d_ref[...]
        e = d * cb
        em = _shift_dn(e, dp_ref[7:8, :] * pp_ref[7:8, 0:256], at_start)
        ep = _shift_up(e, dn_ref[0:1, :] * pn_ref[0:1, 0:256], at_end)
        dz = e * w_ref[1:2, :] + ep * w_ref[0:1, :] + em * w_ref[2:3, :]
        o_ref[:, 0:256] = d * (zm * w_ref[0:1, :] + z * w_ref[1:2, :] + zp * w_ref[2:3, :])
        o_ref[:, 256:512] = dz * cx
        o_ref[:, 512:768] = dz * cc
        dw_ref[...] = jnp.zeros_like(dw_ref)
        dw_ref[0:1, :] = _colsum(e * zm)
        dw_ref[1:2, :] = _colsum(e * z)
        dw_ref[2:3, :] = _colsum(e * zp)

    prev, nxt = _halo_specs(3 * CW, 1, S)
    dprev, dnxt = _halo_specs(CW, 0, S)
    return pl.pallas_call(
        body, name="conv_bwd", grid=(B, nt),
        in_specs=[_tok(3 * CW, 1), prev, nxt, _tok(CW), dprev, dnxt, _full((8, CW))],
        out_specs=[_tok(3 * CW), BS((None, None, 8, CW), lambda b, j: (b, j, 0, 0))],
        out_shape=[SDS((B, S, 3 * CW), F32), SDS((B, nt, 8, CW), F32)],
        compiler_params=_cp(("arbitrary", "arbitrary")),
    )(p, p, p, dcv, dcv, dcv, cw8)


HW = SGRP * SST


def _rev_chunk(k, nk, ncc):
    return jnp.where(k < ncc, ncc - 1 - k, nk - 1 - k + ncc)


def _scan_perm(B):
    n = B * TC
    pm = np.zeros((2 * n, 2 * n), np.float32)
    for t in range(TC):
        for e in range(B):
            pm[t * 2 * B + e, e * TC + t] = 1.0
            pm[t * 2 * B + B + e, n + e * TC + (TC - 1 - t)] = 1.0
    return pm


def _scan_drive(uf_ref, ub_ref, pm_ref, b4_ref, fwd, dbuf, n):
    u_tok = jnp.concatenate([uf_ref[...].reshape(n, SW), ub_ref[...].reshape(n, SW)], axis=0).astype(MXU)
    u_tm = _dot(pm_ref[...], u_tok).astype(MXU)
    d4 = _dot(u_tm, b4_ref[...])
    dbuf[...] = jnp.where(fwd, d4[:, :2 * HW], d4[:, 2 * HW:])
    return u_tm


def _scan_fwd(p, pm, pmt, b4, a, c2, Lc, ex=None):
    B, S, _ = p.shape
    nch, n = 2 * B, B * TC
    nk, ncc = S // TC, Lc // TC
    ucol = (INW - SW) // SW

    def body(uf_ref, ub_ref, pm_ref, pmt_ref, b4_ref, a_ref, c2_ref, yf_ref, yb_ref, hb_ref, hcar, dbuf, hbuf):
        @pl.when(pl.program_id(0) == 0)
        def _():
            hcar[...] = jnp.zeros_like(hcar)

        fwd = (lax.broadcasted_iota(jnp.int32, (2 * n, 1), 0) % nch) < B
        _scan_drive(uf_ref, ub_ref, pm_ref, b4_ref, fwd, dbuf, n)
        hb_ref[...] = hcar[...]
        ar, ai = a_ref[:, :HW], a_ref[:, HW:]

        def step(t, h):
            hr, hi = h
            r0 = pl.multiple_of(t * nch, nch)
            nr = ar * hr - ai * hi + dbuf[pl.ds(r0, nch), :HW]
            ni = ar * hi + ai * hr + dbuf[pl.ds(r0, nch), HW:]
            hbuf[pl.ds(r0, nch), :HW] = nr
            hbuf[pl.ds(r0, nch), HW:] = ni
            return nr, ni

        hr, hi = lax.fori_loop(0, TC, step, (hcar[:, :HW], hcar[:, HW:]))
        hcar[:, :HW] = hr
        hcar[:, HW:] = hi
        h_tok = _dot(pmt_ref[...], hbuf[...].astype(MXU)).astype(MXU)
        yf_ref[...] = _dot(h_tok[:n], c2_ref[:, :SW]).reshape(B, TC, SW)
        yb_ref[...] = _dot(h_tok[n:], c2_ref[:, SW:]).reshape(B, TC, SW)

    return _call_with(
        ex, body, "scan_fwd", (nk,),
        [BS((B, TC, SW), lambda k: (0, k, ucol)),
         BS((B, TC, SW), lambda k: (0, _rev_chunk(k, nk, ncc), ucol)),
         _full((2 * n, 2 * n)), _full((2 * n, 2 * n)),
         _full((SW, 4 * HW)), _full((nch, 2 * HW)), _full((2 * HW, 2 * SW))],
        [BS((B, TC, SW), lambda k: (0, k, 0)),
         BS((B, TC, SW), lambda k: (0, _rev_chunk(k, nk, ncc), 0)),
         BS((None, nch, 2 * HW), lambda k: (k, 0, 0))],
        [SDS((B, S, SW), F32), SDS((B, S, SW), F32), SDS((nk, nch, 2 * HW), F32)],
        [p, p, pm, pmt, b4, a, c2],
        [pltpu.VMEM((nch, 2 * HW), F32), pltpu.VMEM((2 * n, 2 * HW), F32), pltpu.VMEM((2 * n, 2 * HW), F32)],
        _cp(("arbitrary",), 48))


def _scan_bwd(p, dy, hb, pm, pmt, b4, a, c2, Lc, ex=None):
    B, S, _ = p.shape
    nch, n = 2 * B, B * TC
    nk, ncc = S // TC, Lc // TC
    ucol = (INW - SW) // SW
    rows = 2 * n

    def body(uf_ref, ub_ref, dyf_ref, dyb_ref, hb_ref, pm_ref, pmt_ref, b4_ref, a_ref, c2_ref,
             duf_ref, dub_ref, db4_ref, da_ref, dc2_ref, gcar, dbuf, hbuf, gbuf):
        @pl.when(pl.program_id(0) == 0)
        def _():
            gcar[...] = jnp.zeros_like(gcar)
            db4_ref[...] = jnp.zeros_like(db4_ref)
            da_ref[...] = jnp.zeros_like(da_ref)
            dc2_ref[...] = jnp.zeros_like(dc2_ref)

        fwd = (lax.broadcasted_iota(jnp.int32, (rows, 1), 0) % nch) < B
        u_tm = _scan_drive(uf_ref, ub_ref, pm_ref, b4_ref, fwd, dbuf, n)
        ar, ai = a_ref[:, :HW], a_ref[:, HW:]
        hbuf[0:nch, :] = hb_ref[...]

        def fstep(t, h):
            hr, hi = h
            r0 = pl.multiple_of(t * nch, nch)
            nr = ar * hr - ai * hi + dbuf[pl.ds(r0, nch), :HW]
            ni = ar * hi + ai * hr + dbuf[pl.ds(r0, nch), HW:]
            hbuf[pl.ds(r0 + nch, nch), :HW] = nr
            hbuf[pl.ds(r0 + nch, nch), HW:] = ni
            return nr, ni

        lax.fori_loop(0, TC, fstep, (hb_ref[:, :HW], hb_ref[:, HW:]))

        dy_tok = jnp.concatenate([dyf_ref[...].reshape(n, SW), dyb_ref[...].reshape(n, SW)], axis=0).astype(MXU)
        dy_tm = _dot(pm_ref[...], dy_tok)
        dy2 = jnp.concatenate([jnp.where(fwd, dy_tm, 0.0), jnp.where(fwd, 0.0, dy_tm)], axis=1).astype(MXU)
        gbuf[...] = _dot_nt(dy2, c2_ref[...])
        dc2_ref[...] += _dot_tn(hbuf[nch:, :].astype(MXU), dy2)

        def bstep(i, cary):
            cr, ci = cary
            r0 = pl.multiple_of((TC - 1 - i) * nch, nch)
            gr = gbuf[pl.ds(r0, nch), :HW] + cr
            gi = gbuf[pl.ds(r0, nch), HW:] + ci
            gbuf[pl.ds(r0, nch), :HW] = gr
            gbuf[pl.ds(r0, nch), HW:] = gi
            return ar * gr + ai * gi, ar * gi - ai * gr

        cr, ci = lax.fori_loop(0, TC, bstep, (gcar[:, :HW], gcar[:, HW:]))
        gcar[:, :HW] = cr
        gcar[:, HW:] = ci

        gr = gbuf[:, :HW].reshape(TC, nch, HW)
        gi = gbuf[:, HW:].reshape(TC, nch, HW)
        hpr = hbuf[0:rows, :HW].reshape(TC, nch, HW)
        hpi = hbuf[0:rows, HW:].reshape(TC, nch, HW)
        da_ref[:, :HW] += jnp.sum(gr * hpr + gi * hpi, axis=0)
        da_ref[:, HW:] += jnp.sum(gi * hpr - gr * hpi, axis=0)

        g = gbuf[...]
        gm = g.astype(MXU)
        dd4 = jnp.concatenate([jnp.where(fwd, g, 0.0), jnp.where(fwd, 0.0, g)], axis=1).astype(MXU)
        db4_ref[...] += _dot_tn(u_tm, dd4)
        g_tok = _dot(pmt_ref[...], gm).astype(MXU)
        duf_ref[...] = _dot_nt(g_tok[:n], b4_ref[:, :2 * HW]).reshape(B, TC, SW)
        dub_ref[...] = _dot_nt(g_tok[n:], b4_ref[:, 2 * HW:]).reshape(B, TC, SW)

    fwd_blk = lambda col: BS((B, TC, SW), lambda k: (0, nk - 1 - k, col))
    bwd_blk = lambda col: BS((B, TC, SW), lambda k: (0, _rev_chunk(nk - 1 - k, nk, ncc), col))
    return _call_with(
        ex, body, "scan_bwd", (nk,),
        [fwd_blk(ucol), bwd_blk(ucol), fwd_blk(0), bwd_blk(0),
         BS((None, nch, 2 * HW), lambda k: (nk - 1 - k, 0, 0)),
         _full((rows, rows)), _full((rows, rows)),
         _full((SW, 4 * HW)), _full((nch, 2 * HW)), _full((2 * HW, 2 * SW))],
        [fwd_blk(0), bwd_blk(0), _full((SW, 4 * HW)), _full((nch, 2 * HW)), _full((2 * HW, 2 * SW))],
        [SDS((B, S, SW), F32), SDS((B, S, SW), F32), SDS((SW, 4 * HW), F32), SDS((nch, 2 * HW), F32),
         SDS((2 * HW, 2 * SW), F32)],
        [p, p, dy, dy, hb, pm, pmt, b4, a, c2],
        [pltpu.VMEM((nch, 2 * HW), F32), pltpu.VMEM((rows, 2 * HW), F32),
         pltpu.VMEM((rows + nch, 2 * HW), F32), pltpu.VMEM((rows, 2 * HW), F32)],
        _cp(("arbitrary",), 56))


def _post_mix(o, cw8, yf, yb, p, h, modtok, ng, dsk, wg, bg, wo, nct, ex=None):
    B, S, _ = h.shape
    nt = S // TB

    def body(o_ref, p_ref, pp_ref, pn_ref, cw_ref, yf_ref, yb_ref, u_ref, h_ref, mod_ref, g_ref, dsk_ref, wg_ref,
             bg_ref, wo_ref, h1_ref, mix_ref, m_ref, y_ref):
        j = pl.program_id(1)
        at_start = jnp.logical_or(j == 0, j == nct)
        at_end = jnp.logical_or(j == nct - 1, j == nt - 1)
        z = p_ref[:, 256:512] * p_ref[:, 512:768]
        zprev = pp_ref[7:8, 256:512] * pp_ref[7:8, 512:768]
        znext = pn_ref[0:1, 256:512] * pn_ref[0:1, 512:768]
        cv = p_ref[:, 0:256] * (_shift_dn(z, zprev, at_start) * cw_ref[0:1, :] + z * cw_ref[1:2, :]
                                + _shift_up(z, znext, at_end) * cw_ref[2:3, :])
        y = yf_ref[...] + yb_ref[...] + dsk_ref[...] * u_ref[...]
        g = _gelu(y)
        s = g * _sigmoid(_dot(g.astype(MXU), wg_ref[...]) + bg_ref[...])
        mix_ref[:, 0:AW] = o_ref[...]
        mix_ref[:, AW:AW + CW] = cv.astype(MXU)
        mix_ref[:, AW + CW:] = s.astype(MXU)
        m = _dot(mix_ref[...], wo_ref[...])
        n, _ = _rms(m)
        h1_ref[...] = h_ref[...] + mod_ref[2:3, :] * (n * g_ref[1:2, :])
        m_ref[...] = m
        y_ref[...] = y

    prev, nxt = _halo_specs(3 * CW, 1, S)
    return _call_with(
        ex, body, "post_mix", (B, nt),
        [_tok(AW), _tok(3 * CW, 1), prev, nxt, _full((8, CW)), _tok(SW), _tok(SW), _tok(SW, (INW - SW) // SW),
         _tok(D), _mod_spec(nct), _full((8, D)), _full((1, SW)), _full((SW, SW)), _full((1, SW)), _full((D, D))],
        [_tok(D), _tok(D), _tok(D), _tok(SW)],
        [SDS((B, S, D), F32), SDS((B, S, D), MXU), SDS((B, S, D), F32), SDS((B, S, SW), F32)],
        [o, p, p, p, cw8, yf, yb, p, h, modtok, ng, dsk, wg, bg, wo], [], _cp(("arbitrary", "arbitrary"), 40))


def _post_bwd(dh1, m, y, p, modtok, ng, dsk, wg, bg, wo, nct):
    B, S, _ = m.shape

    def body(dh_ref, m_ref, y_ref, u_ref, mod_ref, g_ref, dsk_ref, wg_ref, bg_ref, wo_ref,
             dm_ref, da_ref, dc_ref, dy_ref, du_ref, vec_ref, vec2_ref, dwg_ref):
        @pl.when(jnp.logical_and(pl.program_id(0) == 0, pl.program_id(1) == 0))
        def _():
            dwg_ref[...] = jnp.zeros_like(dwg_ref)

        n, r = _rms(m_ref[...])
        dh1 = dh_ref[...]
        gpm = g_ref[1:2, :]
        dr = dh1 * mod_ref[2:3, :]
        vec_ref[...] = jnp.zeros_like(vec_ref)
        vec_ref[0:1, :] = _colsum(dh1 * (n * gpm))
        vec_ref[1:2, :] = _colsum(dr * n)
        dm = _rms_bwd(dr * gpm, n, r).astype(MXU)
        dm_ref[...] = dm
        dmix = _dot_nt(dm, wo_ref[...])
        for g, t in enumerate(_to_groups([dmix[:, k * 128:(k + 1) * 128] for k in range(AW // 128)])):
            da_ref[g] = t
        dc_ref[...] = dmix[:, AW:AW + CW]
        ds = dmix[:, AW + CW:]
        yv = y_ref[...]
        g = _gelu(yv)
        gb = g.astype(MXU)
        sg = _sigmoid(_dot(gb, wg_ref[...]) + bg_ref[...])
        dv = ds * g * sg * (1.0 - sg)
        dvb = dv.astype(MXU)
        dg = ds * sg + _dot_nt(dvb, wg_ref[...])
        dwg_ref[...] += _dot_tn(gb, dvb)
        dy = dg * _gelu_grad(yv)
        dy_ref[...] = dy
        du_ref[...] = dy * dsk_ref[...]
        vec2_ref[...] = jnp.zeros_like(vec2_ref)
        vec2_ref[0:1, :] = _colsum(dy * u_ref[...])
        vec2_ref[1:2, :] = _colsum(dv)

    nt = S // TB
    return pl.pallas_call(
        body, name="post_bwd", grid=(B, nt),
        in_specs=[_tok(D), _tok(D), _tok(SW), _tok(SW, (INW - SW) // SW), _mod_spec(nct), _full((8, D)),
                  _full((1, SW)), _full((SW, SW)), _full((1, SW)), _full((D, D))],
        out_specs=[_tok(D), _grp(), _tok(CW), _tok(SW), _tok(SW),
                   BS((None, None, 8, D), lambda b, j: (b, j, 0, 0)),
                   BS((None, None, 8, SW), lambda b, j: (b, j, 0, 0)), _full((SW, SW))],
        out_shape=[SDS((B, S, D), MXU), SDS((B, QPK, S, 128), F32), SDS((B, S, CW), F32), SDS((B, S, SW), F32),
                   SDS((B, S, SW), F32), SDS((B, nt, 8, D), F32), SDS((B, nt, 8, SW), F32), SDS((SW, SW), F32)],
        compiler_params=_cp(("arbitrary", "arbitrary"), 40),
    )(dh1, m, y, p, modtok, ng, dsk, wg, bg, wo)


def _mlp_fwd(h1, modtok, ng, w1, w2, nct, ex=None):
    B, S, _ = h1.shape
    assert w1.shape == (DFF // TF, D, TF)

    def body(h_ref, mod_ref, g_ref, w1_ref, w2_ref, h2_ref, a2_ref, z_ref, f_ref):
        n, _ = _rms(h_ref[...])
        a2 = ((n * g_ref[2:3, :]) * (1.0 + mod_ref[4:5, :]) + mod_ref[3:4, :]).astype(MXU)
        a2_ref[...] = a2
        ff = jnp.zeros((TB, D), F32)
        for c in range(DFF // TF):
            cs = slice(c * TF, (c + 1) * TF)
            z = _dot(a2, w1_ref[c])
            z_ref[:, cs] = z.astype(MXU)
            r = jnp.maximum(z, 0.0)
            ff = ff + _dot((r * r).astype(MXU), w2_ref[cs, :])
        f_ref[...] = ff
        n, _ = _rms(ff)
        h2_ref[...] = h_ref[...] + mod_ref[5:6, :] * (n * g_ref[3:4, :])

    return _call_with(
        ex, body, "mlp_fwd", (B, S // TB),
        [_tok(D), _mod_spec(nct), _full((8, D)), _resident((DFF // TF, D, TF)), _resident((DFF, D))],
        [_tok(D), _tok(D), _tok(DFF), _tok(D)],
        [SDS((B, S, D), F32), SDS((B, S, D), MXU), SDS((B, S, DFF), MXU), SDS((B, S, D), F32)],
        [h1, modtok, ng, w1, w2], [], _cp(("arbitrary", "arbitrary"), 48))


def _mlp_bwd(dh2, h1, f, z, modtok, ng, w1, w2, nct, ex=None):
    B, S, _ = h1.shape

    def body(dh_ref, h_ref, f_ref, z_ref, mod_ref, g_ref, w1_ref, w2_ref, dh1_ref, df_ref, dz_ref, vec_ref):
        n, r = _rms(f_ref[...])
        dh2 = dh_ref[...]
        gp = g_ref[3:4, :]
        dr = dh2 * mod_ref[5:6, :]
        vec_ref[...] = jnp.zeros_like(vec_ref)
        vec_ref[3:4, :] = _colsum(dh2 * (n * gp))
        vec_ref[4:5, :] = _colsum(dr * n)
        df = _rms_bwd(dr * gp, n, r).astype(MXU)
        df_ref[...] = df
        da = jnp.zeros((TB, D), F32)
        for c in range(DFF // TF):
            cs = slice(c * TF, (c + 1) * TF)
            dr2 = _dot_nt(df, w2_ref[cs, :])
            dz = (dr2 * 2.0 * jnp.maximum(z_ref[:, cs].astype(F32), 0.0)).astype(MXU)
            dz_ref[:, cs] = dz
            da = da + _dot_nt(dz, w1_ref[c])
        n, r = _rms(h_ref[...])
        g = g_ref[2:3, :]
        sc1 = 1.0 + mod_ref[4:5, :]
        vec_ref[0:1, :] = _colsum(da)
        vec_ref[1:2, :] = _colsum(da * (n * g))
        vec_ref[2:3, :] = _colsum(da * sc1 * n)
        dh1_ref[...] = dh2 + _rms_bwd(da * sc1 * g, n, r)

    return _call_with(
        ex, body, "mlp_bwd", (B, S // TB),
        [_tok(D), _tok(D), _tok(D), _tok(DFF), _mod_spec(nct), _full((8, D)),
         _resident((DFF // TF, D, TF)), _resident((DFF, D))],
        [_tok(D), _tok(D), _tok(DFF), BS((None, None, 8, D), lambda b, j: (b, j, 0, 0))],
        [SDS((B, S, D), F32), SDS((B, S, D), MXU), SDS((B, S, DFF), MXU), SDS((B, S // TB, 8, D), F32)],
        [dh2, h1, f, z, modtok, ng, w1, w2], [], _cp(("arbitrary", "arbitrary"), 56))


def _loss(h, target, Lc):
    B, S, _ = h.shape
    nt, nct = S // TB, Lc // TB

    def body(h_ref, t_ref, dh_ref, l_ref):
        lat = pl.program_id(1) >= nct
        err = jnp.where(lat, h_ref[...] - t_ref[...], 0.0)
        dh_ref[...] = err * (1.0 / D)
        l_ref[...] = jnp.broadcast_to(jnp.sum(err * err, keepdims=True), (8, 128))

    return pl.pallas_call(
        body, name="loss", grid=(B, nt),
        in_specs=[_tok(D), BS((None, TB, D), lambda b, j: (b, jnp.maximum(j - nct, 0), 0))],
        out_specs=[_tok(D), BS((None, None, 8, 128), lambda b, j: (b, j, 0, 0))],
        out_shape=[SDS((B, S, D), F32), SDS((B, nt, 8, 128), F32)],
        compiler_params=_cp(("arbitrary", "arbitrary")),
    )(h, target)


def _mm_tn(a, b, name, relu2=False):
    T, M = a.shape
    N = b.shape[1]
    def blk(n):
        return max(b for b in range(128, 1025, 128) if n % b == 0)

    bm, bn = blk(M), blk(N)
    tk = next(t for t in (1536, 1024, 512, 256, T) if T % t == 0)
    nkk = T // tk

    def body(a_ref, b_ref, o_ref, om_ref):
        k = pl.program_id(2)

        @pl.when(k == 0)
        def _():
            o_ref[...] = jnp.zeros_like(o_ref)

        av = a_ref[...]
        if relu2:
            r = jnp.maximum(av.astype(F32), 0.0)
            av = (r * r).astype(MXU)
        o_ref[...] += _dot_tn(av, b_ref[...])

        @pl.when(k == nkk - 1)
        def _():
            om_ref[...] = o_ref[...].astype(MXU)

    out = BS((bm, bn), lambda i, j, k: (i, j))
    return pl.pallas_call(
        body, name=name, grid=(M // bm, N // bn, nkk),
        in_specs=[BS((tk, bm), lambda i, j, k: (k, i)), BS((tk, bn), lambda i, j, k: (k, j))],
        out_specs=[out, out], out_shape=[SDS((M, N), F32), SDS((M, N), MXU)],
        compiler_params=_cp(("arbitrary", "arbitrary", "arbitrary"), 56),
    )(a, b)


_PAD_ROWS = 64


def _block_rows(R):
    for br in (256, _PAD_ROWS):
        if R % br == 0:
            return br
    raise ValueError(f"row count {R} is not a multiple of {_PAD_ROWS}")


def _adamw(w, ga, gb, m, v, name):
    R, C = w.shape
    br = _block_rows(R) if C <= _LANES else _PAD_ROWS

    def body(w_ref, ga_ref, gb_ref, m_ref, v_ref, g_out, d_out, m_out, v_out):
        g = ga_ref[...] + gb_ref[...]
        m2 = B1 * m_ref[...] + (1.0 - B1) * g
        v2 = B2 * v_ref[...] + (1.0 - B2) * (g * g)
        m_hat = m2 / (1.0 - B1 ** STEP)
        v_hat = v2 / (1.0 - B2 ** STEP)
        g_out[...] = g
        d_out[...] = -LR * (m_hat / (jnp.sqrt(v_hat) + AEPS) + WD * w_ref[...])
        m_out[...] = m2
        v_out[...] = v2

    own = BS((br, C), lambda i: (i, 0))
    return pl.pallas_call(
        body, name=name, grid=(R // br,),
        in_specs=[own] * 5, out_specs=[own] * 4, out_shape=[SDS((R, C), F32)] * 4,
        compiler_params=_cp(("arbitrary",)),
    )(w, ga, gb, m, v)


def _my_xyc():
    return lax.axis_index("x"), lax.axis_index("y"), lax.axis_index("c")


def _chip_peers(x, y):
    return [(1 - x, y), (x, 1 - y), (1 - x, 1 - y)]


class _Gather:
    def __init__(self, shards):
        self.args = list(shards)
        self.n = len(self.args)
        self.specs = [ANY] * self.n
        self.out_shape = [SDS((NCHIP,) + a.shape, a.dtype) for a in self.args]
        self.sems = [pltpu.SemaphoreType.DMA((3 * self.n,)), pltpu.SemaphoreType.DMA((3 * self.n,)),
                     pltpu.SemaphoreType.DMA((self.n,))]

    def _copies(self, ins, outs, ssem, rsem, lsem):
        x, y, c = _my_xyc()
        mine, sends, recvs = [], [], []
        for k, (s_ref, o_ref) in enumerate(zip(ins, outs)):
            mine.append(pltpu.make_async_copy(s_ref, o_ref.at[2 * x + y], lsem.at[k]))
            for j, (px, py) in enumerate(_chip_peers(x, y)):
                sems = dict(send_sem=ssem.at[3 * k + j], recv_sem=rsem.at[3 * k + j], device_id=(px, py, c),
                            device_id_type=MESH)
                sends.append(pltpu.make_async_remote_copy(src_ref=s_ref, dst_ref=o_ref.at[2 * x + y], **sems))
                recvs.append(pltpu.make_async_remote_copy(src_ref=s_ref, dst_ref=o_ref.at[2 * px + py], **sems))
        return mine, sends, recvs

    def start(self, ins, outs, sems):
        mine, sends, _ = self._copies(ins, outs, *sems)
        for cp in mine + sends:
            cp.start()

    def wait(self, ins, outs, sems):
        mine, sends, recvs = self._copies(ins, outs, *sems)
        for cp in recvs:
            cp.wait_recv()
        for cp in sends:
            cp.wait_send()
        for cp in mine:
            cp.wait()


class _Gather2(_Gather):
    def __init__(self, shards):
        super().__init__(shards)
        self.sems = self.sems + [pltpu.SemaphoreType.DMA((3 * self.n,)), pltpu.SemaphoreType.DMA((3 * self.n,))]

    def _copies2(self, ins, outs, ssem, rsem, lsem, fsem, gsem):
        x, y, c = _my_xyc()
        mine, sends, recvs, passes, got = [], [], [], [], []
        for k, (s_ref, o_ref) in enumerate(zip(ins, outs)):
            half = s_ref.shape[0] // 2
            my_rows, sib_rows = pl.ds(c * half, half), pl.ds((1 - c) * half, half)
            mine.append(pltpu.make_async_copy(s_ref, o_ref.at[2 * x + y], lsem.at[k]))
            for j, (px, py) in enumerate(_chip_peers(x, y)):
                ici = dict(send_sem=ssem.at[3 * k + j], recv_sem=rsem.at[3 * k + j], device_id=(px, py, c),
                           device_id_type=MESH)
                d2d = dict(send_sem=fsem.at[3 * k + j], recv_sem=gsem.at[3 * k + j], device_id=(x, y, 1 - c),
                           device_id_type=MESH)
                landed = o_ref.at[2 * px + py, my_rows]
                sends.append(pltpu.make_async_remote_copy(src_ref=s_ref.at[my_rows],
                                                          dst_ref=o_ref.at[2 * x + y, my_rows], **ici))
                recvs.append(pltpu.make_async_remote_copy(src_ref=s_ref.at[my_rows], dst_ref=landed, **ici))
                passes.append(pltpu.make_async_remote_copy(src_ref=landed, dst_ref=landed, **d2d))
                got.append(pltpu.make_async_remote_copy(src_ref=landed, dst_ref=o_ref.at[2 * px + py, sib_rows],
                                                        **d2d))
        return mine, sends, recvs, passes, got

    def start(self, ins, outs, sems):
        mine, sends, _, _, _ = self._copies2(ins, outs, *sems)
        for cp in mine + sends:
            cp.start()

    def wait(self, ins, outs, sems):
        mine, sends, recvs, passes, got = self._copies2(ins, outs, *sems)
        for landed, onward in zip(recvs, passes):
            landed.wait_recv()
            onward.start()
        for cp in got:
            cp.wait_recv()
        for cp in sends + passes:
            cp.wait_send()
        for cp in mine:
            cp.wait()


class _Scatter:
    def __init__(self, sends, cols=None):
        self.args = list(sends)
        self.n = len(self.args)
        self.cols = list(cols) if cols is not None else [None] * self.n
        self.specs = [ANY] * self.n
        self.out_shape = [SDS((3,) + a.shape[1:] if cw is None else (3, a.shape[0], cw), a.dtype)
                          for a, cw in zip(self.args, self.cols)]
        self.sems = [pltpu.SemaphoreType.DMA((3 * self.n,)), pltpu.SemaphoreType.DMA((3 * self.n,))]

    def _copies(self, ins, outs, ssem, rsem):
        x, y, c = _my_xyc()

        def piece(k, q):
            cw = self.cols[k]
            return ins[k].at[q] if cw is None else ins[k].at[:, pl.ds(pl.multiple_of(q * cw, 128), cw)]

        return [pltpu.make_async_remote_copy(
            src_ref=piece(k, 2 * px + py), dst_ref=outs[k].at[j], send_sem=ssem.at[3 * k + j],
            recv_sem=rsem.at[3 * k + j], device_id=(px, py, c), device_id_type=MESH)
            for k in range(self.n) for j, (px, py) in enumerate(_chip_peers(x, y))]

    def start(self, ins, outs, sems):
        for cp in self._copies(ins, outs, *sems):
            cp.start()

    def wait(self, ins, outs, sems):
        for cp in self._copies(ins, outs, *sems):
            cp.wait()


class _Swap(_Scatter):
    def __init__(self, arrays):
        self.args = list(arrays)
        self.n = len(self.args)
        self.specs = [ANY] * self.n
        self.out_shape = [SDS(a.shape, a.dtype) for a in self.args]
        self.sems = [pltpu.SemaphoreType.DMA((self.n,)), pltpu.SemaphoreType.DMA((self.n,))]

    def _copies(self, ins, outs, ssem, rsem):
        x, y, c = _my_xyc()
        return [pltpu.make_async_remote_copy(src_ref=s_ref, dst_ref=o_ref, send_sem=ssem.at[k], recv_sem=rsem.at[k],
                                             device_id=(x, y, 1 - c), device_id_type=MESH)
                for k, (s_ref, o_ref) in enumerate(zip(ins, outs))]


def _exchange_call(ex, name):
    n = ex.n

    def body(*refs):
        ins, outs, sems = refs[:n], refs[n:2 * n], refs[2 * n:]
        ex.start(ins, outs, sems)
        ex.wait(ins, outs, sems)

    return pl.pallas_call(body, name=name, in_specs=ex.specs, out_specs=ex.specs, out_shape=ex.out_shape,
                          scratch_shapes=ex.sems)(*ex.args)


def _carried(ex, refs, n_in, n_out):
    n = ex.n
    ins, cin = refs[:n_in], refs[n_in:n_in + n]
    outs, cout = refs[n_in + n:n_in + n + n_out], refs[n_in + n + n_out:n_in + 2 * n + n_out]
    rest = refs[n_in + 2 * n + n_out:]
    nsem = len(ex.sems)
    return list(ins) + list(outs) + list(rest[:len(rest) - nsem]), (cin, cout, rest[len(rest) - nsem:])


def _gather_all(v, name):
    R, C = v.shape

    def body(s_ref, o_ref, ssem, rsem, lsem):
        x, y, c = _my_xyc()
        me = 4 * x + 2 * y + c
        mine = pltpu.make_async_copy(s_ref, o_ref.at[me], lsem)
        mine.start()

        def peer(j):
            fx, fy, fc = (j >> 2) & 1, (j >> 1) & 1, j & 1
            return (x ^ fx, y ^ fy, c ^ fc)

        cps = []
        for j in range(1, 8):
            cps.append(pltpu.make_async_remote_copy(
                src_ref=s_ref, dst_ref=o_ref.at[me], send_sem=ssem.at[j - 1], recv_sem=rsem.at[j - 1],
                device_id=peer(j), device_id_type=MESH))
            cps[-1].start()
        for j in range(1, 8):
            px, py, pc = peer(j)
            pltpu.make_async_remote_copy(
                src_ref=s_ref, dst_ref=o_ref.at[4 * px + 2 * py + pc], send_sem=ssem.at[j - 1],
                recv_sem=rsem.at[j - 1], device_id=peer(j), device_id_type=MESH).wait_recv()
        for cp in cps:
            cp.wait_send()
        mine.wait()

    return pl.pallas_call(
        body, name=name, in_specs=[ANY], out_specs=ANY, out_shape=SDS((8, R, C), v.dtype),
        scratch_shapes=[pltpu.SemaphoreType.DMA((7,)), pltpu.SemaphoreType.DMA((7,)), pltpu.SemaphoreType.DMA],
    )(v)


def _sum_slots(own, slots, name):
    n, R, C = slots.shape
    br = _block_rows(R)

    def body(*refs):
        o_ref = refs[-1]
        if own is None:
            acc = refs[0][0].astype(F32)
            first = 1
            s_ref = refs[0]
        else:
            acc = refs[0][...]
            first = 0
            s_ref = refs[1]
        for k in range(first, n):
            acc = acc + s_ref[k].astype(F32)
        o_ref[...] = acc

    row = BS((br, C), lambda i: (i, 0))
    slab = BS((n, br, C), lambda i: (0, i, 0))
    ins, args = ([slab], [slots]) if own is None else ([row, slab], [own, slots])
    return pl.pallas_call(
        body, name=name, grid=(R // br,), in_specs=ins, out_specs=row, out_shape=SDS((R, C), F32),
        compiler_params=_cp(("arbitrary",)),
    )(*args)


def _rope_tables(L, Lc):
    n = jnp.arange(L)
    row = (n // GRID_W).astype(F32)
    col = (n % GRID_W).astype(F32)
    freqs = ROPE_BASE ** (-jnp.arange(16, dtype=F32) / 16)
    lane = jnp.arange(128)
    dd = lane % HD
    fr = freqs[dd % 16]
    ang = jnp.where(dd < 32, row[:, None], col[:, None]) * fr[None, :]
    sign = jnp.where((dd % 32) < 16, -1.0, 1.0)
    cos = jnp.concatenate([jnp.ones((Lc, 128), F32), jnp.cos(ang)], axis=0)
    sin = jnp.concatenate([jnp.zeros((Lc, 128), F32), jnp.sin(ang) * sign[None, :]], axis=0)
    return cos, sin


def _ssm_prep(lam_re, lam_im, log_dt, b_re, b_im, c_re, c_im, B):
    dt = jnp.exp(log_dt)[..., None]
    mag = jnp.exp(lam_re * dt)
    ar = mag * jnp.cos(lam_im * dt)
    ai = mag * jnp.sin(lam_im * dt)
    den = lam_re * lam_re + lam_im * lam_im
    kr = ((ar - 1.0) * lam_re + ai * lam_im) / den
    ki = (ai * lam_re - (ar - 1.0) * lam_im) / den
    bbr = kr[..., None] * b_re - ki[..., None] * b_im
    bbi = kr[..., None] * b_im + ki[..., None] * b_re
    eye = jnp.eye(SGRP, dtype=F32)

    def bblk(t):
        return jnp.einsum("gpi,gh->gihp", t, eye).reshape(SGRP * SCH, SGRP * SST)

    def cblk(t):
        return jnp.einsum("gip,gh->gphi", t, eye).reshape(SGRP * SST, SGRP * SCH)

    b4 = jnp.concatenate([bblk(bbr[0]), bblk(bbi[0]), bblk(bbr[1]), bblk(bbi[1])], axis=1)
    c2 = jnp.concatenate([jnp.concatenate([cblk(c_re[0]), -cblk(c_im[0])], axis=0),
                          jnp.concatenate([cblk(c_re[1]), -cblk(c_im[1])], axis=0)], axis=1)
    a2 = jnp.concatenate([ar.reshape(2, -1), ai.reshape(2, -1)], axis=1)
    a = jnp.repeat(a2, B, axis=0)
    return a, b4, c2


def _pad_rows(v, rows=8):
    return jnp.concatenate([v, jnp.zeros((rows - v.shape[0],) + v.shape[1:], v.dtype)], axis=0)


_BIG = ("w_ada", "w_in", "w_out", "w_mlp_in", "w_mlp_out")
_BIG_AXIS = {"w_ada": 2, "w_in": 2, "w_out": 1, "w_mlp_in": 2, "w_mlp_out": 1}


def _join_shards(s, axis):
    _, L, r, c = s.shape
    if axis == 2:
        return jnp.transpose(s, (1, 2, 0, 3)).reshape(L, r, NCHIP * c)
    return jnp.transpose(s, (1, 0, 2, 3)).reshape(L, NCHIP * r, c)


def _local_step(x, c, ctx, loss_target, P, ex):
    B, L, _ = x.shape
    Lc = ctx.shape[1]
    S = Lc + L
    nct = Lc // TB
    nt = S // TB
    cos, sin = _rope_tables(L, Lc)
    nsin = -sin
    bias = _attn_bias(S, Lc)
    perm = _scan_perm(B)
    pm, pmt = jnp.asarray(perm, MXU), jnp.asarray(perm.T, MXU)

    c_act = jax.nn.silu(c)
    cc_act, silu_vjp = jax.vjp(jax.nn.silu, P["c_ctx"])
    cact8 = _pad_rows(jnp.concatenate([c_act, cc_act[None, :]], axis=0))
    h = jnp.concatenate([ctx, x], axis=1)

    ssm_names = ("ssm_lam_re", "ssm_lam_im", "ssm_log_dt", "ssm_b_re", "ssm_b_im", "ssm_c_re", "ssm_c_im")
    (a_all, b4_all, c2_all), prep_vjp = jax.vjp(jax.vmap(lambda *t: _ssm_prep(*t, B)), *[P[n] for n in ssm_names])
    b4m_all, c2m_all = b4_all.astype(MXU), c2_all.astype(MXU)
    ng_all = jnp.pad(P["norm_g"], ((0, 0), (0, 4), (0, 0)))
    cw8_all = jnp.pad(P["conv_w"], ((0, 0), (0, 8 - P["conv_w"].shape[1]), (0, 0)))
    wg_all = P["w_glu"].astype(MXU)

    saved = []
    for l in range(NLAYER):
        mod8 = _ada_fwd(cact8, ex.w(l, "w_ada"), P["b_ada"][l][None, :])
        mod6 = mod8.reshape(8, NMOD, D)
        modtok = jnp.stack([jnp.broadcast_to(mod6[B], (B, NMOD, D)), mod6[:B]], axis=1)
        modtok = jnp.pad(modtok, ((0, 0), (0, 0), (0, 2), (0, 0)))
        ng, cw8, a_s, b4m, c2m, wg = ng_all[l], cw8_all[l], a_all[l], b4m_all[l], c2m_all[l], wg_all[l]
        dsk = P["ssm_d"][l][None, :]
        bg = P["b_glu"][l][None, :]

        p, a, q4, kr = ex.pre_mix(l, h, modtok, ng, ex.w(l, "w_in"), cos, sin, nct)
        (o,) = ex.attn_fwd(l, P["attn_sink"][l], q4, kr, p, bias, Lc)
        yf, yb, hb = ex.scan_fwd(l, p, pm, pmt, b4m, a_s, c2m, Lc)
        h1, mix, m, y = ex.post_mix(l, o, cw8, yf, yb, p, h, modtok, ng, dsk, wg, bg, ex.w(l, "w_out"), nct)
        h2, a2, z, f = ex.mlp_fwd(l, h1, modtok, ng, ex.w(l, "w_mlp_in"), ex.w(l, "w_mlp_out"), nct)
        saved.append(dict(h=h, modtok=modtok, ng=ng, cw8=cw8, a_s=a_s, b4m=b4m, c2m=c2m, wg=wg, dsk=dsk, bg=bg,
                          p=p, a=a, q4=q4, kr=kr, hb=hb, mix=mix, m=m, y=y, h1=h1, a2=a2, z=z, f=f))
        h = h2

    dh, lpart = _loss(h, loss_target, Lc)
    loss_local = 0.5 / D * jnp.sum(lpart[:, :, 0, 0])

    R = {k: [None] * NLAYER for k in ("w_glu", "vec_p2", "scan", "dcw", "dsink", "vec", "b_ada")}
    dcact = jnp.zeros((8, D), F32)
    T = B * S
    for l in reversed(range(NLAYER)):
        sv = saved[l]
        modtok, ng = sv["modtok"], sv["ng"]
        dh1, df, dz, vec_m = ex.mlp_bwd(l, dh, sv["h1"], sv["f"], sv["z"], modtok, ng, ex.w(l, "w_mlp_in"),
                                        ex.w(l, "w_mlp_out"), nct)
        ex.submit(l, "w_mlp_out", *_mm_tn(sv["z"].reshape(T, DFF), df.reshape(T, D), "dw_mlp_out", relu2=True))
        ex.submit(l, "w_mlp_in", *_mm_tn(sv["a2"].reshape(T, D), dz.reshape(T, DFF), "dw_mlp_in"))
        dm, dattn, dcv, dy, du_skip, vec_p, vec_p2, dwg = _post_bwd(
            dh1, sv["m"], sv["y"], sv["p"], modtok, ng, sv["dsk"], sv["wg"], sv["bg"], ex.w(l, "w_out"), nct)
        ex.submit(l, "w_out", *_mm_tn(sv["mix"].reshape(T, D), dm.reshape(T, D), "dw_out"))
        R["w_glu"][l], R["vec_p2"][l] = dwg, vec_p2
        duf, dub, db4, da_s, dc2 = ex.scan_bwd(l, sv["p"], dy, sv["hb"], pm, pmt, sv["b4m"], sv["a_s"], sv["c2m"],
                                               Lc)
        R["scan"][l] = (da_s, db4, dc2)
        dconv, R["dcw"][l] = _conv_bwd(sv["p"], dcv, sv["cw8"], nct)
        dq_r, dk_r, dv, R["dsink"][l] = ex.attn_bwd(l, P["attn_sink"][l], sv["q4"], sv["kr"], sv["p"], dattn, bias,
                                                      Lc)
        dh, vec_i, dp = _pre_bwd(dq_r, dk_r, cos, nsin, dv, dconv, (du_skip, duf, dub), dh1, sv["h"], modtok, ng,
                                 ex.w(l, "w_in"), nct)
        ex.submit(l, "w_in", *_mm_tn(sv["a"].reshape(T, D), dp.reshape(T, INW), "dw_in"))
        vec = jnp.concatenate([vec_i, vec_p, vec_m], axis=2)
        R["vec"][l] = vec
        mod_rows = jnp.concatenate([vec[:, :, 0:2], vec[:, :, 8:9], vec[:, :, 16:18], vec[:, :, 19:20]], axis=2)
        dmod8 = jnp.concatenate([jnp.sum(mod_rows[:, nct:], axis=1).reshape(B, NMOD * D),
                                 jnp.sum(mod_rows[:, :nct], axis=(0, 1)).reshape(1, NMOD * D),
                                 jnp.zeros((7 - B, NMOD * D), F32)], axis=0)
        R["b_ada"][l] = dmod8
        dmod8m = dmod8.astype(MXU)
        ex.submit(l, "w_ada", *_mm_tn(cact8.astype(MXU), dmod8m, "dw_ada"))
        dcact = dcact + _ada_bwd_c(dmod8m, ex.w(l, "w_ada"))

    stk = lambda k: jnp.stack(R[k])
    vec, vec_p2 = jnp.sum(stk("vec"), axis=(1, 2)), jnp.sum(stk("vec_p2"), axis=(1, 2))
    grads = dict(zip(ssm_names, prep_vjp(tuple(jnp.stack([R["scan"][l][k] for l in range(NLAYER)])
                                               for k in range(3)))))
    norm_rows = jnp.concatenate([vec[:, 2:3], vec[:, 9:10], vec[:, 18:19], vec[:, 20:21]], axis=1)
    grads.update(w_glu=stk("w_glu"), ssm_d=vec_p2[:, 0], b_glu=vec_p2[:, 1], norm_g=norm_rows,
                 conv_w=jnp.sum(stk("dcw"), axis=(1, 2))[:, 0:P["conv_w"].shape[1]],
                 attn_sink=jnp.sum(stk("dsink")[..., 0], axis=1), b_ada=jnp.sum(stk("b_ada"), axis=1),
                 c_ctx=silu_vjp(dcact[B])[0])
    return loss_local, dh[:, Lc:, :], grads


_WEIGHTS = ["c_ctx", "w_ada", "b_ada", "norm_g", "w_in", "conv_w", "attn_sink", "ssm_lam_re", "ssm_lam_im",
            "ssm_log_dt", "ssm_b_re", "ssm_b_im", "ssm_c_re", "ssm_c_im", "ssm_d", "w_glu", "b_glu", "w_out",
            "w_mlp_in", "w_mlp_out"]
_SMALL = [n for n in _WEIGHTS if n not in _BIG]
_SMALL_SHARDED = {"norm_g": 2, "conv_w": 2, "w_glu": 1}
_LANES = 1024


_GROUPS = (("w_ada", "w_in", "w_out"), ("w_mlp_in", "w_mlp_out"))
_QUEUE_OF = {"w_ada": "narrow", "w_in": "narrow", "w_out": "out", "w_mlp_in": "mlp", "w_mlp_out": "mlp"}


class _Exchange:
    def __init__(self, W, q_me):
        self.W, self.q_me = W, q_me
        self.wall, self.own, self.recv, self.pending = {}, {}, {}, {}
        first = ("w_ada", "w_in")
        self._keep(0, first, _exchange_call(_Gather2([self._shard(0, n) for n in first]), "gather_big"))

    def _shard(self, l, n):
        return self.W[n][l].astype(MXU)

    def _keep(self, l, names, walls):
        for n, w in zip(names, walls):
            self.wall[l, n] = w

    def w(self, l, n):
        g = self.wall[l, n]
        if n in ("w_ada", "w_mlp_in"):
            return g
        return _join_shards(g[:, None], 2)[0] if n == "w_in" else g.reshape(-1, g.shape[-1])

    def _gathering(self, l, names, fn, *args):
        if l == NLAYER or not names:
            return fn(*args)[0]
        res, walls = fn(*args, ex=_Gather([self._shard(l, n) for n in names]))
        self._keep(l, names, walls)
        return res

    def pre_mix(self, l, *args):
        return self._gathering(l, ("w_out",) if l == 0 else (), _pre_mix, *args)

    def scan_fwd(self, l, *args):
        return self._gathering(l, ("w_mlp_in",) if l == 0 else (), _scan_fwd, *args)

    def post_mix(self, l, *args):
        return self._gathering(l, ("w_mlp_out",) if l == 0 else (), _post_mix, *args)

    def attn_fwd(self, l, *args):
        return self._gathering(l + 1, _GROUPS[0], _attn_fwd, *args)

    def mlp_fwd(self, l, *args):
        return self._gathering(l + 1, _GROUPS[1], _mlp_fwd, *args)

    def submit(self, l, n, g, gm):
        r, c = g.shape
        if _BIG_AXIS[n] == 2:
            cw = c // NCHIP
            self.own[l, n] = lax.dynamic_slice_in_dim(g, self.q_me * cw, cw, axis=1)
            if cw % 128 == 0:
                item = (gm, cw)
            else:
                item = (jnp.transpose(gm.reshape(r, NCHIP, cw), (1, 0, 2)), None)
        else:
            self.own[l, n] = lax.dynamic_slice_in_dim(g, self.q_me * (r // NCHIP), r // NCHIP, axis=0)
            item = (gm.reshape(NCHIP, r // NCHIP, c), None)
        self.pending.setdefault(_QUEUE_OF[n], []).append((l, n) + item)

    def _scatter_of(self, items):
        return _Scatter([it[2] for it in items], [it[3] for it in items])

    def _scattering(self, queue, fn, *args):
        items = self.pending.pop(queue, [])
        if not items:
            return fn(*args)[0]
        res, recvs = fn(*args, ex=self._scatter_of(items))
        for it, r in zip(items, recvs):
            self.recv[it[0], it[1]] = r
        return res

    def mlp_bwd(self, l, *args):
        return self._scattering("narrow", _mlp_bwd, *args)

    def scan_bwd(self, l, *args):
        return self._scattering("mlp", _scan_bwd, *args)

    def attn_bwd(self, l, *args):
        return self._scattering("out", _attn_bwd, *args)

    def finish(self):
        items = [it for queue in sorted(self.pending) for it in self.pending[queue]]
        self.pending = {}
        for it, r in zip(items, _exchange_call(self._scatter_of(items), "scatter_big")):
            self.recv[it[0], it[1]] = r
        part = [jnp.concatenate([_sum_slots(self.own[l, n], self.recv[l, n], "sum_chips") for l in range(NLAYER)],
                                axis=0) for n in _BIG]
        sib = _exchange_call(_Swap(part), "swap_big")
        return {n: (p, s) for n, p, s in zip(_BIG, part, sib)}


def _part_rows(shape):
    return -(-math.prod(shape) // (16 * _LANES)) * 16


def _flat_pad(parts):
    rows = []
    for p in parts:
        v = p.reshape(-1)
        pad = _part_rows(p.shape) * _LANES - v.shape[0]
        rows.append(jnp.concatenate([v, jnp.zeros((pad,), v.dtype)]).reshape(-1, _LANES))
    used = sum(r.shape[0] for r in rows)
    rows.append(jnp.zeros((-used % _PAD_ROWS, _LANES), rows[0].dtype))
    return jnp.concatenate(rows, axis=0)


def _unflat(flat, shapes):
    out, r0 = [], 0
    for s in shapes:
        nr = _part_rows(s)
        out.append(flat[r0:r0 + nr].reshape(-1)[:math.prod(s)].reshape(s))
        r0 += nr
    return out


def kernel(x, c, ctx, c_ctx, w_ada, b_ada, norm_g, w_in, conv_w, attn_sink, ssm_lam_re, ssm_lam_im, ssm_log_dt, ssm_b_re, ssm_b_im, ssm_c_re, ssm_c_im, ssm_d, w_glu, b_glu, w_out, w_mlp_in, w_mlp_out, loss_target, m_c_ctx, m_w_ada, m_b_ada, m_norm_g, m_w_in, m_conv_w, m_attn_sink, m_ssm_lam_re, m_ssm_lam_im, m_ssm_log_dt, m_ssm_b_re, m_ssm_b_im, m_ssm_c_re, m_ssm_c_im, m_ssm_d, m_w_glu, m_b_glu, m_w_out, m_w_mlp_in, m_w_mlp_out, v_c_ctx, v_w_ada, v_b_ada, v_norm_g, v_w_in, v_conv_w, v_attn_sink, v_ssm_lam_re, v_ssm_lam_im, v_ssm_log_dt, v_ssm_b_re, v_ssm_b_im, v_ssm_c_re, v_ssm_c_im, v_ssm_d, v_w_glu, v_b_glu, v_w_out, v_w_mlp_in, v_w_mlp_out):
    W = dict(c_ctx=c_ctx, w_ada=w_ada, b_ada=b_ada, norm_g=norm_g, w_in=w_in, conv_w=conv_w, attn_sink=attn_sink,
             ssm_lam_re=ssm_lam_re, ssm_lam_im=ssm_lam_im, ssm_log_dt=ssm_log_dt, ssm_b_re=ssm_b_re,
             ssm_b_im=ssm_b_im, ssm_c_re=ssm_c_re, ssm_c_im=ssm_c_im, ssm_d=ssm_d, w_glu=w_glu, b_glu=b_glu,
             w_out=w_out, w_mlp_in=w_mlp_in, w_mlp_out=w_mlp_out)
    M = dict(c_ctx=m_c_ctx, w_ada=m_w_ada, b_ada=m_b_ada, norm_g=m_norm_g, w_in=m_w_in, conv_w=m_conv_w,
             attn_sink=m_attn_sink, ssm_lam_re=m_ssm_lam_re, ssm_lam_im=m_ssm_lam_im, ssm_log_dt=m_ssm_log_dt,
             ssm_b_re=m_ssm_b_re, ssm_b_im=m_ssm_b_im, ssm_c_re=m_ssm_c_re, ssm_c_im=m_ssm_c_im, ssm_d=m_ssm_d,
             w_glu=m_w_glu, b_glu=m_b_glu, w_out=m_w_out, w_mlp_in=m_w_mlp_in, w_mlp_out=m_w_mlp_out)
    V = dict(c_ctx=v_c_ctx, w_ada=v_w_ada, b_ada=v_b_ada, norm_g=v_norm_g, w_in=v_w_in, conv_w=v_conv_w,
             attn_sink=v_attn_sink, ssm_lam_re=v_ssm_lam_re, ssm_lam_im=v_ssm_lam_im, ssm_log_dt=v_ssm_log_dt,
             ssm_b_re=v_ssm_b_re, ssm_b_im=v_ssm_b_im, ssm_c_re=v_ssm_c_re, ssm_c_im=v_ssm_c_im, ssm_d=v_ssm_d,
             w_glu=v_w_glu, b_glu=v_b_glu, w_out=v_w_out, w_mlp_in=v_w_mlp_in, w_mlp_out=v_w_mlp_out)
    q_me = 2 * lax.axis_index("x") + lax.axis_index("y")

    ex = _Exchange(W, q_me)
    P = {n: W[n] for n in _SMALL}
    ssh_names = list(_SMALL_SHARDED)
    ssh = _flat_pad([W[n] for n in ssh_names])
    (sall,) = _exchange_call(_Gather([ssh]), "gather_small")
    parts = [_unflat(sall[q], [W[n].shape for n in ssh_names]) for q in range(NCHIP)]
    for k, n in enumerate(ssh_names):
        P[n] = _join_shards(jnp.stack([parts[q][k] for q in range(NCHIP)]), _SMALL_SHARDED[n])

    loss_local, grad_x, G = _local_step(x, c, ctx, loss_target, P, ex)
    loss = lax.psum(loss_local, ("x", "y", "c"))

    sums = ex.finish()
    out = {}
    for n in _BIG:
        flat = lambda t: t.reshape(-1, t.shape[-1])
        res = _adamw(flat(W[n]), *sums[n], flat(M[n]), flat(V[n]), "adamw_" + n)
        out[n] = [t.reshape(W[n].shape) for t in res]

    gsmall = _flat_pad([G[n] for n in _SMALL]).astype(MXU)
    gsum = _sum_slots(None, _gather_all(gsmall, "gather_grads"), "sum_devices")
    gfull = dict(zip(_SMALL, _unflat(gsum, [G[n].shape for n in _SMALL])))
    for n, ax in _SMALL_SHARDED.items():
        width = W[n].shape[ax]
        gfull[n] = lax.dynamic_slice_in_dim(gfull[n], q_me * width, width, axis=ax)
    shapes = [W[n].shape for n in _SMALL]
    gflat = _flat_pad([gfull[n] for n in _SMALL])
    res = _adamw(_flat_pad([W[n] for n in _SMALL]), gflat, jnp.zeros_like(gflat),
                 _flat_pad([M[n] for n in _SMALL]), _flat_pad([V[n] for n in _SMALL]), "adamw_small")
    for k, t in enumerate(res):
        for n, piece in zip(_SMALL, _unflat(t, shapes)):
            out.setdefault(n, [None] * 4)[k] = piece

    return (loss, grad_x, *[out[n][0] for n in _WEIGHTS], *[out[n][1] for n in _WEIGHTS],
            *[out[n][2] for n in _WEIGHTS], *[out[n][3] for n in _WEIGHTS])
```
